```python
import math
import jax
import jax.numpy as jnp
from jax import lax
import numpy as np


D_MODEL = 1024
BATCH = 8
SEQ = 16384
DEPTH = 4

D_MIX = D_MODEL
HEAD_DIM = 64
N_Q_HEADS = (D_MIX // 2) // HEAD_DIM
N_KV_HEADS = 2
Q_PER_KV = N_Q_HEADS // N_KV_HEADS
D_ATTN = N_Q_HEADS * HEAD_DIM
D_KV = N_KV_HEADS * HEAD_DIM
WINDOW = 128
ATTN_BLOCK = 128
ROPE_THETA = 10000.0
D_S5 = D_MIX // 4
S5_GROUP = 16
S5_GROUPS = D_S5 // S5_GROUP
S5_STATE = 64
D_LRU = D_MIX - D_ATTN - D_S5
LRU_HEADS = 4
LRU_HEAD_DIM = D_LRU // LRU_HEADS
LRU_CONV = 4
LRU_C = 8.0
D_IN = D_ATTN + 2 * D_KV + D_S5 + 2 * D_LRU
SPLITS = (D_ATTN, D_ATTN + D_KV, D_ATTN + 2 * D_KV, D_ATTN + 2 * D_KV + D_S5, D_ATTN + 2 * D_KV + D_S5 + D_LRU)
D_FF = ((8 * D_MODEL // 3 + 127) // 128) * 128
FFN_CONV = 3
ALPHA = (2 * DEPTH) ** 0.25
BETA = (8 * DEPTH) ** -0.25
LN_EPS = 1e-5
RMS_EPS = 1e-6

kernel_name = 'hymba_swa_s5_rglru_deepnorm_trunk'


def layer_norm(x, g, b):
    xf = x.astype(jnp.float32)
    mu = jnp.mean(xf, axis=-1, keepdims=True)
    xc = xf - mu
    var = jnp.mean(jnp.square(xc), axis=-1, keepdims=True)
    y = xc * lax.rsqrt(var + LN_EPS) * g.astype(jnp.float32) + b.astype(jnp.float32)
    return y.astype(x.dtype)


def group_rmsnorm(parts, g):
    normed = [p.astype(jnp.float32) * lax.rsqrt(jnp.mean(jnp.square(p.astype(jnp.float32)), axis=-1, keepdims=True) + RMS_EPS) for p in parts]
    return (jnp.concatenate(normed, axis=-1) * g.astype(jnp.float32)).astype(parts[0].dtype)


def causal_dwconv(x, w):
    K = w.shape[0]
    L = x.shape[1]
    xp = jnp.pad(x, ((0, 0), (K - 1, 0), (0, 0)))
    y = xp[:, 0:L] * w[0]
    for k in range(1, K):
        y = y + xp[:, k:k + L] * w[k]
    return y


def rope_tables(L, dtype):
    inv_freq = ROPE_THETA ** (-jnp.arange(0, HEAD_DIM, 2, dtype=jnp.float32) / HEAD_DIM)
    ang = jnp.arange(L, dtype=jnp.float32)[:, None] * inv_freq[None, :]
    return jnp.cos(ang).astype(dtype)[None, :, None, :], jnp.sin(ang).astype(dtype)[None, :, None, :]


def apply_rope(t, cos, sin):
    t1, t2 = jnp.split(t, 2, axis=-1)
    return jnp.concatenate([t1 * cos - t2 * sin, t2 * cos + t1 * sin], axis=-1)


def sliding_window_attention(q, k, v, sinks):
    Bsz, L, _, _ = q.shape
    nb = L // ATTN_BLOCK
    qb = q.reshape(Bsz, nb, ATTN_BLOCK, N_KV_HEADS, Q_PER_KV, HEAD_DIM)

    def band(t):
        tp = jnp.pad(t, ((0, 0), (ATTN_BLOCK, 0), (0, 0), (0, 0))).reshape(Bsz, nb + 1, ATTN_BLOCK, N_KV_HEADS, HEAD_DIM)
        return jnp.concatenate([tp[:, :-1], tp[:, 1:]], axis=2)

    kb, vb = band(k), band(v)
    scores = jnp.einsum('bnqkgd,bnskd->bnkgqs', qb, kb).astype(jnp.float32) * (HEAD_DIM ** -0.5)
    qi = jnp.arange(ATTN_BLOCK)[:, None]
    si = jnp.arange(2 * ATTN_BLOCK)[None, :]
    diff = qi + ATTN_BLOCK - si
    blk = jnp.arange(nb)[:, None, None]
    valid = (diff >= 0) & (diff < WINDOW) & (blk * ATTN_BLOCK + si[None] - ATTN_BLOCK >= 0)
    scores = jnp.where(valid[None, :, None, None], scores, -jnp.inf)
    sink = sinks.astype(jnp.float32).reshape(N_KV_HEADS, Q_PER_KV)[None, None, :, :, None, None]
    m = jnp.maximum(jnp.max(scores, axis=-1, keepdims=True), sink)
    p = jnp.exp(scores - m)
    denom = jnp.sum(p, axis=-1, keepdims=True) + jnp.exp(sink - m)
    p = (p / denom).astype(v.dtype)
    out = jnp.einsum('bnkgqs,bnskd->bnqkgd', p, vb)
    return out.reshape(Bsz, L, D_ATTN)


def _complex_affine_combine(e1, e2):
    ar1, ai1, br1, bi1 = e1
    ar2, ai2, br2, bi2 = e2
    ar = ar2 * ar1 - ai2 * ai1
    ai = ar2 * ai1 + ai2 * ar1
    br = ar2 * br1 - ai2 * bi1 + br2
    bi = ar2 * bi1 + ai2 * br1 + bi2
    return (ar, ai, br, bi)


def _real_affine_combine(e1, e2):
    a1, b1 = e1
    a2, b2 = e2
    return (a1 * a2, a2 * b1 + b2)


def s5_mixer(u, a_re, a_im, b_re, b_im, c_re, c_im, d, log_dt, glu_w, glu_b):
    f32 = jnp.float32
    Bsz, L, _ = u.shape
    uf = u.astype(f32).reshape(Bsz, L, S5_GROUPS, S5_GROUP)
    lam_re = jnp.minimum(a_re.astype(f32), -1e-4)
    lam_im = a_im.astype(f32)
    dt = jnp.exp(log_dt.astype(f32))[:, None]
    decay = jnp.exp(dt * lam_re)
    ang = dt * lam_im
    abar_re = decay * jnp.cos(ang)
    abar_im = decay * jnp.sin(ang)
    den = jnp.square(lam_re) + jnp.square(lam_im)
    nr = abar_re - 1.0
    ni = abar_im
    coef_re = (nr * lam_re + ni * lam_im) / den
    coef_im = (ni * lam_re - nr * lam_im) / den
    br = b_re.astype(f32)
    bi = b_im.astype(f32)
    bbar_re = coef_re[..., None] * br - coef_im[..., None] * bi
    bbar_im = coef_re[..., None] * bi + coef_im[..., None] * br
    bu_re = jnp.einsum('blgc,gpc->blgp', uf, bbar_re)
    bu_im = jnp.einsum('blgc,gpc->blgp', uf, bbar_im)
    shape = bu_re.shape
    elems = (jnp.broadcast_to(abar_re, shape), jnp.broadcast_to(abar_im, shape), bu_re, bu_im)
    _, _, h_re, h_im = lax.associative_scan(_complex_affine_combine, elems, axis=1)
    y = (jnp.einsum('blgp,gcp->blgc', h_re, c_re.astype(f32))
         - jnp.einsum('blgp,gcp->blgc', h_im, c_im.astype(f32))
         + d.astype(f32).reshape(S5_GROUPS, S5_GROUP) * uf)
    y = jax.nn.gelu(y.reshape(Bsz, L, D_S5))
    y = y * jax.nn.sigmoid(y @ glu_w.astype(f32) + glu_b.astype(f32))
    return y.astype(u.dtype)


def rg_lru_mixer(xr, gate, conv_w, conv_b, wx, bx, wa, ba, a_param):
    f32 = jnp.float32
    Bsz, L, _ = xr.shape
    xc = causal_dwconv(xr, conv_w) + conv_b
    xh = xc.reshape(Bsz, L, LRU_HEADS, LRU_HEAD_DIM)
    gx = jax.nn.sigmoid(jnp.einsum('blhi,hij->blhj', xh, wx).reshape(Bsz, L, D_LRU) + bx)
    ga = jax.nn.sigmoid(jnp.einsum('blhi,hij->blhj', xh, wa).reshape(Bsz, L, D_LRU) + ba)
    log_a = -LRU_C * ga.astype(f32) * jax.nn.softplus(-a_param.astype(f32))
    a = jnp.exp(log_a)
    mult = jnp.sqrt(-jnp.expm1(2.0 * log_a))
    mult = jnp.where((jnp.arange(L) == 0)[None, :, None], 1.0, mult)
    b = mult * gx.astype(f32) * xc.astype(f32)
    _, h = lax.associative_scan(_real_affine_combine, (a, b), axis=1)
    return h.astype(xr.dtype) * jax.nn.gelu(gate)


def conv_gated_mlp(x, w_gate, w_up, conv_w, conv_b, w_down):
    g = causal_dwconv(x @ w_gate, conv_w) + conv_b
    return (jax.nn.silu(g) * (x @ w_up)) @ w_down


def _fwd_setup_inputs(seed: int = 0) -> dict:
    key = jax.random.key(seed)
    ks = jax.random.split(key, 40)
    f32 = jnp.float32

    def nrm(k, shape, scale):
        return scale * jax.random.normal(k, shape, f32)

    n_idx = jnp.arange(S5_STATE, dtype=f32)
    a0 = jax.random.uniform(ks[20], (DEPTH, D_LRU), f32, 0.9, 0.999)
    return {
        'x': nrm(ks[0], (BATCH, SEQ, D_MODEL), 1.0),
        'w_in': nrm(ks[1], (DEPTH, D_MODEL, D_IN), D_MODEL ** -0.5),
        'b_in': nrm(ks[2], (DEPTH, D_IN), 0.01),
        'attn_sinks': nrm(ks[3], (DEPTH, N_Q_HEADS), 0.5),
        's5_a_re': -0.5 + nrm(ks[4], (DEPTH, S5_GROUPS, S5_STATE), 0.01),
        's5_a_im': jnp.pi * n_idx + nrm(ks[5], (DEPTH, S5_GROUPS, S5_STATE), 0.01),
        's5_b_re': nrm(ks[6], (DEPTH, S5_GROUPS, S5_STATE, S5_GROUP), (2 * S5_GROUP) ** -0.5),
        's5_b_im': nrm(ks[7], (DEPTH, S5_GROUPS, S5_STATE, S5_GROUP), (2 * S5_GROUP) ** -0.5),
        's5_c_re': nrm(ks[8], (DEPTH, S5_GROUPS, S5_GROUP, S5_STATE), (2 * S5_STATE) ** -0.5),
        's5_c_im': nrm(ks[9], (DEPTH, S5_GROUPS, S5_GROUP, S5_STATE), (2 * S5_STATE) ** -0.5),
        's5_d': nrm(ks[10], (DEPTH, D_S5), 1.0),
        's5_log_dt': jax.random.uniform(ks[11], (DEPTH, S5_GROUPS), f32, math.log(1e-3), math.log(1e-1)),
        's5_glu_w': nrm(ks[12], (DEPTH, D_S5, D_S5), D_S5 ** -0.5),
        's5_glu_b': nrm(ks[13], (DEPTH, D_S5), 0.01),
        'lru_conv_w': nrm(ks[14], (DEPTH, LRU_CONV, D_LRU), LRU_CONV ** -0.5),
        'lru_conv_b': nrm(ks[15], (DEPTH, D_LRU), 0.01),
        'lru_wx': nrm(ks[16], (DEPTH, LRU_HEADS, LRU_HEAD_DIM, LRU_HEAD_DIM), LRU_HEAD_DIM ** -0.5),
        'lru_bx': nrm(ks[17], (DEPTH, D_LRU), 0.01),
        'lru_wa': nrm(ks[18], (DEPTH, LRU_HEADS, LRU_HEAD_DIM, LRU_HEAD_DIM), LRU_HEAD_DIM ** -0.5),
        'lru_ba': nrm(ks[19], (DEPTH, D_LRU), 0.01),
        'lru_a_param': jnp.log(a0) - jnp.log1p(-a0),
        'mix_norm_g': 1.0 + nrm(ks[21], (DEPTH, D_MIX), 0.01),
        'w_out': nrm(ks[22], (DEPTH, D_MIX, D_MODEL), BETA * D_MIX ** -0.5),
        'b_out': nrm(ks[23], (DEPTH, D_MODEL), 0.01),
        'ln1_g': 1.0 + nrm(ks[24], (DEPTH, D_MODEL), 0.01),
        'ln1_b': nrm(ks[25], (DEPTH, D_MODEL), 0.01),
        'ffn_w_gate': nrm(ks[26], (DEPTH, D_MODEL, D_FF), D_MODEL ** -0.5),
        'ffn_w_up': nrm(ks[27], (DEPTH, D_MODEL, D_FF), D_MODEL ** -0.5),
        'ffn_conv_w': nrm(ks[28], (DEPTH, FFN_CONV, D_FF), FFN_CONV ** -0.5),
        'ffn_conv_b': nrm(ks[29], (DEPTH, D_FF), 0.01),
        'ffn_w_down': nrm(ks[30], (DEPTH, D_FF, D_MODEL), BETA * D_FF ** -0.5),
        'ln2_g': 1.0 + nrm(ks[31], (DEPTH, D_MODEL), 0.01),
        'ln2_b': nrm(ks[32], (DEPTH, D_MODEL), 0.01),
    }


def _fwd_reference(x, w_in, b_in, attn_sinks, s5_a_re, s5_a_im, s5_b_re, s5_b_im, s5_c_re, s5_c_im,
              s5_d, s5_log_dt, s5_glu_w, s5_glu_b, lru_conv_w, lru_conv_b, lru_wx, lru_bx, lru_wa,
              lru_ba, lru_a_param, mix_norm_g, w_out, b_out, ln1_g, ln1_b, ffn_w_gate, ffn_w_up,
              ffn_conv_w, ffn_conv_b, ffn_w_down, ln2_g, ln2_b):
    Bsz, L, _ = x.shape
    cos, sin = rope_tables(L, x.dtype)
    for l in range(DEPTH):
        proj = x @ w_in[l] + b_in[l]
        q, k, v, u, xr, gate = jnp.split(proj, SPLITS, axis=-1)
        q = apply_rope(q.reshape(Bsz, L, N_Q_HEADS, HEAD_DIM), cos, sin)
        k = apply_rope(k.reshape(Bsz, L, N_KV_HEADS, HEAD_DIM), cos, sin)
        v = v.reshape(Bsz, L, N_KV_HEADS, HEAD_DIM)
        y_attn = sliding_window_attention(q, k, v, attn_sinks[l])
        y_s5 = s5_mixer(u, s5_a_re[l], s5_a_im[l], s5_b_re[l], s5_b_im[l], s5_c_re[l], s5_c_im[l],
                        s5_d[l], s5_log_dt[l], s5_glu_w[l], s5_glu_b[l])
        y_lru = rg_lru_mixer(xr, gate, lru_conv_w[l], lru_conv_b[l], lru_wx[l], lru_bx[l],
                             lru_wa[l], lru_ba[l], lru_a_param[l])
        mix = group_rmsnorm((y_attn, y_s5, y_lru), mix_norm_g[l])
        x = layer_norm(ALPHA * x + mix @ w_out[l] + b_out[l], ln1_g[l], ln1_b[l])
        f = conv_gated_mlp(x, ffn_w_gate[l], ffn_w_up[l], ffn_conv_w[l], ffn_conv_b[l], ffn_w_down[l])
        x = layer_norm(ALPHA * x + f, ln2_g[l], ln2_b[l])
    return x


import jax as _jax
import jax.numpy as _jnp

TWIN_FORMAT = 'train_step'
FWD_PARAMS = ['x', 'w_in', 'b_in', 'attn_sinks', 's5_a_re', 's5_a_im', 's5_b_re', 's5_b_im', 's5_c_re', 's5_c_im', 's5_d', 's5_log_dt', 's5_glu_w', 's5_glu_b', 'lru_conv_w', 'lru_conv_b', 'lru_wx', 'lru_bx', 'lru_wa', 'lru_ba', 'lru_a_param', 'mix_norm_g', 'w_out', 'b_out', 'ln1_g', 'ln1_b', 'ffn_w_gate', 'ffn_w_up', 'ffn_conv_w', 'ffn_conv_b', 'ffn_w_down', 'ln2_g', 'ln2_b']
TWIN_WEIGHTS = ['w_in', 'b_in', 'attn_sinks', 's5_a_re', 's5_a_im', 's5_b_re', 's5_b_im', 's5_c_re', 's5_c_im', 's5_d', 's5_log_dt', 's5_glu_w', 's5_glu_b', 'lru_conv_w', 'lru_conv_b', 'lru_wx', 'lru_bx', 'lru_wa', 'lru_ba', 'lru_a_param', 'mix_norm_g', 'w_out', 'b_out', 'ln1_g', 'ln1_b', 'ffn_w_gate', 'ffn_w_up', 'ffn_conv_w', 'ffn_conv_b', 'ffn_w_down', 'ln2_g', 'ln2_b']
TWIN_DIFF_INPUT = 'x'
TWIN_INPUTS = ['x', 'w_in', 'b_in', 'attn_sinks', 's5_a_re', 's5_a_im', 's5_b_re', 's5_b_im', 's5_c_re', 's5_c_im', 's5_d', 's5_log_dt', 's5_glu_w', 's5_glu_b', 'lru_conv_w', 'lru_conv_b', 'lru_wx', 'lru_bx', 'lru_wa', 'lru_ba', 'lru_a_param', 'mix_norm_g', 'w_out', 'b_out', 'ln1_g', 'ln1_b', 'ffn_w_gate', 'ffn_w_up', 'ffn_conv_w', 'ffn_conv_b', 'ffn_w_down', 'ln2_g', 'ln2_b', 'loss_target', 'm_w_in', 'm_b_in', 'm_attn_sinks', 'm_s5_a_re', 'm_s5_a_im', 'm_s5_b_re', 'm_s5_b_im', 'm_s5_c_re', 'm_s5_c_im', 'm_s5_d', 'm_s5_log_dt', 'm_s5_glu_w', 'm_s5_glu_b', 'm_lru_conv_w', 'm_lru_conv_b', 'm_lru_wx', 'm_lru_bx', 'm_lru_wa', 'm_lru_ba', 'm_lru_a_param', 'm_mix_norm_g', 'm_w_out', 'm_b_out', 'm_ln1_g', 'm_ln1_b', 'm_ffn_w_gate', 'm_ffn_w_up', 'm_ffn_conv_w', 'm_ffn_conv_b', 'm_ffn_w_down', 'm_ln2_g', 'm_ln2_b', 'v_w_in', 'v_b_in', 'v_attn_sinks', 'v_s5_a_re', 'v_s5_a_im', 'v_s5_b_re', 'v_s5_b_im', 'v_s5_c_re', 'v_s5_c_im', 'v_s5_d', 'v_s5_log_dt', 'v_s5_glu_w', 'v_s5_glu_b', 'v_lru_conv_w', 'v_lru_conv_b', 'v_lru_wx', 'v_lru_bx', 'v_lru_wa', 'v_lru_ba', 'v_lru_a_param', 'v_mix_norm_g', 'v_w_out', 'v_b_out', 'v_ln1_g', 'v_ln1_b', 'v_ffn_w_gate', 'v_ffn_w_up', 'v_ffn_conv_w', 'v_ffn_conv_b', 'v_ffn_w_down', 'v_ln2_g', 'v_ln2_b']
TWIN_OUTPUTS = ['loss', 'grad_x', 'grad_w_in', 'grad_b_in', 'grad_attn_sinks', 'grad_s5_a_re', 'grad_s5_a_im', 'grad_s5_b_re', 'grad_s5_b_im', 'grad_s5_c_re', 'grad_s5_c_im', 'grad_s5_d', 'grad_s5_log_dt', 'grad_s5_glu_w', 'grad_s5_glu_b', 'grad_lru_conv_w', 'grad_lru_conv_b', 'grad_lru_wx', 'grad_lru_bx', 'grad_lru_wa', 'grad_lru_ba', 'grad_lru_a_param', 'grad_mix_norm_g', 'grad_w_out', 'grad_b_out', 'grad_ln1_g', 'grad_ln1_b', 'grad_ffn_w_gate', 'grad_ffn_w_up', 'grad_ffn_conv_w', 'grad_ffn_conv_b', 'grad_ffn_w_down', 'grad_ln2_g', 'grad_ln2_b', 'delta_w_in', 'delta_b_in', 'delta_attn_sinks', 'delta_s5_a_re', 'delta_s5_a_im', 'delta_s5_b_re', 'delta_s5_b_im', 'delta_s5_c_re', 'delta_s5_c_im', 'delta_s5_d', 'delta_s5_log_dt', 'delta_s5_glu_w', 'delta_s5_glu_b', 'delta_lru_conv_w', 'delta_lru_conv_b', 'delta_lru_wx', 'delta_lru_bx', 'delta_lru_wa', 'delta_lru_ba', 'delta_lru_a_param', 'delta_mix_norm_g', 'delta_w_out', 'delta_b_out', 'delta_ln1_g', 'delta_ln1_b', 'delta_ffn_w_gate', 'delta_ffn_w_up', 'delta_ffn_conv_w', 'delta_ffn_conv_b', 'delta_ffn_w_down', 'delta_ln2_g', 'delta_ln2_b', 'new_m_w_in', 'new_m_b_in', 'new_m_attn_sinks', 'new_m_s5_a_re', 'new_m_s5_a_im', 'new_m_s5_b_re', 'new_m_s5_b_im', 'new_m_s5_c_re', 'new_m_s5_c_im', 'new_m_s5_d', 'new_m_s5_log_dt', 'new_m_s5_glu_w', 'new_m_s5_glu_b', 'new_m_lru_conv_w', 'new_m_lru_conv_b', 'new_m_lru_wx', 'new_m_lru_bx', 'new_m_lru_wa', 'new_m_lru_ba', 'new_m_lru_a_param', 'new_m_mix_norm_g', 'new_m_w_out', 'new_m_b_out', 'new_m_ln1_g', 'new_m_ln1_b', 'new_m_ffn_w_gate', 'new_m_ffn_w_up', 'new_m_ffn_conv_w', 'new_m_ffn_conv_b', 'new_m_ffn_w_down', 'new_m_ln2_g', 'new_m_ln2_b', 'new_v_w_in', 'new_v_b_in', 'new_v_attn_sinks', 'new_v_s5_a_re', 'new_v_s5_a_im', 'new_v_s5_b_re', 'new_v_s5_b_im', 'new_v_s5_c_re', 'new_v_s5_c_im', 'new_v_s5_d', 'new_v_s5_log_dt', 'new_v_s5_glu_w', 'new_v_s5_glu_b', 'new_v_lru_conv_w', 'new_v_lru_conv_b', 'new_v_lru_wx', 'new_v_lru_bx', 'new_v_lru_wa', 'new_v_lru_ba', 'new_v_lru_a_param', 'new_v_mix_norm_g', 'new_v_w_out', 'new_v_b_out', 'new_v_ln1_g', 'new_v_ln1_b', 'new_v_ffn_w_gate', 'new_v_ffn_w_up', 'new_v_ffn_conv_w', 'new_v_ffn_conv_b', 'new_v_ffn_w_down', 'new_v_ln2_g', 'new_v_ln2_b']
TWIN_LEAF_KINDS = {'loss': 'loss', 'grad_x': 'grad_x', 'grad_w_in': 'grad_w', 'grad_b_in': 'grad_w', 'grad_attn_sinks': 'grad_w', 'grad_s5_a_re': 'grad_w', 'grad_s5_a_im': 'grad_w', 'grad_s5_b_re': 'grad_w', 'grad_s5_b_im': 'grad_w', 'grad_s5_c_re': 'grad_w', 'grad_s5_c_im': 'grad_w', 'grad_s5_d': 'grad_w', 'grad_s5_log_dt': 'grad_w', 'grad_s5_glu_w': 'grad_w', 'grad_s5_glu_b': 'grad_w', 'grad_lru_conv_w': 'grad_w', 'grad_lru_conv_b': 'grad_w', 'grad_lru_wx': 'grad_w', 'grad_lru_bx': 'grad_w', 'grad_lru_wa': 'grad_w', 'grad_lru_ba': 'grad_w', 'grad_lru_a_param': 'grad_w', 'grad_mix_norm_g': 'grad_w', 'grad_w_out': 'grad_w', 'grad_b_out': 'grad_w', 'grad_ln1_g': 'grad_w', 'grad_ln1_b': 'grad_w', 'grad_ffn_w_gate': 'grad_w', 'grad_ffn_w_up': 'grad_w', 'grad_ffn_conv_w': 'grad_w', 'grad_ffn_conv_b': 'grad_w', 'grad_ffn_w_down': 'grad_w', 'grad_ln2_g': 'grad_w', 'grad_ln2_b': 'grad_w', 'delta_w_in': 'delta_w', 'delta_b_in': 'delta_w', 'delta_attn_sinks': 'delta_w', 'delta_s5_a_re': 'delta_w', 'delta_s5_a_im': 'delta_w', 'delta_s5_b_re': 'delta_w', 'delta_s5_b_im': 'delta_w', 'delta_s5_c_re': 'delta_w', 'delta_s5_c_im': 'delta_w', 'delta_s5_d': 'delta_w', 'delta_s5_log_dt': 'delta_w', 'delta_s5_glu_w': 'delta_w', 'delta_s5_glu_b': 'delta_w', 'delta_lru_conv_w': 'delta_w', 'delta_lru_conv_b': 'delta_w', 'delta_lru_wx': 'delta_w', 'delta_lru_bx': 'delta_w', 'delta_lru_wa': 'delta_w', 'delta_lru_ba': 'delta_w', 'delta_lru_a_param': 'delta_w', 'delta_mix_norm_g': 'delta_w', 'delta_w_out': 'delta_w', 'delta_b_out': 'delta_w', 'delta_ln1_g': 'delta_w', 'delta_ln1_b': 'delta_w', 'delta_ffn_w_gate': 'delta_w', 'delta_ffn_w_up': 'delta_w', 'delta_ffn_conv_w': 'delta_w', 'delta_ffn_conv_b': 'delta_w', 'delta_ffn_w_down': 'delta_w', 'delta_ln2_g': 'delta_w', 'delta_ln2_b': 'delta_w', 'new_m_w_in': 'new_m', 'new_m_b_in': 'new_m', 'new_m_attn_sinks': 'new_m', 'new_m_s5_a_re': 'new_m', 'new_m_s5_a_im': 'new_m', 'new_m_s5_b_re': 'new_m', 'new_m_s5_b_im': 'new_m', 'new_m_s5_c_re': 'new_m', 'new_m_s5_c_im': 'new_m', 'new_m_s5_d': 'new_m', 'new_m_s5_log_dt': 'new_m', 'new_m_s5_glu_w': 'new_m', 'new_m_s5_glu_b': 'new_m', 'new_m_lru_conv_w': 'new_m', 'new_m_lru_conv_b': 'new_m', 'new_m_lru_wx': 'new_m', 'new_m_lru_bx': 'new_m', 'new_m_lru_wa': 'new_m', 'new_m_lru_ba': 'new_m', 'new_m_lru_a_param': 'new_m', 'new_m_mix_norm_g': 'new_m', 'new_m_w_out': 'new_m', 'new_m_b_out': 'new_m', 'new_m_ln1_g': 'new_m', 'new_m_ln1_b': 'new_m', 'new_m_ffn_w_gate': 'new_m', 'new_m_ffn_w_up': 'new_m', 'new_m_ffn_conv_w': 'new_m', 'new_m_ffn_conv_b': 'new_m', 'new_m_ffn_w_down': 'new_m', 'new_m_ln2_g': 'new_m', 'new_m_ln2_b': 'new_m', 'new_v_w_in': 'new_v', 'new_v_b_in': 'new_v', 'new_v_attn_sinks': 'new_v', 'new_v_s5_a_re': 'new_v', 'new_v_s5_a_im': 'new_v', 'new_v_s5_b_re': 'new_v', 'new_v_s5_b_im': 'new_v', 'new_v_s5_c_re': 'new_v', 'new_v_s5_c_im': 'new_v', 'new_v_s5_d': 'new_v', 'new_v_s5_log_dt': 'new_v', 'new_v_s5_glu_w': 'new_v', 'new_v_s5_glu_b': 'new_v', 'new_v_lru_conv_w': 'new_v', 'new_v_lru_conv_b': 'new_v', 'new_v_lru_wx': 'new_v', 'new_v_lru_bx': 'new_v', 'new_v_lru_wa': 'new_v', 'new_v_lru_ba': 'new_v', 'new_v_lru_a_param': 'new_v', 'new_v_mix_norm_g': 'new_v', 'new_v_w_out': 'new_v', 'new_v_b_out': 'new_v', 'new_v_ln1_g': 'new_v', 'new_v_ln1_b': 'new_v', 'new_v_ffn_w_gate': 'new_v', 'new_v_ffn_w_up': 'new_v', 'new_v_ffn_conv_w': 'new_v', 'new_v_ffn_conv_b': 'new_v', 'new_v_ffn_w_down': 'new_v', 'new_v_ln2_g': 'new_v', 'new_v_ln2_b': 'new_v'}


def _forward(args):
    return _fwd_reference(*[args[k] for k in FWD_PARAMS])


def _output_shape():
    def fwd():
        inp = _fwd_setup_inputs(0)
        return _fwd_reference(*[inp[k] for k in FWD_PARAMS])
    out = _jax.eval_shape(fwd)
    return out.shape, out.dtype

N_MICROBATCH = 1
ADAM_LR = 0.001
ADAM_B1 = 0.9
ADAM_B2 = 0.999
ADAM_EPS = 1e-08
ADAM_WD = 0.01
ADAM_STEP = 10
PER_EXAMPLE_BATCH_AXIS = {'x': 0, 'loss_target': 0}
SHARED_INPUTS = []
_WEIGHT_DTYPES = {'w_in': _jnp.float32, 'b_in': _jnp.float32, 'attn_sinks': _jnp.float32, 's5_a_re': _jnp.float32, 's5_a_im': _jnp.float32, 's5_b_re': _jnp.float32, 's5_b_im': _jnp.float32, 's5_c_re': _jnp.float32, 's5_c_im': _jnp.float32, 's5_d': _jnp.float32, 's5_log_dt': _jnp.float32, 's5_glu_w': _jnp.float32, 's5_glu_b': _jnp.float32, 'lru_conv_w': _jnp.float32, 'lru_conv_b': _jnp.float32, 'lru_wx': _jnp.float32, 'lru_bx': _jnp.float32, 'lru_wa': _jnp.float32, 'lru_ba': _jnp.float32, 'lru_a_param': _jnp.float32, 'mix_norm_g': _jnp.float32, 'w_out': _jnp.float32, 'b_out': _jnp.float32, 'ln1_g': _jnp.float32, 'ln1_b': _jnp.float32, 'ffn_w_gate': _jnp.float32, 'ffn_w_up': _jnp.float32, 'ffn_conv_w': _jnp.float32, 'ffn_conv_b': _jnp.float32, 'ffn_w_down': _jnp.float32, 'ln2_g': _jnp.float32, 'ln2_b': _jnp.float32}
MOMENT_SCALE = {'w_in': 1.158020e-01, 'b_in': 1.622333e+00, 'attn_sinks': 3.140112e-02, 's5_a_re': 8.773567e-03, 's5_a_im': 6.383116e-03, 's5_b_re': 3.752171e-03, 's5_b_im': 3.628648e-03, 's5_c_re': 6.878007e-03, 's5_c_im': 7.313243e-03, 's5_d': 1.687785e-01, 's5_log_dt': 5.774251e+00, 's5_glu_w': 3.124015e-02, 's5_glu_b': 6.337421e-02, 'lru_conv_w': 1.076372e-01, 'lru_conv_b': 6.593186e-01, 'lru_wx': 4.873171e-02, 'lru_bx': 3.809067e-02, 'lru_wa': 2.703153e-02, 'lru_ba': 2.587120e-02, 'lru_a_param': 5.469559e-02, 'mix_norm_g': 1.309462e-01, 'w_out': 3.040312e-01, 'b_out': 7.631740e-01, 'ln1_g': 1.524442e+00, 'ln1_b': 1.322186e+00, 'ffn_w_gate': 3.479252e-02, 'ffn_w_up': 3.379810e-02, 'ffn_conv_w': 3.468885e-02, 'ffn_conv_b': 3.349548e-02, 'ffn_w_down': 1.333792e-01, 'ln2_g': 6.406948e+01, 'ln2_b': 1.098664e+01}


def _to_microbatches(a, axis):
    t = _jnp.moveaxis(a, axis, 0)
    t = t.reshape((N_MICROBATCH, t.shape[0] // N_MICROBATCH) + t.shape[1:])
    return _jnp.moveaxis(t, 1, axis + 1)


def setup_inputs(seed: int = 0) -> dict:
    inp = _fwd_setup_inputs(seed)
    key = _jax.random.fold_in(_jax.random.key(seed), 7919)
    shape, _ = _output_shape()
    out = dict(inp)
    out["loss_target"] = _jax.random.normal(_jax.random.fold_in(key, 0), shape, _jnp.float32)
    for i, name in enumerate(TWIN_WEIGHTS):
        w = inp[name].astype(_jnp.float32)
        if MOMENT_SCALE is None:
            s = _jnp.sqrt(_jnp.mean(_jnp.square(w)) + 1e-30)
        else:
            s = MOMENT_SCALE[name]
        km, kv = _jax.random.split(_jax.random.fold_in(key, i + 1))
        out[name] = w
        out["m_" + name] = s * _jax.random.normal(km, w.shape, _jnp.float32)
        out["v_" + name] = (s * s) * _jax.random.uniform(kv, w.shape, _jnp.float32, 0.5, 1.5)
    if N_MICROBATCH > 1:
        for name, axis in PER_EXAMPLE_BATCH_AXIS.items():
            out[name] = _to_microbatches(out[name], axis)
    return {'x': out['x'], 'w_in': out['w_in'], 'b_in': out['b_in'], 'attn_sinks': out['attn_sinks'], 's5_a_re': out['s5_a_re'], 's5_a_im': out['s5_a_im'], 's5_b_re': out['s5_b_re'], 's5_b_im': out['s5_b_im'], 's5_c_re': out['s5_c_re'], 's5_c_im': out['s5_c_im'], 's5_d': out['s5_d'], 's5_log_dt': out['s5_log_dt'], 's5_glu_w': out['s5_glu_w'], 's5_glu_b': out['s5_glu_b'], 'lru_conv_w': out['lru_conv_w'], 'lru_conv_b': out['lru_conv_b'], 'lru_wx': out['lru_wx'], 'lru_bx': out['lru_bx'], 'lru_wa': out['lru_wa'], 'lru_ba': out['lru_ba'], 'lru_a_param': out['lru_a_param'], 'mix_norm_g': out['mix_norm_g'], 'w_out': out['w_out'], 'b_out': out['b_out'], 'ln1_g': out['ln1_g'], 'ln1_b': out['ln1_b'], 'ffn_w_gate': out['ffn_w_gate'], 'ffn_w_up': out['ffn_w_up'], 'ffn_conv_w': out['ffn_conv_w'], 'ffn_conv_b': out['ffn_conv_b'], 'ffn_w_down': out['ffn_w_down'], 'ln2_g': out['ln2_g'], 'ln2_b': out['ln2_b'], 'loss_target': out['loss_target'], 'm_w_in': out['m_w_in'], 'm_b_in': out['m_b_in'], 'm_attn_sinks': out['m_attn_sinks'], 'm_s5_a_re': out['m_s5_a_re'], 'm_s5_a_im': out['m_s5_a_im'], 'm_s5_b_re': out['m_s5_b_re'], 'm_s5_b_im': out['m_s5_b_im'], 'm_s5_c_re': out['m_s5_c_re'], 'm_s5_c_im': out['m_s5_c_im'], 'm_s5_d': out['m_s5_d'], 'm_s5_log_dt': out['m_s5_log_dt'], 'm_s5_glu_w': out['m_s5_glu_w'], 'm_s5_glu_b': out['m_s5_glu_b'], 'm_lru_conv_w': out['m_lru_conv_w'], 'm_lru_conv_b': out['m_lru_conv_b'], 'm_lru_wx': out['m_lru_wx'], 'm_lru_bx': out['m_lru_bx'], 'm_lru_wa': out['m_lru_wa'], 'm_lru_ba': out['m_lru_ba'], 'm_lru_a_param': out['m_lru_a_param'], 'm_mix_norm_g': out['m_mix_norm_g'], 'm_w_out': out['m_w_out'], 'm_b_out': out['m_b_out'], 'm_ln1_g': out['m_ln1_g'], 'm_ln1_b': out['m_ln1_b'], 'm_ffn_w_gate': out['m_ffn_w_gate'], 'm_ffn_w_up': out['m_ffn_w_up'], 'm_ffn_conv_w': out['m_ffn_conv_w'], 'm_ffn_conv_b': out['m_ffn_conv_b'], 'm_ffn_w_down': out['m_ffn_w_down'], 'm_ln2_g': out['m_ln2_g'], 'm_ln2_b': out['m_ln2_b'], 'v_w_in': out['v_w_in'], 'v_b_in': out['v_b_in'], 'v_attn_sinks': out['v_attn_sinks'], 'v_s5_a_re': out['v_s5_a_re'], 'v_s5_a_im': out['v_s5_a_im'], 'v_s5_b_re': out['v_s5_b_re'], 'v_s5_b_im': out['v_s5_b_im'], 'v_s5_c_re': out['v_s5_c_re'], 'v_s5_c_im': out['v_s5_c_im'], 'v_s5_d': out['v_s5_d'], 'v_s5_log_dt': out['v_s5_log_dt'], 'v_s5_glu_w': out['v_s5_glu_w'], 'v_s5_glu_b': out['v_s5_glu_b'], 'v_lru_conv_w': out['v_lru_conv_w'], 'v_lru_conv_b': out['v_lru_conv_b'], 'v_lru_wx': out['v_lru_wx'], 'v_lru_bx': out['v_lru_bx'], 'v_lru_wa': out['v_lru_wa'], 'v_lru_ba': out['v_lru_ba'], 'v_lru_a_param': out['v_lru_a_param'], 'v_mix_norm_g': out['v_mix_norm_g'], 'v_w_out': out['v_w_out'], 'v_b_out': out['v_b_out'], 'v_ln1_g': out['v_ln1_g'], 'v_ln1_b': out['v_ln1_b'], 'v_ffn_w_gate': out['v_ffn_w_gate'], 'v_ffn_w_up': out['v_ffn_w_up'], 'v_ffn_conv_w': out['v_ffn_conv_w'], 'v_ffn_conv_b': out['v_ffn_conv_b'], 'v_ffn_w_down': out['v_ffn_w_down'], 'v_ln2_g': out['v_ln2_g'], 'v_ln2_b': out['v_ln2_b']}


def _loss(weights, diff, rest, loss_target):
    with _jax.named_scope("forward"):
        args = {**rest, TWIN_DIFF_INPUT: diff, **{k: w.astype(_WEIGHT_DTYPES[k]) for k, w in weights.items()}}
        y = _forward(args)
    with _jax.named_scope("loss_head"):
        err = _jnp.square(y.astype(_jnp.float32) - loss_target)
        return 0.5 * _jnp.sum(_jnp.mean(err, axis=-1)) if err.ndim else 0.5 * err


def _adamw(w, g, m, v):
    m = ADAM_B1 * m + (1.0 - ADAM_B1) * g
    v = ADAM_B2 * v + (1.0 - ADAM_B2) * _jnp.square(g)
    m_hat = m / (1.0 - ADAM_B1 ** ADAM_STEP)
    v_hat = v / (1.0 - ADAM_B2 ** ADAM_STEP)
    delta = -ADAM_LR * (m_hat / (_jnp.sqrt(v_hat) + ADAM_EPS) + ADAM_WD * w)
    return delta, m, v


def reference(x, w_in, b_in, attn_sinks, s5_a_re, s5_a_im, s5_b_re, s5_b_im, s5_c_re, s5_c_im, s5_d, s5_log_dt, s5_glu_w, s5_glu_b, lru_conv_w, lru_conv_b, lru_wx, lru_bx, lru_wa, lru_ba, lru_a_param, mix_norm_g, w_out, b_out, ln1_g, ln1_b, ffn_w_gate, ffn_w_up, ffn_conv_w, ffn_conv_b, ffn_w_down, ln2_g, ln2_b, loss_target, m_w_in, m_b_in, m_attn_sinks, m_s5_a_re, m_s5_a_im, m_s5_b_re, m_s5_b_im, m_s5_c_re, m_s5_c_im, m_s5_d, m_s5_log_dt, m_s5_glu_w, m_s5_glu_b, m_lru_conv_w, m_lru_conv_b, m_lru_wx, m_lru_bx, m_lru_wa, m_lru_ba, m_lru_a_param, m_mix_norm_g, m_w_out, m_b_out, m_ln1_g, m_ln1_b, m_ffn_w_gate, m_ffn_w_up, m_ffn_conv_w, m_ffn_conv_b, m_ffn_w_down, m_ln2_g, m_ln2_b, v_w_in, v_b_in, v_attn_sinks, v_s5_a_re, v_s5_a_im, v_s5_b_re, v_s5_b_im, v_s5_c_re, v_s5_c_im, v_s5_d, v_s5_log_dt, v_s5_glu_w, v_s5_glu_b, v_lru_conv_w, v_lru_conv_b, v_lru_wx, v_lru_bx, v_lru_wa, v_lru_ba, v_lru_a_param, v_mix_norm_g, v_w_out, v_b_out, v_ln1_g, v_ln1_b, v_ffn_w_gate, v_ffn_w_up, v_ffn_conv_w, v_ffn_conv_b, v_ffn_w_down, v_ln2_g, v_ln2_b):
    given = dict(x=x, w_in=w_in, b_in=b_in, attn_sinks=attn_sinks, s5_a_re=s5_a_re, s5_a_im=s5_a_im, s5_b_re=s5_b_re, s5_b_im=s5_b_im, s5_c_re=s5_c_re, s5_c_im=s5_c_im, s5_d=s5_d, s5_log_dt=s5_log_dt, s5_glu_w=s5_glu_w, s5_glu_b=s5_glu_b, lru_conv_w=lru_conv_w, lru_conv_b=lru_conv_b, lru_wx=lru_wx, lru_bx=lru_bx, lru_wa=lru_wa, lru_ba=lru_ba, lru_a_param=lru_a_param, mix_norm_g=mix_norm_g, w_out=w_out, b_out=b_out, ln1_g=ln1_g, ln1_b=ln1_b, ffn_w_gate=ffn_w_gate, ffn_w_up=ffn_w_up, ffn_conv_w=ffn_conv_w, ffn_conv_b=ffn_conv_b, ffn_w_down=ffn_w_down, ln2_g=ln2_g, ln2_b=ln2_b, loss_target=loss_target, m_w_in=m_w_in, m_b_in=m_b_in, m_attn_sinks=m_attn_sinks, m_s5_a_re=m_s5_a_re, m_s5_a_im=m_s5_a_im, m_s5_b_re=m_s5_b_re, m_s5_b_im=m_s5_b_im, m_s5_c_re=m_s5_c_re, m_s5_c_im=m_s5_c_im, m_s5_d=m_s5_d, m_s5_log_dt=m_s5_log_dt, m_s5_glu_w=m_s5_glu_w, m_s5_glu_b=m_s5_glu_b, m_lru_conv_w=m_lru_conv_w, m_lru_conv_b=m_lru_conv_b, m_lru_wx=m_lru_wx, m_lru_bx=m_lru_bx, m_lru_wa=m_lru_wa, m_lru_ba=m_lru_ba, m_lru_a_param=m_lru_a_param, m_mix_norm_g=m_mix_norm_g, m_w_out=m_w_out, m_b_out=m_b_out, m_ln1_g=m_ln1_g, m_ln1_b=m_ln1_b, m_ffn_w_gate=m_ffn_w_gate, m_ffn_w_up=m_ffn_w_up, m_ffn_conv_w=m_ffn_conv_w, m_ffn_conv_b=m_ffn_conv_b, m_ffn_w_down=m_ffn_w_down, m_ln2_g=m_ln2_g, m_ln2_b=m_ln2_b, v_w_in=v_w_in, v_b_in=v_b_in, v_attn_sinks=v_attn_sinks, v_s5_a_re=v_s5_a_re, v_s5_a_im=v_s5_a_im, v_s5_b_re=v_s5_b_re, v_s5_b_im=v_s5_b_im, v_s5_c_re=v_s5_c_re, v_s5_c_im=v_s5_c_im, v_s5_d=v_s5_d, v_s5_log_dt=v_s5_log_dt, v_s5_glu_w=v_s5_glu_w, v_s5_glu_b=v_s5_glu_b, v_lru_conv_w=v_lru_conv_w, v_lru_conv_b=v_lru_conv_b, v_lru_wx=v_lru_wx, v_lru_bx=v_lru_bx, v_lru_wa=v_lru_wa, v_lru_ba=v_lru_ba, v_lru_a_param=v_lru_a_param, v_mix_norm_g=v_mix_norm_g, v_w_out=v_w_out, v_b_out=v_b_out, v_ln1_g=v_ln1_g, v_ln1_b=v_ln1_b, v_ffn_w_gate=v_ffn_w_gate, v_ffn_w_up=v_ffn_w_up, v_ffn_conv_w=v_ffn_conv_w, v_ffn_conv_b=v_ffn_conv_b, v_ffn_w_down=v_ffn_w_down, v_ln2_g=v_ln2_g, v_ln2_b=v_ln2_b)
    weights = {n: given[n] for n in TWIN_WEIGHTS}
    shared = {n: given[n] for n in SHARED_INPUTS}
    per_example = {n: given[n] for n in ['x']}
    grad_fn = _jax.value_and_grad(_loss, argnums=(0, 1))

    def one_microbatch(ex, loss_target):
        ex = dict(ex)
        diff = ex.pop(TWIN_DIFF_INPUT)
        return grad_fn(weights, diff, {**shared, **ex}, loss_target)

    if N_MICROBATCH == 1:
        loss, (grad_w, grad_x) = one_microbatch(per_example, given["loss_target"])
    else:
        def body(carry, xs):
            loss_sum, grad_sum = carry
            l_k, (gw_k, gx_k) = one_microbatch(xs[0], xs[1])
            with _jax.named_scope("update"):
                return (loss_sum + l_k, _jax.tree.map(_jnp.add, grad_sum, gw_k)), gx_k

        init = (_jnp.zeros((), _jnp.float32), _jax.tree.map(_jnp.zeros_like, weights))
        (loss, grad_w), grad_x = _jax.lax.scan(body, init, (per_example, given["loss_target"]))
    with _jax.named_scope("update"):
        delta_w, new_m, new_v = {}, {}, {}
        for n in TWIN_WEIGHTS:
            delta_w[n], new_m[n], new_v[n] = _adamw(weights[n], grad_w[n], given["m_" + n], given["v_" + n])
    return (loss, grad_x, *[grad_w[n] for n in TWIN_WEIGHTS], *[delta_w[n] for n in TWIN_WEIGHTS],
            *[new_m[n] for n in TWIN_WEIGHTS], *[new_v[n] for n in TWIN_WEIGHTS])
```

```python
import functools
import math

import jax
import jax.numpy as jnp
from jax import lax
from jax.experimental import pallas as pl
from jax.experimental.pallas import tpu as pltpu

F32 = jnp.float32
BF16 = jnp.bfloat16
MESH = pl.DeviceIdType.MESH
ANY = pl.BlockSpec(memory_space=pl.ANY)

D_MODEL = 1024
DEPTH = 4
HEAD_DIM = 64
N_Q_HEADS = 8
N_KV_HEADS = 2
Q_PER_KV = 4
D_ATTN = 512
D_KV = 128
ATTN_BLOCK = 128
ROPE_THETA = 10000.0
D_S5 = 256
S5_GROUP = 16
S5_GROUPS = 16
S5_STATE = 64
N_STATE = S5_GROUPS * S5_STATE
D_LRU = 256
LRU_HEADS = 4
LRU_HEAD_DIM = 64
LRU_CONV = 4
LRU_C = 8.0
D_IN = 1536
D_FF = 2816
FFN_CONV = 3
N_CHIPS = 4
N_DEV = 8
IN_SH = D_IN // N_CHIPS
FF_SH = D_FF // N_CHIPS
OUT_SH = D_MODEL // N_CHIPS
ALPHA = (2 * DEPTH) ** 0.25
LN_EPS = 1e-5
RMS_EPS = 1e-6
ADAM_LR = 0.001
ADAM_B1 = 0.9
ADAM_B2 = 0.999
ADAM_EPS = 1e-08
ADAM_WD = 0.01
ADAM_STEP = 10

SUBLANES = 8
VMEM_MB = 56


def _params(sem):
    return pltpu.CompilerParams(dimension_semantics=sem, vmem_limit_bytes=VMEM_MB << 20)


def _row_tile(t, pref):
    return min(t, pref)


def _matmul(name, a, b, *, a_blk, a_map, b_blk, b_map, out_shape, o_blk, o_map, grid, dims,
            out_dtype=F32, bias=None, bias_blk=None, bias_map=None, add=None, add_scale=1.0):
    nk = grid[2]
    acc_shape = tuple(d for d in o_blk if d is not None)

    def kern(*refs):
        a_ref, b_ref = refs[0], refs[1]
        p = 2
        bias_ref = add_ref = None
        if bias is not None:
            bias_ref = refs[p]
            p += 1
        if add is not None:
            add_ref = refs[p]
            p += 1
        o_ref, acc = refs[p], refs[p + 1]
        k = pl.program_id(2)

        @pl.when(k == 0)
        def _():
            acc[...] = jnp.zeros_like(acc)

        acc[...] += lax.dot_general(a_ref[...].astype(BF16), b_ref[...].astype(BF16), (dims, ((), ())),
                                    preferred_element_type=F32)

        @pl.when(k == nk - 1)
        def _():
            r = acc[...]
            if bias_ref is not None:
                r = r + bias_ref[...]
            if add_ref is not None:
                r = r + add_scale * add_ref[...]
            o_ref[...] = r.astype(out_dtype)

    in_specs = [pl.BlockSpec(a_blk, a_map), pl.BlockSpec(b_blk, b_map)]
    args = [a, b]
    if bias is not None:
        in_specs.append(pl.BlockSpec(bias_blk, bias_map))
        args.append(bias)
    if add is not None:
        in_specs.append(pl.BlockSpec(o_blk, lambda i, j, k: o_map(i, j)))
        args.append(add)
    return pl.pallas_call(
        kern, name=name, grid=grid, in_specs=in_specs,
        out_specs=pl.BlockSpec(o_blk, lambda i, j, k: o_map(i, j)),
        out_shape=jax.ShapeDtypeStruct(out_shape, out_dtype),
        scratch_shapes=[pltpu.VMEM(acc_shape, F32)],
        compiler_params=_params(("parallel", "parallel", "arbitrary")),
    )(*args)


NN = ((1,), (0,))
NT = ((1,), (1,))
TN = ((0,), (0,))
TM = 512


def _sigmoid(x):
    return 1.0 / (1.0 + jnp.exp(-x))


_GELU_C = math.sqrt(2.0 / math.pi)


def _gelu(x):
    return 0.5 * x * (1.0 + jnp.tanh(_GELU_C * (x + 0.044715 * x * x * x)))


def _gelu_grad(x):
    th = jnp.tanh(_GELU_C * (x + 0.044715 * x * x * x))
    return 0.5 * (1.0 + th) + 0.5 * x * (1.0 - th * th) * _GELU_C * (1.0 + 3 * 0.044715 * x * x)


def _rope_swap(t):
    lane = lax.broadcasted_iota(jnp.int32, t.shape, 1)
    lo = (lane % HEAD_DIM) < (HEAD_DIM // 2)
    return jnp.where(lo, pltpu.roll(t, 128 - HEAD_DIM // 2, 1), pltpu.roll(t, HEAD_DIM // 2, 1))


def _qkv_post(proj, cos, sin_s):
    t = proj.shape[0]
    tm = _row_tile(t, TM)

    def kern(p_ref, c_ref, s_ref, o_ref):
        c = c_ref[...]
        s = s_ref[...]
        for ch in range(6):
            x = p_ref[:, ch * 128:(ch + 1) * 128]
            if ch < 5:
                x = x * c + _rope_swap(x) * s
            if ch < 4:
                x = x * (HEAD_DIM ** -0.5)
            o_ref[:, ch * 128:(ch + 1) * 128] = x.astype(BF16)

    return pl.pallas_call(
        kern, name="qkv_post", grid=(t // tm,),
        in_specs=[pl.BlockSpec((tm, 768), lambda i: (i, 0)), pl.BlockSpec((tm, 128), lambda i: (i, 0)),
                  pl.BlockSpec((tm, 128), lambda i: (i, 0))],
        out_specs=pl.BlockSpec((tm, 768), lambda i: (i, 0)),
        out_shape=jax.ShapeDtypeStruct((t, 768), BF16),
        compiler_params=_params(("parallel",)),
    )(proj, cos, sin_s)


def _attn_mask(i):
    qi = lax.broadcasted_iota(jnp.int32, (ATTN_BLOCK, 2 * ATTN_BLOCK), 0)
    si = lax.broadcasted_iota(jnp.int32, (ATTN_BLOCK, 2 * ATTN_BLOCK), 1)
    diff = qi + ATTN_BLOCK - si
    return (diff >= 0) & (diff < ATTN_BLOCK) & ((si >= ATTN_BLOCK) | (i > 0))


def _attn_fwd(qkv, sinks):
    t = qkv.shape[0]
    nb = t // ATTN_BLOCK

    def kern(q_ref, kp_ref, kc_ref, vp_ref, vc_ref, s_ref, o_ref, l_ref):
        i = pl.program_id(0)
        valid = _attn_mask(i)
        kband = jnp.concatenate([kp_ref[...], kc_ref[...]], axis=0)
        vband = jnp.concatenate([vp_ref[...], vc_ref[...]], axis=0)
        for h in range(N_Q_HEADS):
            kh = h // Q_PER_KV
            q = q_ref[:, h * HEAD_DIM:(h + 1) * HEAD_DIM]
            k = kband[:, kh * HEAD_DIM:(kh + 1) * HEAD_DIM]
            v = vband[:, kh * HEAD_DIM:(kh + 1) * HEAD_DIM]
            s = lax.dot_general(q, k, (NT, ((), ())), preferred_element_type=F32)
            s = jnp.where(valid, s, -jnp.inf)
            sink = s_ref[0:1, h:h + 1]
            m = jnp.maximum(jnp.max(s, axis=-1, keepdims=True), sink)
            e = jnp.exp(s - m)
            denom = jnp.sum(e, axis=-1, keepdims=True) + jnp.exp(sink - m)
            p = (e / denom).astype(BF16)
            o_ref[:, h * HEAD_DIM:(h + 1) * HEAD_DIM] = lax.dot_general(
                p, v, (NN, ((), ())), preferred_element_type=F32)
            l_ref[:, h:h + 1] = m + jnp.log(denom)

    blk = lambda w, f: pl.BlockSpec((ATTN_BLOCK, w), f)
    return pl.pallas_call(
        kern, name="attn_fwd", grid=(nb,),
        in_specs=[blk(512, lambda i: (i, 0)),
                  blk(128, lambda i: (jnp.maximum(i - 1, 0), 4)), blk(128, lambda i: (i, 4)),
                  blk(128, lambda i: (jnp.maximum(i - 1, 0), 5)), blk(128, lambda i: (i, 5)),
                  pl.BlockSpec((1, N_Q_HEADS), lambda i: (0, 0))],
        out_specs=[blk(512, lambda i: (i, 0)), blk(N_Q_HEADS, lambda i: (i, 0))],
        out_shape=[jax.ShapeDtypeStruct((t, D_ATTN), F32), jax.ShapeDtypeStruct((t, N_Q_HEADS), F32)],
        compiler_params=_params(("parallel",)),
    )(qkv, qkv, qkv, qkv, qkv, sinks)


def _attn_bwd(qkv, o, do, lse, sinks):
    t = qkv.shape[0]
    nb = t // ATTN_BLOCK

    def kern(q_ref, kp_ref, kc_ref, vp_ref, vc_ref, o_ref, do_ref, l_ref, s_ref,
             dq_ref, dk_ref, dv_ref, ds_ref, ck, cv):
        i = pl.program_id(0)

        @pl.when(i == 0)
        def _():
            ds_ref[...] = jnp.zeros_like(ds_ref)
            ck[...] = jnp.zeros_like(ck)
            cv[...] = jnp.zeros_like(cv)

        @pl.when(i < nb)
        def _():
            valid = _attn_mask(i)
            kband = jnp.concatenate([kp_ref[...], kc_ref[...]], axis=0)
            vband = jnp.concatenate([vp_ref[...], vc_ref[...]], axis=0)
            dkb = [jnp.zeros((2 * ATTN_BLOCK, HEAD_DIM), F32) for _ in range(N_KV_HEADS)]
            dvb = [jnp.zeros((2 * ATTN_BLOCK, HEAD_DIM), F32) for _ in range(N_KV_HEADS)]
            for h in range(N_Q_HEADS):
                kh = h // Q_PER_KV
                sl = slice(h * HEAD_DIM, (h + 1) * HEAD_DIM)
                q = q_ref[:, sl]
                k = kband[:, kh * HEAD_DIM:(kh + 1) * HEAD_DIM]
                v = vband[:, kh * HEAD_DIM:(kh + 1) * HEAD_DIM]
                d_o = do_ref[:, sl]
                s = lax.dot_general(q, k, (NT, ((), ())), preferred_element_type=F32)
                lse_h = l_ref[:, h:h + 1]
                p = jnp.where(valid, jnp.exp(s - lse_h), 0.0)
                delta = jnp.sum(d_o * o_ref[:, sl], axis=-1, keepdims=True)
                dob = d_o.astype(BF16)
                dp = lax.dot_general(dob, v, (NT, ((), ())), preferred_element_type=F32)
                dsc = (p * (dp - delta)).astype(BF16)
                dq_ref[:, sl] = lax.dot_general(dsc, k, (NN, ((), ())), preferred_element_type=F32)
                dkb[kh] = dkb[kh] + lax.dot_general(dsc, q, (TN, ((), ())), preferred_element_type=F32)
                dvb[kh] = dvb[kh] + lax.dot_general(p.astype(BF16), dob, (TN, ((), ())),
                                                    preferred_element_type=F32)
                psink = jnp.exp(s_ref[0:1, h:h + 1] - lse_h)
                ds_ref[0:1, h:h + 1] += -jnp.sum(psink * delta, axis=0, keepdims=True)
            dk_band = jnp.concatenate(dkb, axis=1)
            dv_band = jnp.concatenate(dvb, axis=1)
            dk_ref[...] = ck[...] + dk_band[:ATTN_BLOCK]
            dv_ref[...] = cv[...] + dv_band[:ATTN_BLOCK]
            ck[...] = dk_band[ATTN_BLOCK:]
            cv[...] = dv_band[ATTN_BLOCK:]

        @pl.when(i == nb)
        def _():
            dk_ref[...] = ck[...]
            dv_ref[...] = cv[...]

    blk = lambda w, f: pl.BlockSpec((ATTN_BLOCK, w), f)
    cur = lambda i: jnp.minimum(i, nb - 1)
    prev = lambda i: jnp.clip(i - 1, 0, nb - 1)
    return pl.pallas_call(
        kern, name="attn_bwd", grid=(nb + 1,),
        in_specs=[blk(512, lambda i: (cur(i), 0)),
                  blk(128, lambda i: (prev(i), 4)), blk(128, lambda i: (cur(i), 4)),
                  blk(128, lambda i: (prev(i), 5)), blk(128, lambda i: (cur(i), 5)),
                  blk(512, lambda i: (cur(i), 0)), blk(512, lambda i: (cur(i), 0)),
                  blk(N_Q_HEADS, lambda i: (cur(i), 0)),
                  pl.BlockSpec((1, N_Q_HEADS), lambda i: (0, 0))],
        out_specs=[blk(512, lambda i: (cur(i), 0)), blk(128, lambda i: (prev(i), 0)),
                   blk(128, lambda i: (prev(i), 0)), pl.BlockSpec((1, N_Q_HEADS), lambda i: (0, 0))],
        out_shape=[jax.ShapeDtypeStruct((t, D_ATTN), F32), jax.ShapeDtypeStruct((t, D_KV), F32),
                   jax.ShapeDtypeStruct((t, D_KV), F32), jax.ShapeDtypeStruct((1, N_Q_HEADS), F32)],
        scratch_shapes=[pltpu.VMEM((ATTN_BLOCK, D_KV), F32), pltpu.VMEM((ATTN_BLOCK, D_KV), F32)],
        compiler_params=_params(("arbitrary",)),
    )(qkv, qkv, qkv, qkv, qkv, o, do, lse, sinks)


_GROUPS = ((0, D_ATTN), (D_ATTN, D_ATTN + D_S5), (D_ATTN + D_S5, D_MODEL))


def _rms_fwd(ya, ys, yl, g):
    t = ya.shape[0]
    tm = _row_tile(t, TM)

    def kern(a_ref, s_ref, l_ref, g_ref, o_ref):
        for (lo, hi), ref in zip(_GROUPS, (a_ref, s_ref, l_ref)):
            y = ref[...]
            n = y * lax.rsqrt(jnp.mean(y * y, axis=-1, keepdims=True) + RMS_EPS)
            o_ref[:, lo:hi] = (n * g_ref[:, lo:hi]).astype(BF16)

    row = lambda w: pl.BlockSpec((tm, w), lambda i: (i, 0))
    return pl.pallas_call(
        kern, name="rms_fwd", grid=(t // tm,),
        in_specs=[row(D_ATTN), row(D_S5), row(D_LRU), pl.BlockSpec((1, D_MODEL), lambda i: (0, 0))],
        out_specs=row(D_MODEL), out_shape=jax.ShapeDtypeStruct((t, D_MODEL), BF16),
        compiler_params=_params(("parallel",)),
    )(ya, ys, yl, g)


def _rms_bwd(dmix, ya, ys, yl, g):
    t = ya.shape[0]
    tm = _row_tile(t, TM)

    def kern(d_ref, a_ref, s_ref, l_ref, g_ref, da_ref, ds_ref, dl_ref, dg_ref):
        @pl.when(pl.program_id(0) == 0)
        def _():
            dg_ref[...] = jnp.zeros_like(dg_ref)

        for (lo, hi), ref, out in zip(_GROUPS, (a_ref, s_ref, l_ref), (da_ref, ds_ref, dl_ref)):
            y = ref[...]
            rstd = lax.rsqrt(jnp.mean(y * y, axis=-1, keepdims=True) + RMS_EPS)
            n = y * rstd
            dm = d_ref[:, lo:hi]
            dg_ref[:, lo:hi] += jnp.sum(dm * n, axis=0, keepdims=True)
            dn = dm * g_ref[:, lo:hi]
            out[...] = rstd * (dn - n * jnp.mean(dn * n, axis=-1, keepdims=True))

    row = lambda w: pl.BlockSpec((tm, w), lambda i: (i, 0))
    vec = pl.BlockSpec((1, D_MODEL), lambda i: (0, 0))
    return pl.pallas_call(
        kern, name="rms_bwd", grid=(t // tm,),
        in_specs=[row(D_MODEL), row(D_ATTN), row(D_S5), row(D_LRU), vec],
        out_specs=[row(D_ATTN), row(D_S5), row(D_LRU), vec],
        out_shape=[jax.ShapeDtypeStruct((t, D_ATTN), F32), jax.ShapeDtypeStruct((t, D_S5), F32),
                   jax.ShapeDtypeStruct((t, D_LRU), F32), jax.ShapeDtypeStruct((1, D_MODEL), F32)],
        compiler_params=_params(("arbitrary",)),
    )(dmix, ya, ys, yl, g)


def _ln_fwd(xres, f, g, b):
    t = xres.shape[0]
    tm = _row_tile(t, TM)

    def kern(x_ref, f_ref, g_ref, b_ref, y_ref, h_ref, r_ref):
        r = ALPHA * x_ref[...] + f_ref[...]
        mu = jnp.mean(r, axis=-1, keepdims=True)
        xc = r - mu
        rstd = lax.rsqrt(jnp.mean(xc * xc, axis=-1, keepdims=True) + LN_EPS)
        xhat = xc * rstd
        h_ref[...] = xhat
        r_ref[...] = rstd
        y_ref[...] = xhat * g_ref[...] + b_ref[...]

    row = pl.BlockSpec((tm, D_MODEL), lambda i: (i, 0))
    vec = pl.BlockSpec((1, D_MODEL), lambda i: (0, 0))
    return pl.pallas_call(
        kern, name="ln_fwd", grid=(t // tm,),
        in_specs=[row, row, vec, vec],
        out_specs=[row, row, pl.BlockSpec((tm, 1), lambda i: (i, 0))],
        out_shape=[jax.ShapeDtypeStruct((t, D_MODEL), F32), jax.ShapeDtypeStruct((t, D_MODEL), F32),
                   jax.ShapeDtypeStruct((t, 1), F32)],
        compiler_params=_params(("parallel",)),
    )(xres, f, g, b)


def _ln_bwd(dy, xhat, rstd, g):
    t = dy.shape[0]
    tm = _row_tile(t, TM)

    def kern(d_ref, h_ref, r_ref, g_ref, dr_ref, dg_ref, db_ref, sr_ref):
        @pl.when(pl.program_id(0) == 0)
        def _():
            dg_ref[...] = jnp.zeros_like(dg_ref)
            db_ref[...] = jnp.zeros_like(db_ref)
            sr_ref[...] = jnp.zeros_like(sr_ref)

        d = d_ref[...]
        xhat = h_ref[...]
        dg_ref[...] += jnp.sum(d * xhat, axis=0, keepdims=True)
        db_ref[...] += jnp.sum(d, axis=0, keepdims=True)
        dh = d * g_ref[...]
        dr = r_ref[...] * (dh - jnp.mean(dh, axis=-1, keepdims=True)
                           - xhat * jnp.mean(dh * xhat, axis=-1, keepdims=True))
        dr_ref[...] = dr
        sr_ref[...] += jnp.sum(dr, axis=0, keepdims=True)

    row = pl.BlockSpec((tm, D_MODEL), lambda i: (i, 0))
    vec = pl.BlockSpec((1, D_MODEL), lambda i: (0, 0))
    vshape = jax.ShapeDtypeStruct((1, D_MODEL), F32)
    return pl.pallas_call(
        kern, name="ln_bwd", grid=(t // tm,),
        in_specs=[row, row, pl.BlockSpec((tm, 1), lambda i: (i, 0)), vec],
        out_specs=[row, vec, vec, vec],
        out_shape=[jax.ShapeDtypeStruct((t, D_MODEL), F32), vshape, vshape, vshape],
        compiler_params=_params(("arbitrary",)),
    )(dy, xhat, rstd, g)


def _loss_head(y, target):
    t = y.shape[0]
    tm = _row_tile(t, TM)

    def kern(y_ref, t_ref, l_ref, d_ref):
        @pl.when(pl.program_id(0) == 0)
        def _():
            l_ref[...] = jnp.zeros_like(l_ref)

        err = y_ref[...] - t_ref[...]
        d_ref[...] = err * (1.0 / D_MODEL)
        part = jnp.sum(jnp.sum(err * err, axis=-1, keepdims=True), axis=0, keepdims=True)
        l_ref[...] += jnp.broadcast_to(part * (0.5 / D_MODEL), l_ref.shape)

    row = pl.BlockSpec((tm, D_MODEL), lambda i: (i, 0))
    return pl.pallas_call(
        kern, name="loss_head", grid=(t // tm,),
        in_specs=[row, row], out_specs=[pl.BlockSpec((1, 128), lambda i: (0, 0)), row],
        out_shape=[jax.ShapeDtypeStruct((1, 128), F32), jax.ShapeDtypeStruct((t, D_MODEL), F32)],
        compiler_params=_params(("arbitrary",)),
    )(y, target)


HALO = 8


def _ffn_mid_specs(t, tm):
    main = pl.BlockSpec((None, tm, FF_SH), lambda j, i: (j, i, 0))
    prev = pl.BlockSpec((None, HALO, FF_SH), lambda j, i: (j, jnp.maximum(i * (tm // HALO) - 1, 0), 0))
    cw = pl.BlockSpec((None, FFN_CONV, FF_SH), lambda j, i: (j, 0, 0))
    cb = pl.BlockSpec((None, 1, FF_SH), lambda j, i: (j, 0, 0))
    return main, prev, cw, cb


def _ffn_conv(ext, g_ref, p_ref, w_ref, b_ref, tm):
    i = pl.program_id(1)
    ext[0:HALO, :] = jnp.where(i > 0, p_ref[...], 0.0)
    ext[HALO:, :] = g_ref[...]
    taps = [ext[pl.ds(HALO - (FFN_CONV - 1) + k, tm), :] for k in range(FFN_CONV)]
    gc = b_ref[...] + sum(w_ref[k:k + 1, :] * taps[k] for k in range(FFN_CONV))
    return gc, taps


def _ffn_mid_fwd(gpre, up, cw, cb):
    t = gpre.shape[1]
    tm = _row_tile(t, TM)

    def kern(g_ref, p_ref, u_ref, w_ref, b_ref, o_ref, ext):
        gc, _ = _ffn_conv(ext, g_ref, p_ref, w_ref, b_ref, tm)
        o_ref[...] = (gc * _sigmoid(gc) * u_ref[...]).astype(BF16)

    main, prev, cws, cbs = _ffn_mid_specs(t, tm)
    return pl.pallas_call(
        kern, name="ffn_mid_fwd", grid=(N_CHIPS, t // tm),
        in_specs=[main, prev, main, cws, cbs], out_specs=main,
        out_shape=jax.ShapeDtypeStruct((N_CHIPS, t, FF_SH), BF16),
        scratch_shapes=[pltpu.VMEM((tm + HALO, FF_SH), F32)],
        compiler_params=_params(("parallel", "parallel")),
    )(gpre, gpre, up, cw, cb)


def _ffn_mid_bwd(gpre, up, dhmid, cw, cb):
    t = gpre.shape[1]
    tm = _row_tile(t, TM)

    def kern(g_ref, p_ref, u_ref, d_ref, w_ref, b_ref, h_ref, du_ref, dg_ref, dw_ref, db_ref, ext):
        @pl.when(pl.program_id(1) == 0)
        def _():
            dw_ref[...] = jnp.zeros_like(dw_ref)
            db_ref[...] = jnp.zeros_like(db_ref)

        gc, taps = _ffn_conv(ext, g_ref, p_ref, w_ref, b_ref, tm)
        sg = _sigmoid(gc)
        s = gc * sg
        u = u_ref[...]
        d = d_ref[...]
        h_ref[...] = (s * u).astype(BF16)
        du_ref[...] = (d * s).astype(BF16)
        dgc = d * u * (sg * (1.0 + gc * (1.0 - sg)))
        dg_ref[...] = dgc
        db_ref[...] += jnp.sum(dgc, axis=0, keepdims=True)
        for k in range(FFN_CONV):
            dw_ref[k:k + 1, :] += jnp.sum(dgc * taps[k], axis=0, keepdims=True)

    main, prev, cws, cbs = _ffn_mid_specs(t, tm)
    big = lambda dt: jax.ShapeDtypeStruct((N_CHIPS, t, FF_SH), dt)
    return pl.pallas_call(
        kern, name="ffn_mid_bwd", grid=(N_CHIPS, t // tm),
        in_specs=[main, prev, main, main, cws, cbs], out_specs=[main, main, main, cws, cbs],
        out_shape=[big(BF16), big(BF16), big(F32), jax.ShapeDtypeStruct((N_CHIPS, FFN_CONV, FF_SH), F32),
                   jax.ShapeDtypeStruct((N_CHIPS, 1, FF_SH), F32)],
        scratch_shapes=[pltpu.VMEM((tm + HALO, FF_SH), F32)],
        compiler_params=_params(("parallel", "arbitrary")),
    )(gpre, gpre, up, dhmid, cw, cb)


def _ffn_conv_t(dgc, cw):
    t = dgc.shape[1]
    tm = _row_tile(t, TM)
    nt = t // tm

    def kern(d_ref, n_ref, w_ref, o_ref, ext):
        i = pl.program_id(1)
        ext[0:tm, :] = d_ref[...]
        ext[tm:, :] = jnp.where(i < nt - 1, n_ref[...], 0.0)
        acc = sum(w_ref[k:k + 1, :] * ext[pl.ds(FFN_CONV - 1 - k, tm), :] for k in range(FFN_CONV))
        o_ref[...] = acc.astype(BF16)

    main, _, cws, _ = _ffn_mid_specs(t, tm)
    nxt = pl.BlockSpec((None, HALO, FF_SH),
                       lambda j, i: (j, jnp.minimum((i + 1) * (tm // HALO), t // HALO - 1), 0))
    return pl.pallas_call(
        kern, name="ffn_conv_t", grid=(N_CHIPS, nt),
        in_specs=[main, nxt, cws], out_specs=main,
        out_shape=jax.ShapeDtypeStruct((N_CHIPS, t, FF_SH), BF16),
        scratch_shapes=[pltpu.VMEM((tm + HALO, FF_SH), F32)],
        compiler_params=_params(("parallel", "parallel")),
    )(dgc, dgc, cw)


def _s5_coefs(ar, ai, reverse):
    if reverse:
        ai = -ai
    pw = [(ar, ai)]
    for _ in range(SUBLANES - 1):
        pr, pi = pw[-1]
        pw.append((pr * ar - pi * ai, pr * ai + pi * ar))
    rows = jnp.arange(SUBLANES)[:, None]
    out = []
    for s in (1, 2, 4):
        keep = (rows + s <= SUBLANES - 1) if reverse else (rows >= s)
        out += [jnp.where(keep, pw[s - 1][0][None], 0.0), jnp.where(keep, pw[s - 1][1][None], 0.0)]
    order = list(range(SUBLANES - 1, -1, -1)) if reverse else list(range(SUBLANES))
    out += [jnp.stack([pw[k][0] for k in order]), jnp.stack([pw[k][1] for k in order])]
    return jnp.stack(out).astype(F32)


def _s5_scan(buf, coef_ref, carry, tm, reverse):
    n8 = tm // SUBLANES

    def body(it, c):
        cre, cim = c
        blk = (n8 - 1 - it) if reverse else it
        r0 = pl.multiple_of(blk * SUBLANES, SUBLANES)
        xre = buf[pl.ds(r0, SUBLANES), 0:N_STATE]
        xim = buf[pl.ds(r0, SUBLANES), N_STATE:]
        for idx, s in enumerate((1, 2, 4)):
            sh = (SUBLANES - s) if reverse else s
            sre = pltpu.roll(xre, sh, 0)
            sim = pltpu.roll(xim, sh, 0)
            are = coef_ref[2 * idx]
            aim = coef_ref[2 * idx + 1]
            xre, xim = xre + are * sre - aim * sim, xim + are * sim + aim * sre
        pre = coef_ref[6]
        pim = coef_ref[7]
        hre = xre + pre * cre - pim * cim
        him = xim + pre * cim + pim * cre
        buf[pl.ds(r0, SUBLANES), 0:N_STATE] = hre
        buf[pl.ds(r0, SUBLANES), N_STATE:] = him
        row = 0 if reverse else SUBLANES - 1
        return (jnp.broadcast_to(hre[row:row + 1], (SUBLANES, N_STATE)),
                jnp.broadcast_to(him[row:row + 1], (SUBLANES, N_STATE)))

    cre, cim = lax.fori_loop(0, n8, body, (carry[:, 0:N_STATE], carry[:, N_STATE:]))
    carry[:, 0:N_STATE] = cre
    carry[:, N_STATE:] = cim


def _real_scan(abuf, bbuf, carry, tm, reverse):
    n8 = tm // SUBLANES
    width = bbuf.shape[1]

    def body(it, c):
        blk = (n8 - 1 - it) if reverse else it
        r0 = pl.multiple_of(blk * SUBLANES, SUBLANES)
        a = abuf[pl.ds(r0, SUBLANES), :]
        b = bbuf[pl.ds(r0, SUBLANES), :]
        rows = lax.broadcasted_iota(jnp.int32, (SUBLANES, width), 0)
        for s in (1, 2, 4):
            sh = (SUBLANES - s) if reverse else s
            keep = (rows + s <= SUBLANES - 1) if reverse else (rows >= s)
            sa = pltpu.roll(a, sh, 0)
            sb = pltpu.roll(b, sh, 0)
            b = b + a * jnp.where(keep, sb, 0.0)
            a = a * jnp.where(keep, sa, 1.0)
        h = b + a * c
        bbuf[pl.ds(r0, SUBLANES), :] = h
        row = 0 if reverse else SUBLANES - 1
        return jnp.broadcast_to(h[row:row + 1], (SUBLANES, width))

    carry[...] = lax.fori_loop(0, n8, body, carry[...])


def _dot(a, b, dims):
    return lax.dot_general(a, b, (dims, ((), ())), preferred_element_type=F32)


TS5 = 256
HALO16 = 16


def _s5_fwd(proj, bmat, coef, cmat, dvec, gw, gb):
    t = proj.shape[0]
    tm = _row_tile(t, TS5)

    def kern(u_ref, b_ref, coef_ref, c_ref, d_ref, gw_ref, gb_ref, h_ref, y_ref, hbuf, carry):
        @pl.when(pl.program_id(0) == 0)
        def _():
            carry[...] = jnp.zeros_like(carry)

        u = u_ref[...]
        hbuf[...] = _dot(u.astype(BF16), b_ref[...], NN)
        _s5_scan(hbuf, coef_ref, carry, tm, False)
        hb = hbuf[...].astype(BF16)
        h_ref[...] = hb
        y = _dot(hb, c_ref[...], NN) + d_ref[...] * u
        ys = _gelu(y)
        z = _dot(ys.astype(BF16), gw_ref[...], NN) + gb_ref[...]
        y_ref[...] = ys * _sigmoid(z)

    full = lambda shp: pl.BlockSpec(shp, lambda i: (0,) * len(shp))
    return pl.pallas_call(
        kern, name="s5_fwd", grid=(t // tm,),
        in_specs=[pl.BlockSpec((tm, D_S5), lambda i: (i, 3)), full((D_S5, 2 * N_STATE)),
                  full((8, SUBLANES, N_STATE)), full((2 * N_STATE, D_S5)), full((1, D_S5)),
                  full((D_S5, D_S5)), full((1, D_S5))],
        out_specs=[pl.BlockSpec((tm, 2 * N_STATE), lambda i: (i, 0)), pl.BlockSpec((tm, D_S5), lambda i: (i, 0))],
        out_shape=[jax.ShapeDtypeStruct((t, 2 * N_STATE), BF16), jax.ShapeDtypeStruct((t, D_S5), F32)],
        scratch_shapes=[pltpu.VMEM((tm, 2 * N_STATE), F32), pltpu.VMEM((SUBLANES, 2 * N_STATE), F32)],
        compiler_params=_params(("arbitrary",)),
    )(proj, bmat, coef, cmat, dvec, gw, gb)


def _s5_bwd(proj, h, dout, bmat, coef_b, cmat, dvec, gw, gb):
    t = proj.shape[0]
    tm = _row_tile(t, TS5)
    nt = t // tm
    rb = lambda i: nt - 1 - i

    def kern(u_ref, h_ref, hp_ref, d_ref, b_ref, coef_ref, c_ref, dv_ref, gw_ref, gb_ref,
             du_ref, dc_ref, db_ref, da_ref, dd_ref, dgw_ref, dgb_ref, gbuf, hext, carry):
        i = pl.program_id(0)

        @pl.when(i == 0)
        def _():
            carry[...] = jnp.zeros_like(carry)
            for r in (dc_ref, db_ref, da_ref, dd_ref, dgw_ref, dgb_ref):
                r[...] = jnp.zeros_like(r)

        u = u_ref[...]
        hb = h_ref[...]
        y = _dot(hb, c_ref[...], NN) + dv_ref[...] * u
        ys = _gelu(y)
        ysb = ys.astype(BF16)
        sg = _sigmoid(_dot(ysb, gw_ref[...], NN) + gb_ref[...])
        d_o = d_ref[...]
        dz = d_o * ys * sg * (1.0 - sg)
        dzb = dz.astype(BF16)
        dys = d_o * sg + _dot(dzb, gw_ref[...], NT)
        dgw_ref[...] += _dot(ysb, dzb, TN)
        dgb_ref[...] += jnp.sum(dz, axis=0, keepdims=True)
        dy = dys * _gelu_grad(y)
        dd_ref[...] += jnp.sum(dy * u, axis=0, keepdims=True)
        dyb = dy.astype(BF16)
        dc_ref[...] += _dot(hb, dyb, TN)
        gbuf[...] = _dot(dyb, c_ref[...], NT)
        _s5_scan(gbuf, coef_ref, carry, tm, True)
        g = gbuf[...]
        first = jnp.where(i < nt - 1, hp_ref[HALO16 - 1:HALO16, :].astype(F32), 0.0)
        hext[SUBLANES - 1:SUBLANES, :] = first
        hext[SUBLANES:, :] = hb.astype(F32)
        hprev = hext[pl.ds(SUBLANES - 1, tm), :]
        gre, gim = g[:, 0:N_STATE], g[:, N_STATE:]
        pre, pim = hprev[:, 0:N_STATE], hprev[:, N_STATE:]
        da_ref[0:1, :] += jnp.sum(gre * pre + gim * pim, axis=0, keepdims=True)
        da_ref[1:2, :] += jnp.sum(gim * pre - gre * pim, axis=0, keepdims=True)
        gb16 = g.astype(BF16)
        db_ref[...] += _dot(u.astype(BF16), gb16, TN)
        du_ref[...] = dy * dv_ref[...] + _dot(gb16, b_ref[...], NT)

    full = lambda shp: pl.BlockSpec(shp, lambda i: (0,) * len(shp))
    shape = lambda shp: jax.ShapeDtypeStruct(shp, F32)
    return pl.pallas_call(
        kern, name="s5_bwd", grid=(nt,),
        in_specs=[pl.BlockSpec((tm, D_S5), lambda i: (rb(i), 3)),
                  pl.BlockSpec((tm, 2 * N_STATE), lambda i: (rb(i), 0)),
                  pl.BlockSpec((HALO16, 2 * N_STATE), lambda i: (jnp.maximum(rb(i) * (tm // HALO16) - 1, 0), 0)),
                  pl.BlockSpec((tm, D_S5), lambda i: (rb(i), 0)),
                  full((D_S5, 2 * N_STATE)), full((8, SUBLANES, N_STATE)), full((2 * N_STATE, D_S5)),
                  full((1, D_S5)), full((D_S5, D_S5)), full((1, D_S5))],
        out_specs=[pl.BlockSpec((tm, D_S5), lambda i: (rb(i), 0)), full((2 * N_STATE, D_S5)),
                   full((D_S5, 2 * N_STATE)), full((2, N_STATE)), full((1, D_S5)), full((D_S5, D_S5)),
                   full((1, D_S5))],
        out_shape=[shape((t, D_S5)), shape((2 * N_STATE, D_S5)), shape((D_S5, 2 * N_STATE)),
                   shape((2, N_STATE)), shape((1, D_S5)), shape((D_S5, D_S5)), shape((1, D_S5))],
        scratch_shapes=[pltpu.VMEM((tm, 2 * N_STATE), F32), pltpu.VMEM((tm + SUBLANES, 2 * N_STATE), F32),
                        pltpu.VMEM((SUBLANES, 2 * N_STATE), F32)],
        compiler_params=_params(("arbitrary",)),
    )(proj, h, h, dout, bmat, coef_b, cmat, dvec, gw, gb)


def _lru_gates(ext, x_ref, p_ref, cw_ref, cb_ref, wx_ref, bx_ref, wa_ref, ba_ref, ap_ref, first_tile, row0, tm):
    ext[0:HALO, :] = jnp.where(first_tile, 0.0, p_ref[...])
    ext[HALO:, :] = x_ref[...]
    taps = [ext[pl.ds(HALO - (LRU_CONV - 1) + k, tm), :] for k in range(LRU_CONV)]
    xc = cb_ref[...] + sum(cw_ref[k:k + 1, :] * taps[k] for k in range(LRU_CONV))
    xcb = xc.astype(BF16)
    gx = _sigmoid(_dot(xcb, wx_ref[...], NN) + bx_ref[...])
    ga = _sigmoid(_dot(xcb, wa_ref[...], NN) + ba_ref[...])
    z = -ap_ref[...]
    sp = jnp.maximum(z, 0.0) + jnp.log(1.0 + jnp.exp(-jnp.abs(z)))
    log_a = -LRU_C * ga * sp
    a = jnp.exp(log_a)
    tok = row0 + lax.broadcasted_iota(jnp.int32, a.shape, 0)
    is0 = tok == 0
    mult = jnp.where(is0, 1.0, jnp.sqrt(1.0 - jnp.exp(2.0 * log_a)))
    return taps, xc, xcb, gx, ga, sp, a, mult, is0


def _lru_specs(tm, blk_of):
    col = lambda cidx: pl.BlockSpec((tm, D_LRU), lambda i: (blk_of(i), cidx))
    prev = lambda cidx: pl.BlockSpec((HALO, D_LRU), lambda i: (jnp.maximum(blk_of(i) * (tm // HALO) - 1, 0), cidx))
    full = lambda shp: pl.BlockSpec(shp, lambda i: (0,) * len(shp))
    wts = [full((LRU_CONV, D_LRU)), full((1, D_LRU)), full((D_LRU, D_LRU)), full((1, D_LRU)),
           full((D_LRU, D_LRU)), full((1, D_LRU)), full((1, D_LRU))]
    return col, prev, full, wts


def _lru_fwd(proj, cw, cb, wx, bx, wa, ba, ap):
    t = proj.shape[0]
    tm = _row_tile(t, TM)

    def kern(x_ref, p_ref, g_ref, cw_ref, cb_ref, wx_ref, bx_ref, wa_ref, ba_ref, ap_ref,
             y_ref, h_ref, ext, abuf, carry):
        i = pl.program_id(0)

        @pl.when(i == 0)
        def _():
            carry[...] = jnp.zeros_like(carry)

        _, xc, _, gx, _, _, a, mult, _ = _lru_gates(ext, x_ref, p_ref, cw_ref, cb_ref, wx_ref, bx_ref, wa_ref,
                                                    ba_ref, ap_ref, i == 0, i * tm, tm)
        abuf[...] = a
        h_ref[...] = mult * gx * xc
        _real_scan(abuf, h_ref, carry, tm, False)
        y_ref[...] = h_ref[...] * _gelu(g_ref[...])

    col, prev, full, wts = _lru_specs(tm, lambda i: i)
    out = pl.BlockSpec((tm, D_LRU), lambda i: (i, 0))
    return pl.pallas_call(
        kern, name="lru_fwd", grid=(t // tm,),
        in_specs=[col(4), prev(4), col(5)] + wts, out_specs=[out, out],
        out_shape=[jax.ShapeDtypeStruct((t, D_LRU), F32), jax.ShapeDtypeStruct((t, D_LRU), F32)],
        scratch_shapes=[pltpu.VMEM((tm + HALO, D_LRU), F32), pltpu.VMEM((tm, D_LRU), F32),
                        pltpu.VMEM((SUBLANES, D_LRU), F32)],
        compiler_params=_params(("arbitrary",)),
    )(proj, proj, proj, cw, cb, wx, bx, wa, ba, ap)


def _lru_bwd(proj, h, dout, cw, cb, wx, bx, wa, ba, ap):
    t = proj.shape[0]
    tm = _row_tile(t, TM)
    nt = t // tm
    rb = lambda i: nt - 1 - i

    def kern(x_ref, p_ref, g_ref, h_ref, hp_ref, d_ref, cw_ref, cb_ref, wx_ref, bx_ref, wa_ref, ba_ref, ap_ref,
             dxc_ref, dg_ref, dcw_ref, dcb_ref, dwx_ref, dbx_ref, dwa_ref, dba_ref, dap_ref,
             ext, aext, abuf, gbuf, carry, acarry):
        i = pl.program_id(0)
        blk = nt - 1 - i

        @pl.when(i == 0)
        def _():
            carry[...] = jnp.zeros_like(carry)
            acarry[...] = jnp.zeros_like(acarry)
            for r in (dcw_ref, dcb_ref, dwx_ref, dbx_ref, dwa_ref, dba_ref, dap_ref):
                r[...] = jnp.zeros_like(r)

        taps, xc, xcb, gx, ga, sp, a, mult, is0 = _lru_gates(
            ext, x_ref, p_ref, cw_ref, cb_ref, wx_ref, bx_ref, wa_ref, ba_ref, ap_ref, blk == 0, blk * tm, tm)
        gate = g_ref[...]
        d_o = d_ref[...]
        hcur = h_ref[...]
        dg_ref[...] = d_o * hcur * _gelu_grad(gate)
        aext[0:tm, :] = a
        aext[tm:, :] = acarry[...]
        abuf[...] = aext[pl.ds(1, tm), :]
        gbuf[...] = d_o * _gelu(gate)
        _real_scan(abuf, gbuf, carry, tm, True)
        acarry[...] = jnp.broadcast_to(a[0:1], acarry.shape)
        g = gbuf[...]
        ext[0:HALO, :] = jnp.where(blk == 0, 0.0, hp_ref[...])
        ext[HALO:, :] = hcur
        hprev = ext[pl.ds(HALO - 1, tm), :]
        dmult = jnp.where(is0, 0.0, g * gx * xc)
        dgx = g * mult * xc
        dxc = g * mult * gx
        dlog_a = g * hprev * a - dmult * (a * a) / mult
        dga = dlog_a * (-LRU_C * sp)
        dsp = jnp.sum(dlog_a * (-LRU_C * ga), axis=0, keepdims=True)
        dap_ref[...] += dsp * (-_sigmoid(-ap_ref[...]))
        dpa = (dga * ga * (1.0 - ga))
        dpx = (dgx * gx * (1.0 - gx))
        dpab, dpxb = dpa.astype(BF16), dpx.astype(BF16)
        dwx_ref[...] += _dot(xcb, dpxb, TN)
        dwa_ref[...] += _dot(xcb, dpab, TN)
        dbx_ref[...] += jnp.sum(dpx, axis=0, keepdims=True)
        dba_ref[...] += jnp.sum(dpa, axis=0, keepdims=True)
        dxc = dxc + _dot(dpxb, wx_ref[...], NT) + _dot(dpab, wa_ref[...], NT)
        dxc_ref[...] = dxc
        dcb_ref[...] += jnp.sum(dxc, axis=0, keepdims=True)
        for k in range(LRU_CONV):
            dcw_ref[k:k + 1, :] += jnp.sum(dxc * taps[k], axis=0, keepdims=True)

    col, prev, full, wts = _lru_specs(tm, rb)
    row = pl.BlockSpec((tm, D_LRU), lambda i: (rb(i), 0))
    hprev_spec = pl.BlockSpec((HALO, D_LRU), lambda i: (jnp.maximum(rb(i) * (tm // HALO) - 1, 0), 0))
    shape = lambda shp: jax.ShapeDtypeStruct(shp, F32)
    vec = (1, D_LRU)
    sq = (D_LRU, D_LRU)
    return pl.pallas_call(
        kern, name="lru_bwd", grid=(nt,),
        in_specs=[col(4), prev(4), col(5), row, hprev_spec, row] + wts,
        out_specs=[row, row, full((LRU_CONV, D_LRU)), full(vec), full(sq), full(vec), full(sq), full(vec), full(vec)],
        out_shape=[shape((t, D_LRU)), shape((t, D_LRU)), shape((LRU_CONV, D_LRU)), shape(vec), shape(sq),
                   shape(vec), shape(sq), shape(vec), shape(vec)],
        scratch_shapes=[pltpu.VMEM((tm + HALO, D_LRU), F32), pltpu.VMEM((tm + HALO, D_LRU), F32),
                        pltpu.VMEM((tm, D_LRU), F32), pltpu.VMEM((tm, D_LRU), F32),
                        pltpu.VMEM((SUBLANES, D_LRU), F32), pltpu.VMEM((SUBLANES, D_LRU), F32)],
        compiler_params=_params(("arbitrary",)),
    )(proj, proj, proj, h, h, dout, cw, cb, wx, bx, wa, ba, ap)


def _assemble_dproj(dq, dk, dv, du, dxc, dgate, cos, sin_s, cw):
    t = dq.shape[0]
    tm = _row_tile(t, TM)
    nt = t // tm

    def kern(dq_ref, dk_ref, dv_ref, du_ref, dx_ref, dn_ref, dg_ref, c_ref, s_ref, cw_ref, o_ref, b_ref, ext):
        i = pl.program_id(0)

        @pl.when(i == 0)
        def _():
            b_ref[...] = jnp.zeros_like(b_ref)

        def put(lo, val):
            hi = lo + val.shape[1]
            o_ref[:, lo:hi] = val.astype(BF16)
            b_ref[:, lo:hi] += jnp.sum(val, axis=0, keepdims=True)

        c = c_ref[...]
        s = s_ref[...]
        for ch in range(4):
            x = dq_ref[:, ch * 128:(ch + 1) * 128] * (HEAD_DIM ** -0.5)
            put(ch * 128, x * c - _rope_swap(x) * s)
        x = dk_ref[...]
        put(512, x * c - _rope_swap(x) * s)
        put(640, dv_ref[...])
        put(768, du_ref[...])
        ext[0:tm, :] = dx_ref[...]
        ext[tm:, :] = jnp.where(i < nt - 1, dn_ref[...], 0.0)
        put(1024, sum(cw_ref[k:k + 1, :] * ext[pl.ds(LRU_CONV - 1 - k, tm), :] for k in range(LRU_CONV)))
        put(1280, dg_ref[...])

    row = lambda w: pl.BlockSpec((tm, w), lambda i: (i, 0))
    nxt = pl.BlockSpec((HALO, D_LRU), lambda i: (jnp.minimum((i + 1) * (tm // HALO), t // HALO - 1), 0))
    return pl.pallas_call(
        kern, name="assemble_dproj", grid=(nt,),
        in_specs=[row(512), row(128), row(128), row(256), row(256), nxt, row(256), row(128), row(128),
                  pl.BlockSpec((LRU_CONV, D_LRU), lambda i: (0, 0))],
        out_specs=[row(D_IN), pl.BlockSpec((1, D_IN), lambda i: (0, 0))],
        out_shape=[jax.ShapeDtypeStruct((t, D_IN), BF16), jax.ShapeDtypeStruct((1, D_IN), F32)],
        scratch_shapes=[pltpu.VMEM((tm + HALO, D_LRU), F32)],
        compiler_params=_params(("arbitrary",)),
    )(dq, dk, dv, du, dxc, dxc, dgate, cos, sin_s, cw)


def _blockdiag_s5(bbar_re, bbar_im, c_re, c_im):
    eye = jnp.eye(S5_GROUPS, dtype=F32)
    b_of = lambda m: jnp.einsum('gpc,gh->gchp', m, eye).reshape(D_S5, N_STATE)
    c_of = lambda m: jnp.einsum('gcp,gh->gphc', m, eye).reshape(N_STATE, D_S5)
    bmat = jnp.concatenate([b_of(bbar_re), b_of(bbar_im)], axis=1)
    cmat = jnp.concatenate([c_of(c_re), -c_of(c_im)], axis=0)
    return bmat, cmat


def _s5_prepare(a_re, a_im, b_re, b_im, c_re, c_im, log_dt):
    lam_re = jnp.minimum(a_re, -1e-4)
    lam_im = a_im
    dt = jnp.exp(log_dt)[:, None]
    decay = jnp.exp(dt * lam_re)
    ang = dt * lam_im
    abar_re = decay * jnp.cos(ang)
    abar_im = decay * jnp.sin(ang)
    den = jnp.square(lam_re) + jnp.square(lam_im)
    nr = abar_re - 1.0
    ni = abar_im
    coef_re = (nr * lam_re + ni * lam_im) / den
    coef_im = (ni * lam_re - nr * lam_im) / den
    bbar_re = coef_re[..., None] * b_re - coef_im[..., None] * b_im
    bbar_im = coef_re[..., None] * b_im + coef_im[..., None] * b_re
    bmat, cmat = _blockdiag_s5(bbar_re, bbar_im, c_re, c_im)
    return abar_re.reshape(N_STATE), abar_im.reshape(N_STATE), bmat, cmat


def _blockdiag_lru(w):
    eye = jnp.eye(LRU_HEADS, dtype=F32)
    return jnp.einsum('hij,hk->hikj', w, eye).reshape(D_LRU, D_LRU)


def _rope_tables(t):
    inv_freq = ROPE_THETA ** (-jnp.arange(0, HEAD_DIM, 2, dtype=F32) / HEAD_DIM)
    ang = jnp.arange(t, dtype=F32)[:, None] * inv_freq[None, :]
    cos, sin = jnp.cos(ang), jnp.sin(ang)
    return jnp.tile(jnp.concatenate([cos, cos], axis=1), (1, 2)), jnp.tile(jnp.concatenate([-sin, sin], axis=1), (1, 2))


def _vec(v):
    return v.reshape(1, -1)


def _layer_weights(p):
    abar_re, abar_im, bmat, cmat = _s5_prepare(p['s5_a_re'], p['s5_a_im'], p['s5_b_re'], p['s5_b_im'],
                                               p['s5_c_re'], p['s5_c_im'], p['s5_log_dt'])
    return dict(
        coef_f=_s5_coefs(abar_re, abar_im, False), coef_b=_s5_coefs(abar_re, abar_im, True),
        bmat=bmat.astype(BF16), cmat=cmat.astype(BF16),
        wx=_blockdiag_lru(p['lru_wx']).astype(BF16), wa=_blockdiag_lru(p['lru_wa']).astype(BF16),
        gw=p['s5_glu_w'].astype(BF16))


def _layer_fwd(x, p, w, cos, sin_s):
    t = x.shape[0]
    nt = t // _row_tile(t, TM)
    tm = t // nt
    proj = _matmul("in_proj", x, p['w_in'], a_blk=(tm, D_MODEL), a_map=lambda i, j, k: (i, 0),
                   b_blk=(None, D_MODEL, IN_SH), b_map=lambda i, j, k: (j, 0, 0), out_shape=(t, D_IN),
                   o_blk=(tm, IN_SH), o_map=lambda i, j: (i, j), grid=(nt, N_CHIPS, 1), dims=NN,
                   bias=_vec(p['b_in']), bias_blk=(1, IN_SH), bias_map=lambda i, j, k: (0, j))
    qkv = _qkv_post(proj, cos, sin_s)
    ya, lse = _attn_fwd(qkv, _vec(p['attn_sinks']))
    h5, ys = _s5_fwd(proj, w['bmat'], w['coef_f'], w['cmat'], _vec(p['s5_d']), w['gw'], _vec(p['s5_glu_b']))
    lru_w = (p['lru_conv_w'], _vec(p['lru_conv_b']), w['wx'], _vec(p['lru_bx']), w['wa'], _vec(p['lru_ba']),
             _vec(p['lru_a_param']))
    yl, hl = _lru_fwd(proj, *lru_w)
    mix = _rms_fwd(ya, ys, yl, _vec(p['mix_norm_g']))
    f1 = _matmul("out_proj", mix, p['w_out'], a_blk=(tm, D_MODEL), a_map=lambda i, j, k: (i, 0),
                 b_blk=(D_MODEL, D_MODEL), b_map=lambda i, j, k: (0, 0), out_shape=(t, D_MODEL),
                 o_blk=(tm, D_MODEL), o_map=lambda i, j: (i, 0), grid=(nt, 1, 1), dims=NN,
                 bias=_vec(p['b_out']), bias_blk=(1, D_MODEL), bias_map=lambda i, j, k: (0, 0))
    x1, xhat1, rstd1 = _ln_fwd(x, f1, _vec(p['ln1_g']), _vec(p['ln1_b']))
    ffn_in = lambda name, wmat: _matmul(
        name, x1, wmat, a_blk=(tm, D_MODEL), a_map=lambda i, j, k: (i, 0), b_blk=(None, D_MODEL, FF_SH),
        b_map=lambda i, j, k: (j, 0, 0), out_shape=(N_CHIPS, t, FF_SH), o_blk=(None, tm, FF_SH),
        o_map=lambda i, j: (j, i, 0), grid=(nt, N_CHIPS, 1), dims=NN)
    gpre = ffn_in("ffn_gate", p['ffn_w_gate'])
    up = ffn_in("ffn_up", p['ffn_w_up'])
    hmid = _ffn_mid_fwd(gpre, up, p['ffn_conv_w'], p['ffn_conv_b'])
    f2 = _matmul("ffn_down", hmid, p['ffn_w_down'], a_blk=(None, tm, FF_SH), a_map=lambda i, j, k: (k, i, 0),
                 b_blk=(None, FF_SH, D_MODEL), b_map=lambda i, j, k: (k, 0, 0), out_shape=(t, D_MODEL),
                 o_blk=(tm, D_MODEL), o_map=lambda i, j: (i, 0), grid=(nt, 1, N_CHIPS), dims=NN)
    x2, xhat2, rstd2 = _ln_fwd(x1, f2, _vec(p['ln2_g']), _vec(p['ln2_b']))
    saved = dict(x=x, proj=proj, qkv=qkv, ya=ya, lse=lse, h5=h5, ys=ys, yl=yl, hl=hl, mix=mix, x1=x1, xhat1=xhat1,
                 rstd1=rstd1, gpre=gpre, up=up, xhat2=xhat2, rstd2=rstd2, lru_w=lru_w)
    return x2, saved


def _layer_bwd(dx2, s, p, w, cos, sin_s):
    t = dx2.shape[0]
    nt = t // _row_tile(t, TM)
    tm = t // nt
    g = {}
    dr2, g['ln2_g'], g['ln2_b'], _ = _ln_bwd(dx2, s['xhat2'], s['rstd2'], _vec(p['ln2_g']))
    dhmid = _matmul("d_hmid", dr2, p['ffn_w_down'], a_blk=(tm, D_MODEL), a_map=lambda i, j, k: (i, 0),
                    b_blk=(None, FF_SH, D_MODEL), b_map=lambda i, j, k: (j, 0, 0), out_shape=(N_CHIPS, t, FF_SH),
                    o_blk=(None, tm, FF_SH), o_map=lambda i, j: (j, i, 0), grid=(nt, N_CHIPS, 1), dims=NT)
    hmid, dup, dgc, g['ffn_conv_w'], g['ffn_conv_b'] = _ffn_mid_bwd(s['gpre'], s['up'], dhmid, p['ffn_conv_w'],
                                                                    p['ffn_conv_b'])
    g['ffn_w_down'] = _matmul("d_w_down", hmid, dr2, a_blk=(None, tm, FF_SH), a_map=lambda i, j, k: (i, k, 0),
                              b_blk=(tm, D_MODEL), b_map=lambda i, j, k: (k, 0), out_shape=(N_CHIPS, FF_SH, D_MODEL),
                              o_blk=(None, FF_SH, D_MODEL), o_map=lambda i, j: (i, 0, 0), grid=(N_CHIPS, 1, nt), dims=TN)
    dgpre = _ffn_conv_t(dgc, p['ffn_conv_w'])
    d_ffn_w = lambda name, dact: _matmul(
        name, s['x1'], dact, a_blk=(tm, D_MODEL), a_map=lambda i, j, k: (k, 0), b_blk=(None, tm, FF_SH),
        b_map=lambda i, j, k: (j, k, 0), out_shape=(N_CHIPS, D_MODEL, FF_SH), o_blk=(None, D_MODEL, FF_SH),
        o_map=lambda i, j: (j, 0, 0), grid=(1, N_CHIPS, nt), dims=TN)
    g['ffn_w_gate'] = d_ffn_w("d_w_gate", dgpre)
    g['ffn_w_up'] = d_ffn_w("d_w_up", dup)
    d_ffn_x = lambda name, dact, wmat, add, scale: _matmul(
        name, dact, wmat, a_blk=(None, tm, FF_SH), a_map=lambda i, j, k: (k, i, 0), b_blk=(None, D_MODEL, FF_SH),
        b_map=lambda i, j, k: (k, 0, 0), out_shape=(t, D_MODEL), o_blk=(tm, D_MODEL), o_map=lambda i, j: (i, 0),
        grid=(nt, 1, N_CHIPS), dims=NT, add=add, add_scale=scale)
    dx1 = d_ffn_x("d_x1_gate", dgpre, p['ffn_w_gate'], dr2, ALPHA)
    dx1 = d_ffn_x("d_x1_up", dup, p['ffn_w_up'], dx1, 1.0)
    dr1, g['ln1_g'], g['ln1_b'], g['b_out'] = _ln_bwd(dx1, s['xhat1'], s['rstd1'], _vec(p['ln1_g']))
    g['w_out'] = _matmul("d_w_out", s['mix'], dr1, a_blk=(tm, D_MODEL), a_map=lambda i, j, k: (k, 0),
                         b_blk=(tm, D_MODEL), b_map=lambda i, j, k: (k, 0), out_shape=(D_MODEL, D_MODEL),
                         o_blk=(D_MODEL, D_MODEL), o_map=lambda i, j: (0, 0), grid=(1, 1, nt), dims=TN)
    dmix = _matmul("d_mix", dr1, p['w_out'], a_blk=(tm, D_MODEL), a_map=lambda i, j, k: (i, 0),
                   b_blk=(D_MODEL, D_MODEL), b_map=lambda i, j, k: (0, 0), out_shape=(t, D_MODEL),
                   o_blk=(tm, D_MODEL), o_map=lambda i, j: (i, 0), grid=(nt, 1, 1), dims=NT)
    dya, dys, dyl, g['mix_norm_g'] = _rms_bwd(dmix, s['ya'], s['ys'], s['yl'], _vec(p['mix_norm_g']))
    dq, dk, dv, g['attn_sinks'] = _attn_bwd(s['qkv'], s['ya'], dya, s['lse'], _vec(p['attn_sinks']))
    du, dcmat, dbmat, dabar, g['s5_d'], g['s5_glu_w'], g['s5_glu_b'] = _s5_bwd(
        s['proj'], s['h5'], dys, w['bmat'], w['coef_b'], w['cmat'], _vec(p['s5_d']), w['gw'], _vec(p['s5_glu_b']))
    (dxc, dgate, g['lru_conv_w'], g['lru_conv_b'], dwx, g['lru_bx'], dwa, g['lru_ba'],
     g['lru_a_param']) = _lru_bwd(s['proj'], s['hl'], dyl, *s['lru_w'])
    dproj, g['b_in'] = _assemble_dproj(dq, dk, dv, du, dxc, dgate, cos, sin_s, p['lru_conv_w'])
    g['w_in'] = _matmul("d_w_in", s['x'], dproj, a_blk=(tm, D_MODEL), a_map=lambda i, j, k: (k, 0),
                        b_blk=(tm, IN_SH), b_map=lambda i, j, k: (k, j), out_shape=(N_CHIPS, D_MODEL, IN_SH),
                        o_blk=(None, D_MODEL, IN_SH), o_map=lambda i, j: (j, 0, 0), grid=(1, N_CHIPS, nt), dims=TN)
    dx = _matmul("d_x", dproj, p['w_in'], a_blk=(tm, IN_SH), a_map=lambda i, j, k: (i, k),
                 b_blk=(None, D_MODEL, IN_SH), b_map=lambda i, j, k: (k, 0, 0), out_shape=(t, D_MODEL),
                 o_blk=(tm, D_MODEL), o_map=lambda i, j: (i, 0), grid=(nt, 1, N_CHIPS), dims=NT,
                 add=dr1, add_scale=ALPHA)
    s5_names = ('s5_a_re', 's5_a_im', 's5_b_re', 's5_b_im', 's5_c_re', 's5_c_im', 's5_log_dt')
    _, s5_vjp = jax.vjp(_s5_prepare, *[p[n] for n in s5_names])
    for n, val in zip(s5_names, s5_vjp((dabar[0], dabar[1], dbmat, dcmat))):
        g[n] = val
    g['lru_wx'] = jax.vjp(_blockdiag_lru, p['lru_wx'])[1](dwx)[0]
    g['lru_wa'] = jax.vjp(_blockdiag_lru, p['lru_wa'])[1](dwa)[0]
    return dx, g


ROW_TILE = 512


def _pick_rows(rows):
    for rt in range(min(rows, ROW_TILE), 0, -1):
        if rows % rt == 0 and (rt % 16 == 0 or rt == rows):
            return rt
    return rows


def _cast_bf16(a):
    a2 = a.reshape(-1, a.shape[-1])
    rows, c = a2.shape
    rt = _pick_rows(rows)

    def kern(a_ref, o_ref):
        o_ref[...] = a_ref[...].astype(BF16)

    spec = pl.BlockSpec((rt, c), lambda i: (i, 0))
    out = pl.pallas_call(kern, name="cast_bf16", grid=(rows // rt,), in_specs=[spec], out_specs=spec,
                         out_shape=jax.ShapeDtypeStruct((rows, c), BF16), compiler_params=_params(("parallel",)))(a2)
    return out.reshape(a.shape)


def _sum_parts(name, parts, shape):
    c = shape[-1]
    rows = math.prod(shape[:-1])
    rt = _pick_rows(rows)
    n = len(parts)

    def kern(*refs):
        acc = refs[0][...]
        for r in refs[1:n]:
            acc = acc + r[...]
        refs[n][...] = acc

    specs, args = [], []
    for arr, j in parts:
        if j is None:
            specs.append(pl.BlockSpec((rt, c), lambda i: (i, 0)))
            args.append(arr.reshape(rows, c))
        else:
            specs.append(pl.BlockSpec((None, rt, c), functools.partial(lambda i, jj: (jj, i, 0), jj=j)))
            args.append(arr.reshape(arr.shape[0], rows, c))
    out = pl.pallas_call(kern, name=name, grid=(rows // rt,), in_specs=specs,
                         out_specs=pl.BlockSpec((rt, c), lambda i: (i, 0)),
                         out_shape=jax.ShapeDtypeStruct((rows, c), F32), compiler_params=_params(("parallel",)))(*args)
    return out.reshape(shape)


def _adamw(name, w, g, m, v):
    shape = w.shape
    c = shape[-1]
    rows = math.prod(shape[:-1])
    rt = _pick_rows(rows)

    def kern(w_ref, g_ref, m_ref, v_ref, d_ref, nm_ref, nv_ref):
        g_ = g_ref[...]
        m_ = ADAM_B1 * m_ref[...] + (1.0 - ADAM_B1) * g_
        v_ = ADAM_B2 * v_ref[...] + (1.0 - ADAM_B2) * jnp.square(g_)
        m_hat = m_ / (1.0 - ADAM_B1 ** ADAM_STEP)
        v_hat = v_ / (1.0 - ADAM_B2 ** ADAM_STEP)
        d_ref[...] = -ADAM_LR * (m_hat / (jnp.sqrt(v_hat) + ADAM_EPS) + ADAM_WD * w_ref[...])
        nm_ref[...] = m_
        nv_ref[...] = v_

    spec = pl.BlockSpec((rt, c), lambda i: (i, 0))
    outs = pl.pallas_call(kern, name=name, grid=(rows // rt,), in_specs=[spec] * 4, out_specs=[spec] * 3,
                          out_shape=[jax.ShapeDtypeStruct((rows, c), F32)] * 3,
                          compiler_params=_params(("parallel",)))(*[a.reshape(rows, c) for a in (w, g, m, v)])
    return tuple(o.reshape(shape) for o in outs)


def _position():
    return lax.axis_index("x"), lax.axis_index("y"), lax.axis_index("c")


def _other_chips(x, y):
    return [(1 - x, y), (x, 1 - y), (1 - x, 1 - y)]


def _comm_call(name, kern, arrs, out_shapes, n_remote, n_local):
    return pl.pallas_call(
        kern, name=name, in_specs=[ANY] * len(arrs), out_specs=[ANY] * len(out_shapes), out_shape=out_shapes,
        scratch_shapes=[pltpu.SemaphoreType.DMA((n_remote,)), pltpu.SemaphoreType.DMA((n_remote,)),
                        pltpu.SemaphoreType.DMA((n_local,))],
    )(*arrs)


def _allgather_chips(arrs):
    n = len(arrs)

    def kern(*refs):
        ins, outs = refs[:n], refs[n:2 * n]
        send, recv, loc = refs[2 * n:]
        x, y, c = _position()
        me = 2 * x + y
        chips = _other_chips(x, y)
        own, sent = [], []
        for t in range(n):
            own.append(pltpu.make_async_copy(ins[t], outs[t].at[:, pl.ds(me, 1)], loc.at[t]))
            own[-1].start()
            for j, (px, py) in enumerate(chips):
                sent.append(pltpu.make_async_remote_copy(
                    src_ref=ins[t], dst_ref=outs[t].at[:, pl.ds(me, 1)], send_sem=send.at[3 * t + j],
                    recv_sem=recv.at[3 * t + j], device_id=(px, py, c), device_id_type=MESH))
                sent[-1].start()
        for t in range(n):
            for j, (px, py) in enumerate(chips):
                pltpu.make_async_remote_copy(
                    src_ref=ins[t], dst_ref=outs[t].at[:, pl.ds(2 * px + py, 1)], send_sem=send.at[3 * t + j],
                    recv_sem=recv.at[3 * t + j], device_id=(px, py, c), device_id_type=MESH).wait_recv()
        for cp in sent:
            cp.wait_send()
        for cp in own:
            cp.wait()

    outs = [jax.ShapeDtypeStruct((a.shape[0], N_CHIPS) + a.shape[2:], a.dtype) for a in arrs]
    return _comm_call("allgather_chips", kern, arrs, outs, 3 * n, n)


def _pair_exchange(arrs):
    n = len(arrs)

    def kern(*refs):
        ins, outs = refs[:n], refs[n:3 * n]
        send, recv, loc = refs[3 * n:]
        x, y, c = _position()
        own, sent = [], []
        for t in range(n):
            r2 = ins[t].shape[2] // 2
            own.append(pltpu.make_async_copy(ins[t].at[:, :, pl.ds(c * r2, r2)], outs[2 * t], loc.at[t]))
            own[-1].start()
            sent.append(pltpu.make_async_remote_copy(
                src_ref=ins[t].at[:, :, pl.ds((1 - c) * r2, r2)], dst_ref=outs[2 * t + 1], send_sem=send.at[t],
                recv_sem=recv.at[t], device_id=(x, y, 1 - c), device_id_type=MESH))
            sent[-1].start()
        for cp in sent:
            cp.wait()
        for cp in own:
            cp.wait()

    outs = []
    for a in arrs:
        half = jax.ShapeDtypeStruct(a.shape[:2] + (a.shape[2] // 2, a.shape[3]), a.dtype)
        outs += [half, half]
    return _comm_call("pair_exchange", kern, arrs, outs, n, n)


def _chip_scatter(arrs):
    n = len(arrs)

    def kern(*refs):
        ins, outs = refs[:n], refs[n:3 * n]
        send, recv, loc = refs[3 * n:]
        x, y, c = _position()
        me = 2 * x + y
        chips = _other_chips(x, y)
        own, sent = [], []
        for t in range(n):
            own.append(pltpu.make_async_copy(ins[t].at[:, pl.ds(me, 1)], outs[2 * t], loc.at[t]))
            own[-1].start()
            for j, (px, py) in enumerate(chips):
                sent.append(pltpu.make_async_remote_copy(
                    src_ref=ins[t].at[:, pl.ds(2 * px + py, 1)], dst_ref=outs[2 * t + 1].at[j],
                    send_sem=send.at[3 * t + j], recv_sem=recv.at[3 * t + j], device_id=(px, py, c),
                    device_id_type=MESH))
                sent[-1].start()
        for cp in sent:
            cp.wait()
        for cp in own:
            cp.wait()

    outs = []
    for a in arrs:
        one = (a.shape[0], 1) + a.shape[2:]
        outs += [jax.ShapeDtypeStruct(one, a.dtype), jax.ShapeDtypeStruct((3,) + one, a.dtype)]
    return _comm_call("chip_scatter", kern, arrs, outs, 3 * n, n)


def _pair_gather(arrs):
    n = len(arrs)

    def kern(*refs):
        ins, outs = refs[:n], refs[n:2 * n]
        send, recv, loc = refs[2 * n:]
        x, y, c = _position()
        own, sent = [], []
        for t in range(n):
            own.append(pltpu.make_async_copy(ins[t], outs[t].at[:, pl.ds(c, 1)], loc.at[t]))
            own[-1].start()
            sent.append(pltpu.make_async_remote_copy(
                src_ref=ins[t], dst_ref=outs[t].at[:, pl.ds(c, 1)], send_sem=send.at[t], recv_sem=recv.at[t],
                device_id=(x, y, 1 - c), device_id_type=MESH))
            sent[-1].start()
        for t in range(n):
            sent[t].wait_send()
            pltpu.make_async_remote_copy(
                src_ref=ins[t], dst_ref=outs[t].at[:, pl.ds(1 - c, 1)], send_sem=send.at[t], recv_sem=recv.at[t],
                device_id=(x, y, 1 - c), device_id_type=MESH).wait_recv()
        for cp in own:
            cp.wait()

    outs = [jax.ShapeDtypeStruct((a.shape[0], 2) + a.shape[2:], a.dtype) for a in arrs]
    return _comm_call("pair_gather", kern, arrs, outs, n, n)


_FLIPS = [(0, 0, 1), (1, 0, 0), (0, 1, 0), (1, 1, 0), (1, 0, 1), (0, 1, 1), (1, 1, 1)]


def _allgather_devices(v):
    def kern(v_ref, o_ref, send, recv, loc):
        x, y, c = _position()
        me = 4 * x + 2 * y + c
        peers = [((1 - x) if fx else x, (1 - y) if fy else y, (1 - c) if fc else c) for fx, fy, fc in _FLIPS]
        own = pltpu.make_async_copy(v_ref, o_ref.at[pl.ds(me, 1)], loc.at[0])
        own.start()
        sent = []
        for k, peer in enumerate(peers):
            sent.append(pltpu.make_async_remote_copy(
                src_ref=v_ref, dst_ref=o_ref.at[pl.ds(me, 1)], send_sem=send.at[k], recv_sem=recv.at[k],
                device_id=peer, device_id_type=MESH))
            sent[-1].start()
        for k, (px, py, pc) in enumerate(peers):
            pltpu.make_async_remote_copy(
                src_ref=v_ref, dst_ref=o_ref.at[pl.ds(4 * px + 2 * py + pc, 1)], send_sem=send.at[k],
                recv_sem=recv.at[k], device_id=(px, py, pc), device_id_type=MESH).wait_recv()
        for cp in sent:
            cp.wait_send()
        own.wait()

    out = jax.ShapeDtypeStruct((N_DEV,) + v.shape[1:], v.dtype)
    return _comm_call("allgather_devices", kern, [v], [out], len(_FLIPS), 1)[0]


WEIGHTS = ['w_in', 'b_in', 'attn_sinks', 's5_a_re', 's5_a_im', 's5_b_re', 's5_b_im', 's5_c_re', 's5_c_im', 's5_d',
           's5_log_dt', 's5_glu_w', 's5_glu_b', 'lru_conv_w', 'lru_conv_b', 'lru_wx', 'lru_bx', 'lru_wa', 'lru_ba',
           'lru_a_param', 'mix_norm_g', 'w_out', 'b_out', 'ln1_g', 'ln1_b', 'ffn_w_gate', 'ffn_w_up', 'ffn_conv_w',
           'ffn_conv_b', 'ffn_w_down', 'ln2_g', 'ln2_b']
BIG = ('w_in', 'w_out', 'ffn_w_gate', 'ffn_w_up', 'ffn_w_down')
SMALL = tuple(n for n in WEIGHTS if n not in BIG)
PACK_ROWS = ROW_TILE


def _pack(arrs):
    flat = jnp.concatenate([a.reshape(-1) for a in arrs])
    unit = 128 * PACK_ROWS
    size = -(-flat.shape[0] // unit) * unit
    return jnp.pad(flat, (0, size - flat.shape[0])).reshape(-1, 128)


def _unpack(packed, shapes):
    flat = packed.reshape(-1)
    out, pos = [], 0
    for shp in shapes:
        n = math.prod(shp)
        out.append(flat[pos:pos + n].reshape(shp))
        pos += n
    return out


def _reduce_big(grads):
    mixed = _pair_exchange(grads)
    pair = [_sum_parts("pair_sum", [(mixed[2 * t], None), (mixed[2 * t + 1], None)], mixed[2 * t].shape)
            for t in range(len(grads))]
    scat = _chip_scatter(pair)
    chip = [_sum_parts("chip_sum", [(scat[2 * t], None)] + [(scat[2 * t + 1], j) for j in range(3)],
                       scat[2 * t].shape) for t in range(len(grads))]
    both = _pair_gather(chip)
    return [b.reshape(b.shape[0], 2 * b.shape[2], b.shape[3]) for b in both]


def _step(a):
    x = a['x'][0]
    target = a['loss_target'][0]
    t = x.shape[0]
    xi, yi, _ = _position()
    chip = 2 * xi + yi
    cos, sin_s = _rope_tables(t)

    gathered = _allgather_chips([_cast_bf16(a[n])[:, None] for n in BIG]
                                + [a[n][:, None] for n in ('s5_glu_w', 'lru_conv_w', 'ffn_conv_w')])
    full = dict(zip(BIG + ('s5_glu_w', 'lru_conv_w', 'ffn_conv_w'), gathered))

    def layer_params(l):
        p = {n: a[n][l] for n in SMALL}
        p['w_in'] = full['w_in'][l]
        p['w_out'] = full['w_out'][l].reshape(D_MODEL, D_MODEL)
        p['ffn_w_gate'] = full['ffn_w_gate'][l]
        p['ffn_w_up'] = full['ffn_w_up'][l]
        p['ffn_w_down'] = full['ffn_w_down'][l]
        p['s5_glu_w'] = full['s5_glu_w'][l].reshape(D_S5, D_S5)
        p['lru_conv_w'] = full['lru_conv_w'][l].transpose(1, 0, 2).reshape(LRU_CONV, D_LRU)
        p['ffn_conv_w'] = full['ffn_conv_w'][l]
        p['ffn_conv_b'] = a['ffn_conv_b'][l].reshape(N_CHIPS, 1, FF_SH)
        return p

    params = [layer_params(l) for l in range(DEPTH)]
    derived = [_layer_weights(p) for p in params]
    saved = []
    h = x
    for l in range(DEPTH):
        h, s = _layer_fwd(h, params[l], derived[l], cos, sin_s)
        saved.append(s)
    loss_part, dh = _loss_head(h, target)
    loss = lax.psum(loss_part[0, 0], ("x", "y", "c"))
    grads = [None] * DEPTH
    for l in reversed(range(DEPTH)):
        dh, grads[l] = _layer_bwd(dh, saved[l], params[l], derived[l], cos, sin_s)
    grad_x = dh[None]

    def stacked(n):
        return jnp.stack([grads[l][n] for l in range(DEPTH)])

    big_local = [stacked(n) for n in BIG]
    big_local[1] = big_local[1].reshape(DEPTH, N_CHIPS, OUT_SH, D_MODEL)
    grad = dict(zip(BIG, _reduce_big(big_local)))
    small_local = [stacked(n) for n in SMALL]
    packed = _allgather_devices(_pack(small_local)[None])
    total = _sum_parts("device_sum", [(packed, j) for j in range(N_DEV)], packed.shape[1:])
    small_sum = dict(zip(SMALL, _unpack(total, [g.shape for g in small_local])))
    for n in SMALL:
        g = small_sum[n]
        if n == 's5_glu_w':
            g = lax.dynamic_slice_in_dim(g, chip * (D_S5 // N_CHIPS), D_S5 // N_CHIPS, axis=1)
        elif n == 'lru_conv_w':
            g = lax.dynamic_slice_in_dim(g, chip * (D_LRU // N_CHIPS), D_LRU // N_CHIPS, axis=2)
        elif n == 'ffn_conv_w':
            g = lax.dynamic_index_in_dim(g, chip, axis=1, keepdims=False)
        grad[n] = g.reshape(a[n].shape)

    delta, new_m, new_v = {}, {}, {}
    for n in BIG:
        delta[n], new_m[n], new_v[n] = _adamw("adamw_" + n, a[n], grad[n], a['m_' + n], a['v_' + n])
    shapes = [a[n].shape for n in SMALL]
    outs = _adamw("adamw_small", _pack([a[n] for n in SMALL]), _pack([grad[n] for n in SMALL]),
                  _pack([a['m_' + n] for n in SMALL]), _pack([a['v_' + n] for n in SMALL]))
    for res, o in zip((delta, new_m, new_v), outs):
        res.update(zip(SMALL, _unpack(o, shapes)))
    return (loss, grad_x, *[grad[n] for n in WEIGHTS], *[delta[n] for n in WEIGHTS],
            *[new_m[n] for n in WEIGHTS], *[new_v[n] for n in WEIGHTS])


def kernel(x, w_in, b_in, attn_sinks, s5_a_re, s5_a_im, s5_b_re, s5_b_im, s5_c_re, s5_c_im, s5_d, s5_log_dt, s5_glu_w, s5_glu_b, lru_conv_w, lru_conv_b, lru_wx, lru_bx, lru_wa, lru_ba, lru_a_param, mix_norm_g, w_out, b_out, ln1_g, ln1_b, ffn_w_gate, ffn_w_up, ffn_conv_w, ffn_conv_b, ffn_w_down, ln2_g, ln2_b, loss_target, m_w_in, m_b_in, m_attn_sinks, m_s5_a_re, m_s5_a_im, m_s5_b_re, m_s5_b_im, m_s5_c_re, m_s5_c_im, m_s5_d, m_s5_log_dt, m_s5_glu_w, m_s5_glu_b, m_lru_conv_w, m_lru_conv_b, m_lru_wx, m_lru_bx, m_lru_wa, m_lru_ba, m_lru_a_param, m_mix_norm_g, m_w_out, m_b_out, m_ln1_g, m_ln1_b, m_ffn_w_gate, m_ffn_w_up, m_ffn_conv_w, m_ffn_conv_b, m_ffn_w_down, m_ln2_g, m_ln2_b, v_w_in, v_b_in, v_attn_sinks, v_s5_a_re, v_s5_a_im, v_s5_b_re, v_s5_b_im, v_s5_c_re, v_s5_c_im, v_s5_d, v_s5_log_dt, v_s5_glu_w, v_s5_glu_b, v_lru_conv_w, v_lru_conv_b, v_lru_wx, v_lru_bx, v_lru_wa, v_lru_ba, v_lru_a_param, v_mix_norm_g, v_w_out, v_b_out, v_ln1_g, v_ln1_b, v_ffn_w_gate, v_ffn_w_up, v_ffn_conv_w, v_ffn_conv_b, v_ffn_w_down, v_ln2_g, v_ln2_b):
    return _step(dict(locals()))
```

```python
import functools
import math

import jax
import jax.numpy as jnp
from jax import lax
from jax.experimental import pallas as pl
from jax.experimental.pallas import tpu as pltpu

F32 = jnp.float32
BF16 = jnp.bfloat16
MESH = pl.DeviceIdType.MESH
ANY = pl.BlockSpec(memory_space=pl.ANY)

D_MODEL = 1024
DEPTH = 4
HEAD_DIM = 64
N_Q_HEADS = 8
N_KV_HEADS = 2
Q_PER_KV = 4
D_ATTN = 512
D_KV = 128
ATTN_BLOCK = 128
ROPE_THETA = 10000.0
D_S5 = 256
S5_GROUP = 16
S5_GROUPS = 16
S5_STATE = 64
N_STATE = S5_GROUPS * S5_STATE
D_LRU = 256
LRU_HEADS = 4
LRU_HEAD_DIM = 64
LRU_CONV = 4
LRU_C = 8.0
D_IN = 1536
D_FF = 2816
FFN_CONV = 3
N_CHIPS = 4
N_DEV = 8
IN_SH = D_IN // N_CHIPS
FF_SH = D_FF // N_CHIPS
OUT_SH = D_MODEL // N_CHIPS
ALPHA = (2 * DEPTH) ** 0.25
LN_EPS = 1e-5
RMS_EPS = 1e-6
ADAM_LR = 0.001
ADAM_B1 = 0.9
ADAM_B2 = 0.999
ADAM_EPS = 1e-08
ADAM_WD = 0.01
ADAM_STEP = 10

SUBLANES = 8
VMEM_MB = 56


def _params(sem):
    return pltpu.CompilerParams(dimension_semantics=sem, vmem_limit_bytes=VMEM_MB << 20)


def _row_tile(t, pref):
    return min(t, pref)


def _matmul(name, a, b, *, a_blk, a_map, b_blk, b_map, out_shape, o_blk, o_map, grid, dims,
            out_dtype=F32, bias=None, bias_blk=None, bias_map=None, add=None, add_scale=1.0, pair2=None):
    nk = grid[2]
    acc_shape = tuple(d for d in o_blk if d is not None)
    n_in = 2 if pair2 is None else 4

    def kern(*refs):
        p = n_in
        bias_ref = add_ref = None
        if bias is not None:
            bias_ref = refs[p]
            p += 1
        if add is not None:
            add_ref = refs[p]
            p += 1
        o_ref, acc = refs[p], refs[p + 1]
        k = pl.program_id(2)

        def product():
            r = _dot(refs[0][...].astype(BF16), refs[1][...].astype(BF16), dims)
            if pair2 is not None:
                r = r + _dot(refs[2][...].astype(BF16), refs[3][...].astype(BF16), dims)
            return r

        def finish(r):
            if bias_ref is not None:
                r = r + bias_ref[...]
            if add_ref is not None:
                r = r + add_scale * add_ref[...]
            o_ref[...] = r.astype(out_dtype)

        if nk == 1:
            finish(product())
        else:
            @pl.when(k == 0)
            def _():
                acc[...] = jnp.zeros_like(acc)

            acc[...] += product()

            @pl.when(k == nk - 1)
            def _():
                finish(acc[...])

    in_specs = [pl.BlockSpec(a_blk, a_map), pl.BlockSpec(b_blk, b_map)]
    args = [a, b]
    if pair2 is not None:
        in_specs += [pl.BlockSpec(a_blk, a_map), pl.BlockSpec(b_blk, b_map)]
        args += list(pair2)
    if bias is not None:
        in_specs.append(pl.BlockSpec(bias_blk, bias_map))
        args.append(bias)
    if add is not None:
        in_specs.append(pl.BlockSpec(o_blk, lambda i, j, k: o_map(i, j)))
        args.append(add)
    return pl.pallas_call(
        kern, name=name, grid=grid, in_specs=in_specs,
        out_specs=pl.BlockSpec(o_blk, lambda i, j, k: o_map(i, j)),
        out_shape=jax.ShapeDtypeStruct(out_shape, out_dtype),
        scratch_shapes=[pltpu.VMEM(acc_shape if nk > 1 else (SUBLANES, 128), F32)],
        compiler_params=_params(("parallel", "parallel", "arbitrary")),
    )(*args)


NN = ((1,), (0,))
NT = ((1,), (1,))
TN = ((0,), (0,))
TM = 512


def _sigmoid(x):
    return 1.0 / (1.0 + jnp.exp(-x))


_GELU_C = math.sqrt(2.0 / math.pi)


def _gelu(x):
    return 0.5 * x * (1.0 + jnp.tanh(_GELU_C * (x + 0.044715 * x * x * x)))


def _gelu_grad(x):
    th = jnp.tanh(_GELU_C * (x + 0.044715 * x * x * x))
    return 0.5 * (1.0 + th) + 0.5 * x * (1.0 - th * th) * _GELU_C * (1.0 + 3 * 0.044715 * x * x)


def _rope_swap(t):
    lane = lax.broadcasted_iota(jnp.int32, t.shape, 1)
    lo = (lane % HEAD_DIM) < (HEAD_DIM // 2)
    return jnp.where(lo, pltpu.roll(t, 128 - HEAD_DIM // 2, 1), pltpu.roll(t, HEAD_DIM // 2, 1))


D_QKV = D_ATTN + 2 * D_KV
TMM = 1024


def _in_proj(xb, w_in, b_in, cos, sin_s):
    t = xb.shape[0]
    tm = _row_tile(t, TMM)

    def kern(x_ref, w_ref, b_ref, c_ref, s_ref, q_ref, u_ref):
        x = x_ref[...]
        c = c_ref[...]
        s = s_ref[...]
        for j in range(N_CHIPS):
            pj = _dot(x, w_ref[j], NN) + b_ref[:, j * IN_SH:(j + 1) * IN_SH]
            for ch in range(IN_SH // 128):
                col = j * IN_SH + ch * 128
                v = pj[:, ch * 128:(ch + 1) * 128]
                if col < D_ATTN + D_KV:
                    v = v * c + _rope_swap(v) * s
                if col < D_ATTN:
                    v = v * (HEAD_DIM ** -0.5)
                if col < D_QKV:
                    q_ref[:, col:col + 128] = v.astype(BF16)
                else:
                    u_ref[:, col - D_QKV:col - D_QKV + 128] = v

    row = lambda w: pl.BlockSpec((tm, w), lambda i: (i, 0))
    return pl.pallas_call(
        kern, name="in_proj", grid=(t // tm,),
        in_specs=[row(D_MODEL), pl.BlockSpec((N_CHIPS, D_MODEL, IN_SH), lambda i: (0, 0, 0)),
                  pl.BlockSpec((1, D_IN), lambda i: (0, 0)), row(128), row(128)],
        out_specs=[row(D_QKV), row(D_IN - D_QKV)],
        out_shape=[jax.ShapeDtypeStruct((t, D_QKV), BF16), jax.ShapeDtypeStruct((t, D_IN - D_QKV), F32)],
        compiler_params=_params(("parallel",)),
    )(xb, w_in, b_in, cos, sin_s)


def _attn_mask(i):
    qi = lax.broadcasted_iota(jnp.int32, (ATTN_BLOCK, 2 * ATTN_BLOCK), 0)
    si = lax.broadcasted_iota(jnp.int32, (ATTN_BLOCK, 2 * ATTN_BLOCK), 1)
    diff = qi + ATTN_BLOCK - si
    return (diff >= 0) & (diff < ATTN_BLOCK) & ((si >= ATTN_BLOCK) | (i > 0))


def _attn_fwd(qkv, sinks):
    t = qkv.shape[0]
    nb = t // ATTN_BLOCK

    def kern(q_ref, kp_ref, kc_ref, vp_ref, vc_ref, s_ref, o_ref, l_ref):
        i = pl.program_id(0)
        valid = _attn_mask(i)
        kband = jnp.concatenate([kp_ref[...], kc_ref[...]], axis=0)
        vband = jnp.concatenate([vp_ref[...], vc_ref[...]], axis=0)
        for h in range(N_Q_HEADS):
            kh = h // Q_PER_KV
            q = q_ref[:, h * HEAD_DIM:(h + 1) * HEAD_DIM]
            k = kband[:, kh * HEAD_DIM:(kh + 1) * HEAD_DIM]
            v = vband[:, kh * HEAD_DIM:(kh + 1) * HEAD_DIM]
            s = lax.dot_general(q, k, (NT, ((), ())), preferred_element_type=F32)
            s = jnp.where(valid, s, -jnp.inf)
            sink = s_ref[0:1, h:h + 1]
            m = jnp.maximum(jnp.max(s, axis=-1, keepdims=True), sink)
            e = jnp.exp(s - m)
            denom = jnp.sum(e, axis=-1, keepdims=True) + jnp.exp(sink - m)
            p = (e / denom).astype(BF16)
            o_ref[:, h * HEAD_DIM:(h + 1) * HEAD_DIM] = lax.dot_general(
                p, v, (NN, ((), ())), preferred_element_type=F32)
            l_ref[:, h:h + 1] = m + jnp.log(denom)

    blk = lambda w, f: pl.BlockSpec((ATTN_BLOCK, w), f)
    return pl.pallas_call(
        kern, name="attn_fwd", grid=(nb,),
        in_specs=[blk(512, lambda i: (i, 0)),
                  blk(128, lambda i: (jnp.maximum(i - 1, 0), 4)), blk(128, lambda i: (i, 4)),
                  blk(128, lambda i: (jnp.maximum(i - 1, 0), 5)), blk(128, lambda i: (i, 5)),
                  pl.BlockSpec((1, N_Q_HEADS), lambda i: (0, 0))],
        out_specs=[blk(512, lambda i: (i, 0)), blk(N_Q_HEADS, lambda i: (i, 0))],
        out_shape=[jax.ShapeDtypeStruct((t, D_ATTN), F32), jax.ShapeDtypeStruct((t, N_Q_HEADS), F32)],
        compiler_params=_params(("parallel",)),
    )(qkv, qkv, qkv, qkv, qkv, sinks)


def _attn_bwd(qkv, o, do, lse, sinks):
    t = qkv.shape[0]
    nb = t // ATTN_BLOCK

    def kern(q_ref, kp_ref, kc_ref, vp_ref, vc_ref, o_ref, do_ref, l_ref, s_ref,
             dq_ref, dk_ref, dv_ref, ds_ref, ck, cv):
        i = pl.program_id(0)

        @pl.when(i == 0)
        def _():
            ds_ref[...] = jnp.zeros_like(ds_ref)
            ck[...] = jnp.zeros_like(ck)
            cv[...] = jnp.zeros_like(cv)

        @pl.when(i < nb)
        def _():
            valid = _attn_mask(i)
            kband = jnp.concatenate([kp_ref[...], kc_ref[...]], axis=0)
            vband = jnp.concatenate([vp_ref[...], vc_ref[...]], axis=0)
            dkb = [jnp.zeros((2 * ATTN_BLOCK, HEAD_DIM), F32) for _ in range(N_KV_HEADS)]
            dvb = [jnp.zeros((2 * ATTN_BLOCK, HEAD_DIM), F32) for _ in range(N_KV_HEADS)]
            for h in range(N_Q_HEADS):
                kh = h // Q_PER_KV
                sl = slice(h * HEAD_DIM, (h + 1) * HEAD_DIM)
                q = q_ref[:, sl]
                k = kband[:, kh * HEAD_DIM:(kh + 1) * HEAD_DIM]
                v = vband[:, kh * HEAD_DIM:(kh + 1) * HEAD_DIM]
                d_o = do_ref[:, sl]
                s = lax.dot_general(q, k, (NT, ((), ())), preferred_element_type=F32)
                lse_h = l_ref[:, h:h + 1]
                p = jnp.where(valid, jnp.exp(s - lse_h), 0.0)
                delta = jnp.sum(d_o * o_ref[:, sl], axis=-1, keepdims=True)
                dob = d_o.astype(BF16)
                dp = lax.dot_general(dob, v, (NT, ((), ())), preferred_element_type=F32)
                dsc = (p * (dp - delta)).astype(BF16)
                dq_ref[:, sl] = lax.dot_general(dsc, k, (NN, ((), ())), preferred_element_type=F32)
                dkb[kh] = dkb[kh] + lax.dot_general(dsc, q, (TN, ((), ())), preferred_element_type=F32)
                dvb[kh] = dvb[kh] + lax.dot_general(p.astype(BF16), dob, (TN, ((), ())),
                                                    preferred_element_type=F32)
                psink = jnp.exp(s_ref[0:1, h:h + 1] - lse_h)
                ds_ref[0:1, h:h + 1] += -jnp.sum(psink * delta, axis=0, keepdims=True)
            dk_band = jnp.concatenate(dkb, axis=1)
            dv_band = jnp.concatenate(dvb, axis=1)
            dk_ref[...] = ck[...] + dk_band[:ATTN_BLOCK]
            dv_ref[...] = cv[...] + dv_band[:ATTN_BLOCK]
            ck[...] = dk_band[ATTN_BLOCK:]
            cv[...] = dv_band[ATTN_BLOCK:]

        @pl.when(i == nb)
        def _():
            dk_ref[...] = ck[...]
            dv_ref[...] = cv[...]

    blk = lambda w, f: pl.BlockSpec((ATTN_BLOCK, w), f)
    cur = lambda i: jnp.minimum(i, nb - 1)
    prev = lambda i: jnp.clip(i - 1, 0, nb - 1)
    return pl.pallas_call(
        kern, name="attn_bwd", grid=(nb + 1,),
        in_specs=[blk(512, lambda i: (cur(i), 0)),
                  blk(128, lambda i: (prev(i), 4)), blk(128, lambda i: (cur(i), 4)),
                  blk(128, lambda i: (prev(i), 5)), blk(128, lambda i: (cur(i), 5)),
                  blk(512, lambda i: (cur(i), 0)), blk(512, lambda i: (cur(i), 0)),
                  blk(N_Q_HEADS, lambda i: (cur(i), 0)),
                  pl.BlockSpec((1, N_Q_HEADS), lambda i: (0, 0))],
        out_specs=[blk(512, lambda i: (cur(i), 0)), blk(128, lambda i: (prev(i), 0)),
                   blk(128, lambda i: (prev(i), 0)), pl.BlockSpec((1, N_Q_HEADS), lambda i: (0, 0))],
        out_shape=[jax.ShapeDtypeStruct((t, D_ATTN), F32), jax.ShapeDtypeStruct((t, D_KV), F32),
                   jax.ShapeDtypeStruct((t, D_KV), F32), jax.ShapeDtypeStruct((1, N_Q_HEADS), F32)],
        scratch_shapes=[pltpu.VMEM((ATTN_BLOCK, D_KV), F32), pltpu.VMEM((ATTN_BLOCK, D_KV), F32)],
        compiler_params=_params(("arbitrary",)),
    )(qkv, qkv, qkv, qkv, qkv, o, do, lse, sinks)


_GROUPS = ((0, D_ATTN), (D_ATTN, D_ATTN + D_S5), (D_ATTN + D_S5, D_MODEL))


def _rms_fwd(ya, ys, yl, g):
    t = ya.shape[0]
    tm = _row_tile(t, TM)

    def kern(a_ref, s_ref, l_ref, g_ref, o_ref):
        for (lo, hi), ref in zip(_GROUPS, (a_ref, s_ref, l_ref)):
            y = ref[...]
            n = y * lax.rsqrt(jnp.mean(y * y, axis=-1, keepdims=True) + RMS_EPS)
            o_ref[:, lo:hi] = (n * g_ref[:, lo:hi]).astype(BF16)

    row = lambda w: pl.BlockSpec((tm, w), lambda i: (i, 0))
    return pl.pallas_call(
        kern, name="rms_fwd", grid=(t // tm,),
        in_specs=[row(D_ATTN), row(D_S5), row(D_LRU), pl.BlockSpec((1, D_MODEL), lambda i: (0, 0))],
        out_specs=row(D_MODEL), out_shape=jax.ShapeDtypeStruct((t, D_MODEL), BF16),
        compiler_params=_params(("parallel",)),
    )(ya, ys, yl, g)


def _rms_bwd(dmix, ya, ys, yl, g):
    t = ya.shape[0]
    tm = _row_tile(t, TM)

    def kern(d_ref, a_ref, s_ref, l_ref, g_ref, da_ref, ds_ref, dl_ref, dg_ref):
        @pl.when(pl.program_id(0) == 0)
        def _():
            dg_ref[...] = jnp.zeros_like(dg_ref)

        for (lo, hi), ref, out in zip(_GROUPS, (a_ref, s_ref, l_ref), (da_ref, ds_ref, dl_ref)):
            y = ref[...]
            rstd = lax.rsqrt(jnp.mean(y * y, axis=-1, keepdims=True) + RMS_EPS)
            n = y * rstd
            dm = d_ref[:, lo:hi]
            dg_ref[:, lo:hi] += jnp.sum(dm * n, axis=0, keepdims=True)
            dn = dm * g_ref[:, lo:hi]
            out[...] = rstd * (dn - n * jnp.mean(dn * n, axis=-1, keepdims=True))

    row = lambda w: pl.BlockSpec((tm, w), lambda i: (i, 0))
    vec = pl.BlockSpec((1, D_MODEL), lambda i: (0, 0))
    return pl.pallas_call(
        kern, name="rms_bwd", grid=(t // tm,),
        in_specs=[row(D_MODEL), row(D_ATTN), row(D_S5), row(D_LRU), vec],
        out_specs=[row(D_ATTN), row(D_S5), row(D_LRU), vec],
        out_shape=[jax.ShapeDtypeStruct((t, D_ATTN), F32), jax.ShapeDtypeStruct((t, D_S5), F32),
                   jax.ShapeDtypeStruct((t, D_LRU), F32), jax.ShapeDtypeStruct((1, D_MODEL), F32)],
        compiler_params=_params(("arbitrary",)),
    )(dmix, ya, ys, yl, g)


def _matmul_ln(name, a, w, bias, xres, g, b, a_blk, a_map, w_blk, parts):
    t = xres.shape[0]
    tm = a_blk[-2]

    def kern(a_ref, w_ref, bias_ref, x_ref, g_ref, b_ref, y_ref, yb_ref, h_ref, r_ref):
        if parts is None:
            f = _dot(a_ref[...], w_ref[...], NN)
        else:
            f = sum(_dot(a_ref[j], w_ref[j], NN) for j in range(parts))
        r = ALPHA * x_ref[...] + f + bias_ref[...]
        mu = jnp.mean(r, axis=-1, keepdims=True)
        xc = r - mu
        rstd = lax.rsqrt(jnp.mean(xc * xc, axis=-1, keepdims=True) + LN_EPS)
        xhat = xc * rstd
        h_ref[...] = xhat
        r_ref[...] = rstd
        y = xhat * g_ref[...] + b_ref[...]
        y_ref[...] = y
        yb_ref[...] = y.astype(BF16)

    row = pl.BlockSpec((tm, D_MODEL), lambda i: (i, 0))
    vec = pl.BlockSpec((1, D_MODEL), lambda i: (0, 0))
    big = lambda dt: jax.ShapeDtypeStruct((t, D_MODEL), dt)
    return pl.pallas_call(
        kern, name=name, grid=(t // tm,),
        in_specs=[pl.BlockSpec(a_blk, a_map), pl.BlockSpec(w_blk, lambda i: (0,) * len(w_blk)), vec, row, vec, vec],
        out_specs=[row, row, row, pl.BlockSpec((tm, 1), lambda i: (i, 0))],
        out_shape=[big(F32), big(BF16), big(F32), jax.ShapeDtypeStruct((t, 1), F32)],
        compiler_params=_params(("parallel",)),
    )(a, w, bias, xres, g, b)


def _ln_bwd(dy, xhat, rstd, g):
    t = dy.shape[0]
    tm = _row_tile(t, TM)

    def kern(d_ref, h_ref, r_ref, g_ref, dr_ref, drb_ref, dg_ref, db_ref, sr_ref):
        @pl.when(pl.program_id(0) == 0)
        def _():
            dg_ref[...] = jnp.zeros_like(dg_ref)
            db_ref[...] = jnp.zeros_like(db_ref)
            sr_ref[...] = jnp.zeros_like(sr_ref)

        d = d_ref[...]
        xhat = h_ref[...]
        dg_ref[...] += jnp.sum(d * xhat, axis=0, keepdims=True)
        db_ref[...] += jnp.sum(d, axis=0, keepdims=True)
        dh = d * g_ref[...]
        dr = r_ref[...] * (dh - jnp.mean(dh, axis=-1, keepdims=True)
                           - xhat * jnp.mean(dh * xhat, axis=-1, keepdims=True))
        dr_ref[...] = dr
        drb_ref[...] = dr.astype(BF16)
        sr_ref[...] += jnp.sum(dr, axis=0, keepdims=True)

    row = pl.BlockSpec((tm, D_MODEL), lambda i: (i, 0))
    vec = pl.BlockSpec((1, D_MODEL), lambda i: (0, 0))
    vshape = jax.ShapeDtypeStruct((1, D_MODEL), F32)
    return pl.pallas_call(
        kern, name="ln_bwd", grid=(t // tm,),
        in_specs=[row, row, pl.BlockSpec((tm, 1), lambda i: (i, 0)), vec],
        out_specs=[row, row, vec, vec, vec],
        out_shape=[jax.ShapeDtypeStruct((t, D_MODEL), F32), jax.ShapeDtypeStruct((t, D_MODEL), BF16),
                   vshape, vshape, vshape],
        compiler_params=_params(("arbitrary",)),
    )(dy, xhat, rstd, g)


def _loss_head(y, target):
    t = y.shape[0]
    tm = _row_tile(t, TM)

    def kern(y_ref, t_ref, l_ref, d_ref):
        @pl.when(pl.program_id(0) == 0)
        def _():
            l_ref[...] = jnp.zeros_like(l_ref)

        err = y_ref[...] - t_ref[...]
        d_ref[...] = err * (1.0 / D_MODEL)
        part = jnp.sum(jnp.sum(err * err, axis=-1, keepdims=True), axis=0, keepdims=True)
        l_ref[...] += jnp.broadcast_to(part * (0.5 / D_MODEL), l_ref.shape)

    row = pl.BlockSpec((tm, D_MODEL), lambda i: (i, 0))
    return pl.pallas_call(
        kern, name="loss_head", grid=(t // tm,),
        in_specs=[row, row], out_specs=[pl.BlockSpec((1, 128), lambda i: (0, 0)), row],
        out_shape=[jax.ShapeDtypeStruct((1, 128), F32), jax.ShapeDtypeStruct((t, D_MODEL), F32)],
        compiler_params=_params(("arbitrary",)),
    )(y, target)


HALO = 8


def _ffn_mid_specs(t, tm):
    main = pl.BlockSpec((None, tm, FF_SH), lambda j, i: (j, i, 0))
    prev = pl.BlockSpec((None, HALO, FF_SH), lambda j, i: (j, jnp.maximum(i * (tm // HALO) - 1, 0), 0))
    cw = pl.BlockSpec((None, FFN_CONV, FF_SH), lambda j, i: (j, 0, 0))
    cb = pl.BlockSpec((None, 1, FF_SH), lambda j, i: (j, 0, 0))
    return main, prev, cw, cb


def _ffn_conv(ext, g_ref, p_ref, w_ref, b_ref, tm):
    i = pl.program_id(1)
    ext[0:HALO, :] = jnp.where(i > 0, p_ref[...], 0.0)
    ext[HALO:, :] = g_ref[...]
    taps = [ext[pl.ds(HALO - (FFN_CONV - 1) + k, tm), :] for k in range(FFN_CONV)]
    gc = b_ref[...] + sum(w_ref[k:k + 1, :] * taps[k] for k in range(FFN_CONV))
    return gc, taps


def _ffn_mid_fwd(gpre, up, cw, cb):
    t = gpre.shape[1]
    tm = _row_tile(t, TM)

    def kern(g_ref, p_ref, u_ref, w_ref, b_ref, o_ref, ext):
        gc, _ = _ffn_conv(ext, g_ref, p_ref, w_ref, b_ref, tm)
        o_ref[...] = (gc * _sigmoid(gc) * u_ref[...]).astype(BF16)

    main, prev, cws, cbs = _ffn_mid_specs(t, tm)
    return pl.pallas_call(
        kern, name="ffn_mid_fwd", grid=(N_CHIPS, t // tm),
        in_specs=[main, prev, main, cws, cbs], out_specs=main,
        out_shape=jax.ShapeDtypeStruct((N_CHIPS, t, FF_SH), BF16),
        scratch_shapes=[pltpu.VMEM((tm + HALO, FF_SH), F32)],
        compiler_params=_params(("parallel", "parallel")),
    )(gpre, gpre, up, cw, cb)


def _ffn_mid_bwd(gpre, up, dhmid, cw, cb):
    t = gpre.shape[1]
    tm = _row_tile(t, TM)

    def kern(g_ref, p_ref, u_ref, d_ref, w_ref, b_ref, h_ref, du_ref, dg_ref, dw_ref, db_ref, ext):
        @pl.when(pl.program_id(1) == 0)
        def _():
            dw_ref[...] = jnp.zeros_like(dw_ref)
            db_ref[...] = jnp.zeros_like(db_ref)

        gc, taps = _ffn_conv(ext, g_ref, p_ref, w_ref, b_ref, tm)
        sg = _sigmoid(gc)
        s = gc * sg
        u = u_ref[...]
        d = d_ref[...]
        h_ref[...] = (s * u).astype(BF16)
        du_ref[...] = (d * s).astype(BF16)
        dgc = d * u * (sg * (1.0 + gc * (1.0 - sg)))
        dg_ref[...] = dgc
        db_ref[...] += jnp.sum(dgc, axis=0, keepdims=True)
        for k in range(FFN_CONV):
            dw_ref[k:k + 1, :] += jnp.sum(dgc * taps[k], axis=0, keepdims=True)

    main, prev, cws, cbs = _ffn_mid_specs(t, tm)
    big = lambda dt: jax.ShapeDtypeStruct((N_CHIPS, t, FF_SH), dt)
    return pl.pallas_call(
        kern, name="ffn_mid_bwd", grid=(N_CHIPS, t // tm),
        in_specs=[main, prev, main, main, cws, cbs], out_specs=[main, main, main, cws, cbs],
        out_shape=[big(BF16), big(BF16), big(F32), jax.ShapeDtypeStruct((N_CHIPS, FFN_CONV, FF_SH), F32),
                   jax.ShapeDtypeStruct((N_CHIPS, 1, FF_SH), F32)],
        scratch_shapes=[pltpu.VMEM((tm + HALO, FF_SH), F32)],
        compiler_params=_params(("parallel", "arbitrary")),
    )(gpre, gpre, up, dhmid, cw, cb)


def _ffn_conv_t(dgc, cw):
    t = dgc.shape[1]
    tm = _row_tile(t, TM)
    nt = t // tm

    def kern(d_ref, n_ref, w_ref, o_ref, ext):
        i = pl.program_id(1)
        ext[0:tm, :] = d_ref[...]
        ext[tm:, :] = jnp.where(i < nt - 1, n_ref[...], 0.0)
        acc = sum(w_ref[k:k + 1, :] * ext[pl.ds(FFN_CONV - 1 - k, tm), :] for k in range(FFN_CONV))
        o_ref[...] = acc.astype(BF16)

    main, _, cws, _ = _ffn_mid_specs(t, tm)
    nxt = pl.BlockSpec((None, HALO, FF_SH),
                       lambda j, i: (j, jnp.minimum((i + 1) * (tm // HALO), t // HALO - 1), 0))
    return pl.pallas_call(
        kern, name="ffn_conv_t", grid=(N_CHIPS, nt),
        in_specs=[main, nxt, cws], out_specs=main,
        out_shape=jax.ShapeDtypeStruct((N_CHIPS, t, FF_SH), BF16),
        scratch_shapes=[pltpu.VMEM((tm + HALO, FF_SH), F32)],
        compiler_params=_params(("parallel", "parallel")),
    )(dgc, dgc, cw)


def _ffn_hidden_fwd(xb, wg, wu, cw, cb):
    t = xb.shape[0]
    tm = _row_tile(t, TM)

    def kern(x_ref, wg_ref, wu_ref, cw_ref, cb_ref, g_ref, u_ref, h_ref, ext):
        @pl.when(pl.program_id(1) == 0)
        def _():
            ext[0:HALO, :] = jnp.zeros((HALO, FF_SH), F32)

        x = x_ref[...]
        gb = _dot(x, wg_ref[...], NN).astype(BF16)
        ub = _dot(x, wu_ref[...], NN).astype(BF16)
        g_ref[...] = gb
        u_ref[...] = ub
        g = gb.astype(F32)
        ext[HALO:, :] = g
        gc = cb_ref[...] + sum(cw_ref[k:k + 1, :] * ext[pl.ds(HALO - (FFN_CONV - 1) + k, tm), :]
                               for k in range(FFN_CONV))
        h_ref[...] = (gc * _sigmoid(gc) * ub.astype(F32)).astype(BF16)
        ext[0:HALO, :] = g[tm - HALO:, :]

    col = pl.BlockSpec((None, tm, FF_SH), lambda j, i: (j, i, 0))
    wspec = pl.BlockSpec((None, D_MODEL, FF_SH), lambda j, i: (j, 0, 0))
    big = jax.ShapeDtypeStruct((N_CHIPS, t, FF_SH), BF16)
    return pl.pallas_call(
        kern, name="ffn_hidden_fwd", grid=(N_CHIPS, t // tm),
        in_specs=[pl.BlockSpec((tm, D_MODEL), lambda j, i: (i, 0)), wspec, wspec,
                  pl.BlockSpec((None, FFN_CONV, FF_SH), lambda j, i: (j, 0, 0)),
                  pl.BlockSpec((None, 1, FF_SH), lambda j, i: (j, 0, 0))],
        out_specs=[col, col, col], out_shape=[big, big, big],
        scratch_shapes=[pltpu.VMEM((tm + HALO, FF_SH), F32)],
        compiler_params=_params(("parallel", "arbitrary")),
    )(xb, wg, wu, cw, cb)


def _ffn_hidden_bwd(drb, gpre, up, wd, cw, cb):
    t = drb.shape[0]
    tm = _row_tile(t, TM)
    nt = t // tm
    rb = lambda i: nt - 1 - i

    def kern(d_ref, g_ref, p_ref, u_ref, wd_ref, cw_ref, cb_ref, h_ref, du_ref, dg_ref, dw_ref, db_ref, ext, ext2):
        i = pl.program_id(1)

        @pl.when(i == 0)
        def _():
            dw_ref[...] = jnp.zeros_like(dw_ref)
            db_ref[...] = jnp.zeros_like(db_ref)
            ext2[tm:, :] = jnp.zeros((HALO, FF_SH), F32)

        dh = _dot(d_ref[...], wd_ref[...], NT)
        ext[0:HALO, :] = jnp.where(i < nt - 1, p_ref[...].astype(F32)[HALO16 - HALO:, :], 0.0)
        ext[HALO:, :] = g_ref[...].astype(F32)
        taps = [ext[pl.ds(HALO - (FFN_CONV - 1) + k, tm), :] for k in range(FFN_CONV)]
        gc = cb_ref[...] + sum(cw_ref[k:k + 1, :] * taps[k] for k in range(FFN_CONV))
        sg = _sigmoid(gc)
        s = gc * sg
        u = u_ref[...].astype(F32)
        h_ref[...] = (s * u).astype(BF16)
        du_ref[...] = (dh * s).astype(BF16)
        dgc = dh * u * (sg * (1.0 + gc * (1.0 - sg)))
        db_ref[...] += jnp.sum(dgc, axis=0, keepdims=True)
        for k in range(FFN_CONV):
            dw_ref[k:k + 1, :] += jnp.sum(dgc * taps[k], axis=0, keepdims=True)
        ext2[0:tm, :] = dgc
        dg_ref[...] = sum(cw_ref[k:k + 1, :] * ext2[pl.ds(FFN_CONV - 1 - k, tm), :]
                          for k in range(FFN_CONV)).astype(BF16)
        ext2[tm:, :] = dgc[0:HALO, :]

    col = pl.BlockSpec((None, tm, FF_SH), lambda j, i: (j, rb(i), 0))
    prev = pl.BlockSpec((None, HALO16, FF_SH), lambda j, i: (j, jnp.maximum(rb(i) * (tm // HALO16) - 1, 0), 0))
    cws = pl.BlockSpec((None, FFN_CONV, FF_SH), lambda j, i: (j, 0, 0))
    cbs = pl.BlockSpec((None, 1, FF_SH), lambda j, i: (j, 0, 0))
    big = jax.ShapeDtypeStruct((N_CHIPS, t, FF_SH), BF16)
    return pl.pallas_call(
        kern, name="ffn_hidden_bwd", grid=(N_CHIPS, nt),
        in_specs=[pl.BlockSpec((tm, D_MODEL), lambda j, i: (rb(i), 0)), col, prev, col,
                  pl.BlockSpec((None, FF_SH, D_MODEL), lambda j, i: (j, 0, 0)), cws, cbs],
        out_specs=[col, col, col, cws, cbs],
        out_shape=[big, big, big, jax.ShapeDtypeStruct((N_CHIPS, FFN_CONV, FF_SH), F32),
                   jax.ShapeDtypeStruct((N_CHIPS, 1, FF_SH), F32)],
        scratch_shapes=[pltpu.VMEM((tm + HALO, FF_SH), F32), pltpu.VMEM((tm + HALO, FF_SH), F32)],
        compiler_params=_params(("parallel", "arbitrary")),
    )(drb, gpre, gpre, up, wd, cw, cb)


def _s5_coefs(ar, ai, reverse):
    if reverse:
        ai = -ai
    pw = [(ar, ai)]
    for _ in range(SUBLANES - 1):
        pr, pi = pw[-1]
        pw.append((pr * ar - pi * ai, pr * ai + pi * ar))
    rows = jnp.arange(SUBLANES)[:, None]
    out = []
    for s in (1, 2, 4):
        keep = (rows + s <= SUBLANES - 1) if reverse else (rows >= s)
        out += [jnp.where(keep, pw[s - 1][0][None], 0.0), jnp.where(keep, pw[s - 1][1][None], 0.0)]
    order = list(range(SUBLANES - 1, -1, -1)) if reverse else list(range(SUBLANES))
    out += [jnp.stack([pw[k][0] for k in order]), jnp.stack([pw[k][1] for k in order])]
    return jnp.stack(out).astype(F32)


def _s5_scan(buf, coef_ref, carry, tm, reverse):
    n8 = tm // SUBLANES

    def body(it, c):
        cre, cim = c
        blk = (n8 - 1 - it) if reverse else it
        r0 = pl.multiple_of(blk * SUBLANES, SUBLANES)
        xre = buf[pl.ds(r0, SUBLANES), 0:N_STATE]
        xim = buf[pl.ds(r0, SUBLANES), N_STATE:]
        for idx, s in enumerate((1, 2, 4)):
            sh = (SUBLANES - s) if reverse else s
            sre = pltpu.roll(xre, sh, 0)
            sim = pltpu.roll(xim, sh, 0)
            are = coef_ref[2 * idx]
            aim = coef_ref[2 * idx + 1]
            xre, xim = xre + are * sre - aim * sim, xim + are * sim + aim * sre
        pre = coef_ref[6]
        pim = coef_ref[7]
        hre = xre + pre * cre - pim * cim
        him = xim + pre * cim + pim * cre
        buf[pl.ds(r0, SUBLANES), 0:N_STATE] = hre
        buf[pl.ds(r0, SUBLANES), N_STATE:] = him
        row = 0 if reverse else SUBLANES - 1
        return (jnp.broadcast_to(hre[row:row + 1], (SUBLANES, N_STATE)),
                jnp.broadcast_to(him[row:row + 1], (SUBLANES, N_STATE)))

    cre, cim = lax.fori_loop(0, n8, body, (carry[:, 0:N_STATE], carry[:, N_STATE:]))
    carry[:, 0:N_STATE] = cre
    carry[:, N_STATE:] = cim


def _real_scan(abuf, bbuf, carry, tm, reverse):
    n8 = tm // SUBLANES
    width = bbuf.shape[1]

    def body(it, c):
        blk = (n8 - 1 - it) if reverse else it
        r0 = pl.multiple_of(blk * SUBLANES, SUBLANES)
        a = abuf[pl.ds(r0, SUBLANES), :]
        b = bbuf[pl.ds(r0, SUBLANES), :]
        rows = lax.broadcasted_iota(jnp.int32, (SUBLANES, width), 0)
        for s in (1, 2, 4):
            sh = (SUBLANES - s) if reverse else s
            keep = (rows + s <= SUBLANES - 1) if reverse else (rows >= s)
            sa = pltpu.roll(a, sh, 0)
            sb = pltpu.roll(b, sh, 0)
            b = b + a * jnp.where(keep, sb, 0.0)
            a = a * jnp.where(keep, sa, 1.0)
        h = b + a * c
        bbuf[pl.ds(r0, SUBLANES), :] = h
        row = 0 if reverse else SUBLANES - 1
        return jnp.broadcast_to(h[row:row + 1], (SUBLANES, width))

    carry[...] = lax.fori_loop(0, n8, body, carry[...])


def _dot(a, b, dims):
    return lax.dot_general(a, b, (dims, ((), ())), preferred_element_type=F32)


TS5 = 256
HALO16 = 16


def _s5_fwd(proj, bmat, coef, cmat, dvec, gw, gb):
    t = proj.shape[0]
    tm = _row_tile(t, TS5)

    def kern(u_ref, b_ref, coef_ref, c_ref, d_ref, gw_ref, gb_ref, h_ref, y_ref, hbuf, carry):
        @pl.when(pl.program_id(0) == 0)
        def _():
            carry[...] = jnp.zeros_like(carry)

        u = u_ref[...]
        hbuf[...] = _dot(u.astype(BF16), b_ref[...], NN)
        _s5_scan(hbuf, coef_ref, carry, tm, False)
        hb = hbuf[...].astype(BF16)
        h_ref[...] = hb
        y = _dot(hb, c_ref[...], NN) + d_ref[...] * u
        ys = _gelu(y)
        z = _dot(ys.astype(BF16), gw_ref[...], NN) + gb_ref[...]
        y_ref[...] = ys * _sigmoid(z)

    full = lambda shp: pl.BlockSpec(shp, lambda i: (0,) * len(shp))
    return pl.pallas_call(
        kern, name="s5_fwd", grid=(t // tm,),
        in_specs=[pl.BlockSpec((tm, D_S5), lambda i: (i, 0)), full((D_S5, 2 * N_STATE)),
                  full((8, SUBLANES, N_STATE)), full((2 * N_STATE, D_S5)), full((1, D_S5)),
                  full((D_S5, D_S5)), full((1, D_S5))],
        out_specs=[pl.BlockSpec((tm, 2 * N_STATE), lambda i: (i, 0)), pl.BlockSpec((tm, D_S5), lambda i: (i, 0))],
        out_shape=[jax.ShapeDtypeStruct((t, 2 * N_STATE), BF16), jax.ShapeDtypeStruct((t, D_S5), F32)],
        scratch_shapes=[pltpu.VMEM((tm, 2 * N_STATE), F32), pltpu.VMEM((SUBLANES, 2 * N_STATE), F32)],
        compiler_params=_params(("arbitrary",)),
    )(proj, bmat, coef, cmat, dvec, gw, gb)


def _s5_bwd(proj, h, dout, bmat, coef_b, cmat, dvec, gw, gb):
    t = proj.shape[0]
    tm = _row_tile(t, TS5)
    nt = t // tm
    rb = lambda i: nt - 1 - i

    def kern(u_ref, h_ref, hp_ref, d_ref, b_ref, coef_ref, c_ref, dv_ref, gw_ref, gb_ref,
             du_ref, dc_ref, db_ref, da_ref, dd_ref, dgw_ref, dgb_ref, gbuf, hext, carry):
        i = pl.program_id(0)

        @pl.when(i == 0)
        def _():
            carry[...] = jnp.zeros_like(carry)
            for r in (dc_ref, db_ref, da_ref, dd_ref, dgw_ref, dgb_ref):
                r[...] = jnp.zeros_like(r)

        u = u_ref[...]
        hb = h_ref[...]
        y = _dot(hb, c_ref[...], NN) + dv_ref[...] * u
        ys = _gelu(y)
        ysb = ys.astype(BF16)
        sg = _sigmoid(_dot(ysb, gw_ref[...], NN) + gb_ref[...])
        d_o = d_ref[...]
        dz = d_o * ys * sg * (1.0 - sg)
        dzb = dz.astype(BF16)
        dys = d_o * sg + _dot(dzb, gw_ref[...], NT)
        dgw_ref[...] += _dot(ysb, dzb, TN)
        dgb_ref[...] += jnp.sum(dz, axis=0, keepdims=True)
        dy = dys * _gelu_grad(y)
        dd_ref[...] += jnp.sum(dy * u, axis=0, keepdims=True)
        dyb = dy.astype(BF16)
        dc_ref[...] += _dot(hb, dyb, TN)
        gbuf[...] = _dot(dyb, c_ref[...], NT)
        _s5_scan(gbuf, coef_ref, carry, tm, True)
        g = gbuf[...]
        first = jnp.where(i < nt - 1, hp_ref[HALO16 - 1:HALO16, :].astype(F32), 0.0)
        hext[SUBLANES - 1:SUBLANES, :] = first
        hext[SUBLANES:, :] = hb.astype(F32)
        hprev = hext[pl.ds(SUBLANES - 1, tm), :]
        gre, gim = g[:, 0:N_STATE], g[:, N_STATE:]
        pre, pim = hprev[:, 0:N_STATE], hprev[:, N_STATE:]
        da_ref[0:1, :] += jnp.sum(gre * pre + gim * pim, axis=0, keepdims=True)
        da_ref[1:2, :] += jnp.sum(gim * pre - gre * pim, axis=0, keepdims=True)
        gb16 = g.astype(BF16)
        db_ref[...] += _dot(u.astype(BF16), gb16, TN)
        du_ref[...] = dy * dv_ref[...] + _dot(gb16, b_ref[...], NT)

    full = lambda shp: pl.BlockSpec(shp, lambda i: (0,) * len(shp))
    shape = lambda shp: jax.ShapeDtypeStruct(shp, F32)
    return pl.pallas_call(
        kern, name="s5_bwd", grid=(nt,),
        in_specs=[pl.BlockSpec((tm, D_S5), lambda i: (rb(i), 0)),
                  pl.BlockSpec((tm, 2 * N_STATE), lambda i: (rb(i), 0)),
                  pl.BlockSpec((HALO16, 2 * N_STATE), lambda i: (jnp.maximum(rb(i) * (tm // HALO16) - 1, 0), 0)),
                  pl.BlockSpec((tm, D_S5), lambda i: (rb(i), 0)),
                  full((D_S5, 2 * N_STATE)), full((8, SUBLANES, N_STATE)), full((2 * N_STATE, D_S5)),
                  full((1, D_S5)), full((D_S5, D_S5)), full((1, D_S5))],
        out_specs=[pl.BlockSpec((tm, D_S5), lambda i: (rb(i), 0)), full((2 * N_STATE, D_S5)),
                   full((D_S5, 2 * N_STATE)), full((2, N_STATE)), full((1, D_S5)), full((D_S5, D_S5)),
                   full((1, D_S5))],
        out_shape=[shape((t, D_S5)), shape((2 * N_STATE, D_S5)), shape((D_S5, 2 * N_STATE)),
                   shape((2, N_STATE)), shape((1, D_S5)), shape((D_S5, D_S5)), shape((1, D_S5))],
        scratch_shapes=[pltpu.VMEM((tm, 2 * N_STATE), F32), pltpu.VMEM((tm + SUBLANES, 2 * N_STATE), F32),
                        pltpu.VMEM((SUBLANES, 2 * N_STATE), F32)],
        compiler_params=_params(("arbitrary",)),
    )(proj, h, h, dout, bmat, coef_b, cmat, dvec, gw, gb)


def _lru_gates(ext, x_ref, p_ref, cw_ref, cb_ref, wx_ref, bx_ref, wa_ref, ba_ref, ap_ref, first_tile, row0, tm):
    ext[0:HALO, :] = jnp.where(first_tile, 0.0, p_ref[...])
    ext[HALO:, :] = x_ref[...]
    taps = [ext[pl.ds(HALO - (LRU_CONV - 1) + k, tm), :] for k in range(LRU_CONV)]
    xc = cb_ref[...] + sum(cw_ref[k:k + 1, :] * taps[k] for k in range(LRU_CONV))
    xcb = xc.astype(BF16)
    gx = _sigmoid(_dot(xcb, wx_ref[...], NN) + bx_ref[...])
    ga = _sigmoid(_dot(xcb, wa_ref[...], NN) + ba_ref[...])
    z = -ap_ref[...]
    sp = jnp.maximum(z, 0.0) + jnp.log(1.0 + jnp.exp(-jnp.abs(z)))
    log_a = -LRU_C * ga * sp
    a = jnp.exp(log_a)
    tok = row0 + lax.broadcasted_iota(jnp.int32, a.shape, 0)
    is0 = tok == 0
    mult = jnp.where(is0, 1.0, jnp.sqrt(1.0 - jnp.exp(2.0 * log_a)))
    return taps, xc, xcb, gx, ga, sp, a, mult, is0


def _lru_specs(tm, blk_of):
    col = lambda cidx: pl.BlockSpec((tm, D_LRU), lambda i: (blk_of(i), cidx))
    prev = lambda cidx: pl.BlockSpec((HALO, D_LRU), lambda i: (jnp.maximum(blk_of(i) * (tm // HALO) - 1, 0), cidx))
    full = lambda shp: pl.BlockSpec(shp, lambda i: (0,) * len(shp))
    wts = [full((LRU_CONV, D_LRU)), full((1, D_LRU)), full((D_LRU, D_LRU)), full((1, D_LRU)),
           full((D_LRU, D_LRU)), full((1, D_LRU)), full((1, D_LRU))]
    return col, prev, full, wts


def _lru_fwd(proj, cw, cb, wx, bx, wa, ba, ap):
    t = proj.shape[0]
    tm = _row_tile(t, TM)

    def kern(x_ref, p_ref, g_ref, cw_ref, cb_ref, wx_ref, bx_ref, wa_ref, ba_ref, ap_ref,
             y_ref, h_ref, ext, abuf, carry):
        i = pl.program_id(0)

        @pl.when(i == 0)
        def _():
            carry[...] = jnp.zeros_like(carry)

        _, xc, _, gx, _, _, a, mult, _ = _lru_gates(ext, x_ref, p_ref, cw_ref, cb_ref, wx_ref, bx_ref, wa_ref,
                                                    ba_ref, ap_ref, i == 0, i * tm, tm)
        abuf[...] = a
        h_ref[...] = mult * gx * xc
        _real_scan(abuf, h_ref, carry, tm, False)
        y_ref[...] = h_ref[...] * _gelu(g_ref[...])

    col, prev, full, wts = _lru_specs(tm, lambda i: i)
    out = pl.BlockSpec((tm, D_LRU), lambda i: (i, 0))
    return pl.pallas_call(
        kern, name="lru_fwd", grid=(t // tm,),
        in_specs=[col(1), prev(1), col(2)] + wts, out_specs=[out, out],
        out_shape=[jax.ShapeDtypeStruct((t, D_LRU), F32), jax.ShapeDtypeStruct((t, D_LRU), F32)],
        scratch_shapes=[pltpu.VMEM((tm + HALO, D_LRU), F32), pltpu.VMEM((tm, D_LRU), F32),
                        pltpu.VMEM((SUBLANES, D_LRU), F32)],
        compiler_params=_params(("arbitrary",)),
    )(proj, proj, proj, cw, cb, wx, bx, wa, ba, ap)


def _lru_bwd(proj, h, dout, cw, cb, wx, bx, wa, ba, ap):
    t = proj.shape[0]
    tm = _row_tile(t, TM)
    nt = t // tm
    rb = lambda i: nt - 1 - i

    def kern(x_ref, p_ref, g_ref, h_ref, hp_ref, d_ref, cw_ref, cb_ref, wx_ref, bx_ref, wa_ref, ba_ref, ap_ref,
             dxc_ref, dg_ref, dcw_ref, dcb_ref, dwx_ref, dbx_ref, dwa_ref, dba_ref, dap_ref,
             ext, aext, abuf, gbuf, carry, acarry):
        i = pl.program_id(0)
        blk = nt - 1 - i

        @pl.when(i == 0)
        def _():
            carry[...] = jnp.zeros_like(carry)
            acarry[...] = jnp.zeros_like(acarry)
            for r in (dcw_ref, dcb_ref, dwx_ref, dbx_ref, dwa_ref, dba_ref, dap_ref):
                r[...] = jnp.zeros_like(r)

        taps, xc, xcb, gx, ga, sp, a, mult, is0 = _lru_gates(
            ext, x_ref, p_ref, cw_ref, cb_ref, wx_ref, bx_ref, wa_ref, ba_ref, ap_ref, blk == 0, blk * tm, tm)
        gate = g_ref[...]
        d_o = d_ref[...]
        hcur = h_ref[...]
        dg_ref[...] = d_o * hcur * _gelu_grad(gate)
        aext[0:tm, :] = a
        aext[tm:, :] = acarry[...]
        abuf[...] = aext[pl.ds(1, tm), :]
        gbuf[...] = d_o * _gelu(gate)
        _real_scan(abuf, gbuf, carry, tm, True)
        acarry[...] = jnp.broadcast_to(a[0:1], acarry.shape)
        g = gbuf[...]
        ext[0:HALO, :] = jnp.where(blk == 0, 0.0, hp_ref[...])
        ext[HALO:, :] = hcur
        hprev = ext[pl.ds(HALO - 1, tm), :]
        dmult = jnp.where(is0, 0.0, g * gx * xc)
        dgx = g * mult * xc
        dxc = g * mult * gx
        dlog_a = g * hprev * a - dmult * (a * a) / mult
        dga = dlog_a * (-LRU_C * sp)
        dsp = jnp.sum(dlog_a * (-LRU_C * ga), axis=0, keepdims=True)
        dap_ref[...] += dsp * (-_sigmoid(-ap_ref[...]))
        dpa = (dga * ga * (1.0 - ga))
        dpx = (dgx * gx * (1.0 - gx))
        dpab, dpxb = dpa.astype(BF16), dpx.astype(BF16)
        dwx_ref[...] += _dot(xcb, dpxb, TN)
        dwa_ref[...] += _dot(xcb, dpab, TN)
        dbx_ref[...] += jnp.sum(dpx, axis=0, keepdims=True)
        dba_ref[...] += jnp.sum(dpa, axis=0, keepdims=True)
        dxc = dxc + _dot(dpxb, wx_ref[...], NT) + _dot(dpab, wa_ref[...], NT)
        dxc_ref[...] = dxc
        dcb_ref[...] += jnp.sum(dxc, axis=0, keepdims=True)
        for k in range(LRU_CONV):
            dcw_ref[k:k + 1, :] += jnp.sum(dxc * taps[k], axis=0, keepdims=True)

    col, prev, full, wts = _lru_specs(tm, rb)
    row = pl.BlockSpec((tm, D_LRU), lambda i: (rb(i), 0))
    hprev_spec = pl.BlockSpec((HALO, D_LRU), lambda i: (jnp.maximum(rb(i) * (tm // HALO) - 1, 0), 0))
    shape = lambda shp: jax.ShapeDtypeStruct(shp, F32)
    vec = (1, D_LRU)
    sq = (D_LRU, D_LRU)
    return pl.pallas_call(
        kern, name="lru_bwd", grid=(nt,),
        in_specs=[col(1), prev(1), col(2), row, hprev_spec, row] + wts,
        out_specs=[row, row, full((LRU_CONV, D_LRU)), full(vec), full(sq), full(vec), full(sq), full(vec), full(vec)],
        out_shape=[shape((t, D_LRU)), shape((t, D_LRU)), shape((LRU_CONV, D_LRU)), shape(vec), shape(sq),
                   shape(vec), shape(sq), shape(vec), shape(vec)],
        scratch_shapes=[pltpu.VMEM((tm + HALO, D_LRU), F32), pltpu.VMEM((tm + HALO, D_LRU), F32),
                        pltpu.VMEM((tm, D_LRU), F32), pltpu.VMEM((tm, D_LRU), F32),
                        pltpu.VMEM((SUBLANES, D_LRU), F32), pltpu.VMEM((SUBLANES, D_LRU), F32)],
        compiler_params=_params(("arbitrary",)),
    )(proj, proj, proj, h, h, dout, cw, cb, wx, bx, wa, ba, ap)


def _assemble_dproj(dq, dk, dv, du, dxc, dgate, cos, sin_s, cw):
    t = dq.shape[0]
    tm = _row_tile(t, TM)
    nt = t // tm

    def kern(dq_ref, dk_ref, dv_ref, du_ref, dx_ref, dn_ref, dg_ref, c_ref, s_ref, cw_ref, o_ref, b_ref, ext):
        i = pl.program_id(0)

        @pl.when(i == 0)
        def _():
            b_ref[...] = jnp.zeros_like(b_ref)

        def put(lo, val):
            hi = lo + val.shape[1]
            o_ref[:, lo:hi] = val.astype(BF16)
            b_ref[:, lo:hi] += jnp.sum(val, axis=0, keepdims=True)

        c = c_ref[...]
        s = s_ref[...]
        for ch in range(4):
            x = dq_ref[:, ch * 128:(ch + 1) * 128] * (HEAD_DIM ** -0.5)
            put(ch * 128, x * c - _rope_swap(x) * s)
        x = dk_ref[...]
        put(512, x * c - _rope_swap(x) * s)
        put(640, dv_ref[...])
        put(768, du_ref[...])
        ext[0:tm, :] = dx_ref[...]
        ext[tm:, :] = jnp.where(i < nt - 1, dn_ref[...], 0.0)
        put(1024, sum(cw_ref[k:k + 1, :] * ext[pl.ds(LRU_CONV - 1 - k, tm), :] for k in range(LRU_CONV)))
        put(1280, dg_ref[...])

    row = lambda w: pl.BlockSpec((tm, w), lambda i: (i, 0))
    nxt = pl.BlockSpec((HALO, D_LRU), lambda i: (jnp.minimum((i + 1) * (tm // HALO), t // HALO - 1), 0))
    return pl.pallas_call(
        kern, name="assemble_dproj", grid=(nt,),
        in_specs=[row(512), row(128), row(128), row(256), row(256), nxt, row(256), row(128), row(128),
                  pl.BlockSpec((LRU_CONV, D_LRU), lambda i: (0, 0))],
        out_specs=[row(D_IN), pl.BlockSpec((1, D_IN), lambda i: (0, 0))],
        out_shape=[jax.ShapeDtypeStruct((t, D_IN), BF16), jax.ShapeDtypeStruct((1, D_IN), F32)],
        scratch_shapes=[pltpu.VMEM((tm + HALO, D_LRU), F32)],
        compiler_params=_params(("arbitrary",)),
    )(dq, dk, dv, du, dxc, dxc, dgate, cos, sin_s, cw)


def _blockdiag_s5(bbar_re, bbar_im, c_re, c_im):
    eye = jnp.eye(S5_GROUPS, dtype=F32)
    b_of = lambda m: jnp.einsum('gpc,gh->gchp', m, eye).reshape(D_S5, N_STATE)
    c_of = lambda m: jnp.einsum('gcp,gh->gphc', m, eye).reshape(N_STATE, D_S5)
    bmat = jnp.concatenate([b_of(bbar_re), b_of(bbar_im)], axis=1)
    cmat = jnp.concatenate([c_of(c_re), -c_of(c_im)], axis=0)
    return bmat, cmat


def _s5_prepare(a_re, a_im, b_re, b_im, c_re, c_im, log_dt):
    lam_re = jnp.minimum(a_re, -1e-4)
    lam_im = a_im
    dt = jnp.exp(log_dt)[:, None]
    decay = jnp.exp(dt * lam_re)
    ang = dt * lam_im
    abar_re = decay * jnp.cos(ang)
    abar_im = decay * jnp.sin(ang)
    den = jnp.square(lam_re) + jnp.square(lam_im)
    nr = abar_re - 1.0
    ni = abar_im
    coef_re = (nr * lam_re + ni * lam_im) / den
    coef_im = (ni * lam_re - nr * lam_im) / den
    bbar_re = coef_re[..., None] * b_re - coef_im[..., None] * b_im
    bbar_im = coef_re[..., None] * b_im + coef_im[..., None] * b_re
    bmat, cmat = _blockdiag_s5(bbar_re, bbar_im, c_re, c_im)
    return abar_re.reshape(N_STATE), abar_im.reshape(N_STATE), bmat, cmat


def _blockdiag_lru(w):
    eye = jnp.eye(LRU_HEADS, dtype=F32)
    return jnp.einsum('hij,hk->hikj', w, eye).reshape(D_LRU, D_LRU)


def _rope_tables(t):
    inv_freq = ROPE_THETA ** (-jnp.arange(0, HEAD_DIM, 2, dtype=F32) / HEAD_DIM)
    ang = jnp.arange(t, dtype=F32)[:, None] * inv_freq[None, :]
    cos, sin = jnp.cos(ang), jnp.sin(ang)
    return jnp.tile(jnp.concatenate([cos, cos], axis=1), (1, 2)), jnp.tile(jnp.concatenate([-sin, sin], axis=1), (1, 2))


def _vec(v):
    return v.reshape(1, -1)


def _layer_weights(p):
    abar_re, abar_im, bmat, cmat = _s5_prepare(p['s5_a_re'], p['s5_a_im'], p['s5_b_re'], p['s5_b_im'],
                                               p['s5_c_re'], p['s5_c_im'], p['s5_log_dt'])
    return dict(
        coef_f=_s5_coefs(abar_re, abar_im, False), coef_b=_s5_coefs(abar_re, abar_im, True),
        bmat=bmat.astype(BF16), cmat=cmat.astype(BF16),
        wx=_blockdiag_lru(p['lru_wx']).astype(BF16), wa=_blockdiag_lru(p['lru_wa']).astype(BF16),
        gw=p['s5_glu_w'].astype(BF16))


def _layer_fwd(x, xb, p, w, cos, sin_s):
    t = x.shape[0]
    tm = _row_tile(t, TM)
    qkv, uxg = _in_proj(xb, p['w_in'], _vec(p['b_in']), cos, sin_s)
    ya, lse = _attn_fwd(qkv, _vec(p['attn_sinks']))
    h5, ys = _s5_fwd(uxg, w['bmat'], w['coef_f'], w['cmat'], _vec(p['s5_d']), w['gw'], _vec(p['s5_glu_b']))
    lru_w = (p['lru_conv_w'], _vec(p['lru_conv_b']), w['wx'], _vec(p['lru_bx']), w['wa'], _vec(p['lru_ba']),
             _vec(p['lru_a_param']))
    yl, hl = _lru_fwd(uxg, *lru_w)
    mix = _rms_fwd(ya, ys, yl, _vec(p['mix_norm_g']))
    x1, x1b, xhat1, rstd1 = _matmul_ln(
        "out_proj_ln", mix, p['w_out'], _vec(p['b_out']), x, _vec(p['ln1_g']), _vec(p['ln1_b']),
        a_blk=(tm, D_MODEL), a_map=lambda i: (i, 0), w_blk=(D_MODEL, D_MODEL), parts=None)
    gpre, up, hmid = _ffn_hidden_fwd(x1b, p['ffn_w_gate'], p['ffn_w_up'], p['ffn_conv_w'], p['ffn_conv_b'])
    x2, x2b, xhat2, rstd2 = _matmul_ln(
        "ffn_down_ln", hmid, p['ffn_w_down'], jnp.zeros((1, D_MODEL), F32), x1, _vec(p['ln2_g']), _vec(p['ln2_b']),
        a_blk=(N_CHIPS, tm, FF_SH), a_map=lambda i: (0, i, 0), w_blk=(N_CHIPS, FF_SH, D_MODEL), parts=N_CHIPS)
    saved = dict(xb=xb, uxg=uxg, qkv=qkv, ya=ya, lse=lse, h5=h5, ys=ys, yl=yl, hl=hl, mix=mix, x1b=x1b, xhat1=xhat1,
                 rstd1=rstd1, gpre=gpre, up=up, xhat2=xhat2, rstd2=rstd2, lru_w=lru_w)
    return x2, x2b, saved


def _layer_bwd(dx2, s, p, w, cos, sin_s):
    t = dx2.shape[0]
    tk = _row_tile(t, TMM)
    nk = t // tk
    g = {}
    dr2, dr2b, g['ln2_g'], g['ln2_b'], _ = _ln_bwd(dx2, s['xhat2'], s['rstd2'], _vec(p['ln2_g']))
    hmid, dup, dgpre, g['ffn_conv_w'], g['ffn_conv_b'] = _ffn_hidden_bwd(
        dr2b, s['gpre'], s['up'], p['ffn_w_down'], p['ffn_conv_w'], p['ffn_conv_b'])
    g['ffn_w_down'] = _matmul("d_w_down", hmid, dr2b, a_blk=(None, tk, FF_SH), a_map=lambda i, j, k: (i, k, 0),
                              b_blk=(tk, D_MODEL), b_map=lambda i, j, k: (k, 0), out_shape=(N_CHIPS, FF_SH, D_MODEL),
                              o_blk=(None, FF_SH, D_MODEL), o_map=lambda i, j: (i, 0, 0), grid=(N_CHIPS, 1, nk), dims=TN)
    d_ffn_w = lambda name, dact: _matmul(
        name, s['x1b'], dact, a_blk=(tk, D_MODEL), a_map=lambda i, j, k: (k, 0), b_blk=(None, tk, FF_SH),
        b_map=lambda i, j, k: (j, k, 0), out_shape=(N_CHIPS, D_MODEL, FF_SH), o_blk=(None, D_MODEL, FF_SH),
        o_map=lambda i, j: (j, 0, 0), grid=(1, N_CHIPS, nk), dims=TN)
    g['ffn_w_gate'] = d_ffn_w("d_w_gate", dgpre)
    g['ffn_w_up'] = d_ffn_w("d_w_up", dup)
    dx1 = _matmul("d_x1", dgpre, p['ffn_w_gate'], pair2=(dup, p['ffn_w_up']), a_blk=(None, tk, FF_SH),
                  a_map=lambda i, j, k: (k, i, 0), b_blk=(None, D_MODEL, FF_SH), b_map=lambda i, j, k: (k, 0, 0),
                  out_shape=(t, D_MODEL), o_blk=(tk, D_MODEL), o_map=lambda i, j: (i, 0), grid=(nk, 1, N_CHIPS),
                  dims=NT, add=dr2, add_scale=ALPHA)
    dr1, dr1b, g['ln1_g'], g['ln1_b'], g['b_out'] = _ln_bwd(dx1, s['xhat1'], s['rstd1'], _vec(p['ln1_g']))
    g['w_out'] = _matmul("d_w_out", s['mix'], dr1b, a_blk=(tk, D_MODEL), a_map=lambda i, j, k: (k, 0),
                         b_blk=(tk, D_MODEL), b_map=lambda i, j, k: (k, 0), out_shape=(D_MODEL, D_MODEL),
                         o_blk=(D_MODEL, D_MODEL), o_map=lambda i, j: (0, 0), grid=(1, 1, nk), dims=TN)
    dmix = _matmul("d_mix", dr1b, p['w_out'], a_blk=(tk, D_MODEL), a_map=lambda i, j, k: (i, 0),
                   b_blk=(D_MODEL, D_MODEL), b_map=lambda i, j, k: (0, 0), out_shape=(t, D_MODEL),
                   o_blk=(tk, D_MODEL), o_map=lambda i, j: (i, 0), grid=(nk, 1, 1), dims=NT)
    dya, dys, dyl, g['mix_norm_g'] = _rms_bwd(dmix, s['ya'], s['ys'], s['yl'], _vec(p['mix_norm_g']))
    dq, dk, dv, g['attn_sinks'] = _attn_bwd(s['qkv'], s['ya'], dya, s['lse'], _vec(p['attn_sinks']))
    du, dcmat, dbmat, dabar, g['s5_d'], g['s5_glu_w'], g['s5_glu_b'] = _s5_bwd(
        s['uxg'], s['h5'], dys, w['bmat'], w['coef_b'], w['cmat'], _vec(p['s5_d']), w['gw'], _vec(p['s5_glu_b']))
    (dxc, dgate, g['lru_conv_w'], g['lru_conv_b'], dwx, g['lru_bx'], dwa, g['lru_ba'],
     g['lru_a_param']) = _lru_bwd(s['uxg'], s['hl'], dyl, *s['lru_w'])
    dproj, g['b_in'] = _assemble_dproj(dq, dk, dv, du, dxc, dgate, cos, sin_s, p['lru_conv_w'])
    g['w_in'] = _matmul("d_w_in", s['xb'], dproj, a_blk=(tk, D_MODEL), a_map=lambda i, j, k: (k, 0),
                        b_blk=(tk, IN_SH), b_map=lambda i, j, k: (k, j), out_shape=(N_CHIPS, D_MODEL, IN_SH),
                        o_blk=(None, D_MODEL, IN_SH), o_map=lambda i, j: (j, 0, 0), grid=(1, N_CHIPS, nk), dims=TN)
    dx = _matmul("d_x", dproj, p['w_in'], a_blk=(tk, IN_SH), a_map=lambda i, j, k: (i, k),
                 b_blk=(None, D_MODEL, IN_SH), b_map=lambda i, j, k: (k, 0, 0), out_shape=(t, D_MODEL),
                 o_blk=(tk, D_MODEL), o_map=lambda i, j: (i, 0), grid=(nk, 1, N_CHIPS), dims=NT,
                 add=dr1, add_scale=ALPHA)
    return dx, _param_chain(g, p, dabar, dbmat, dcmat, dwx, dwa)


def _layer_fwd_v1(x, p, w, cos, sin_s):
    t = x.shape[0]
    nt = t // _row_tile(t, TM)
    tm = t // nt
    proj = _matmul("in_proj", x, p['w_in'], a_blk=(tm, D_MODEL), a_map=lambda i, j, k: (i, 0),
                   b_blk=(None, D_MODEL, IN_SH), b_map=lambda i, j, k: (j, 0, 0), out_shape=(t, D_IN),
                   o_blk=(tm, IN_SH), o_map=lambda i, j: (i, j), grid=(nt, N_CHIPS, 1), dims=NN,
                   bias=_vec(p['b_in']), bias_blk=(1, IN_SH), bias_map=lambda i, j, k: (0, j))
    qkv = _qkv_post(proj, cos, sin_s)
    ya, lse = _attn_fwd(qkv, _vec(p['attn_sinks']))
    h5, ys = _s5_fwd(proj, w['bmat'], w['coef_f'], w['cmat'], _vec(p['s5_d']), w['gw'], _vec(p['s5_glu_b']))
    lru_w = (p['lru_conv_w'], _vec(p['lru_conv_b']), w['wx'], _vec(p['lru_bx']), w['wa'], _vec(p['lru_ba']),
             _vec(p['lru_a_param']))
    yl, hl = _lru_fwd(proj, *lru_w)
    mix = _rms_fwd(ya, ys, yl, _vec(p['mix_norm_g']))
    f1 = _matmul("out_proj", mix, p['w_out'], a_blk=(tm, D_MODEL), a_map=lambda i, j, k: (i, 0),
                 b_blk=(D_MODEL, D_MODEL), b_map=lambda i, j, k: (0, 0), out_shape=(t, D_MODEL),
                 o_blk=(tm, D_MODEL), o_map=lambda i, j: (i, 0), grid=(nt, 1, 1), dims=NN,
                 bias=_vec(p['b_out']), bias_blk=(1, D_MODEL), bias_map=lambda i, j, k: (0, 0))
    x1, xhat1, rstd1 = _ln_fwd(x, f1, _vec(p['ln1_g']), _vec(p['ln1_b']))
    ffn_in = lambda name, wmat: _matmul(
        name, x1, wmat, a_blk=(tm, D_MODEL), a_map=lambda i, j, k: (i, 0), b_blk=(None, D_MODEL, FF_SH),
        b_map=lambda i, j, k: (j, 0, 0), out_shape=(N_CHIPS, t, FF_SH), o_blk=(None, tm, FF_SH),
        o_map=lambda i, j: (j, i, 0), grid=(nt, N_CHIPS, 1), dims=NN)
    gpre = ffn_in("ffn_gate", p['ffn_w_gate'])
    up = ffn_in("ffn_up", p['ffn_w_up'])
    hmid = _ffn_mid_fwd(gpre, up, p['ffn_conv_w'], p['ffn_conv_b'])
    f2 = _matmul("ffn_down", hmid, p['ffn_w_down'], a_blk=(None, tm, FF_SH), a_map=lambda i, j, k: (k, i, 0),
                 b_blk=(None, FF_SH, D_MODEL), b_map=lambda i, j, k: (k, 0, 0), out_shape=(t, D_MODEL),
                 o_blk=(tm, D_MODEL), o_map=lambda i, j: (i, 0), grid=(nt, 1, N_CHIPS), dims=NN)
    x2, xhat2, rstd2 = _ln_fwd(x1, f2, _vec(p['ln2_g']), _vec(p['ln2_b']))
    saved = dict(x=x, proj=proj, qkv=qkv, ya=ya, lse=lse, h5=h5, ys=ys, yl=yl, hl=hl, mix=mix, x1=x1, xhat1=xhat1,
                 rstd1=rstd1, gpre=gpre, up=up, xhat2=xhat2, rstd2=rstd2, lru_w=lru_w)
    return x2, saved


def _param_chain(g, p, dabar, dbmat, dcmat, dwx, dwa):
    s5_names = ('s5_a_re', 's5_a_im', 's5_b_re', 's5_b_im', 's5_c_re', 's5_c_im', 's5_log_dt')
    _, s5_vjp = jax.vjp(_s5_prepare, *[p[n] for n in s5_names])
    for n, val in zip(s5_names, s5_vjp((dabar[0], dabar[1], dbmat, dcmat))):
        g[n] = val
    g['lru_wx'] = jax.vjp(_blockdiag_lru, p['lru_wx'])[1](dwx)[0]
    g['lru_wa'] = jax.vjp(_blockdiag_lru, p['lru_wa'])[1](dwa)[0]
    return g


def _layer_bwd_v1(dx2, s, p, w, cos, sin_s):
    t = dx2.shape[0]
    nt = t // _row_tile(t, TM)
    tm = t // nt
    g = {}
    dr2, g['ln2_g'], g['ln2_b'], _ = _ln_bwd(dx2, s['xhat2'], s['rstd2'], _vec(p['ln2_g']))
    dhmid = _matmul("d_hmid", dr2, p['ffn_w_down'], a_blk=(tm, D_MODEL), a_map=lambda i, j, k: (i, 0),
                    b_blk=(None, FF_SH, D_MODEL), b_map=lambda i, j, k: (j, 0, 0), out_shape=(N_CHIPS, t, FF_SH),
                    o_blk=(None, tm, FF_SH), o_map=lambda i, j: (j, i, 0), grid=(nt, N_CHIPS, 1), dims=NT)
    hmid, dup, dgc, g['ffn_conv_w'], g['ffn_conv_b'] = _ffn_mid_bwd(s['gpre'], s['up'], dhmid, p['ffn_conv_w'],
                                                                    p['ffn_conv_b'])
    g['ffn_w_down'] = _matmul("d_w_down", hmid, dr2, a_blk=(None, tm, FF_SH), a_map=lambda i, j, k: (i, k, 0),
                              b_blk=(tm, D_MODEL), b_map=lambda i, j, k: (k, 0), out_shape=(N_CHIPS, FF_SH, D_MODEL),
                              o_blk=(None, FF_SH, D_MODEL), o_map=lambda i, j: (i, 0, 0), grid=(N_CHIPS, 1, nt), dims=TN)
    dgpre = _ffn_conv_t(dgc, p['ffn_conv_w'])
    d_ffn_w = lambda name, dact: _matmul(
        name, s['x1'], dact, a_blk=(tm, D_MODEL), a_map=lambda i, j, k: (k, 0), b_blk=(None, tm, FF_SH),
        b_map=lambda i, j, k: (j, k, 0), out_shape=(N_CHIPS, D_MODEL, FF_SH), o_blk=(None, D_MODEL, FF_SH),
        o_map=lambda i, j: (j, 0, 0), grid=(1, N_CHIPS, nt), dims=TN)
    g['ffn_w_gate'] = d_ffn_w("d_w_gate", dgpre)
    g['ffn_w_up'] = d_ffn_w("d_w_up", dup)
    d_ffn_x = lambda name, dact, wmat, add, scale: _matmul(
        name, dact, wmat, a_blk=(None, tm, FF_SH), a_map=lambda i, j, k: (k, i, 0), b_blk=(None, D_MODEL, FF_SH),
        b_map=lambda i, j, k: (k, 0, 0), out_shape=(t, D_MODEL), o_blk=(tm, D_MODEL), o_map=lambda i, j: (i, 0),
        grid=(nt, 1, N_CHIPS), dims=NT, add=add, add_scale=scale)
    dx1 = d_ffn_x("d_x1_gate", dgpre, p['ffn_w_gate'], dr2, ALPHA)
    dx1 = d_ffn_x("d_x1_up", dup, p['ffn_w_up'], dx1, 1.0)
    dr1, g['ln1_g'], g['ln1_b'], g['b_out'] = _ln_bwd(dx1, s['xhat1'], s['rstd1'], _vec(p['ln1_g']))
    g['w_out'] = _matmul("d_w_out", s['mix'], dr1, a_blk=(tm, D_MODEL), a_map=lambda i, j, k: (k, 0),
                         b_blk=(tm, D_MODEL), b_map=lambda i, j, k: (k, 0), out_shape=(D_MODEL, D_MODEL),
                         o_blk=(D_MODEL, D_MODEL), o_map=lambda i, j: (0, 0), grid=(1, 1, nt), dims=TN)
    dmix = _matmul("d_mix", dr1, p['w_out'], a_blk=(tm, D_MODEL), a_map=lambda i, j, k: (i, 0),
                   b_blk=(D_MODEL, D_MODEL), b_map=lambda i, j, k: (0, 0), out_shape=(t, D_MODEL),
                   o_blk=(tm, D_MODEL), o_map=lambda i, j: (i, 0), grid=(nt, 1, 1), dims=NT)
    dya, dys, dyl, g['mix_norm_g'] = _rms_bwd(dmix, s['ya'], s['ys'], s['yl'], _vec(p['mix_norm_g']))
    dq, dk, dv, g['attn_sinks'] = _attn_bwd(s['qkv'], s['ya'], dya, s['lse'], _vec(p['attn_sinks']))
    du, dcmat, dbmat, dabar, g['s5_d'], g['s5_glu_w'], g['s5_glu_b'] = _s5_bwd(
        s['proj'], s['h5'], dys, w['bmat'], w['coef_b'], w['cmat'], _vec(p['s5_d']), w['gw'], _vec(p['s5_glu_b']))
    (dxc, dgate, g['lru_conv_w'], g['lru_conv_b'], dwx, g['lru_bx'], dwa, g['lru_ba'],
     g['lru_a_param']) = _lru_bwd(s['proj'], s['hl'], dyl, *s['lru_w'])
    dproj, g['b_in'] = _assemble_dproj(dq, dk, dv, du, dxc, dgate, cos, sin_s, p['lru_conv_w'])
    g['w_in'] = _matmul("d_w_in", s['x'], dproj, a_blk=(tm, D_MODEL), a_map=lambda i, j, k: (k, 0),
                        b_blk=(tm, IN_SH), b_map=lambda i, j, k: (k, j), out_shape=(N_CHIPS, D_MODEL, IN_SH),
                        o_blk=(None, D_MODEL, IN_SH), o_map=lambda i, j: (j, 0, 0), grid=(1, N_CHIPS, nt), dims=TN)
    dx = _matmul("d_x", dproj, p['w_in'], a_blk=(tm, IN_SH), a_map=lambda i, j, k: (i, k),
                 b_blk=(None, D_MODEL, IN_SH), b_map=lambda i, j, k: (k, 0, 0), out_shape=(t, D_MODEL),
                 o_blk=(tm, D_MODEL), o_map=lambda i, j: (i, 0), grid=(nt, 1, N_CHIPS), dims=NT,
                 add=dr1, add_scale=ALPHA)
    s5_names = ('s5_a_re', 's5_a_im', 's5_b_re', 's5_b_im', 's5_c_re', 's5_c_im', 's5_log_dt')
    _, s5_vjp = jax.vjp(_s5_prepare, *[p[n] for n in s5_names])
    for n, val in zip(s5_names, s5_vjp((dabar[0], dabar[1], dbmat, dcmat))):
        g[n] = val
    g['lru_wx'] = jax.vjp(_blockdiag_lru, p['lru_wx'])[1](dwx)[0]
    g['lru_wa'] = jax.vjp(_blockdiag_lru, p['lru_wa'])[1](dwa)[0]
    return dx, g


ROW_TILE = 512


def _pick_rows(rows):
    for rt in range(min(rows, ROW_TILE), 0, -1):
        if rows % rt == 0 and (rt % 16 == 0 or rt == rows):
            return rt
    return rows


def _cast_bf16(a):
    a2 = a.reshape(-1, a.shape[-1])
    rows, c = a2.shape
    rt = _pick_rows(rows)

    def kern(a_ref, o_ref):
        o_ref[...] = a_ref[...].astype(BF16)

    spec = pl.BlockSpec((rt, c), lambda i: (i, 0))
    out = pl.pallas_call(kern, name="cast_bf16", grid=(rows // rt,), in_specs=[spec], out_specs=spec,
                         out_shape=jax.ShapeDtypeStruct((rows, c), BF16), compiler_params=_params(("parallel",)))(a2)
    return out.reshape(a.shape)


def _sum_parts(name, parts, shape):
    c = shape[-1]
    rows = math.prod(shape[:-1])
    rt = _pick_rows(rows)
    n = len(parts)

    def kern(*refs):
        acc = refs[0][...]
        for r in refs[1:n]:
            acc = acc + r[...]
        refs[n][...] = acc

    specs, args = [], []
    for arr, j in parts:
        if j is None:
            specs.append(pl.BlockSpec((rt, c), lambda i: (i, 0)))
            args.append(arr.reshape(rows, c))
        else:
            specs.append(pl.BlockSpec((None, rt, c), functools.partial(lambda i, jj: (jj, i, 0), jj=j)))
            args.append(arr.reshape(arr.shape[0], rows, c))
    out = pl.pallas_call(kern, name=name, grid=(rows // rt,), in_specs=specs,
                         out_specs=pl.BlockSpec((rt, c), lambda i: (i, 0)),
                         out_shape=jax.ShapeDtypeStruct((rows, c), F32), compiler_params=_params(("parallel",)))(*args)
    return out.reshape(shape)


def _adamw(name, w, g, m, v):
    shape = w.shape
    c = shape[-1]
    rows = math.prod(shape[:-1])
    rt = _pick_rows(rows)

    def kern(w_ref, g_ref, m_ref, v_ref, d_ref, nm_ref, nv_ref):
        g_ = g_ref[...]
        m_ = ADAM_B1 * m_ref[...] + (1.0 - ADAM_B1) * g_
        v_ = ADAM_B2 * v_ref[...] + (1.0 - ADAM_B2) * jnp.square(g_)
        m_hat = m_ / (1.0 - ADAM_B1 ** ADAM_STEP)
        v_hat = v_ / (1.0 - ADAM_B2 ** ADAM_STEP)
        d_ref[...] = -ADAM_LR * (m_hat / (jnp.sqrt(v_hat) + ADAM_EPS) + ADAM_WD * w_ref[...])
        nm_ref[...] = m_
        nv_ref[...] = v_

    spec = pl.BlockSpec((rt, c), lambda i: (i, 0))
    outs = pl.pallas_call(kern, name=name, grid=(rows // rt,), in_specs=[spec] * 4, out_specs=[spec] * 3,
                          out_shape=[jax.ShapeDtypeStruct((rows, c), F32)] * 3,
                          compiler_params=_params(("parallel",)))(*[a.reshape(rows, c) for a in (w, g, m, v)])
    return tuple(o.reshape(shape) for o in outs)


def _position():
    return lax.axis_index("x"), lax.axis_index("y"), lax.axis_index("c")


def _other_chips(x, y):
    return [(1 - x, y), (x, 1 - y), (1 - x, 1 - y)]


def _comm_call(name, kern, arrs, out_shapes, n_remote, n_local):
    return pl.pallas_call(
        kern, name=name, in_specs=[ANY] * len(arrs), out_specs=[ANY] * len(out_shapes), out_shape=out_shapes,
        scratch_shapes=[pltpu.SemaphoreType.DMA((n_remote,)), pltpu.SemaphoreType.DMA((n_remote,)),
                        pltpu.SemaphoreType.DMA((n_local,))],
    )(*arrs)


def _allgather_chips(arrs):
    n = len(arrs)

    def kern(*refs):
        ins, outs = refs[:n], refs[n:2 * n]
        send, recv, loc = refs[2 * n:]
        x, y, c = _position()
        me = 2 * x + y
        chips = _other_chips(x, y)
        own, sent = [], []
        for t in range(n):
            own.append(pltpu.make_async_copy(ins[t], outs[t].at[:, pl.ds(me, 1)], loc.at[t]))
            own[-1].start()
            for j, (px, py) in enumerate(chips):
                sent.append(pltpu.make_async_remote_copy(
                    src_ref=ins[t], dst_ref=outs[t].at[:, pl.ds(me, 1)], send_sem=send.at[3 * t + j],
                    recv_sem=recv.at[3 * t + j], device_id=(px, py, c), device_id_type=MESH))
                sent[-1].start()
        for t in range(n):
            for j, (px, py) in enumerate(chips):
                pltpu.make_async_remote_copy(
                    src_ref=ins[t], dst_ref=outs[t].at[:, pl.ds(2 * px + py, 1)], send_sem=send.at[3 * t + j],
                    recv_sem=recv.at[3 * t + j], device_id=(px, py, c), device_id_type=MESH).wait_recv()
        for cp in sent:
            cp.wait_send()
        for cp in own:
            cp.wait()

    outs = [jax.ShapeDtypeStruct((a.shape[0], N_CHIPS) + a.shape[2:], a.dtype) for a in arrs]
    return _comm_call("allgather_chips", kern, arrs, outs, 3 * n, n)


def _pair_exchange(arrs):
    n = len(arrs)

    def kern(*refs):
        ins, outs = refs[:n], refs[n:3 * n]
        send, recv, loc = refs[3 * n:]
        x, y, c = _position()
        own, sent = [], []
        for t in range(n):
            r2 = ins[t].shape[2] // 2
            own.append(pltpu.make_async_copy(ins[t].at[:, :, pl.ds(c * r2, r2)], outs[2 * t], loc.at[t]))
            own[-1].start()
            sent.append(pltpu.make_async_remote_copy(
                src_ref=ins[t].at[:, :, pl.ds((1 - c) * r2, r2)], dst_ref=outs[2 * t + 1], send_sem=send.at[t],
                recv_sem=recv.at[t], device_id=(x, y, 1 - c), device_id_type=MESH))
            sent[-1].start()
        for cp in sent:
            cp.wait()
        for cp in own:
            cp.wait()

    outs = []
    for a in arrs:
        half = jax.ShapeDtypeStruct(a.shape[:2] + (a.shape[2] // 2, a.shape[3]), a.dtype)
        outs += [half, half]
    return _comm_call("pair_exchange", kern, arrs, outs, n, n)


def _chip_scatter(arrs):
    n = len(arrs)

    def kern(*refs):
        ins, outs = refs[:n], refs[n:3 * n]
        send, recv, loc = refs[3 * n:]
        x, y, c = _position()
        me = 2 * x + y
        chips = _other_chips(x, y)
        own, sent = [], []
        for t in range(n):
            own.append(pltpu.make_async_copy(ins[t].at[:, pl.ds(me, 1)], outs[2 * t], loc.at[t]))
            own[-1].start()
            for j, (px, py) in enumerate(chips):
                sent.append(pltpu.make_async_remote_copy(
                    src_ref=ins[t].at[:, pl.ds(2 * px + py, 1)], dst_ref=outs[2 * t + 1].at[j],
                    send_sem=send.at[3 * t + j], recv_sem=recv.at[3 * t + j], device_id=(px, py, c),
                    device_id_type=MESH))
                sent[-1].start()
        for cp in sent:
            cp.wait()
        for cp in own:
            cp.wait()

    outs = []
    for a in arrs:
        one = (a.shape[0], 1) + a.shape[2:]
        outs += [jax.ShapeDtypeStruct(one, a.dtype), jax.ShapeDtypeStruct((3,) + one, a.dtype)]
    return _comm_call("chip_scatter", kern, arrs, outs, 3 * n, n)


def _pair_gather(arrs):
    n = len(arrs)

    def kern(*refs):
        ins, outs = refs[:n], refs[n:2 * n]
        send, recv, loc = refs[2 * n:]
        x, y, c = _position()
        own, sent = [], []
        for t in range(n):
            own.append(pltpu.make_async_copy(ins[t], outs[t].at[:, pl.ds(c, 1)], loc.at[t]))
            own[-1].start()
            sent.append(pltpu.make_async_remote_copy(
                src_ref=ins[t], dst_ref=outs[t].at[:, pl.ds(c, 1)], send_sem=send.at[t], recv_sem=recv.at[t],
                device_id=(x, y, 1 - c), device_id_type=MESH))
            sent[-1].start()
        for t in range(n):
            sent[t].wait_send()
            pltpu.make_async_remote_copy(
                src_ref=ins[t], dst_ref=outs[t].at[:, pl.ds(1 - c, 1)], send_sem=send.at[t], recv_sem=recv.at[t],
                device_id=(x, y, 1 - c), device_id_type=MESH).wait_recv()
        for cp in own:
            cp.wait()

    outs = [jax.ShapeDtypeStruct((a.shape[0], 2) + a.shape[2:], a.dtype) for a in arrs]
    return _comm_call("pair_gather", kern, arrs, outs, n, n)


_FLIPS = [(0, 0, 1), (1, 0, 0), (0, 1, 0), (1, 1, 0), (1, 0, 1), (0, 1, 1), (1, 1, 1)]


def _allgather_devices(v):
    def kern(v_ref, o_ref, send, recv, loc):
        x, y, c = _position()
        me = 4 * x + 2 * y + c
        peers = [((1 - x) if fx else x, (1 - y) if fy else y, (1 - c) if fc else c) for fx, fy, fc in _FLIPS]
        own = pltpu.make_async_copy(v_ref, o_ref.at[pl.ds(me, 1)], loc.at[0])
        own.start()
        sent = []
        for k, peer in enumerate(peers):
            sent.append(pltpu.make_async_remote_copy(
                src_ref=v_ref, dst_ref=o_ref.at[pl.ds(me, 1)], send_sem=send.at[k], recv_sem=recv.at[k],
                device_id=peer, device_id_type=MESH))
            sent[-1].start()
        for k, (px, py, pc) in enumerate(peers):
            pltpu.make_async_remote_copy(
                src_ref=v_ref, dst_ref=o_ref.at[pl.ds(4 * px + 2 * py + pc, 1)], send_sem=send.at[k],
                recv_sem=recv.at[k], device_id=(px, py, pc), device_id_type=MESH).wait_recv()
        for cp in sent:
            cp.wait_send()
        own.wait()

    out = jax.ShapeDtypeStruct((N_DEV,) + v.shape[1:], v.dtype)
    return _comm_call("allgather_devices", kern, [v], [out], len(_FLIPS), 1)[0]


def _exchange(name, arrs, out_shapes, n_local, n_remote, plan):
    n_in, n_out = len(arrs), len(out_shapes)

    def kern(*refs):
        ins, outs = refs[:n_in], refs[n_in:n_in + n_out]
        send, recv, loc = refs[n_in + n_out:]
        local, remote = plan(ins, outs, *_position())
        assert len(local) == n_local and len(remote) == n_remote
        own = [pltpu.make_async_copy(s, d, loc.at[k]) for k, (s, d) in enumerate(local)]
        for cp in own:
            cp.start()
        sent = [pltpu.make_async_remote_copy(src_ref=s, dst_ref=d, send_sem=send.at[k], recv_sem=recv.at[k],
                                             device_id=peer, device_id_type=MESH)
                for k, (s, d, peer, _) in enumerate(remote)]
        for cp in sent:
            cp.start()
        for k, (s, _, peer, landing) in enumerate(remote):
            pltpu.make_async_remote_copy(src_ref=s, dst_ref=landing, send_sem=send.at[k], recv_sem=recv.at[k],
                                         device_id=peer, device_id_type=MESH).wait_recv()
        for cp in sent:
            cp.wait_send()
        for cp in own:
            cp.wait()

    return pl.pallas_call(
        kern, name=name, in_specs=[ANY] * n_in, out_specs=[ANY] * n_out, out_shape=out_shapes,
        scratch_shapes=[pltpu.SemaphoreType.DMA((n_remote,)), pltpu.SemaphoreType.DMA((n_remote,)),
                        pltpu.SemaphoreType.DMA((max(n_local, 1),))],
    )(*arrs)


def _allgather_chips(arrs):
    n = len(arrs)
    layers = arrs[0].shape[0]

    def plan(ins, outs, x, y, c):
        me = 2 * x + y
        local, remote = [], []
        for t in range(n):
            for l in range(layers):
                local.append((ins[t].at[l], outs[t].at[l, pl.ds(me, 1)]))
                for px, py in _other_chips(x, y):
                    remote.append((ins[t].at[l], outs[t].at[l, pl.ds(me, 1)], (px, py, c),
                                   outs[t].at[l, pl.ds(2 * px + py, 1)]))
        return local, remote

    outs = [jax.ShapeDtypeStruct((a.shape[0], N_CHIPS) + a.shape[2:], a.dtype) for a in arrs]
    return _exchange("allgather_chips", arrs, outs, n * layers, 3 * n * layers, plan)


def _pair_exchange(arrs):
    n = len(arrs)
    layers, shards = arrs[0].shape[:2]

    def plan(ins, outs, x, y, c):
        local, remote = [], []
        for t in range(n):
            r2 = ins[t].shape[2] // 2
            for l in range(layers):
                for s in range(shards):
                    local.append((ins[t].at[l, s, pl.ds(c * r2, r2)], outs[2 * t].at[l, s]))
                    remote.append((ins[t].at[l, s, pl.ds((1 - c) * r2, r2)], outs[2 * t + 1].at[l, s],
                                   (x, y, 1 - c), outs[2 * t + 1].at[l, s]))
        return local, remote

    outs = []
    for a in arrs:
        half = jax.ShapeDtypeStruct(a.shape[:2] + (a.shape[2] // 2, a.shape[3]), a.dtype)
        outs += [half, half]
    return _exchange("pair_exchange", arrs, outs, n * layers * shards, n * layers * shards, plan)


def _chip_scatter(arrs):
    n = len(arrs)
    layers = arrs[0].shape[0]

    def plan(ins, outs, x, y, c):
        me = 2 * x + y
        local, remote = [], []
        for t in range(n):
            for l in range(layers):
                local.append((ins[t].at[l, pl.ds(me, 1)], outs[2 * t].at[l]))
                for j, (px, py) in enumerate(_other_chips(x, y)):
                    remote.append((ins[t].at[l, pl.ds(2 * px + py, 1)], outs[2 * t + 1].at[j, l], (px, py, c),
                                   outs[2 * t + 1].at[j, l]))
        return local, remote

    outs = []
    for a in arrs:
        one = (a.shape[0], 1) + a.shape[2:]
        outs += [jax.ShapeDtypeStruct(one, a.dtype), jax.ShapeDtypeStruct((3,) + one, a.dtype)]
    return _exchange("chip_scatter", arrs, outs, n * layers, 3 * n * layers, plan)


def _pair_gather(arrs):
    n = len(arrs)
    layers = arrs[0].shape[0]

    def plan(ins, outs, x, y, c):
        local, remote = [], []
        for t in range(n):
            for l in range(layers):
                local.append((ins[t].at[l], outs[t].at[l, pl.ds(c, 1)]))
                remote.append((ins[t].at[l], outs[t].at[l, pl.ds(c, 1)], (x, y, 1 - c),
                               outs[t].at[l, pl.ds(1 - c, 1)]))
        return local, remote

    outs = [jax.ShapeDtypeStruct((a.shape[0], 2) + a.shape[2:], a.dtype) for a in arrs]
    return _exchange("pair_gather", arrs, outs, n * layers, n * layers, plan)


GATHER_PIECES = 4


def _allgather_devices(v):
    rq = v.shape[1] // GATHER_PIECES

    def plan(ins, outs, x, y, c):
        me = 4 * x + 2 * y + c
        local, remote = [], []
        for q in range(GATHER_PIECES):
            rows = pl.ds(q * rq, rq)
            local.append((ins[0].at[0, rows], outs[0].at[me, rows]))
            for fx, fy, fc in _FLIPS:
                px, py, pc = (1 - x) if fx else x, (1 - y) if fy else y, (1 - c) if fc else c
                remote.append((ins[0].at[0, rows], outs[0].at[me, rows], (px, py, pc),
                               outs[0].at[4 * px + 2 * py + pc, rows]))
        return local, remote

    out = jax.ShapeDtypeStruct((N_DEV,) + v.shape[1:], v.dtype)
    return _exchange("allgather_devices", [v], [out], GATHER_PIECES, GATHER_PIECES * len(_FLIPS), plan)[0]


WEIGHTS = ['w_in', 'b_in', 'attn_sinks', 's5_a_re', 's5_a_im', 's5_b_re', 's5_b_im', 's5_c_re', 's5_c_im', 's5_d',
           's5_log_dt', 's5_glu_w', 's5_glu_b', 'lru_conv_w', 'lru_conv_b', 'lru_wx', 'lru_bx', 'lru_wa', 'lru_ba',
           'lru_a_param', 'mix_norm_g', 'w_out', 'b_out', 'ln1_g', 'ln1_b', 'ffn_w_gate', 'ffn_w_up', 'ffn_conv_w',
           'ffn_conv_b', 'ffn_w_down', 'ln2_g', 'ln2_b']
BIG = ('w_in', 'w_out', 'ffn_w_gate', 'ffn_w_up', 'ffn_w_down')
SMALL = tuple(n for n in WEIGHTS if n not in BIG)
PACK_ROWS = ROW_TILE


def _pack(arrs):
    flat = jnp.concatenate([a.reshape(-1) for a in arrs])
    unit = 128 * PACK_ROWS
    size = -(-flat.shape[0] // unit) * unit
    return jnp.pad(flat, (0, size - flat.shape[0])).reshape(-1, 128)


def _unpack(packed, shapes):
    flat = packed.reshape(-1)
    out, pos = [], 0
    for shp in shapes:
        n = math.prod(shp)
        out.append(flat[pos:pos + n].reshape(shp))
        pos += n
    return out


def _reduce_big(grads):
    mixed = _pair_exchange(grads)
    pair = [_sum_parts("pair_sum", [(mixed[2 * t], None), (mixed[2 * t + 1], None)], mixed[2 * t].shape)
            for t in range(len(grads))]
    scat = _chip_scatter(pair)
    chip = [_sum_parts("chip_sum", [(scat[2 * t], None)] + [(scat[2 * t + 1], j) for j in range(3)],
                       scat[2 * t].shape) for t in range(len(grads))]
    both = _pair_gather(chip)
    return [b.reshape(b.shape[0], 2 * b.shape[2], b.shape[3]) for b in both]


def _step(a):
    x = a['x'][0]
    target = a['loss_target'][0]
    t = x.shape[0]
    xi, yi, _ = _position()
    chip = 2 * xi + yi
    cos, sin_s = _rope_tables(t)

    gathered = _allgather_chips([_cast_bf16(a[n])[:, None] for n in BIG]
                                + [a[n][:, None] for n in ('s5_glu_w', 'lru_conv_w', 'ffn_conv_w')])
    full = dict(zip(BIG + ('s5_glu_w', 'lru_conv_w', 'ffn_conv_w'), gathered))

    def layer_params(l):
        p = {n: a[n][l] for n in SMALL}
        p['w_in'] = full['w_in'][l]
        p['w_out'] = full['w_out'][l].reshape(D_MODEL, D_MODEL)
        p['ffn_w_gate'] = full['ffn_w_gate'][l]
        p['ffn_w_up'] = full['ffn_w_up'][l]
        p['ffn_w_down'] = full['ffn_w_down'][l]
        p['s5_glu_w'] = full['s5_glu_w'][l].reshape(D_S5, D_S5)
        p['lru_conv_w'] = full['lru_conv_w'][l].transpose(1, 0, 2).reshape(LRU_CONV, D_LRU)
        p['ffn_conv_w'] = full['ffn_conv_w'][l]
        p['ffn_conv_b'] = a['ffn_conv_b'][l].reshape(N_CHIPS, 1, FF_SH)
        return p

    params = [layer_params(l) for l in range(DEPTH)]
    derived = [_layer_weights(p) for p in params]
    saved = []
    h, hb = x, _cast_bf16(x)
    for l in range(DEPTH):
        h, hb, s = _layer_fwd(h, hb, params[l], derived[l], cos, sin_s)
        saved.append(s)
    loss_part, dh = _loss_head(h, target)
    loss = lax.psum(loss_part[0, 0], ("x", "y", "c"))
    grads = [None] * DEPTH
    for l in reversed(range(DEPTH)):
        dh, grads[l] = _layer_bwd(dh, saved[l], params[l], derived[l], cos, sin_s)
    grad_x = dh[None]

    def stacked(n):
        return jnp.stack([grads[l][n] for l in range(DEPTH)])

    big_local = [stacked(n) for n in BIG]
    big_local[1] = big_local[1].reshape(DEPTH, N_CHIPS, OUT_SH, D_MODEL)
    grad = dict(zip(BIG, _reduce_big(big_local)))
    small_local = [stacked(n) for n in SMALL]
    packed = _allgather_devices(_pack(small_local)[None])
    total = _sum_parts("device_sum", [(packed, j) for j in range(N_DEV)], packed.shape[1:])
    small_sum = dict(zip(SMALL, _unpack(total, [g.shape for g in small_local])))
    for n in SMALL:
        g = small_sum[n]
        if n == 's5_glu_w':
            g = lax.dynamic_slice_in_dim(g, chip * (D_S5 // N_CHIPS), D_S5 // N_CHIPS, axis=1)
        elif n == 'lru_conv_w':
            g = lax.dynamic_slice_in_dim(g, chip * (D_LRU // N_CHIPS), D_LRU // N_CHIPS, axis=2)
        elif n == 'ffn_conv_w':
            g = lax.dynamic_index_in_dim(g, chip, axis=1, keepdims=False)
        grad[n] = g.reshape(a[n].shape)

    delta, new_m, new_v = {}, {}, {}
    for n in BIG:
        delta[n], new_m[n], new_v[n] = _adamw("adamw_" + n, a[n], grad[n], a['m_' + n], a['v_' + n])
    shapes = [a[n].shape for n in SMALL]
    outs = _adamw("adamw_small", _pack([a[n] for n in SMALL]), _pack([grad[n] for n in SMALL]),
                  _pack([a['m_' + n] for n in SMALL]), _pack([a['v_' + n] for n in SMALL]))
    for res, o in zip((delta, new_m, new_v), outs):
        res.update(zip(SMALL, _unpack(o, shapes)))
    return (loss, grad_x, *[grad[n] for n in WEIGHTS], *[delta[n] for n in WEIGHTS],
            *[new_m[n] for n in WEIGHTS], *[new_v[n] for n in WEIGHTS])


def kernel(x, w_in, b_in, attn_sinks, s5_a_re, s5_a_im, s5_b_re, s5_b_im, s5_c_re, s5_c_im, s5_d, s5_log_dt, s5_glu_w, s5_glu_b, lru_conv_w, lru_conv_b, lru_wx, lru_bx, lru_wa, lru_ba, lru_a_param, mix_norm_g, w_out, b_out, ln1_g, ln1_b, ffn_w_gate, ffn_w_up, ffn_conv_w, ffn_conv_b, ffn_w_down, ln2_g, ln2_b, loss_target, m_w_in, m_b_in, m_attn_sinks, m_s5_a_re, m_s5_a_im, m_s5_b_re, m_s5_b_im, m_s5_c_re, m_s5_c_im, m_s5_d, m_s5_log_dt, m_s5_glu_w, m_s5_glu_b, m_lru_conv_w, m_lru_conv_b, m_lru_wx, m_lru_bx, m_lru_wa, m_lru_ba, m_lru_a_param, m_mix_norm_g, m_w_out, m_b_out, m_ln1_g, m_ln1_b, m_ffn_w_gate, m_ffn_w_up, m_ffn_conv_w, m_ffn_conv_b, m_ffn_w_down, m_ln2_g, m_ln2_b, v_w_in, v_b_in, v_attn_sinks, v_s5_a_re, v_s5_a_im, v_s5_b_re, v_s5_b_im, v_s5_c_re, v_s5_c_im, v_s5_d, v_s5_log_dt, v_s5_glu_w, v_s5_glu_b, v_lru_conv_w, v_lru_conv_b, v_lru_wx, v_lru_bx, v_lru_wa, v_lru_ba, v_lru_a_param, v_mix_norm_g, v_w_out, v_b_out, v_ln1_g, v_ln1_b, v_ffn_w_gate, v_ffn_w_up, v_ffn_conv_w, v_ffn_conv_b, v_ffn_w_down, v_ln2_g, v_ln2_b):
    return _step(dict(locals()))
```

```python
import functools
import math

import jax
import jax.numpy as jnp
from jax import lax
from jax.experimental import pallas as pl
from jax.experimental.pallas import tpu as pltpu

F32 = jnp.float32
BF16 = jnp.bfloat16
MESH = pl.DeviceIdType.MESH
ANY = pl.BlockSpec(memory_space=pl.ANY)

D_MODEL = 1024
DEPTH = 4
HEAD_DIM = 64
N_Q_HEADS = 8
N_KV_HEADS = 2
Q_PER_KV = 4
D_ATTN = 512
D_KV = 128
ATTN_BLOCK = 128
ROPE_THETA = 10000.0
D_S5 = 256
S5_GROUP = 16
S5_GROUPS = 16
S5_STATE = 64
N_STATE = S5_GROUPS * S5_STATE
D_LRU = 256
LRU_HEADS = 4
LRU_HEAD_DIM = 64
LRU_CONV = 4
LRU_C = 8.0
D_IN = 1536
D_FF = 2816
FFN_CONV = 3
N_CHIPS = 4
N_DEV = 8
IN_SH = D_IN // N_CHIPS
FF_SH = D_FF // N_CHIPS
OUT_SH = D_MODEL // N_CHIPS
ALPHA = (2 * DEPTH) ** 0.25
LN_EPS = 1e-5
RMS_EPS = 1e-6
ADAM_LR = 0.001
ADAM_B1 = 0.9
ADAM_B2 = 0.999
ADAM_EPS = 1e-08
ADAM_WD = 0.01
ADAM_STEP = 10

SUBLANES = 8
VMEM_MB = 56


def _params(sem):
    return pltpu.CompilerParams(dimension_semantics=sem, vmem_limit_bytes=VMEM_MB << 20)


def _row_tile(t, pref):
    return min(t, pref)


def _matmul(name, a, b, *, a_blk, a_map, b_blk, b_map, out_shape, o_blk, o_map, grid, dims,
            out_dtype=F32, bias=None, bias_blk=None, bias_map=None, add=None, add_scale=1.0, pair2=None):
    nk = grid[2]
    acc_shape = tuple(d for d in o_blk if d is not None)
    n_in = 2 if pair2 is None else 4

    def kern(*refs):
        p = n_in
        bias_ref = add_ref = None
        if bias is not None:
            bias_ref = refs[p]
            p += 1
        if add is not None:
            add_ref = refs[p]
            p += 1
        o_ref, acc = refs[p], refs[p + 1]
        k = pl.program_id(2)

        def product():
            r = _dot(refs[0][...].astype(BF16), refs[1][...].astype(BF16), dims)
            if pair2 is not None:
                r = r + _dot(refs[2][...].astype(BF16), refs[3][...].astype(BF16), dims)
            return r

        def finish(r):
            if bias_ref is not None:
                r = r + bias_ref[...]
            if add_ref is not None:
                r = r + add_scale * add_ref[...]
            o_ref[...] = r.astype(out_dtype)

        if nk == 1:
            finish(product())
        else:
            @pl.when(k == 0)
            def _():
                acc[...] = jnp.zeros_like(acc)

            acc[...] += product()

            @pl.when(k == nk - 1)
            def _():
                finish(acc[...])

    in_specs = [pl.BlockSpec(a_blk, a_map), pl.BlockSpec(b_blk, b_map)]
    args = [a, b]
    if pair2 is not None:
        in_specs += [pl.BlockSpec(a_blk, a_map), pl.BlockSpec(b_blk, b_map)]
        args += list(pair2)
    if bias is not None:
        in_specs.append(pl.BlockSpec(bias_blk, bias_map))
        args.append(bias)
    if add is not None:
        in_specs.append(pl.BlockSpec(o_blk, lambda i, j, k: o_map(i, j)))
        args.append(add)
    return pl.pallas_call(
        kern, name=name, grid=grid, in_specs=in_specs,
        out_specs=pl.BlockSpec(o_blk, lambda i, j, k: o_map(i, j)),
        out_shape=jax.ShapeDtypeStruct(out_shape, out_dtype),
        scratch_shapes=[pltpu.VMEM(acc_shape if nk > 1 else (SUBLANES, 128), F32)],
        compiler_params=_params(("parallel", "parallel", "arbitrary")),
    )(*args)


NN = ((1,), (0,))
NT = ((1,), (1,))
TN = ((0,), (0,))
TM = 512


def _sigmoid(x):
    return 0.5 * jnp.tanh(0.5 * x) + 0.5


_GELU_C = math.sqrt(2.0 / math.pi)


def _gelu(x):
    return 0.5 * x * (1.0 + jnp.tanh(_GELU_C * (x + 0.044715 * x * x * x)))


def _gelu_grad(x):
    th = jnp.tanh(_GELU_C * (x + 0.044715 * x * x * x))
    return 0.5 * (1.0 + th) + 0.5 * x * (1.0 - th * th) * _GELU_C * (1.0 + 3 * 0.044715 * x * x)


def _rope_swap(t):
    lane = lax.broadcasted_iota(jnp.int32, t.shape, 1)
    lo = (lane % HEAD_DIM) < (HEAD_DIM // 2)
    return jnp.where(lo, pltpu.roll(t, 128 - HEAD_DIM // 2, 1), pltpu.roll(t, HEAD_DIM // 2, 1))


D_QKV = D_ATTN + 2 * D_KV
TMM = 1024


def _in_proj(xb, w_in, b_in, cos, sin_s):
    t = xb.shape[0]
    tm = _row_tile(t, TMM)

    def kern(x_ref, w_ref, b_ref, c_ref, s_ref, q_ref, u_ref):
        x = x_ref[...]
        c = c_ref[...]
        s = s_ref[...]
        for j in range(N_CHIPS):
            pj = _dot(x, w_ref[j], NN) + b_ref[:, j * IN_SH:(j + 1) * IN_SH]
            for ch in range(IN_SH // 128):
                col = j * IN_SH + ch * 128
                v = pj[:, ch * 128:(ch + 1) * 128]
                if col < D_ATTN + D_KV:
                    v = v * c + _rope_swap(v) * s
                if col < D_ATTN:
                    v = v * (HEAD_DIM ** -0.5)
                if col < D_QKV:
                    q_ref[:, col:col + 128] = v.astype(BF16)
                else:
                    u_ref[:, col - D_QKV:col - D_QKV + 128] = v

    row = lambda w: pl.BlockSpec((tm, w), lambda i: (i, 0))
    return pl.pallas_call(
        kern, name="in_proj", grid=(t // tm,),
        in_specs=[row(D_MODEL), pl.BlockSpec((N_CHIPS, D_MODEL, IN_SH), lambda i: (0, 0, 0)),
                  pl.BlockSpec((1, D_IN), lambda i: (0, 0)), row(128), row(128)],
        out_specs=[row(D_QKV), row(D_IN - D_QKV)],
        out_shape=[jax.ShapeDtypeStruct((t, D_QKV), BF16), jax.ShapeDtypeStruct((t, D_IN - D_QKV), F32)],
        compiler_params=_params(("parallel",)),
    )(xb, w_in, b_in, cos, sin_s)


def _attn_mask(i):
    qi = lax.broadcasted_iota(jnp.int32, (ATTN_BLOCK, 2 * ATTN_BLOCK), 0)
    si = lax.broadcasted_iota(jnp.int32, (ATTN_BLOCK, 2 * ATTN_BLOCK), 1)
    diff = qi + ATTN_BLOCK - si
    return (diff >= 0) & (diff < ATTN_BLOCK) & ((si >= ATTN_BLOCK) | (i > 0))


def _attn_fwd(qkv, sinks):
    t = qkv.shape[0]
    nb = t // ATTN_BLOCK

    def kern(q_ref, kp_ref, kc_ref, vp_ref, vc_ref, s_ref, o_ref, l_ref):
        i = pl.program_id(0)
        valid = _attn_mask(i)
        kband = jnp.concatenate([kp_ref[...], kc_ref[...]], axis=0)
        vband = jnp.concatenate([vp_ref[...], vc_ref[...]], axis=0)
        ks = [kband[:, kh * HEAD_DIM:(kh + 1) * HEAD_DIM] for kh in range(N_KV_HEADS)]
        vs = [vband[:, kh * HEAD_DIM:(kh + 1) * HEAD_DIM] for kh in range(N_KV_HEADS)]
        scores = [_dot(q_ref[:, h * HEAD_DIM:(h + 1) * HEAD_DIM], ks[h // Q_PER_KV], NT) for h in range(N_Q_HEADS)]
        probs, lses = [], []
        for h in range(N_Q_HEADS):
            s = jnp.where(valid, scores[h], -jnp.inf)
            sink = s_ref[0:1, h:h + 1]
            m = jnp.maximum(jnp.max(s, axis=-1, keepdims=True), sink)
            e = jnp.exp(s - m)
            denom = jnp.sum(e, axis=-1, keepdims=True) + jnp.exp(sink - m)
            probs.append((e / denom).astype(BF16))
            lses.append(m + jnp.log(denom))
        outs = [_dot(probs[h], vs[h // Q_PER_KV], NN) for h in range(N_Q_HEADS)]
        for h in range(N_Q_HEADS):
            o_ref[:, h * HEAD_DIM:(h + 1) * HEAD_DIM] = outs[h]
            l_ref[:, h:h + 1] = lses[h]

    blk = lambda w, f: pl.BlockSpec((ATTN_BLOCK, w), f)
    return pl.pallas_call(
        kern, name="attn_fwd", grid=(nb,),
        in_specs=[blk(512, lambda i: (i, 0)),
                  blk(128, lambda i: (jnp.maximum(i - 1, 0), 4)), blk(128, lambda i: (i, 4)),
                  blk(128, lambda i: (jnp.maximum(i - 1, 0), 5)), blk(128, lambda i: (i, 5)),
                  pl.BlockSpec((1, N_Q_HEADS), lambda i: (0, 0))],
        out_specs=[blk(512, lambda i: (i, 0)), blk(N_Q_HEADS, lambda i: (i, 0))],
        out_shape=[jax.ShapeDtypeStruct((t, D_ATTN), F32), jax.ShapeDtypeStruct((t, N_Q_HEADS), F32)],
        compiler_params=_params(("parallel",)),
    )(qkv, qkv, qkv, qkv, qkv, sinks)


def _attn_bwd(qkv, o, do, lse, sinks):
    t = qkv.shape[0]
    nb = t // ATTN_BLOCK

    def kern(q_ref, kp_ref, kc_ref, vp_ref, vc_ref, o_ref, do_ref, l_ref, s_ref,
             dq_ref, dk_ref, dv_ref, ds_ref, ck, cv):
        i = pl.program_id(0)

        @pl.when(i == 0)
        def _():
            ds_ref[...] = jnp.zeros_like(ds_ref)
            ck[...] = jnp.zeros_like(ck)
            cv[...] = jnp.zeros_like(cv)

        @pl.when(i < nb)
        def _():
            valid = _attn_mask(i)
            kband = jnp.concatenate([kp_ref[...], kc_ref[...]], axis=0)
            vband = jnp.concatenate([vp_ref[...], vc_ref[...]], axis=0)
            heads = range(N_Q_HEADS)
            sl = [slice(h * HEAD_DIM, (h + 1) * HEAD_DIM) for h in heads]
            ks = [kband[:, kh * HEAD_DIM:(kh + 1) * HEAD_DIM] for kh in range(N_KV_HEADS)]
            vs = [vband[:, kh * HEAD_DIM:(kh + 1) * HEAD_DIM] for kh in range(N_KV_HEADS)]
            qs = [q_ref[:, sl[h]] for h in heads]
            d_os = [do_ref[:, sl[h]] for h in heads]
            dobs = [d.astype(BF16) for d in d_os]
            scores = [_dot(qs[h], ks[h // Q_PER_KV], NT) for h in heads]
            dps = [_dot(dobs[h], vs[h // Q_PER_KV], NT) for h in heads]
            pbs, dscs = [], []
            for h in heads:
                lse_h = l_ref[:, h:h + 1]
                p = jnp.where(valid, jnp.exp(scores[h] - lse_h), 0.0)
                delta = jnp.sum(d_os[h] * o_ref[:, sl[h]], axis=-1, keepdims=True)
                pbs.append(p.astype(BF16))
                dscs.append((p * (dps[h] - delta)).astype(BF16))
                psink = jnp.exp(s_ref[0:1, h:h + 1] - lse_h)
                ds_ref[0:1, h:h + 1] += -jnp.sum(psink * delta, axis=0, keepdims=True)
            dqs = [_dot(dscs[h], ks[h // Q_PER_KV], NN) for h in heads]
            dkb = [sum(_dot(dscs[h], qs[h], TN) for h in heads if h // Q_PER_KV == kh) for kh in range(N_KV_HEADS)]
            dvb = [sum(_dot(pbs[h], dobs[h], TN) for h in heads if h // Q_PER_KV == kh) for kh in range(N_KV_HEADS)]
            for h in heads:
                dq_ref[:, sl[h]] = dqs[h]
            dk_band = jnp.concatenate(dkb, axis=1)
            dv_band = jnp.concatenate(dvb, axis=1)
            dk_ref[...] = ck[...] + dk_band[:ATTN_BLOCK]
            dv_ref[...] = cv[...] + dv_band[:ATTN_BLOCK]
            ck[...] = dk_band[ATTN_BLOCK:]
            cv[...] = dv_band[ATTN_BLOCK:]

        @pl.when(i == nb)
        def _():
            dk_ref[...] = ck[...]
            dv_ref[...] = cv[...]

    blk = lambda w, f: pl.BlockSpec((ATTN_BLOCK, w), f)
    cur = lambda i: jnp.minimum(i, nb - 1)
    prev = lambda i: jnp.clip(i - 1, 0, nb - 1)
    return pl.pallas_call(
        kern, name="attn_bwd", grid=(nb + 1,),
        in_specs=[blk(512, lambda i: (cur(i), 0)),
                  blk(128, lambda i: (prev(i), 4)), blk(128, lambda i: (cur(i), 4)),
                  blk(128, lambda i: (prev(i), 5)), blk(128, lambda i: (cur(i), 5)),
                  blk(512, lambda i: (cur(i), 0)), blk(512, lambda i: (cur(i), 0)),
                  blk(N_Q_HEADS, lambda i: (cur(i), 0)),
                  pl.BlockSpec((1, N_Q_HEADS), lambda i: (0, 0))],
        out_specs=[blk(512, lambda i: (cur(i), 0)), blk(128, lambda i: (prev(i), 0)),
                   blk(128, lambda i: (prev(i), 0)), pl.BlockSpec((1, N_Q_HEADS), lambda i: (0, 0))],
        out_shape=[jax.ShapeDtypeStruct((t, D_ATTN), F32), jax.ShapeDtypeStruct((t, D_KV), F32),
                   jax.ShapeDtypeStruct((t, D_KV), F32), jax.ShapeDtypeStruct((1, N_Q_HEADS), F32)],
        scratch_shapes=[pltpu.VMEM((ATTN_BLOCK, D_KV), F32), pltpu.VMEM((ATTN_BLOCK, D_KV), F32)],
        compiler_params=_params(("arbitrary",)),
    )(qkv, qkv, qkv, qkv, qkv, o, do, lse, sinks)


_GROUPS = ((0, D_ATTN), (D_ATTN, D_ATTN + D_S5), (D_ATTN + D_S5, D_MODEL))


def _rms_fwd(ya, ys, yl, g):
    t = ya.shape[0]
    tm = _row_tile(t, TM)

    def kern(a_ref, s_ref, l_ref, g_ref, o_ref):
        for (lo, hi), ref in zip(_GROUPS, (a_ref, s_ref, l_ref)):
            y = ref[...]
            n = y * lax.rsqrt(jnp.mean(y * y, axis=-1, keepdims=True) + RMS_EPS)
            o_ref[:, lo:hi] = (n * g_ref[:, lo:hi]).astype(BF16)

    row = lambda w: pl.BlockSpec((tm, w), lambda i: (i, 0))
    return pl.pallas_call(
        kern, name="rms_fwd", grid=(t // tm,),
        in_specs=[row(D_ATTN), row(D_S5), row(D_LRU), pl.BlockSpec((1, D_MODEL), lambda i: (0, 0))],
        out_specs=row(D_MODEL), out_shape=jax.ShapeDtypeStruct((t, D_MODEL), BF16),
        compiler_params=_params(("parallel",)),
    )(ya, ys, yl, g)


def _rms_bwd(dmix, ya, ys, yl, g):
    t = ya.shape[0]
    tm = _row_tile(t, TM)

    def kern(d_ref, a_ref, s_ref, l_ref, g_ref, da_ref, ds_ref, dl_ref, dg_ref):
        @pl.when(pl.program_id(0) == 0)
        def _():
            dg_ref[...] = jnp.zeros_like(dg_ref)

        for (lo, hi), ref, out in zip(_GROUPS, (a_ref, s_ref, l_ref), (da_ref, ds_ref, dl_ref)):
            y = ref[...]
            rstd = lax.rsqrt(jnp.mean(y * y, axis=-1, keepdims=True) + RMS_EPS)
            n = y * rstd
            dm = d_ref[:, lo:hi]
            dg_ref[:, lo:hi] += jnp.sum(dm * n, axis=0, keepdims=True)
            dn = dm * g_ref[:, lo:hi]
            out[...] = rstd * (dn - n * jnp.mean(dn * n, axis=-1, keepdims=True))

    row = lambda w: pl.BlockSpec((tm, w), lambda i: (i, 0))
    vec = pl.BlockSpec((1, D_MODEL), lambda i: (0, 0))
    return pl.pallas_call(
        kern, name="rms_bwd", grid=(t // tm,),
        in_specs=[row(D_MODEL), row(D_ATTN), row(D_S5), row(D_LRU), vec],
        out_specs=[row(D_ATTN), row(D_S5), row(D_LRU), vec],
        out_shape=[jax.ShapeDtypeStruct((t, D_ATTN), F32), jax.ShapeDtypeStruct((t, D_S5), F32),
                   jax.ShapeDtypeStruct((t, D_LRU), F32), jax.ShapeDtypeStruct((1, D_MODEL), F32)],
        compiler_params=_params(("arbitrary",)),
    )(dmix, ya, ys, yl, g)


def _matmul_ln(name, a, w, bias, xres, g, b, a_blk, a_map, w_blk, parts):
    t = xres.shape[0]
    tm = a_blk[-2]

    def kern(a_ref, w_ref, bias_ref, x_ref, g_ref, b_ref, y_ref, yb_ref, h_ref, r_ref):
        if parts is None:
            f = _dot(a_ref[...], w_ref[...], NN)
        else:
            f = sum(_dot(a_ref[j], w_ref[j], NN) for j in range(parts))
        r = ALPHA * x_ref[...] + f + bias_ref[...]
        mu = jnp.mean(r, axis=-1, keepdims=True)
        xc = r - mu
        rstd = lax.rsqrt(jnp.mean(xc * xc, axis=-1, keepdims=True) + LN_EPS)
        xhat = xc * rstd
        h_ref[...] = xhat
        r_ref[...] = rstd
        y = xhat * g_ref[...] + b_ref[...]
        y_ref[...] = y
        yb_ref[...] = y.astype(BF16)

    row = pl.BlockSpec((tm, D_MODEL), lambda i: (i, 0))
    vec = pl.BlockSpec((1, D_MODEL), lambda i: (0, 0))
    big = lambda dt: jax.ShapeDtypeStruct((t, D_MODEL), dt)
    return pl.pallas_call(
        kern, name=name, grid=(t // tm,),
        in_specs=[pl.BlockSpec(a_blk, a_map), pl.BlockSpec(w_blk, lambda i: (0,) * len(w_blk)), vec, row, vec, vec],
        out_specs=[row, row, row, pl.BlockSpec((tm, 1), lambda i: (i, 0))],
        out_shape=[big(F32), big(BF16), big(F32), jax.ShapeDtypeStruct((t, 1), F32)],
        compiler_params=_params(("parallel",)),
    )(a, w, bias, xres, g, b)


def _ln_bwd(dy, xhat, rstd, g):
    t = dy.shape[0]
    tm = _row_tile(t, TM)

    def kern(d_ref, h_ref, r_ref, g_ref, dr_ref, drb_ref, dg_ref, db_ref, sr_ref):
        @pl.when(pl.program_id(0) == 0)
        def _():
            dg_ref[...] = jnp.zeros_like(dg_ref)
            db_ref[...] = jnp.zeros_like(db_ref)
            sr_ref[...] = jnp.zeros_like(sr_ref)

        d = d_ref[...]
        xhat = h_ref[...]
        dg_ref[...] += jnp.sum(d * xhat, axis=0, keepdims=True)
        db_ref[...] += jnp.sum(d, axis=0, keepdims=True)
        dh = d * g_ref[...]
        dr = r_ref[...] * (dh - jnp.mean(dh, axis=-1, keepdims=True)
                           - xhat * jnp.mean(dh * xhat, axis=-1, keepdims=True))
        dr_ref[...] = dr
        drb_ref[...] = dr.astype(BF16)
        sr_ref[...] += jnp.sum(dr, axis=0, keepdims=True)

    row = pl.BlockSpec((tm, D_MODEL), lambda i: (i, 0))
    vec = pl.BlockSpec((1, D_MODEL), lambda i: (0, 0))
    vshape = jax.ShapeDtypeStruct((1, D_MODEL), F32)
    return pl.pallas_call(
        kern, name="ln_bwd", grid=(t // tm,),
        in_specs=[row, row, pl.BlockSpec((tm, 1), lambda i: (i, 0)), vec],
        out_specs=[row, row, vec, vec, vec],
        out_shape=[jax.ShapeDtypeStruct((t, D_MODEL), F32), jax.ShapeDtypeStruct((t, D_MODEL), BF16),
                   vshape, vshape, vshape],
        compiler_params=_params(("arbitrary",)),
    )(dy, xhat, rstd, g)


def _loss_head(y, target):
    t = y.shape[0]
    tm = _row_tile(t, TM)

    def kern(y_ref, t_ref, l_ref, d_ref):
        @pl.when(pl.program_id(0) == 0)
        def _():
            l_ref[...] = jnp.zeros_like(l_ref)

        err = y_ref[...] - t_ref[...]
        d_ref[...] = err * (1.0 / D_MODEL)
        part = jnp.sum(jnp.sum(err * err, axis=-1, keepdims=True), axis=0, keepdims=True)
        l_ref[...] += jnp.broadcast_to(part * (0.5 / D_MODEL), l_ref.shape)

    row = pl.BlockSpec((tm, D_MODEL), lambda i: (i, 0))
    return pl.pallas_call(
        kern, name="loss_head", grid=(t // tm,),
        in_specs=[row, row], out_specs=[pl.BlockSpec((1, 128), lambda i: (0, 0)), row],
        out_shape=[jax.ShapeDtypeStruct((1, 128), F32), jax.ShapeDtypeStruct((t, D_MODEL), F32)],
        compiler_params=_params(("arbitrary",)),
    )(y, target)


HALO = 8


def _ffn_mid_specs(t, tm):
    main = pl.BlockSpec((None, tm, FF_SH), lambda j, i: (j, i, 0))
    prev = pl.BlockSpec((None, HALO, FF_SH), lambda j, i: (j, jnp.maximum(i * (tm // HALO) - 1, 0), 0))
    cw = pl.BlockSpec((None, FFN_CONV, FF_SH), lambda j, i: (j, 0, 0))
    cb = pl.BlockSpec((None, 1, FF_SH), lambda j, i: (j, 0, 0))
    return main, prev, cw, cb


def _ffn_conv(ext, g_ref, p_ref, w_ref, b_ref, tm):
    i = pl.program_id(1)
    ext[0:HALO, :] = jnp.where(i > 0, p_ref[...], 0.0)
    ext[HALO:, :] = g_ref[...]
    taps = [ext[pl.ds(HALO - (FFN_CONV - 1) + k, tm), :] for k in range(FFN_CONV)]
    gc = b_ref[...] + sum(w_ref[k:k + 1, :] * taps[k] for k in range(FFN_CONV))
    return gc, taps


def _ffn_mid_fwd(gpre, up, cw, cb):
    t = gpre.shape[1]
    tm = _row_tile(t, TM)

    def kern(g_ref, p_ref, u_ref, w_ref, b_ref, o_ref, ext):
        gc, _ = _ffn_conv(ext, g_ref, p_ref, w_ref, b_ref, tm)
        o_ref[...] = (gc * _sigmoid(gc) * u_ref[...]).astype(BF16)

    main, prev, cws, cbs = _ffn_mid_specs(t, tm)
    return pl.pallas_call(
        kern, name="ffn_mid_fwd", grid=(N_CHIPS, t // tm),
        in_specs=[main, prev, main, cws, cbs], out_specs=main,
        out_shape=jax.ShapeDtypeStruct((N_CHIPS, t, FF_SH), BF16),
        scratch_shapes=[pltpu.VMEM((tm + HALO, FF_SH), F32)],
        compiler_params=_params(("parallel", "parallel")),
    )(gpre, gpre, up, cw, cb)


def _ffn_mid_bwd(gpre, up, dhmid, cw, cb):
    t = gpre.shape[1]
    tm = _row_tile(t, TM)

    def kern(g_ref, p_ref, u_ref, d_ref, w_ref, b_ref, h_ref, du_ref, dg_ref, dw_ref, db_ref, ext):
        @pl.when(pl.program_id(1) == 0)
        def _():
            dw_ref[...] = jnp.zeros_like(dw_ref)
            db_ref[...] = jnp.zeros_like(db_ref)

        gc, taps = _ffn_conv(ext, g_ref, p_ref, w_ref, b_ref, tm)
        sg = _sigmoid(gc)
        s = gc * sg
        u = u_ref[...]
        d = d_ref[...]
        h_ref[...] = (s * u).astype(BF16)
        du_ref[...] = (d * s).astype(BF16)
        dgc = d * u * (sg * (1.0 + gc * (1.0 - sg)))
        dg_ref[...] = dgc
        db_ref[...] += jnp.sum(dgc, axis=0, keepdims=True)
        for k in range(FFN_CONV):
            dw_ref[k:k + 1, :] += jnp.sum(dgc * taps[k], axis=0, keepdims=True)

    main, prev, cws, cbs = _ffn_mid_specs(t, tm)
    big = lambda dt: jax.ShapeDtypeStruct((N_CHIPS, t, FF_SH), dt)
    return pl.pallas_call(
        kern, name="ffn_mid_bwd", grid=(N_CHIPS, t // tm),
        in_specs=[main, prev, main, main, cws, cbs], out_specs=[main, main, main, cws, cbs],
        out_shape=[big(BF16), big(BF16), big(F32), jax.ShapeDtypeStruct((N_CHIPS, FFN_CONV, FF_SH), F32),
                   jax.ShapeDtypeStruct((N_CHIPS, 1, FF_SH), F32)],
        scratch_shapes=[pltpu.VMEM((tm + HALO, FF_SH), F32)],
        compiler_params=_params(("parallel", "arbitrary")),
    )(gpre, gpre, up, dhmid, cw, cb)


def _ffn_conv_t(dgc, cw):
    t = dgc.shape[1]
    tm = _row_tile(t, TM)
    nt = t // tm

    def kern(d_ref, n_ref, w_ref, o_ref, ext):
        i = pl.program_id(1)
        ext[0:tm, :] = d_ref[...]
        ext[tm:, :] = jnp.where(i < nt - 1, n_ref[...], 0.0)
        acc = sum(w_ref[k:k + 1, :] * ext[pl.ds(FFN_CONV - 1 - k, tm), :] for k in range(FFN_CONV))
        o_ref[...] = acc.astype(BF16)

    main, _, cws, _ = _ffn_mid_specs(t, tm)
    nxt = pl.BlockSpec((None, HALO, FF_SH),
                       lambda j, i: (j, jnp.minimum((i + 1) * (tm // HALO), t // HALO - 1), 0))
    return pl.pallas_call(
        kern, name="ffn_conv_t", grid=(N_CHIPS, nt),
        in_specs=[main, nxt, cws], out_specs=main,
        out_shape=jax.ShapeDtypeStruct((N_CHIPS, t, FF_SH), BF16),
        scratch_shapes=[pltpu.VMEM((tm + HALO, FF_SH), F32)],
        compiler_params=_params(("parallel", "parallel")),
    )(dgc, dgc, cw)


def _ffn_hidden_fwd(xb, wg, wu, cw, cb):
    t = xb.shape[0]
    tm = _row_tile(t, TM)

    def kern(x_ref, wg_ref, wu_ref, cw_ref, cb_ref, g_ref, c_ref, u_ref, h_ref, ext):
        @pl.when(pl.program_id(1) == 0)
        def _():
            ext[0:HALO, :] = jnp.zeros((HALO, FF_SH), F32)

        x = x_ref[...]
        gb = _dot(x, wg_ref[...], NN).astype(BF16)
        ub = _dot(x, wu_ref[...], NN).astype(BF16)
        g_ref[...] = gb
        u_ref[...] = ub
        g = gb.astype(F32)
        ext[HALO:, :] = g
        gcb = (cb_ref[...] + sum(cw_ref[k:k + 1, :] * ext[pl.ds(HALO - (FFN_CONV - 1) + k, tm), :]
                                 for k in range(FFN_CONV))).astype(BF16)
        c_ref[...] = gcb
        gc = gcb.astype(F32)
        h_ref[...] = (gc * _sigmoid(gc) * ub.astype(F32)).astype(BF16)
        ext[0:HALO, :] = g[tm - HALO:, :]

    col = pl.BlockSpec((None, tm, FF_SH), lambda j, i: (j, i, 0))
    wspec = pl.BlockSpec((None, D_MODEL, FF_SH), lambda j, i: (j, 0, 0))
    big = jax.ShapeDtypeStruct((N_CHIPS, t, FF_SH), BF16)
    return pl.pallas_call(
        kern, name="ffn_hidden_fwd", grid=(N_CHIPS, t // tm),
        in_specs=[pl.BlockSpec((tm, D_MODEL), lambda j, i: (i, 0)), wspec, wspec,
                  pl.BlockSpec((None, FFN_CONV, FF_SH), lambda j, i: (j, 0, 0)),
                  pl.BlockSpec((None, 1, FF_SH), lambda j, i: (j, 0, 0))],
        out_specs=[col, col, col, col], out_shape=[big, big, big, big],
        scratch_shapes=[pltpu.VMEM((tm + HALO, FF_SH), F32)],
        compiler_params=_params(("parallel", "arbitrary")),
    )(xb, wg, wu, cw, cb)


def _ffn_hidden_bwd(drb, gpre, gconv, up, wd, cw):
    t = drb.shape[0]
    tm = _row_tile(t, TM)
    nt = t // tm
    rb = lambda i: nt - 1 - i

    def kern(d_ref, g_ref, c_ref, u_ref, wd_ref, cw_ref, du_ref, dg_ref, dw_ref, db_ref, ext):
        @pl.when(pl.program_id(1) == 0)
        def _():
            dw_ref[...] = jnp.zeros_like(dw_ref)
            db_ref[...] = jnp.zeros_like(db_ref)
            ext[tm:, :] = jnp.zeros((HALO, FF_SH), F32)

        dh = _dot(d_ref[...], wd_ref[...], NT)
        gc = c_ref[...].astype(F32)
        sg = _sigmoid(gc)
        du_ref[...] = (dh * (gc * sg)).astype(BF16)
        dgc = dh * u_ref[...].astype(F32) * (sg * (1.0 + gc * (1.0 - sg)))
        db_ref[...] += jnp.sum(dgc, axis=0, keepdims=True)
        ext[0:tm, :] = dgc
        g = g_ref[...].astype(F32)
        acc = None
        for k in range(FFN_CONV):
            tap = ext[pl.ds(FFN_CONV - 1 - k, tm), :]
            dw_ref[k:k + 1, :] += jnp.sum(g * tap, axis=0, keepdims=True)
            term = cw_ref[k:k + 1, :] * tap
            acc = term if acc is None else acc + term
        dg_ref[...] = acc.astype(BF16)
        ext[tm:, :] = dgc[0:HALO, :]

    col = pl.BlockSpec((None, tm, FF_SH), lambda j, i: (j, rb(i), 0))
    cws = pl.BlockSpec((None, FFN_CONV, FF_SH), lambda j, i: (j, 0, 0))
    cbs = pl.BlockSpec((None, 1, FF_SH), lambda j, i: (j, 0, 0))
    big = jax.ShapeDtypeStruct((N_CHIPS, t, FF_SH), BF16)
    return pl.pallas_call(
        kern, name="ffn_hidden_bwd", grid=(N_CHIPS, nt),
        in_specs=[pl.BlockSpec((tm, D_MODEL), lambda j, i: (rb(i), 0)), col, col, col,
                  pl.BlockSpec((None, FF_SH, D_MODEL), lambda j, i: (j, 0, 0)), cws],
        out_specs=[col, col, cws, cbs],
        out_shape=[big, big, jax.ShapeDtypeStruct((N_CHIPS, FFN_CONV, FF_SH), F32),
                   jax.ShapeDtypeStruct((N_CHIPS, 1, FF_SH), F32)],
        scratch_shapes=[pltpu.VMEM((tm + HALO, FF_SH), F32)],
        compiler_params=_params(("parallel", "arbitrary")),
    )(drb, gpre, gconv, up, wd, cw)


def _s5_coefs(ar, ai, reverse):
    if reverse:
        ai = -ai
    pw = [(ar, ai)]
    for _ in range(SUBLANES - 1):
        pr, pi = pw[-1]
        pw.append((pr * ar - pi * ai, pr * ai + pi * ar))
    rows = jnp.arange(SUBLANES)[:, None]
    out = []
    for s in (1, 2, 4):
        keep = (rows + s <= SUBLANES - 1) if reverse else (rows >= s)
        out += [jnp.where(keep, pw[s - 1][0][None], 0.0), jnp.where(keep, pw[s - 1][1][None], 0.0)]
    order = list(range(SUBLANES - 1, -1, -1)) if reverse else list(range(SUBLANES))
    out += [jnp.stack([pw[k][0] for k in order]), jnp.stack([pw[k][1] for k in order])]
    return jnp.stack(out).astype(F32)


def _s5_scan(buf, coef_ref, carry, tm, reverse):
    n8 = tm // SUBLANES

    def body(it, c):
        cre, cim = c
        blk = (n8 - 1 - it) if reverse else it
        r0 = pl.multiple_of(blk * SUBLANES, SUBLANES)
        xre = buf[pl.ds(r0, SUBLANES), 0:N_STATE]
        xim = buf[pl.ds(r0, SUBLANES), N_STATE:]
        for idx, s in enumerate((1, 2, 4)):
            sh = (SUBLANES - s) if reverse else s
            sre = pltpu.roll(xre, sh, 0)
            sim = pltpu.roll(xim, sh, 0)
            are = coef_ref[2 * idx]
            aim = coef_ref[2 * idx + 1]
            xre, xim = xre + are * sre - aim * sim, xim + are * sim + aim * sre
        pre = coef_ref[6]
        pim = coef_ref[7]
        hre = xre + pre * cre - pim * cim
        him = xim + pre * cim + pim * cre
        buf[pl.ds(r0, SUBLANES), 0:N_STATE] = hre
        buf[pl.ds(r0, SUBLANES), N_STATE:] = him
        row = 0 if reverse else SUBLANES - 1
        return (jnp.broadcast_to(hre[row:row + 1], (SUBLANES, N_STATE)),
                jnp.broadcast_to(him[row:row + 1], (SUBLANES, N_STATE)))

    cre, cim = lax.fori_loop(0, n8, body, (carry[:, 0:N_STATE], carry[:, N_STATE:]))
    carry[:, 0:N_STATE] = cre
    carry[:, N_STATE:] = cim


def _real_scan(abuf, bbuf, carry, tm, reverse):
    n8 = tm // SUBLANES
    width = bbuf.shape[1]

    def body(it, c):
        blk = (n8 - 1 - it) if reverse else it
        r0 = pl.multiple_of(blk * SUBLANES, SUBLANES)
        a = abuf[pl.ds(r0, SUBLANES), :]
        b = bbuf[pl.ds(r0, SUBLANES), :]
        rows = lax.broadcasted_iota(jnp.int32, (SUBLANES, width), 0)
        for s in (1, 2, 4):
            sh = (SUBLANES - s) if reverse else s
            keep = (rows + s <= SUBLANES - 1) if reverse else (rows >= s)
            sa = pltpu.roll(a, sh, 0)
            sb = pltpu.roll(b, sh, 0)
            b = b + a * jnp.where(keep, sb, 0.0)
            a = a * jnp.where(keep, sa, 1.0)
        h = b + a * c
        bbuf[pl.ds(r0, SUBLANES), :] = h
        row = 0 if reverse else SUBLANES - 1
        return jnp.broadcast_to(h[row:row + 1], (SUBLANES, width))

    carry[...] = lax.fori_loop(0, n8, body, carry[...])


def _dot(a, b, dims):
    return lax.dot_general(a, b, (dims, ((), ())), preferred_element_type=F32)


TS5 = 256
HALO16 = 16


def _s5_fwd(proj, bmat, coef, cmat, dvec, gw, gb):
    t = proj.shape[0]
    tm = _row_tile(t, TS5)

    def kern(u_ref, b_ref, coef_ref, c_ref, d_ref, gw_ref, gb_ref, h_ref, y_ref, hbuf, carry):
        @pl.when(pl.program_id(0) == 0)
        def _():
            carry[...] = jnp.zeros_like(carry)

        u = u_ref[...]
        hbuf[...] = _dot(u.astype(BF16), b_ref[...], NN)
        _s5_scan(hbuf, coef_ref, carry, tm, False)
        hb = hbuf[...].astype(BF16)
        h_ref[...] = hb
        y = _dot(hb, c_ref[...], NN) + d_ref[...] * u
        ys = _gelu(y)
        z = _dot(ys.astype(BF16), gw_ref[...], NN) + gb_ref[...]
        y_ref[...] = ys * _sigmoid(z)

    full = lambda shp: pl.BlockSpec(shp, lambda i: (0,) * len(shp))
    return pl.pallas_call(
        kern, name="s5_fwd", grid=(t // tm,),
        in_specs=[pl.BlockSpec((tm, D_S5), lambda i: (i, 0)), full((D_S5, 2 * N_STATE)),
                  full((8, SUBLANES, N_STATE)), full((2 * N_STATE, D_S5)), full((1, D_S5)),
                  full((D_S5, D_S5)), full((1, D_S5))],
        out_specs=[pl.BlockSpec((tm, 2 * N_STATE), lambda i: (i, 0)), pl.BlockSpec((tm, D_S5), lambda i: (i, 0))],
        out_shape=[jax.ShapeDtypeStruct((t, 2 * N_STATE), BF16), jax.ShapeDtypeStruct((t, D_S5), F32)],
        scratch_shapes=[pltpu.VMEM((tm, 2 * N_STATE), F32), pltpu.VMEM((SUBLANES, 2 * N_STATE), F32)],
        compiler_params=_params(("arbitrary",)),
    )(proj, bmat, coef, cmat, dvec, gw, gb)


def _s5_bwd(proj, h, dout, bmat, coef_b, cmat, dvec, gw, gb):
    t = proj.shape[0]
    tm = _row_tile(t, TS5)
    nt = t // tm
    rb = lambda i: nt - 1 - i

    def kern(u_ref, h_ref, hp_ref, d_ref, b_ref, coef_ref, c_ref, dv_ref, gw_ref, gb_ref,
             du_ref, dc_ref, db_ref, da_ref, dd_ref, dgw_ref, dgb_ref, gbuf, hext, carry):
        i = pl.program_id(0)

        @pl.when(i == 0)
        def _():
            carry[...] = jnp.zeros_like(carry)
            for r in (dc_ref, db_ref, da_ref, dd_ref, dgw_ref, dgb_ref):
                r[...] = jnp.zeros_like(r)

        u = u_ref[...]
        hb = h_ref[...]
        y = _dot(hb, c_ref[...], NN) + dv_ref[...] * u
        ys = _gelu(y)
        ysb = ys.astype(BF16)
        sg = _sigmoid(_dot(ysb, gw_ref[...], NN) + gb_ref[...])
        d_o = d_ref[...]
        dz = d_o * ys * sg * (1.0 - sg)
        dzb = dz.astype(BF16)
        dys = d_o * sg + _dot(dzb, gw_ref[...], NT)
        dgw_ref[...] += _dot(ysb, dzb, TN)
        dgb_ref[...] += jnp.sum(dz, axis=0, keepdims=True)
        dy = dys * _gelu_grad(y)
        dd_ref[...] += jnp.sum(dy * u, axis=0, keepdims=True)
        dyb = dy.astype(BF16)
        dc_ref[...] += _dot(hb, dyb, TN)
        gbuf[...] = _dot(dyb, c_ref[...], NT)
        _s5_scan(gbuf, coef_ref, carry, tm, True)
        g = gbuf[...]
        first = jnp.where(i < nt - 1, hp_ref[HALO16 - 1:HALO16, :].astype(F32), 0.0)
        hext[SUBLANES - 1:SUBLANES, :] = first
        hext[SUBLANES:, :] = hb.astype(F32)
        hprev = hext[pl.ds(SUBLANES - 1, tm), :]
        gre, gim = g[:, 0:N_STATE], g[:, N_STATE:]
        pre, pim = hprev[:, 0:N_STATE], hprev[:, N_STATE:]
        da_ref[0:1, :] += jnp.sum(gre * pre + gim * pim, axis=0, keepdims=True)
        da_ref[1:2, :] += jnp.sum(gim * pre - gre * pim, axis=0, keepdims=True)
        gb16 = g.astype(BF16)
        db_ref[...] += _dot(u.astype(BF16), gb16, TN)
        du_ref[...] = dy * dv_ref[...] + _dot(gb16, b_ref[...], NT)

    full = lambda shp: pl.BlockSpec(shp, lambda i: (0,) * len(shp))
    shape = lambda shp: jax.ShapeDtypeStruct(shp, F32)
    return pl.pallas_call(
        kern, name="s5_bwd", grid=(nt,),
        in_specs=[pl.BlockSpec((tm, D_S5), lambda i: (rb(i), 0)),
                  pl.BlockSpec((tm, 2 * N_STATE), lambda i: (rb(i), 0)),
                  pl.BlockSpec((HALO16, 2 * N_STATE), lambda i: (jnp.maximum(rb(i) * (tm // HALO16) - 1, 0), 0)),
                  pl.BlockSpec((tm, D_S5), lambda i: (rb(i), 0)),
                  full((D_S5, 2 * N_STATE)), full((8, SUBLANES, N_STATE)), full((2 * N_STATE, D_S5)),
                  full((1, D_S5)), full((D_S5, D_S5)), full((1, D_S5))],
        out_specs=[pl.BlockSpec((tm, D_S5), lambda i: (rb(i), 0)), full((2 * N_STATE, D_S5)),
                   full((D_S5, 2 * N_STATE)), full((2, N_STATE)), full((1, D_S5)), full((D_S5, D_S5)),
                   full((1, D_S5))],
        out_shape=[shape((t, D_S5)), shape((2 * N_STATE, D_S5)), shape((D_S5, 2 * N_STATE)),
                   shape((2, N_STATE)), shape((1, D_S5)), shape((D_S5, D_S5)), shape((1, D_S5))],
        scratch_shapes=[pltpu.VMEM((tm, 2 * N_STATE), F32), pltpu.VMEM((tm + SUBLANES, 2 * N_STATE), F32),
                        pltpu.VMEM((SUBLANES, 2 * N_STATE), F32)],
        compiler_params=_params(("arbitrary",)),
    )(proj, h, h, dout, bmat, coef_b, cmat, dvec, gw, gb)


def _lru_gates(ext, x_ref, p_ref, cw_ref, cb_ref, wx_ref, bx_ref, wa_ref, ba_ref, ap_ref, first_tile, row0, tm):
    ext[0:HALO, :] = jnp.where(first_tile, 0.0, p_ref[...])
    ext[HALO:, :] = x_ref[...]
    taps = [ext[pl.ds(HALO - (LRU_CONV - 1) + k, tm), :] for k in range(LRU_CONV)]
    xc = cb_ref[...] + sum(cw_ref[k:k + 1, :] * taps[k] for k in range(LRU_CONV))
    xcb = xc.astype(BF16)
    gx = _sigmoid(_dot(xcb, wx_ref[...], NN) + bx_ref[...])
    ga = _sigmoid(_dot(xcb, wa_ref[...], NN) + ba_ref[...])
    z = -ap_ref[...]
    sp = jnp.maximum(z, 0.0) + jnp.log(1.0 + jnp.exp(-jnp.abs(z)))
    log_a = -LRU_C * ga * sp
    a = jnp.exp(log_a)
    tok = row0 + lax.broadcasted_iota(jnp.int32, a.shape, 0)
    is0 = tok == 0
    mult = jnp.where(is0, 1.0, jnp.sqrt(1.0 - jnp.exp(2.0 * log_a)))
    return taps, xc, xcb, gx, ga, sp, a, mult, is0


def _lru_specs(tm, blk_of):
    col = lambda cidx: pl.BlockSpec((tm, D_LRU), lambda i: (blk_of(i), cidx))
    prev = lambda cidx: pl.BlockSpec((HALO, D_LRU), lambda i: (jnp.maximum(blk_of(i) * (tm // HALO) - 1, 0), cidx))
    full = lambda shp: pl.BlockSpec(shp, lambda i: (0,) * len(shp))
    wts = [full((LRU_CONV, D_LRU)), full((1, D_LRU)), full((D_LRU, D_LRU)), full((1, D_LRU)),
           full((D_LRU, D_LRU)), full((1, D_LRU)), full((1, D_LRU))]
    return col, prev, full, wts


def _lru_fwd(proj, cw, cb, wx, bx, wa, ba, ap):
    t = proj.shape[0]
    tm = _row_tile(t, TM)

    def kern(x_ref, p_ref, g_ref, cw_ref, cb_ref, wx_ref, bx_ref, wa_ref, ba_ref, ap_ref,
             y_ref, h_ref, ext, abuf, carry):
        i = pl.program_id(0)

        @pl.when(i == 0)
        def _():
            carry[...] = jnp.zeros_like(carry)

        _, xc, _, gx, _, _, a, mult, _ = _lru_gates(ext, x_ref, p_ref, cw_ref, cb_ref, wx_ref, bx_ref, wa_ref,
                                                    ba_ref, ap_ref, i == 0, i * tm, tm)
        abuf[...] = a
        h_ref[...] = mult * gx * xc
        _real_scan(abuf, h_ref, carry, tm, False)
        y_ref[...] = h_ref[...] * _gelu(g_ref[...])

    col, prev, full, wts = _lru_specs(tm, lambda i: i)
    out = pl.BlockSpec((tm, D_LRU), lambda i: (i, 0))
    return pl.pallas_call(
        kern, name="lru_fwd", grid=(t // tm,),
        in_specs=[col(1), prev(1), col(2)] + wts, out_specs=[out, out],
        out_shape=[jax.ShapeDtypeStruct((t, D_LRU), F32), jax.ShapeDtypeStruct((t, D_LRU), F32)],
        scratch_shapes=[pltpu.VMEM((tm + HALO, D_LRU), F32), pltpu.VMEM((tm, D_LRU), F32),
                        pltpu.VMEM((SUBLANES, D_LRU), F32)],
        compiler_params=_params(("arbitrary",)),
    )(proj, proj, proj, cw, cb, wx, bx, wa, ba, ap)


def _lru_bwd(proj, h, dout, cw, cb, wx, bx, wa, ba, ap):
    t = proj.shape[0]
    tm = _row_tile(t, TM)
    nt = t // tm
    rb = lambda i: nt - 1 - i

    def kern(x_ref, p_ref, g_ref, h_ref, hp_ref, d_ref, cw_ref, cb_ref, wx_ref, bx_ref, wa_ref, ba_ref, ap_ref,
             dxc_ref, dg_ref, dcw_ref, dcb_ref, dwx_ref, dbx_ref, dwa_ref, dba_ref, dap_ref,
             ext, aext, abuf, gbuf, carry, acarry):
        i = pl.program_id(0)
        blk = nt - 1 - i

        @pl.when(i == 0)
        def _():
            carry[...] = jnp.zeros_like(carry)
            acarry[...] = jnp.zeros_like(acarry)
            for r in (dcw_ref, dcb_ref, dwx_ref, dbx_ref, dwa_ref, dba_ref, dap_ref):
                r[...] = jnp.zeros_like(r)

        taps, xc, xcb, gx, ga, sp, a, mult, is0 = _lru_gates(
            ext, x_ref, p_ref, cw_ref, cb_ref, wx_ref, bx_ref, wa_ref, ba_ref, ap_ref, blk == 0, blk * tm, tm)
        gate = g_ref[...]
        d_o = d_ref[...]
        hcur = h_ref[...]
        dg_ref[...] = d_o * hcur * _gelu_grad(gate)
        aext[0:tm, :] = a
        aext[tm:, :] = acarry[...]
        abuf[...] = aext[pl.ds(1, tm), :]
        gbuf[...] = d_o * _gelu(gate)
        _real_scan(abuf, gbuf, carry, tm, True)
        acarry[...] = jnp.broadcast_to(a[0:1], acarry.shape)
        g = gbuf[...]
        ext[0:HALO, :] = jnp.where(blk == 0, 0.0, hp_ref[...])
        ext[HALO:, :] = hcur
        hprev = ext[pl.ds(HALO - 1, tm), :]
        dmult = jnp.where(is0, 0.0, g * gx * xc)
        dgx = g * mult * xc
        dxc = g * mult * gx
        dlog_a = g * hprev * a - dmult * (a * a) / mult
        dga = dlog_a * (-LRU_C * sp)
        dsp = jnp.sum(dlog_a * (-LRU_C * ga), axis=0, keepdims=True)
        dap_ref[...] += dsp * (-_sigmoid(-ap_ref[...]))
        dpa = (dga * ga * (1.0 - ga))
        dpx = (dgx * gx * (1.0 - gx))
        dpab, dpxb = dpa.astype(BF16), dpx.astype(BF16)
        dwx_ref[...] += _dot(xcb, dpxb, TN)
        dwa_ref[...] += _dot(xcb, dpab, TN)
        dbx_ref[...] += jnp.sum(dpx, axis=0, keepdims=True)
        dba_ref[...] += jnp.sum(dpa, axis=0, keepdims=True)
        dxc = dxc + _dot(dpxb, wx_ref[...], NT) + _dot(dpab, wa_ref[...], NT)
        dxc_ref[...] = dxc
        dcb_ref[...] += jnp.sum(dxc, axis=0, keepdims=True)
        for k in range(LRU_CONV):
            dcw_ref[k:k + 1, :] += jnp.sum(dxc * taps[k], axis=0, keepdims=True)

    col, prev, full, wts = _lru_specs(tm, rb)
    row = pl.BlockSpec((tm, D_LRU), lambda i: (rb(i), 0))
    hprev_spec = pl.BlockSpec((HALO, D_LRU), lambda i: (jnp.maximum(rb(i) * (tm // HALO) - 1, 0), 0))
    shape = lambda shp: jax.ShapeDtypeStruct(shp, F32)
    vec = (1, D_LRU)
    sq = (D_LRU, D_LRU)
    return pl.pallas_call(
        kern, name="lru_bwd", grid=(nt,),
        in_specs=[col(1), prev(1), col(2), row, hprev_spec, row] + wts,
        out_specs=[row, row, full((LRU_CONV, D_LRU)), full(vec), full(sq), full(vec), full(sq), full(vec), full(vec)],
        out_shape=[shape((t, D_LRU)), shape((t, D_LRU)), shape((LRU_CONV, D_LRU)), shape(vec), shape(sq),
                   shape(vec), shape(sq), shape(vec), shape(vec)],
        scratch_shapes=[pltpu.VMEM((tm + HALO, D_LRU), F32), pltpu.VMEM((tm + HALO, D_LRU), F32),
                        pltpu.VMEM((tm, D_LRU), F32), pltpu.VMEM((tm, D_LRU), F32),
                        pltpu.VMEM((SUBLANES, D_LRU), F32), pltpu.VMEM((SUBLANES, D_LRU), F32)],
        compiler_params=_params(("arbitrary",)),
    )(proj, proj, proj, h, h, dout, cw, cb, wx, bx, wa, ba, ap)


def _assemble_dproj(dq, dk, dv, du, dxc, dgate, cos, sin_s, cw):
    t = dq.shape[0]
    tm = _row_tile(t, TM)
    nt = t // tm

    def kern(dq_ref, dk_ref, dv_ref, du_ref, dx_ref, dn_ref, dg_ref, c_ref, s_ref, cw_ref, o_ref, b_ref, ext):
        i = pl.program_id(0)

        @pl.when(i == 0)
        def _():
            b_ref[...] = jnp.zeros_like(b_ref)

        def put(lo, val):
            hi = lo + val.shape[1]
            o_ref[:, lo:hi] = val.astype(BF16)
            b_ref[:, lo:hi] += jnp.sum(val, axis=0, keepdims=True)

        c = c_ref[...]
        s = s_ref[...]
        for ch in range(4):
            x = dq_ref[:, ch * 128:(ch + 1) * 128] * (HEAD_DIM ** -0.5)
            put(ch * 128, x * c - _rope_swap(x) * s)
        x = dk_ref[...]
        put(512, x * c - _rope_swap(x) * s)
        put(640, dv_ref[...])
        put(768, du_ref[...])
        ext[0:tm, :] = dx_ref[...]
        ext[tm:, :] = jnp.where(i < nt - 1, dn_ref[...], 0.0)
        put(1024, sum(cw_ref[k:k + 1, :] * ext[pl.ds(LRU_CONV - 1 - k, tm), :] for k in range(LRU_CONV)))
        put(1280, dg_ref[...])

    row = lambda w: pl.BlockSpec((tm, w), lambda i: (i, 0))
    nxt = pl.BlockSpec((HALO, D_LRU), lambda i: (jnp.minimum((i + 1) * (tm // HALO), t // HALO - 1), 0))
    return pl.pallas_call(
        kern, name="assemble_dproj", grid=(nt,),
        in_specs=[row(512), row(128), row(128), row(256), row(256), nxt, row(256), row(128), row(128),
                  pl.BlockSpec((LRU_CONV, D_LRU), lambda i: (0, 0))],
        out_specs=[row(D_IN), pl.BlockSpec((1, D_IN), lambda i: (0, 0))],
        out_shape=[jax.ShapeDtypeStruct((t, D_IN), BF16), jax.ShapeDtypeStruct((1, D_IN), F32)],
        scratch_shapes=[pltpu.VMEM((tm + HALO, D_LRU), F32)],
        compiler_params=_params(("arbitrary",)),
    )(dq, dk, dv, du, dxc, dxc, dgate, cos, sin_s, cw)


def _blockdiag_s5(bbar_re, bbar_im, c_re, c_im):
    eye = jnp.eye(S5_GROUPS, dtype=F32)
    b_of = lambda m: jnp.einsum('gpc,gh->gchp', m, eye).reshape(D_S5, N_STATE)
    c_of = lambda m: jnp.einsum('gcp,gh->gphc', m, eye).reshape(N_STATE, D_S5)
    bmat = jnp.concatenate([b_of(bbar_re), b_of(bbar_im)], axis=1)
    cmat = jnp.concatenate([c_of(c_re), -c_of(c_im)], axis=0)
    return bmat, cmat


def _s5_prepare(a_re, a_im, b_re, b_im, c_re, c_im, log_dt):
    lam_re = jnp.minimum(a_re, -1e-4)
    lam_im = a_im
    dt = jnp.exp(log_dt)[:, None]
    decay = jnp.exp(dt * lam_re)
    ang = dt * lam_im
    abar_re = decay * jnp.cos(ang)
    abar_im = decay * jnp.sin(ang)
    den = jnp.square(lam_re) + jnp.square(lam_im)
    nr = abar_re - 1.0
    ni = abar_im
    coef_re = (nr * lam_re + ni * lam_im) / den
    coef_im = (ni * lam_re - nr * lam_im) / den
    bbar_re = coef_re[..., None] * b_re - coef_im[..., None] * b_im
    bbar_im = coef_re[..., None] * b_im + coef_im[..., None] * b_re
    bmat, cmat = _blockdiag_s5(bbar_re, bbar_im, c_re, c_im)
    return abar_re.reshape(N_STATE), abar_im.reshape(N_STATE), bmat, cmat


def _blockdiag_lru(w):
    eye = jnp.eye(LRU_HEADS, dtype=F32)
    return jnp.einsum('hij,hk->hikj', w, eye).reshape(D_LRU, D_LRU)


def _rope_tables(t):
    inv_freq = ROPE_THETA ** (-jnp.arange(0, HEAD_DIM, 2, dtype=F32) / HEAD_DIM)
    ang = jnp.arange(t, dtype=F32)[:, None] * inv_freq[None, :]
    cos, sin = jnp.cos(ang), jnp.sin(ang)
    return jnp.tile(jnp.concatenate([cos, cos], axis=1), (1, 2)), jnp.tile(jnp.concatenate([-sin, sin], axis=1), (1, 2))


def _vec(v):
    return v.reshape(1, -1)


def _layer_weights(p):
    abar_re, abar_im, bmat, cmat = _s5_prepare(p['s5_a_re'], p['s5_a_im'], p['s5_b_re'], p['s5_b_im'],
                                               p['s5_c_re'], p['s5_c_im'], p['s5_log_dt'])
    return dict(
        coef_f=_s5_coefs(abar_re, abar_im, False), coef_b=_s5_coefs(abar_re, abar_im, True),
        bmat=bmat.astype(BF16), cmat=cmat.astype(BF16),
        wx=_blockdiag_lru(p['lru_wx']).astype(BF16), wa=_blockdiag_lru(p['lru_wa']).astype(BF16),
        gw=p['s5_glu_w'].astype(BF16))


def _layer_fwd(x, xb, p, w, cos, sin_s):
    t = x.shape[0]
    tm = _row_tile(t, TM)
    qkv, uxg = _in_proj(xb, p['w_in'], _vec(p['b_in']), cos, sin_s)
    ya, lse = _attn_fwd(qkv, _vec(p['attn_sinks']))
    h5, ys = _s5_fwd(uxg, w['bmat'], w['coef_f'], w['cmat'], _vec(p['s5_d']), w['gw'], _vec(p['s5_glu_b']))
    lru_w = (p['lru_conv_w'], _vec(p['lru_conv_b']), w['wx'], _vec(p['lru_bx']), w['wa'], _vec(p['lru_ba']),
             _vec(p['lru_a_param']))
    yl, hl = _lru_fwd(uxg, *lru_w)
    mix = _rms_fwd(ya, ys, yl, _vec(p['mix_norm_g']))
    x1, x1b, xhat1, rstd1 = _matmul_ln(
        "out_proj_ln", mix, p['w_out'], _vec(p['b_out']), x, _vec(p['ln1_g']), _vec(p['ln1_b']),
        a_blk=(tm, D_MODEL), a_map=lambda i: (i, 0), w_blk=(D_MODEL, D_MODEL), parts=None)
    gpre, gconv, up, hmid = _ffn_hidden_fwd(x1b, p['ffn_w_gate'], p['ffn_w_up'], p['ffn_conv_w'], p['ffn_conv_b'])
    x2, x2b, xhat2, rstd2 = _matmul_ln(
        "ffn_down_ln", hmid, p['ffn_w_down'], jnp.zeros((1, D_MODEL), F32), x1, _vec(p['ln2_g']), _vec(p['ln2_b']),
        a_blk=(N_CHIPS, tm, FF_SH), a_map=lambda i: (0, i, 0), w_blk=(N_CHIPS, FF_SH, D_MODEL), parts=N_CHIPS)
    saved = dict(xb=xb, uxg=uxg, qkv=qkv, ya=ya, lse=lse, h5=h5, ys=ys, yl=yl, hl=hl, mix=mix, x1b=x1b, xhat1=xhat1,
                 rstd1=rstd1, gpre=gpre, gconv=gconv, up=up, hmid=hmid, xhat2=xhat2, rstd2=rstd2, lru_w=lru_w)
    return x2, x2b, saved


def _layer_bwd(dx2, s, p, w, cos, sin_s):
    t = dx2.shape[0]
    tk = _row_tile(t, TMM)
    nk = t // tk
    g = {}
    dr2, dr2b, g['ln2_g'], g['ln2_b'], _ = _ln_bwd(dx2, s['xhat2'], s['rstd2'], _vec(p['ln2_g']))
    dup, dgpre, g['ffn_conv_w'], g['ffn_conv_b'] = _ffn_hidden_bwd(
        dr2b, s['gpre'], s['gconv'], s['up'], p['ffn_w_down'], p['ffn_conv_w'])
    g['ffn_w_down'] = _matmul("d_w_down", s['hmid'], dr2b, a_blk=(None, tk, FF_SH), a_map=lambda i, j, k: (i, k, 0),
                              b_blk=(tk, D_MODEL), b_map=lambda i, j, k: (k, 0), out_shape=(N_CHIPS, FF_SH, D_MODEL),
                              o_blk=(None, FF_SH, D_MODEL), o_map=lambda i, j: (i, 0, 0), grid=(N_CHIPS, 1, nk), dims=TN)
    d_ffn_w = lambda name, dact: _matmul(
        name, s['x1b'], dact, a_blk=(tk, D_MODEL), a_map=lambda i, j, k: (k, 0), b_blk=(None, tk, FF_SH),
        b_map=lambda i, j, k: (j, k, 0), out_shape=(N_CHIPS, D_MODEL, FF_SH), o_blk=(None, D_MODEL, FF_SH),
        o_map=lambda i, j: (j, 0, 0), grid=(1, N_CHIPS, nk), dims=TN)
    g['ffn_w_gate'] = d_ffn_w("d_w_gate", dgpre)
    g['ffn_w_up'] = d_ffn_w("d_w_up", dup)
    dx1 = _matmul("d_x1", dgpre, p['ffn_w_gate'], pair2=(dup, p['ffn_w_up']), a_blk=(None, tk, FF_SH),
                  a_map=lambda i, j, k: (k, i, 0), b_blk=(None, D_MODEL, FF_SH), b_map=lambda i, j, k: (k, 0, 0),
                  out_shape=(t, D_MODEL), o_blk=(tk, D_MODEL), o_map=lambda i, j: (i, 0), grid=(nk, 1, N_CHIPS),
                  dims=NT, add=dr2, add_scale=ALPHA)
    dr1, dr1b, g['ln1_g'], g['ln1_b'], g['b_out'] = _ln_bwd(dx1, s['xhat1'], s['rstd1'], _vec(p['ln1_g']))
    g['w_out'] = _matmul("d_w_out", s['mix'], dr1b, a_blk=(tk, D_MODEL), a_map=lambda i, j, k: (k, 0),
                         b_blk=(tk, D_MODEL), b_map=lambda i, j, k: (k, 0), out_shape=(D_MODEL, D_MODEL),
                         o_blk=(D_MODEL, D_MODEL), o_map=lambda i, j: (0, 0), grid=(1, 1, nk), dims=TN)
    dmix = _matmul("d_mix", dr1b, p['w_out'], a_blk=(tk, D_MODEL), a_map=lambda i, j, k: (i, 0),
                   b_blk=(D_MODEL, D_MODEL), b_map=lambda i, j, k: (0, 0), out_shape=(t, D_MODEL),
                   o_blk=(tk, D_MODEL), o_map=lambda i, j: (i, 0), grid=(nk, 1, 1), dims=NT)
    dya, dys, dyl, g['mix_norm_g'] = _rms_bwd(dmix, s['ya'], s['ys'], s['yl'], _vec(p['mix_norm_g']))
    dq, dk, dv, g['attn_sinks'] = _attn_bwd(s['qkv'], s['ya'], dya, s['lse'], _vec(p['attn_sinks']))
    du, dcmat, dbmat, dabar, g['s5_d'], g['s5_glu_w'], g['s5_glu_b'] = _s5_bwd(
        s['uxg'], s['h5'], dys, w['bmat'], w['coef_b'], w['cmat'], _vec(p['s5_d']), w['gw'], _vec(p['s5_glu_b']))
    (dxc, dgate, g['lru_conv_w'], g['lru_conv_b'], dwx, g['lru_bx'], dwa, g['lru_ba'],
     g['lru_a_param']) = _lru_bwd(s['uxg'], s['hl'], dyl, *s['lru_w'])
    dproj, g['b_in'] = _assemble_dproj(dq, dk, dv, du, dxc, dgate, cos, sin_s, p['lru_conv_w'])
    g['w_in'] = _matmul("d_w_in", s['xb'], dproj, a_blk=(tk, D_MODEL), a_map=lambda i, j, k: (k, 0),
                        b_blk=(tk, IN_SH), b_map=lambda i, j, k: (k, j), out_shape=(N_CHIPS, D_MODEL, IN_SH),
                        o_blk=(None, D_MODEL, IN_SH), o_map=lambda i, j: (j, 0, 0), grid=(1, N_CHIPS, nk), dims=TN)
    dx = _matmul("d_x", dproj, p['w_in'], a_blk=(tk, IN_SH), a_map=lambda i, j, k: (i, k),
                 b_blk=(None, D_MODEL, IN_SH), b_map=lambda i, j, k: (k, 0, 0), out_shape=(t, D_MODEL),
                 o_blk=(tk, D_MODEL), o_map=lambda i, j: (i, 0), grid=(nk, 1, N_CHIPS), dims=NT,
                 add=dr1, add_scale=ALPHA)
    return dx, _param_chain(g, p, dabar, dbmat, dcmat, dwx, dwa)


def _layer_fwd_v1(x, p, w, cos, sin_s):
    t = x.shape[0]
    nt = t // _row_tile(t, TM)
    tm = t // nt
    proj = _matmul("in_proj", x, p['w_in'], a_blk=(tm, D_MODEL), a_map=lambda i, j, k: (i, 0),
                   b_blk=(None, D_MODEL, IN_SH), b_map=lambda i, j, k: (j, 0, 0), out_shape=(t, D_IN),
                   o_blk=(tm, IN_SH), o_map=lambda i, j: (i, j), grid=(nt, N_CHIPS, 1), dims=NN,
                   bias=_vec(p['b_in']), bias_blk=(1, IN_SH), bias_map=lambda i, j, k: (0, j))
    qkv = _qkv_post(proj, cos, sin_s)
    ya, lse = _attn_fwd(qkv, _vec(p['attn_sinks']))
    h5, ys = _s5_fwd(proj, w['bmat'], w['coef_f'], w['cmat'], _vec(p['s5_d']), w['gw'], _vec(p['s5_glu_b']))
    lru_w = (p['lru_conv_w'], _vec(p['lru_conv_b']), w['wx'], _vec(p['lru_bx']), w['wa'], _vec(p['lru_ba']),
             _vec(p['lru_a_param']))
    yl, hl = _lru_fwd(proj, *lru_w)
    mix = _rms_fwd(ya, ys, yl, _vec(p['mix_norm_g']))
    f1 = _matmul("out_proj", mix, p['w_out'], a_blk=(tm, D_MODEL), a_map=lambda i, j, k: (i, 0),
                 b_blk=(D_MODEL, D_MODEL), b_map=lambda i, j, k: (0, 0), out_shape=(t, D_MODEL),
                 o_blk=(tm, D_MODEL), o_map=lambda i, j: (i, 0), grid=(nt, 1, 1), dims=NN,
                 bias=_vec(p['b_out']), bias_blk=(1, D_MODEL), bias_map=lambda i, j, k: (0, 0))
    x1, xhat1, rstd1 = _ln_fwd(x, f1, _vec(p['ln1_g']), _vec(p['ln1_b']))
    ffn_in = lambda name, wmat: _matmul(
        name, x1, wmat, a_blk=(tm, D_MODEL), a_map=lambda i, j, k: (i, 0), b_blk=(None, D_MODEL, FF_SH),
        b_map=lambda i, j, k: (j, 0, 0), out_shape=(N_CHIPS, t, FF_SH), o_blk=(None, tm, FF_SH),
        o_map=lambda i, j: (j, i, 0), grid=(nt, N_CHIPS, 1), dims=NN)
    gpre = ffn_in("ffn_gate", p['ffn_w_gate'])
    up = ffn_in("ffn_up", p['ffn_w_up'])
    hmid = _ffn_mid_fwd(gpre, up, p['ffn_conv_w'], p['ffn_conv_b'])
    f2 = _matmul("ffn_down", hmid, p['ffn_w_down'], a_blk=(None, tm, FF_SH), a_map=lambda i, j, k: (k, i, 0),
                 b_blk=(None, FF_SH, D_MODEL), b_map=lambda i, j, k: (k, 0, 0), out_shape=(t, D_MODEL),
                 o_blk=(tm, D_MODEL), o_map=lambda i, j: (i, 0), grid=(nt, 1, N_CHIPS), dims=NN)
    x2, xhat2, rstd2 = _ln_fwd(x1, f2, _vec(p['ln2_g']), _vec(p['ln2_b']))
    saved = dict(x=x, proj=proj, qkv=qkv, ya=ya, lse=lse, h5=h5, ys=ys, yl=yl, hl=hl, mix=mix, x1=x1, xhat1=xhat1,
                 rstd1=rstd1, gpre=gpre, up=up, xhat2=xhat2, rstd2=rstd2, lru_w=lru_w)
    return x2, saved


def _param_chain(g, p, dabar, dbmat, dcmat, dwx, dwa):
    s5_names = ('s5_a_re', 's5_a_im', 's5_b_re', 's5_b_im', 's5_c_re', 's5_c_im', 's5_log_dt')
    _, s5_vjp = jax.vjp(_s5_prepare, *[p[n] for n in s5_names])
    for n, val in zip(s5_names, s5_vjp((dabar[0], dabar[1], dbmat, dcmat))):
        g[n] = val
    g['lru_wx'] = jax.vjp(_blockdiag_lru, p['lru_wx'])[1](dwx)[0]
    g['lru_wa'] = jax.vjp(_blockdiag_lru, p['lru_wa'])[1](dwa)[0]
    return g


def _layer_bwd_v1(dx2, s, p, w, cos, sin_s):
    t = dx2.shape[0]
    nt = t // _row_tile(t, TM)
    tm = t // nt
    g = {}
    dr2, g['ln2_g'], g['ln2_b'], _ = _ln_bwd(dx2, s['xhat2'], s['rstd2'], _vec(p['ln2_g']))
    dhmid = _matmul("d_hmid", dr2, p['ffn_w_down'], a_blk=(tm, D_MODEL), a_map=lambda i, j, k: (i, 0),
                    b_blk=(None, FF_SH, D_MODEL), b_map=lambda i, j, k: (j, 0, 0), out_shape=(N_CHIPS, t, FF_SH),
                    o_blk=(None, tm, FF_SH), o_map=lambda i, j: (j, i, 0), grid=(nt, N_CHIPS, 1), dims=NT)
    hmid, dup, dgc, g['ffn_conv_w'], g['ffn_conv_b'] = _ffn_mid_bwd(s['gpre'], s['up'], dhmid, p['ffn_conv_w'],
                                                                    p['ffn_conv_b'])
    g['ffn_w_down'] = _matmul("d_w_down", hmid, dr2, a_blk=(None, tm, FF_SH), a_map=lambda i, j, k: (i, k, 0),
                              b_blk=(tm, D_MODEL), b_map=lambda i, j, k: (k, 0), out_shape=(N_CHIPS, FF_SH, D_MODEL),
                              o_blk=(None, FF_SH, D_MODEL), o_map=lambda i, j: (i, 0, 0), grid=(N_CHIPS, 1, nt), dims=TN)
    dgpre = _ffn_conv_t(dgc, p['ffn_conv_w'])
    d_ffn_w = lambda name, dact: _matmul(
        name, s['x1'], dact, a_blk=(tm, D_MODEL), a_map=lambda i, j, k: (k, 0), b_blk=(None, tm, FF_SH),
        b_map=lambda i, j, k: (j, k, 0), out_shape=(N_CHIPS, D_MODEL, FF_SH), o_blk=(None, D_MODEL, FF_SH),
        o_map=lambda i, j: (j, 0, 0), grid=(1, N_CHIPS, nt), dims=TN)
    g['ffn_w_gate'] = d_ffn_w("d_w_gate", dgpre)
    g['ffn_w_up'] = d_ffn_w("d_w_up", dup)
    d_ffn_x = lambda name, dact, wmat, add, scale: _matmul(
        name, dact, wmat, a_blk=(None, tm, FF_SH), a_map=lambda i, j, k: (k, i, 0), b_blk=(None, D_MODEL, FF_SH),
        b_map=lambda i, j, k: (k, 0, 0), out_shape=(t, D_MODEL), o_blk=(tm, D_MODEL), o_map=lambda i, j: (i, 0),
        grid=(nt, 1, N_CHIPS), dims=NT, add=add, add_scale=scale)
    dx1 = d_ffn_x("d_x1_gate", dgpre, p['ffn_w_gate'], dr2, ALPHA)
    dx1 = d_ffn_x("d_x1_up", dup, p['ffn_w_up'], dx1, 1.0)
    dr1, g['ln1_g'], g['ln1_b'], g['b_out'] = _ln_bwd(dx1, s['xhat1'], s['rstd1'], _vec(p['ln1_g']))
    g['w_out'] = _matmul("d_w_out", s['mix'], dr1, a_blk=(tm, D_MODEL), a_map=lambda i, j, k: (k, 0),
                         b_blk=(tm, D_MODEL), b_map=lambda i, j, k: (k, 0), out_shape=(D_MODEL, D_MODEL),
                         o_blk=(D_MODEL, D_MODEL), o_map=lambda i, j: (0, 0), grid=(1, 1, nt), dims=TN)
    dmix = _matmul("d_mix", dr1, p['w_out'], a_blk=(tm, D_MODEL), a_map=lambda i, j, k: (i, 0),
                   b_blk=(D_MODEL, D_MODEL), b_map=lambda i, j, k: (0, 0), out_shape=(t, D_MODEL),
                   o_blk=(tm, D_MODEL), o_map=lambda i, j: (i, 0), grid=(nt, 1, 1), dims=NT)
    dya, dys, dyl, g['mix_norm_g'] = _rms_bwd(dmix, s['ya'], s['ys'], s['yl'], _vec(p['mix_norm_g']))
    dq, dk, dv, g['attn_sinks'] = _attn_bwd(s['qkv'], s['ya'], dya, s['lse'], _vec(p['attn_sinks']))
    du, dcmat, dbmat, dabar, g['s5_d'], g['s5_glu_w'], g['s5_glu_b'] = _s5_bwd(
        s['proj'], s['h5'], dys, w['bmat'], w['coef_b'], w['cmat'], _vec(p['s5_d']), w['gw'], _vec(p['s5_glu_b']))
    (dxc, dgate, g['lru_conv_w'], g['lru_conv_b'], dwx, g['lru_bx'], dwa, g['lru_ba'],
     g['lru_a_param']) = _lru_bwd(s['proj'], s['hl'], dyl, *s['lru_w'])
    dproj, g['b_in'] = _assemble_dproj(dq, dk, dv, du, dxc, dgate, cos, sin_s, p['lru_conv_w'])
    g['w_in'] = _matmul("d_w_in", s['x'], dproj, a_blk=(tm, D_MODEL), a_map=lambda i, j, k: (k, 0),
                        b_blk=(tm, IN_SH), b_map=lambda i, j, k: (k, j), out_shape=(N_CHIPS, D_MODEL, IN_SH),
                        o_blk=(None, D_MODEL, IN_SH), o_map=lambda i, j: (j, 0, 0), grid=(1, N_CHIPS, nt), dims=TN)
    dx = _matmul("d_x", dproj, p['w_in'], a_blk=(tm, IN_SH), a_map=lambda i, j, k: (i, k),
                 b_blk=(None, D_MODEL, IN_SH), b_map=lambda i, j, k: (k, 0, 0), out_shape=(t, D_MODEL),
                 o_blk=(tm, D_MODEL), o_map=lambda i, j: (i, 0), grid=(nt, 1, N_CHIPS), dims=NT,
                 add=dr1, add_scale=ALPHA)
    s5_names = ('s5_a_re', 's5_a_im', 's5_b_re', 's5_b_im', 's5_c_re', 's5_c_im', 's5_log_dt')
    _, s5_vjp = jax.vjp(_s5_prepare, *[p[n] for n in s5_names])
    for n, val in zip(s5_names, s5_vjp((dabar[0], dabar[1], dbmat, dcmat))):
        g[n] = val
    g['lru_wx'] = jax.vjp(_blockdiag_lru, p['lru_wx'])[1](dwx)[0]
    g['lru_wa'] = jax.vjp(_blockdiag_lru, p['lru_wa'])[1](dwa)[0]
    return dx, g


ROW_TILE = 512


def _pick_rows(rows):
    for rt in range(min(rows, ROW_TILE), 0, -1):
        if rows % rt == 0 and (rt % 16 == 0 or rt == rows):
            return rt
    return rows


def _cast_bf16(a):
    a2 = a.reshape(-1, a.shape[-1])
    rows, c = a2.shape
    rt = _pick_rows(rows)

    def kern(a_ref, o_ref):
        o_ref[...] = a_ref[...].astype(BF16)

    spec = pl.BlockSpec((rt, c), lambda i: (i, 0))
    out = pl.pallas_call(kern, name="cast_bf16", grid=(rows // rt,), in_specs=[spec], out_specs=spec,
                         out_shape=jax.ShapeDtypeStruct((rows, c), BF16), compiler_params=_params(("parallel",)))(a2)
    return out.reshape(a.shape)


def _sum_parts(name, parts, shape):
    c = shape[-1]
    rows = math.prod(shape[:-1])
    rt = _pick_rows(rows)
    n = len(parts)

    def kern(*refs):
        acc = refs[0][...]
        for r in refs[1:n]:
            acc = acc + r[...]
        refs[n][...] = acc

    specs, args = [], []
    for arr, j in parts:
        if j is None:
            specs.append(pl.BlockSpec((rt, c), lambda i: (i, 0)))
            args.append(arr.reshape(rows, c))
        else:
            specs.append(pl.BlockSpec((None, rt, c), functools.partial(lambda i, jj: (jj, i, 0), jj=j)))
            args.append(arr.reshape(arr.shape[0], rows, c))
    out = pl.pallas_call(kern, name=name, grid=(rows // rt,), in_specs=specs,
                         out_specs=pl.BlockSpec((rt, c), lambda i: (i, 0)),
                         out_shape=jax.ShapeDtypeStruct((rows, c), F32), compiler_params=_params(("parallel",)))(*args)
    return out.reshape(shape)


def _adamw(name, w, g, m, v):
    shape = w.shape
    c = shape[-1]
    rows = math.prod(shape[:-1])
    rt = _pick_rows(rows)

    def kern(w_ref, g_ref, m_ref, v_ref, d_ref, nm_ref, nv_ref):
        g_ = g_ref[...]
        m_ = ADAM_B1 * m_ref[...] + (1.0 - ADAM_B1) * g_
        v_ = ADAM_B2 * v_ref[...] + (1.0 - ADAM_B2) * jnp.square(g_)
        m_hat = m_ / (1.0 - ADAM_B1 ** ADAM_STEP)
        v_hat = v_ / (1.0 - ADAM_B2 ** ADAM_STEP)
        d_ref[...] = -ADAM_LR * (m_hat / (jnp.sqrt(v_hat) + ADAM_EPS) + ADAM_WD * w_ref[...])
        nm_ref[...] = m_
        nv_ref[...] = v_

    spec = pl.BlockSpec((rt, c), lambda i: (i, 0))
    outs = pl.pallas_call(kern, name=name, grid=(rows // rt,), in_specs=[spec] * 4, out_specs=[spec] * 3,
                          out_shape=[jax.ShapeDtypeStruct((rows, c), F32)] * 3,
                          compiler_params=_params(("parallel",)))(*[a.reshape(rows, c) for a in (w, g, m, v)])
    return tuple(o.reshape(shape) for o in outs)


def _position():
    return lax.axis_index("x"), lax.axis_index("y"), lax.axis_index("c")


def _other_chips(x, y):
    return [(1 - x, y), (x, 1 - y), (1 - x, 1 - y)]


def _comm_call(name, kern, arrs, out_shapes, n_remote, n_local):
    return pl.pallas_call(
        kern, name=name, in_specs=[ANY] * len(arrs), out_specs=[ANY] * len(out_shapes), out_shape=out_shapes,
        scratch_shapes=[pltpu.SemaphoreType.DMA((n_remote,)), pltpu.SemaphoreType.DMA((n_remote,)),
                        pltpu.SemaphoreType.DMA((n_local,))],
    )(*arrs)


def _allgather_chips(arrs):
    n = len(arrs)

    def kern(*refs):
        ins, outs = refs[:n], refs[n:2 * n]
        send, recv, loc = refs[2 * n:]
        x, y, c = _position()
        me = 2 * x + y
        chips = _other_chips(x, y)
        own, sent = [], []
        for t in range(n):
            own.append(pltpu.make_async_copy(ins[t], outs[t].at[:, pl.ds(me, 1)], loc.at[t]))
            own[-1].start()
            for j, (px, py) in enumerate(chips):
                sent.append(pltpu.make_async_remote_copy(
                    src_ref=ins[t], dst_ref=outs[t].at[:, pl.ds(me, 1)], send_sem=send.at[3 * t + j],
                    recv_sem=recv.at[3 * t + j], device_id=(px, py, c), device_id_type=MESH))
                sent[-1].start()
        for t in range(n):
            for j, (px, py) in enumerate(chips):
                pltpu.make_async_remote_copy(
                    src_ref=ins[t], dst_ref=outs[t].at[:, pl.ds(2 * px + py, 1)], send_sem=send.at[3 * t + j],
                    recv_sem=recv.at[3 * t + j], device_id=(px, py, c), device_id_type=MESH).wait_recv()
        for cp in sent:
            cp.wait_send()
        for cp in own:
            cp.wait()

    outs = [jax.ShapeDtypeStruct((a.shape[0], N_CHIPS) + a.shape[2:], a.dtype) for a in arrs]
    return _comm_call("allgather_chips", kern, arrs, outs, 3 * n, n)


def _pair_exchange(arrs):
    n = len(arrs)

    def kern(*refs):
        ins, outs = refs[:n], refs[n:3 * n]
        send, recv, loc = refs[3 * n:]
        x, y, c = _position()
        own, sent = [], []
        for t in range(n):
            r2 = ins[t].shape[2] // 2
            own.append(pltpu.make_async_copy(ins[t].at[:, :, pl.ds(c * r2, r2)], outs[2 * t], loc.at[t]))
            own[-1].start()
            sent.append(pltpu.make_async_remote_copy(
                src_ref=ins[t].at[:, :, pl.ds((1 - c) * r2, r2)], dst_ref=outs[2 * t + 1], send_sem=send.at[t],
                recv_sem=recv.at[t], device_id=(x, y, 1 - c), device_id_type=MESH))
            sent[-1].start()
        for cp in sent:
            cp.wait()
        for cp in own:
            cp.wait()

    outs = []
    for a in arrs:
        half = jax.ShapeDtypeStruct(a.shape[:2] + (a.shape[2] // 2, a.shape[3]), a.dtype)
        outs += [half, half]
    return _comm_call("pair_exchange", kern, arrs, outs, n, n)


def _chip_scatter(arrs):
    n = len(arrs)

    def kern(*refs):
        ins, outs = refs[:n], refs[n:3 * n]
        send, recv, loc = refs[3 * n:]
        x, y, c = _position()
        me = 2 * x + y
        chips = _other_chips(x, y)
        own, sent = [], []
        for t in range(n):
            own.append(pltpu.make_async_copy(ins[t].at[:, pl.ds(me, 1)], outs[2 * t], loc.at[t]))
            own[-1].start()
            for j, (px, py) in enumerate(chips):
                sent.append(pltpu.make_async_remote_copy(
                    src_ref=ins[t].at[:, pl.ds(2 * px + py, 1)], dst_ref=outs[2 * t + 1].at[j],
                    send_sem=send.at[3 * t + j], recv_sem=recv.at[3 * t + j], device_id=(px, py, c),
                    device_id_type=MESH))
                sent[-1].start()
        for cp in sent:
            cp.wait()
        for cp in own:
            cp.wait()

    outs = []
    for a in arrs:
        one = (a.shape[0], 1) + a.shape[2:]
        outs += [jax.ShapeDtypeStruct(one, a.dtype), jax.ShapeDtypeStruct((3,) + one, a.dtype)]
    return _comm_call("chip_scatter", kern, arrs, outs, 3 * n, n)


def _pair_gather(arrs):
    n = len(arrs)

    def kern(*refs):
        ins, outs = refs[:n], refs[n:2 * n]
        send, recv, loc = refs[2 * n:]
        x, y, c = _position()
        own, sent = [], []
        for t in range(n):
            own.append(pltpu.make_async_copy(ins[t], outs[t].at[:, pl.ds(c, 1)], loc.at[t]))
            own[-1].start()
            sent.append(pltpu.make_async_remote_copy(
                src_ref=ins[t], dst_ref=outs[t].at[:, pl.ds(c, 1)], send_sem=send.at[t], recv_sem=recv.at[t],
                device_id=(x, y, 1 - c), device_id_type=MESH))
            sent[-1].start()
        for t in range(n):
            sent[t].wait_send()
            pltpu.make_async_remote_copy(
                src_ref=ins[t], dst_ref=outs[t].at[:, pl.ds(1 - c, 1)], send_sem=send.at[t], recv_sem=recv.at[t],
                device_id=(x, y, 1 - c), device_id_type=MESH).wait_recv()
        for cp in own:
            cp.wait()

    outs = [jax.ShapeDtypeStruct((a.shape[0], 2) + a.shape[2:], a.dtype) for a in arrs]
    return _comm_call("pair_gather", kern, arrs, outs, n, n)


_FLIPS = [(0, 0, 1), (1, 0, 0), (0, 1, 0), (1, 1, 0), (1, 0, 1), (0, 1, 1), (1, 1, 1)]


def _allgather_devices(v):
    def kern(v_ref, o_ref, send, recv, loc):
        x, y, c = _position()
        me = 4 * x + 2 * y + c
        peers = [((1 - x) if fx else x, (1 - y) if fy else y, (1 - c) if fc else c) for fx, fy, fc in _FLIPS]
        own = pltpu.make_async_copy(v_ref, o_ref.at[pl.ds(me, 1)], loc.at[0])
        own.start()
        sent = []
        for k, peer in enumerate(peers):
            sent.append(pltpu.make_async_remote_copy(
                src_ref=v_ref, dst_ref=o_ref.at[pl.ds(me, 1)], send_sem=send.at[k], recv_sem=recv.at[k],
                device_id=peer, device_id_type=MESH))
            sent[-1].start()
        for k, (px, py, pc) in enumerate(peers):
            pltpu.make_async_remote_copy(
                src_ref=v_ref, dst_ref=o_ref.at[pl.ds(4 * px + 2 * py + pc, 1)], send_sem=send.at[k],
                recv_sem=recv.at[k], device_id=(px, py, pc), device_id_type=MESH).wait_recv()
        for cp in sent:
            cp.wait_send()
        own.wait()

    out = jax.ShapeDtypeStruct((N_DEV,) + v.shape[1:], v.dtype)
    return _comm_call("allgather_devices", kern, [v], [out], len(_FLIPS), 1)[0]


def _exchange(name, arrs, out_shapes, n_local, n_remote, plan):
    n_in, n_out = len(arrs), len(out_shapes)

    def kern(*refs):
        ins, outs = refs[:n_in], refs[n_in:n_in + n_out]
        send, recv, loc = refs[n_in + n_out:]
        local, remote = plan(ins, outs, *_position())
        assert len(local) == n_local and len(remote) == n_remote
        own = [pltpu.make_async_copy(s, d, loc.at[k]) for k, (s, d) in enumerate(local)]
        for cp in own:
            cp.start()
        sent = [pltpu.make_async_remote_copy(src_ref=s, dst_ref=d, send_sem=send.at[k], recv_sem=recv.at[k],
                                             device_id=peer, device_id_type=MESH)
                for k, (s, d, peer, _) in enumerate(remote)]
        for cp in sent:
            cp.start()
        for k, (s, _, peer, landing) in enumerate(remote):
            pltpu.make_async_remote_copy(src_ref=s, dst_ref=landing, send_sem=send.at[k], recv_sem=recv.at[k],
                                         device_id=peer, device_id_type=MESH).wait_recv()
        for cp in sent:
            cp.wait_send()
        for cp in own:
            cp.wait()

    return pl.pallas_call(
        kern, name=name, in_specs=[ANY] * n_in, out_specs=[ANY] * n_out, out_shape=out_shapes,
        scratch_shapes=[pltpu.SemaphoreType.DMA((n_remote,)), pltpu.SemaphoreType.DMA((n_remote,)),
                        pltpu.SemaphoreType.DMA((max(n_local, 1),))],
    )(*arrs)


def _allgather_chips(arrs):
    n = len(arrs)
    layers = arrs[0].shape[0]

    def plan(ins, outs, x, y, c):
        me = 2 * x + y
        local, remote = [], []
        for t in range(n):
            for l in range(layers):
                local.append((ins[t].at[l], outs[t].at[l, pl.ds(me, 1)]))
                for px, py in _other_chips(x, y):
                    remote.append((ins[t].at[l], outs[t].at[l, pl.ds(me, 1)], (px, py, c),
                                   outs[t].at[l, pl.ds(2 * px + py, 1)]))
        return local, remote

    outs = [jax.ShapeDtypeStruct((a.shape[0], N_CHIPS) + a.shape[2:], a.dtype) for a in arrs]
    return _exchange("allgather_chips", arrs, outs, n * layers, 3 * n * layers, plan)


def _pair_exchange(arrs):
    n = len(arrs)
    layers, shards = arrs[0].shape[:2]

    def plan(ins, outs, x, y, c):
        local, remote = [], []
        for t in range(n):
            r2 = ins[t].shape[2] // 2
            for l in range(layers):
                for s in range(shards):
                    local.append((ins[t].at[l, s, pl.ds(c * r2, r2)], outs[2 * t].at[l, s]))
                    remote.append((ins[t].at[l, s, pl.ds((1 - c) * r2, r2)], outs[2 * t + 1].at[l, s],
                                   (x, y, 1 - c), outs[2 * t + 1].at[l, s]))
        return local, remote

    outs = []
    for a in arrs:
        half = jax.ShapeDtypeStruct(a.shape[:2] + (a.shape[2] // 2, a.shape[3]), a.dtype)
        outs += [half, half]
    return _exchange("pair_exchange", arrs, outs, n * layers * shards, n * layers * shards, plan)


def _chip_scatter(arrs):
    n = len(arrs)
    layers = arrs[0].shape[0]

    def plan(ins, outs, x, y, c):
        me = 2 * x + y
        local, remote = [], []
        for t in range(n):
            for l in range(layers):
                local.append((ins[t].at[l, pl.ds(me, 1)], outs[2 * t].at[l]))
                for j, (px, py) in enumerate(_other_chips(x, y)):
                    remote.append((ins[t].at[l, pl.ds(2 * px + py, 1)], outs[2 * t + 1].at[j, l], (px, py, c),
                                   outs[2 * t + 1].at[j, l]))
        return local, remote

    outs = []
    for a in arrs:
        one = (a.shape[0], 1) + a.shape[2:]
        outs += [jax.ShapeDtypeStruct(one, a.dtype), jax.ShapeDtypeStruct((3,) + one, a.dtype)]
    return _exchange("chip_scatter", arrs, outs, n * layers, 3 * n * layers, plan)


def _pair_gather(arrs):
    n = len(arrs)
    layers = arrs[0].shape[0]

    def plan(ins, outs, x, y, c):
        local, remote = [], []
        for t in range(n):
            for l in range(layers):
                local.append((ins[t].at[l], outs[t].at[l, pl.ds(c, 1)]))
                remote.append((ins[t].at[l], outs[t].at[l, pl.ds(c, 1)], (x, y, 1 - c),
                               outs[t].at[l, pl.ds(1 - c, 1)]))
        return local, remote

    outs = [jax.ShapeDtypeStruct((a.shape[0], 2) + a.shape[2:], a.dtype) for a in arrs]
    return _exchange("pair_gather", arrs, outs, n * layers, n * layers, plan)


GATHER_PIECES = 4


def _allgather_devices(v):
    rq = v.shape[1] // GATHER_PIECES

    def plan(ins, outs, x, y, c):
        me = 4 * x + 2 * y + c
        local, remote = [], []
        for q in range(GATHER_PIECES):
            rows = pl.ds(q * rq, rq)
            local.append((ins[0].at[0, rows], outs[0].at[me, rows]))
            for fx, fy, fc in _FLIPS:
                px, py, pc = (1 - x) if fx else x, (1 - y) if fy else y, (1 - c) if fc else c
                remote.append((ins[0].at[0, rows], outs[0].at[me, rows], (px, py, pc),
                               outs[0].at[4 * px + 2 * py + pc, rows]))
        return local, remote

    out = jax.ShapeDtypeStruct((N_DEV,) + v.shape[1:], v.dtype)
    return _exchange("allgather_devices", [v], [out], GATHER_PIECES, GATHER_PIECES * len(_FLIPS), plan)[0]


WEIGHTS = ['w_in', 'b_in', 'attn_sinks', 's5_a_re', 's5_a_im', 's5_b_re', 's5_b_im', 's5_c_re', 's5_c_im', 's5_d',
           's5_log_dt', 's5_glu_w', 's5_glu_b', 'lru_conv_w', 'lru_conv_b', 'lru_wx', 'lru_bx', 'lru_wa', 'lru_ba',
           'lru_a_param', 'mix_norm_g', 'w_out', 'b_out', 'ln1_g', 'ln1_b', 'ffn_w_gate', 'ffn_w_up', 'ffn_conv_w',
           'ffn_conv_b', 'ffn_w_down', 'ln2_g', 'ln2_b']
BIG = ('w_in', 'w_out', 'ffn_w_gate', 'ffn_w_up', 'ffn_w_down')
SMALL = tuple(n for n in WEIGHTS if n not in BIG)
PACK_ROWS = ROW_TILE


def _pack(arrs):
    flat = jnp.concatenate([a.reshape(-1) for a in arrs])
    unit = 128 * PACK_ROWS
    size = -(-flat.shape[0] // unit) * unit
    return jnp.pad(flat, (0, size - flat.shape[0])).reshape(-1, 128)


def _unpack(packed, shapes):
    flat = packed.reshape(-1)
    out, pos = [], 0
    for shp in shapes:
        n = math.prod(shp)
        out.append(flat[pos:pos + n].reshape(shp))
        pos += n
    return out


def _pair_reduce(name, g):
    layers, shards, rows, cols = g.shape
    r2 = rows // 2
    rt = _pick_rows(r2)
    nr = r2 // rt
    nsteps = layers * shards * nr

    def kern(c_ref, mine_ref, other_ref, o_ref, buf, send, recv, credit):
        x, y, c = _position()
        sibling = (x, y, 1 - c)
        k = pl.program_id(0) * nr + pl.program_id(1)
        slot = k % 2

        @pl.when(k >= 2)
        def _():
            pl.semaphore_wait(credit, 1)

        cp = pltpu.make_async_remote_copy(src_ref=other_ref, dst_ref=buf.at[slot], send_sem=send.at[slot],
                                          recv_sem=recv.at[slot], device_id=sibling, device_id_type=MESH)
        cp.start()
        cp.wait_recv()
        o_ref[...] = mine_ref[...] + buf[slot]
        cp.wait_send()

        @pl.when(k + 2 < nsteps)
        def _():
            pl.semaphore_signal(credit, 1, device_id=sibling, device_id_type=MESH)

    blk = (1, rt, cols)
    grid_spec = pltpu.PrefetchScalarGridSpec(
        num_scalar_prefetch=1, grid=(layers * shards, nr),
        in_specs=[pl.BlockSpec(blk, lambda m, r, c_ref: (m, c_ref[0] * nr + r, 0)),
                  pl.BlockSpec(blk, lambda m, r, c_ref: (m, (1 - c_ref[0]) * nr + r, 0))],
        out_specs=pl.BlockSpec(blk, lambda m, r, c_ref: (m, r, 0)),
        scratch_shapes=[pltpu.VMEM((2,) + blk, F32), pltpu.SemaphoreType.DMA((2,)),
                        pltpu.SemaphoreType.DMA((2,)), pltpu.SemaphoreType.REGULAR])
    core = lax.axis_index("c").astype(jnp.int32).reshape(1)
    g3 = g.reshape(layers * shards, rows, cols)
    out = pl.pallas_call(
        kern, name=name, grid_spec=grid_spec,
        out_shape=jax.ShapeDtypeStruct((layers * shards, r2, cols), F32),
        compiler_params=_params(("arbitrary", "arbitrary")),
    )(core, g3, g3)
    return out.reshape(layers, shards, r2, cols)


def _reduce_big(grads):
    pair = [_pair_reduce("pair_reduce_" + n, g) for n, g in zip(BIG, grads)]
    scat = _chip_scatter(pair)
    chip = [_sum_parts("chip_sum", [(scat[2 * t], None)] + [(scat[2 * t + 1], j) for j in range(3)],
                       scat[2 * t].shape) for t in range(len(grads))]
    both = _pair_gather(chip)
    return [b.reshape(b.shape[0], 2 * b.shape[2], b.shape[3]) for b in both]


def _step(a):
    x = a['x'][0]
    target = a['loss_target'][0]
    t = x.shape[0]
    xi, yi, _ = _position()
    chip = 2 * xi + yi
    cos, sin_s = _rope_tables(t)

    gathered = _allgather_chips([_cast_bf16(a[n])[:, None] for n in BIG]
                                + [a[n][:, None] for n in ('s5_glu_w', 'lru_conv_w', 'ffn_conv_w')])
    full = dict(zip(BIG + ('s5_glu_w', 'lru_conv_w', 'ffn_conv_w'), gathered))

    def layer_params(l):
        p = {n: a[n][l] for n in SMALL}
        p['w_in'] = full['w_in'][l]
        p['w_out'] = full['w_out'][l].reshape(D_MODEL, D_MODEL)
        p['ffn_w_gate'] = full['ffn_w_gate'][l]
        p['ffn_w_up'] = full['ffn_w_up'][l]
        p['ffn_w_down'] = full['ffn_w_down'][l]
        p['s5_glu_w'] = full['s5_glu_w'][l].reshape(D_S5, D_S5)
        p['lru_conv_w'] = full['lru_conv_w'][l].transpose(1, 0, 2).reshape(LRU_CONV, D_LRU)
        p['ffn_conv_w'] = full['ffn_conv_w'][l]
        p['ffn_conv_b'] = a['ffn_conv_b'][l].reshape(N_CHIPS, 1, FF_SH)
        return p

    params = [layer_params(l) for l in range(DEPTH)]
    derived = [_layer_weights(p) for p in params]
    saved = []
    h, hb = x, _cast_bf16(x)
    for l in range(DEPTH):
        h, hb, s = _layer_fwd(h, hb, params[l], derived[l], cos, sin_s)
        saved.append(s)
    loss_part, dh = _loss_head(h, target)
    loss = lax.psum(loss_part[0, 0], ("x", "y", "c"))
    grads = [None] * DEPTH
    for l in reversed(range(DEPTH)):
        dh, grads[l] = _layer_bwd(dh, saved[l], params[l], derived[l], cos, sin_s)
    grad_x = dh[None]

    def stacked(n):
        return jnp.stack([grads[l][n] for l in range(DEPTH)])

    big_local = [stacked(n) for n in BIG]
    big_local[1] = big_local[1].reshape(DEPTH, N_CHIPS, OUT_SH, D_MODEL)
    grad = dict(zip(BIG, _reduce_big(big_local)))
    small_local = [stacked(n) for n in SMALL]
    packed = _allgather_devices(_pack(small_local)[None])
    total = _sum_parts("device_sum", [(packed, j) for j in range(N_DEV)], packed.shape[1:])
    small_sum = dict(zip(SMALL, _unpack(total, [g.shape for g in small_local])))
    for n in SMALL:
        g = small_sum[n]
        if n == 's5_glu_w':
            g = lax.dynamic_slice_in_dim(g, chip * (D_S5 // N_CHIPS), D_S5 // N_CHIPS, axis=1)
        elif n == 'lru_conv_w':
            g = lax.dynamic_slice_in_dim(g, chip * (D_LRU // N_CHIPS), D_LRU // N_CHIPS, axis=2)
        elif n == 'ffn_conv_w':
            g = lax.dynamic_index_in_dim(g, chip, axis=1, keepdims=False)
        grad[n] = g.reshape(a[n].shape)

    delta, new_m, new_v = {}, {}, {}
    for n in BIG:
        delta[n], new_m[n], new_v[n] = _adamw("adamw_" + n, a[n], grad[n], a['m_' + n], a['v_' + n])
    shapes = [a[n].shape for n in SMALL]
    outs = _adamw("adamw_small", _pack([a[n] for n in SMALL]), _pack([grad[n] for n in SMALL]),
                  _pack([a['m_' + n] for n in SMALL]), _pack([a['v_' + n] for n in SMALL]))
    for res, o in zip((delta, new_m, new_v), outs):
        res.update(zip(SMALL, _unpack(o, shapes)))
    return (loss, grad_x, *[grad[n] for n in WEIGHTS], *[delta[n] for n in WEIGHTS],
            *[new_m[n] for n in WEIGHTS], *[new_v[n] for n in WEIGHTS])


def kernel(x, w_in, b_in, attn_sinks, s5_a_re, s5_a_im, s5_b_re, s5_b_im, s5_c_re, s5_c_im, s5_d, s5_log_dt, s5_glu_w, s5_glu_b, lru_conv_w, lru_conv_b, lru_wx, lru_bx, lru_wa, lru_ba, lru_a_param, mix_norm_g, w_out, b_out, ln1_g, ln1_b, ffn_w_gate, ffn_w_up, ffn_conv_w, ffn_conv_b, ffn_w_down, ln2_g, ln2_b, loss_target, m_w_in, m_b_in, m_attn_sinks, m_s5_a_re, m_s5_a_im, m_s5_b_re, m_s5_b_im, m_s5_c_re, m_s5_c_im, m_s5_d, m_s5_log_dt, m_s5_glu_w, m_s5_glu_b, m_lru_conv_w, m_lru_conv_b, m_lru_wx, m_lru_bx, m_lru_wa, m_lru_ba, m_lru_a_param, m_mix_norm_g, m_w_out, m_b_out, m_ln1_g, m_ln1_b, m_ffn_w_gate, m_ffn_w_up, m_ffn_conv_w, m_ffn_conv_b, m_ffn_w_down, m_ln2_g, m_ln2_b, v_w_in, v_b_in, v_attn_sinks, v_s5_a_re, v_s5_a_im, v_s5_b_re, v_s5_b_im, v_s5_c_re, v_s5_c_im, v_s5_d, v_s5_log_dt, v_s5_glu_w, v_s5_glu_b, v_lru_conv_w, v_lru_conv_b, v_lru_wx, v_lru_bx, v_lru_wa, v_lru_ba, v_lru_a_param, v_mix_norm_g, v_w_out, v_b_out, v_ln1_g, v_ln1_b, v_ffn_w_gate, v_ffn_w_up, v_ffn_conv_w, v_ffn_conv_b, v_ffn_w_down, v_ln2_g, v_ln2_b):
    return _step(dict(locals()))
```

```python
import functools
import math

import jax
import jax.numpy as jnp
from jax import lax
from jax.experimental import pallas as pl
from jax.experimental.pallas import tpu as pltpu

F32 = jnp.float32
BF16 = jnp.bfloat16
MESH = pl.DeviceIdType.MESH
ANY = pl.BlockSpec(memory_space=pl.ANY)

D_MODEL = 1024
DEPTH = 4
HEAD_DIM = 64
N_Q_HEADS = 8
N_KV_HEADS = 2
Q_PER_KV = 4
D_ATTN = 512
D_KV = 128
ATTN_BLOCK = 128
ROPE_THETA = 10000.0
D_S5 = 256
S5_GROUP = 16
S5_GROUPS = 16
S5_STATE = 64
N_STATE = S5_GROUPS * S5_STATE
D_LRU = 256
LRU_HEADS = 4
LRU_HEAD_DIM = 64
LRU_CONV = 4
LRU_C = 8.0
D_IN = 1536
D_FF = 2816
FFN_CONV = 3
N_CHIPS = 4
N_DEV = 8
IN_SH = D_IN // N_CHIPS
FF_SH = D_FF // N_CHIPS
OUT_SH = D_MODEL // N_CHIPS
ALPHA = (2 * DEPTH) ** 0.25
LN_EPS = 1e-5
RMS_EPS = 1e-6
ADAM_LR = 0.001
ADAM_B1 = 0.9
ADAM_B2 = 0.999
ADAM_EPS = 1e-08
ADAM_WD = 0.01
ADAM_STEP = 10

SUBLANES = 8
VMEM_MB = 56


def _params(sem):
    return pltpu.CompilerParams(dimension_semantics=sem, vmem_limit_bytes=VMEM_MB << 20)


def _row_tile(t, pref):
    return min(t, pref)


def _matmul(name, a, b, *, a_blk, a_map, b_blk, b_map, out_shape, o_blk, o_map, grid, dims,
            out_dtype=F32, bias=None, bias_blk=None, bias_map=None, add=None, add_scale=1.0, pair2=None,
            into=None):
    nk = grid[2]
    acc_shape = tuple(d for d in o_blk if d is not None)
    n_in = 2 if pair2 is None else 4

    def kern(*refs):
        p = n_in
        bias_ref = add_ref = None
        if bias is not None:
            bias_ref = refs[p]
            p += 1
        if add is not None:
            add_ref = refs[p]
            p += 1
        if into is not None:
            p += 1
        o_ref, acc = refs[p], refs[p + 1]
        k = pl.program_id(2)

        def product():
            r = _dot(refs[0][...].astype(BF16), refs[1][...].astype(BF16), dims)
            if pair2 is not None:
                r = r + _dot(refs[2][...].astype(BF16), refs[3][...].astype(BF16), dims)
            return r

        def finish(r):
            if bias_ref is not None:
                r = r + bias_ref[...]
            if add_ref is not None:
                r = r + add_scale * add_ref[...]
            o_ref[...] = r.astype(out_dtype)

        if nk == 1:
            finish(product())
        else:
            @pl.when(k == 0)
            def _():
                acc[...] = jnp.zeros_like(acc)

            acc[...] += product()

            @pl.when(k == nk - 1)
            def _():
                finish(acc[...])

    in_specs = [pl.BlockSpec(a_blk, a_map), pl.BlockSpec(b_blk, b_map)]
    args = [a, b]
    if pair2 is not None:
        in_specs += [pl.BlockSpec(a_blk, a_map), pl.BlockSpec(b_blk, b_map)]
        args += list(pair2)
    if bias is not None:
        in_specs.append(pl.BlockSpec(bias_blk, bias_map))
        args.append(bias)
    if add is not None:
        in_specs.append(pl.BlockSpec(o_blk, lambda i, j, k: o_map(i, j)))
        args.append(add)
    aliases = {}
    if into is not None:
        aliases = {len(args): 0}
        in_specs.append(ANY)
        args.append(into)
    return pl.pallas_call(
        kern, name=name, grid=grid, in_specs=in_specs,
        out_specs=pl.BlockSpec(o_blk, lambda i, j, k: o_map(i, j)),
        out_shape=jax.ShapeDtypeStruct(out_shape, out_dtype),
        scratch_shapes=[pltpu.VMEM(acc_shape if nk > 1 else (SUBLANES, 128), F32)],
        input_output_aliases=aliases,
        compiler_params=_params(("parallel", "parallel", "arbitrary")),
    )(*args)


NN = ((1,), (0,))
NT = ((1,), (1,))
TN = ((0,), (0,))
TM = 512


def _sigmoid(x):
    return 0.5 * jnp.tanh(0.5 * x) + 0.5


_GELU_C = math.sqrt(2.0 / math.pi)


def _gelu(x):
    return 0.5 * x * (1.0 + jnp.tanh(_GELU_C * (x + 0.044715 * x * x * x)))


def _gelu_grad(x):
    th = jnp.tanh(_GELU_C * (x + 0.044715 * x * x * x))
    return 0.5 * (1.0 + th) + 0.5 * x * (1.0 - th * th) * _GELU_C * (1.0 + 3 * 0.044715 * x * x)


def _rope_swap(t):
    lane = lax.broadcasted_iota(jnp.int32, t.shape, 1)
    lo = (lane % HEAD_DIM) < (HEAD_DIM // 2)
    return jnp.where(lo, pltpu.roll(t, 128 - HEAD_DIM // 2, 1), pltpu.roll(t, HEAD_DIM // 2, 1))


D_QKV = D_ATTN + 2 * D_KV
TMM = 1024


def _in_proj(xb, w_in, b_in, cos, sin_s, layer):
    t = xb.shape[0]
    tm = _row_tile(t, TMM)

    def kern(x_ref, w_ref, b_ref, c_ref, s_ref, q_ref, u_ref):
        x = x_ref[...]
        c = c_ref[...]
        s = s_ref[...]
        for j in range(N_CHIPS):
            pj = _dot(x, w_ref[j], NN) + b_ref[:, j * IN_SH:(j + 1) * IN_SH]
            for ch in range(IN_SH // 128):
                col = j * IN_SH + ch * 128
                v = pj[:, ch * 128:(ch + 1) * 128]
                if col < D_ATTN + D_KV:
                    v = v * c + _rope_swap(v) * s
                if col < D_ATTN:
                    v = v * (HEAD_DIM ** -0.5)
                if col < D_QKV:
                    q_ref[:, col:col + 128] = v.astype(BF16)
                else:
                    u_ref[:, col - D_QKV:col - D_QKV + 128] = v

    row = lambda w: pl.BlockSpec((tm, w), lambda i: (i, 0))
    return pl.pallas_call(
        kern, name="in_proj", grid=(t // tm,),
        in_specs=[row(D_MODEL), pl.BlockSpec((None, N_CHIPS, D_MODEL, IN_SH), lambda i: (layer, 0, 0, 0)),
                  pl.BlockSpec((1, D_IN), lambda i: (0, 0)), row(128), row(128)],
        out_specs=[row(D_QKV), row(D_IN - D_QKV)],
        out_shape=[jax.ShapeDtypeStruct((t, D_QKV), BF16), jax.ShapeDtypeStruct((t, D_IN - D_QKV), F32)],
        compiler_params=_params(("parallel",)),
    )(xb, w_in, b_in, cos, sin_s)


def _attn_mask(i):
    qi = lax.broadcasted_iota(jnp.int32, (ATTN_BLOCK, 2 * ATTN_BLOCK), 0)
    si = lax.broadcasted_iota(jnp.int32, (ATTN_BLOCK, 2 * ATTN_BLOCK), 1)
    diff = qi + ATTN_BLOCK - si
    return (diff >= 0) & (diff < ATTN_BLOCK) & ((si >= ATTN_BLOCK) | (i > 0))


def _attn_fwd(qkv, sinks):
    t = qkv.shape[0]
    nb = t // ATTN_BLOCK

    def kern(q_ref, kp_ref, kc_ref, vp_ref, vc_ref, s_ref, o_ref, l_ref):
        i = pl.program_id(0)
        valid = _attn_mask(i)
        kband = jnp.concatenate([kp_ref[...], kc_ref[...]], axis=0)
        vband = jnp.concatenate([vp_ref[...], vc_ref[...]], axis=0)
        ks = [kband[:, kh * HEAD_DIM:(kh + 1) * HEAD_DIM] for kh in range(N_KV_HEADS)]
        vs = [vband[:, kh * HEAD_DIM:(kh + 1) * HEAD_DIM] for kh in range(N_KV_HEADS)]
        scores = [_dot(q_ref[:, h * HEAD_DIM:(h + 1) * HEAD_DIM], ks[h // Q_PER_KV], NT) for h in range(N_Q_HEADS)]
        probs, lses = [], []
        for h in range(N_Q_HEADS):
            s = jnp.where(valid, scores[h], -jnp.inf)
            sink = s_ref[0:1, h:h + 1]
            m = jnp.maximum(jnp.max(s, axis=-1, keepdims=True), sink)
            e = jnp.exp(s - m)
            denom = jnp.sum(e, axis=-1, keepdims=True) + jnp.exp(sink - m)
            probs.append((e / denom).astype(BF16))
            lses.append(m + jnp.log(denom))
        outs = [_dot(probs[h], vs[h // Q_PER_KV], NN) for h in range(N_Q_HEADS)]
        for h in range(N_Q_HEADS):
            o_ref[:, h * HEAD_DIM:(h + 1) * HEAD_DIM] = outs[h]
            l_ref[:, h:h + 1] = lses[h]

    blk = lambda w, f: pl.BlockSpec((ATTN_BLOCK, w), f)
    return pl.pallas_call(
        kern, name="attn_fwd", grid=(nb,),
        in_specs=[blk(512, lambda i: (i, 0)),
                  blk(128, lambda i: (jnp.maximum(i - 1, 0), 4)), blk(128, lambda i: (i, 4)),
                  blk(128, lambda i: (jnp.maximum(i - 1, 0), 5)), blk(128, lambda i: (i, 5)),
                  pl.BlockSpec((1, N_Q_HEADS), lambda i: (0, 0))],
        out_specs=[blk(512, lambda i: (i, 0)), blk(N_Q_HEADS, lambda i: (i, 0))],
        out_shape=[jax.ShapeDtypeStruct((t, D_ATTN), F32), jax.ShapeDtypeStruct((t, N_Q_HEADS), F32)],
        compiler_params=_params(("parallel",)),
    )(qkv, qkv, qkv, qkv, qkv, sinks)


def _attn_bwd(qkv, o, do, lse, sinks):
    t = qkv.shape[0]
    nb = t // ATTN_BLOCK

    def kern(q_ref, kp_ref, kc_ref, vp_ref, vc_ref, o_ref, do_ref, l_ref, s_ref,
             dq_ref, dk_ref, dv_ref, ds_ref, ck, cv):
        i = pl.program_id(0)

        @pl.when(i == 0)
        def _():
            ds_ref[...] = jnp.zeros_like(ds_ref)
            ck[...] = jnp.zeros_like(ck)
            cv[...] = jnp.zeros_like(cv)

        @pl.when(i < nb)
        def _():
            valid = _attn_mask(i)
            kband = jnp.concatenate([kp_ref[...], kc_ref[...]], axis=0)
            vband = jnp.concatenate([vp_ref[...], vc_ref[...]], axis=0)
            heads = range(N_Q_HEADS)
            sl = [slice(h * HEAD_DIM, (h + 1) * HEAD_DIM) for h in heads]
            ks = [kband[:, kh * HEAD_DIM:(kh + 1) * HEAD_DIM] for kh in range(N_KV_HEADS)]
            vs = [vband[:, kh * HEAD_DIM:(kh + 1) * HEAD_DIM] for kh in range(N_KV_HEADS)]
            qs = [q_ref[:, sl[h]] for h in heads]
            d_os = [do_ref[:, sl[h]] for h in heads]
            dobs = [d.astype(BF16) for d in d_os]
            scores = [_dot(qs[h], ks[h // Q_PER_KV], NT) for h in heads]
            dps = [_dot(dobs[h], vs[h // Q_PER_KV], NT) for h in heads]
            pbs, dscs = [], []
            for h in heads:
                lse_h = l_ref[:, h:h + 1]
                p = jnp.where(valid, jnp.exp(scores[h] - lse_h), 0.0)
                delta = jnp.sum(d_os[h] * o_ref[:, sl[h]], axis=-1, keepdims=True)
                pbs.append(p.astype(BF16))
                dscs.append((p * (dps[h] - delta)).astype(BF16))
                psink = jnp.exp(s_ref[0:1, h:h + 1] - lse_h)
                ds_ref[0:1, h:h + 1] += -jnp.sum(psink * delta, axis=0, keepdims=True)
            dqs = [_dot(dscs[h], ks[h // Q_PER_KV], NN) for h in heads]
            dkb = [sum(_dot(dscs[h], qs[h], TN) for h in heads if h // Q_PER_KV == kh) for kh in range(N_KV_HEADS)]
            dvb = [sum(_dot(pbs[h], dobs[h], TN) for h in heads if h // Q_PER_KV == kh) for kh in range(N_KV_HEADS)]
            for h in heads:
                dq_ref[:, sl[h]] = dqs[h]
            dk_band = jnp.concatenate(dkb, axis=1)
            dv_band = jnp.concatenate(dvb, axis=1)
            dk_ref[...] = ck[...] + dk_band[:ATTN_BLOCK]
            dv_ref[...] = cv[...] + dv_band[:ATTN_BLOCK]
            ck[...] = dk_band[ATTN_BLOCK:]
            cv[...] = dv_band[ATTN_BLOCK:]

        @pl.when(i == nb)
        def _():
            dk_ref[...] = ck[...]
            dv_ref[...] = cv[...]

    blk = lambda w, f: pl.BlockSpec((ATTN_BLOCK, w), f)
    cur = lambda i: jnp.minimum(i, nb - 1)
    prev = lambda i: jnp.clip(i - 1, 0, nb - 1)
    return pl.pallas_call(
        kern, name="attn_bwd", grid=(nb + 1,),
        in_specs=[blk(512, lambda i: (cur(i), 0)),
                  blk(128, lambda i: (prev(i), 4)), blk(128, lambda i: (cur(i), 4)),
                  blk(128, lambda i: (prev(i), 5)), blk(128, lambda i: (cur(i), 5)),
                  blk(512, lambda i: (cur(i), 0)), blk(512, lambda i: (cur(i), 0)),
                  blk(N_Q_HEADS, lambda i: (cur(i), 0)),
                  pl.BlockSpec((1, N_Q_HEADS), lambda i: (0, 0))],
        out_specs=[blk(512, lambda i: (cur(i), 0)), blk(128, lambda i: (prev(i), 0)),
                   blk(128, lambda i: (prev(i), 0)), pl.BlockSpec((1, N_Q_HEADS), lambda i: (0, 0))],
        out_shape=[jax.ShapeDtypeStruct((t, D_ATTN), F32), jax.ShapeDtypeStruct((t, D_KV), F32),
                   jax.ShapeDtypeStruct((t, D_KV), F32), jax.ShapeDtypeStruct((1, N_Q_HEADS), F32)],
        scratch_shapes=[pltpu.VMEM((ATTN_BLOCK, D_KV), F32), pltpu.VMEM((ATTN_BLOCK, D_KV), F32)],
        compiler_params=_params(("arbitrary",)),
    )(qkv, qkv, qkv, qkv, qkv, o, do, lse, sinks)


_GROUPS = ((0, D_ATTN), (D_ATTN, D_ATTN + D_S5), (D_ATTN + D_S5, D_MODEL))


def _rms_fwd(ya, ys, yl, g):
    t = ya.shape[0]
    tm = _row_tile(t, TM)

    def kern(a_ref, s_ref, l_ref, g_ref, o_ref):
        for (lo, hi), ref in zip(_GROUPS, (a_ref, s_ref, l_ref)):
            y = ref[...]
            n = y * lax.rsqrt(jnp.mean(y * y, axis=-1, keepdims=True) + RMS_EPS)
            o_ref[:, lo:hi] = (n * g_ref[:, lo:hi]).astype(BF16)

    row = lambda w: pl.BlockSpec((tm, w), lambda i: (i, 0))
    return pl.pallas_call(
        kern, name="rms_fwd", grid=(t // tm,),
        in_specs=[row(D_ATTN), row(D_S5), row(D_LRU), pl.BlockSpec((1, D_MODEL), lambda i: (0, 0))],
        out_specs=row(D_MODEL), out_shape=jax.ShapeDtypeStruct((t, D_MODEL), BF16),
        compiler_params=_params(("parallel",)),
    )(ya, ys, yl, g)


def _rms_bwd(dmix, ya, ys, yl, g):
    t = ya.shape[0]
    tm = _row_tile(t, TM)

    def kern(d_ref, a_ref, s_ref, l_ref, g_ref, da_ref, ds_ref, dl_ref, dg_ref):
        @pl.when(pl.program_id(0) == 0)
        def _():
            dg_ref[...] = jnp.zeros_like(dg_ref)

        for (lo, hi), ref, out in zip(_GROUPS, (a_ref, s_ref, l_ref), (da_ref, ds_ref, dl_ref)):
            y = ref[...]
            rstd = lax.rsqrt(jnp.mean(y * y, axis=-1, keepdims=True) + RMS_EPS)
            n = y * rstd
            dm = d_ref[:, lo:hi]
            dg_ref[:, lo:hi] += jnp.sum(dm * n, axis=0, keepdims=True)
            dn = dm * g_ref[:, lo:hi]
            out[...] = rstd * (dn - n * jnp.mean(dn * n, axis=-1, keepdims=True))

    row = lambda w: pl.BlockSpec((tm, w), lambda i: (i, 0))
    vec = pl.BlockSpec((1, D_MODEL), lambda i: (0, 0))
    return pl.pallas_call(
        kern, name="rms_bwd", grid=(t // tm,),
        in_specs=[row(D_MODEL), row(D_ATTN), row(D_S5), row(D_LRU), vec],
        out_specs=[row(D_ATTN), row(D_S5), row(D_LRU), vec],
        out_shape=[jax.ShapeDtypeStruct((t, D_ATTN), F32), jax.ShapeDtypeStruct((t, D_S5), F32),
                   jax.ShapeDtypeStruct((t, D_LRU), F32), jax.ShapeDtypeStruct((1, D_MODEL), F32)],
        compiler_params=_params(("arbitrary",)),
    )(dmix, ya, ys, yl, g)


def _matmul_ln(name, a, w, bias, xres, g, b, a_blk, a_map, w_blk, parts, layer):
    t = xres.shape[0]
    tm = a_blk[-2]

    def kern(a_ref, w_ref, bias_ref, x_ref, g_ref, b_ref, y_ref, yb_ref, h_ref, r_ref):
        if parts is None:
            f = _dot(a_ref[...], w_ref[...], NN)
        else:
            f = sum(_dot(a_ref[j], w_ref[j], NN) for j in range(parts))
        r = ALPHA * x_ref[...] + f + bias_ref[...]
        mu = jnp.mean(r, axis=-1, keepdims=True)
        xc = r - mu
        rstd = lax.rsqrt(jnp.mean(xc * xc, axis=-1, keepdims=True) + LN_EPS)
        xhat = xc * rstd
        h_ref[...] = xhat
        r_ref[...] = rstd
        y = xhat * g_ref[...] + b_ref[...]
        y_ref[...] = y
        yb_ref[...] = y.astype(BF16)

    row = pl.BlockSpec((tm, D_MODEL), lambda i: (i, 0))
    vec = pl.BlockSpec((1, D_MODEL), lambda i: (0, 0))
    big = lambda dt: jax.ShapeDtypeStruct((t, D_MODEL), dt)
    return pl.pallas_call(
        kern, name=name, grid=(t // tm,),
        in_specs=[pl.BlockSpec(a_blk, a_map), pl.BlockSpec((None,) + w_blk, lambda i: (layer,) + (0,) * len(w_blk)), vec, row, vec, vec],
        out_specs=[row, row, row, pl.BlockSpec((tm, 1), lambda i: (i, 0))],
        out_shape=[big(F32), big(BF16), big(F32), jax.ShapeDtypeStruct((t, 1), F32)],
        compiler_params=_params(("parallel",)),
    )(a, w, bias, xres, g, b)


def _ln_bwd(dy, xhat, rstd, g):
    t = dy.shape[0]
    tm = _row_tile(t, TM)

    def kern(d_ref, h_ref, r_ref, g_ref, dr_ref, drb_ref, dg_ref, db_ref, sr_ref):
        @pl.when(pl.program_id(0) == 0)
        def _():
            dg_ref[...] = jnp.zeros_like(dg_ref)
            db_ref[...] = jnp.zeros_like(db_ref)
            sr_ref[...] = jnp.zeros_like(sr_ref)

        d = d_ref[...]
        xhat = h_ref[...]
        dg_ref[...] += jnp.sum(d * xhat, axis=0, keepdims=True)
        db_ref[...] += jnp.sum(d, axis=0, keepdims=True)
        dh = d * g_ref[...]
        dr = r_ref[...] * (dh - jnp.mean(dh, axis=-1, keepdims=True)
                           - xhat * jnp.mean(dh * xhat, axis=-1, keepdims=True))
        dr_ref[...] = dr
        drb_ref[...] = dr.astype(BF16)
        sr_ref[...] += jnp.sum(dr, axis=0, keepdims=True)

    row = pl.BlockSpec((tm, D_MODEL), lambda i: (i, 0))
    vec = pl.BlockSpec((1, D_MODEL), lambda i: (0, 0))
    vshape = jax.ShapeDtypeStruct((1, D_MODEL), F32)
    return pl.pallas_call(
        kern, name="ln_bwd", grid=(t // tm,),
        in_specs=[row, row, pl.BlockSpec((tm, 1), lambda i: (i, 0)), vec],
        out_specs=[row, row, vec, vec, vec],
        out_shape=[jax.ShapeDtypeStruct((t, D_MODEL), F32), jax.ShapeDtypeStruct((t, D_MODEL), BF16),
                   vshape, vshape, vshape],
        compiler_params=_params(("arbitrary",)),
    )(dy, xhat, rstd, g)


def _loss_head(y, target):
    t = y.shape[0]
    tm = _row_tile(t, TM)

    def kern(y_ref, t_ref, l_ref, d_ref):
        @pl.when(pl.program_id(0) == 0)
        def _():
            l_ref[...] = jnp.zeros_like(l_ref)

        err = y_ref[...] - t_ref[...]
        d_ref[...] = err * (1.0 / D_MODEL)
        part = jnp.sum(jnp.sum(err * err, axis=-1, keepdims=True), axis=0, keepdims=True)
        l_ref[...] += jnp.broadcast_to(part * (0.5 / D_MODEL), l_ref.shape)

    row = pl.BlockSpec((tm, D_MODEL), lambda i: (i, 0))
    return pl.pallas_call(
        kern, name="loss_head", grid=(t // tm,),
        in_specs=[row, row], out_specs=[pl.BlockSpec((1, 128), lambda i: (0, 0)), row],
        out_shape=[jax.ShapeDtypeStruct((1, 128), F32), jax.ShapeDtypeStruct((t, D_MODEL), F32)],
        compiler_params=_params(("arbitrary",)),
    )(y, target)


HALO = 8


def _ffn_mid_specs(t, tm):
    main = pl.BlockSpec((None, tm, FF_SH), lambda j, i: (j, i, 0))
    prev = pl.BlockSpec((None, HALO, FF_SH), lambda j, i: (j, jnp.maximum(i * (tm // HALO) - 1, 0), 0))
    cw = pl.BlockSpec((None, FFN_CONV, FF_SH), lambda j, i: (j, 0, 0))
    cb = pl.BlockSpec((None, 1, FF_SH), lambda j, i: (j, 0, 0))
    return main, prev, cw, cb


def _ffn_conv(ext, g_ref, p_ref, w_ref, b_ref, tm):
    i = pl.program_id(1)
    ext[0:HALO, :] = jnp.where(i > 0, p_ref[...], 0.0)
    ext[HALO:, :] = g_ref[...]
    taps = [ext[pl.ds(HALO - (FFN_CONV - 1) + k, tm), :] for k in range(FFN_CONV)]
    gc = b_ref[...] + sum(w_ref[k:k + 1, :] * taps[k] for k in range(FFN_CONV))
    return gc, taps


def _ffn_mid_fwd(gpre, up, cw, cb):
    t = gpre.shape[1]
    tm = _row_tile(t, TM)

    def kern(g_ref, p_ref, u_ref, w_ref, b_ref, o_ref, ext):
        gc, _ = _ffn_conv(ext, g_ref, p_ref, w_ref, b_ref, tm)
        o_ref[...] = (gc * _sigmoid(gc) * u_ref[...]).astype(BF16)

    main, prev, cws, cbs = _ffn_mid_specs(t, tm)
    return pl.pallas_call(
        kern, name="ffn_mid_fwd", grid=(N_CHIPS, t // tm),
        in_specs=[main, prev, main, cws, cbs], out_specs=main,
        out_shape=jax.ShapeDtypeStruct((N_CHIPS, t, FF_SH), BF16),
        scratch_shapes=[pltpu.VMEM((tm + HALO, FF_SH), F32)],
        compiler_params=_params(("parallel", "parallel")),
    )(gpre, gpre, up, cw, cb)


def _ffn_mid_bwd(gpre, up, dhmid, cw, cb):
    t = gpre.shape[1]
    tm = _row_tile(t, TM)

    def kern(g_ref, p_ref, u_ref, d_ref, w_ref, b_ref, h_ref, du_ref, dg_ref, dw_ref, db_ref, ext):
        @pl.when(pl.program_id(1) == 0)
        def _():
            dw_ref[...] = jnp.zeros_like(dw_ref)
            db_ref[...] = jnp.zeros_like(db_ref)

        gc, taps = _ffn_conv(ext, g_ref, p_ref, w_ref, b_ref, tm)
        sg = _sigmoid(gc)
        s = gc * sg
        u = u_ref[...]
        d = d_ref[...]
        h_ref[...] = (s * u).astype(BF16)
        du_ref[...] = (d * s).astype(BF16)
        dgc = d * u * (sg * (1.0 + gc * (1.0 - sg)))
        dg_ref[...] = dgc
        db_ref[...] += jnp.sum(dgc, axis=0, keepdims=True)
        for k in range(FFN_CONV):
            dw_ref[k:k + 1, :] += jnp.sum(dgc * taps[k], axis=0, keepdims=True)

    main, prev, cws, cbs = _ffn_mid_specs(t, tm)
    big = lambda dt: jax.ShapeDtypeStruct((N_CHIPS, t, FF_SH), dt)
    return pl.pallas_call(
        kern, name="ffn_mid_bwd", grid=(N_CHIPS, t // tm),
        in_specs=[main, prev, main, main, cws, cbs], out_specs=[main, main, main, cws, cbs],
        out_shape=[big(BF16), big(BF16), big(F32), jax.ShapeDtypeStruct((N_CHIPS, FFN_CONV, FF_SH), F32),
                   jax.ShapeDtypeStruct((N_CHIPS, 1, FF_SH), F32)],
        scratch_shapes=[pltpu.VMEM((tm + HALO, FF_SH), F32)],
        compiler_params=_params(("parallel", "arbitrary")),
    )(gpre, gpre, up, dhmid, cw, cb)


def _ffn_conv_t(dgc, cw):
    t = dgc.shape[1]
    tm = _row_tile(t, TM)
    nt = t // tm

    def kern(d_ref, n_ref, w_ref, o_ref, ext):
        i = pl.program_id(1)
        ext[0:tm, :] = d_ref[...]
        ext[tm:, :] = jnp.where(i < nt - 1, n_ref[...], 0.0)
        acc = sum(w_ref[k:k + 1, :] * ext[pl.ds(FFN_CONV - 1 - k, tm), :] for k in range(FFN_CONV))
        o_ref[...] = acc.astype(BF16)

    main, _, cws, _ = _ffn_mid_specs(t, tm)
    nxt = pl.BlockSpec((None, HALO, FF_SH),
                       lambda j, i: (j, jnp.minimum((i + 1) * (tm // HALO), t // HALO - 1), 0))
    return pl.pallas_call(
        kern, name="ffn_conv_t", grid=(N_CHIPS, nt),
        in_specs=[main, nxt, cws], out_specs=main,
        out_shape=jax.ShapeDtypeStruct((N_CHIPS, t, FF_SH), BF16),
        scratch_shapes=[pltpu.VMEM((tm + HALO, FF_SH), F32)],
        compiler_params=_params(("parallel", "parallel")),
    )(dgc, dgc, cw)


def _ffn_hidden_fwd(xb, wg, wu, cw, cb, layer):
    t = xb.shape[0]
    tm = _row_tile(t, TM)

    def kern(x_ref, wg_ref, wu_ref, cw_ref, cb_ref, g_ref, c_ref, u_ref, h_ref, ext):
        @pl.when(pl.program_id(1) == 0)
        def _():
            ext[0:HALO, :] = jnp.zeros((HALO, FF_SH), F32)

        x = x_ref[...]
        gb = _dot(x, wg_ref[...], NN).astype(BF16)
        ub = _dot(x, wu_ref[...], NN).astype(BF16)
        g_ref[...] = gb
        u_ref[...] = ub
        g = gb.astype(F32)
        ext[HALO:, :] = g
        gcb = (cb_ref[...] + sum(cw_ref[k:k + 1, :] * ext[pl.ds(HALO - (FFN_CONV - 1) + k, tm), :]
                                 for k in range(FFN_CONV))).astype(BF16)
        c_ref[...] = gcb
        gc = gcb.astype(F32)
        h_ref[...] = (gc * _sigmoid(gc) * ub.astype(F32)).astype(BF16)
        ext[0:HALO, :] = g[tm - HALO:, :]

    col = pl.BlockSpec((None, tm, FF_SH), lambda j, i: (j, i, 0))
    wspec = pl.BlockSpec((None, None, D_MODEL, FF_SH), lambda j, i: (layer, j, 0, 0))
    big = jax.ShapeDtypeStruct((N_CHIPS, t, FF_SH), BF16)
    return pl.pallas_call(
        kern, name="ffn_hidden_fwd", grid=(N_CHIPS, t // tm),
        in_specs=[pl.BlockSpec((tm, D_MODEL), lambda j, i: (i, 0)), wspec, wspec,
                  pl.BlockSpec((None, FFN_CONV, FF_SH), lambda j, i: (j, 0, 0)),
                  pl.BlockSpec((None, 1, FF_SH), lambda j, i: (j, 0, 0))],
        out_specs=[col, col, col, col], out_shape=[big, big, big, big],
        scratch_shapes=[pltpu.VMEM((tm + HALO, FF_SH), F32)],
        compiler_params=_params(("parallel", "arbitrary")),
    )(xb, wg, wu, cw, cb)


def _ffn_hidden_bwd(drb, gpre, gconv, up, wd, cw, layer):
    t = drb.shape[0]
    tm = _row_tile(t, TM)
    nt = t // tm
    rb = lambda i: nt - 1 - i

    def kern(d_ref, g_ref, c_ref, u_ref, wd_ref, cw_ref, du_ref, dg_ref, dw_ref, db_ref, ext):
        @pl.when(pl.program_id(1) == 0)
        def _():
            dw_ref[...] = jnp.zeros_like(dw_ref)
            db_ref[...] = jnp.zeros_like(db_ref)
            ext[tm:, :] = jnp.zeros((HALO, FF_SH), F32)

        dh = _dot(d_ref[...], wd_ref[...], NT)
        gc = c_ref[...].astype(F32)
        sg = _sigmoid(gc)
        du_ref[...] = (dh * (gc * sg)).astype(BF16)
        dgc = dh * u_ref[...].astype(F32) * (sg * (1.0 + gc * (1.0 - sg)))
        db_ref[...] += jnp.sum(dgc, axis=0, keepdims=True)
        ext[0:tm, :] = dgc
        g = g_ref[...].astype(F32)
        acc = None
        for k in range(FFN_CONV):
            tap = ext[pl.ds(FFN_CONV - 1 - k, tm), :]
            dw_ref[k:k + 1, :] += jnp.sum(g * tap, axis=0, keepdims=True)
            term = cw_ref[k:k + 1, :] * tap
            acc = term if acc is None else acc + term
        dg_ref[...] = acc.astype(BF16)
        ext[tm:, :] = dgc[0:HALO, :]

    col = pl.BlockSpec((None, tm, FF_SH), lambda j, i: (j, rb(i), 0))
    cws = pl.BlockSpec((None, FFN_CONV, FF_SH), lambda j, i: (j, 0, 0))
    cbs = pl.BlockSpec((None, 1, FF_SH), lambda j, i: (j, 0, 0))
    big = jax.ShapeDtypeStruct((N_CHIPS, t, FF_SH), BF16)
    return pl.pallas_call(
        kern, name="ffn_hidden_bwd", grid=(N_CHIPS, nt),
        in_specs=[pl.BlockSpec((tm, D_MODEL), lambda j, i: (rb(i), 0)), col, col, col,
                  pl.BlockSpec((None, None, FF_SH, D_MODEL), lambda j, i: (layer, j, 0, 0)), cws],
        out_specs=[col, col, cws, cbs],
        out_shape=[big, big, jax.ShapeDtypeStruct((N_CHIPS, FFN_CONV, FF_SH), F32),
                   jax.ShapeDtypeStruct((N_CHIPS, 1, FF_SH), F32)],
        scratch_shapes=[pltpu.VMEM((tm + HALO, FF_SH), F32)],
        compiler_params=_params(("parallel", "arbitrary")),
    )(drb, gpre, gconv, up, wd, cw)


def _s5_coefs(ar, ai, reverse):
    if reverse:
        ai = -ai
    pw = [(ar, ai)]
    for _ in range(SUBLANES - 1):
        pr, pi = pw[-1]
        pw.append((pr * ar - pi * ai, pr * ai + pi * ar))
    rows = jnp.arange(SUBLANES)[:, None]
    out = []
    for s in (1, 2, 4):
        keep = (rows + s <= SUBLANES - 1) if reverse else (rows >= s)
        out += [jnp.where(keep, pw[s - 1][0][None], 0.0), jnp.where(keep, pw[s - 1][1][None], 0.0)]
    order = list(range(SUBLANES - 1, -1, -1)) if reverse else list(range(SUBLANES))
    out += [jnp.stack([pw[k][0] for k in order]), jnp.stack([pw[k][1] for k in order])]
    return jnp.stack(out).astype(F32)


def _s5_scan(buf, coef_ref, carry, tm, reverse):
    n8 = tm // SUBLANES

    def body(it, c):
        cre, cim = c
        blk = (n8 - 1 - it) if reverse else it
        r0 = pl.multiple_of(blk * SUBLANES, SUBLANES)
        xre = buf[pl.ds(r0, SUBLANES), 0:N_STATE]
        xim = buf[pl.ds(r0, SUBLANES), N_STATE:]
        for idx, s in enumerate((1, 2, 4)):
            sh = (SUBLANES - s) if reverse else s
            sre = pltpu.roll(xre, sh, 0)
            sim = pltpu.roll(xim, sh, 0)
            are = coef_ref[2 * idx]
            aim = coef_ref[2 * idx + 1]
            xre, xim = xre + are * sre - aim * sim, xim + are * sim + aim * sre
        pre = coef_ref[6]
        pim = coef_ref[7]
        hre = xre + pre * cre - pim * cim
        him = xim + pre * cim + pim * cre
        buf[pl.ds(r0, SUBLANES), 0:N_STATE] = hre
        buf[pl.ds(r0, SUBLANES), N_STATE:] = him
        row = 0 if reverse else SUBLANES - 1
        return (jnp.broadcast_to(hre[row:row + 1], (SUBLANES, N_STATE)),
                jnp.broadcast_to(him[row:row + 1], (SUBLANES, N_STATE)))

    cre, cim = lax.fori_loop(0, n8, body, (carry[:, 0:N_STATE], carry[:, N_STATE:]))
    carry[:, 0:N_STATE] = cre
    carry[:, N_STATE:] = cim


def _real_scan(abuf, bbuf, carry, tm, reverse):
    n8 = tm // SUBLANES
    width = bbuf.shape[1]

    def body(it, c):
        blk = (n8 - 1 - it) if reverse else it
        r0 = pl.multiple_of(blk * SUBLANES, SUBLANES)
        a = abuf[pl.ds(r0, SUBLANES), :]
        b = bbuf[pl.ds(r0, SUBLANES), :]
        rows = lax.broadcasted_iota(jnp.int32, (SUBLANES, width), 0)
        for s in (1, 2, 4):
            sh = (SUBLANES - s) if reverse else s
            keep = (rows + s <= SUBLANES - 1) if reverse else (rows >= s)
            sa = pltpu.roll(a, sh, 0)
            sb = pltpu.roll(b, sh, 0)
            b = b + a * jnp.where(keep, sb, 0.0)
            a = a * jnp.where(keep, sa, 1.0)
        h = b + a * c
        bbuf[pl.ds(r0, SUBLANES), :] = h
        row = 0 if reverse else SUBLANES - 1
        return jnp.broadcast_to(h[row:row + 1], (SUBLANES, width))

    carry[...] = lax.fori_loop(0, n8, body, carry[...])


def _dot(a, b, dims):
    return lax.dot_general(a, b, (dims, ((), ())), preferred_element_type=F32)


TS5 = 256
HALO16 = 16


def _s5_fwd(proj, bmat, coef, cmat, dvec, gw, gb):
    t = proj.shape[0]
    tm = _row_tile(t, TS5)

    def kern(u_ref, b_ref, coef_ref, c_ref, d_ref, gw_ref, gb_ref, h_ref, y_ref, hbuf, carry):
        @pl.when(pl.program_id(0) == 0)
        def _():
            carry[...] = jnp.zeros_like(carry)

        u = u_ref[...]
        hbuf[...] = _dot(u.astype(BF16), b_ref[...], NN)
        _s5_scan(hbuf, coef_ref, carry, tm, False)
        hb = hbuf[...].astype(BF16)
        h_ref[...] = hb
        y = _dot(hb, c_ref[...], NN) + d_ref[...] * u
        ys = _gelu(y)
        z = _dot(ys.astype(BF16), gw_ref[...], NN) + gb_ref[...]
        y_ref[...] = ys * _sigmoid(z)

    full = lambda shp: pl.BlockSpec(shp, lambda i: (0,) * len(shp))
    return pl.pallas_call(
        kern, name="s5_fwd", grid=(t // tm,),
        in_specs=[pl.BlockSpec((tm, D_S5), lambda i: (i, 0)), full((D_S5, 2 * N_STATE)),
                  full((8, SUBLANES, N_STATE)), full((2 * N_STATE, D_S5)), full((1, D_S5)),
                  full((D_S5, D_S5)), full((1, D_S5))],
        out_specs=[pl.BlockSpec((tm, 2 * N_STATE), lambda i: (i, 0)), pl.BlockSpec((tm, D_S5), lambda i: (i, 0))],
        out_shape=[jax.ShapeDtypeStruct((t, 2 * N_STATE), BF16), jax.ShapeDtypeStruct((t, D_S5), F32)],
        scratch_shapes=[pltpu.VMEM((tm, 2 * N_STATE), F32), pltpu.VMEM((SUBLANES, 2 * N_STATE), F32)],
        compiler_params=_params(("arbitrary",)),
    )(proj, bmat, coef, cmat, dvec, gw, gb)


def _s5_bwd(proj, h, dout, bmat, coef_b, cmat, dvec, gw, gb):
    t = proj.shape[0]
    tm = _row_tile(t, TS5)
    nt = t // tm
    rb = lambda i: nt - 1 - i

    def kern(u_ref, h_ref, hp_ref, d_ref, b_ref, coef_ref, c_ref, dv_ref, gw_ref, gb_ref,
             du_ref, dc_ref, db_ref, da_ref, dd_ref, dgw_ref, dgb_ref, gbuf, hext, carry):
        i = pl.program_id(0)

        @pl.when(i == 0)
        def _():
            carry[...] = jnp.zeros_like(carry)
            for r in (dc_ref, db_ref, da_ref, dd_ref, dgw_ref, dgb_ref):
                r[...] = jnp.zeros_like(r)

        u = u_ref[...]
        hb = h_ref[...]
        y = _dot(hb, c_ref[...], NN) + dv_ref[...] * u
        ys = _gelu(y)
        ysb = ys.astype(BF16)
        sg = _sigmoid(_dot(ysb, gw_ref[...], NN) + gb_ref[...])
        d_o = d_ref[...]
        dz = d_o * ys * sg * (1.0 - sg)
        dzb = dz.astype(BF16)
        dys = d_o * sg + _dot(dzb, gw_ref[...], NT)
        dgw_ref[...] += _dot(ysb, dzb, TN)
        dgb_ref[...] += jnp.sum(dz, axis=0, keepdims=True)
        dy = dys * _gelu_grad(y)
        dd_ref[...] += jnp.sum(dy * u, axis=0, keepdims=True)
        dyb = dy.astype(BF16)
        dc_ref[...] += _dot(hb, dyb, TN)
        gbuf[...] = _dot(dyb, c_ref[...], NT)
        _s5_scan(gbuf, coef_ref, carry, tm, True)
        g = gbuf[...]
        first = jnp.where(i < nt - 1, hp_ref[HALO16 - 1:HALO16, :].astype(F32), 0.0)
        hext[SUBLANES - 1:SUBLANES, :] = first
        hext[SUBLANES:, :] = hb.astype(F32)
        hprev = hext[pl.ds(SUBLANES - 1, tm), :]
        gre, gim = g[:, 0:N_STATE], g[:, N_STATE:]
        pre, pim = hprev[:, 0:N_STATE], hprev[:, N_STATE:]
        da_ref[0:1, :] += jnp.sum(gre * pre + gim * pim, axis=0, keepdims=True)
        da_ref[1:2, :] += jnp.sum(gim * pre - gre * pim, axis=0, keepdims=True)
        gb16 = g.astype(BF16)
        db_ref[...] += _dot(u.astype(BF16), gb16, TN)
        du_ref[...] = dy * dv_ref[...] + _dot(gb16, b_ref[...], NT)

    full = lambda shp: pl.BlockSpec(shp, lambda i: (0,) * len(shp))
    shape = lambda shp: jax.ShapeDtypeStruct(shp, F32)
    return pl.pallas_call(
        kern, name="s5_bwd", grid=(nt,),
        in_specs=[pl.BlockSpec((tm, D_S5), lambda i: (rb(i), 0)),
                  pl.BlockSpec((tm, 2 * N_STATE), lambda i: (rb(i), 0)),
                  pl.BlockSpec((HALO16, 2 * N_STATE), lambda i: (jnp.maximum(rb(i) * (tm // HALO16) - 1, 0), 0)),
                  pl.BlockSpec((tm, D_S5), lambda i: (rb(i), 0)),
                  full((D_S5, 2 * N_STATE)), full((8, SUBLANES, N_STATE)), full((2 * N_STATE, D_S5)),
                  full((1, D_S5)), full((D_S5, D_S5)), full((1, D_S5))],
        out_specs=[pl.BlockSpec((tm, D_S5), lambda i: (rb(i), 0)), full((2 * N_STATE, D_S5)),
                   full((D_S5, 2 * N_STATE)), full((2, N_STATE)), full((1, D_S5)), full((D_S5, D_S5)),
                   full((1, D_S5))],
        out_shape=[shape((t, D_S5)), shape((2 * N_STATE, D_S5)), shape((D_S5, 2 * N_STATE)),
                   shape((2, N_STATE)), shape((1, D_S5)), shape((D_S5, D_S5)), shape((1, D_S5))],
        scratch_shapes=[pltpu.VMEM((tm, 2 * N_STATE), F32), pltpu.VMEM((tm + SUBLANES, 2 * N_STATE), F32),
                        pltpu.VMEM((SUBLANES, 2 * N_STATE), F32)],
        compiler_params=_params(("arbitrary",)),
    )(proj, h, h, dout, bmat, coef_b, cmat, dvec, gw, gb)


def _lru_gates(ext, x_ref, p_ref, cw_ref, cb_ref, wx_ref, bx_ref, wa_ref, ba_ref, ap_ref, first_tile, row0, tm):
    ext[0:HALO, :] = jnp.where(first_tile, 0.0, p_ref[...])
    ext[HALO:, :] = x_ref[...]
    taps = [ext[pl.ds(HALO - (LRU_CONV - 1) + k, tm), :] for k in range(LRU_CONV)]
    xc = cb_ref[...] + sum(cw_ref[k:k + 1, :] * taps[k] for k in range(LRU_CONV))
    xcb = xc.astype(BF16)
    gx = _sigmoid(_dot(xcb, wx_ref[...], NN) + bx_ref[...])
    ga = _sigmoid(_dot(xcb, wa_ref[...], NN) + ba_ref[...])
    z = -ap_ref[...]
    sp = jnp.maximum(z, 0.0) + jnp.log(1.0 + jnp.exp(-jnp.abs(z)))
    log_a = -LRU_C * ga * sp
    a = jnp.exp(log_a)
    tok = row0 + lax.broadcasted_iota(jnp.int32, a.shape, 0)
    is0 = tok == 0
    mult = jnp.where(is0, 1.0, jnp.sqrt(1.0 - jnp.exp(2.0 * log_a)))
    return taps, xc, xcb, gx, ga, sp, a, mult, is0


def _lru_specs(tm, blk_of):
    col = lambda cidx: pl.BlockSpec((tm, D_LRU), lambda i: (blk_of(i), cidx))
    prev = lambda cidx: pl.BlockSpec((HALO, D_LRU), lambda i: (jnp.maximum(blk_of(i) * (tm // HALO) - 1, 0), cidx))
    full = lambda shp: pl.BlockSpec(shp, lambda i: (0,) * len(shp))
    wts = [full((LRU_CONV, D_LRU)), full((1, D_LRU)), full((D_LRU, D_LRU)), full((1, D_LRU)),
           full((D_LRU, D_LRU)), full((1, D_LRU)), full((1, D_LRU))]
    return col, prev, full, wts


def _lru_fwd(proj, cw, cb, wx, bx, wa, ba, ap):
    t = proj.shape[0]
    tm = _row_tile(t, TM)

    def kern(x_ref, p_ref, g_ref, cw_ref, cb_ref, wx_ref, bx_ref, wa_ref, ba_ref, ap_ref,
             y_ref, h_ref, ext, abuf, carry):
        i = pl.program_id(0)

        @pl.when(i == 0)
        def _():
            carry[...] = jnp.zeros_like(carry)

        _, xc, _, gx, _, _, a, mult, _ = _lru_gates(ext, x_ref, p_ref, cw_ref, cb_ref, wx_ref, bx_ref, wa_ref,
                                                    ba_ref, ap_ref, i == 0, i * tm, tm)
        abuf[...] = a
        h_ref[...] = mult * gx * xc
        _real_scan(abuf, h_ref, carry, tm, False)
        y_ref[...] = h_ref[...] * _gelu(g_ref[...])

    col, prev, full, wts = _lru_specs(tm, lambda i: i)
    out = pl.BlockSpec((tm, D_LRU), lambda i: (i, 0))
    return pl.pallas_call(
        kern, name="lru_fwd", grid=(t // tm,),
        in_specs=[col(1), prev(1), col(2)] + wts, out_specs=[out, out],
        out_shape=[jax.ShapeDtypeStruct((t, D_LRU), F32), jax.ShapeDtypeStruct((t, D_LRU), F32)],
        scratch_shapes=[pltpu.VMEM((tm + HALO, D_LRU), F32), pltpu.VMEM((tm, D_LRU), F32),
                        pltpu.VMEM((SUBLANES, D_LRU), F32)],
        compiler_params=_params(("arbitrary",)),
    )(proj, proj, proj, cw, cb, wx, bx, wa, ba, ap)


def _lru_bwd(proj, h, dout, cw, cb, wx, bx, wa, ba, ap):
    t = proj.shape[0]
    tm = _row_tile(t, TM)
    nt = t // tm
    rb = lambda i: nt - 1 - i

    def kern(x_ref, p_ref, g_ref, h_ref, hp_ref, d_ref, cw_ref, cb_ref, wx_ref, bx_ref, wa_ref, ba_ref, ap_ref,
             dxc_ref, dg_ref, dcw_ref, dcb_ref, dwx_ref, dbx_ref, dwa_ref, dba_ref, dap_ref,
             ext, aext, abuf, gbuf, carry, acarry):
        i = pl.program_id(0)
        blk = nt - 1 - i

        @pl.when(i == 0)
        def _():
            carry[...] = jnp.zeros_like(carry)
            acarry[...] = jnp.zeros_like(acarry)
            for r in (dcw_ref, dcb_ref, dwx_ref, dbx_ref, dwa_ref, dba_ref, dap_ref):
                r[...] = jnp.zeros_like(r)

        taps, xc, xcb, gx, ga, sp, a, mult, is0 = _lru_gates(
            ext, x_ref, p_ref, cw_ref, cb_ref, wx_ref, bx_ref, wa_ref, ba_ref, ap_ref, blk == 0, blk * tm, tm)
        gate = g_ref[...]
        d_o = d_ref[...]
        hcur = h_ref[...]
        dg_ref[...] = d_o * hcur * _gelu_grad(gate)
        aext[0:tm, :] = a
        aext[tm:, :] = acarry[...]
        abuf[...] = aext[pl.ds(1, tm), :]
        gbuf[...] = d_o * _gelu(gate)
        _real_scan(abuf, gbuf, carry, tm, True)
        acarry[...] = jnp.broadcast_to(a[0:1], acarry.shape)
        g = gbuf[...]
        ext[0:HALO, :] = jnp.where(blk == 0, 0.0, hp_ref[...])
        ext[HALO:, :] = hcur
        hprev = ext[pl.ds(HALO - 1, tm), :]
        dmult = jnp.where(is0, 0.0, g * gx * xc)
        dgx = g * mult * xc
        dxc = g * mult * gx
        dlog_a = g * hprev * a - dmult * (a * a) / mult
        dga = dlog_a * (-LRU_C * sp)
        dsp = jnp.sum(dlog_a * (-LRU_C * ga), axis=0, keepdims=True)
        dap_ref[...] += dsp * (-_sigmoid(-ap_ref[...]))
        dpa = (dga * ga * (1.0 - ga))
        dpx = (dgx * gx * (1.0 - gx))
        dpab, dpxb = dpa.astype(BF16), dpx.astype(BF16)
        dwx_ref[...] += _dot(xcb, dpxb, TN)
        dwa_ref[...] += _dot(xcb, dpab, TN)
        dbx_ref[...] += jnp.sum(dpx, axis=0, keepdims=True)
        dba_ref[...] += jnp.sum(dpa, axis=0, keepdims=True)
        dxc = dxc + _dot(dpxb, wx_ref[...], NT) + _dot(dpab, wa_ref[...], NT)
        dxc_ref[...] = dxc
        dcb_ref[...] += jnp.sum(dxc, axis=0, keepdims=True)
        for k in range(LRU_CONV):
            dcw_ref[k:k + 1, :] += jnp.sum(dxc * taps[k], axis=0, keepdims=True)

    col, prev, full, wts = _lru_specs(tm, rb)
    row = pl.BlockSpec((tm, D_LRU), lambda i: (rb(i), 0))
    hprev_spec = pl.BlockSpec((HALO, D_LRU), lambda i: (jnp.maximum(rb(i) * (tm // HALO) - 1, 0), 0))
    shape = lambda shp: jax.ShapeDtypeStruct(shp, F32)
    vec = (1, D_LRU)
    sq = (D_LRU, D_LRU)
    return pl.pallas_call(
        kern, name="lru_bwd", grid=(nt,),
        in_specs=[col(1), prev(1), col(2), row, hprev_spec, row] + wts,
        out_specs=[row, row, full((LRU_CONV, D_LRU)), full(vec), full(sq), full(vec), full(sq), full(vec), full(vec)],
        out_shape=[shape((t, D_LRU)), shape((t, D_LRU)), shape((LRU_CONV, D_LRU)), shape(vec), shape(sq),
                   shape(vec), shape(sq), shape(vec), shape(vec)],
        scratch_shapes=[pltpu.VMEM((tm + HALO, D_LRU), F32), pltpu.VMEM((tm + HALO, D_LRU), F32),
                        pltpu.VMEM((tm, D_LRU), F32), pltpu.VMEM((tm, D_LRU), F32),
                        pltpu.VMEM((SUBLANES, D_LRU), F32), pltpu.VMEM((SUBLANES, D_LRU), F32)],
        compiler_params=_params(("arbitrary",)),
    )(proj, proj, proj, h, h, dout, cw, cb, wx, bx, wa, ba, ap)


def _assemble_dproj(dq, dk, dv, du, dxc, dgate, cos, sin_s, cw):
    t = dq.shape[0]
    tm = _row_tile(t, TM)
    nt = t // tm

    def kern(dq_ref, dk_ref, dv_ref, du_ref, dx_ref, dn_ref, dg_ref, c_ref, s_ref, cw_ref, o_ref, b_ref, ext):
        i = pl.program_id(0)

        @pl.when(i == 0)
        def _():
            b_ref[...] = jnp.zeros_like(b_ref)

        def put(lo, val):
            hi = lo + val.shape[1]
            o_ref[:, lo:hi] = val.astype(BF16)
            b_ref[:, lo:hi] += jnp.sum(val, axis=0, keepdims=True)

        c = c_ref[...]
        s = s_ref[...]
        for ch in range(4):
            x = dq_ref[:, ch * 128:(ch + 1) * 128] * (HEAD_DIM ** -0.5)
            put(ch * 128, x * c - _rope_swap(x) * s)
        x = dk_ref[...]
        put(512, x * c - _rope_swap(x) * s)
        put(640, dv_ref[...])
        put(768, du_ref[...])
        ext[0:tm, :] = dx_ref[...]
        ext[tm:, :] = jnp.where(i < nt - 1, dn_ref[...], 0.0)
        put(1024, sum(cw_ref[k:k + 1, :] * ext[pl.ds(LRU_CONV - 1 - k, tm), :] for k in range(LRU_CONV)))
        put(1280, dg_ref[...])

    row = lambda w: pl.BlockSpec((tm, w), lambda i: (i, 0))
    nxt = pl.BlockSpec((HALO, D_LRU), lambda i: (jnp.minimum((i + 1) * (tm // HALO), t // HALO - 1), 0))
    return pl.pallas_call(
        kern, name="assemble_dproj", grid=(nt,),
        in_specs=[row(512), row(128), row(128), row(256), row(256), nxt, row(256), row(128), row(128),
                  pl.BlockSpec((LRU_CONV, D_LRU), lambda i: (0, 0))],
        out_specs=[row(D_IN), pl.BlockSpec((1, D_IN), lambda i: (0, 0))],
        out_shape=[jax.ShapeDtypeStruct((t, D_IN), BF16), jax.ShapeDtypeStruct((1, D_IN), F32)],
        scratch_shapes=[pltpu.VMEM((tm + HALO, D_LRU), F32)],
        compiler_params=_params(("arbitrary",)),
    )(dq, dk, dv, du, dxc, dxc, dgate, cos, sin_s, cw)


def _blockdiag_s5(bbar_re, bbar_im, c_re, c_im):
    eye = jnp.eye(S5_GROUPS, dtype=F32)
    b_of = lambda m: jnp.einsum('gpc,gh->gchp', m, eye).reshape(D_S5, N_STATE)
    c_of = lambda m: jnp.einsum('gcp,gh->gphc', m, eye).reshape(N_STATE, D_S5)
    bmat = jnp.concatenate([b_of(bbar_re), b_of(bbar_im)], axis=1)
    cmat = jnp.concatenate([c_of(c_re), -c_of(c_im)], axis=0)
    return bmat, cmat


def _s5_prepare(a_re, a_im, b_re, b_im, c_re, c_im, log_dt):
    lam_re = jnp.minimum(a_re, -1e-4)
    lam_im = a_im
    dt = jnp.exp(log_dt)[:, None]
    decay = jnp.exp(dt * lam_re)
    ang = dt * lam_im
    abar_re = decay * jnp.cos(ang)
    abar_im = decay * jnp.sin(ang)
    den = jnp.square(lam_re) + jnp.square(lam_im)
    nr = abar_re - 1.0
    ni = abar_im
    coef_re = (nr * lam_re + ni * lam_im) / den
    coef_im = (ni * lam_re - nr * lam_im) / den
    bbar_re = coef_re[..., None] * b_re - coef_im[..., None] * b_im
    bbar_im = coef_re[..., None] * b_im + coef_im[..., None] * b_re
    bmat, cmat = _blockdiag_s5(bbar_re, bbar_im, c_re, c_im)
    return abar_re.reshape(N_STATE), abar_im.reshape(N_STATE), bmat, cmat


def _blockdiag_lru(w):
    eye = jnp.eye(LRU_HEADS, dtype=F32)
    return jnp.einsum('hij,hk->hikj', w, eye).reshape(D_LRU, D_LRU)


def _rope_tables(t):
    inv_freq = ROPE_THETA ** (-jnp.arange(0, HEAD_DIM, 2, dtype=F32) / HEAD_DIM)
    ang = jnp.arange(t, dtype=F32)[:, None] * inv_freq[None, :]
    cos, sin = jnp.cos(ang), jnp.sin(ang)
    return jnp.tile(jnp.concatenate([cos, cos], axis=1), (1, 2)), jnp.tile(jnp.concatenate([-sin, sin], axis=1), (1, 2))


def _vec(v):
    return v.reshape(1, -1)


def _layer_weights(p):
    abar_re, abar_im, bmat, cmat = _s5_prepare(p['s5_a_re'], p['s5_a_im'], p['s5_b_re'], p['s5_b_im'],
                                               p['s5_c_re'], p['s5_c_im'], p['s5_log_dt'])
    return dict(
        coef_f=_s5_coefs(abar_re, abar_im, False), coef_b=_s5_coefs(abar_re, abar_im, True),
        bmat=bmat.astype(BF16), cmat=cmat.astype(BF16),
        wx=_blockdiag_lru(p['lru_wx']).astype(BF16), wa=_blockdiag_lru(p['lru_wa']).astype(BF16),
        gw=p['s5_glu_w'].astype(BF16))


def _layer_fwd(x, xb, p, w, cos, sin_s):
    t = x.shape[0]
    tm = _row_tile(t, TM)
    layer = p['layer']
    qkv, uxg = _in_proj(xb, p['w_in'], _vec(p['b_in']), cos, sin_s, layer)
    ya, lse = _attn_fwd(qkv, _vec(p['attn_sinks']))
    h5, ys = _s5_fwd(uxg, w['bmat'], w['coef_f'], w['cmat'], _vec(p['s5_d']), w['gw'], _vec(p['s5_glu_b']))
    lru_w = (p['lru_conv_w'], _vec(p['lru_conv_b']), w['wx'], _vec(p['lru_bx']), w['wa'], _vec(p['lru_ba']),
             _vec(p['lru_a_param']))
    yl, hl = _lru_fwd(uxg, *lru_w)
    mix = _rms_fwd(ya, ys, yl, _vec(p['mix_norm_g']))
    x1, x1b, xhat1, rstd1 = _matmul_ln(
        "out_proj_ln", mix, p['w_out'], _vec(p['b_out']), x, _vec(p['ln1_g']), _vec(p['ln1_b']),
        a_blk=(tm, D_MODEL), a_map=lambda i: (i, 0), w_blk=(D_MODEL, D_MODEL), parts=None, layer=layer)
    gpre, gconv, up, hmid = _ffn_hidden_fwd(x1b, p['ffn_w_gate'], p['ffn_w_up'], p['ffn_conv_w'], p['ffn_conv_b'],
                                            layer)
    x2, x2b, xhat2, rstd2 = _matmul_ln(
        "ffn_down_ln", hmid, p['ffn_w_down'], jnp.zeros((1, D_MODEL), F32), x1, _vec(p['ln2_g']), _vec(p['ln2_b']),
        a_blk=(N_CHIPS, tm, FF_SH), a_map=lambda i: (0, i, 0), w_blk=(N_CHIPS, FF_SH, D_MODEL), parts=N_CHIPS,
        layer=layer)
    saved = dict(xb=xb, uxg=uxg, qkv=qkv, ya=ya, lse=lse, h5=h5, ys=ys, yl=yl, hl=hl, mix=mix, x1b=x1b, xhat1=xhat1,
                 rstd1=rstd1, gpre=gpre, gconv=gconv, up=up, hmid=hmid, xhat2=xhat2, rstd2=rstd2, lru_w=lru_w)
    return x2, x2b, saved


def _layer_bwd(dx2, s, p, w, cos, sin_s, big):
    t = dx2.shape[0]
    tk = _row_tile(t, TMM)
    nk = t // tk
    layer = p['layer']
    big = dict(big)
    g = {}
    dr2, dr2b, g['ln2_g'], g['ln2_b'], _ = _ln_bwd(dx2, s['xhat2'], s['rstd2'], _vec(p['ln2_g']))
    dup, dgpre, g['ffn_conv_w'], g['ffn_conv_b'] = _ffn_hidden_bwd(
        dr2b, s['gpre'], s['gconv'], s['up'], p['ffn_w_down'], p['ffn_conv_w'], layer)
    big['ffn_w_down'] = _matmul(
        "d_w_down", s['hmid'], dr2b, a_blk=(None, tk, FF_SH), a_map=lambda i, j, k: (i, k, 0), b_blk=(tk, D_MODEL),
        b_map=lambda i, j, k: (k, 0), out_shape=(DEPTH, N_CHIPS, FF_SH, D_MODEL), o_blk=(None, None, FF_SH, D_MODEL),
        o_map=lambda i, j: (layer, i, 0, 0), grid=(N_CHIPS, 1, nk), dims=TN, into=big['ffn_w_down'])
    d_ffn_w = lambda name, dact, buf: _matmul(
        name, s['x1b'], dact, a_blk=(tk, D_MODEL), a_map=lambda i, j, k: (k, 0), b_blk=(None, tk, FF_SH),
        b_map=lambda i, j, k: (j, k, 0), out_shape=(DEPTH, N_CHIPS, D_MODEL, FF_SH),
        o_blk=(None, None, D_MODEL, FF_SH), o_map=lambda i, j: (layer, j, 0, 0), grid=(1, N_CHIPS, nk), dims=TN,
        into=buf)
    big['ffn_w_gate'] = d_ffn_w("d_w_gate", dgpre, big['ffn_w_gate'])
    big['ffn_w_up'] = d_ffn_w("d_w_up", dup, big['ffn_w_up'])
    wspec = dict(b_blk=(None, None, D_MODEL, FF_SH), b_map=lambda i, j, k: (layer, k, 0, 0))
    dx1 = _matmul("d_x1", dgpre, p['ffn_w_gate'], pair2=(dup, p['ffn_w_up']), a_blk=(None, tk, FF_SH),
                  a_map=lambda i, j, k: (k, i, 0), out_shape=(t, D_MODEL), o_blk=(tk, D_MODEL),
                  o_map=lambda i, j: (i, 0), grid=(nk, 1, N_CHIPS), dims=NT, add=dr2, add_scale=ALPHA, **wspec)
    dr1, dr1b, g['ln1_g'], g['ln1_b'], g['b_out'] = _ln_bwd(dx1, s['xhat1'], s['rstd1'], _vec(p['ln1_g']))
    big['w_out'] = _matmul(
        "d_w_out", s['mix'], dr1b, a_blk=(tk, D_MODEL), a_map=lambda i, j, k: (k, 0), b_blk=(tk, D_MODEL),
        b_map=lambda i, j, k: (k, 0), out_shape=(DEPTH, D_MODEL, D_MODEL), o_blk=(None, D_MODEL, D_MODEL),
        o_map=lambda i, j: (layer, 0, 0), grid=(1, 1, nk), dims=TN, into=big['w_out'])
    dmix = _matmul("d_mix", dr1b, p['w_out'], a_blk=(tk, D_MODEL), a_map=lambda i, j, k: (i, 0),
                   b_blk=(None, D_MODEL, D_MODEL), b_map=lambda i, j, k: (layer, 0, 0), out_shape=(t, D_MODEL),
                   o_blk=(tk, D_MODEL), o_map=lambda i, j: (i, 0), grid=(nk, 1, 1), dims=NT)
    dya, dys, dyl, g['mix_norm_g'] = _rms_bwd(dmix, s['ya'], s['ys'], s['yl'], _vec(p['mix_norm_g']))
    dq, dk, dv, g['attn_sinks'] = _attn_bwd(s['qkv'], s['ya'], dya, s['lse'], _vec(p['attn_sinks']))
    du, dcmat, dbmat, dabar, g['s5_d'], g['s5_glu_w'], g['s5_glu_b'] = _s5_bwd(
        s['uxg'], s['h5'], dys, w['bmat'], w['coef_b'], w['cmat'], _vec(p['s5_d']), w['gw'], _vec(p['s5_glu_b']))
    (dxc, dgate, g['lru_conv_w'], g['lru_conv_b'], dwx, g['lru_bx'], dwa, g['lru_ba'],
     g['lru_a_param']) = _lru_bwd(s['uxg'], s['hl'], dyl, *s['lru_w'])
    dproj, g['b_in'] = _assemble_dproj(dq, dk, dv, du, dxc, dgate, cos, sin_s, p['lru_conv_w'])
    big['w_in'] = _matmul(
        "d_w_in", s['xb'], dproj, a_blk=(tk, D_MODEL), a_map=lambda i, j, k: (k, 0), b_blk=(tk, IN_SH),
        b_map=lambda i, j, k: (k, j), out_shape=(DEPTH, N_CHIPS, D_MODEL, IN_SH), o_blk=(None, None, D_MODEL, IN_SH),
        o_map=lambda i, j: (layer, j, 0, 0), grid=(1, N_CHIPS, nk), dims=TN, into=big['w_in'])
    dx = _matmul("d_x", dproj, p['w_in'], a_blk=(tk, IN_SH), a_map=lambda i, j, k: (i, k),
                 b_blk=(None, None, D_MODEL, IN_SH), b_map=lambda i, j, k: (layer, k, 0, 0), out_shape=(t, D_MODEL),
                 o_blk=(tk, D_MODEL), o_map=lambda i, j: (i, 0), grid=(nk, 1, N_CHIPS), dims=NT,
                 add=dr1, add_scale=ALPHA)
    return dx, _param_chain(g, p, dabar, dbmat, dcmat, dwx, dwa), big


def _layer_fwd_v1(x, p, w, cos, sin_s):
    t = x.shape[0]
    nt = t // _row_tile(t, TM)
    tm = t // nt
    proj = _matmul("in_proj", x, p['w_in'], a_blk=(tm, D_MODEL), a_map=lambda i, j, k: (i, 0),
                   b_blk=(None, D_MODEL, IN_SH), b_map=lambda i, j, k: (j, 0, 0), out_shape=(t, D_IN),
                   o_blk=(tm, IN_SH), o_map=lambda i, j: (i, j), grid=(nt, N_CHIPS, 1), dims=NN,
                   bias=_vec(p['b_in']), bias_blk=(1, IN_SH), bias_map=lambda i, j, k: (0, j))
    qkv = _qkv_post(proj, cos, sin_s)
    ya, lse = _attn_fwd(qkv, _vec(p['attn_sinks']))
    h5, ys = _s5_fwd(proj, w['bmat'], w['coef_f'], w['cmat'], _vec(p['s5_d']), w['gw'], _vec(p['s5_glu_b']))
    lru_w = (p['lru_conv_w'], _vec(p['lru_conv_b']), w['wx'], _vec(p['lru_bx']), w['wa'], _vec(p['lru_ba']),
             _vec(p['lru_a_param']))
    yl, hl = _lru_fwd(proj, *lru_w)
    mix = _rms_fwd(ya, ys, yl, _vec(p['mix_norm_g']))
    f1 = _matmul("out_proj", mix, p['w_out'], a_blk=(tm, D_MODEL), a_map=lambda i, j, k: (i, 0),
                 b_blk=(D_MODEL, D_MODEL), b_map=lambda i, j, k: (0, 0), out_shape=(t, D_MODEL),
                 o_blk=(tm, D_MODEL), o_map=lambda i, j: (i, 0), grid=(nt, 1, 1), dims=NN,
                 bias=_vec(p['b_out']), bias_blk=(1, D_MODEL), bias_map=lambda i, j, k: (0, 0))
    x1, xhat1, rstd1 = _ln_fwd(x, f1, _vec(p['ln1_g']), _vec(p['ln1_b']))
    ffn_in = lambda name, wmat: _matmul(
        name, x1, wmat, a_blk=(tm, D_MODEL), a_map=lambda i, j, k: (i, 0), b_blk=(None, D_MODEL, FF_SH),
        b_map=lambda i, j, k: (j, 0, 0), out_shape=(N_CHIPS, t, FF_SH), o_blk=(None, tm, FF_SH),
        o_map=lambda i, j: (j, i, 0), grid=(nt, N_CHIPS, 1), dims=NN)
    gpre = ffn_in("ffn_gate", p['ffn_w_gate'])
    up = ffn_in("ffn_up", p['ffn_w_up'])
    hmid = _ffn_mid_fwd(gpre, up, p['ffn_conv_w'], p['ffn_conv_b'])
    f2 = _matmul("ffn_down", hmid, p['ffn_w_down'], a_blk=(None, tm, FF_SH), a_map=lambda i, j, k: (k, i, 0),
                 b_blk=(None, FF_SH, D_MODEL), b_map=lambda i, j, k: (k, 0, 0), out_shape=(t, D_MODEL),
                 o_blk=(tm, D_MODEL), o_map=lambda i, j: (i, 0), grid=(nt, 1, N_CHIPS), dims=NN)
    x2, xhat2, rstd2 = _ln_fwd(x1, f2, _vec(p['ln2_g']), _vec(p['ln2_b']))
    saved = dict(x=x, proj=proj, qkv=qkv, ya=ya, lse=lse, h5=h5, ys=ys, yl=yl, hl=hl, mix=mix, x1=x1, xhat1=xhat1,
                 rstd1=rstd1, gpre=gpre, up=up, xhat2=xhat2, rstd2=rstd2, lru_w=lru_w)
    return x2, saved


def _param_chain(g, p, dabar, dbmat, dcmat, dwx, dwa):
    s5_names = ('s5_a_re', 's5_a_im', 's5_b_re', 's5_b_im', 's5_c_re', 's5_c_im', 's5_log_dt')
    _, s5_vjp = jax.vjp(_s5_prepare, *[p[n] for n in s5_names])
    for n, val in zip(s5_names, s5_vjp((dabar[0], dabar[1], dbmat, dcmat))):
        g[n] = val
    g['lru_wx'] = jax.vjp(_blockdiag_lru, p['lru_wx'])[1](dwx)[0]
    g['lru_wa'] = jax.vjp(_blockdiag_lru, p['lru_wa'])[1](dwa)[0]
    return g


def _layer_bwd_v1(dx2, s, p, w, cos, sin_s):
    t = dx2.shape[0]
    nt = t // _row_tile(t, TM)
    tm = t // nt
    g = {}
    dr2, g['ln2_g'], g['ln2_b'], _ = _ln_bwd(dx2, s['xhat2'], s['rstd2'], _vec(p['ln2_g']))
    dhmid = _matmul("d_hmid", dr2, p['ffn_w_down'], a_blk=(tm, D_MODEL), a_map=lambda i, j, k: (i, 0),
                    b_blk=(None, FF_SH, D_MODEL), b_map=lambda i, j, k: (j, 0, 0), out_shape=(N_CHIPS, t, FF_SH),
                    o_blk=(None, tm, FF_SH), o_map=lambda i, j: (j, i, 0), grid=(nt, N_CHIPS, 1), dims=NT)
    hmid, dup, dgc, g['ffn_conv_w'], g['ffn_conv_b'] = _ffn_mid_bwd(s['gpre'], s['up'], dhmid, p['ffn_conv_w'],
                                                                    p['ffn_conv_b'])
    g['ffn_w_down'] = _matmul("d_w_down", hmid, dr2, a_blk=(None, tm, FF_SH), a_map=lambda i, j, k: (i, k, 0),
                              b_blk=(tm, D_MODEL), b_map=lambda i, j, k: (k, 0), out_shape=(N_CHIPS, FF_SH, D_MODEL),
                              o_blk=(None, FF_SH, D_MODEL), o_map=lambda i, j: (i, 0, 0), grid=(N_CHIPS, 1, nt), dims=TN)
    dgpre = _ffn_conv_t(dgc, p['ffn_conv_w'])
    d_ffn_w = lambda name, dact: _matmul(
        name, s['x1'], dact, a_blk=(tm, D_MODEL), a_map=lambda i, j, k: (k, 0), b_blk=(None, tm, FF_SH),
        b_map=lambda i, j, k: (j, k, 0), out_shape=(N_CHIPS, D_MODEL, FF_SH), o_blk=(None, D_MODEL, FF_SH),
        o_map=lambda i, j: (j, 0, 0), grid=(1, N_CHIPS, nt), dims=TN)
    g['ffn_w_gate'] = d_ffn_w("d_w_gate", dgpre)
    g['ffn_w_up'] = d_ffn_w("d_w_up", dup)
    d_ffn_x = lambda name, dact, wmat, add, scale: _matmul(
        name, dact, wmat, a_blk=(None, tm, FF_SH), a_map=lambda i, j, k: (k, i, 0), b_blk=(None, D_MODEL, FF_SH),
        b_map=lambda i, j, k: (k, 0, 0), out_shape=(t, D_MODEL), o_blk=(tm, D_MODEL), o_map=lambda i, j: (i, 0),
        grid=(nt, 1, N_CHIPS), dims=NT, add=add, add_scale=scale)
    dx1 = d_ffn_x("d_x1_gate", dgpre, p['ffn_w_gate'], dr2, ALPHA)
    dx1 = d_ffn_x("d_x1_up", dup, p['ffn_w_up'], dx1, 1.0)
    dr1, g['ln1_g'], g['ln1_b'], g['b_out'] = _ln_bwd(dx1, s['xhat1'], s['rstd1'], _vec(p['ln1_g']))
    g['w_out'] = _matmul("d_w_out", s['mix'], dr1, a_blk=(tm, D_MODEL), a_map=lambda i, j, k: (k, 0),
                         b_blk=(tm, D_MODEL), b_map=lambda i, j, k: (k, 0), out_shape=(D_MODEL, D_MODEL),
                         o_blk=(D_MODEL, D_MODEL), o_map=lambda i, j: (0, 0), grid=(1, 1, nt), dims=TN)
    dmix = _matmul("d_mix", dr1, p['w_out'], a_blk=(tm, D_MODEL), a_map=lambda i, j, k: (i, 0),
                   b_blk=(D_MODEL, D_MODEL), b_map=lambda i, j, k: (0, 0), out_shape=(t, D_MODEL),
                   o_blk=(tm, D_MODEL), o_map=lambda i, j: (i, 0), grid=(nt, 1, 1), dims=NT)
    dya, dys, dyl, g['mix_norm_g'] = _rms_bwd(dmix, s['ya'], s['ys'], s['yl'], _vec(p['mix_norm_g']))
    dq, dk, dv, g['attn_sinks'] = _attn_bwd(s['qkv'], s['ya'], dya, s['lse'], _vec(p['attn_sinks']))
    du, dcmat, dbmat, dabar, g['s5_d'], g['s5_glu_w'], g['s5_glu_b'] = _s5_bwd(
        s['proj'], s['h5'], dys, w['bmat'], w['coef_b'], w['cmat'], _vec(p['s5_d']), w['gw'], _vec(p['s5_glu_b']))
    (dxc, dgate, g['lru_conv_w'], g['lru_conv_b'], dwx, g['lru_bx'], dwa, g['lru_ba'],
     g['lru_a_param']) = _lru_bwd(s['proj'], s['hl'], dyl, *s['lru_w'])
    dproj, g['b_in'] = _assemble_dproj(dq, dk, dv, du, dxc, dgate, cos, sin_s, p['lru_conv_w'])
    g['w_in'] = _matmul("d_w_in", s['x'], dproj, a_blk=(tm, D_MODEL), a_map=lambda i, j, k: (k, 0),
                        b_blk=(tm, IN_SH), b_map=lambda i, j, k: (k, j), out_shape=(N_CHIPS, D_MODEL, IN_SH),
                        o_blk=(None, D_MODEL, IN_SH), o_map=lambda i, j: (j, 0, 0), grid=(1, N_CHIPS, nt), dims=TN)
    dx = _matmul("d_x", dproj, p['w_in'], a_blk=(tm, IN_SH), a_map=lambda i, j, k: (i, k),
                 b_blk=(None, D_MODEL, IN_SH), b_map=lambda i, j, k: (k, 0, 0), out_shape=(t, D_MODEL),
                 o_blk=(tm, D_MODEL), o_map=lambda i, j: (i, 0), grid=(nt, 1, N_CHIPS), dims=NT,
                 add=dr1, add_scale=ALPHA)
    s5_names = ('s5_a_re', 's5_a_im', 's5_b_re', 's5_b_im', 's5_c_re', 's5_c_im', 's5_log_dt')
    _, s5_vjp = jax.vjp(_s5_prepare, *[p[n] for n in s5_names])
    for n, val in zip(s5_names, s5_vjp((dabar[0], dabar[1], dbmat, dcmat))):
        g[n] = val
    g['lru_wx'] = jax.vjp(_blockdiag_lru, p['lru_wx'])[1](dwx)[0]
    g['lru_wa'] = jax.vjp(_blockdiag_lru, p['lru_wa'])[1](dwa)[0]
    return dx, g


ROW_TILE = 512


def _pick_rows(rows):
    for rt in range(min(rows, ROW_TILE), 0, -1):
        if rows % rt == 0 and (rt % 16 == 0 or rt == rows):
            return rt
    return rows


def _cast_bf16(a):
    a2 = a.reshape(-1, a.shape[-1])
    rows, c = a2.shape
    rt = _pick_rows(rows)

    def kern(a_ref, o_ref):
        o_ref[...] = a_ref[...].astype(BF16)

    spec = pl.BlockSpec((rt, c), lambda i: (i, 0))
    out = pl.pallas_call(kern, name="cast_bf16", grid=(rows // rt,), in_specs=[spec], out_specs=spec,
                         out_shape=jax.ShapeDtypeStruct((rows, c), BF16), compiler_params=_params(("parallel",)))(a2)
    return out.reshape(a.shape)


def _sum_parts(name, parts, shape):
    c = shape[-1]
    rows = math.prod(shape[:-1])
    rt = _pick_rows(rows)
    n = len(parts)

    def kern(*refs):
        acc = refs[0][...].astype(F32)
        for r in refs[1:n]:
            acc = acc + r[...].astype(F32)
        refs[n][...] = acc

    specs, args = [], []
    for arr, j in parts:
        if j is None:
            specs.append(pl.BlockSpec((rt, c), lambda i: (i, 0)))
            args.append(arr.reshape(rows, c))
        else:
            specs.append(pl.BlockSpec((None, rt, c), functools.partial(lambda i, jj: (jj, i, 0), jj=j)))
            args.append(arr.reshape(arr.shape[0], rows, c))
    out = pl.pallas_call(kern, name=name, grid=(rows // rt,), in_specs=specs,
                         out_specs=pl.BlockSpec((rt, c), lambda i: (i, 0)),
                         out_shape=jax.ShapeDtypeStruct((rows, c), F32), compiler_params=_params(("parallel",)))(*args)
    return out.reshape(shape)


def _adamw(name, w, g, m, v):
    shape = w.shape
    c = shape[-1]
    rows = math.prod(shape[:-1])
    rt = _pick_rows(rows)

    def kern(w_ref, g_ref, m_ref, v_ref, d_ref, nm_ref, nv_ref):
        g_ = g_ref[...]
        m_ = ADAM_B1 * m_ref[...] + (1.0 - ADAM_B1) * g_
        v_ = ADAM_B2 * v_ref[...] + (1.0 - ADAM_B2) * jnp.square(g_)
        m_hat = m_ / (1.0 - ADAM_B1 ** ADAM_STEP)
        v_hat = v_ / (1.0 - ADAM_B2 ** ADAM_STEP)
        d_ref[...] = -ADAM_LR * (m_hat / (jnp.sqrt(v_hat) + ADAM_EPS) + ADAM_WD * w_ref[...])
        nm_ref[...] = m_
        nv_ref[...] = v_

    spec = pl.BlockSpec((rt, c), lambda i: (i, 0))
    outs = pl.pallas_call(kern, name=name, grid=(rows // rt,), in_specs=[spec] * 4, out_specs=[spec] * 3,
                          out_shape=[jax.ShapeDtypeStruct((rows, c), F32)] * 3,
                          compiler_params=_params(("parallel",)))(*[a.reshape(rows, c) for a in (w, g, m, v)])
    return tuple(o.reshape(shape) for o in outs)


def _position():
    return lax.axis_index("x"), lax.axis_index("y"), lax.axis_index("c")


def _other_chips(x, y):
    return [(1 - x, y), (x, 1 - y), (1 - x, 1 - y)]


def _comm_call(name, kern, arrs, out_shapes, n_remote, n_local):
    return pl.pallas_call(
        kern, name=name, in_specs=[ANY] * len(arrs), out_specs=[ANY] * len(out_shapes), out_shape=out_shapes,
        scratch_shapes=[pltpu.SemaphoreType.DMA((n_remote,)), pltpu.SemaphoreType.DMA((n_remote,)),
                        pltpu.SemaphoreType.DMA((n_local,))],
    )(*arrs)


def _allgather_chips(arrs):
    n = len(arrs)

    def kern(*refs):
        ins, outs = refs[:n], refs[n:2 * n]
        send, recv, loc = refs[2 * n:]
        x, y, c = _position()
        me = 2 * x + y
        chips = _other_chips(x, y)
        own, sent = [], []
        for t in range(n):
            own.append(pltpu.make_async_copy(ins[t], outs[t].at[:, pl.ds(me, 1)], loc.at[t]))
            own[-1].start()
            for j, (px, py) in enumerate(chips):
                sent.append(pltpu.make_async_remote_copy(
                    src_ref=ins[t], dst_ref=outs[t].at[:, pl.ds(me, 1)], send_sem=send.at[3 * t + j],
                    recv_sem=recv.at[3 * t + j], device_id=(px, py, c), device_id_type=MESH))
                sent[-1].start()
        for t in range(n):
            for j, (px, py) in enumerate(chips):
                pltpu.make_async_remote_copy(
                    src_ref=ins[t], dst_ref=outs[t].at[:, pl.ds(2 * px + py, 1)], send_sem=send.at[3 * t + j],
                    recv_sem=recv.at[3 * t + j], device_id=(px, py, c), device_id_type=MESH).wait_recv()
        for cp in sent:
            cp.wait_send()
        for cp in own:
            cp.wait()

    outs = [jax.ShapeDtypeStruct((a.shape[0], N_CHIPS) + a.shape[2:], a.dtype) for a in arrs]
    return _comm_call("allgather_chips", kern, arrs, outs, 3 * n, n)


def _pair_exchange(arrs):
    n = len(arrs)

    def kern(*refs):
        ins, outs = refs[:n], refs[n:3 * n]
        send, recv, loc = refs[3 * n:]
        x, y, c = _position()
        own, sent = [], []
        for t in range(n):
            r2 = ins[t].shape[2] // 2
            own.append(pltpu.make_async_copy(ins[t].at[:, :, pl.ds(c * r2, r2)], outs[2 * t], loc.at[t]))
            own[-1].start()
            sent.append(pltpu.make_async_remote_copy(
                src_ref=ins[t].at[:, :, pl.ds((1 - c) * r2, r2)], dst_ref=outs[2 * t + 1], send_sem=send.at[t],
                recv_sem=recv.at[t], device_id=(x, y, 1 - c), device_id_type=MESH))
            sent[-1].start()
        for cp in sent:
            cp.wait()
        for cp in own:
            cp.wait()

    outs = []
    for a in arrs:
        half = jax.ShapeDtypeStruct(a.shape[:2] + (a.shape[2] // 2, a.shape[3]), a.dtype)
        outs += [half, half]
    return _comm_call("pair_exchange", kern, arrs, outs, n, n)


def _chip_scatter(arrs):
    n = len(arrs)

    def kern(*refs):
        ins, outs = refs[:n], refs[n:3 * n]
        send, recv, loc = refs[3 * n:]
        x, y, c = _position()
        me = 2 * x + y
        chips = _other_chips(x, y)
        own, sent = [], []
        for t in range(n):
            own.append(pltpu.make_async_copy(ins[t].at[:, pl.ds(me, 1)], outs[2 * t], loc.at[t]))
            own[-1].start()
            for j, (px, py) in enumerate(chips):
                sent.append(pltpu.make_async_remote_copy(
                    src_ref=ins[t].at[:, pl.ds(2 * px + py, 1)], dst_ref=outs[2 * t + 1].at[j],
                    send_sem=send.at[3 * t + j], recv_sem=recv.at[3 * t + j], device_id=(px, py, c),
                    device_id_type=MESH))
                sent[-1].start()
        for cp in sent:
            cp.wait()
        for cp in own:
            cp.wait()

    outs = []
    for a in arrs:
        one = (a.shape[0], 1) + a.shape[2:]
        outs += [jax.ShapeDtypeStruct(one, a.dtype), jax.ShapeDtypeStruct((3,) + one, a.dtype)]
    return _comm_call("chip_scatter", kern, arrs, outs, 3 * n, n)


def _pair_gather(arrs):
    n = len(arrs)

    def kern(*refs):
        ins, outs = refs[:n], refs[n:2 * n]
        send, recv, loc = refs[2 * n:]
        x, y, c = _position()
        own, sent = [], []
        for t in range(n):
            own.append(pltpu.make_async_copy(ins[t], outs[t].at[:, pl.ds(c, 1)], loc.at[t]))
            own[-1].start()
            sent.append(pltpu.make_async_remote_copy(
                src_ref=ins[t], dst_ref=outs[t].at[:, pl.ds(c, 1)], send_sem=send.at[t], recv_sem=recv.at[t],
                device_id=(x, y, 1 - c), device_id_type=MESH))
            sent[-1].start()
        for t in range(n):
            sent[t].wait_send()
            pltpu.make_async_remote_copy(
                src_ref=ins[t], dst_ref=outs[t].at[:, pl.ds(1 - c, 1)], send_sem=send.at[t], recv_sem=recv.at[t],
                device_id=(x, y, 1 - c), device_id_type=MESH).wait_recv()
        for cp in own:
            cp.wait()

    outs = [jax.ShapeDtypeStruct((a.shape[0], 2) + a.shape[2:], a.dtype) for a in arrs]
    return _comm_call("pair_gather", kern, arrs, outs, n, n)


_FLIPS = [(0, 0, 1), (1, 0, 0), (0, 1, 0), (1, 1, 0), (1, 0, 1), (0, 1, 1), (1, 1, 1)]


def _allgather_devices(v):
    def kern(v_ref, o_ref, send, recv, loc):
        x, y, c = _position()
        me = 4 * x + 2 * y + c
        peers = [((1 - x) if fx else x, (1 - y) if fy else y, (1 - c) if fc else c) for fx, fy, fc in _FLIPS]
        own = pltpu.make_async_copy(v_ref, o_ref.at[pl.ds(me, 1)], loc.at[0])
        own.start()
        sent = []
        for k, peer in enumerate(peers):
            sent.append(pltpu.make_async_remote_copy(
                src_ref=v_ref, dst_ref=o_ref.at[pl.ds(me, 1)], send_sem=send.at[k], recv_sem=recv.at[k],
                device_id=peer, device_id_type=MESH))
            sent[-1].start()
        for k, (px, py, pc) in enumerate(peers):
            pltpu.make_async_remote_copy(
                src_ref=v_ref, dst_ref=o_ref.at[pl.ds(4 * px + 2 * py + pc, 1)], send_sem=send.at[k],
                recv_sem=recv.at[k], device_id=(px, py, pc), device_id_type=MESH).wait_recv()
        for cp in sent:
            cp.wait_send()
        own.wait()

    out = jax.ShapeDtypeStruct((N_DEV,) + v.shape[1:], v.dtype)
    return _comm_call("allgather_devices", kern, [v], [out], len(_FLIPS), 1)[0]


def _exchange(name, arrs, out_shapes, n_local, n_remote, plan):
    n_in, n_out = len(arrs), len(out_shapes)

    def kern(*refs):
        ins, outs = refs[:n_in], refs[n_in:n_in + n_out]
        send, recv, loc = refs[n_in + n_out:]
        local, remote = plan(ins, outs, *_position())
        assert len(local) == n_local and len(remote) == n_remote
        own = [pltpu.make_async_copy(s, d, loc.at[k]) for k, (s, d) in enumerate(local)]
        for cp in own:
            cp.start()
        sent = [pltpu.make_async_remote_copy(src_ref=s, dst_ref=d, send_sem=send.at[k], recv_sem=recv.at[k],
                                             device_id=peer, device_id_type=MESH)
                for k, (s, d, peer, _) in enumerate(remote)]
        for cp in sent:
            cp.start()
        for k, (s, _, peer, landing) in enumerate(remote):
            pltpu.make_async_remote_copy(src_ref=s, dst_ref=landing, send_sem=send.at[k], recv_sem=recv.at[k],
                                         device_id=peer, device_id_type=MESH).wait_recv()
        for cp in sent:
            cp.wait_send()
        for cp in own:
            cp.wait()

    return pl.pallas_call(
        kern, name=name, in_specs=[ANY] * n_in, out_specs=[ANY] * n_out, out_shape=out_shapes,
        scratch_shapes=[pltpu.SemaphoreType.DMA((n_remote,)), pltpu.SemaphoreType.DMA((n_remote,)),
                        pltpu.SemaphoreType.DMA((max(n_local, 1),))],
    )(*arrs)


def _allgather_chips(arrs, halved=()):
    n = len(arrs)
    layers = arrs[0].shape[0]

    def plan(ins, outs, x, y, c):
        me = 2 * x + y
        local, remote = [], []
        for t in range(n):
            for l in range(layers):
                src = ins[t].at[l]
                if t in halved:
                    r2 = ins[t].shape[2] // 2
                    src = ins[t].at[l, :, pl.ds(c * r2, r2)]
                local.append((src, outs[t].at[l, pl.ds(me, 1)]))
                for px, py in _other_chips(x, y):
                    remote.append((src, outs[t].at[l, pl.ds(me, 1)], (px, py, c),
                                   outs[t].at[l, pl.ds(2 * px + py, 1)]))
        return local, remote

    outs = []
    for t, a in enumerate(arrs):
        tail = (a.shape[2] // 2,) + a.shape[3:] if t in halved else a.shape[2:]
        outs.append(jax.ShapeDtypeStruct((a.shape[0], N_CHIPS) + tail, a.dtype))
    return _exchange("allgather_chips", arrs, outs, n * layers, 3 * n * layers, plan)


def _pair_exchange(arrs):
    n = len(arrs)
    layers, shards = arrs[0].shape[:2]

    def plan(ins, outs, x, y, c):
        local, remote = [], []
        for t in range(n):
            r2 = ins[t].shape[2] // 2
            for l in range(layers):
                for s in range(shards):
                    local.append((ins[t].at[l, s, pl.ds(c * r2, r2)], outs[2 * t].at[l, s]))
                    remote.append((ins[t].at[l, s, pl.ds((1 - c) * r2, r2)], outs[2 * t + 1].at[l, s],
                                   (x, y, 1 - c), outs[2 * t + 1].at[l, s]))
        return local, remote

    outs = []
    for a in arrs:
        half = jax.ShapeDtypeStruct(a.shape[:2] + (a.shape[2] // 2, a.shape[3]), a.dtype)
        outs += [half, half]
    return _exchange("pair_exchange", arrs, outs, n * layers * shards, n * layers * shards, plan)


def _chip_scatter(arrs):
    n = len(arrs)
    layers = arrs[0].shape[0]

    def plan(ins, outs, x, y, c):
        me = 2 * x + y
        local, remote = [], []
        for t in range(n):
            for l in range(layers):
                local.append((ins[t].at[l, pl.ds(me, 1)], outs[2 * t].at[l]))
                for j, (px, py) in enumerate(_other_chips(x, y)):
                    remote.append((ins[t].at[l, pl.ds(2 * px + py, 1)], outs[2 * t + 1].at[j, l], (px, py, c),
                                   outs[2 * t + 1].at[j, l]))
        return local, remote

    outs = []
    for a in arrs:
        one = (a.shape[0], 1) + a.shape[2:]
        outs += [jax.ShapeDtypeStruct(one, a.dtype), jax.ShapeDtypeStruct((3,) + one, a.dtype)]
    return _exchange("chip_scatter", arrs, outs, n * layers, 3 * n * layers, plan)


def _pair_gather(arrs):
    n = len(arrs)
    layers = arrs[0].shape[0]

    def plan(ins, outs, x, y, c):
        local, remote = [], []
        for t in range(n):
            for l in range(layers):
                local.append((ins[t].at[l], outs[t].at[l, pl.ds(c, 1)]))
                remote.append((ins[t].at[l], outs[t].at[l, pl.ds(c, 1)], (x, y, 1 - c),
                               outs[t].at[l, pl.ds(1 - c, 1)]))
        return local, remote

    outs = [jax.ShapeDtypeStruct((a.shape[0], 2) + a.shape[2:], a.dtype) for a in arrs]
    return _exchange("pair_gather", arrs, outs, n * layers, n * layers, plan)


GATHER_PIECES = 4


def _allgather_devices(v):
    rq = v.shape[1] // GATHER_PIECES

    def plan(ins, outs, x, y, c):
        me = 4 * x + 2 * y + c
        local, remote = [], []
        for q in range(GATHER_PIECES):
            rows = pl.ds(q * rq, rq)
            local.append((ins[0].at[0, rows], outs[0].at[me, rows]))
            for fx, fy, fc in _FLIPS:
                px, py, pc = (1 - x) if fx else x, (1 - y) if fy else y, (1 - c) if fc else c
                remote.append((ins[0].at[0, rows], outs[0].at[me, rows], (px, py, pc),
                               outs[0].at[4 * px + 2 * py + pc, rows]))
        return local, remote

    out = jax.ShapeDtypeStruct((N_DEV,) + v.shape[1:], v.dtype)
    return _exchange("allgather_devices", [v], [out], GATHER_PIECES, GATHER_PIECES * len(_FLIPS), plan)[0]


WEIGHTS = ['w_in', 'b_in', 'attn_sinks', 's5_a_re', 's5_a_im', 's5_b_re', 's5_b_im', 's5_c_re', 's5_c_im', 's5_d',
           's5_log_dt', 's5_glu_w', 's5_glu_b', 'lru_conv_w', 'lru_conv_b', 'lru_wx', 'lru_bx', 'lru_wa', 'lru_ba',
           'lru_a_param', 'mix_norm_g', 'w_out', 'b_out', 'ln1_g', 'ln1_b', 'ffn_w_gate', 'ffn_w_up', 'ffn_conv_w',
           'ffn_conv_b', 'ffn_w_down', 'ln2_g', 'ln2_b']
BIG = ('w_in', 'w_out', 'ffn_w_gate', 'ffn_w_up', 'ffn_w_down')
SMALL = tuple(n for n in WEIGHTS if n not in BIG)
PACK_ROWS = ROW_TILE


def _pack(arrs):
    flat = jnp.concatenate([a.reshape(-1) for a in arrs])
    unit = 128 * PACK_ROWS
    size = -(-flat.shape[0] // unit) * unit
    return jnp.pad(flat, (0, size - flat.shape[0])).reshape(-1, 128)


def _unpack(packed, shapes):
    flat = packed.reshape(-1)
    out, pos = [], 0
    for shp in shapes:
        n = math.prod(shp)
        out.append(flat[pos:pos + n].reshape(shp))
        pos += n
    return out


def _pair_reduce(name, g):
    layers, shards, rows, cols = g.shape
    r2 = rows // 2
    rt = _pick_rows(r2)
    nr = r2 // rt
    nsteps = layers * shards * nr

    def kern(c_ref, mine_ref, other_ref, o_ref, buf, send, recv, credit):
        x, y, c = _position()
        sibling = (x, y, 1 - c)
        k = pl.program_id(0) * nr + pl.program_id(1)
        slot = k % 2

        @pl.when(k >= 2)
        def _():
            pl.semaphore_wait(credit, 1)

        cp = pltpu.make_async_remote_copy(src_ref=other_ref, dst_ref=buf.at[slot], send_sem=send.at[slot],
                                          recv_sem=recv.at[slot], device_id=sibling, device_id_type=MESH)
        cp.start()
        cp.wait_recv()
        o_ref[...] = (mine_ref[...] + buf[slot]).astype(BF16)
        cp.wait_send()

        @pl.when(k + 2 < nsteps)
        def _():
            pl.semaphore_signal(credit, 1, device_id=sibling, device_id_type=MESH)

    blk = (1, rt, cols)
    grid_spec = pltpu.PrefetchScalarGridSpec(
        num_scalar_prefetch=1, grid=(layers * shards, nr),
        in_specs=[pl.BlockSpec(blk, lambda m, r, c_ref: (m, c_ref[0] * nr + r, 0)),
                  pl.BlockSpec(blk, lambda m, r, c_ref: (m, (1 - c_ref[0]) * nr + r, 0))],
        out_specs=pl.BlockSpec(blk, lambda m, r, c_ref: (m, r, 0)),
        scratch_shapes=[pltpu.VMEM((2,) + blk, F32), pltpu.SemaphoreType.DMA((2,)),
                        pltpu.SemaphoreType.DMA((2,)), pltpu.SemaphoreType.REGULAR])
    core = lax.axis_index("c").astype(jnp.int32).reshape(1)
    g3 = g.reshape(layers * shards, rows, cols)
    out = pl.pallas_call(
        kern, name=name, grid_spec=grid_spec,
        out_shape=jax.ShapeDtypeStruct((layers * shards, r2, cols), BF16),
        compiler_params=_params(("arbitrary", "arbitrary")),
    )(core, g3, g3)
    return out.reshape(layers, shards, r2, cols)


def _pair_merge(name, h):
    m, r2, cols = h.shape
    rt = _pick_rows(r2)
    nr = r2 // rt
    nsteps = m * nr

    def kern(h_ref, o_ref, buf, send, recv, credit):
        x, y, c = _position()
        sibling = (x, y, 1 - c)
        k = pl.program_id(0) * nr + pl.program_id(1)
        slot = k % 2

        @pl.when(k >= 2)
        def _():
            pl.semaphore_wait(credit, 1)

        cp = pltpu.make_async_remote_copy(src_ref=h_ref, dst_ref=buf.at[slot], send_sem=send.at[slot],
                                          recv_sem=recv.at[slot], device_id=sibling, device_id_type=MESH)
        cp.start()
        cp.wait_recv()
        o_ref[0, pl.ds(c, 1)] = h_ref[...]
        o_ref[0, pl.ds(1 - c, 1)] = buf[slot]
        cp.wait_send()

        @pl.when(k + 2 < nsteps)
        def _():
            pl.semaphore_signal(credit, 1, device_id=sibling, device_id_type=MESH)

    blk = (1, rt, cols)
    out = pl.pallas_call(
        kern, name=name, grid=(m, nr),
        in_specs=[pl.BlockSpec(blk, lambda i, r: (i, r, 0))],
        out_specs=pl.BlockSpec((1, 2, rt, cols), lambda i, r: (i, 0, r, 0)),
        out_shape=jax.ShapeDtypeStruct((m, 2, r2, cols), h.dtype),
        scratch_shapes=[pltpu.VMEM((2,) + blk, h.dtype), pltpu.SemaphoreType.DMA((2,)),
                        pltpu.SemaphoreType.DMA((2,)), pltpu.SemaphoreType.REGULAR],
        compiler_params=_params(("arbitrary", "arbitrary")),
    )(h)
    return out.reshape(m, 2 * r2, cols)


def _reduce_big(grads):
    pair = [_pair_reduce("pair_reduce_" + n, g) for n, g in zip(BIG, grads)]
    scat = _chip_scatter(pair)
    out = []
    for t, n in enumerate(BIG):
        own, got = scat[2 * t], scat[2 * t + 1]
        half = _sum_parts("chip_sum", [(own, None)] + [(got, j) for j in range(3)], own.shape)
        out.append(_pair_merge("grad_merge_" + n, half.reshape(half.shape[0], half.shape[2], half.shape[3])))
    return out


def _step(a):
    x = a['x'][0]
    target = a['loss_target'][0]
    t = x.shape[0]
    xi, yi, _ = _position()
    chip = 2 * xi + yi
    cos, sin_s = _rope_tables(t)

    gathered = _allgather_chips([_cast_bf16(a[n])[:, None] for n in BIG]
                                + [a[n][:, None] for n in ('s5_glu_w', 'lru_conv_w', 'ffn_conv_w')],
                                halved=range(len(BIG)))
    full = dict(zip(BIG + ('s5_glu_w', 'lru_conv_w', 'ffn_conv_w'), gathered))
    for n in BIG:
        layers, chips, r2, cols = full[n].shape
        full[n] = _pair_merge("weight_merge_" + n, full[n].reshape(layers * chips, r2, cols)).reshape(
            layers, chips, 2 * r2, cols)

    def layer_params(l):
        p = {n: a[n][l] for n in SMALL}
        p['layer'] = l
        p['w_in'] = full['w_in']
        p['w_out'] = full['w_out'].reshape(DEPTH, D_MODEL, D_MODEL)
        p['ffn_w_gate'] = full['ffn_w_gate']
        p['ffn_w_up'] = full['ffn_w_up']
        p['ffn_w_down'] = full['ffn_w_down']
        p['s5_glu_w'] = full['s5_glu_w'][l].reshape(D_S5, D_S5)
        p['lru_conv_w'] = full['lru_conv_w'][l].transpose(1, 0, 2).reshape(LRU_CONV, D_LRU)
        p['ffn_conv_w'] = full['ffn_conv_w'][l]
        p['ffn_conv_b'] = a['ffn_conv_b'][l].reshape(N_CHIPS, 1, FF_SH)
        return p

    params = [layer_params(l) for l in range(DEPTH)]
    derived = [_layer_weights(p) for p in params]
    saved = []
    h, hb = x, _cast_bf16(x)
    for l in range(DEPTH):
        h, hb, s = _layer_fwd(h, hb, params[l], derived[l], cos, sin_s)
        saved.append(s)
    loss_part, dh = _loss_head(h, target)
    loss = lax.psum(loss_part[0, 0], ("x", "y", "c"))
    grads = [None] * DEPTH
    big = {n: lax.empty((DEPTH, N_CHIPS) + a[n].shape[1:], F32) for n in BIG}
    big['w_out'] = big['w_out'].reshape(DEPTH, D_MODEL, D_MODEL)
    for l in reversed(range(DEPTH)):
        dh, grads[l], big = _layer_bwd(dh, saved[l], params[l], derived[l], cos, sin_s, big)
    grad_x = dh[None]

    def stacked(n):
        return jnp.stack([grads[l][n] for l in range(DEPTH)])

    big['w_out'] = big['w_out'].reshape(DEPTH, N_CHIPS, OUT_SH, D_MODEL)
    grad = dict(zip(BIG, _reduce_big([big[n] for n in BIG])))
    small_local = [stacked(n) for n in SMALL]
    packed = _allgather_devices(_pack(small_local)[None])
    total = _sum_parts("device_sum", [(packed, j) for j in range(N_DEV)], packed.shape[1:])
    small_sum = dict(zip(SMALL, _unpack(total, [g.shape for g in small_local])))
    for n in SMALL:
        g = small_sum[n]
        if n == 's5_glu_w':
            g = lax.dynamic_slice_in_dim(g, chip * (D_S5 // N_CHIPS), D_S5 // N_CHIPS, axis=1)
        elif n == 'lru_conv_w':
            g = lax.dynamic_slice_in_dim(g, chip * (D_LRU // N_CHIPS), D_LRU // N_CHIPS, axis=2)
        elif n == 'ffn_conv_w':
            g = lax.dynamic_index_in_dim(g, chip, axis=1, keepdims=False)
        grad[n] = g.reshape(a[n].shape)

    delta, new_m, new_v = {}, {}, {}
    for n in BIG:
        delta[n], new_m[n], new_v[n] = _adamw("adamw_" + n, a[n], grad[n], a['m_' + n], a['v_' + n])
    shapes = [a[n].shape for n in SMALL]
    outs = _adamw("adamw_small", _pack([a[n] for n in SMALL]), _pack([grad[n] for n in SMALL]),
                  _pack([a['m_' + n] for n in SMALL]), _pack([a['v_' + n] for n in SMALL]))
    for res, o in zip((delta, new_m, new_v), outs):
        res.update(zip(SMALL, _unpack(o, shapes)))
    return (loss, grad_x, *[grad[n] for n in WEIGHTS], *[delta[n] for n in WEIGHTS],
            *[new_m[n] for n in WEIGHTS], *[new_v[n] for n in WEIGHTS])


def kernel(x, w_in, b_in, attn_sinks, s5_a_re, s5_a_im, s5_b_re, s5_b_im, s5_c_re, s5_c_im, s5_d, s5_log_dt, s5_glu_w, s5_glu_b, lru_conv_w, lru_conv_b, lru_wx, lru_bx, lru_wa, lru_ba, lru_a_param, mix_norm_g, w_out, b_out, ln1_g, ln1_b, ffn_w_gate, ffn_w_up, ffn_conv_w, ffn_conv_b, ffn_w_down, ln2_g, ln2_b, loss_target, m_w_in, m_b_in, m_attn_sinks, m_s5_a_re, m_s5_a_im, m_s5_b_re, m_s5_b_im, m_s5_c_re, m_s5_c_im, m_s5_d, m_s5_log_dt, m_s5_glu_w, m_s5_glu_b, m_lru_conv_w, m_lru_conv_b, m_lru_wx, m_lru_bx, m_lru_wa, m_lru_ba, m_lru_a_param, m_mix_norm_g, m_w_out, m_b_out, m_ln1_g, m_ln1_b, m_ffn_w_gate, m_ffn_w_up, m_ffn_conv_w, m_ffn_conv_b, m_ffn_w_down, m_ln2_g, m_ln2_b, v_w_in, v_b_in, v_attn_sinks, v_s5_a_re, v_s5_a_im, v_s5_b_re, v_s5_b_im, v_s5_c_re, v_s5_c_im, v_s5_d, v_s5_log_dt, v_s5_glu_w, v_s5_glu_b, v_lru_conv_w, v_lru_conv_b, v_lru_wx, v_lru_bx, v_lru_wa, v_lru_ba, v_lru_a_param, v_mix_norm_g, v_w_out, v_b_out, v_ln1_g, v_ln1_b, v_ffn_w_gate, v_ffn_w_up, v_ffn_conv_w, v_ffn_conv_b, v_ffn_w_down, v_ln2_g, v_ln2_b):
    return _step(dict(locals()))
```

```python
import functools
import math

import jax
import jax.numpy as jnp
from jax import lax
from jax.experimental import pallas as pl
from jax.experimental.pallas import tpu as pltpu

F32 = jnp.float32
BF16 = jnp.bfloat16
MESH = pl.DeviceIdType.MESH
ANY = pl.BlockSpec(memory_space=pl.ANY)

D_MODEL = 1024
DEPTH = 4
HEAD_DIM = 64
N_Q_HEADS = 8
N_KV_HEADS = 2
Q_PER_KV = 4
D_ATTN = 512
D_KV = 128
ATTN_BLOCK = 128
ROPE_THETA = 10000.0
D_S5 = 256
S5_GROUP = 16
S5_GROUPS = 16
S5_STATE = 64
N_STATE = S5_GROUPS * S5_STATE
D_LRU = 256
LRU_HEADS = 4
LRU_HEAD_DIM = 64
LRU_CONV = 4
LRU_C = 8.0
D_IN = 1536
D_FF = 2816
FFN_CONV = 3
N_CHIPS = 4
N_DEV = 8
IN_SH = D_IN // N_CHIPS
FF_SH = D_FF // N_CHIPS
OUT_SH = D_MODEL // N_CHIPS
ALPHA = (2 * DEPTH) ** 0.25
LN_EPS = 1e-5
RMS_EPS = 1e-6
ADAM_LR = 0.001
ADAM_B1 = 0.9
ADAM_B2 = 0.999
ADAM_EPS = 1e-08
ADAM_WD = 0.01
ADAM_STEP = 10

SUBLANES = 8
VMEM_MB = 56


def _params(sem):
    return pltpu.CompilerParams(dimension_semantics=sem, vmem_limit_bytes=VMEM_MB << 20)


def _row_tile(t, pref):
    return min(t, pref)


def _matmul(name, a, b, *, a_blk, a_map, b_blk, b_map, out_shape, o_blk, o_map, grid, dims,
            out_dtype=F32, bias=None, bias_blk=None, bias_map=None, add=None, add_scale=1.0, pair2=None,
            into=None, ln_bwd=None):
    nk = grid[2]
    acc_shape = tuple(d for d in o_blk if d is not None)
    n_in = 2 if pair2 is None else 4

    def kern(*refs):
        p = n_in
        bias_ref = add_ref = None
        if bias is not None:
            bias_ref = refs[p]
            p += 1
        if add is not None:
            add_ref = refs[p]
            p += 1
        if into is not None:
            p += 1
        if ln_bwd is not None:
            ln_in = refs[p:p + 3]
            ln_out = refs[p + 4:p + 8]
            o_ref, acc = refs[p + 3], refs[p + 8]
        else:
            o_ref, acc = refs[p], refs[p + 1]
        k = pl.program_id(2)
        first_tile = pl.program_id(0) == 0

        def product():
            r = _dot(refs[0][...].astype(BF16), refs[1][...].astype(BF16), dims)
            if pair2 is not None:
                r = r + _dot(refs[2][...].astype(BF16), refs[3][...].astype(BF16), dims)
            return r

        def finish(r):
            if bias_ref is not None:
                r = r + bias_ref[...]
            if add_ref is not None:
                r = r + add_scale * add_ref[...]
            if ln_bwd is None:
                o_ref[...] = r.astype(out_dtype)
            else:
                @pl.when(first_tile)
                def _():
                    for ref in ln_out[1:]:
                        ref[...] = jnp.zeros_like(ref)

                _ln_bwd_tile(r, ln_in[0][...], ln_in[1][...], ln_in[2][...], o_ref, *ln_out)

        if nk == 1:
            finish(product())
        else:
            @pl.when(k == 0)
            def _():
                acc[...] = jnp.zeros_like(acc)

            acc[...] += product()

            @pl.when(k == nk - 1)
            def _():
                finish(acc[...])

    in_specs = [pl.BlockSpec(a_blk, a_map), pl.BlockSpec(b_blk, b_map)]
    args = [a, b]
    if pair2 is not None:
        in_specs += [pl.BlockSpec(a_blk, a_map), pl.BlockSpec(b_blk, b_map)]
        args += list(pair2)
    if bias is not None:
        in_specs.append(pl.BlockSpec(bias_blk, bias_map))
        args.append(bias)
    if add is not None:
        in_specs.append(pl.BlockSpec(o_blk, lambda i, j, k: o_map(i, j)))
        args.append(add)
    aliases = {}
    if into is not None:
        aliases = {len(args): 0}
        in_specs.append(ANY)
        args.append(into)
    o_spec = pl.BlockSpec(o_blk, lambda i, j, k: o_map(i, j))
    out_specs, out_shapes = o_spec, jax.ShapeDtypeStruct(out_shape, out_dtype)
    semantics = ("parallel", "parallel", "arbitrary")
    if ln_bwd is not None:
        vec = pl.BlockSpec((1, o_blk[-1]), lambda i, j, k: (0, 0))
        in_specs += [o_spec, pl.BlockSpec((o_blk[0], 1), lambda i, j, k: (i, 0)), vec]
        args += list(ln_bwd)
        vshape = jax.ShapeDtypeStruct((1, o_blk[-1]), F32)
        out_specs = [o_spec, o_spec, vec, vec, vec]
        out_shapes = [out_shapes, jax.ShapeDtypeStruct(out_shape, BF16), vshape, vshape, vshape]
        semantics = ("arbitrary", "arbitrary", "arbitrary")
    return pl.pallas_call(
        kern, name=name, grid=grid, in_specs=in_specs, out_specs=out_specs, out_shape=out_shapes,
        scratch_shapes=[pltpu.VMEM(acc_shape if nk > 1 else (SUBLANES, 128), F32)],
        input_output_aliases=aliases,
        compiler_params=_params(semantics),
    )(*args)


NN = ((1,), (0,))
NT = ((1,), (1,))
TN = ((0,), (0,))
TM = 512


def _sigmoid(x):
    return 0.5 * jnp.tanh(0.5 * x) + 0.5


_GELU_C = math.sqrt(2.0 / math.pi)


def _gelu(x):
    return 0.5 * x * (1.0 + jnp.tanh(_GELU_C * (x + 0.044715 * x * x * x)))


def _gelu_grad(x):
    th = jnp.tanh(_GELU_C * (x + 0.044715 * x * x * x))
    return 0.5 * (1.0 + th) + 0.5 * x * (1.0 - th * th) * _GELU_C * (1.0 + 3 * 0.044715 * x * x)


def _rope_swap(t):
    lane = lax.broadcasted_iota(jnp.int32, t.shape, 1)
    lo = (lane % HEAD_DIM) < (HEAD_DIM // 2)
    return jnp.where(lo, pltpu.roll(t, 128 - HEAD_DIM // 2, 1), pltpu.roll(t, HEAD_DIM // 2, 1))


D_QKV = D_ATTN + 2 * D_KV
TMM = 1024


def _in_proj(xb, w_in, b_in, cos, sin_s, layer):
    t = xb.shape[0]
    tm = _row_tile(t, TMM)

    def kern(x_ref, w_ref, b_ref, c_ref, s_ref, q_ref, u_ref):
        x = x_ref[...]
        c = c_ref[...]
        s = s_ref[...]
        for j in range(N_CHIPS):
            pj = _dot(x, w_ref[j], NN) + b_ref[:, j * IN_SH:(j + 1) * IN_SH]
            for ch in range(IN_SH // 128):
                col = j * IN_SH + ch * 128
                v = pj[:, ch * 128:(ch + 1) * 128]
                if col < D_ATTN + D_KV:
                    v = v * c + _rope_swap(v) * s
                if col < D_ATTN:
                    v = v * (HEAD_DIM ** -0.5)
                if col < D_QKV:
                    q_ref[:, col:col + 128] = v.astype(BF16)
                else:
                    u_ref[:, col - D_QKV:col - D_QKV + 128] = v

    row = lambda w: pl.BlockSpec((tm, w), lambda i: (i, 0))
    return pl.pallas_call(
        kern, name="in_proj", grid=(t // tm,),
        in_specs=[row(D_MODEL), pl.BlockSpec((None, N_CHIPS, D_MODEL, IN_SH), lambda i: (layer, 0, 0, 0)),
                  pl.BlockSpec((1, D_IN), lambda i: (0, 0)), row(128), row(128)],
        out_specs=[row(D_QKV), row(D_IN - D_QKV)],
        out_shape=[jax.ShapeDtypeStruct((t, D_QKV), BF16), jax.ShapeDtypeStruct((t, D_IN - D_QKV), F32)],
        compiler_params=_params(("parallel",)),
    )(xb, w_in, b_in, cos, sin_s)


def _attn_mask(i):
    qi = lax.broadcasted_iota(jnp.int32, (ATTN_BLOCK, 2 * ATTN_BLOCK), 0)
    si = lax.broadcasted_iota(jnp.int32, (ATTN_BLOCK, 2 * ATTN_BLOCK), 1)
    diff = qi + ATTN_BLOCK - si
    return (diff >= 0) & (diff < ATTN_BLOCK) & ((si >= ATTN_BLOCK) | (i > 0))


def _attn_fwd(qkv, sinks):
    t = qkv.shape[0]
    nb = t // ATTN_BLOCK

    def kern(q_ref, kp_ref, kc_ref, vp_ref, vc_ref, s_ref, o_ref, l_ref):
        i = pl.program_id(0)
        valid = _attn_mask(i)
        kband = jnp.concatenate([kp_ref[...], kc_ref[...]], axis=0)
        vband = jnp.concatenate([vp_ref[...], vc_ref[...]], axis=0)
        ks = [kband[:, kh * HEAD_DIM:(kh + 1) * HEAD_DIM] for kh in range(N_KV_HEADS)]
        vs = [vband[:, kh * HEAD_DIM:(kh + 1) * HEAD_DIM] for kh in range(N_KV_HEADS)]
        scores = [_dot(q_ref[:, h * HEAD_DIM:(h + 1) * HEAD_DIM], ks[h // Q_PER_KV], NT) for h in range(N_Q_HEADS)]
        probs, lses = [], []
        for h in range(N_Q_HEADS):
            s = jnp.where(valid, scores[h], -jnp.inf)
            sink = s_ref[0:1, h:h + 1]
            m = jnp.maximum(jnp.max(s, axis=-1, keepdims=True), sink)
            e = jnp.exp(s - m)
            denom = jnp.sum(e, axis=-1, keepdims=True) + jnp.exp(sink - m)
            probs.append((e / denom).astype(BF16))
            lses.append(m + jnp.log(denom))
        outs = [_dot(probs[h], vs[h // Q_PER_KV], NN) for h in range(N_Q_HEADS)]
        for h in range(N_Q_HEADS):
            o_ref[:, h * HEAD_DIM:(h + 1) * HEAD_DIM] = outs[h]
            l_ref[:, h:h + 1] = lses[h]

    blk = lambda w, f: pl.BlockSpec((ATTN_BLOCK, w), f)
    return pl.pallas_call(
        kern, name="attn_fwd", grid=(nb,),
        in_specs=[blk(512, lambda i: (i, 0)),
                  blk(128, lambda i: (jnp.maximum(i - 1, 0), 4)), blk(128, lambda i: (i, 4)),
                  blk(128, lambda i: (jnp.maximum(i - 1, 0), 5)), blk(128, lambda i: (i, 5)),
                  pl.BlockSpec((1, N_Q_HEADS), lambda i: (0, 0))],
        out_specs=[blk(512, lambda i: (i, 0)), blk(N_Q_HEADS, lambda i: (i, 0))],
        out_shape=[jax.ShapeDtypeStruct((t, D_ATTN), F32), jax.ShapeDtypeStruct((t, N_Q_HEADS), F32)],
        compiler_params=_params(("parallel",)),
    )(qkv, qkv, qkv, qkv, qkv, sinks)


def _attn_bwd(qkv, o, do, lse, sinks):
    t = qkv.shape[0]
    nb = t // ATTN_BLOCK

    def kern(q_ref, kp_ref, kc_ref, vp_ref, vc_ref, o_ref, do_ref, l_ref, s_ref,
             dq_ref, dk_ref, dv_ref, ds_ref, ck, cv):
        i = pl.program_id(0)

        @pl.when(i == 0)
        def _():
            ds_ref[...] = jnp.zeros_like(ds_ref)
            ck[...] = jnp.zeros_like(ck)
            cv[...] = jnp.zeros_like(cv)

        @pl.when(i < nb)
        def _():
            valid = _attn_mask(i)
            kband = jnp.concatenate([kp_ref[...], kc_ref[...]], axis=0)
            vband = jnp.concatenate([vp_ref[...], vc_ref[...]], axis=0)
            heads = range(N_Q_HEADS)
            sl = [slice(h * HEAD_DIM, (h + 1) * HEAD_DIM) for h in heads]
            ks = [kband[:, kh * HEAD_DIM:(kh + 1) * HEAD_DIM] for kh in range(N_KV_HEADS)]
            vs = [vband[:, kh * HEAD_DIM:(kh + 1) * HEAD_DIM] for kh in range(N_KV_HEADS)]
            qs = [q_ref[:, sl[h]] for h in heads]
            d_os = [do_ref[:, sl[h]] for h in heads]
            dobs = [d.astype(BF16) for d in d_os]
            scores = [_dot(qs[h], ks[h // Q_PER_KV], NT) for h in heads]
            dps = [_dot(dobs[h], vs[h // Q_PER_KV], NT) for h in heads]
            pbs, dscs = [], []
            for h in heads:
                lse_h = l_ref[:, h:h + 1]
                p = jnp.where(valid, jnp.exp(scores[h] - lse_h), 0.0)
                delta = jnp.sum(d_os[h] * o_ref[:, sl[h]], axis=-1, keepdims=True)
                pbs.append(p.astype(BF16))
                dscs.append((p * (dps[h] - delta)).astype(BF16))
                psink = jnp.exp(s_ref[0:1, h:h + 1] - lse_h)
                ds_ref[0:1, h:h + 1] += -jnp.sum(psink * delta, axis=0, keepdims=True)
            dqs = [_dot(dscs[h], ks[h // Q_PER_KV], NN) for h in heads]
            dkb = [sum(_dot(dscs[h], qs[h], TN) for h in heads if h // Q_PER_KV == kh) for kh in range(N_KV_HEADS)]
            dvb = [sum(_dot(pbs[h], dobs[h], TN) for h in heads if h // Q_PER_KV == kh) for kh in range(N_KV_HEADS)]
            for h in heads:
                dq_ref[:, sl[h]] = dqs[h]
            dk_band = jnp.concatenate(dkb, axis=1)
            dv_band = jnp.concatenate(dvb, axis=1)
            dk_ref[...] = ck[...] + dk_band[:ATTN_BLOCK]
            dv_ref[...] = cv[...] + dv_band[:ATTN_BLOCK]
            ck[...] = dk_band[ATTN_BLOCK:]
            cv[...] = dv_band[ATTN_BLOCK:]

        @pl.when(i == nb)
        def _():
            dk_ref[...] = ck[...]
            dv_ref[...] = cv[...]

    blk = lambda w, f: pl.BlockSpec((ATTN_BLOCK, w), f)
    cur = lambda i: jnp.minimum(i, nb - 1)
    prev = lambda i: jnp.clip(i - 1, 0, nb - 1)
    return pl.pallas_call(
        kern, name="attn_bwd", grid=(nb + 1,),
        in_specs=[blk(512, lambda i: (cur(i), 0)),
                  blk(128, lambda i: (prev(i), 4)), blk(128, lambda i: (cur(i), 4)),
                  blk(128, lambda i: (prev(i), 5)), blk(128, lambda i: (cur(i), 5)),
                  blk(512, lambda i: (cur(i), 0)), blk(512, lambda i: (cur(i), 0)),
                  blk(N_Q_HEADS, lambda i: (cur(i), 0)),
                  pl.BlockSpec((1, N_Q_HEADS), lambda i: (0, 0))],
        out_specs=[blk(512, lambda i: (cur(i), 0)), blk(128, lambda i: (prev(i), 0)),
                   blk(128, lambda i: (prev(i), 0)), pl.BlockSpec((1, N_Q_HEADS), lambda i: (0, 0))],
        out_shape=[jax.ShapeDtypeStruct((t, D_ATTN), F32), jax.ShapeDtypeStruct((t, D_KV), F32),
                   jax.ShapeDtypeStruct((t, D_KV), F32), jax.ShapeDtypeStruct((1, N_Q_HEADS), F32)],
        scratch_shapes=[pltpu.VMEM((ATTN_BLOCK, D_KV), F32), pltpu.VMEM((ATTN_BLOCK, D_KV), F32)],
        compiler_params=_params(("arbitrary",)),
    )(qkv, qkv, qkv, qkv, qkv, o, do, lse, sinks)


_GROUPS = ((0, D_ATTN), (D_ATTN, D_ATTN + D_S5), (D_ATTN + D_S5, D_MODEL))


def _rms_fwd(ya, ys, yl, g):
    t = ya.shape[0]
    tm = _row_tile(t, TM)

    def kern(a_ref, s_ref, l_ref, g_ref, o_ref):
        for (lo, hi), ref in zip(_GROUPS, (a_ref, s_ref, l_ref)):
            y = ref[...]
            n = y * lax.rsqrt(jnp.mean(y * y, axis=-1, keepdims=True) + RMS_EPS)
            o_ref[:, lo:hi] = (n * g_ref[:, lo:hi]).astype(BF16)

    row = lambda w: pl.BlockSpec((tm, w), lambda i: (i, 0))
    return pl.pallas_call(
        kern, name="rms_fwd", grid=(t // tm,),
        in_specs=[row(D_ATTN), row(D_S5), row(D_LRU), pl.BlockSpec((1, D_MODEL), lambda i: (0, 0))],
        out_specs=row(D_MODEL), out_shape=jax.ShapeDtypeStruct((t, D_MODEL), BF16),
        compiler_params=_params(("parallel",)),
    )(ya, ys, yl, g)


def _rms_bwd(dmix, ya, ys, yl, g):
    t = ya.shape[0]
    tm = _row_tile(t, TM)

    def kern(d_ref, a_ref, s_ref, l_ref, g_ref, da_ref, ds_ref, dl_ref, dg_ref):
        @pl.when(pl.program_id(0) == 0)
        def _():
            dg_ref[...] = jnp.zeros_like(dg_ref)

        for (lo, hi), ref, out in zip(_GROUPS, (a_ref, s_ref, l_ref), (da_ref, ds_ref, dl_ref)):
            y = ref[...]
            rstd = lax.rsqrt(jnp.mean(y * y, axis=-1, keepdims=True) + RMS_EPS)
            n = y * rstd
            dm = d_ref[:, lo:hi]
            dg_ref[:, lo:hi] += jnp.sum(dm * n, axis=0, keepdims=True)
            dn = dm * g_ref[:, lo:hi]
            out[...] = rstd * (dn - n * jnp.mean(dn * n, axis=-1, keepdims=True))

    row = lambda w: pl.BlockSpec((tm, w), lambda i: (i, 0))
    vec = pl.BlockSpec((1, D_MODEL), lambda i: (0, 0))
    return pl.pallas_call(
        kern, name="rms_bwd", grid=(t // tm,),
        in_specs=[row(D_MODEL), row(D_ATTN), row(D_S5), row(D_LRU), vec],
        out_specs=[row(D_ATTN), row(D_S5), row(D_LRU), vec],
        out_shape=[jax.ShapeDtypeStruct((t, D_ATTN), F32), jax.ShapeDtypeStruct((t, D_S5), F32),
                   jax.ShapeDtypeStruct((t, D_LRU), F32), jax.ShapeDtypeStruct((1, D_MODEL), F32)],
        compiler_params=_params(("arbitrary",)),
    )(dmix, ya, ys, yl, g)


def _mix_out_ln(ya, ys, yl, mg, w_out, b_out, xres, g, b, layer):
    t = xres.shape[0]
    tm = _row_tile(t, TM)

    def kern(a_ref, s_ref, l_ref, mg_ref, w_ref, bias_ref, x_ref, g_ref, b_ref, m_ref, y_ref, yb_ref, h_ref, r_ref):
        for (lo, hi), ref in zip(_GROUPS, (a_ref, s_ref, l_ref)):
            v = ref[...]
            n = v * lax.rsqrt(jnp.mean(v * v, axis=-1, keepdims=True) + RMS_EPS)
            m_ref[:, lo:hi] = (n * mg_ref[:, lo:hi]).astype(BF16)
        r = ALPHA * x_ref[...] + _dot(m_ref[...], w_ref[...], NN) + bias_ref[...]
        mu = jnp.mean(r, axis=-1, keepdims=True)
        xc = r - mu
        rstd = lax.rsqrt(jnp.mean(xc * xc, axis=-1, keepdims=True) + LN_EPS)
        xhat = xc * rstd
        h_ref[...] = xhat
        r_ref[...] = rstd
        y = xhat * g_ref[...] + b_ref[...]
        y_ref[...] = y
        yb_ref[...] = y.astype(BF16)

    rowb = lambda w: pl.BlockSpec((tm, w), lambda i: (i, 0))
    row = rowb(D_MODEL)
    vec = pl.BlockSpec((1, D_MODEL), lambda i: (0, 0))
    big = lambda dt: jax.ShapeDtypeStruct((t, D_MODEL), dt)
    return pl.pallas_call(
        kern, name="mix_out_ln", grid=(t // tm,),
        in_specs=[rowb(D_ATTN), rowb(D_S5), rowb(D_LRU), vec,
                  pl.BlockSpec((None, D_MODEL, D_MODEL), lambda i: (layer, 0, 0)), vec, row, vec, vec],
        out_specs=[row, row, row, row, pl.BlockSpec((tm, 1), lambda i: (i, 0))],
        out_shape=[big(BF16), big(F32), big(BF16), big(F32), jax.ShapeDtypeStruct((t, 1), F32)],
        compiler_params=_params(("parallel",)),
    )(ya, ys, yl, mg, w_out, b_out, xres, g, b)


def _d_mix_rms(drb, w_out, ya, ys, yl, mg, layer):
    t = drb.shape[0]
    tm = _row_tile(t, TM)

    def kern(d_ref, w_ref, a_ref, s_ref, l_ref, g_ref, da_ref, ds_ref, dl_ref, dg_ref):
        @pl.when(pl.program_id(0) == 0)
        def _():
            dg_ref[...] = jnp.zeros_like(dg_ref)

        dmix = _dot(d_ref[...], w_ref[...], NT)
        for (lo, hi), ref, out in zip(_GROUPS, (a_ref, s_ref, l_ref), (da_ref, ds_ref, dl_ref)):
            v = ref[...]
            rstd = lax.rsqrt(jnp.mean(v * v, axis=-1, keepdims=True) + RMS_EPS)
            n = v * rstd
            dm = dmix[:, lo:hi]
            dg_ref[:, lo:hi] += jnp.sum(dm * n, axis=0, keepdims=True)
            dn = dm * g_ref[:, lo:hi]
            out[...] = rstd * (dn - n * jnp.mean(dn * n, axis=-1, keepdims=True))

    rowb = lambda w: pl.BlockSpec((tm, w), lambda i: (i, 0))
    vec = pl.BlockSpec((1, D_MODEL), lambda i: (0, 0))
    return pl.pallas_call(
        kern, name="d_mix_rms", grid=(t // tm,),
        in_specs=[rowb(D_MODEL), pl.BlockSpec((None, D_MODEL, D_MODEL), lambda i: (layer, 0, 0)),
                  rowb(D_ATTN), rowb(D_S5), rowb(D_LRU), vec],
        out_specs=[rowb(D_ATTN), rowb(D_S5), rowb(D_LRU), vec],
        out_shape=[jax.ShapeDtypeStruct((t, D_ATTN), F32), jax.ShapeDtypeStruct((t, D_S5), F32),
                   jax.ShapeDtypeStruct((t, D_LRU), F32), jax.ShapeDtypeStruct((1, D_MODEL), F32)],
        compiler_params=_params(("arbitrary",)),
    )(drb, w_out, ya, ys, yl, mg)


def _matmul_ln(name, a, w, bias, xres, g, b, a_blk, a_map, w_blk, parts, layer):
    t = xres.shape[0]
    tm = a_blk[-2]

    def kern(a_ref, w_ref, bias_ref, x_ref, g_ref, b_ref, y_ref, yb_ref, h_ref, r_ref):
        if parts is None:
            f = _dot(a_ref[...], w_ref[...], NN)
        else:
            f = sum(_dot(a_ref[j], w_ref[j], NN) for j in range(parts))
        r = ALPHA * x_ref[...] + f + bias_ref[...]
        mu = jnp.mean(r, axis=-1, keepdims=True)
        xc = r - mu
        rstd = lax.rsqrt(jnp.mean(xc * xc, axis=-1, keepdims=True) + LN_EPS)
        xhat = xc * rstd
        h_ref[...] = xhat
        r_ref[...] = rstd
        y = xhat * g_ref[...] + b_ref[...]
        y_ref[...] = y
        yb_ref[...] = y.astype(BF16)

    row = pl.BlockSpec((tm, D_MODEL), lambda i: (i, 0))
    vec = pl.BlockSpec((1, D_MODEL), lambda i: (0, 0))
    big = lambda dt: jax.ShapeDtypeStruct((t, D_MODEL), dt)
    return pl.pallas_call(
        kern, name=name, grid=(t // tm,),
        in_specs=[pl.BlockSpec(a_blk, a_map), pl.BlockSpec((None,) + w_blk, lambda i: (layer,) + (0,) * len(w_blk)), vec, row, vec, vec],
        out_specs=[row, row, row, pl.BlockSpec((tm, 1), lambda i: (i, 0))],
        out_shape=[big(F32), big(BF16), big(F32), jax.ShapeDtypeStruct((t, 1), F32)],
        compiler_params=_params(("parallel",)),
    )(a, w, bias, xres, g, b)


def _ln_bwd(dy, xhat, rstd, g):
    t = dy.shape[0]
    tm = _row_tile(t, TM)

    def kern(d_ref, h_ref, r_ref, g_ref, dr_ref, drb_ref, dg_ref, db_ref, sr_ref):
        @pl.when(pl.program_id(0) == 0)
        def _():
            dg_ref[...] = jnp.zeros_like(dg_ref)
            db_ref[...] = jnp.zeros_like(db_ref)
            sr_ref[...] = jnp.zeros_like(sr_ref)

        d = d_ref[...]
        xhat = h_ref[...]
        dg_ref[...] += jnp.sum(d * xhat, axis=0, keepdims=True)
        db_ref[...] += jnp.sum(d, axis=0, keepdims=True)
        dh = d * g_ref[...]
        dr = r_ref[...] * (dh - jnp.mean(dh, axis=-1, keepdims=True)
                           - xhat * jnp.mean(dh * xhat, axis=-1, keepdims=True))
        dr_ref[...] = dr
        drb_ref[...] = dr.astype(BF16)
        sr_ref[...] += jnp.sum(dr, axis=0, keepdims=True)

    row = pl.BlockSpec((tm, D_MODEL), lambda i: (i, 0))
    vec = pl.BlockSpec((1, D_MODEL), lambda i: (0, 0))
    vshape = jax.ShapeDtypeStruct((1, D_MODEL), F32)
    return pl.pallas_call(
        kern, name="ln_bwd", grid=(t // tm,),
        in_specs=[row, row, pl.BlockSpec((tm, 1), lambda i: (i, 0)), vec],
        out_specs=[row, row, vec, vec, vec],
        out_shape=[jax.ShapeDtypeStruct((t, D_MODEL), F32), jax.ShapeDtypeStruct((t, D_MODEL), BF16),
                   vshape, vshape, vshape],
        compiler_params=_params(("arbitrary",)),
    )(dy, xhat, rstd, g)


def _ln_bwd_tile(d, xhat, rstd, g, dr_ref, drb_ref, dg_ref, db_ref, sr_ref):
    dg_ref[...] += jnp.sum(d * xhat, axis=0, keepdims=True)
    db_ref[...] += jnp.sum(d, axis=0, keepdims=True)
    dh = d * g
    dr = rstd * (dh - jnp.mean(dh, axis=-1, keepdims=True) - xhat * jnp.mean(dh * xhat, axis=-1, keepdims=True))
    dr_ref[...] = dr
    drb_ref[...] = dr.astype(BF16)
    sr_ref[...] += jnp.sum(dr, axis=0, keepdims=True)


def _loss_head(y, target, xhat, rstd, g):
    t = y.shape[0]
    tm = _row_tile(t, TM)

    def kern(y_ref, t_ref, h_ref, r_ref, g_ref, l_ref, dr_ref, drb_ref, dg_ref, db_ref, sr_ref):
        @pl.when(pl.program_id(0) == 0)
        def _():
            for ref in (l_ref, dg_ref, db_ref, sr_ref):
                ref[...] = jnp.zeros_like(ref)

        err = y_ref[...] - t_ref[...]
        part = jnp.sum(jnp.sum(err * err, axis=-1, keepdims=True), axis=0, keepdims=True)
        l_ref[...] += jnp.broadcast_to(part * (0.5 / D_MODEL), l_ref.shape)
        _ln_bwd_tile(err * (1.0 / D_MODEL), h_ref[...], r_ref[...], g_ref[...], dr_ref, drb_ref, dg_ref, db_ref, sr_ref)

    row = pl.BlockSpec((tm, D_MODEL), lambda i: (i, 0))
    vec = pl.BlockSpec((1, D_MODEL), lambda i: (0, 0))
    vshape = jax.ShapeDtypeStruct((1, D_MODEL), F32)
    return pl.pallas_call(
        kern, name="loss_head", grid=(t // tm,),
        in_specs=[row, row, row, pl.BlockSpec((tm, 1), lambda i: (i, 0)), vec],
        out_specs=[pl.BlockSpec((1, 128), lambda i: (0, 0)), row, row, vec, vec, vec],
        out_shape=[jax.ShapeDtypeStruct((1, 128), F32), jax.ShapeDtypeStruct((t, D_MODEL), F32),
                   jax.ShapeDtypeStruct((t, D_MODEL), BF16), vshape, vshape, vshape],
        compiler_params=_params(("arbitrary",)),
    )(y, target, xhat, rstd, g)


HALO = 8


def _ffn_mid_specs(t, tm):
    main = pl.BlockSpec((None, tm, FF_SH), lambda j, i: (j, i, 0))
    prev = pl.BlockSpec((None, HALO, FF_SH), lambda j, i: (j, jnp.maximum(i * (tm // HALO) - 1, 0), 0))
    cw = pl.BlockSpec((None, FFN_CONV, FF_SH), lambda j, i: (j, 0, 0))
    cb = pl.BlockSpec((None, 1, FF_SH), lambda j, i: (j, 0, 0))
    return main, prev, cw, cb


def _ffn_conv(ext, g_ref, p_ref, w_ref, b_ref, tm):
    i = pl.program_id(1)
    ext[0:HALO, :] = jnp.where(i > 0, p_ref[...], 0.0)
    ext[HALO:, :] = g_ref[...]
    taps = [ext[pl.ds(HALO - (FFN_CONV - 1) + k, tm), :] for k in range(FFN_CONV)]
    gc = b_ref[...] + sum(w_ref[k:k + 1, :] * taps[k] for k in range(FFN_CONV))
    return gc, taps


def _ffn_mid_fwd(gpre, up, cw, cb):
    t = gpre.shape[1]
    tm = _row_tile(t, TM)

    def kern(g_ref, p_ref, u_ref, w_ref, b_ref, o_ref, ext):
        gc, _ = _ffn_conv(ext, g_ref, p_ref, w_ref, b_ref, tm)
        o_ref[...] = (gc * _sigmoid(gc) * u_ref[...]).astype(BF16)

    main, prev, cws, cbs = _ffn_mid_specs(t, tm)
    return pl.pallas_call(
        kern, name="ffn_mid_fwd", grid=(N_CHIPS, t // tm),
        in_specs=[main, prev, main, cws, cbs], out_specs=main,
        out_shape=jax.ShapeDtypeStruct((N_CHIPS, t, FF_SH), BF16),
        scratch_shapes=[pltpu.VMEM((tm + HALO, FF_SH), F32)],
        compiler_params=_params(("parallel", "parallel")),
    )(gpre, gpre, up, cw, cb)


def _ffn_mid_bwd(gpre, up, dhmid, cw, cb):
    t = gpre.shape[1]
    tm = _row_tile(t, TM)

    def kern(g_ref, p_ref, u_ref, d_ref, w_ref, b_ref, h_ref, du_ref, dg_ref, dw_ref, db_ref, ext):
        @pl.when(pl.program_id(1) == 0)
        def _():
            dw_ref[...] = jnp.zeros_like(dw_ref)
            db_ref[...] = jnp.zeros_like(db_ref)

        gc, taps = _ffn_conv(ext, g_ref, p_ref, w_ref, b_ref, tm)
        sg = _sigmoid(gc)
        s = gc * sg
        u = u_ref[...]
        d = d_ref[...]
        h_ref[...] = (s * u).astype(BF16)
        du_ref[...] = (d * s).astype(BF16)
        dgc = d * u * (sg * (1.0 + gc * (1.0 - sg)))
        dg_ref[...] = dgc
        db_ref[...] += jnp.sum(dgc, axis=0, keepdims=True)
        for k in range(FFN_CONV):
            dw_ref[k:k + 1, :] += jnp.sum(dgc * taps[k], axis=0, keepdims=True)

    main, prev, cws, cbs = _ffn_mid_specs(t, tm)
    big = lambda dt: jax.ShapeDtypeStruct((N_CHIPS, t, FF_SH), dt)
    return pl.pallas_call(
        kern, name="ffn_mid_bwd", grid=(N_CHIPS, t // tm),
        in_specs=[main, prev, main, main, cws, cbs], out_specs=[main, main, main, cws, cbs],
        out_shape=[big(BF16), big(BF16), big(F32), jax.ShapeDtypeStruct((N_CHIPS, FFN_CONV, FF_SH), F32),
                   jax.ShapeDtypeStruct((N_CHIPS, 1, FF_SH), F32)],
        scratch_shapes=[pltpu.VMEM((tm + HALO, FF_SH), F32)],
        compiler_params=_params(("parallel", "arbitrary")),
    )(gpre, gpre, up, dhmid, cw, cb)


def _ffn_conv_t(dgc, cw):
    t = dgc.shape[1]
    tm = _row_tile(t, TM)
    nt = t // tm

    def kern(d_ref, n_ref, w_ref, o_ref, ext):
        i = pl.program_id(1)
        ext[0:tm, :] = d_ref[...]
        ext[tm:, :] = jnp.where(i < nt - 1, n_ref[...], 0.0)
        acc = sum(w_ref[k:k + 1, :] * ext[pl.ds(FFN_CONV - 1 - k, tm), :] for k in range(FFN_CONV))
        o_ref[...] = acc.astype(BF16)

    main, _, cws, _ = _ffn_mid_specs(t, tm)
    nxt = pl.BlockSpec((None, HALO, FF_SH),
                       lambda j, i: (j, jnp.minimum((i + 1) * (tm // HALO), t // HALO - 1), 0))
    return pl.pallas_call(
        kern, name="ffn_conv_t", grid=(N_CHIPS, nt),
        in_specs=[main, nxt, cws], out_specs=main,
        out_shape=jax.ShapeDtypeStruct((N_CHIPS, t, FF_SH), BF16),
        scratch_shapes=[pltpu.VMEM((tm + HALO, FF_SH), F32)],
        compiler_params=_params(("parallel", "parallel")),
    )(dgc, dgc, cw)


def _ffn_hidden_fwd(xb, wg, wu, cw, cb, layer):
    t = xb.shape[0]
    tm = _row_tile(t, TM)

    def kern(x_ref, wg_ref, wu_ref, cw_ref, cb_ref, g_ref, c_ref, u_ref, h_ref, ext):
        @pl.when(pl.program_id(1) == 0)
        def _():
            ext[0:HALO, :] = jnp.zeros((HALO, FF_SH), F32)

        x = x_ref[...]
        gb = _dot(x, wg_ref[...], NN).astype(BF16)
        ub = _dot(x, wu_ref[...], NN).astype(BF16)
        g_ref[...] = gb
        u_ref[...] = ub
        g = gb.astype(F32)
        w = [cw_ref[k:k + 1, :] for k in range(FFN_CONV)]
        body = cb_ref[...] + w[2] * g + w[1] * pltpu.roll(g, 1, 0) + w[0] * pltpu.roll(g, 2, 0)
        ext[HALO:, :] = g[0:HALO, :]
        head = cb_ref[...] + sum(w[k] * ext[pl.ds(HALO - (FFN_CONV - 1) + k, HALO), :] for k in range(FFN_CONV))
        gcb = jnp.concatenate([head, body[HALO:, :]], axis=0).astype(BF16)
        c_ref[...] = gcb
        gc = gcb.astype(F32)
        h_ref[...] = (gc * _sigmoid(gc) * ub.astype(F32)).astype(BF16)
        ext[0:HALO, :] = g[tm - HALO:, :]

    col = pl.BlockSpec((None, tm, FF_SH), lambda j, i: (j, i, 0))
    wspec = pl.BlockSpec((None, None, D_MODEL, FF_SH), lambda j, i: (layer, j, 0, 0))
    big = jax.ShapeDtypeStruct((N_CHIPS, t, FF_SH), BF16)
    return pl.pallas_call(
        kern, name="ffn_hidden_fwd", grid=(N_CHIPS, t // tm),
        in_specs=[pl.BlockSpec((tm, D_MODEL), lambda j, i: (i, 0)), wspec, wspec,
                  pl.BlockSpec((None, FFN_CONV, FF_SH), lambda j, i: (j, 0, 0)),
                  pl.BlockSpec((None, 1, FF_SH), lambda j, i: (j, 0, 0))],
        out_specs=[col, col, col, col], out_shape=[big, big, big, big],
        scratch_shapes=[pltpu.VMEM((2 * HALO, FF_SH), F32)],
        compiler_params=_params(("parallel", "arbitrary")),
    )(xb, wg, wu, cw, cb)


def _ffn_hidden_bwd(drb, gpre, gconv, up, wd, cw, layer):
    t = drb.shape[0]
    tm = _row_tile(t, TM)
    nt = t // tm
    rb = lambda i: nt - 1 - i

    def kern(d_ref, g_ref, c_ref, u_ref, wd_ref, cw_ref, du_ref, dg_ref, dw_ref, db_ref, ext):
        @pl.when(pl.program_id(1) == 0)
        def _():
            dw_ref[...] = jnp.zeros_like(dw_ref)
            db_ref[...] = jnp.zeros_like(db_ref)
            ext[HALO:, :] = jnp.zeros((HALO, FF_SH), F32)

        dh = _dot(d_ref[...], wd_ref[...], NT)
        gc = c_ref[...].astype(F32)
        sg = _sigmoid(gc)
        du_ref[...] = (dh * (gc * sg)).astype(BF16)
        dgc = dh * u_ref[...].astype(F32) * (sg * (1.0 + gc * (1.0 - sg)))
        db_ref[...] += jnp.sum(dgc, axis=0, keepdims=True)
        g = g_ref[...].astype(F32)
        w = [cw_ref[k:k + 1, :] for k in range(FFN_CONV)]
        taps = [pltpu.roll(dgc, tm - 2, 0), pltpu.roll(dgc, tm - 1, 0), dgc]
        body = sum(w[k] * taps[k] for k in range(FFN_CONV))
        last = slice(tm - HALO, tm)
        ext[0:HALO, :] = dgc[last, :]
        tail_taps = [ext[pl.ds(FFN_CONV - 1 - k, HALO), :] for k in range(FFN_CONV)]
        tail = sum(w[k] * tail_taps[k] for k in range(FFN_CONV))
        dg_ref[...] = jnp.concatenate([body[0:tm - HALO, :], tail], axis=0).astype(BF16)
        for k in range(FFN_CONV):
            dw_ref[k:k + 1, :] += (jnp.sum(g * taps[k], axis=0, keepdims=True)
                                   + jnp.sum(g[last, :] * (tail_taps[k] - taps[k][last, :]), axis=0, keepdims=True))
        ext[HALO:, :] = dgc[0:HALO, :]

    col = pl.BlockSpec((None, tm, FF_SH), lambda j, i: (j, rb(i), 0))
    cws = pl.BlockSpec((None, FFN_CONV, FF_SH), lambda j, i: (j, 0, 0))
    cbs = pl.BlockSpec((None, 1, FF_SH), lambda j, i: (j, 0, 0))
    big = jax.ShapeDtypeStruct((N_CHIPS, t, FF_SH), BF16)
    return pl.pallas_call(
        kern, name="ffn_hidden_bwd", grid=(N_CHIPS, nt),
        in_specs=[pl.BlockSpec((tm, D_MODEL), lambda j, i: (rb(i), 0)), col, col, col,
                  pl.BlockSpec((None, None, FF_SH, D_MODEL), lambda j, i: (layer, j, 0, 0)), cws],
        out_specs=[col, col, cws, cbs],
        out_shape=[big, big, jax.ShapeDtypeStruct((N_CHIPS, FFN_CONV, FF_SH), F32),
                   jax.ShapeDtypeStruct((N_CHIPS, 1, FF_SH), F32)],
        scratch_shapes=[pltpu.VMEM((2 * HALO, FF_SH), F32)],
        compiler_params=_params(("parallel", "arbitrary")),
    )(drb, gpre, gconv, up, wd, cw)


def _s5_coefs(ar, ai, reverse):
    if reverse:
        ai = -ai
    pw = [(ar, ai)]
    for _ in range(SUBLANES - 1):
        pr, pi = pw[-1]
        pw.append((pr * ar - pi * ai, pr * ai + pi * ar))
    rows = jnp.arange(SUBLANES)[:, None]
    out = []
    for s in (1, 2, 4):
        keep = (rows + s <= SUBLANES - 1) if reverse else (rows >= s)
        out += [jnp.where(keep, pw[s - 1][0][None], 0.0), jnp.where(keep, pw[s - 1][1][None], 0.0)]
    order = list(range(SUBLANES - 1, -1, -1)) if reverse else list(range(SUBLANES))
    out += [jnp.stack([pw[k][0] for k in order]), jnp.stack([pw[k][1] for k in order])]
    return jnp.stack(out).astype(F32)


def _s5_scan(buf, coef_ref, carry, tm, reverse):
    n8 = tm // SUBLANES

    def body(it, c):
        cre, cim = c
        blk = (n8 - 1 - it) if reverse else it
        r0 = pl.multiple_of(blk * SUBLANES, SUBLANES)
        xre = buf[pl.ds(r0, SUBLANES), 0:N_STATE]
        xim = buf[pl.ds(r0, SUBLANES), N_STATE:]
        for idx, s in enumerate((1, 2, 4)):
            sh = (SUBLANES - s) if reverse else s
            sre = pltpu.roll(xre, sh, 0)
            sim = pltpu.roll(xim, sh, 0)
            are = coef_ref[2 * idx]
            aim = coef_ref[2 * idx + 1]
            xre, xim = xre + are * sre - aim * sim, xim + are * sim + aim * sre
        pre = coef_ref[6]
        pim = coef_ref[7]
        hre = xre + pre * cre - pim * cim
        him = xim + pre * cim + pim * cre
        buf[pl.ds(r0, SUBLANES), 0:N_STATE] = hre
        buf[pl.ds(r0, SUBLANES), N_STATE:] = him
        row = 0 if reverse else SUBLANES - 1
        return (jnp.broadcast_to(hre[row:row + 1], (SUBLANES, N_STATE)),
                jnp.broadcast_to(him[row:row + 1], (SUBLANES, N_STATE)))

    cre, cim = lax.fori_loop(0, n8, body, (carry[:, 0:N_STATE], carry[:, N_STATE:]))
    carry[:, 0:N_STATE] = cre
    carry[:, N_STATE:] = cim


def _real_scan(abuf, bbuf, carry, tm, reverse):
    n8 = tm // SUBLANES
    width = bbuf.shape[1]

    def body(it, c):
        blk = (n8 - 1 - it) if reverse else it
        r0 = pl.multiple_of(blk * SUBLANES, SUBLANES)
        a = abuf[pl.ds(r0, SUBLANES), :]
        b = bbuf[pl.ds(r0, SUBLANES), :]
        rows = lax.broadcasted_iota(jnp.int32, (SUBLANES, width), 0)
        for s in (1, 2, 4):
            sh = (SUBLANES - s) if reverse else s
            keep = (rows + s <= SUBLANES - 1) if reverse else (rows >= s)
            sa = pltpu.roll(a, sh, 0)
            sb = pltpu.roll(b, sh, 0)
            b = b + a * jnp.where(keep, sb, 0.0)
            a = a * jnp.where(keep, sa, 1.0)
        h = b + a * c
        bbuf[pl.ds(r0, SUBLANES), :] = h
        row = 0 if reverse else SUBLANES - 1
        return jnp.broadcast_to(h[row:row + 1], (SUBLANES, width))

    carry[...] = lax.fori_loop(0, n8, body, carry[...])


def _dot(a, b, dims):
    return lax.dot_general(a, b, (dims, ((), ())), preferred_element_type=F32)


TS5 = 256
HALO16 = 16


def _s5_fwd(proj, bmat, coef, cmat, dvec, gw, gb):
    t = proj.shape[0]
    tm = _row_tile(t, TS5)

    def kern(u_ref, b_ref, coef_ref, c_ref, d_ref, gw_ref, gb_ref, h_ref, y_ref, hbuf, carry):
        @pl.when(pl.program_id(0) == 0)
        def _():
            carry[...] = jnp.zeros_like(carry)

        u = u_ref[...]
        hbuf[...] = _dot(u.astype(BF16), b_ref[...], NN)
        _s5_scan(hbuf, coef_ref, carry, tm, False)
        hb = hbuf[...].astype(BF16)
        h_ref[...] = hb
        y = _dot(hb, c_ref[...], NN) + d_ref[...] * u
        ys = _gelu(y)
        z = _dot(ys.astype(BF16), gw_ref[...], NN) + gb_ref[...]
        y_ref[...] = ys * _sigmoid(z)

    full = lambda shp: pl.BlockSpec(shp, lambda i: (0,) * len(shp))
    return pl.pallas_call(
        kern, name="s5_fwd", grid=(t // tm,),
        in_specs=[pl.BlockSpec((tm, D_S5), lambda i: (i, 0)), full((D_S5, 2 * N_STATE)),
                  full((8, SUBLANES, N_STATE)), full((2 * N_STATE, D_S5)), full((1, D_S5)),
                  full((D_S5, D_S5)), full((1, D_S5))],
        out_specs=[pl.BlockSpec((tm, 2 * N_STATE), lambda i: (i, 0)), pl.BlockSpec((tm, D_S5), lambda i: (i, 0))],
        out_shape=[jax.ShapeDtypeStruct((t, 2 * N_STATE), BF16), jax.ShapeDtypeStruct((t, D_S5), F32)],
        scratch_shapes=[pltpu.VMEM((tm, 2 * N_STATE), F32), pltpu.VMEM((SUBLANES, 2 * N_STATE), F32)],
        compiler_params=_params(("arbitrary",)),
    )(proj, bmat, coef, cmat, dvec, gw, gb)


def _s5_bwd(proj, h, dout, bmat, coef_b, cmat, dvec, gw, gb):
    t = proj.shape[0]
    tm = _row_tile(t, TS5)
    nt = t // tm
    rb = lambda i: nt - 1 - i

    def kern(u_ref, h_ref, hp_ref, d_ref, b_ref, coef_ref, c_ref, dv_ref, gw_ref, gb_ref,
             du_ref, dc_ref, db_ref, da_ref, dd_ref, dgw_ref, dgb_ref, gbuf, hext, carry):
        i = pl.program_id(0)

        @pl.when(i == 0)
        def _():
            carry[...] = jnp.zeros_like(carry)
            for r in (dc_ref, db_ref, da_ref, dd_ref, dgw_ref, dgb_ref):
                r[...] = jnp.zeros_like(r)

        u = u_ref[...]
        hb = h_ref[...]
        y = _dot(hb, c_ref[...], NN) + dv_ref[...] * u
        ys = _gelu(y)
        ysb = ys.astype(BF16)
        sg = _sigmoid(_dot(ysb, gw_ref[...], NN) + gb_ref[...])
        d_o = d_ref[...]
        dz = d_o * ys * sg * (1.0 - sg)
        dzb = dz.astype(BF16)
        dys = d_o * sg + _dot(dzb, gw_ref[...], NT)
        dgw_ref[...] += _dot(ysb, dzb, TN)
        dgb_ref[...] += jnp.sum(dz, axis=0, keepdims=True)
        dy = dys * _gelu_grad(y)
        dd_ref[...] += jnp.sum(dy * u, axis=0, keepdims=True)
        dyb = dy.astype(BF16)
        dc_ref[...] += _dot(hb, dyb, TN)
        gbuf[...] = _dot(dyb, c_ref[...], NT)
        _s5_scan(gbuf, coef_ref, carry, tm, True)
        g = gbuf[...]
        first = jnp.where(i < nt - 1, hp_ref[HALO16 - 1:HALO16, :].astype(F32), 0.0)
        hext[SUBLANES - 1:SUBLANES, :] = first
        hext[SUBLANES:, :] = hb.astype(F32)
        hprev = hext[pl.ds(SUBLANES - 1, tm), :]
        gre, gim = g[:, 0:N_STATE], g[:, N_STATE:]
        pre, pim = hprev[:, 0:N_STATE], hprev[:, N_STATE:]
        da_ref[0:1, :] += jnp.sum(gre * pre + gim * pim, axis=0, keepdims=True)
        da_ref[1:2, :] += jnp.sum(gim * pre - gre * pim, axis=0, keepdims=True)
        gb16 = g.astype(BF16)
        db_ref[...] += _dot(u.astype(BF16), gb16, TN)
        du_ref[...] = dy * dv_ref[...] + _dot(gb16, b_ref[...], NT)

    full = lambda shp: pl.BlockSpec(shp, lambda i: (0,) * len(shp))
    shape = lambda shp: jax.ShapeDtypeStruct(shp, F32)
    return pl.pallas_call(
        kern, name="s5_bwd", grid=(nt,),
        in_specs=[pl.BlockSpec((tm, D_S5), lambda i: (rb(i), 0)),
                  pl.BlockSpec((tm, 2 * N_STATE), lambda i: (rb(i), 0)),
                  pl.BlockSpec((HALO16, 2 * N_STATE), lambda i: (jnp.maximum(rb(i) * (tm // HALO16) - 1, 0), 0)),
                  pl.BlockSpec((tm, D_S5), lambda i: (rb(i), 0)),
                  full((D_S5, 2 * N_STATE)), full((8, SUBLANES, N_STATE)), full((2 * N_STATE, D_S5)),
                  full((1, D_S5)), full((D_S5, D_S5)), full((1, D_S5))],
        out_specs=[pl.BlockSpec((tm, D_S5), lambda i: (rb(i), 0)), full((2 * N_STATE, D_S5)),
                   full((D_S5, 2 * N_STATE)), full((2, N_STATE)), full((1, D_S5)), full((D_S5, D_S5)),
                   full((1, D_S5))],
        out_shape=[shape((t, D_S5)), shape((2 * N_STATE, D_S5)), shape((D_S5, 2 * N_STATE)),
                   shape((2, N_STATE)), shape((1, D_S5)), shape((D_S5, D_S5)), shape((1, D_S5))],
        scratch_shapes=[pltpu.VMEM((tm, 2 * N_STATE), F32), pltpu.VMEM((tm + SUBLANES, 2 * N_STATE), F32),
                        pltpu.VMEM((SUBLANES, 2 * N_STATE), F32)],
        compiler_params=_params(("arbitrary",)),
    )(proj, h, h, dout, bmat, coef_b, cmat, dvec, gw, gb)


def _lru_gates(ext, x_ref, p_ref, cw_ref, cb_ref, wx_ref, bx_ref, wa_ref, ba_ref, ap_ref, first_tile, row0, tm):
    ext[0:HALO, :] = jnp.where(first_tile, 0.0, p_ref[...])
    ext[HALO:, :] = x_ref[...]
    taps = [ext[pl.ds(HALO - (LRU_CONV - 1) + k, tm), :] for k in range(LRU_CONV)]
    xc = cb_ref[...] + sum(cw_ref[k:k + 1, :] * taps[k] for k in range(LRU_CONV))
    xcb = xc.astype(BF16)
    gx = _sigmoid(_dot(xcb, wx_ref[...], NN) + bx_ref[...])
    ga = _sigmoid(_dot(xcb, wa_ref[...], NN) + ba_ref[...])
    z = -ap_ref[...]
    sp = jnp.maximum(z, 0.0) + jnp.log(1.0 + jnp.exp(-jnp.abs(z)))
    log_a = -LRU_C * ga * sp
    a = jnp.exp(log_a)
    tok = row0 + lax.broadcasted_iota(jnp.int32, a.shape, 0)
    is0 = tok == 0
    mult = jnp.where(is0, 1.0, jnp.sqrt(1.0 - jnp.exp(2.0 * log_a)))
    return taps, xc, xcb, gx, ga, sp, a, mult, is0


def _lru_specs(tm, blk_of):
    col = lambda cidx: pl.BlockSpec((tm, D_LRU), lambda i: (blk_of(i), cidx))
    prev = lambda cidx: pl.BlockSpec((HALO, D_LRU), lambda i: (jnp.maximum(blk_of(i) * (tm // HALO) - 1, 0), cidx))
    full = lambda shp: pl.BlockSpec(shp, lambda i: (0,) * len(shp))
    wts = [full((LRU_CONV, D_LRU)), full((1, D_LRU)), full((D_LRU, D_LRU)), full((1, D_LRU)),
           full((D_LRU, D_LRU)), full((1, D_LRU)), full((1, D_LRU))]
    return col, prev, full, wts


def _lru_fwd(proj, cw, cb, wx, bx, wa, ba, ap):
    t = proj.shape[0]
    tm = _row_tile(t, TM)

    def kern(x_ref, p_ref, g_ref, cw_ref, cb_ref, wx_ref, bx_ref, wa_ref, ba_ref, ap_ref,
             y_ref, h_ref, ext, abuf, carry):
        i = pl.program_id(0)

        @pl.when(i == 0)
        def _():
            carry[...] = jnp.zeros_like(carry)

        _, xc, _, gx, _, _, a, mult, _ = _lru_gates(ext, x_ref, p_ref, cw_ref, cb_ref, wx_ref, bx_ref, wa_ref,
                                                    ba_ref, ap_ref, i == 0, i * tm, tm)
        abuf[...] = a
        h_ref[...] = mult * gx * xc
        _real_scan(abuf, h_ref, carry, tm, False)
        y_ref[...] = h_ref[...] * _gelu(g_ref[...])

    col, prev, full, wts = _lru_specs(tm, lambda i: i)
    out = pl.BlockSpec((tm, D_LRU), lambda i: (i, 0))
    return pl.pallas_call(
        kern, name="lru_fwd", grid=(t // tm,),
        in_specs=[col(1), prev(1), col(2)] + wts, out_specs=[out, out],
        out_shape=[jax.ShapeDtypeStruct((t, D_LRU), F32), jax.ShapeDtypeStruct((t, D_LRU), F32)],
        scratch_shapes=[pltpu.VMEM((tm + HALO, D_LRU), F32), pltpu.VMEM((tm, D_LRU), F32),
                        pltpu.VMEM((SUBLANES, D_LRU), F32)],
        compiler_params=_params(("arbitrary",)),
    )(proj, proj, proj, cw, cb, wx, bx, wa, ba, ap)


def _lru_bwd(proj, h, dout, cw, cb, wx, bx, wa, ba, ap):
    t = proj.shape[0]
    tm = _row_tile(t, TM)
    nt = t // tm
    rb = lambda i: nt - 1 - i

    def kern(x_ref, p_ref, g_ref, h_ref, hp_ref, d_ref, cw_ref, cb_ref, wx_ref, bx_ref, wa_ref, ba_ref, ap_ref,
             dxc_ref, dg_ref, dcw_ref, dcb_ref, dwx_ref, dbx_ref, dwa_ref, dba_ref, dap_ref,
             ext, aext, abuf, gbuf, carry, acarry):
        i = pl.program_id(0)
        blk = nt - 1 - i

        @pl.when(i == 0)
        def _():
            carry[...] = jnp.zeros_like(carry)
            acarry[...] = jnp.zeros_like(acarry)
            for r in (dcw_ref, dcb_ref, dwx_ref, dbx_ref, dwa_ref, dba_ref, dap_ref):
                r[...] = jnp.zeros_like(r)

        taps, xc, xcb, gx, ga, sp, a, mult, is0 = _lru_gates(
            ext, x_ref, p_ref, cw_ref, cb_ref, wx_ref, bx_ref, wa_ref, ba_ref, ap_ref, blk == 0, blk * tm, tm)
        gate = g_ref[...]
        d_o = d_ref[...]
        hcur = h_ref[...]
        dg_ref[...] = d_o * hcur * _gelu_grad(gate)
        aext[0:tm, :] = a
        aext[tm:, :] = acarry[...]
        abuf[...] = aext[pl.ds(1, tm), :]
        gbuf[...] = d_o * _gelu(gate)
        _real_scan(abuf, gbuf, carry, tm, True)
        acarry[...] = jnp.broadcast_to(a[0:1], acarry.shape)
        g = gbuf[...]
        ext[0:HALO, :] = jnp.where(blk == 0, 0.0, hp_ref[...])
        ext[HALO:, :] = hcur
        hprev = ext[pl.ds(HALO - 1, tm), :]
        dmult = jnp.where(is0, 0.0, g * gx * xc)
        dgx = g * mult * xc
        dxc = g * mult * gx
        dlog_a = g * hprev * a - dmult * (a * a) / mult
        dga = dlog_a * (-LRU_C * sp)
        dsp = jnp.sum(dlog_a * (-LRU_C * ga), axis=0, keepdims=True)
        dap_ref[...] += dsp * (-_sigmoid(-ap_ref[...]))
        dpa = (dga * ga * (1.0 - ga))
        dpx = (dgx * gx * (1.0 - gx))
        dpab, dpxb = dpa.astype(BF16), dpx.astype(BF16)
        dwx_ref[...] += _dot(xcb, dpxb, TN)
        dwa_ref[...] += _dot(xcb, dpab, TN)
        dbx_ref[...] += jnp.sum(dpx, axis=0, keepdims=True)
        dba_ref[...] += jnp.sum(dpa, axis=0, keepdims=True)
        dxc = dxc + _dot(dpxb, wx_ref[...], NT) + _dot(dpab, wa_ref[...], NT)
        dxc_ref[...] = dxc
        dcb_ref[...] += jnp.sum(dxc, axis=0, keepdims=True)
        for k in range(LRU_CONV):
            dcw_ref[k:k + 1, :] += jnp.sum(dxc * taps[k], axis=0, keepdims=True)

    col, prev, full, wts = _lru_specs(tm, rb)
    row = pl.BlockSpec((tm, D_LRU), lambda i: (rb(i), 0))
    hprev_spec = pl.BlockSpec((HALO, D_LRU), lambda i: (jnp.maximum(rb(i) * (tm // HALO) - 1, 0), 0))
    shape = lambda shp: jax.ShapeDtypeStruct(shp, F32)
    vec = (1, D_LRU)
    sq = (D_LRU, D_LRU)
    return pl.pallas_call(
        kern, name="lru_bwd", grid=(nt,),
        in_specs=[col(1), prev(1), col(2), row, hprev_spec, row] + wts,
        out_specs=[row, row, full((LRU_CONV, D_LRU)), full(vec), full(sq), full(vec), full(sq), full(vec), full(vec)],
        out_shape=[shape((t, D_LRU)), shape((t, D_LRU)), shape((LRU_CONV, D_LRU)), shape(vec), shape(sq),
                   shape(vec), shape(sq), shape(vec), shape(vec)],
        scratch_shapes=[pltpu.VMEM((tm + HALO, D_LRU), F32), pltpu.VMEM((tm + HALO, D_LRU), F32),
                        pltpu.VMEM((tm, D_LRU), F32), pltpu.VMEM((tm, D_LRU), F32),
                        pltpu.VMEM((SUBLANES, D_LRU), F32), pltpu.VMEM((SUBLANES, D_LRU), F32)],
        compiler_params=_params(("arbitrary",)),
    )(proj, proj, proj, h, h, dout, cw, cb, wx, bx, wa, ba, ap)


def _assemble_dproj(dq, dk, dv, du, dxc, dgate, cos, sin_s, cw):
    t = dq.shape[0]
    tm = _row_tile(t, TM)
    nt = t // tm

    def kern(dq_ref, dk_ref, dv_ref, du_ref, dx_ref, dn_ref, dg_ref, c_ref, s_ref, cw_ref, o_ref, b_ref, ext):
        i = pl.program_id(0)

        @pl.when(i == 0)
        def _():
            b_ref[...] = jnp.zeros_like(b_ref)

        def put(lo, val):
            hi = lo + val.shape[1]
            o_ref[:, lo:hi] = val.astype(BF16)
            b_ref[:, lo:hi] += jnp.sum(val, axis=0, keepdims=True)

        c = c_ref[...]
        s = s_ref[...]
        for ch in range(4):
            x = dq_ref[:, ch * 128:(ch + 1) * 128] * (HEAD_DIM ** -0.5)
            put(ch * 128, x * c - _rope_swap(x) * s)
        x = dk_ref[...]
        put(512, x * c - _rope_swap(x) * s)
        put(640, dv_ref[...])
        put(768, du_ref[...])
        ext[0:tm, :] = dx_ref[...]
        ext[tm:, :] = jnp.where(i < nt - 1, dn_ref[...], 0.0)
        put(1024, sum(cw_ref[k:k + 1, :] * ext[pl.ds(LRU_CONV - 1 - k, tm), :] for k in range(LRU_CONV)))
        put(1280, dg_ref[...])

    row = lambda w: pl.BlockSpec((tm, w), lambda i: (i, 0))
    nxt = pl.BlockSpec((HALO, D_LRU), lambda i: (jnp.minimum((i + 1) * (tm // HALO), t // HALO - 1), 0))
    return pl.pallas_call(
        kern, name="assemble_dproj", grid=(nt,),
        in_specs=[row(512), row(128), row(128), row(256), row(256), nxt, row(256), row(128), row(128),
                  pl.BlockSpec((LRU_CONV, D_LRU), lambda i: (0, 0))],
        out_specs=[row(D_IN), pl.BlockSpec((1, D_IN), lambda i: (0, 0))],
        out_shape=[jax.ShapeDtypeStruct((t, D_IN), BF16), jax.ShapeDtypeStruct((1, D_IN), F32)],
        scratch_shapes=[pltpu.VMEM((tm + HALO, D_LRU), F32)],
        compiler_params=_params(("arbitrary",)),
    )(dq, dk, dv, du, dxc, dxc, dgate, cos, sin_s, cw)


def _blockdiag_s5(bbar_re, bbar_im, c_re, c_im):
    eye = jnp.eye(S5_GROUPS, dtype=F32)
    b_of = lambda m: jnp.einsum('gpc,gh->gchp', m, eye).reshape(D_S5, N_STATE)
    c_of = lambda m: jnp.einsum('gcp,gh->gphc', m, eye).reshape(N_STATE, D_S5)
    bmat = jnp.concatenate([b_of(bbar_re), b_of(bbar_im)], axis=1)
    cmat = jnp.concatenate([c_of(c_re), -c_of(c_im)], axis=0)
    return bmat, cmat


def _s5_prepare(a_re, a_im, b_re, b_im, c_re, c_im, log_dt):
    lam_re = jnp.minimum(a_re, -1e-4)
    lam_im = a_im
    dt = jnp.exp(log_dt)[:, None]
    decay = jnp.exp(dt * lam_re)
    ang = dt * lam_im
    abar_re = decay * jnp.cos(ang)
    abar_im = decay * jnp.sin(ang)
    den = jnp.square(lam_re) + jnp.square(lam_im)
    nr = abar_re - 1.0
    ni = abar_im
    coef_re = (nr * lam_re + ni * lam_im) / den
    coef_im = (ni * lam_re - nr * lam_im) / den
    bbar_re = coef_re[..., None] * b_re - coef_im[..., None] * b_im
    bbar_im = coef_re[..., None] * b_im + coef_im[..., None] * b_re
    bmat, cmat = _blockdiag_s5(bbar_re, bbar_im, c_re, c_im)
    return abar_re.reshape(N_STATE), abar_im.reshape(N_STATE), bmat, cmat


def _blockdiag_lru(w):
    eye = jnp.eye(LRU_HEADS, dtype=F32)
    return jnp.einsum('hij,hk->hikj', w, eye).reshape(D_LRU, D_LRU)


def _rope_tables(t):
    inv_freq = ROPE_THETA ** (-jnp.arange(0, HEAD_DIM, 2, dtype=F32) / HEAD_DIM)
    ang = jnp.arange(t, dtype=F32)[:, None] * inv_freq[None, :]
    cos, sin = jnp.cos(ang), jnp.sin(ang)
    return jnp.tile(jnp.concatenate([cos, cos], axis=1), (1, 2)), jnp.tile(jnp.concatenate([-sin, sin], axis=1), (1, 2))


def _vec(v):
    return v.reshape(1, -1)


def _layer_weights(p):
    abar_re, abar_im, bmat, cmat = _s5_prepare(p['s5_a_re'], p['s5_a_im'], p['s5_b_re'], p['s5_b_im'],
                                               p['s5_c_re'], p['s5_c_im'], p['s5_log_dt'])
    return dict(
        coef_f=_s5_coefs(abar_re, abar_im, False), coef_b=_s5_coefs(abar_re, abar_im, True),
        bmat=bmat.astype(BF16), cmat=cmat.astype(BF16),
        wx=_blockdiag_lru(p['lru_wx']).astype(BF16), wa=_blockdiag_lru(p['lru_wa']).astype(BF16),
        gw=p['s5_glu_w'].astype(BF16))


def _layer_fwd(x, xb, p, w, cos, sin_s):
    t = x.shape[0]
    tm = _row_tile(t, TM)
    layer = p['layer']
    qkv, uxg = _in_proj(xb, p['w_in'], _vec(p['b_in']), cos, sin_s, layer)
    ya, lse = _attn_fwd(qkv, _vec(p['attn_sinks']))
    h5, ys = _s5_fwd(uxg, w['bmat'], w['coef_f'], w['cmat'], _vec(p['s5_d']), w['gw'], _vec(p['s5_glu_b']))
    lru_w = (p['lru_conv_w'], _vec(p['lru_conv_b']), w['wx'], _vec(p['lru_bx']), w['wa'], _vec(p['lru_ba']),
             _vec(p['lru_a_param']))
    yl, hl = _lru_fwd(uxg, *lru_w)
    mix, x1, x1b, xhat1, rstd1 = _mix_out_ln(ya, ys, yl, _vec(p['mix_norm_g']), p['w_out'], _vec(p['b_out']), x,
                                             _vec(p['ln1_g']), _vec(p['ln1_b']), layer)
    gpre, gconv, up, hmid = _ffn_hidden_fwd(x1b, p['ffn_w_gate'], p['ffn_w_up'], p['ffn_conv_w'], p['ffn_conv_b'],
                                            layer)
    x2, x2b, xhat2, rstd2 = _matmul_ln(
        "ffn_down_ln", hmid, p['ffn_w_down'], jnp.zeros((1, D_MODEL), F32), x1, _vec(p['ln2_g']), _vec(p['ln2_b']),
        a_blk=(N_CHIPS, tm, FF_SH), a_map=lambda i: (0, i, 0), w_blk=(N_CHIPS, FF_SH, D_MODEL), parts=N_CHIPS,
        layer=layer)
    saved = dict(xb=xb, uxg=uxg, qkv=qkv, ya=ya, lse=lse, h5=h5, ys=ys, yl=yl, hl=hl, mix=mix, x1b=x1b, xhat1=xhat1,
                 rstd1=rstd1, gpre=gpre, gconv=gconv, up=up, hmid=hmid, xhat2=xhat2, rstd2=rstd2, lru_w=lru_w)
    return x2, x2b, saved


def _layer_bwd(dr2, dr2b, s, p, w, cos, sin_s, big, below):
    t = dr2.shape[0]
    tk = _row_tile(t, TMM)
    nk = t // tk
    tm = _row_tile(t, TM)
    layer = p['layer']
    big = dict(big)
    g = {}
    dup, dgpre, g['ffn_conv_w'], g['ffn_conv_b'] = _ffn_hidden_bwd(
        dr2b, s['gpre'], s['gconv'], s['up'], p['ffn_w_down'], p['ffn_conv_w'], layer)
    big['ffn_w_down'] = _matmul(
        "d_w_down", s['hmid'], dr2b, a_blk=(None, tk, FF_SH), a_map=lambda i, j, k: (i, k, 0), b_blk=(tk, D_MODEL),
        b_map=lambda i, j, k: (k, 0), out_shape=(DEPTH, N_CHIPS, FF_SH, D_MODEL), o_blk=(None, None, FF_SH, D_MODEL),
        o_map=lambda i, j: (layer, i, 0, 0), grid=(N_CHIPS, 1, nk), dims=TN, into=big['ffn_w_down'])
    d_ffn_w = lambda name, dact, buf: _matmul(
        name, s['x1b'], dact, a_blk=(tk, D_MODEL), a_map=lambda i, j, k: (k, 0), b_blk=(None, tk, FF_SH),
        b_map=lambda i, j, k: (j, k, 0), out_shape=(DEPTH, N_CHIPS, D_MODEL, FF_SH),
        o_blk=(None, None, D_MODEL, FF_SH), o_map=lambda i, j: (layer, j, 0, 0), grid=(1, N_CHIPS, nk), dims=TN,
        into=buf)
    big['ffn_w_gate'] = d_ffn_w("d_w_gate", dgpre, big['ffn_w_gate'])
    big['ffn_w_up'] = d_ffn_w("d_w_up", dup, big['ffn_w_up'])
    wspec = dict(b_blk=(None, None, D_MODEL, FF_SH), b_map=lambda i, j, k: (layer, k, 0, 0))
    dr1, dr1b, g['ln1_g'], g['ln1_b'], g['b_out'] = _matmul(
        "d_x1", dgpre, p['ffn_w_gate'], pair2=(dup, p['ffn_w_up']), a_blk=(None, tm, FF_SH),
        a_map=lambda i, j, k: (k, i, 0), out_shape=(t, D_MODEL), o_blk=(tm, D_MODEL), o_map=lambda i, j: (i, 0),
        grid=(t // tm, 1, N_CHIPS), dims=NT, add=dr2, add_scale=ALPHA,
        ln_bwd=(s['xhat1'], s['rstd1'], _vec(p['ln1_g'])), **wspec)
    big['w_out'] = _matmul(
        "d_w_out", s['mix'], dr1b, a_blk=(tk, D_MODEL), a_map=lambda i, j, k: (k, 0), b_blk=(tk, D_MODEL),
        b_map=lambda i, j, k: (k, 0), out_shape=(DEPTH, D_MODEL, D_MODEL), o_blk=(None, D_MODEL, D_MODEL),
        o_map=lambda i, j: (layer, 0, 0), grid=(1, 1, nk), dims=TN, into=big['w_out'])
    dya, dys, dyl, g['mix_norm_g'] = _d_mix_rms(dr1b, p['w_out'], s['ya'], s['ys'], s['yl'], _vec(p['mix_norm_g']),
                                                layer)
    dq, dk, dv, g['attn_sinks'] = _attn_bwd(s['qkv'], s['ya'], dya, s['lse'], _vec(p['attn_sinks']))
    du, dcmat, dbmat, dabar, g['s5_d'], g['s5_glu_w'], g['s5_glu_b'] = _s5_bwd(
        s['uxg'], s['h5'], dys, w['bmat'], w['coef_b'], w['cmat'], _vec(p['s5_d']), w['gw'], _vec(p['s5_glu_b']))
    (dxc, dgate, g['lru_conv_w'], g['lru_conv_b'], dwx, g['lru_bx'], dwa, g['lru_ba'],
     g['lru_a_param']) = _lru_bwd(s['uxg'], s['hl'], dyl, *s['lru_w'])
    dproj, g['b_in'] = _assemble_dproj(dq, dk, dv, du, dxc, dgate, cos, sin_s, p['lru_conv_w'])
    big['w_in'] = _matmul(
        "d_w_in", s['xb'], dproj, a_blk=(tk, D_MODEL), a_map=lambda i, j, k: (k, 0), b_blk=(tk, IN_SH),
        b_map=lambda i, j, k: (k, j), out_shape=(DEPTH, N_CHIPS, D_MODEL, IN_SH), o_blk=(None, None, D_MODEL, IN_SH),
        o_map=lambda i, j: (layer, j, 0, 0), grid=(1, N_CHIPS, nk), dims=TN, into=big['w_in'])
    ln_below = None
    if below is not None:
        ln_below = (below[0]['xhat2'], below[0]['rstd2'], _vec(below[1]['ln2_g']))
    dx = _matmul("d_x", dproj, p['w_in'], a_blk=(tm, IN_SH), a_map=lambda i, j, k: (i, k),
                 b_blk=(None, None, D_MODEL, IN_SH), b_map=lambda i, j, k: (layer, k, 0, 0), out_shape=(t, D_MODEL),
                 o_blk=(tm, D_MODEL), o_map=lambda i, j: (i, 0), grid=(t // tm, 1, N_CHIPS), dims=NT,
                 add=dr1, add_scale=ALPHA, ln_bwd=ln_below)
    return dx, _param_chain(g, p, dabar, dbmat, dcmat, dwx, dwa), big


def _layer_fwd_v1(x, p, w, cos, sin_s):
    t = x.shape[0]
    nt = t // _row_tile(t, TM)
    tm = t // nt
    proj = _matmul("in_proj", x, p['w_in'], a_blk=(tm, D_MODEL), a_map=lambda i, j, k: (i, 0),
                   b_blk=(None, D_MODEL, IN_SH), b_map=lambda i, j, k: (j, 0, 0), out_shape=(t, D_IN),
                   o_blk=(tm, IN_SH), o_map=lambda i, j: (i, j), grid=(nt, N_CHIPS, 1), dims=NN,
                   bias=_vec(p['b_in']), bias_blk=(1, IN_SH), bias_map=lambda i, j, k: (0, j))
    qkv = _qkv_post(proj, cos, sin_s)
    ya, lse = _attn_fwd(qkv, _vec(p['attn_sinks']))
    h5, ys = _s5_fwd(proj, w['bmat'], w['coef_f'], w['cmat'], _vec(p['s5_d']), w['gw'], _vec(p['s5_glu_b']))
    lru_w = (p['lru_conv_w'], _vec(p['lru_conv_b']), w['wx'], _vec(p['lru_bx']), w['wa'], _vec(p['lru_ba']),
             _vec(p['lru_a_param']))
    yl, hl = _lru_fwd(proj, *lru_w)
    mix = _rms_fwd(ya, ys, yl, _vec(p['mix_norm_g']))
    f1 = _matmul("out_proj", mix, p['w_out'], a_blk=(tm, D_MODEL), a_map=lambda i, j, k: (i, 0),
                 b_blk=(D_MODEL, D_MODEL), b_map=lambda i, j, k: (0, 0), out_shape=(t, D_MODEL),
                 o_blk=(tm, D_MODEL), o_map=lambda i, j: (i, 0), grid=(nt, 1, 1), dims=NN,
                 bias=_vec(p['b_out']), bias_blk=(1, D_MODEL), bias_map=lambda i, j, k: (0, 0))
    x1, xhat1, rstd1 = _ln_fwd(x, f1, _vec(p['ln1_g']), _vec(p['ln1_b']))
    ffn_in = lambda name, wmat: _matmul(
        name, x1, wmat, a_blk=(tm, D_MODEL), a_map=lambda i, j, k: (i, 0), b_blk=(None, D_MODEL, FF_SH),
        b_map=lambda i, j, k: (j, 0, 0), out_shape=(N_CHIPS, t, FF_SH), o_blk=(None, tm, FF_SH),
        o_map=lambda i, j: (j, i, 0), grid=(nt, N_CHIPS, 1), dims=NN)
    gpre = ffn_in("ffn_gate", p['ffn_w_gate'])
    up = ffn_in("ffn_up", p['ffn_w_up'])
    hmid = _ffn_mid_fwd(gpre, up, p['ffn_conv_w'], p['ffn_conv_b'])
    f2 = _matmul("ffn_down", hmid, p['ffn_w_down'], a_blk=(None, tm, FF_SH), a_map=lambda i, j, k: (k, i, 0),
                 b_blk=(None, FF_SH, D_MODEL), b_map=lambda i, j, k: (k, 0, 0), out_shape=(t, D_MODEL),
                 o_blk=(tm, D_MODEL), o_map=lambda i, j: (i, 0), grid=(nt, 1, N_CHIPS), dims=NN)
    x2, xhat2, rstd2 = _ln_fwd(x1, f2, _vec(p['ln2_g']), _vec(p['ln2_b']))
    saved = dict(x=x, proj=proj, qkv=qkv, ya=ya, lse=lse, h5=h5, ys=ys, yl=yl, hl=hl, mix=mix, x1=x1, xhat1=xhat1,
                 rstd1=rstd1, gpre=gpre, up=up, xhat2=xhat2, rstd2=rstd2, lru_w=lru_w)
    return x2, saved


def _param_chain(g, p, dabar, dbmat, dcmat, dwx, dwa):
    s5_names = ('s5_a_re', 's5_a_im', 's5_b_re', 's5_b_im', 's5_c_re', 's5_c_im', 's5_log_dt')
    _, s5_vjp = jax.vjp(_s5_prepare, *[p[n] for n in s5_names])
    for n, val in zip(s5_names, s5_vjp((dabar[0], dabar[1], dbmat, dcmat))):
        g[n] = val
    g['lru_wx'] = jax.vjp(_blockdiag_lru, p['lru_wx'])[1](dwx)[0]
    g['lru_wa'] = jax.vjp(_blockdiag_lru, p['lru_wa'])[1](dwa)[0]
    return g


def _layer_bwd_v1(dx2, s, p, w, cos, sin_s):
    t = dx2.shape[0]
    nt = t // _row_tile(t, TM)
    tm = t // nt
    g = {}
    dr2, g['ln2_g'], g['ln2_b'], _ = _ln_bwd(dx2, s['xhat2'], s['rstd2'], _vec(p['ln2_g']))
    dhmid = _matmul("d_hmid", dr2, p['ffn_w_down'], a_blk=(tm, D_MODEL), a_map=lambda i, j, k: (i, 0),
                    b_blk=(None, FF_SH, D_MODEL), b_map=lambda i, j, k: (j, 0, 0), out_shape=(N_CHIPS, t, FF_SH),
                    o_blk=(None, tm, FF_SH), o_map=lambda i, j: (j, i, 0), grid=(nt, N_CHIPS, 1), dims=NT)
    hmid, dup, dgc, g['ffn_conv_w'], g['ffn_conv_b'] = _ffn_mid_bwd(s['gpre'], s['up'], dhmid, p['ffn_conv_w'],
                                                                    p['ffn_conv_b'])
    g['ffn_w_down'] = _matmul("d_w_down", hmid, dr2, a_blk=(None, tm, FF_SH), a_map=lambda i, j, k: (i, k, 0),
                              b_blk=(tm, D_MODEL), b_map=lambda i, j, k: (k, 0), out_shape=(N_CHIPS, FF_SH, D_MODEL),
                              o_blk=(None, FF_SH, D_MODEL), o_map=lambda i, j: (i, 0, 0), grid=(N_CHIPS, 1, nt), dims=TN)
    dgpre = _ffn_conv_t(dgc, p['ffn_conv_w'])
    d_ffn_w = lambda name, dact: _matmul(
        name, s['x1'], dact, a_blk=(tm, D_MODEL), a_map=lambda i, j, k: (k, 0), b_blk=(None, tm, FF_SH),
        b_map=lambda i, j, k: (j, k, 0), out_shape=(N_CHIPS, D_MODEL, FF_SH), o_blk=(None, D_MODEL, FF_SH),
        o_map=lambda i, j: (j, 0, 0), grid=(1, N_CHIPS, nt), dims=TN)
    g['ffn_w_gate'] = d_ffn_w("d_w_gate", dgpre)
    g['ffn_w_up'] = d_ffn_w("d_w_up", dup)
    d_ffn_x = lambda name, dact, wmat, add, scale: _matmul(
        name, dact, wmat, a_blk=(None, tm, FF_SH), a_map=lambda i, j, k: (k, i, 0), b_blk=(None, D_MODEL, FF_SH),
        b_map=lambda i, j, k: (k, 0, 0), out_shape=(t, D_MODEL), o_blk=(tm, D_MODEL), o_map=lambda i, j: (i, 0),
        grid=(nt, 1, N_CHIPS), dims=NT, add=add, add_scale=scale)
    dx1 = d_ffn_x("d_x1_gate", dgpre, p['ffn_w_gate'], dr2, ALPHA)
    dx1 = d_ffn_x("d_x1_up", dup, p['ffn_w_up'], dx1, 1.0)
    dr1, g['ln1_g'], g['ln1_b'], g['b_out'] = _ln_bwd(dx1, s['xhat1'], s['rstd1'], _vec(p['ln1_g']))
    g['w_out'] = _matmul("d_w_out", s['mix'], dr1, a_blk=(tm, D_MODEL), a_map=lambda i, j, k: (k, 0),
                         b_blk=(tm, D_MODEL), b_map=lambda i, j, k: (k, 0), out_shape=(D_MODEL, D_MODEL),
                         o_blk=(D_MODEL, D_MODEL), o_map=lambda i, j: (0, 0), grid=(1, 1, nt), dims=TN)
    dmix = _matmul("d_mix", dr1, p['w_out'], a_blk=(tm, D_MODEL), a_map=lambda i, j, k: (i, 0),
                   b_blk=(D_MODEL, D_MODEL), b_map=lambda i, j, k: (0, 0), out_shape=(t, D_MODEL),
                   o_blk=(tm, D_MODEL), o_map=lambda i, j: (i, 0), grid=(nt, 1, 1), dims=NT)
    dya, dys, dyl, g['mix_norm_g'] = _rms_bwd(dmix, s['ya'], s['ys'], s['yl'], _vec(p['mix_norm_g']))
    dq, dk, dv, g['attn_sinks'] = _attn_bwd(s['qkv'], s['ya'], dya, s['lse'], _vec(p['attn_sinks']))
    du, dcmat, dbmat, dabar, g['s5_d'], g['s5_glu_w'], g['s5_glu_b'] = _s5_bwd(
        s['proj'], s['h5'], dys, w['bmat'], w['coef_b'], w['cmat'], _vec(p['s5_d']), w['gw'], _vec(p['s5_glu_b']))
    (dxc, dgate, g['lru_conv_w'], g['lru_conv_b'], dwx, g['lru_bx'], dwa, g['lru_ba'],
     g['lru_a_param']) = _lru_bwd(s['proj'], s['hl'], dyl, *s['lru_w'])
    dproj, g['b_in'] = _assemble_dproj(dq, dk, dv, du, dxc, dgate, cos, sin_s, p['lru_conv_w'])
    g['w_in'] = _matmul("d_w_in", s['x'], dproj, a_blk=(tm, D_MODEL), a_map=lambda i, j, k: (k, 0),
                        b_blk=(tm, IN_SH), b_map=lambda i, j, k: (k, j), out_shape=(N_CHIPS, D_MODEL, IN_SH),
                        o_blk=(None, D_MODEL, IN_SH), o_map=lambda i, j: (j, 0, 0), grid=(1, N_CHIPS, nt), dims=TN)
    dx = _matmul("d_x", dproj, p['w_in'], a_blk=(tm, IN_SH), a_map=lambda i, j, k: (i, k),
                 b_blk=(None, D_MODEL, IN_SH), b_map=lambda i, j, k: (k, 0, 0), out_shape=(t, D_MODEL),
                 o_blk=(tm, D_MODEL), o_map=lambda i, j: (i, 0), grid=(nt, 1, N_CHIPS), dims=NT,
                 add=dr1, add_scale=ALPHA)
    s5_names = ('s5_a_re', 's5_a_im', 's5_b_re', 's5_b_im', 's5_c_re', 's5_c_im', 's5_log_dt')
    _, s5_vjp = jax.vjp(_s5_prepare, *[p[n] for n in s5_names])
    for n, val in zip(s5_names, s5_vjp((dabar[0], dabar[1], dbmat, dcmat))):
        g[n] = val
    g['lru_wx'] = jax.vjp(_blockdiag_lru, p['lru_wx'])[1](dwx)[0]
    g['lru_wa'] = jax.vjp(_blockdiag_lru, p['lru_wa'])[1](dwa)[0]
    return dx, g


ROW_TILE = 512


def _pick_rows(rows):
    for rt in range(min(rows, ROW_TILE), 0, -1):
        if rows % rt == 0 and (rt % 16 == 0 or rt == rows):
            return rt
    return rows


def _cast_bf16(a):
    a2 = a.reshape(-1, a.shape[-1])
    rows, c = a2.shape
    rt = _pick_rows(rows)

    def kern(a_ref, o_ref):
        o_ref[...] = a_ref[...].astype(BF16)

    spec = pl.BlockSpec((rt, c), lambda i: (i, 0))
    out = pl.pallas_call(kern, name="cast_bf16", grid=(rows // rt,), in_specs=[spec], out_specs=spec,
                         out_shape=jax.ShapeDtypeStruct((rows, c), BF16), compiler_params=_params(("parallel",)))(a2)
    return out.reshape(a.shape)


def _sum_parts(name, parts, shape):
    c = shape[-1]
    rows = math.prod(shape[:-1])
    rt = _pick_rows(rows)
    n = len(parts)

    def kern(*refs):
        acc = refs[0][...].astype(F32)
        for r in refs[1:n]:
            acc = acc + r[...].astype(F32)
        refs[n][...] = acc

    specs, args = [], []
    for arr, j in parts:
        if j is None:
            specs.append(pl.BlockSpec((rt, c), lambda i: (i, 0)))
            args.append(arr.reshape(rows, c))
        else:
            specs.append(pl.BlockSpec((None, rt, c), functools.partial(lambda i, jj: (jj, i, 0), jj=j)))
            args.append(arr.reshape(arr.shape[0], rows, c))
    out = pl.pallas_call(kern, name=name, grid=(rows // rt,), in_specs=specs,
                         out_specs=pl.BlockSpec((rt, c), lambda i: (i, 0)),
                         out_shape=jax.ShapeDtypeStruct((rows, c), F32), compiler_params=_params(("parallel",)))(*args)
    return out.reshape(shape)


def _adamw(name, w, g, m, v):
    shape = w.shape
    c = shape[-1]
    rows = math.prod(shape[:-1])
    rt = _pick_rows(rows)

    def kern(w_ref, g_ref, m_ref, v_ref, d_ref, nm_ref, nv_ref):
        g_ = g_ref[...]
        m_ = ADAM_B1 * m_ref[...] + (1.0 - ADAM_B1) * g_
        v_ = ADAM_B2 * v_ref[...] + (1.0 - ADAM_B2) * jnp.square(g_)
        m_hat = m_ / (1.0 - ADAM_B1 ** ADAM_STEP)
        v_hat = v_ / (1.0 - ADAM_B2 ** ADAM_STEP)
        d_ref[...] = -ADAM_LR * (m_hat / (jnp.sqrt(v_hat) + ADAM_EPS) + ADAM_WD * w_ref[...])
        nm_ref[...] = m_
        nv_ref[...] = v_

    spec = pl.BlockSpec((rt, c), lambda i: (i, 0))
    outs = pl.pallas_call(kern, name=name, grid=(rows // rt,), in_specs=[spec] * 4, out_specs=[spec] * 3,
                          out_shape=[jax.ShapeDtypeStruct((rows, c), F32)] * 3,
                          compiler_params=_params(("parallel",)))(*[a.reshape(rows, c) for a in (w, g, m, v)])
    return tuple(o.reshape(shape) for o in outs)


def _position():
    return lax.axis_index("x"), lax.axis_index("y"), lax.axis_index("c")


def _other_chips(x, y):
    return [(1 - x, y), (x, 1 - y), (1 - x, 1 - y)]


def _comm_call(name, kern, arrs, out_shapes, n_remote, n_local):
    return pl.pallas_call(
        kern, name=name, in_specs=[ANY] * len(arrs), out_specs=[ANY] * len(out_shapes), out_shape=out_shapes,
        scratch_shapes=[pltpu.SemaphoreType.DMA((n_remote,)), pltpu.SemaphoreType.DMA((n_remote,)),
                        pltpu.SemaphoreType.DMA((n_local,))],
    )(*arrs)


def _allgather_chips(arrs):
    n = len(arrs)

    def kern(*refs):
        ins, outs = refs[:n], refs[n:2 * n]
        send, recv, loc = refs[2 * n:]
        x, y, c = _position()
        me = 2 * x + y
        chips = _other_chips(x, y)
        own, sent = [], []
        for t in range(n):
            own.append(pltpu.make_async_copy(ins[t], outs[t].at[:, pl.ds(me, 1)], loc.at[t]))
            own[-1].start()
            for j, (px, py) in enumerate(chips):
                sent.append(pltpu.make_async_remote_copy(
                    src_ref=ins[t], dst_ref=outs[t].at[:, pl.ds(me, 1)], send_sem=send.at[3 * t + j],
                    recv_sem=recv.at[3 * t + j], device_id=(px, py, c), device_id_type=MESH))
                sent[-1].start()
        for t in range(n):
            for j, (px, py) in enumerate(chips):
                pltpu.make_async_remote_copy(
                    src_ref=ins[t], dst_ref=outs[t].at[:, pl.ds(2 * px + py, 1)], send_sem=send.at[3 * t + j],
                    recv_sem=recv.at[3 * t + j], device_id=(px, py, c), device_id_type=MESH).wait_recv()
        for cp in sent:
            cp.wait_send()
        for cp in own:
            cp.wait()

    outs = [jax.ShapeDtypeStruct((a.shape[0], N_CHIPS) + a.shape[2:], a.dtype) for a in arrs]
    return _comm_call("allgather_chips", kern, arrs, outs, 3 * n, n)


def _pair_exchange(arrs):
    n = len(arrs)

    def kern(*refs):
        ins, outs = refs[:n], refs[n:3 * n]
        send, recv, loc = refs[3 * n:]
        x, y, c = _position()
        own, sent = [], []
        for t in range(n):
            r2 = ins[t].shape[2] // 2
            own.append(pltpu.make_async_copy(ins[t].at[:, :, pl.ds(c * r2, r2)], outs[2 * t], loc.at[t]))
            own[-1].start()
            sent.append(pltpu.make_async_remote_copy(
                src_ref=ins[t].at[:, :, pl.ds((1 - c) * r2, r2)], dst_ref=outs[2 * t + 1], send_sem=send.at[t],
                recv_sem=recv.at[t], device_id=(x, y, 1 - c), device_id_type=MESH))
            sent[-1].start()
        for cp in sent:
            cp.wait()
        for cp in own:
            cp.wait()

    outs = []
    for a in arrs:
        half = jax.ShapeDtypeStruct(a.shape[:2] + (a.shape[2] // 2, a.shape[3]), a.dtype)
        outs += [half, half]
    return _comm_call("pair_exchange", kern, arrs, outs, n, n)


def _chip_scatter(arrs):
    n = len(arrs)

    def kern(*refs):
        ins, outs = refs[:n], refs[n:3 * n]
        send, recv, loc = refs[3 * n:]
        x, y, c = _position()
        me = 2 * x + y
        chips = _other_chips(x, y)
        own, sent = [], []
        for t in range(n):
            own.append(pltpu.make_async_copy(ins[t].at[:, pl.ds(me, 1)], outs[2 * t], loc.at[t]))
            own[-1].start()
            for j, (px, py) in enumerate(chips):
                sent.append(pltpu.make_async_remote_copy(
                    src_ref=ins[t].at[:, pl.ds(2 * px + py, 1)], dst_ref=outs[2 * t + 1].at[j],
                    send_sem=send.at[3 * t + j], recv_sem=recv.at[3 * t + j], device_id=(px, py, c),
                    device_id_type=MESH))
                sent[-1].start()
        for cp in sent:
            cp.wait()
        for cp in own:
            cp.wait()

    outs = []
    for a in arrs:
        one = (a.shape[0], 1) + a.shape[2:]
        outs += [jax.ShapeDtypeStruct(one, a.dtype), jax.ShapeDtypeStruct((3,) + one, a.dtype)]
    return _comm_call("chip_scatter", kern, arrs, outs, 3 * n, n)


def _pair_gather(arrs):
    n = len(arrs)

    def kern(*refs):
        ins, outs = refs[:n], refs[n:2 * n]
        send, recv, loc = refs[2 * n:]
        x, y, c = _position()
        own, sent = [], []
        for t in range(n):
            own.append(pltpu.make_async_copy(ins[t], outs[t].at[:, pl.ds(c, 1)], loc.at[t]))
            own[-1].start()
            sent.append(pltpu.make_async_remote_copy(
                src_ref=ins[t], dst_ref=outs[t].at[:, pl.ds(c, 1)], send_sem=send.at[t], recv_sem=recv.at[t],
                device_id=(x, y, 1 - c), device_id_type=MESH))
            sent[-1].start()
        for t in range(n):
            sent[t].wait_send()
            pltpu.make_async_remote_copy(
                src_ref=ins[t], dst_ref=outs[t].at[:, pl.ds(1 - c, 1)], send_sem=send.at[t], recv_sem=recv.at[t],
                device_id=(x, y, 1 - c), device_id_type=MESH).wait_recv()
        for cp in own:
            cp.wait()

    outs = [jax.ShapeDtypeStruct((a.shape[0], 2) + a.shape[2:], a.dtype) for a in arrs]
    return _comm_call("pair_gather", kern, arrs, outs, n, n)


_FLIPS = [(0, 0, 1), (1, 0, 0), (0, 1, 0), (1, 1, 0), (1, 0, 1), (0, 1, 1), (1, 1, 1)]


def _allgather_devices(v):
    def kern(v_ref, o_ref, send, recv, loc):
        x, y, c = _position()
        me = 4 * x + 2 * y + c
        peers = [((1 - x) if fx else x, (1 - y) if fy else y, (1 - c) if fc else c) for fx, fy, fc in _FLIPS]
        own = pltpu.make_async_copy(v_ref, o_ref.at[pl.ds(me, 1)], loc.at[0])
        own.start()
        sent = []
        for k, peer in enumerate(peers):
            sent.append(pltpu.make_async_remote_copy(
                src_ref=v_ref, dst_ref=o_ref.at[pl.ds(me, 1)], send_sem=send.at[k], recv_sem=recv.at[k],
                device_id=peer, device_id_type=MESH))
            sent[-1].start()
        for k, (px, py, pc) in enumerate(peers):
            pltpu.make_async_remote_copy(
                src_ref=v_ref, dst_ref=o_ref.at[pl.ds(4 * px + 2 * py + pc, 1)], send_sem=send.at[k],
                recv_sem=recv.at[k], device_id=(px, py, pc), device_id_type=MESH).wait_recv()
        for cp in sent:
            cp.wait_send()
        own.wait()

    out = jax.ShapeDtypeStruct((N_DEV,) + v.shape[1:], v.dtype)
    return _comm_call("allgather_devices", kern, [v], [out], len(_FLIPS), 1)[0]


def _exchange(name, arrs, out_shapes, n_local, n_remote, plan):
    n_in, n_out = len(arrs), len(out_shapes)

    def kern(*refs):
        ins, outs = refs[:n_in], refs[n_in:n_in + n_out]
        send, recv, loc = refs[n_in + n_out:]
        local, remote = plan(ins, outs, *_position())
        assert len(local) == n_local and len(remote) == n_remote
        own = [pltpu.make_async_copy(s, d, loc.at[k]) for k, (s, d) in enumerate(local)]
        for cp in own:
            cp.start()
        sent = [pltpu.make_async_remote_copy(src_ref=s, dst_ref=d, send_sem=send.at[k], recv_sem=recv.at[k],
                                             device_id=peer, device_id_type=MESH)
                for k, (s, d, peer, _) in enumerate(remote)]
        for cp in sent:
            cp.start()
        for k, (s, _, peer, landing) in enumerate(remote):
            pltpu.make_async_remote_copy(src_ref=s, dst_ref=landing, send_sem=send.at[k], recv_sem=recv.at[k],
                                         device_id=peer, device_id_type=MESH).wait_recv()
        for cp in sent:
            cp.wait_send()
        for cp in own:
            cp.wait()

    return pl.pallas_call(
        kern, name=name, in_specs=[ANY] * n_in, out_specs=[ANY] * n_out, out_shape=out_shapes,
        scratch_shapes=[pltpu.SemaphoreType.DMA((n_remote,)), pltpu.SemaphoreType.DMA((n_remote,)),
                        pltpu.SemaphoreType.DMA((max(n_local, 1),))],
    )(*arrs)


def _allgather_chips(arrs, halved=()):
    n = len(arrs)
    layers = arrs[0].shape[0]

    def plan(ins, outs, x, y, c):
        me = 2 * x + y
        local, remote = [], []
        for t in range(n):
            for l in range(layers):
                src = ins[t].at[l]
                if t in halved:
                    r2 = ins[t].shape[2] // 2
                    src = ins[t].at[l, :, pl.ds(c * r2, r2)]
                local.append((src, outs[t].at[l, pl.ds(me, 1)]))
                for px, py in _other_chips(x, y):
                    remote.append((src, outs[t].at[l, pl.ds(me, 1)], (px, py, c),
                                   outs[t].at[l, pl.ds(2 * px + py, 1)]))
        return local, remote

    outs = []
    for t, a in enumerate(arrs):
        tail = (a.shape[2] // 2,) + a.shape[3:] if t in halved else a.shape[2:]
        outs.append(jax.ShapeDtypeStruct((a.shape[0], N_CHIPS) + tail, a.dtype))
    return _exchange("allgather_chips", arrs, outs, n * layers, 3 * n * layers, plan)


def _pair_exchange(arrs):
    n = len(arrs)
    layers, shards = arrs[0].shape[:2]

    def plan(ins, outs, x, y, c):
        local, remote = [], []
        for t in range(n):
            r2 = ins[t].shape[2] // 2
            for l in range(layers):
                for s in range(shards):
                    local.append((ins[t].at[l, s, pl.ds(c * r2, r2)], outs[2 * t].at[l, s]))
                    remote.append((ins[t].at[l, s, pl.ds((1 - c) * r2, r2)], outs[2 * t + 1].at[l, s],
                                   (x, y, 1 - c), outs[2 * t + 1].at[l, s]))
        return local, remote

    outs = []
    for a in arrs:
        half = jax.ShapeDtypeStruct(a.shape[:2] + (a.shape[2] // 2, a.shape[3]), a.dtype)
        outs += [half, half]
    return _exchange("pair_exchange", arrs, outs, n * layers * shards, n * layers * shards, plan)


def _chip_scatter(arrs):
    n = len(arrs)
    layers = arrs[0].shape[0]

    def plan(ins, outs, x, y, c):
        me = 2 * x + y
        local, remote = [], []
        for t in range(n):
            for l in range(layers):
                local.append((ins[t].at[l, pl.ds(me, 1)], outs[2 * t].at[l]))
                for j, (px, py) in enumerate(_other_chips(x, y)):
                    remote.append((ins[t].at[l, pl.ds(2 * px + py, 1)], outs[2 * t + 1].at[j, l], (px, py, c),
                                   outs[2 * t + 1].at[j, l]))
        return local, remote

    outs = []
    for a in arrs:
        one = (a.shape[0], 1) + a.shape[2:]
        outs += [jax.ShapeDtypeStruct(one, a.dtype), jax.ShapeDtypeStruct((3,) + one, a.dtype)]
    return _exchange("chip_scatter", arrs, outs, n * layers, 3 * n * layers, plan)


def _pair_gather(arrs):
    n = len(arrs)
    layers = arrs[0].shape[0]

    def plan(ins, outs, x, y, c):
        local, remote = [], []
        for t in range(n):
            for l in range(layers):
                local.append((ins[t].at[l], outs[t].at[l, pl.ds(c, 1)]))
                remote.append((ins[t].at[l], outs[t].at[l, pl.ds(c, 1)], (x, y, 1 - c),
                               outs[t].at[l, pl.ds(1 - c, 1)]))
        return local, remote

    outs = [jax.ShapeDtypeStruct((a.shape[0], 2) + a.shape[2:], a.dtype) for a in arrs]
    return _exchange("pair_gather", arrs, outs, n * layers, n * layers, plan)


GATHER_PIECES = 4


def _allgather_devices(v):
    rq = v.shape[1] // GATHER_PIECES

    def plan(ins, outs, x, y, c):
        me = 4 * x + 2 * y + c
        local, remote = [], []
        for q in range(GATHER_PIECES):
            rows = pl.ds(q * rq, rq)
            local.append((ins[0].at[0, rows], outs[0].at[me, rows]))
            for fx, fy, fc in _FLIPS:
                px, py, pc = (1 - x) if fx else x, (1 - y) if fy else y, (1 - c) if fc else c
                remote.append((ins[0].at[0, rows], outs[0].at[me, rows], (px, py, pc),
                               outs[0].at[4 * px + 2 * py + pc, rows]))
        return local, remote

    out = jax.ShapeDtypeStruct((N_DEV,) + v.shape[1:], v.dtype)
    return _exchange("allgather_devices", [v], [out], GATHER_PIECES, GATHER_PIECES * len(_FLIPS), plan)[0]


WEIGHTS = ['w_in', 'b_in', 'attn_sinks', 's5_a_re', 's5_a_im', 's5_b_re', 's5_b_im', 's5_c_re', 's5_c_im', 's5_d',
           's5_log_dt', 's5_glu_w', 's5_glu_b', 'lru_conv_w', 'lru_conv_b', 'lru_wx', 'lru_bx', 'lru_wa', 'lru_ba',
           'lru_a_param', 'mix_norm_g', 'w_out', 'b_out', 'ln1_g', 'ln1_b', 'ffn_w_gate', 'ffn_w_up', 'ffn_conv_w',
           'ffn_conv_b', 'ffn_w_down', 'ln2_g', 'ln2_b']
BIG = ('w_in', 'w_out', 'ffn_w_gate', 'ffn_w_up', 'ffn_w_down')
SMALL = tuple(n for n in WEIGHTS if n not in BIG)
PACK_ROWS = ROW_TILE


def _pack(arrs):
    flat = jnp.concatenate([a.reshape(-1) for a in arrs])
    unit = 128 * PACK_ROWS
    size = -(-flat.shape[0] // unit) * unit
    return jnp.pad(flat, (0, size - flat.shape[0])).reshape(-1, 128)


def _unpack(packed, shapes):
    flat = packed.reshape(-1)
    out, pos = [], 0
    for shp in shapes:
        n = math.prod(shp)
        out.append(flat[pos:pos + n].reshape(shp))
        pos += n
    return out


def _pair_reduce(name, g):
    layers, shards, rows, cols = g.shape
    r2 = rows // 2
    rt = _pick_rows(r2)
    nr = r2 // rt
    nsteps = layers * shards * nr

    def kern(c_ref, mine_ref, other_ref, o_ref, buf, send, recv, credit):
        x, y, c = _position()
        sibling = (x, y, 1 - c)
        k = pl.program_id(0) * nr + pl.program_id(1)
        slot = k % 2

        @pl.when(k >= 2)
        def _():
            pl.semaphore_wait(credit, 1)

        cp = pltpu.make_async_remote_copy(src_ref=other_ref, dst_ref=buf.at[slot], send_sem=send.at[slot],
                                          recv_sem=recv.at[slot], device_id=sibling, device_id_type=MESH)
        cp.start()
        cp.wait_recv()
        o_ref[...] = (mine_ref[...] + buf[slot]).astype(BF16)
        cp.wait_send()

        @pl.when(k + 2 < nsteps)
        def _():
            pl.semaphore_signal(credit, 1, device_id=sibling, device_id_type=MESH)

    blk = (1, rt, cols)
    grid_spec = pltpu.PrefetchScalarGridSpec(
        num_scalar_prefetch=1, grid=(layers * shards, nr),
        in_specs=[pl.BlockSpec(blk, lambda m, r, c_ref: (m, c_ref[0] * nr + r, 0)),
                  pl.BlockSpec(blk, lambda m, r, c_ref: (m, (1 - c_ref[0]) * nr + r, 0))],
        out_specs=pl.BlockSpec(blk, lambda m, r, c_ref: (m, r, 0)),
        scratch_shapes=[pltpu.VMEM((2,) + blk, F32), pltpu.SemaphoreType.DMA((2,)),
                        pltpu.SemaphoreType.DMA((2,)), pltpu.SemaphoreType.REGULAR])
    core = lax.axis_index("c").astype(jnp.int32).reshape(1)
    g3 = g.reshape(layers * shards, rows, cols)
    out = pl.pallas_call(
        kern, name=name, grid_spec=grid_spec,
        out_shape=jax.ShapeDtypeStruct((layers * shards, r2, cols), BF16),
        compiler_params=_params(("arbitrary", "arbitrary")),
    )(core, g3, g3)
    return out.reshape(layers, shards, r2, cols)


def _pair_merge(name, h):
    m, r2, cols = h.shape
    rt = _pick_rows(r2)
    nr = r2 // rt
    nsteps = m * nr

    def kern(h_ref, o_ref, buf, send, recv, credit):
        x, y, c = _position()
        sibling = (x, y, 1 - c)
        k = pl.program_id(0) * nr + pl.program_id(1)
        slot = k % 2

        @pl.when(k >= 2)
        def _():
            pl.semaphore_wait(credit, 1)

        cp = pltpu.make_async_remote_copy(src_ref=h_ref, dst_ref=buf.at[slot], send_sem=send.at[slot],
                                          recv_sem=recv.at[slot], device_id=sibling, device_id_type=MESH)
        cp.start()
        cp.wait_recv()
        o_ref[0, pl.ds(c, 1)] = h_ref[...]
        o_ref[0, pl.ds(1 - c, 1)] = buf[slot]
        cp.wait_send()

        @pl.when(k + 2 < nsteps)
        def _():
            pl.semaphore_signal(credit, 1, device_id=sibling, device_id_type=MESH)

    blk = (1, rt, cols)
    out = pl.pallas_call(
        kern, name=name, grid=(m, nr),
        in_specs=[pl.BlockSpec(blk, lambda i, r: (i, r, 0))],
        out_specs=pl.BlockSpec((1, 2, rt, cols), lambda i, r: (i, 0, r, 0)),
        out_shape=jax.ShapeDtypeStruct((m, 2, r2, cols), h.dtype),
        scratch_shapes=[pltpu.VMEM((2,) + blk, h.dtype), pltpu.SemaphoreType.DMA((2,)),
                        pltpu.SemaphoreType.DMA((2,)), pltpu.SemaphoreType.REGULAR],
        compiler_params=_params(("arbitrary", "arbitrary")),
    )(h)
    return out.reshape(m, 2 * r2, cols)


def _reduce_big(grads):
    pair = [_pair_reduce("pair_reduce_" + n, g) for n, g in zip(BIG, grads)]
    scat = _chip_scatter(pair)
    out = []
    for t, n in enumerate(BIG):
        own, got = scat[2 * t], scat[2 * t + 1]
        half = _sum_parts("chip_sum", [(own, None)] + [(got, j) for j in range(3)], own.shape)
        out.append(_pair_merge("grad_merge_" + n, half.reshape(half.shape[0], half.shape[2], half.shape[3])))
    return out


def _step(a):
    x = a['x'][0]
    target = a['loss_target'][0]
    t = x.shape[0]
    xi, yi, _ = _position()
    chip = 2 * xi + yi
    cos, sin_s = _rope_tables(t)

    gathered = _allgather_chips([_cast_bf16(a[n])[:, None] for n in BIG]
                                + [a[n][:, None] for n in ('s5_glu_w', 'lru_conv_w', 'ffn_conv_w')],
                                halved=range(len(BIG)))
    full = dict(zip(BIG + ('s5_glu_w', 'lru_conv_w', 'ffn_conv_w'), gathered))
    for n in BIG:
        layers, chips, r2, cols = full[n].shape
        full[n] = _pair_merge("weight_merge_" + n, full[n].reshape(layers * chips, r2, cols)).reshape(
            layers, chips, 2 * r2, cols)

    def layer_params(l):
        p = {n: a[n][l] for n in SMALL}
        p['layer'] = l
        p['w_in'] = full['w_in']
        p['w_out'] = full['w_out'].reshape(DEPTH, D_MODEL, D_MODEL)
        p['ffn_w_gate'] = full['ffn_w_gate']
        p['ffn_w_up'] = full['ffn_w_up']
        p['ffn_w_down'] = full['ffn_w_down']
        p['s5_glu_w'] = full['s5_glu_w'][l].reshape(D_S5, D_S5)
        p['lru_conv_w'] = full['lru_conv_w'][l].transpose(1, 0, 2).reshape(LRU_CONV, D_LRU)
        p['ffn_conv_w'] = full['ffn_conv_w'][l]
        p['ffn_conv_b'] = a['ffn_conv_b'][l].reshape(N_CHIPS, 1, FF_SH)
        return p

    params = [layer_params(l) for l in range(DEPTH)]
    derived = [_layer_weights(p) for p in params]
    saved = []
    h, hb = x, _cast_bf16(x)
    for l in range(DEPTH):
        h, hb, s = _layer_fwd(h, hb, params[l], derived[l], cos, sin_s)
        saved.append(s)
    loss_part, dr, drb, ln2_g, ln2_b, _ = _loss_head(h, target, saved[-1]['xhat2'], saved[-1]['rstd2'],
                                                     _vec(params[-1]['ln2_g']))
    loss = lax.psum(loss_part[0, 0], ("x", "y", "c"))
    grads = [None] * DEPTH
    big = {n: lax.empty((DEPTH, N_CHIPS) + a[n].shape[1:], F32) for n in BIG}
    big['w_out'] = big['w_out'].reshape(DEPTH, D_MODEL, D_MODEL)
    for l in reversed(range(DEPTH)):
        below = (saved[l - 1], params[l - 1]) if l > 0 else None
        out, grads[l], big = _layer_bwd(dr, drb, saved[l], params[l], derived[l], cos, sin_s, big, below)
        grads[l]['ln2_g'], grads[l]['ln2_b'] = ln2_g, ln2_b
        if l > 0:
            dr, drb, ln2_g, ln2_b, _ = out
        else:
            grad_x = out[None]

    def stacked(n):
        return jnp.stack([grads[l][n] for l in range(DEPTH)])

    big['w_out'] = big['w_out'].reshape(DEPTH, N_CHIPS, OUT_SH, D_MODEL)
    grad = dict(zip(BIG, _reduce_big([big[n] for n in BIG])))
    small_local = [stacked(n) for n in SMALL]
    packed = _allgather_devices(_pack(small_local)[None])
    total = _sum_parts("device_sum", [(packed, j) for j in range(N_DEV)], packed.shape[1:])
    small_sum = dict(zip(SMALL, _unpack(total, [g.shape for g in small_local])))
    for n in SMALL:
        g = small_sum[n]
        if n == 's5_glu_w':
            g = lax.dynamic_slice_in_dim(g, chip * (D_S5 // N_CHIPS), D_S5 // N_CHIPS, axis=1)
        elif n == 'lru_conv_w':
            g = lax.dynamic_slice_in_dim(g, chip * (D_LRU // N_CHIPS), D_LRU // N_CHIPS, axis=2)
        elif n == 'ffn_conv_w':
            g = lax.dynamic_index_in_dim(g, chip, axis=1, keepdims=False)
        grad[n] = g.reshape(a[n].shape)

    delta, new_m, new_v = {}, {}, {}
    for n in BIG:
        delta[n], new_m[n], new_v[n] = _adamw("adamw_" + n, a[n], grad[n], a['m_' + n], a['v_' + n])
    shapes = [a[n].shape for n in SMALL]
    outs = _adamw("adamw_small", _pack([a[n] for n in SMALL]), _pack([grad[n] for n in SMALL]),
                  _pack([a['m_' + n] for n in SMALL]), _pack([a['v_' + n] for n in SMALL]))
    for res, o in zip((delta, new_m, new_v), outs):
        res.update(zip(SMALL, _unpack(o, shapes)))
    return (loss, grad_x, *[grad[n] for n in WEIGHTS], *[delta[n] for n in WEIGHTS],
            *[new_m[n] for n in WEIGHTS], *[new_v[n] for n in WEIGHTS])


def kernel(x, w_in, b_in, attn_sinks, s5_a_re, s5_a_im, s5_b_re, s5_b_im, s5_c_re, s5_c_im, s5_d, s5_log_dt, s5_glu_w, s5_glu_b, lru_conv_w, lru_conv_b, lru_wx, lru_bx, lru_wa, lru_ba, lru_a_param, mix_norm_g, w_out, b_out, ln1_g, ln1_b, ffn_w_gate, ffn_w_up, ffn_conv_w, ffn_conv_b, ffn_w_down, ln2_g, ln2_b, loss_target, m_w_in, m_b_in, m_attn_sinks, m_s5_a_re, m_s5_a_im, m_s5_b_re, m_s5_b_im, m_s5_c_re, m_s5_c_im, m_s5_d, m_s5_log_dt, m_s5_glu_w, m_s5_glu_b, m_lru_conv_w, m_lru_conv_b, m_lru_wx, m_lru_bx, m_lru_wa, m_lru_ba, m_lru_a_param, m_mix_norm_g, m_w_out, m_b_out, m_ln1_g, m_ln1_b, m_ffn_w_gate, m_ffn_w_up, m_ffn_conv_w, m_ffn_conv_b, m_ffn_w_down, m_ln2_g, m_ln2_b, v_w_in, v_b_in, v_attn_sinks, v_s5_a_re, v_s5_a_im, v_s5_b_re, v_s5_b_im, v_s5_c_re, v_s5_c_im, v_s5_d, v_s5_log_dt, v_s5_glu_w, v_s5_glu_b, v_lru_conv_w, v_lru_conv_b, v_lru_wx, v_lru_bx, v_lru_wa, v_lru_ba, v_lru_a_param, v_mix_norm_g, v_w_out, v_b_out, v_ln1_g, v_ln1_b, v_ffn_w_gate, v_ffn_w_up, v_ffn_conv_w, v_ffn_conv_b, v_ffn_w_down, v_ln2_g, v_ln2_b):
    return _step(dict(locals()))
```

```python
import functools
import math

import jax
import jax.numpy as jnp
from jax import lax
from jax.experimental import pallas as pl
from jax.experimental.pallas import tpu as pltpu

F32 = jnp.float32
BF16 = jnp.bfloat16
MESH = pl.DeviceIdType.MESH
ANY = pl.BlockSpec(memory_space=pl.ANY)

D_MODEL = 1024
DEPTH = 4
HEAD_DIM = 64
N_Q_HEADS = 8
N_KV_HEADS = 2
Q_PER_KV = 4
D_ATTN = 512
D_KV = 128
ATTN_BLOCK = 128
ROPE_THETA = 10000.0
D_S5 = 256
S5_GROUP = 16
S5_GROUPS = 16
S5_STATE = 64
N_STATE = S5_GROUPS * S5_STATE
D_LRU = 256
LRU_HEADS = 4
LRU_HEAD_DIM = 64
LRU_CONV = 4
LRU_C = 8.0
D_IN = 1536
D_FF = 2816
FFN_CONV = 3
N_CHIPS = 4
N_DEV = 8
IN_SH = D_IN // N_CHIPS
FF_SH = D_FF // N_CHIPS
OUT_SH = D_MODEL // N_CHIPS
ALPHA = (2 * DEPTH) ** 0.25
LN_EPS = 1e-5
RMS_EPS = 1e-6
ADAM_LR = 0.001
ADAM_B1 = 0.9
ADAM_B2 = 0.999
ADAM_EPS = 1e-08
ADAM_WD = 0.01
ADAM_STEP = 10

SUBLANES = 8
VMEM_MB = 56


def _params(sem):
    return pltpu.CompilerParams(dimension_semantics=sem, vmem_limit_bytes=VMEM_MB << 20)


def _row_tile(t, pref):
    return min(t, pref)


def _matmul(name, a, b, *, a_blk, a_map, b_blk, b_map, out_shape, o_blk, o_map, grid, dims,
            out_dtype=F32, bias=None, bias_blk=None, bias_map=None, add=None, add_scale=1.0, pair2=None,
            into=None, ln_bwd=None):
    nk = grid[2]
    acc_shape = tuple(d for d in o_blk if d is not None)
    n_in = 2 if pair2 is None else 4

    def kern(*refs):
        p = n_in
        bias_ref = add_ref = None
        if bias is not None:
            bias_ref = refs[p]
            p += 1
        if add is not None:
            add_ref = refs[p]
            p += 1
        if into is not None:
            p += 1
        if ln_bwd is not None:
            ln_in = refs[p:p + 3]
            ln_out = refs[p + 4:p + 8]
            o_ref, acc = refs[p + 3], refs[p + 8]
        else:
            o_ref, acc = refs[p], refs[p + 1]
        k = pl.program_id(2)
        first_tile = pl.program_id(0) == 0

        def product():
            r = _dot(refs[0][...].astype(BF16), refs[1][...].astype(BF16), dims)
            if pair2 is not None:
                r = r + _dot(refs[2][...].astype(BF16), refs[3][...].astype(BF16), dims)
            return r

        def finish(r):
            if bias_ref is not None:
                r = r + bias_ref[...]
            if add_ref is not None:
                r = r + add_scale * add_ref[...]
            if ln_bwd is None:
                o_ref[...] = r.astype(out_dtype)
            else:
                @pl.when(first_tile)
                def _():
                    for ref in ln_out[1:]:
                        ref[...] = jnp.zeros_like(ref)

                _ln_bwd_tile(r, ln_in[0][...], ln_in[1][...], ln_in[2][...], o_ref, *ln_out)

        if nk == 1:
            finish(product())
        else:
            @pl.when(k == 0)
            def _():
                acc[...] = jnp.zeros_like(acc)

            acc[...] += product()

            @pl.when(k == nk - 1)
            def _():
                finish(acc[...])

    in_specs = [pl.BlockSpec(a_blk, a_map), pl.BlockSpec(b_blk, b_map)]
    args = [a, b]
    if pair2 is not None:
        in_specs += [pl.BlockSpec(a_blk, a_map), pl.BlockSpec(b_blk, b_map)]
        args += list(pair2)
    if bias is not None:
        in_specs.append(pl.BlockSpec(bias_blk, bias_map))
        args.append(bias)
    if add is not None:
        in_specs.append(pl.BlockSpec(o_blk, lambda i, j, k: o_map(i, j)))
        args.append(add)
    aliases = {}
    if into is not None:
        aliases = {len(args): 0}
        in_specs.append(ANY)
        args.append(into)
    o_spec = pl.BlockSpec(o_blk, lambda i, j, k: o_map(i, j))
    out_specs, out_shapes = o_spec, jax.ShapeDtypeStruct(out_shape, out_dtype)
    semantics = ("parallel", "parallel", "arbitrary")
    if ln_bwd is not None:
        vec = pl.BlockSpec((1, o_blk[-1]), lambda i, j, k: (0, 0))
        in_specs += [o_spec, pl.BlockSpec((o_blk[0], 1), lambda i, j, k: (i, 0)), vec]
        args += list(ln_bwd)
        vshape = jax.ShapeDtypeStruct((1, o_blk[-1]), F32)
        out_specs = [o_spec, o_spec, vec, vec, vec]
        out_shapes = [out_shapes, jax.ShapeDtypeStruct(out_shape, BF16), vshape, vshape, vshape]
        semantics = ("arbitrary", "arbitrary", "arbitrary")
    return pl.pallas_call(
        kern, name=name, grid=grid, in_specs=in_specs, out_specs=out_specs, out_shape=out_shapes,
        scratch_shapes=[pltpu.VMEM(acc_shape if nk > 1 else (SUBLANES, 128), F32)],
        input_output_aliases=aliases,
        compiler_params=_params(semantics),
    )(*args)


NN = ((1,), (0,))
NT = ((1,), (1,))
TN = ((0,), (0,))
TM = 512


def _sigmoid(x):
    return 0.5 * jnp.tanh(0.5 * x) + 0.5


_GELU_C = math.sqrt(2.0 / math.pi)


def _gelu(x):
    return 0.5 * x * (1.0 + jnp.tanh(_GELU_C * (x + 0.044715 * x * x * x)))


def _gelu_grad(x):
    th = jnp.tanh(_GELU_C * (x + 0.044715 * x * x * x))
    return 0.5 * (1.0 + th) + 0.5 * x * (1.0 - th * th) * _GELU_C * (1.0 + 3 * 0.044715 * x * x)


def _rope_swap(t):
    lane = lax.broadcasted_iota(jnp.int32, t.shape, 1)
    lo = (lane % HEAD_DIM) < (HEAD_DIM // 2)
    return jnp.where(lo, pltpu.roll(t, 128 - HEAD_DIM // 2, 1), pltpu.roll(t, HEAD_DIM // 2, 1))


D_QKV = D_ATTN + 2 * D_KV
TMM = 1024


def _in_proj(xb, w_in, b_in, cos, sin_s, layer):
    t = xb.shape[0]
    tm = _row_tile(t, TMM)

    def kern(x_ref, w_ref, b_ref, c_ref, s_ref, q_ref, u_ref):
        x = x_ref[...]
        c = c_ref[...]
        s = s_ref[...]
        for j in range(N_CHIPS):
            pj = _dot(x, w_ref[j], NN) + b_ref[:, j * IN_SH:(j + 1) * IN_SH]
            for ch in range(IN_SH // 128):
                col = j * IN_SH + ch * 128
                v = pj[:, ch * 128:(ch + 1) * 128]
                if col < D_ATTN + D_KV:
                    v = v * c + _rope_swap(v) * s
                if col < D_ATTN:
                    v = v * (HEAD_DIM ** -0.5)
                if col < D_QKV:
                    q_ref[:, col:col + 128] = v.astype(BF16)
                else:
                    u_ref[:, col - D_QKV:col - D_QKV + 128] = v

    row = lambda w: pl.BlockSpec((tm, w), lambda i: (i, 0))
    return pl.pallas_call(
        kern, name="in_proj", grid=(t // tm,),
        in_specs=[row(D_MODEL), pl.BlockSpec((None, N_CHIPS, D_MODEL, IN_SH), lambda i: (layer, 0, 0, 0)),
                  pl.BlockSpec((1, D_IN), lambda i: (0, 0)), row(128), row(128)],
        out_specs=[row(D_QKV), row(D_IN - D_QKV)],
        out_shape=[jax.ShapeDtypeStruct((t, D_QKV), BF16), jax.ShapeDtypeStruct((t, D_IN - D_QKV), F32)],
        compiler_params=_params(("parallel",)),
    )(xb, w_in, b_in, cos, sin_s)


def _attn_mask(i):
    qi = lax.broadcasted_iota(jnp.int32, (ATTN_BLOCK, 2 * ATTN_BLOCK), 0)
    si = lax.broadcasted_iota(jnp.int32, (ATTN_BLOCK, 2 * ATTN_BLOCK), 1)
    diff = qi + ATTN_BLOCK - si
    return (diff >= 0) & (diff < ATTN_BLOCK) & ((si >= ATTN_BLOCK) | (i > 0))


def _row_sums(x, ones):
    hi = x.astype(BF16)
    lo = (x - hi.astype(F32)).astype(BF16)
    return _dot(hi, ones, NN) + _dot(lo, ones, NN)


def _attn_fwd(qkv, sinks):
    t = qkv.shape[0]
    nb = t // ATTN_BLOCK

    def kern(q_ref, kp_ref, kc_ref, vp_ref, vc_ref, s_ref, o_ref, l_ref):
        i = pl.program_id(0)
        valid = _attn_mask(i)
        ones = jnp.ones((2 * ATTN_BLOCK, 128), BF16)
        kband = jnp.concatenate([kp_ref[...], kc_ref[...]], axis=0)
        vband = jnp.concatenate([vp_ref[...], vc_ref[...]], axis=0)
        ks = [kband[:, kh * HEAD_DIM:(kh + 1) * HEAD_DIM] for kh in range(N_KV_HEADS)]
        vs = [vband[:, kh * HEAD_DIM:(kh + 1) * HEAD_DIM] for kh in range(N_KV_HEADS)]
        scores = [_dot(q_ref[:, h * HEAD_DIM:(h + 1) * HEAD_DIM], ks[h // Q_PER_KV], NT) for h in range(N_Q_HEADS)]
        probs, lses = [], []
        for h in range(N_Q_HEADS):
            s = jnp.where(valid, scores[h], -jnp.inf)
            sink = s_ref[0:1, h:h + 1]
            m = jnp.maximum(jnp.max(s, axis=-1, keepdims=True), sink)
            e = jnp.exp(s - m)
            denom = _row_sums(e, ones)[:, 0:1] + jnp.exp(sink - m)
            probs.append((e * (1.0 / denom)).astype(BF16))
            lses.append(m + jnp.log(denom))
        outs = [_dot(probs[h], vs[h // Q_PER_KV], NN) for h in range(N_Q_HEADS)]
        for h in range(N_Q_HEADS):
            o_ref[:, h * HEAD_DIM:(h + 1) * HEAD_DIM] = outs[h]
            l_ref[:, h:h + 1] = lses[h]

    blk = lambda w, f: pl.BlockSpec((ATTN_BLOCK, w), f)
    return pl.pallas_call(
        kern, name="attn_fwd", grid=(nb,),
        in_specs=[blk(512, lambda i: (i, 0)),
                  blk(128, lambda i: (jnp.maximum(i - 1, 0), 4)), blk(128, lambda i: (i, 4)),
                  blk(128, lambda i: (jnp.maximum(i - 1, 0), 5)), blk(128, lambda i: (i, 5)),
                  pl.BlockSpec((1, N_Q_HEADS), lambda i: (0, 0))],
        out_specs=[blk(512, lambda i: (i, 0)), blk(N_Q_HEADS, lambda i: (i, 0))],
        out_shape=[jax.ShapeDtypeStruct((t, D_ATTN), F32), jax.ShapeDtypeStruct((t, N_Q_HEADS), F32)],
        compiler_params=_params(("parallel",)),
    )(qkv, qkv, qkv, qkv, qkv, sinks)


def _attn_bwd(qkv, o, do, lse, sinks):
    t = qkv.shape[0]
    nb = t // ATTN_BLOCK

    def kern(q_ref, kp_ref, kc_ref, vp_ref, vc_ref, o_ref, do_ref, l_ref, s_ref,
             dq_ref, dk_ref, dv_ref, ds_ref, ck, cv):
        i = pl.program_id(0)

        @pl.when(i == 0)
        def _():
            ds_ref[...] = jnp.zeros_like(ds_ref)
            ck[...] = jnp.zeros_like(ck)
            cv[...] = jnp.zeros_like(cv)

        @pl.when(i < nb)
        def _():
            valid = _attn_mask(i)
            kband = jnp.concatenate([kp_ref[...], kc_ref[...]], axis=0)
            vband = jnp.concatenate([vp_ref[...], vc_ref[...]], axis=0)
            heads = range(N_Q_HEADS)
            sl = [slice(h * HEAD_DIM, (h + 1) * HEAD_DIM) for h in heads]
            ks = [kband[:, kh * HEAD_DIM:(kh + 1) * HEAD_DIM] for kh in range(N_KV_HEADS)]
            vs = [vband[:, kh * HEAD_DIM:(kh + 1) * HEAD_DIM] for kh in range(N_KV_HEADS)]
            qs = [q_ref[:, sl[h]] for h in heads]
            d_os = [do_ref[:, sl[h]] for h in heads]
            dobs = [d.astype(BF16) for d in d_os]
            scores = [_dot(qs[h], ks[h // Q_PER_KV], NT) for h in heads]
            dps = [_dot(dobs[h], vs[h // Q_PER_KV], NT) for h in heads]
            col_head = lax.broadcasted_iota(jnp.int32, (D_ATTN, 128), 0) // HEAD_DIM
            head_ones = (col_head == lax.broadcasted_iota(jnp.int32, (D_ATTN, 128), 1)).astype(BF16)
            deltas = _row_sums(do_ref[...] * o_ref[...], head_ones)
            pbs, dscs = [], []
            for h in heads:
                lse_h = l_ref[:, h:h + 1]
                p = jnp.where(valid, jnp.exp(scores[h] - lse_h), 0.0)
                delta = deltas[:, h:h + 1]
                pbs.append(p.astype(BF16))
                dscs.append((p * (dps[h] - delta)).astype(BF16))
                psink = jnp.exp(s_ref[0:1, h:h + 1] - lse_h)
                ds_ref[0:1, h:h + 1] += -jnp.sum(psink * delta, axis=0, keepdims=True)
            dqs = [_dot(dscs[h], ks[h // Q_PER_KV], NN) for h in heads]
            dkb = [sum(_dot(dscs[h], qs[h], TN) for h in heads if h // Q_PER_KV == kh) for kh in range(N_KV_HEADS)]
            dvb = [sum(_dot(pbs[h], dobs[h], TN) for h in heads if h // Q_PER_KV == kh) for kh in range(N_KV_HEADS)]
            for h in heads:
                dq_ref[:, sl[h]] = dqs[h]
            dk_band = jnp.concatenate(dkb, axis=1)
            dv_band = jnp.concatenate(dvb, axis=1)
            dk_ref[...] = ck[...] + dk_band[:ATTN_BLOCK]
            dv_ref[...] = cv[...] + dv_band[:ATTN_BLOCK]
            ck[...] = dk_band[ATTN_BLOCK:]
            cv[...] = dv_band[ATTN_BLOCK:]

        @pl.when(i == nb)
        def _():
            dk_ref[...] = ck[...]
            dv_ref[...] = cv[...]

    blk = lambda w, f: pl.BlockSpec((ATTN_BLOCK, w), f)
    cur = lambda i: jnp.minimum(i, nb - 1)
    prev = lambda i: jnp.clip(i - 1, 0, nb - 1)
    return pl.pallas_call(
        kern, name="attn_bwd", grid=(nb + 1,),
        in_specs=[blk(512, lambda i: (cur(i), 0)),
                  blk(128, lambda i: (prev(i), 4)), blk(128, lambda i: (cur(i), 4)),
                  blk(128, lambda i: (prev(i), 5)), blk(128, lambda i: (cur(i), 5)),
                  blk(512, lambda i: (cur(i), 0)), blk(512, lambda i: (cur(i), 0)),
                  blk(N_Q_HEADS, lambda i: (cur(i), 0)),
                  pl.BlockSpec((1, N_Q_HEADS), lambda i: (0, 0))],
        out_specs=[blk(512, lambda i: (cur(i), 0)), blk(128, lambda i: (prev(i), 0)),
                   blk(128, lambda i: (prev(i), 0)), pl.BlockSpec((1, N_Q_HEADS), lambda i: (0, 0))],
        out_shape=[jax.ShapeDtypeStruct((t, D_ATTN), F32), jax.ShapeDtypeStruct((t, D_KV), F32),
                   jax.ShapeDtypeStruct((t, D_KV), F32), jax.ShapeDtypeStruct((1, N_Q_HEADS), F32)],
        scratch_shapes=[pltpu.VMEM((ATTN_BLOCK, D_KV), F32), pltpu.VMEM((ATTN_BLOCK, D_KV), F32)],
        compiler_params=_params(("arbitrary",)),
    )(qkv, qkv, qkv, qkv, qkv, o, do, lse, sinks)


_GROUPS = ((0, D_ATTN), (D_ATTN, D_ATTN + D_S5), (D_ATTN + D_S5, D_MODEL))


def _rms_fwd(ya, ys, yl, g):
    t = ya.shape[0]
    tm = _row_tile(t, TM)

    def kern(a_ref, s_ref, l_ref, g_ref, o_ref):
        for (lo, hi), ref in zip(_GROUPS, (a_ref, s_ref, l_ref)):
            y = ref[...]
            n = y * lax.rsqrt(jnp.mean(y * y, axis=-1, keepdims=True) + RMS_EPS)
            o_ref[:, lo:hi] = (n * g_ref[:, lo:hi]).astype(BF16)

    row = lambda w: pl.BlockSpec((tm, w), lambda i: (i, 0))
    return pl.pallas_call(
        kern, name="rms_fwd", grid=(t // tm,),
        in_specs=[row(D_ATTN), row(D_S5), row(D_LRU), pl.BlockSpec((1, D_MODEL), lambda i: (0, 0))],
        out_specs=row(D_MODEL), out_shape=jax.ShapeDtypeStruct((t, D_MODEL), BF16),
        compiler_params=_params(("parallel",)),
    )(ya, ys, yl, g)


def _rms_bwd(dmix, ya, ys, yl, g):
    t = ya.shape[0]
    tm = _row_tile(t, TM)

    def kern(d_ref, a_ref, s_ref, l_ref, g_ref, da_ref, ds_ref, dl_ref, dg_ref):
        @pl.when(pl.program_id(0) == 0)
        def _():
            dg_ref[...] = jnp.zeros_like(dg_ref)

        for (lo, hi), ref, out in zip(_GROUPS, (a_ref, s_ref, l_ref), (da_ref, ds_ref, dl_ref)):
            y = ref[...]
            rstd = lax.rsqrt(jnp.mean(y * y, axis=-1, keepdims=True) + RMS_EPS)
            n = y * rstd
            dm = d_ref[:, lo:hi]
            dg_ref[:, lo:hi] += jnp.sum(dm * n, axis=0, keepdims=True)
            dn = dm * g_ref[:, lo:hi]
            out[...] = rstd * (dn - n * jnp.mean(dn * n, axis=-1, keepdims=True))

    row = lambda w: pl.BlockSpec((tm, w), lambda i: (i, 0))
    vec = pl.BlockSpec((1, D_MODEL), lambda i: (0, 0))
    return pl.pallas_call(
        kern, name="rms_bwd", grid=(t // tm,),
        in_specs=[row(D_MODEL), row(D_ATTN), row(D_S5), row(D_LRU), vec],
        out_specs=[row(D_ATTN), row(D_S5), row(D_LRU), vec],
        out_shape=[jax.ShapeDtypeStruct((t, D_ATTN), F32), jax.ShapeDtypeStruct((t, D_S5), F32),
                   jax.ShapeDtypeStruct((t, D_LRU), F32), jax.ShapeDtypeStruct((1, D_MODEL), F32)],
        compiler_params=_params(("arbitrary",)),
    )(dmix, ya, ys, yl, g)


def _mix_out_ln(ya, ys, yl, mg, w_out, b_out, xres, g, b, layer):
    t = xres.shape[0]
    tm = _row_tile(t, TM)

    def kern(a_ref, s_ref, l_ref, mg_ref, w_ref, bias_ref, x_ref, g_ref, b_ref, m_ref, y_ref, yb_ref, h_ref, r_ref):
        for (lo, hi), ref in zip(_GROUPS, (a_ref, s_ref, l_ref)):
            v = ref[...]
            n = v * lax.rsqrt(jnp.mean(v * v, axis=-1, keepdims=True) + RMS_EPS)
            m_ref[:, lo:hi] = (n * mg_ref[:, lo:hi]).astype(BF16)
        r = ALPHA * x_ref[...] + _dot(m_ref[...], w_ref[...], NN) + bias_ref[...]
        mu = jnp.mean(r, axis=-1, keepdims=True)
        xc = r - mu
        rstd = lax.rsqrt(jnp.mean(xc * xc, axis=-1, keepdims=True) + LN_EPS)
        xhat = xc * rstd
        h_ref[...] = xhat
        r_ref[...] = rstd
        y = xhat * g_ref[...] + b_ref[...]
        y_ref[...] = y
        yb_ref[...] = y.astype(BF16)

    rowb = lambda w: pl.BlockSpec((tm, w), lambda i: (i, 0))
    row = rowb(D_MODEL)
    vec = pl.BlockSpec((1, D_MODEL), lambda i: (0, 0))
    big = lambda dt: jax.ShapeDtypeStruct((t, D_MODEL), dt)
    return pl.pallas_call(
        kern, name="mix_out_ln", grid=(t // tm,),
        in_specs=[rowb(D_ATTN), rowb(D_S5), rowb(D_LRU), vec,
                  pl.BlockSpec((None, D_MODEL, D_MODEL), lambda i: (layer, 0, 0)), vec, row, vec, vec],
        out_specs=[row, row, row, row, pl.BlockSpec((tm, 1), lambda i: (i, 0))],
        out_shape=[big(BF16), big(F32), big(BF16), big(F32), jax.ShapeDtypeStruct((t, 1), F32)],
        compiler_params=_params(("parallel",)),
    )(ya, ys, yl, mg, w_out, b_out, xres, g, b)


def _d_mix_rms(dx1, xhat, rstd, lg, w_out, ya, ys, yl, mg, layer):
    t = dx1.shape[0]
    tm = _row_tile(t, TM)

    def kern(d_ref, h_ref, r_ref, lg_ref, w_ref, a_ref, s_ref, l_ref, g_ref,
             dr_ref, drb_ref, dlg_ref, dlb_ref, sr_ref, da_ref, ds_ref, dl_ref, dg_ref):
        @pl.when(pl.program_id(0) == 0)
        def _():
            for ref in (dlg_ref, dlb_ref, sr_ref, dg_ref):
                ref[...] = jnp.zeros_like(ref)

        _ln_bwd_tile(d_ref[...], h_ref[...], r_ref[...], lg_ref[...], dr_ref, drb_ref, dlg_ref, dlb_ref, sr_ref)
        dmix = _dot(drb_ref[...], w_ref[...], NT)
        for (lo, hi), ref, out in zip(_GROUPS, (a_ref, s_ref, l_ref), (da_ref, ds_ref, dl_ref)):
            v = ref[...]
            rstd = lax.rsqrt(jnp.mean(v * v, axis=-1, keepdims=True) + RMS_EPS)
            n = v * rstd
            dm = dmix[:, lo:hi]
            dg_ref[:, lo:hi] += jnp.sum(dm * n, axis=0, keepdims=True)
            dn = dm * g_ref[:, lo:hi]
            out[...] = rstd * (dn - n * jnp.mean(dn * n, axis=-1, keepdims=True))

    rowb = lambda w: pl.BlockSpec((tm, w), lambda i: (i, 0))
    vec = pl.BlockSpec((1, D_MODEL), lambda i: (0, 0))
    vshape = jax.ShapeDtypeStruct((1, D_MODEL), F32)
    return pl.pallas_call(
        kern, name="d_mix_rms", grid=(t // tm,),
        in_specs=[rowb(D_MODEL), rowb(D_MODEL), pl.BlockSpec((tm, 1), lambda i: (i, 0)), vec,
                  pl.BlockSpec((None, D_MODEL, D_MODEL), lambda i: (layer, 0, 0)),
                  rowb(D_ATTN), rowb(D_S5), rowb(D_LRU), vec],
        out_specs=[rowb(D_MODEL), rowb(D_MODEL), vec, vec, vec, rowb(D_ATTN), rowb(D_S5), rowb(D_LRU), vec],
        out_shape=[jax.ShapeDtypeStruct((t, D_MODEL), F32), jax.ShapeDtypeStruct((t, D_MODEL), BF16),
                   vshape, vshape, vshape, jax.ShapeDtypeStruct((t, D_ATTN), F32),
                   jax.ShapeDtypeStruct((t, D_S5), F32), jax.ShapeDtypeStruct((t, D_LRU), F32), vshape],
        compiler_params=_params(("arbitrary",)),
    )(dx1, xhat, rstd, lg, w_out, ya, ys, yl, mg)


def _matmul_ln(name, a, w, bias, xres, g, b, a_blk, a_map, w_blk, parts, layer):
    t = xres.shape[0]
    tm = a_blk[-2]

    def kern(a_ref, w_ref, bias_ref, x_ref, g_ref, b_ref, y_ref, yb_ref, h_ref, r_ref):
        if parts is None:
            f = _dot(a_ref[...], w_ref[...], NN)
        else:
            f = sum(_dot(a_ref[j], w_ref[j], NN) for j in range(parts))
        r = ALPHA * x_ref[...] + f + bias_ref[...]
        mu = jnp.mean(r, axis=-1, keepdims=True)
        xc = r - mu
        rstd = lax.rsqrt(jnp.mean(xc * xc, axis=-1, keepdims=True) + LN_EPS)
        xhat = xc * rstd
        h_ref[...] = xhat
        r_ref[...] = rstd
        y = xhat * g_ref[...] + b_ref[...]
        y_ref[...] = y
        yb_ref[...] = y.astype(BF16)

    row = pl.BlockSpec((tm, D_MODEL), lambda i: (i, 0))
    vec = pl.BlockSpec((1, D_MODEL), lambda i: (0, 0))
    big = lambda dt: jax.ShapeDtypeStruct((t, D_MODEL), dt)
    return pl.pallas_call(
        kern, name=name, grid=(t // tm,),
        in_specs=[pl.BlockSpec(a_blk, a_map), pl.BlockSpec((None,) + w_blk, lambda i: (layer,) + (0,) * len(w_blk)), vec, row, vec, vec],
        out_specs=[row, row, row, pl.BlockSpec((tm, 1), lambda i: (i, 0))],
        out_shape=[big(F32), big(BF16), big(F32), jax.ShapeDtypeStruct((t, 1), F32)],
        compiler_params=_params(("parallel",)),
    )(a, w, bias, xres, g, b)


def _ln_bwd(dy, xhat, rstd, g):
    t = dy.shape[0]
    tm = _row_tile(t, TM)

    def kern(d_ref, h_ref, r_ref, g_ref, dr_ref, drb_ref, dg_ref, db_ref, sr_ref):
        @pl.when(pl.program_id(0) == 0)
        def _():
            dg_ref[...] = jnp.zeros_like(dg_ref)
            db_ref[...] = jnp.zeros_like(db_ref)
            sr_ref[...] = jnp.zeros_like(sr_ref)

        d = d_ref[...]
        xhat = h_ref[...]
        dg_ref[...] += jnp.sum(d * xhat, axis=0, keepdims=True)
        db_ref[...] += jnp.sum(d, axis=0, keepdims=True)
        dh = d * g_ref[...]
        dr = r_ref[...] * (dh - jnp.mean(dh, axis=-1, keepdims=True)
                           - xhat * jnp.mean(dh * xhat, axis=-1, keepdims=True))
        dr_ref[...] = dr
        drb_ref[...] = dr.astype(BF16)
        sr_ref[...] += jnp.sum(dr, axis=0, keepdims=True)

    row = pl.BlockSpec((tm, D_MODEL), lambda i: (i, 0))
    vec = pl.BlockSpec((1, D_MODEL), lambda i: (0, 0))
    vshape = jax.ShapeDtypeStruct((1, D_MODEL), F32)
    return pl.pallas_call(
        kern, name="ln_bwd", grid=(t // tm,),
        in_specs=[row, row, pl.BlockSpec((tm, 1), lambda i: (i, 0)), vec],
        out_specs=[row, row, vec, vec, vec],
        out_shape=[jax.ShapeDtypeStruct((t, D_MODEL), F32), jax.ShapeDtypeStruct((t, D_MODEL), BF16),
                   vshape, vshape, vshape],
        compiler_params=_params(("arbitrary",)),
    )(dy, xhat, rstd, g)


def _ln_bwd_tile(d, xhat, rstd, g, dr_ref, drb_ref, dg_ref, db_ref, sr_ref):
    dg_ref[...] += jnp.sum(d * xhat, axis=0, keepdims=True)
    db_ref[...] += jnp.sum(d, axis=0, keepdims=True)
    dh = d * g
    dr = rstd * (dh - jnp.mean(dh, axis=-1, keepdims=True) - xhat * jnp.mean(dh * xhat, axis=-1, keepdims=True))
    dr_ref[...] = dr
    drb_ref[...] = dr.astype(BF16)
    sr_ref[...] += jnp.sum(dr, axis=0, keepdims=True)


def _loss_head(y, target, xhat, rstd, g):
    t = y.shape[0]
    tm = _row_tile(t, TM)

    def kern(y_ref, t_ref, h_ref, r_ref, g_ref, l_ref, dr_ref, drb_ref, dg_ref, db_ref, sr_ref):
        @pl.when(pl.program_id(0) == 0)
        def _():
            for ref in (l_ref, dg_ref, db_ref, sr_ref):
                ref[...] = jnp.zeros_like(ref)

        err = y_ref[...] - t_ref[...]
        part = jnp.sum(jnp.sum(err * err, axis=-1, keepdims=True), axis=0, keepdims=True)
        l_ref[...] += jnp.broadcast_to(part * (0.5 / D_MODEL), l_ref.shape)
        _ln_bwd_tile(err * (1.0 / D_MODEL), h_ref[...], r_ref[...], g_ref[...], dr_ref, drb_ref, dg_ref, db_ref, sr_ref)

    row = pl.BlockSpec((tm, D_MODEL), lambda i: (i, 0))
    vec = pl.BlockSpec((1, D_MODEL), lambda i: (0, 0))
    vshape = jax.ShapeDtypeStruct((1, D_MODEL), F32)
    return pl.pallas_call(
        kern, name="loss_head", grid=(t // tm,),
        in_specs=[row, row, row, pl.BlockSpec((tm, 1), lambda i: (i, 0)), vec],
        out_specs=[pl.BlockSpec((1, 128), lambda i: (0, 0)), row, row, vec, vec, vec],
        out_shape=[jax.ShapeDtypeStruct((1, 128), F32), jax.ShapeDtypeStruct((t, D_MODEL), F32),
                   jax.ShapeDtypeStruct((t, D_MODEL), BF16), vshape, vshape, vshape],
        compiler_params=_params(("arbitrary",)),
    )(y, target, xhat, rstd, g)


HALO = 8


def _ffn_mid_specs(t, tm):
    main = pl.BlockSpec((None, tm, FF_SH), lambda j, i: (j, i, 0))
    prev = pl.BlockSpec((None, HALO, FF_SH), lambda j, i: (j, jnp.maximum(i * (tm // HALO) - 1, 0), 0))
    cw = pl.BlockSpec((None, FFN_CONV, FF_SH), lambda j, i: (j, 0, 0))
    cb = pl.BlockSpec((None, 1, FF_SH), lambda j, i: (j, 0, 0))
    return main, prev, cw, cb


def _ffn_conv(ext, g_ref, p_ref, w_ref, b_ref, tm):
    i = pl.program_id(1)
    ext[0:HALO, :] = jnp.where(i > 0, p_ref[...], 0.0)
    ext[HALO:, :] = g_ref[...]
    taps = [ext[pl.ds(HALO - (FFN_CONV - 1) + k, tm), :] for k in range(FFN_CONV)]
    gc = b_ref[...] + sum(w_ref[k:k + 1, :] * taps[k] for k in range(FFN_CONV))
    return gc, taps


def _ffn_mid_fwd(gpre, up, cw, cb):
    t = gpre.shape[1]
    tm = _row_tile(t, TM)

    def kern(g_ref, p_ref, u_ref, w_ref, b_ref, o_ref, ext):
        gc, _ = _ffn_conv(ext, g_ref, p_ref, w_ref, b_ref, tm)
        o_ref[...] = (gc * _sigmoid(gc) * u_ref[...]).astype(BF16)

    main, prev, cws, cbs = _ffn_mid_specs(t, tm)
    return pl.pallas_call(
        kern, name="ffn_mid_fwd", grid=(N_CHIPS, t // tm),
        in_specs=[main, prev, main, cws, cbs], out_specs=main,
        out_shape=jax.ShapeDtypeStruct((N_CHIPS, t, FF_SH), BF16),
        scratch_shapes=[pltpu.VMEM((tm + HALO, FF_SH), F32)],
        compiler_params=_params(("parallel", "parallel")),
    )(gpre, gpre, up, cw, cb)


def _ffn_mid_bwd(gpre, up, dhmid, cw, cb):
    t = gpre.shape[1]
    tm = _row_tile(t, TM)

    def kern(g_ref, p_ref, u_ref, d_ref, w_ref, b_ref, h_ref, du_ref, dg_ref, dw_ref, db_ref, ext):
        @pl.when(pl.program_id(1) == 0)
        def _():
            dw_ref[...] = jnp.zeros_like(dw_ref)
            db_ref[...] = jnp.zeros_like(db_ref)

        gc, taps = _ffn_conv(ext, g_ref, p_ref, w_ref, b_ref, tm)
        sg = _sigmoid(gc)
        s = gc * sg
        u = u_ref[...]
        d = d_ref[...]
        h_ref[...] = (s * u).astype(BF16)
        du_ref[...] = (d * s).astype(BF16)
        dgc = d * u * (sg * (1.0 + gc * (1.0 - sg)))
        dg_ref[...] = dgc
        db_ref[...] += jnp.sum(dgc, axis=0, keepdims=True)
        for k in range(FFN_CONV):
            dw_ref[k:k + 1, :] += jnp.sum(dgc * taps[k], axis=0, keepdims=True)

    main, prev, cws, cbs = _ffn_mid_specs(t, tm)
    big = lambda dt: jax.ShapeDtypeStruct((N_CHIPS, t, FF_SH), dt)
    return pl.pallas_call(
        kern, name="ffn_mid_bwd", grid=(N_CHIPS, t // tm),
        in_specs=[main, prev, main, main, cws, cbs], out_specs=[main, main, main, cws, cbs],
        out_shape=[big(BF16), big(BF16), big(F32), jax.ShapeDtypeStruct((N_CHIPS, FFN_CONV, FF_SH), F32),
                   jax.ShapeDtypeStruct((N_CHIPS, 1, FF_SH), F32)],
        scratch_shapes=[pltpu.VMEM((tm + HALO, FF_SH), F32)],
        compiler_params=_params(("parallel", "arbitrary")),
    )(gpre, gpre, up, dhmid, cw, cb)


def _ffn_conv_t(dgc, cw):
    t = dgc.shape[1]
    tm = _row_tile(t, TM)
    nt = t // tm

    def kern(d_ref, n_ref, w_ref, o_ref, ext):
        i = pl.program_id(1)
        ext[0:tm, :] = d_ref[...]
        ext[tm:, :] = jnp.where(i < nt - 1, n_ref[...], 0.0)
        acc = sum(w_ref[k:k + 1, :] * ext[pl.ds(FFN_CONV - 1 - k, tm), :] for k in range(FFN_CONV))
        o_ref[...] = acc.astype(BF16)

    main, _, cws, _ = _ffn_mid_specs(t, tm)
    nxt = pl.BlockSpec((None, HALO, FF_SH),
                       lambda j, i: (j, jnp.minimum((i + 1) * (tm // HALO), t // HALO - 1), 0))
    return pl.pallas_call(
        kern, name="ffn_conv_t", grid=(N_CHIPS, nt),
        in_specs=[main, nxt, cws], out_specs=main,
        out_shape=jax.ShapeDtypeStruct((N_CHIPS, t, FF_SH), BF16),
        scratch_shapes=[pltpu.VMEM((tm + HALO, FF_SH), F32)],
        compiler_params=_params(("parallel", "parallel")),
    )(dgc, dgc, cw)


def _ffn_hidden_fwd(xb, wg, wu, cw, cb, layer):
    t = xb.shape[0]
    tm = _row_tile(t, TM)

    def kern(x_ref, wg_ref, wu_ref, cw_ref, cb_ref, g_ref, c_ref, u_ref, h_ref, ext):
        @pl.when(pl.program_id(1) == 0)
        def _():
            ext[0:HALO, :] = jnp.zeros((HALO, FF_SH), F32)

        x = x_ref[...]
        gb = _dot(x, wg_ref[...], NN).astype(BF16)
        ub = _dot(x, wu_ref[...], NN).astype(BF16)
        g_ref[...] = gb
        u_ref[...] = ub
        g = gb.astype(F32)
        w = [cw_ref[k:k + 1, :] for k in range(FFN_CONV)]
        body = cb_ref[...] + w[2] * g + w[1] * pltpu.roll(g, 1, 0) + w[0] * pltpu.roll(g, 2, 0)
        ext[HALO:, :] = g[0:HALO, :]
        head = cb_ref[...] + sum(w[k] * ext[pl.ds(HALO - (FFN_CONV - 1) + k, HALO), :] for k in range(FFN_CONV))
        gcb = jnp.concatenate([head, body[HALO:, :]], axis=0).astype(BF16)
        c_ref[...] = gcb
        gc = gcb.astype(F32)
        h_ref[...] = (gc * _sigmoid(gc) * ub.astype(F32)).astype(BF16)
        ext[0:HALO, :] = g[tm - HALO:, :]

    col = pl.BlockSpec((None, tm, FF_SH), lambda j, i: (j, i, 0))
    wspec = pl.BlockSpec((None, None, D_MODEL, FF_SH), lambda j, i: (layer, j, 0, 0))
    big = jax.ShapeDtypeStruct((N_CHIPS, t, FF_SH), BF16)
    return pl.pallas_call(
        kern, name="ffn_hidden_fwd", grid=(N_CHIPS, t // tm),
        in_specs=[pl.BlockSpec((tm, D_MODEL), lambda j, i: (i, 0)), wspec, wspec,
                  pl.BlockSpec((None, FFN_CONV, FF_SH), lambda j, i: (j, 0, 0)),
                  pl.BlockSpec((None, 1, FF_SH), lambda j, i: (j, 0, 0))],
        out_specs=[col, col, col, col], out_shape=[big, big, big, big],
        scratch_shapes=[pltpu.VMEM((2 * HALO, FF_SH), F32)],
        compiler_params=_params(("parallel", "arbitrary")),
    )(xb, wg, wu, cw, cb)


def _ffn_hidden_bwd(drb, gpre, gconv, up, wd, cw, layer):
    t = drb.shape[0]
    tm = _row_tile(t, TM)
    nt = t // tm
    rb = lambda i: nt - 1 - i

    def kern(d_ref, g_ref, c_ref, u_ref, wd_ref, cw_ref, du_ref, dg_ref, dw_ref, db_ref, ext):
        @pl.when(pl.program_id(1) == 0)
        def _():
            dw_ref[...] = jnp.zeros_like(dw_ref)
            db_ref[...] = jnp.zeros_like(db_ref)
            ext[HALO:, :] = jnp.zeros((HALO, FF_SH), F32)

        dh = _dot(d_ref[...], wd_ref[...], NT)
        gc = c_ref[...].astype(F32)
        sg = _sigmoid(gc)
        du_ref[...] = (dh * (gc * sg)).astype(BF16)
        dgc = dh * u_ref[...].astype(F32) * (sg * (1.0 + gc * (1.0 - sg)))
        db_ref[...] += jnp.sum(dgc, axis=0, keepdims=True)
        g = g_ref[...].astype(F32)
        w = [cw_ref[k:k + 1, :] for k in range(FFN_CONV)]
        taps = [pltpu.roll(dgc, tm - 2, 0), pltpu.roll(dgc, tm - 1, 0), dgc]
        body = sum(w[k] * taps[k] for k in range(FFN_CONV))
        last = slice(tm - HALO, tm)
        ext[0:HALO, :] = dgc[last, :]
        tail_taps = [ext[pl.ds(FFN_CONV - 1 - k, HALO), :] for k in range(FFN_CONV)]
        tail = sum(w[k] * tail_taps[k] for k in range(FFN_CONV))
        dg_ref[...] = jnp.concatenate([body[0:tm - HALO, :], tail], axis=0).astype(BF16)
        for k in range(FFN_CONV):
            dw_ref[k:k + 1, :] += (jnp.sum(g * taps[k], axis=0, keepdims=True)
                                   + jnp.sum(g[last, :] * (tail_taps[k] - taps[k][last, :]), axis=0, keepdims=True))
        ext[HALO:, :] = dgc[0:HALO, :]

    col = pl.BlockSpec((None, tm, FF_SH), lambda j, i: (j, rb(i), 0))
    cws = pl.BlockSpec((None, FFN_CONV, FF_SH), lambda j, i: (j, 0, 0))
    cbs = pl.BlockSpec((None, 1, FF_SH), lambda j, i: (j, 0, 0))
    big = jax.ShapeDtypeStruct((N_CHIPS, t, FF_SH), BF16)
    return pl.pallas_call(
        kern, name="ffn_hidden_bwd", grid=(N_CHIPS, nt),
        in_specs=[pl.BlockSpec((tm, D_MODEL), lambda j, i: (rb(i), 0)), col, col, col,
                  pl.BlockSpec((None, None, FF_SH, D_MODEL), lambda j, i: (layer, j, 0, 0)), cws],
        out_specs=[col, col, cws, cbs],
        out_shape=[big, big, jax.ShapeDtypeStruct((N_CHIPS, FFN_CONV, FF_SH), F32),
                   jax.ShapeDtypeStruct((N_CHIPS, 1, FF_SH), F32)],
        scratch_shapes=[pltpu.VMEM((2 * HALO, FF_SH), F32)],
        compiler_params=_params(("parallel", "arbitrary")),
    )(drb, gpre, gconv, up, wd, cw)


def _s5_coefs(ar, ai, reverse):
    if reverse:
        ai = -ai
    pw = [(ar, ai)]
    for _ in range(SUBLANES - 1):
        pr, pi = pw[-1]
        pw.append((pr * ar - pi * ai, pr * ai + pi * ar))
    rows = jnp.arange(SUBLANES)[:, None]
    out = []
    for s in (1, 2, 4):
        keep = (rows + s <= SUBLANES - 1) if reverse else (rows >= s)
        out += [jnp.where(keep, pw[s - 1][0][None], 0.0), jnp.where(keep, pw[s - 1][1][None], 0.0)]
    order = list(range(SUBLANES - 1, -1, -1)) if reverse else list(range(SUBLANES))
    out += [jnp.stack([pw[k][0] for k in order]), jnp.stack([pw[k][1] for k in order])]
    return jnp.stack(out).astype(F32)


def _s5_scan(buf, coef_ref, carry, tm, reverse):
    n8 = tm // SUBLANES

    def body(it, c):
        cre, cim = c
        blk = (n8 - 1 - it) if reverse else it
        r0 = pl.multiple_of(blk * SUBLANES, SUBLANES)
        xre = buf[pl.ds(r0, SUBLANES), 0:N_STATE]
        xim = buf[pl.ds(r0, SUBLANES), N_STATE:]
        for idx, s in enumerate((1, 2, 4)):
            sh = (SUBLANES - s) if reverse else s
            sre = pltpu.roll(xre, sh, 0)
            sim = pltpu.roll(xim, sh, 0)
            are = coef_ref[2 * idx]
            aim = coef_ref[2 * idx + 1]
            xre, xim = xre + are * sre - aim * sim, xim + are * sim + aim * sre
        pre = coef_ref[6]
        pim = coef_ref[7]
        hre = xre + pre * cre - pim * cim
        him = xim + pre * cim + pim * cre
        buf[pl.ds(r0, SUBLANES), 0:N_STATE] = hre
        buf[pl.ds(r0, SUBLANES), N_STATE:] = him
        row = 0 if reverse else SUBLANES - 1
        return (jnp.broadcast_to(hre[row:row + 1], (SUBLANES, N_STATE)),
                jnp.broadcast_to(him[row:row + 1], (SUBLANES, N_STATE)))

    cre, cim = lax.fori_loop(0, n8, body, (carry[:, 0:N_STATE], carry[:, N_STATE:]))
    carry[:, 0:N_STATE] = cre
    carry[:, N_STATE:] = cim


def _real_scan(abuf, bbuf, carry, tm, reverse):
    n8 = tm // SUBLANES
    width = bbuf.shape[1]

    def body(it, c):
        blk = (n8 - 1 - it) if reverse else it
        r0 = pl.multiple_of(blk * SUBLANES, SUBLANES)
        a = abuf[pl.ds(r0, SUBLANES), :]
        b = bbuf[pl.ds(r0, SUBLANES), :]
        rows = lax.broadcasted_iota(jnp.int32, (SUBLANES, width), 0)
        for s in (1, 2, 4):
            sh = (SUBLANES - s) if reverse else s
            keep = (rows + s <= SUBLANES - 1) if reverse else (rows >= s)
            sa = pltpu.roll(a, sh, 0)
            sb = pltpu.roll(b, sh, 0)
            b = b + a * jnp.where(keep, sb, 0.0)
            a = a * jnp.where(keep, sa, 1.0)
        h = b + a * c
        bbuf[pl.ds(r0, SUBLANES), :] = h
        row = 0 if reverse else SUBLANES - 1
        return jnp.broadcast_to(h[row:row + 1], (SUBLANES, width))

    carry[...] = lax.fori_loop(0, n8, body, carry[...])


def _dot(a, b, dims):
    return lax.dot_general(a, b, (dims, ((), ())), preferred_element_type=F32)


TS5 = 256
HALO16 = 16


def _s5_fwd(proj, bmat, coef, cmat, dvec, gw, gb):
    t = proj.shape[0]
    tm = _row_tile(t, TS5)

    def kern(u_ref, b_ref, coef_ref, c_ref, d_ref, gw_ref, gb_ref, h_ref, y_ref, hbuf, carry):
        @pl.when(pl.program_id(0) == 0)
        def _():
            carry[...] = jnp.zeros_like(carry)

        u = u_ref[...]
        hbuf[...] = _dot(u.astype(BF16), b_ref[...], NN)
        _s5_scan(hbuf, coef_ref, carry, tm, False)
        hb = hbuf[...].astype(BF16)
        h_ref[...] = hb
        y = _dot(hb, c_ref[...], NN) + d_ref[...] * u
        ys = _gelu(y)
        z = _dot(ys.astype(BF16), gw_ref[...], NN) + gb_ref[...]
        y_ref[...] = ys * _sigmoid(z)

    full = lambda shp: pl.BlockSpec(shp, lambda i: (0,) * len(shp))
    return pl.pallas_call(
        kern, name="s5_fwd", grid=(t // tm,),
        in_specs=[pl.BlockSpec((tm, D_S5), lambda i: (i, 0)), full((D_S5, 2 * N_STATE)),
                  full((8, SUBLANES, N_STATE)), full((2 * N_STATE, D_S5)), full((1, D_S5)),
                  full((D_S5, D_S5)), full((1, D_S5))],
        out_specs=[pl.BlockSpec((tm, 2 * N_STATE), lambda i: (i, 0)), pl.BlockSpec((tm, D_S5), lambda i: (i, 0))],
        out_shape=[jax.ShapeDtypeStruct((t, 2 * N_STATE), BF16), jax.ShapeDtypeStruct((t, D_S5), F32)],
        scratch_shapes=[pltpu.VMEM((tm, 2 * N_STATE), F32), pltpu.VMEM((SUBLANES, 2 * N_STATE), F32)],
        compiler_params=_params(("arbitrary",)),
    )(proj, bmat, coef, cmat, dvec, gw, gb)


def _s5_bwd(proj, h, dout, bmat, coef_b, cmat, dvec, gw, gb):
    t = proj.shape[0]
    tm = _row_tile(t, TS5)
    nt = t // tm
    rb = lambda i: nt - 1 - i

    def kern(u_ref, h_ref, hp_ref, d_ref, b_ref, coef_ref, c_ref, dv_ref, gw_ref, gb_ref,
             du_ref, dc_ref, db_ref, da_ref, dd_ref, dgw_ref, dgb_ref, gbuf, hext, carry):
        i = pl.program_id(0)

        @pl.when(i == 0)
        def _():
            carry[...] = jnp.zeros_like(carry)
            for r in (dc_ref, db_ref, da_ref, dd_ref, dgw_ref, dgb_ref):
                r[...] = jnp.zeros_like(r)

        u = u_ref[...]
        hb = h_ref[...]
        y = _dot(hb, c_ref[...], NN) + dv_ref[...] * u
        ys = _gelu(y)
        ysb = ys.astype(BF16)
        sg = _sigmoid(_dot(ysb, gw_ref[...], NN) + gb_ref[...])
        d_o = d_ref[...]
        dz = d_o * ys * sg * (1.0 - sg)
        dzb = dz.astype(BF16)
        dys = d_o * sg + _dot(dzb, gw_ref[...], NT)
        dgw_ref[...] += _dot(ysb, dzb, TN)
        dgb_ref[...] += jnp.sum(dz, axis=0, keepdims=True)
        dy = dys * _gelu_grad(y)
        dd_ref[...] += jnp.sum(dy * u, axis=0, keepdims=True)
        dyb = dy.astype(BF16)
        dc_ref[...] += _dot(hb, dyb, TN)
        gbuf[...] = _dot(dyb, c_ref[...], NT)
        _s5_scan(gbuf, coef_ref, carry, tm, True)
        g = gbuf[...]
        first = jnp.where(i < nt - 1, hp_ref[HALO16 - 1:HALO16, :].astype(F32), 0.0)
        hext[SUBLANES - 1:SUBLANES, :] = first
        hext[SUBLANES:, :] = hb.astype(F32)
        hprev = hext[pl.ds(SUBLANES - 1, tm), :]
        gre, gim = g[:, 0:N_STATE], g[:, N_STATE:]
        pre, pim = hprev[:, 0:N_STATE], hprev[:, N_STATE:]
        da_ref[0:1, :] += jnp.sum(gre * pre + gim * pim, axis=0, keepdims=True)
        da_ref[1:2, :] += jnp.sum(gim * pre - gre * pim, axis=0, keepdims=True)
        gb16 = g.astype(BF16)
        db_ref[...] += _dot(u.astype(BF16), gb16, TN)
        du_ref[...] = dy * dv_ref[...] + _dot(gb16, b_ref[...], NT)

    full = lambda shp: pl.BlockSpec(shp, lambda i: (0,) * len(shp))
    shape = lambda shp: jax.ShapeDtypeStruct(shp, F32)
    return pl.pallas_call(
        kern, name="s5_bwd", grid=(nt,),
        in_specs=[pl.BlockSpec((tm, D_S5), lambda i: (rb(i), 0)),
                  pl.BlockSpec((tm, 2 * N_STATE), lambda i: (rb(i), 0)),
                  pl.BlockSpec((HALO16, 2 * N_STATE), lambda i: (jnp.maximum(rb(i) * (tm // HALO16) - 1, 0), 0)),
                  pl.BlockSpec((tm, D_S5), lambda i: (rb(i), 0)),
                  full((D_S5, 2 * N_STATE)), full((8, SUBLANES, N_STATE)), full((2 * N_STATE, D_S5)),
                  full((1, D_S5)), full((D_S5, D_S5)), full((1, D_S5))],
        out_specs=[pl.BlockSpec((tm, D_S5), lambda i: (rb(i), 0)), full((2 * N_STATE, D_S5)),
                   full((D_S5, 2 * N_STATE)), full((2, N_STATE)), full((1, D_S5)), full((D_S5, D_S5)),
                   full((1, D_S5))],
        out_shape=[shape((t, D_S5)), shape((2 * N_STATE, D_S5)), shape((D_S5, 2 * N_STATE)),
                   shape((2, N_STATE)), shape((1, D_S5)), shape((D_S5, D_S5)), shape((1, D_S5))],
        scratch_shapes=[pltpu.VMEM((tm, 2 * N_STATE), F32), pltpu.VMEM((tm + SUBLANES, 2 * N_STATE), F32),
                        pltpu.VMEM((SUBLANES, 2 * N_STATE), F32)],
        compiler_params=_params(("arbitrary",)),
    )(proj, h, h, dout, bmat, coef_b, cmat, dvec, gw, gb)


def _lru_gates(ext, x_ref, p_ref, cw_ref, cb_ref, wx_ref, bx_ref, wa_ref, ba_ref, ap_ref, first_tile, row0, tm):
    ext[0:HALO, :] = jnp.where(first_tile, 0.0, p_ref[...])
    ext[HALO:, :] = x_ref[...]
    taps = [ext[pl.ds(HALO - (LRU_CONV - 1) + k, tm), :] for k in range(LRU_CONV)]
    xc = cb_ref[...] + sum(cw_ref[k:k + 1, :] * taps[k] for k in range(LRU_CONV))
    xcb = xc.astype(BF16)
    gx = _sigmoid(_dot(xcb, wx_ref[...], NN) + bx_ref[...])
    ga = _sigmoid(_dot(xcb, wa_ref[...], NN) + ba_ref[...])
    z = -ap_ref[...]
    sp = jnp.maximum(z, 0.0) + jnp.log(1.0 + jnp.exp(-jnp.abs(z)))
    log_a = -LRU_C * ga * sp
    a = jnp.exp(log_a)
    tok = row0 + lax.broadcasted_iota(jnp.int32, a.shape, 0)
    is0 = tok == 0
    mult = jnp.where(is0, 1.0, jnp.sqrt(1.0 - jnp.exp(2.0 * log_a)))
    return taps, xc, xcb, gx, ga, sp, a, mult, is0


def _lru_specs(tm, blk_of):
    col = lambda cidx: pl.BlockSpec((tm, D_LRU), lambda i: (blk_of(i), cidx))
    prev = lambda cidx: pl.BlockSpec((HALO, D_LRU), lambda i: (jnp.maximum(blk_of(i) * (tm // HALO) - 1, 0), cidx))
    full = lambda shp: pl.BlockSpec(shp, lambda i: (0,) * len(shp))
    wts = [full((LRU_CONV, D_LRU)), full((1, D_LRU)), full((D_LRU, D_LRU)), full((1, D_LRU)),
           full((D_LRU, D_LRU)), full((1, D_LRU)), full((1, D_LRU))]
    return col, prev, full, wts


def _lru_fwd(proj, cw, cb, wx, bx, wa, ba, ap):
    t = proj.shape[0]
    tm = _row_tile(t, TM)

    def kern(x_ref, p_ref, g_ref, cw_ref, cb_ref, wx_ref, bx_ref, wa_ref, ba_ref, ap_ref,
             y_ref, h_ref, ext, abuf, carry):
        i = pl.program_id(0)

        @pl.when(i == 0)
        def _():
            carry[...] = jnp.zeros_like(carry)

        _, xc, _, gx, _, _, a, mult, _ = _lru_gates(ext, x_ref, p_ref, cw_ref, cb_ref, wx_ref, bx_ref, wa_ref,
                                                    ba_ref, ap_ref, i == 0, i * tm, tm)
        abuf[...] = a
        h_ref[...] = mult * gx * xc
        _real_scan(abuf, h_ref, carry, tm, False)
        y_ref[...] = h_ref[...] * _gelu(g_ref[...])

    col, prev, full, wts = _lru_specs(tm, lambda i: i)
    out = pl.BlockSpec((tm, D_LRU), lambda i: (i, 0))
    return pl.pallas_call(
        kern, name="lru_fwd", grid=(t // tm,),
        in_specs=[col(1), prev(1), col(2)] + wts, out_specs=[out, out],
        out_shape=[jax.ShapeDtypeStruct((t, D_LRU), F32), jax.ShapeDtypeStruct((t, D_LRU), F32)],
        scratch_shapes=[pltpu.VMEM((tm + HALO, D_LRU), F32), pltpu.VMEM((tm, D_LRU), F32),
                        pltpu.VMEM((SUBLANES, D_LRU), F32)],
        compiler_params=_params(("arbitrary",)),
    )(proj, proj, proj, cw, cb, wx, bx, wa, ba, ap)


def _lru_bwd(proj, h, dout, cw, cb, wx, bx, wa, ba, ap):
    t = proj.shape[0]
    tm = _row_tile(t, TM)
    nt = t // tm
    rb = lambda i: nt - 1 - i

    def kern(x_ref, p_ref, g_ref, h_ref, hp_ref, d_ref, cw_ref, cb_ref, wx_ref, bx_ref, wa_ref, ba_ref, ap_ref,
             dxc_ref, dg_ref, dcw_ref, dcb_ref, dwx_ref, dbx_ref, dwa_ref, dba_ref, dap_ref,
             ext, aext, abuf, gbuf, carry, acarry):
        i = pl.program_id(0)
        blk = nt - 1 - i

        @pl.when(i == 0)
        def _():
            carry[...] = jnp.zeros_like(carry)
            acarry[...] = jnp.zeros_like(acarry)
            for r in (dcw_ref, dcb_ref, dwx_ref, dbx_ref, dwa_ref, dba_ref, dap_ref):
                r[...] = jnp.zeros_like(r)

        taps, xc, xcb, gx, ga, sp, a, mult, is0 = _lru_gates(
            ext, x_ref, p_ref, cw_ref, cb_ref, wx_ref, bx_ref, wa_ref, ba_ref, ap_ref, blk == 0, blk * tm, tm)
        gate = g_ref[...]
        d_o = d_ref[...]
        hcur = h_ref[...]
        dg_ref[...] = d_o * hcur * _gelu_grad(gate)
        aext[0:tm, :] = a
        aext[tm:, :] = acarry[...]
        abuf[...] = aext[pl.ds(1, tm), :]
        gbuf[...] = d_o * _gelu(gate)
        _real_scan(abuf, gbuf, carry, tm, True)
        acarry[...] = jnp.broadcast_to(a[0:1], acarry.shape)
        g = gbuf[...]
        ext[0:HALO, :] = jnp.where(blk == 0, 0.0, hp_ref[...])
        ext[HALO:, :] = hcur
        hprev = ext[pl.ds(HALO - 1, tm), :]
        dmult = jnp.where(is0, 0.0, g * gx * xc)
        dgx = g * mult * xc
        dxc = g * mult * gx
        dlog_a = g * hprev * a - dmult * (a * a) / mult
        dga = dlog_a * (-LRU_C * sp)
        dsp = jnp.sum(dlog_a * (-LRU_C * ga), axis=0, keepdims=True)
        dap_ref[...] += dsp * (-_sigmoid(-ap_ref[...]))
        dpa = (dga * ga * (1.0 - ga))
        dpx = (dgx * gx * (1.0 - gx))
        dpab, dpxb = dpa.astype(BF16), dpx.astype(BF16)
        dwx_ref[...] += _dot(xcb, dpxb, TN)
        dwa_ref[...] += _dot(xcb, dpab, TN)
        dbx_ref[...] += jnp.sum(dpx, axis=0, keepdims=True)
        dba_ref[...] += jnp.sum(dpa, axis=0, keepdims=True)
        dxc = dxc + _dot(dpxb, wx_ref[...], NT) + _dot(dpab, wa_ref[...], NT)
        dxc_ref[...] = dxc
        dcb_ref[...] += jnp.sum(dxc, axis=0, keepdims=True)
        for k in range(LRU_CONV):
            dcw_ref[k:k + 1, :] += jnp.sum(dxc * taps[k], axis=0, keepdims=True)

    col, prev, full, wts = _lru_specs(tm, rb)
    row = pl.BlockSpec((tm, D_LRU), lambda i: (rb(i), 0))
    hprev_spec = pl.BlockSpec((HALO, D_LRU), lambda i: (jnp.maximum(rb(i) * (tm // HALO) - 1, 0), 0))
    shape = lambda shp: jax.ShapeDtypeStruct(shp, F32)
    vec = (1, D_LRU)
    sq = (D_LRU, D_LRU)
    return pl.pallas_call(
        kern, name="lru_bwd", grid=(nt,),
        in_specs=[col(1), prev(1), col(2), row, hprev_spec, row] + wts,
        out_specs=[row, row, full((LRU_CONV, D_LRU)), full(vec), full(sq), full(vec), full(sq), full(vec), full(vec)],
        out_shape=[shape((t, D_LRU)), shape((t, D_LRU)), shape((LRU_CONV, D_LRU)), shape(vec), shape(sq),
                   shape(vec), shape(sq), shape(vec), shape(vec)],
        scratch_shapes=[pltpu.VMEM((tm + HALO, D_LRU), F32), pltpu.VMEM((tm + HALO, D_LRU), F32),
                        pltpu.VMEM((tm, D_LRU), F32), pltpu.VMEM((tm, D_LRU), F32),
                        pltpu.VMEM((SUBLANES, D_LRU), F32), pltpu.VMEM((SUBLANES, D_LRU), F32)],
        compiler_params=_params(("arbitrary",)),
    )(proj, proj, proj, h, h, dout, cw, cb, wx, bx, wa, ba, ap)


def _assemble_dproj(dq, dk, dv, du, dxc, dgate, cos, sin_s, cw):
    t = dq.shape[0]
    tm = _row_tile(t, TM)
    nt = t // tm

    def kern(dq_ref, dk_ref, dv_ref, du_ref, dx_ref, dn_ref, dg_ref, c_ref, s_ref, cw_ref, o_ref, b_ref, ext):
        i = pl.program_id(0)

        @pl.when(i == 0)
        def _():
            b_ref[...] = jnp.zeros_like(b_ref)

        def put(lo, val):
            hi = lo + val.shape[1]
            o_ref[:, lo:hi] = val.astype(BF16)
            b_ref[:, lo:hi] += jnp.sum(val, axis=0, keepdims=True)

        c = c_ref[...]
        s = s_ref[...]
        for ch in range(4):
            x = dq_ref[:, ch * 128:(ch + 1) * 128] * (HEAD_DIM ** -0.5)
            put(ch * 128, x * c - _rope_swap(x) * s)
        x = dk_ref[...]
        put(512, x * c - _rope_swap(x) * s)
        put(640, dv_ref[...])
        put(768, du_ref[...])
        ext[0:tm, :] = dx_ref[...]
        ext[tm:, :] = jnp.where(i < nt - 1, dn_ref[...], 0.0)
        put(1024, sum(cw_ref[k:k + 1, :] * ext[pl.ds(LRU_CONV - 1 - k, tm), :] for k in range(LRU_CONV)))
        put(1280, dg_ref[...])

    row = lambda w: pl.BlockSpec((tm, w), lambda i: (i, 0))
    nxt = pl.BlockSpec((HALO, D_LRU), lambda i: (jnp.minimum((i + 1) * (tm // HALO), t // HALO - 1), 0))
    return pl.pallas_call(
        kern, name="assemble_dproj", grid=(nt,),
        in_specs=[row(512), row(128), row(128), row(256), row(256), nxt, row(256), row(128), row(128),
                  pl.BlockSpec((LRU_CONV, D_LRU), lambda i: (0, 0))],
        out_specs=[row(D_IN), pl.BlockSpec((1, D_IN), lambda i: (0, 0))],
        out_shape=[jax.ShapeDtypeStruct((t, D_IN), BF16), jax.ShapeDtypeStruct((1, D_IN), F32)],
        scratch_shapes=[pltpu.VMEM((tm + HALO, D_LRU), F32)],
        compiler_params=_params(("arbitrary",)),
    )(dq, dk, dv, du, dxc, dxc, dgate, cos, sin_s, cw)


def _blockdiag_s5(bbar_re, bbar_im, c_re, c_im):
    eye = jnp.eye(S5_GROUPS, dtype=F32)
    b_of = lambda m: jnp.einsum('gpc,gh->gchp', m, eye).reshape(D_S5, N_STATE)
    c_of = lambda m: jnp.einsum('gcp,gh->gphc', m, eye).reshape(N_STATE, D_S5)
    bmat = jnp.concatenate([b_of(bbar_re), b_of(bbar_im)], axis=1)
    cmat = jnp.concatenate([c_of(c_re), -c_of(c_im)], axis=0)
    return bmat, cmat


def _s5_prepare(a_re, a_im, b_re, b_im, c_re, c_im, log_dt):
    lam_re = jnp.minimum(a_re, -1e-4)
    lam_im = a_im
    dt = jnp.exp(log_dt)[:, None]
    decay = jnp.exp(dt * lam_re)
    ang = dt * lam_im
    abar_re = decay * jnp.cos(ang)
    abar_im = decay * jnp.sin(ang)
    den = jnp.square(lam_re) + jnp.square(lam_im)
    nr = abar_re - 1.0
    ni = abar_im
    coef_re = (nr * lam_re + ni * lam_im) / den
    coef_im = (ni * lam_re - nr * lam_im) / den
    bbar_re = coef_re[..., None] * b_re - coef_im[..., None] * b_im
    bbar_im = coef_re[..., None] * b_im + coef_im[..., None] * b_re
    bmat, cmat = _blockdiag_s5(bbar_re, bbar_im, c_re, c_im)
    return abar_re.reshape(N_STATE), abar_im.reshape(N_STATE), bmat, cmat


def _blockdiag_lru(w):
    eye = jnp.eye(LRU_HEADS, dtype=F32)
    return jnp.einsum('hij,hk->hikj', w, eye).reshape(D_LRU, D_LRU)


def _rope_tables(t):
    inv_freq = ROPE_THETA ** (-jnp.arange(0, HEAD_DIM, 2, dtype=F32) / HEAD_DIM)
    ang = jnp.arange(t, dtype=F32)[:, None] * inv_freq[None, :]
    cos, sin = jnp.cos(ang), jnp.sin(ang)
    return jnp.tile(jnp.concatenate([cos, cos], axis=1), (1, 2)), jnp.tile(jnp.concatenate([-sin, sin], axis=1), (1, 2))


def _vec(v):
    return v.reshape(1, -1)


def _layer_weights(p):
    abar_re, abar_im, bmat, cmat = _s5_prepare(p['s5_a_re'], p['s5_a_im'], p['s5_b_re'], p['s5_b_im'],
                                               p['s5_c_re'], p['s5_c_im'], p['s5_log_dt'])
    return dict(
        coef_f=_s5_coefs(abar_re, abar_im, False), coef_b=_s5_coefs(abar_re, abar_im, True),
        bmat=bmat.astype(BF16), cmat=cmat.astype(BF16),
        wx=_blockdiag_lru(p['lru_wx']).astype(BF16), wa=_blockdiag_lru(p['lru_wa']).astype(BF16),
        gw=p['s5_glu_w'].astype(BF16))


def _layer_fwd(x, xb, p, w, cos, sin_s):
    t = x.shape[0]
    tm = _row_tile(t, TM)
    layer = p['layer']
    qkv, uxg = _in_proj(xb, p['w_in'], _vec(p['b_in']), cos, sin_s, layer)
    ya, lse = _attn_fwd(qkv, _vec(p['attn_sinks']))
    h5, ys = _s5_fwd(uxg, w['bmat'], w['coef_f'], w['cmat'], _vec(p['s5_d']), w['gw'], _vec(p['s5_glu_b']))
    lru_w = (p['lru_conv_w'], _vec(p['lru_conv_b']), w['wx'], _vec(p['lru_bx']), w['wa'], _vec(p['lru_ba']),
             _vec(p['lru_a_param']))
    yl, hl = _lru_fwd(uxg, *lru_w)
    mix, x1, x1b, xhat1, rstd1 = _mix_out_ln(ya, ys, yl, _vec(p['mix_norm_g']), p['w_out'], _vec(p['b_out']), x,
                                             _vec(p['ln1_g']), _vec(p['ln1_b']), layer)
    gpre, gconv, up, hmid = _ffn_hidden_fwd(x1b, p['ffn_w_gate'], p['ffn_w_up'], p['ffn_conv_w'], p['ffn_conv_b'],
                                            layer)
    x2, x2b, xhat2, rstd2 = _matmul_ln(
        "ffn_down_ln", hmid, p['ffn_w_down'], jnp.zeros((1, D_MODEL), F32), x1, _vec(p['ln2_g']), _vec(p['ln2_b']),
        a_blk=(N_CHIPS, tm, FF_SH), a_map=lambda i: (0, i, 0), w_blk=(N_CHIPS, FF_SH, D_MODEL), parts=N_CHIPS,
        layer=layer)
    saved = dict(xb=xb, uxg=uxg, qkv=qkv, ya=ya, lse=lse, h5=h5, ys=ys, yl=yl, hl=hl, mix=mix, x1b=x1b, xhat1=xhat1,
                 rstd1=rstd1, gpre=gpre, gconv=gconv, up=up, hmid=hmid, xhat2=xhat2, rstd2=rstd2, lru_w=lru_w)
    return x2, x2b, saved


def _layer_bwd(dr2, dr2b, s, p, w, cos, sin_s, big, below):
    t = dr2.shape[0]
    tk = _row_tile(t, TMM)
    nk = t // tk
    tm = _row_tile(t, TM)
    layer = p['layer']
    big = dict(big)
    g = {}
    dup, dgpre, g['ffn_conv_w'], g['ffn_conv_b'] = _ffn_hidden_bwd(
        dr2b, s['gpre'], s['gconv'], s['up'], p['ffn_w_down'], p['ffn_conv_w'], layer)
    big['ffn_w_down'] = _matmul(
        "d_w_down", s['hmid'], dr2b, a_blk=(None, tk, FF_SH), a_map=lambda i, j, k: (i, k, 0), b_blk=(tk, D_MODEL),
        b_map=lambda i, j, k: (k, 0), out_shape=(DEPTH, N_CHIPS, FF_SH, D_MODEL), o_blk=(None, None, FF_SH, D_MODEL),
        o_map=lambda i, j: (layer, i, 0, 0), grid=(N_CHIPS, 1, nk), dims=TN, into=big['ffn_w_down'])
    d_ffn_w = lambda name, dact, buf: _matmul(
        name, s['x1b'], dact, a_blk=(tk, D_MODEL), a_map=lambda i, j, k: (k, 0), b_blk=(None, tk, FF_SH),
        b_map=lambda i, j, k: (j, k, 0), out_shape=(DEPTH, N_CHIPS, D_MODEL, FF_SH),
        o_blk=(None, None, D_MODEL, FF_SH), o_map=lambda i, j: (layer, j, 0, 0), grid=(1, N_CHIPS, nk), dims=TN,
        into=buf)
    big['ffn_w_gate'] = d_ffn_w("d_w_gate", dgpre, big['ffn_w_gate'])
    big['ffn_w_up'] = d_ffn_w("d_w_up", dup, big['ffn_w_up'])
    wspec = dict(b_blk=(None, None, D_MODEL, FF_SH), b_map=lambda i, j, k: (layer, k, 0, 0))
    dx1 = _matmul(
        "d_x1", dgpre, p['ffn_w_gate'], pair2=(dup, p['ffn_w_up']), a_blk=(None, tk, FF_SH),
        a_map=lambda i, j, k: (k, i, 0), out_shape=(t, D_MODEL), o_blk=(tk, D_MODEL), o_map=lambda i, j: (i, 0),
        grid=(nk, 1, N_CHIPS), dims=NT, add=dr2, add_scale=ALPHA, **wspec)
    dr1, dr1b, g['ln1_g'], g['ln1_b'], g['b_out'], dya, dys, dyl, g['mix_norm_g'] = _d_mix_rms(
        dx1, s['xhat1'], s['rstd1'], _vec(p['ln1_g']), p['w_out'], s['ya'], s['ys'], s['yl'],
        _vec(p['mix_norm_g']), layer)
    big['w_out'] = _matmul(
        "d_w_out", s['mix'], dr1b, a_blk=(tk, D_MODEL), a_map=lambda i, j, k: (k, 0), b_blk=(tk, D_MODEL),
        b_map=lambda i, j, k: (k, 0), out_shape=(DEPTH, D_MODEL, D_MODEL), o_blk=(None, D_MODEL, D_MODEL),
        o_map=lambda i, j: (layer, 0, 0), grid=(1, 1, nk), dims=TN, into=big['w_out'])
    dq, dk, dv, g['attn_sinks'] = _attn_bwd(s['qkv'], s['ya'], dya, s['lse'], _vec(p['attn_sinks']))
    du, dcmat, dbmat, dabar, g['s5_d'], g['s5_glu_w'], g['s5_glu_b'] = _s5_bwd(
        s['uxg'], s['h5'], dys, w['bmat'], w['coef_b'], w['cmat'], _vec(p['s5_d']), w['gw'], _vec(p['s5_glu_b']))
    (dxc, dgate, g['lru_conv_w'], g['lru_conv_b'], dwx, g['lru_bx'], dwa, g['lru_ba'],
     g['lru_a_param']) = _lru_bwd(s['uxg'], s['hl'], dyl, *s['lru_w'])
    dproj, g['b_in'] = _assemble_dproj(dq, dk, dv, du, dxc, dgate, cos, sin_s, p['lru_conv_w'])
    big['w_in'] = _matmul(
        "d_w_in", s['xb'], dproj, a_blk=(tk, D_MODEL), a_map=lambda i, j, k: (k, 0), b_blk=(tk, IN_SH),
        b_map=lambda i, j, k: (k, j), out_shape=(DEPTH, N_CHIPS, D_MODEL, IN_SH), o_blk=(None, None, D_MODEL, IN_SH),
        o_map=lambda i, j: (layer, j, 0, 0), grid=(1, N_CHIPS, nk), dims=TN, into=big['w_in'])
    dx = _matmul("d_x", dproj, p['w_in'], a_blk=(tk, IN_SH), a_map=lambda i, j, k: (i, k),
                 b_blk=(None, None, D_MODEL, IN_SH), b_map=lambda i, j, k: (layer, k, 0, 0), out_shape=(t, D_MODEL),
                 o_blk=(tk, D_MODEL), o_map=lambda i, j: (i, 0), grid=(nk, 1, N_CHIPS), dims=NT,
                 add=dr1, add_scale=ALPHA)
    if below is not None:
        dx = _ln_bwd(dx, below[0]['xhat2'], below[0]['rstd2'], _vec(below[1]['ln2_g']))
    return dx, _param_chain(g, p, dabar, dbmat, dcmat, dwx, dwa), big


def _layer_fwd_v1(x, p, w, cos, sin_s):
    t = x.shape[0]
    nt = t // _row_tile(t, TM)
    tm = t // nt
    proj = _matmul("in_proj", x, p['w_in'], a_blk=(tm, D_MODEL), a_map=lambda i, j, k: (i, 0),
                   b_blk=(None, D_MODEL, IN_SH), b_map=lambda i, j, k: (j, 0, 0), out_shape=(t, D_IN),
                   o_blk=(tm, IN_SH), o_map=lambda i, j: (i, j), grid=(nt, N_CHIPS, 1), dims=NN,
                   bias=_vec(p['b_in']), bias_blk=(1, IN_SH), bias_map=lambda i, j, k: (0, j))
    qkv = _qkv_post(proj, cos, sin_s)
    ya, lse = _attn_fwd(qkv, _vec(p['attn_sinks']))
    h5, ys = _s5_fwd(proj, w['bmat'], w['coef_f'], w['cmat'], _vec(p['s5_d']), w['gw'], _vec(p['s5_glu_b']))
    lru_w = (p['lru_conv_w'], _vec(p['lru_conv_b']), w['wx'], _vec(p['lru_bx']), w['wa'], _vec(p['lru_ba']),
             _vec(p['lru_a_param']))
    yl, hl = _lru_fwd(proj, *lru_w)
    mix = _rms_fwd(ya, ys, yl, _vec(p['mix_norm_g']))
    f1 = _matmul("out_proj", mix, p['w_out'], a_blk=(tm, D_MODEL), a_map=lambda i, j, k: (i, 0),
                 b_blk=(D_MODEL, D_MODEL), b_map=lambda i, j, k: (0, 0), out_shape=(t, D_MODEL),
                 o_blk=(tm, D_MODEL), o_map=lambda i, j: (i, 0), grid=(nt, 1, 1), dims=NN,
                 bias=_vec(p['b_out']), bias_blk=(1, D_MODEL), bias_map=lambda i, j, k: (0, 0))
    x1, xhat1, rstd1 = _ln_fwd(x, f1, _vec(p['ln1_g']), _vec(p['ln1_b']))
    ffn_in = lambda name, wmat: _matmul(
        name, x1, wmat, a_blk=(tm, D_MODEL), a_map=lambda i, j, k: (i, 0), b_blk=(None, D_MODEL, FF_SH),
        b_map=lambda i, j, k: (j, 0, 0), out_shape=(N_CHIPS, t, FF_SH), o_blk=(None, tm, FF_SH),
        o_map=lambda i, j: (j, i, 0), grid=(nt, N_CHIPS, 1), dims=NN)
    gpre = ffn_in("ffn_gate", p['ffn_w_gate'])
    up = ffn_in("ffn_up", p['ffn_w_up'])
    hmid = _ffn_mid_fwd(gpre, up, p['ffn_conv_w'], p['ffn_conv_b'])
    f2 = _matmul("ffn_down", hmid, p['ffn_w_down'], a_blk=(None, tm, FF_SH), a_map=lambda i, j, k: (k, i, 0),
                 b_blk=(None, FF_SH, D_MODEL), b_map=lambda i, j, k: (k, 0, 0), out_shape=(t, D_MODEL),
                 o_blk=(tm, D_MODEL), o_map=lambda i, j: (i, 0), grid=(nt, 1, N_CHIPS), dims=NN)
    x2, xhat2, rstd2 = _ln_fwd(x1, f2, _vec(p['ln2_g']), _vec(p['ln2_b']))
    saved = dict(x=x, proj=proj, qkv=qkv, ya=ya, lse=lse, h5=h5, ys=ys, yl=yl, hl=hl, mix=mix, x1=x1, xhat1=xhat1,
                 rstd1=rstd1, gpre=gpre, up=up, xhat2=xhat2, rstd2=rstd2, lru_w=lru_w)
    return x2, saved


def _param_chain(g, p, dabar, dbmat, dcmat, dwx, dwa):
    s5_names = ('s5_a_re', 's5_a_im', 's5_b_re', 's5_b_im', 's5_c_re', 's5_c_im', 's5_log_dt')
    _, s5_vjp = jax.vjp(_s5_prepare, *[p[n] for n in s5_names])
    for n, val in zip(s5_names, s5_vjp((dabar[0], dabar[1], dbmat, dcmat))):
        g[n] = val
    g['lru_wx'] = jax.vjp(_blockdiag_lru, p['lru_wx'])[1](dwx)[0]
    g['lru_wa'] = jax.vjp(_blockdiag_lru, p['lru_wa'])[1](dwa)[0]
    return g


def _layer_bwd_v1(dx2, s, p, w, cos, sin_s):
    t = dx2.shape[0]
    nt = t // _row_tile(t, TM)
    tm = t // nt
    g = {}
    dr2, g['ln2_g'], g['ln2_b'], _ = _ln_bwd(dx2, s['xhat2'], s['rstd2'], _vec(p['ln2_g']))
    dhmid = _matmul("d_hmid", dr2, p['ffn_w_down'], a_blk=(tm, D_MODEL), a_map=lambda i, j, k: (i, 0),
                    b_blk=(None, FF_SH, D_MODEL), b_map=lambda i, j, k: (j, 0, 0), out_shape=(N_CHIPS, t, FF_SH),
                    o_blk=(None, tm, FF_SH), o_map=lambda i, j: (j, i, 0), grid=(nt, N_CHIPS, 1), dims=NT)
    hmid, dup, dgc, g['ffn_conv_w'], g['ffn_conv_b'] = _ffn_mid_bwd(s['gpre'], s['up'], dhmid, p['ffn_conv_w'],
                                                                    p['ffn_conv_b'])
    g['ffn_w_down'] = _matmul("d_w_down", hmid, dr2, a_blk=(None, tm, FF_SH), a_map=lambda i, j, k: (i, k, 0),
                              b_blk=(tm, D_MODEL), b_map=lambda i, j, k: (k, 0), out_shape=(N_CHIPS, FF_SH, D_MODEL),
                              o_blk=(None, FF_SH, D_MODEL), o_map=lambda i, j: (i, 0, 0), grid=(N_CHIPS, 1, nt), dims=TN)
    dgpre = _ffn_conv_t(dgc, p['ffn_conv_w'])
    d_ffn_w = lambda name, dact: _matmul(
        name, s['x1'], dact, a_blk=(tm, D_MODEL), a_map=lambda i, j, k: (k, 0), b_blk=(None, tm, FF_SH),
        b_map=lambda i, j, k: (j, k, 0), out_shape=(N_CHIPS, D_MODEL, FF_SH), o_blk=(None, D_MODEL, FF_SH),
        o_map=lambda i, j: (j, 0, 0), grid=(1, N_CHIPS, nt), dims=TN)
    g['ffn_w_gate'] = d_ffn_w("d_w_gate", dgpre)
    g['ffn_w_up'] = d_ffn_w("d_w_up", dup)
    d_ffn_x = lambda name, dact, wmat, add, scale: _matmul(
        name, dact, wmat, a_blk=(None, tm, FF_SH), a_map=lambda i, j, k: (k, i, 0), b_blk=(None, D_MODEL, FF_SH),
        b_map=lambda i, j, k: (k, 0, 0), out_shape=(t, D_MODEL), o_blk=(tm, D_MODEL), o_map=lambda i, j: (i, 0),
        grid=(nt, 1, N_CHIPS), dims=NT, add=add, add_scale=scale)
    dx1 = d_ffn_x("d_x1_gate", dgpre, p['ffn_w_gate'], dr2, ALPHA)
    dx1 = d_ffn_x("d_x1_up", dup, p['ffn_w_up'], dx1, 1.0)
    dr1, g['ln1_g'], g['ln1_b'], g['b_out'] = _ln_bwd(dx1, s['xhat1'], s['rstd1'], _vec(p['ln1_g']))
    g['w_out'] = _matmul("d_w_out", s['mix'], dr1, a_blk=(tm, D_MODEL), a_map=lambda i, j, k: (k, 0),
                         b_blk=(tm, D_MODEL), b_map=lambda i, j, k: (k, 0), out_shape=(D_MODEL, D_MODEL),
                         o_blk=(D_MODEL, D_MODEL), o_map=lambda i, j: (0, 0), grid=(1, 1, nt), dims=TN)
    dmix = _matmul("d_mix", dr1, p['w_out'], a_blk=(tm, D_MODEL), a_map=lambda i, j, k: (i, 0),
                   b_blk=(D_MODEL, D_MODEL), b_map=lambda i, j, k: (0, 0), out_shape=(t, D_MODEL),
                   o_blk=(tm, D_MODEL), o_map=lambda i, j: (i, 0), grid=(nt, 1, 1), dims=NT)
    dya, dys, dyl, g['mix_norm_g'] = _rms_bwd(dmix, s['ya'], s['ys'], s['yl'], _vec(p['mix_norm_g']))
    dq, dk, dv, g['attn_sinks'] = _attn_bwd(s['qkv'], s['ya'], dya, s['lse'], _vec(p['attn_sinks']))
    du, dcmat, dbmat, dabar, g['s5_d'], g['s5_glu_w'], g['s5_glu_b'] = _s5_bwd(
        s['proj'], s['h5'], dys, w['bmat'], w['coef_b'], w['cmat'], _vec(p['s5_d']), w['gw'], _vec(p['s5_glu_b']))
    (dxc, dgate, g['lru_conv_w'], g['lru_conv_b'], dwx, g['lru_bx'], dwa, g['lru_ba'],
     g['lru_a_param']) = _lru_bwd(s['proj'], s['hl'], dyl, *s['lru_w'])
    dproj, g['b_in'] = _assemble_dproj(dq, dk, dv, du, dxc, dgate, cos, sin_s, p['lru_conv_w'])
    g['w_in'] = _matmul("d_w_in", s['x'], dproj, a_blk=(tm, D_MODEL), a_map=lambda i, j, k: (k, 0),
                        b_blk=(tm, IN_SH), b_map=lambda i, j, k: (k, j), out_shape=(N_CHIPS, D_MODEL, IN_SH),
                        o_blk=(None, D_MODEL, IN_SH), o_map=lambda i, j: (j, 0, 0), grid=(1, N_CHIPS, nt), dims=TN)
    dx = _matmul("d_x", dproj, p['w_in'], a_blk=(tm, IN_SH), a_map=lambda i, j, k: (i, k),
                 b_blk=(None, D_MODEL, IN_SH), b_map=lambda i, j, k: (k, 0, 0), out_shape=(t, D_MODEL),
                 o_blk=(tm, D_MODEL), o_map=lambda i, j: (i, 0), grid=(nt, 1, N_CHIPS), dims=NT,
                 add=dr1, add_scale=ALPHA)
    s5_names = ('s5_a_re', 's5_a_im', 's5_b_re', 's5_b_im', 's5_c_re', 's5_c_im', 's5_log_dt')
    _, s5_vjp = jax.vjp(_s5_prepare, *[p[n] for n in s5_names])
    for n, val in zip(s5_names, s5_vjp((dabar[0], dabar[1], dbmat, dcmat))):
        g[n] = val
    g['lru_wx'] = jax.vjp(_blockdiag_lru, p['lru_wx'])[1](dwx)[0]
    g['lru_wa'] = jax.vjp(_blockdiag_lru, p['lru_wa'])[1](dwa)[0]
    return dx, g


ROW_TILE = 512


def _pick_rows(rows):
    for rt in range(min(rows, ROW_TILE), 0, -1):
        if rows % rt == 0 and (rt % 16 == 0 or rt == rows):
            return rt
    return rows


def _cast_bf16(a):
    a2 = a.reshape(-1, a.shape[-1])
    rows, c = a2.shape
    rt = _pick_rows(rows)

    def kern(a_ref, o_ref):
        o_ref[...] = a_ref[...].astype(BF16)

    spec = pl.BlockSpec((rt, c), lambda i: (i, 0))
    out = pl.pallas_call(kern, name="cast_bf16", grid=(rows // rt,), in_specs=[spec], out_specs=spec,
                         out_shape=jax.ShapeDtypeStruct((rows, c), BF16), compiler_params=_params(("parallel",)))(a2)
    return out.reshape(a.shape)


def _sum_parts(name, parts, shape):
    c = shape[-1]
    rows = math.prod(shape[:-1])
    rt = _pick_rows(rows)
    n = len(parts)

    def kern(*refs):
        acc = refs[0][...].astype(F32)
        for r in refs[1:n]:
            acc = acc + r[...].astype(F32)
        refs[n][...] = acc

    specs, args = [], []
    for arr, j in parts:
        if j is None:
            specs.append(pl.BlockSpec((rt, c), lambda i: (i, 0)))
            args.append(arr.reshape(rows, c))
        else:
            specs.append(pl.BlockSpec((None, rt, c), functools.partial(lambda i, jj: (jj, i, 0), jj=j)))
            args.append(arr.reshape(arr.shape[0], rows, c))
    out = pl.pallas_call(kern, name=name, grid=(rows // rt,), in_specs=specs,
                         out_specs=pl.BlockSpec((rt, c), lambda i: (i, 0)),
                         out_shape=jax.ShapeDtypeStruct((rows, c), F32), compiler_params=_params(("parallel",)))(*args)
    return out.reshape(shape)


def _adamw(name, w, g, m, v):
    shape = w.shape
    c = shape[-1]
    rows = math.prod(shape[:-1])
    rt = _pick_rows(rows)

    def kern(w_ref, g_ref, m_ref, v_ref, d_ref, nm_ref, nv_ref):
        g_ = g_ref[...]
        m_ = ADAM_B1 * m_ref[...] + (1.0 - ADAM_B1) * g_
        v_ = ADAM_B2 * v_ref[...] + (1.0 - ADAM_B2) * jnp.square(g_)
        m_hat = m_ / (1.0 - ADAM_B1 ** ADAM_STEP)
        v_hat = v_ / (1.0 - ADAM_B2 ** ADAM_STEP)
        d_ref[...] = -ADAM_LR * (m_hat / (jnp.sqrt(v_hat) + ADAM_EPS) + ADAM_WD * w_ref[...])
        nm_ref[...] = m_
        nv_ref[...] = v_

    spec = pl.BlockSpec((rt, c), lambda i: (i, 0))
    outs = pl.pallas_call(kern, name=name, grid=(rows // rt,), in_specs=[spec] * 4, out_specs=[spec] * 3,
                          out_shape=[jax.ShapeDtypeStruct((rows, c), F32)] * 3,
                          compiler_params=_params(("parallel",)))(*[a.reshape(rows, c) for a in (w, g, m, v)])
    return tuple(o.reshape(shape) for o in outs)


def _position():
    return lax.axis_index("x"), lax.axis_index("y"), lax.axis_index("c")


def _other_chips(x, y):
    return [(1 - x, y), (x, 1 - y), (1 - x, 1 - y)]


def _comm_call(name, kern, arrs, out_shapes, n_remote, n_local):
    return pl.pallas_call(
        kern, name=name, in_specs=[ANY] * len(arrs), out_specs=[ANY] * len(out_shapes), out_shape=out_shapes,
        scratch_shapes=[pltpu.SemaphoreType.DMA((n_remote,)), pltpu.SemaphoreType.DMA((n_remote,)),
                        pltpu.SemaphoreType.DMA((n_local,))],
    )(*arrs)


def _allgather_chips(arrs):
    n = len(arrs)

    def kern(*refs):
        ins, outs = refs[:n], refs[n:2 * n]
        send, recv, loc = refs[2 * n:]
        x, y, c = _position()
        me = 2 * x + y
        chips = _other_chips(x, y)
        own, sent = [], []
        for t in range(n):
            own.append(pltpu.make_async_copy(ins[t], outs[t].at[:, pl.ds(me, 1)], loc.at[t]))
            own[-1].start()
            for j, (px, py) in enumerate(chips):
                sent.append(pltpu.make_async_remote_copy(
                    src_ref=ins[t], dst_ref=outs[t].at[:, pl.ds(me, 1)], send_sem=send.at[3 * t + j],
                    recv_sem=recv.at[3 * t + j], device_id=(px, py, c), device_id_type=MESH))
                sent[-1].start()
        for t in range(n):
            for j, (px, py) in enumerate(chips):
                pltpu.make_async_remote_copy(
                    src_ref=ins[t], dst_ref=outs[t].at[:, pl.ds(2 * px + py, 1)], send_sem=send.at[3 * t + j],
                    recv_sem=recv.at[3 * t + j], device_id=(px, py, c), device_id_type=MESH).wait_recv()
        for cp in sent:
            cp.wait_send()
        for cp in own:
            cp.wait()

    outs = [jax.ShapeDtypeStruct((a.shape[0], N_CHIPS) + a.shape[2:], a.dtype) for a in arrs]
    return _comm_call("allgather_chips", kern, arrs, outs, 3 * n, n)


def _pair_exchange(arrs):
    n = len(arrs)

    def kern(*refs):
        ins, outs = refs[:n], refs[n:3 * n]
        send, recv, loc = refs[3 * n:]
        x, y, c = _position()
        own, sent = [], []
        for t in range(n):
            r2 = ins[t].shape[2] // 2
            own.append(pltpu.make_async_copy(ins[t].at[:, :, pl.ds(c * r2, r2)], outs[2 * t], loc.at[t]))
            own[-1].start()
            sent.append(pltpu.make_async_remote_copy(
                src_ref=ins[t].at[:, :, pl.ds((1 - c) * r2, r2)], dst_ref=outs[2 * t + 1], send_sem=send.at[t],
                recv_sem=recv.at[t], device_id=(x, y, 1 - c), device_id_type=MESH))
            sent[-1].start()
        for cp in sent:
            cp.wait()
        for cp in own:
            cp.wait()

    outs = []
    for a in arrs:
        half = jax.ShapeDtypeStruct(a.shape[:2] + (a.shape[2] // 2, a.shape[3]), a.dtype)
        outs += [half, half]
    return _comm_call("pair_exchange", kern, arrs, outs, n, n)


def _chip_scatter(arrs):
    n = len(arrs)

    def kern(*refs):
        ins, outs = refs[:n], refs[n:3 * n]
        send, recv, loc = refs[3 * n:]
        x, y, c = _position()
        me = 2 * x + y
        chips = _other_chips(x, y)
        own, sent = [], []
        for t in range(n):
            own.append(pltpu.make_async_copy(ins[t].at[:, pl.ds(me, 1)], outs[2 * t], loc.at[t]))
            own[-1].start()
            for j, (px, py) in enumerate(chips):
                sent.append(pltpu.make_async_remote_copy(
                    src_ref=ins[t].at[:, pl.ds(2 * px + py, 1)], dst_ref=outs[2 * t + 1].at[j],
                    send_sem=send.at[3 * t + j], recv_sem=recv.at[3 * t + j], device_id=(px, py, c),
                    device_id_type=MESH))
                sent[-1].start()
        for cp in sent:
            cp.wait()
        for cp in own:
            cp.wait()

    outs = []
    for a in arrs:
        one = (a.shape[0], 1) + a.shape[2:]
        outs += [jax.ShapeDtypeStruct(one, a.dtype), jax.ShapeDtypeStruct((3,) + one, a.dtype)]
    return _comm_call("chip_scatter", kern, arrs, outs, 3 * n, n)


def _pair_gather(arrs):
    n = len(arrs)

    def kern(*refs):
        ins, outs = refs[:n], refs[n:2 * n]
        send, recv, loc = refs[2 * n:]
        x, y, c = _position()
        own, sent = [], []
        for t in range(n):
            own.append(pltpu.make_async_copy(ins[t], outs[t].at[:, pl.ds(c, 1)], loc.at[t]))
            own[-1].start()
            sent.append(pltpu.make_async_remote_copy(
                src_ref=ins[t], dst_ref=outs[t].at[:, pl.ds(c, 1)], send_sem=send.at[t], recv_sem=recv.at[t],
                device_id=(x, y, 1 - c), device_id_type=MESH))
            sent[-1].start()
        for t in range(n):
            sent[t].wait_send()
            pltpu.make_async_remote_copy(
                src_ref=ins[t], dst_ref=outs[t].at[:, pl.ds(1 - c, 1)], send_sem=send.at[t], recv_sem=recv.at[t],
                device_id=(x, y, 1 - c), device_id_type=MESH).wait_recv()
        for cp in own:
            cp.wait()

    outs = [jax.ShapeDtypeStruct((a.shape[0], 2) + a.shape[2:], a.dtype) for a in arrs]
    return _comm_call("pair_gather", kern, arrs, outs, n, n)


_FLIPS = [(0, 0, 1), (1, 0, 0), (0, 1, 0), (1, 1, 0), (1, 0, 1), (0, 1, 1), (1, 1, 1)]


def _allgather_devices(v):
    def kern(v_ref, o_ref, send, recv, loc):
        x, y, c = _position()
        me = 4 * x + 2 * y + c
        peers = [((1 - x) if fx else x, (1 - y) if fy else y, (1 - c) if fc else c) for fx, fy, fc in _FLIPS]
        own = pltpu.make_async_copy(v_ref, o_ref.at[pl.ds(me, 1)], loc.at[0])
        own.start()
        sent = []
        for k, peer in enumerate(peers):
            sent.append(pltpu.make_async_remote_copy(
                src_ref=v_ref, dst_ref=o_ref.at[pl.ds(me, 1)], send_sem=send.at[k], recv_sem=recv.at[k],
                device_id=peer, device_id_type=MESH))
            sent[-1].start()
        for k, (px, py, pc) in enumerate(peers):
            pltpu.make_async_remote_copy(
                src_ref=v_ref, dst_ref=o_ref.at[pl.ds(4 * px + 2 * py + pc, 1)], send_sem=send.at[k],
                recv_sem=recv.at[k], device_id=(px, py, pc), device_id_type=MESH).wait_recv()
        for cp in sent:
            cp.wait_send()
        own.wait()

    out = jax.ShapeDtypeStruct((N_DEV,) + v.shape[1:], v.dtype)
    return _comm_call("allgather_devices", kern, [v], [out], len(_FLIPS), 1)[0]


def _exchange(name, arrs, out_shapes, n_local, n_remote, plan):
    n_in, n_out = len(arrs), len(out_shapes)

    def kern(*refs):
        ins, outs = refs[:n_in], refs[n_in:n_in + n_out]
        send, recv, loc = refs[n_in + n_out:]
        local, remote = plan(ins, outs, *_position())
        assert len(local) == n_local and len(remote) == n_remote
        own = [pltpu.make_async_copy(s, d, loc.at[k]) for k, (s, d) in enumerate(local)]
        for cp in own:
            cp.start()
        sent = [pltpu.make_async_remote_copy(src_ref=s, dst_ref=d, send_sem=send.at[k], recv_sem=recv.at[k],
                                             device_id=peer, device_id_type=MESH)
                for k, (s, d, peer, _) in enumerate(remote)]
        for cp in sent:
            cp.start()
        for k, (s, _, peer, landing) in enumerate(remote):
            pltpu.make_async_remote_copy(src_ref=s, dst_ref=landing, send_sem=send.at[k], recv_sem=recv.at[k],
                                         device_id=peer, device_id_type=MESH).wait_recv()
        for cp in sent:
            cp.wait_send()
        for cp in own:
            cp.wait()

    return pl.pallas_call(
        kern, name=name, in_specs=[ANY] * n_in, out_specs=[ANY] * n_out, out_shape=out_shapes,
        scratch_shapes=[pltpu.SemaphoreType.DMA((n_remote,)), pltpu.SemaphoreType.DMA((n_remote,)),
                        pltpu.SemaphoreType.DMA((max(n_local, 1),))],
    )(*arrs)


def _allgather_chips(arrs, halved=()):
    n = len(arrs)
    layers = arrs[0].shape[0]

    def plan(ins, outs, x, y, c):
        me = 2 * x + y
        local, remote = [], []
        for t in range(n):
            for l in range(layers):
                src = ins[t].at[l]
                if t in halved:
                    r2 = ins[t].shape[2] // 2
                    src = ins[t].at[l, :, pl.ds(c * r2, r2)]
                local.append((src, outs[t].at[l, pl.ds(me, 1)]))
                for px, py in _other_chips(x, y):
                    remote.append((src, outs[t].at[l, pl.ds(me, 1)], (px, py, c),
                                   outs[t].at[l, pl.ds(2 * px + py, 1)]))
        return local, remote

    outs = []
    for t, a in enumerate(arrs):
        tail = (a.shape[2] // 2,) + a.shape[3:] if t in halved else a.shape[2:]
        outs.append(jax.ShapeDtypeStruct((a.shape[0], N_CHIPS) + tail, a.dtype))
    return _exchange("allgather_chips", arrs, outs, n * layers, 3 * n * layers, plan)


def _pair_exchange(arrs):
    n = len(arrs)
    layers, shards = arrs[0].shape[:2]

    def plan(ins, outs, x, y, c):
        local, remote = [], []
        for t in range(n):
            r2 = ins[t].shape[2] // 2
            for l in range(layers):
                for s in range(shards):
                    local.append((ins[t].at[l, s, pl.ds(c * r2, r2)], outs[2 * t].at[l, s]))
                    remote.append((ins[t].at[l, s, pl.ds((1 - c) * r2, r2)], outs[2 * t + 1].at[l, s],
                                   (x, y, 1 - c), outs[2 * t + 1].at[l, s]))
        return local, remote

    outs = []
    for a in arrs:
        half = jax.ShapeDtypeStruct(a.shape[:2] + (a.shape[2] // 2, a.shape[3]), a.dtype)
        outs += [half, half]
    return _exchange("pair_exchange", arrs, outs, n * layers * shards, n * layers * shards, plan)


def _chip_scatter(arrs):
    n = len(arrs)
    layers = arrs[0].shape[0]

    def plan(ins, outs, x, y, c):
        me = 2 * x + y
        local, remote = [], []
        for t in range(n):
            for l in range(layers):
                local.append((ins[t].at[l, pl.ds(me, 1)], outs[2 * t].at[l]))
                for j, (px, py) in enumerate(_other_chips(x, y)):
                    remote.append((ins[t].at[l, pl.ds(2 * px + py, 1)], outs[2 * t + 1].at[j, l], (px, py, c),
                                   outs[2 * t + 1].at[j, l]))
        return local, remote

    outs = []
    for a in arrs:
        one = (a.shape[0], 1) + a.shape[2:]
        outs += [jax.ShapeDtypeStruct(one, a.dtype), jax.ShapeDtypeStruct((3,) + one, a.dtype)]
    return _exchange("chip_scatter", arrs, outs, n * layers, 3 * n * layers, plan)


def _pair_gather(arrs):
    n = len(arrs)
    layers = arrs[0].shape[0]

    def plan(ins, outs, x, y, c):
        local, remote = [], []
        for t in range(n):
            for l in range(layers):
                local.append((ins[t].at[l], outs[t].at[l, pl.ds(c, 1)]))
                remote.append((ins[t].at[l], outs[t].at[l, pl.ds(c, 1)], (x, y, 1 - c),
                               outs[t].at[l, pl.ds(1 - c, 1)]))
        return local, remote

    outs = [jax.ShapeDtypeStruct((a.shape[0], 2) + a.shape[2:], a.dtype) for a in arrs]
    return _exchange("pair_gather", arrs, outs, n * layers, n * layers, plan)


GATHER_PIECES = 4


def _allgather_devices(v):
    rq = v.shape[1] // GATHER_PIECES

    def plan(ins, outs, x, y, c):
        me = 4 * x + 2 * y + c
        local, remote = [], []
        for q in range(GATHER_PIECES):
            rows = pl.ds(q * rq, rq)
            local.append((ins[0].at[0, rows], outs[0].at[me, rows]))
            for fx, fy, fc in _FLIPS:
                px, py, pc = (1 - x) if fx else x, (1 - y) if fy else y, (1 - c) if fc else c
                remote.append((ins[0].at[0, rows], outs[0].at[me, rows], (px, py, pc),
                               outs[0].at[4 * px + 2 * py + pc, rows]))
        return local, remote

    out = jax.ShapeDtypeStruct((N_DEV,) + v.shape[1:], v.dtype)
    return _exchange("allgather_devices", [v], [out], GATHER_PIECES, GATHER_PIECES * len(_FLIPS), plan)[0]


WEIGHTS = ['w_in', 'b_in', 'attn_sinks', 's5_a_re', 's5_a_im', 's5_b_re', 's5_b_im', 's5_c_re', 's5_c_im', 's5_d',
           's5_log_dt', 's5_glu_w', 's5_glu_b', 'lru_conv_w', 'lru_conv_b', 'lru_wx', 'lru_bx', 'lru_wa', 'lru_ba',
           'lru_a_param', 'mix_norm_g', 'w_out', 'b_out', 'ln1_g', 'ln1_b', 'ffn_w_gate', 'ffn_w_up', 'ffn_conv_w',
           'ffn_conv_b', 'ffn_w_down', 'ln2_g', 'ln2_b']
BIG = ('w_in', 'w_out', 'ffn_w_gate', 'ffn_w_up', 'ffn_w_down')
SMALL = tuple(n for n in WEIGHTS if n not in BIG)
PACK_ROWS = ROW_TILE


def _pack(arrs):
    flat = jnp.concatenate([a.reshape(-1) for a in arrs])
    unit = 128 * PACK_ROWS
    size = -(-flat.shape[0] // unit) * unit
    return jnp.pad(flat, (0, size - flat.shape[0])).reshape(-1, 128)


def _unpack(packed, shapes):
    flat = packed.reshape(-1)
    out, pos = [], 0
    for shp in shapes:
        n = math.prod(shp)
        out.append(flat[pos:pos + n].reshape(shp))
        pos += n
    return out


def _pair_reduce(name, g):
    layers, shards, rows, cols = g.shape
    r2 = rows // 2
    rt = _pick_rows(r2)
    nr = r2 // rt
    nsteps = layers * shards * nr

    def kern(c_ref, mine_ref, other_ref, o_ref, buf, send, recv, credit):
        x, y, c = _position()
        sibling = (x, y, 1 - c)
        k = pl.program_id(0) * nr + pl.program_id(1)
        slot = k % 2

        @pl.when(k >= 2)
        def _():
            pl.semaphore_wait(credit, 1)

        cp = pltpu.make_async_remote_copy(src_ref=other_ref, dst_ref=buf.at[slot], send_sem=send.at[slot],
                                          recv_sem=recv.at[slot], device_id=sibling, device_id_type=MESH)
        cp.start()
        cp.wait_recv()
        o_ref[...] = (mine_ref[...] + buf[slot]).astype(BF16)
        cp.wait_send()

        @pl.when(k + 2 < nsteps)
        def _():
            pl.semaphore_signal(credit, 1, device_id=sibling, device_id_type=MESH)

    blk = (1, rt, cols)
    grid_spec = pltpu.PrefetchScalarGridSpec(
        num_scalar_prefetch=1, grid=(layers * shards, nr),
        in_specs=[pl.BlockSpec(blk, lambda m, r, c_ref: (m, c_ref[0] * nr + r, 0)),
                  pl.BlockSpec(blk, lambda m, r, c_ref: (m, (1 - c_ref[0]) * nr + r, 0))],
        out_specs=pl.BlockSpec(blk, lambda m, r, c_ref: (m, r, 0)),
        scratch_shapes=[pltpu.VMEM((2,) + blk, F32), pltpu.SemaphoreType.DMA((2,)),
                        pltpu.SemaphoreType.DMA((2,)), pltpu.SemaphoreType.REGULAR])
    core = lax.axis_index("c").astype(jnp.int32).reshape(1)
    g3 = g.reshape(layers * shards, rows, cols)
    out = pl.pallas_call(
        kern, name=name, grid_spec=grid_spec,
        out_shape=jax.ShapeDtypeStruct((layers * shards, r2, cols), BF16),
        compiler_params=_params(("arbitrary", "arbitrary")),
    )(core, g3, g3)
    return out.reshape(layers, shards, r2, cols)


def _pair_merge(name, h):
    m, r2, cols = h.shape
    rt = _pick_rows(r2)
    nr = r2 // rt
    nsteps = m * nr

    def kern(h_ref, o_ref, buf, send, recv, credit):
        x, y, c = _position()
        sibling = (x, y, 1 - c)
        k = pl.program_id(0) * nr + pl.program_id(1)
        slot = k % 2

        @pl.when(k >= 2)
        def _():
            pl.semaphore_wait(credit, 1)

        cp = pltpu.make_async_remote_copy(src_ref=h_ref, dst_ref=buf.at[slot], send_sem=send.at[slot],
                                          recv_sem=recv.at[slot], device_id=sibling, device_id_type=MESH)
        cp.start()
        cp.wait_recv()
        o_ref[0, pl.ds(c, 1)] = h_ref[...]
        o_ref[0, pl.ds(1 - c, 1)] = buf[slot]
        cp.wait_send()

        @pl.when(k + 2 < nsteps)
        def _():
            pl.semaphore_signal(credit, 1, device_id=sibling, device_id_type=MESH)

    blk = (1, rt, cols)
    out = pl.pallas_call(
        kern, name=name, grid=(m, nr),
        in_specs=[pl.BlockSpec(blk, lambda i, r: (i, r, 0))],
        out_specs=pl.BlockSpec((1, 2, rt, cols), lambda i, r: (i, 0, r, 0)),
        out_shape=jax.ShapeDtypeStruct((m, 2, r2, cols), h.dtype),
        scratch_shapes=[pltpu.VMEM((2,) + blk, h.dtype), pltpu.SemaphoreType.DMA((2,)),
                        pltpu.SemaphoreType.DMA((2,)), pltpu.SemaphoreType.REGULAR],
        compiler_params=_params(("arbitrary", "arbitrary")),
    )(h)
    return out.reshape(m, 2 * r2, cols)


def _reduce_big(grads):
    pair = [_pair_reduce("pair_reduce_" + n, g) for n, g in zip(BIG, grads)]
    scat = _chip_scatter(pair)
    out = []
    for t, n in enumerate(BIG):
        own, got = scat[2 * t], scat[2 * t + 1]
        half = _sum_parts("chip_sum", [(own, None)] + [(got, j) for j in range(3)], own.shape)
        out.append(_pair_merge("grad_merge_" + n, half.reshape(half.shape[0], half.shape[2], half.shape[3])))
    return out


def _step(a):
    x = a['x'][0]
    target = a['loss_target'][0]
    t = x.shape[0]
    xi, yi, _ = _position()
    chip = 2 * xi + yi
    cos, sin_s = _rope_tables(t)

    gathered = _allgather_chips([_cast_bf16(a[n])[:, None] for n in BIG]
                                + [a[n][:, None] for n in ('s5_glu_w', 'lru_conv_w', 'ffn_conv_w')],
                                halved=range(len(BIG)))
    full = dict(zip(BIG + ('s5_glu_w', 'lru_conv_w', 'ffn_conv_w'), gathered))
    for n in BIG:
        layers, chips, r2, cols = full[n].shape
        full[n] = _pair_merge("weight_merge_" + n, full[n].reshape(layers * chips, r2, cols)).reshape(
            layers, chips, 2 * r2, cols)

    def layer_params(l):
        p = {n: a[n][l] for n in SMALL}
        p['layer'] = l
        p['w_in'] = full['w_in']
        p['w_out'] = full['w_out'].reshape(DEPTH, D_MODEL, D_MODEL)
        p['ffn_w_gate'] = full['ffn_w_gate']
        p['ffn_w_up'] = full['ffn_w_up']
        p['ffn_w_down'] = full['ffn_w_down']
        p['s5_glu_w'] = full['s5_glu_w'][l].reshape(D_S5, D_S5)
        p['lru_conv_w'] = full['lru_conv_w'][l].transpose(1, 0, 2).reshape(LRU_CONV, D_LRU)
        p['ffn_conv_w'] = full['ffn_conv_w'][l]
        p['ffn_conv_b'] = a['ffn_conv_b'][l].reshape(N_CHIPS, 1, FF_SH)
        return p

    params = [layer_params(l) for l in range(DEPTH)]
    derived = [_layer_weights(p) for p in params]
    saved = []
    h, hb = x, _cast_bf16(x)
    for l in range(DEPTH):
        h, hb, s = _layer_fwd(h, hb, params[l], derived[l], cos, sin_s)
        saved.append(s)
    loss_part, dr, drb, ln2_g, ln2_b, _ = _loss_head(h, target, saved[-1]['xhat2'], saved[-1]['rstd2'],
                                                     _vec(params[-1]['ln2_g']))
    loss = lax.psum(loss_part[0, 0], ("x", "y", "c"))
    grads = [None] * DEPTH
    big = {n: lax.empty((DEPTH, N_CHIPS) + a[n].shape[1:], F32) for n in BIG}
    big['w_out'] = big['w_out'].reshape(DEPTH, D_MODEL, D_MODEL)
    for l in reversed(range(DEPTH)):
        below = (saved[l - 1], params[l - 1]) if l > 0 else None
        out, grads[l], big = _layer_bwd(dr, drb, saved[l], params[l], derived[l], cos, sin_s, big, below)
        grads[l]['ln2_g'], grads[l]['ln2_b'] = ln2_g, ln2_b
        if l > 0:
            dr, drb, ln2_g, ln2_b, _ = out
        else:
            grad_x = out[None]

    def stacked(n):
        return jnp.stack([grads[l][n] for l in range(DEPTH)])

    big['w_out'] = big['w_out'].reshape(DEPTH, N_CHIPS, OUT_SH, D_MODEL)
    grad = dict(zip(BIG, _reduce_big([big[n] for n in BIG])))
    small_local = [stacked(n) for n in SMALL]
    packed = _allgather_devices(_pack(small_local)[None])
    total = _sum_parts("device_sum", [(packed, j) for j in range(N_DEV)], packed.shape[1:])
    small_sum = dict(zip(SMALL, _unpack(total, [g.shape for g in small_local])))
    for n in SMALL:
        g = small_sum[n]
        if n == 's5_glu_w':
            g = lax.dynamic_slice_in_dim(g, chip * (D_S5 // N_CHIPS), D_S5 // N_CHIPS, axis=1)
        elif n == 'lru_conv_w':
            g = lax.dynamic_slice_in_dim(g, chip * (D_LRU // N_CHIPS), D_LRU // N_CHIPS, axis=2)
        elif n == 'ffn_conv_w':
            g = lax.dynamic_index_in_dim(g, chip, axis=1, keepdims=False)
        grad[n] = g.reshape(a[n].shape)

    delta, new_m, new_v = {}, {}, {}
    for n in BIG:
        delta[n], new_m[n], new_v[n] = _adamw("adamw_" + n, a[n], grad[n], a['m_' + n], a['v_' + n])
    shapes = [a[n].shape for n in SMALL]
    outs = _adamw("adamw_small", _pack([a[n] for n in SMALL]), _pack([grad[n] for n in SMALL]),
                  _pack([a['m_' + n] for n in SMALL]), _pack([a['v_' + n] for n in SMALL]))
    for res, o in zip((delta, new_m, new_v), outs):
        res.update(zip(SMALL, _unpack(o, shapes)))
    return (loss, grad_x, *[grad[n] for n in WEIGHTS], *[delta[n] for n in WEIGHTS],
            *[new_m[n] for n in WEIGHTS], *[new_v[n] for n in WEIGHTS])


def kernel(x, w_in, b_in, attn_sinks, s5_a_re, s5_a_im, s5_b_re, s5_b_im, s5_c_re, s5_c_im, s5_d, s5_log_dt, s5_glu_w, s5_glu_b, lru_conv_w, lru_conv_b, lru_wx, lru_bx, lru_wa, lru_ba, lru_a_param, mix_norm_g, w_out, b_out, ln1_g, ln1_b, ffn_w_gate, ffn_w_up, ffn_conv_w, ffn_conv_b, ffn_w_down, ln2_g, ln2_b, loss_target, m_w_in, m_b_in, m_attn_sinks, m_s5_a_re, m_s5_a_im, m_s5_b_re, m_s5_b_im, m_s5_c_re, m_s5_c_im, m_s5_d, m_s5_log_dt, m_s5_glu_w, m_s5_glu_b, m_lru_conv_w, m_lru_conv_b, m_lru_wx, m_lru_bx, m_lru_wa, m_lru_ba, m_lru_a_param, m_mix_norm_g, m_w_out, m_b_out, m_ln1_g, m_ln1_b, m_ffn_w_gate, m_ffn_w_up, m_ffn_conv_w, m_ffn_conv_b, m_ffn_w_down, m_ln2_g, m_ln2_b, v_w_in, v_b_in, v_attn_sinks, v_s5_a_re, v_s5_a_im, v_s5_b_re, v_s5_b_im, v_s5_c_re, v_s5_c_im, v_s5_d, v_s5_log_dt, v_s5_glu_w, v_s5_glu_b, v_lru_conv_w, v_lru_conv_b, v_lru_wx, v_lru_bx, v_lru_wa, v_lru_ba, v_lru_a_param, v_mix_norm_g, v_w_out, v_b_out, v_ln1_g, v_ln1_b, v_ffn_w_gate, v_ffn_w_up, v_ffn_conv_w, v_ffn_conv_b, v_ffn_w_down, v_ln2_g, v_ln2_b):
    return _step(dict(locals()))
```

```python
import functools
import math

import jax
import jax.numpy as jnp
from jax import lax
from jax.experimental import pallas as pl
from jax.experimental.pallas import tpu as pltpu

F32 = jnp.float32
BF16 = jnp.bfloat16
MESH = pl.DeviceIdType.MESH
ANY = pl.BlockSpec(memory_space=pl.ANY)

D_MODEL = 1024
DEPTH = 4
HEAD_DIM = 64
N_Q_HEADS = 8
N_KV_HEADS = 2
Q_PER_KV = 4
D_ATTN = 512
D_KV = 128
ATTN_BLOCK = 128
ROPE_THETA = 10000.0
D_S5 = 256
S5_GROUP = 16
S5_GROUPS = 16
S5_STATE = 64
N_STATE = S5_GROUPS * S5_STATE
D_LRU = 256
LRU_HEADS = 4
LRU_HEAD_DIM = 64
LRU_CONV = 4
LRU_C = 8.0
D_IN = 1536
D_FF = 2816
FFN_CONV = 3
N_CHIPS = 4
N_DEV = 8
IN_SH = D_IN // N_CHIPS
FF_SH = D_FF // N_CHIPS
OUT_SH = D_MODEL // N_CHIPS
ALPHA = (2 * DEPTH) ** 0.25
LN_EPS = 1e-5
RMS_EPS = 1e-6
ADAM_LR = 0.001
ADAM_B1 = 0.9
ADAM_B2 = 0.999
ADAM_EPS = 1e-08
ADAM_WD = 0.01
ADAM_STEP = 10

SUBLANES = 8
VMEM_MB = 56


def _params(sem):
    return pltpu.CompilerParams(dimension_semantics=sem, vmem_limit_bytes=VMEM_MB << 20)


def _row_tile(t, pref):
    return min(t, pref)


def _matmul(name, a, b, *, a_blk, a_map, b_blk, b_map, out_shape, o_blk, o_map, grid, dims,
            out_dtype=F32, bias=None, bias_blk=None, bias_map=None, add=None, add_scale=1.0, pair2=None,
            into=None, ln_bwd=None):
    nk = grid[2]
    acc_shape = tuple(d for d in o_blk if d is not None)
    n_in = 2 if pair2 is None else 4

    def kern(*refs):
        p = n_in
        bias_ref = add_ref = None
        if bias is not None:
            bias_ref = refs[p]
            p += 1
        if add is not None:
            add_ref = refs[p]
            p += 1
        if into is not None:
            p += 1
        if ln_bwd is not None:
            ln_in = refs[p:p + 3]
            ln_out = refs[p + 4:p + 8]
            o_ref, acc = refs[p + 3], refs[p + 8]
        else:
            o_ref, acc = refs[p], refs[p + 1]
        k = pl.program_id(2)
        first_tile = pl.program_id(0) == 0

        def product():
            r = _dot(refs[0][...].astype(BF16), refs[1][...].astype(BF16), dims)
            if pair2 is not None:
                r = r + _dot(refs[2][...].astype(BF16), refs[3][...].astype(BF16), dims)
            return r

        def finish(r):
            if bias_ref is not None:
                r = r + bias_ref[...]
            if add_ref is not None:
                r = r + add_scale * add_ref[...]
            if ln_bwd is None:
                o_ref[...] = r.astype(out_dtype)
            else:
                @pl.when(first_tile)
                def _():
                    for ref in ln_out[1:]:
                        ref[...] = jnp.zeros_like(ref)

                _ln_bwd_tile(r, ln_in[0][...], ln_in[1][...], ln_in[2][...], o_ref, *ln_out)

        if nk == 1:
            finish(product())
        else:
            @pl.when(k == 0)
            def _():
                acc[...] = jnp.zeros_like(acc)

            acc[...] += product()

            @pl.when(k == nk - 1)
            def _():
                finish(acc[...])

    in_specs = [pl.BlockSpec(a_blk, a_map), pl.BlockSpec(b_blk, b_map)]
    args = [a, b]
    if pair2 is not None:
        in_specs += [pl.BlockSpec(a_blk, a_map), pl.BlockSpec(b_blk, b_map)]
        args += list(pair2)
    if bias is not None:
        in_specs.append(pl.BlockSpec(bias_blk, bias_map))
        args.append(bias)
    if add is not None:
        in_specs.append(pl.BlockSpec(o_blk, lambda i, j, k: o_map(i, j)))
        args.append(add)
    aliases = {}
    if into is not None:
        aliases = {len(args): 0}
        in_specs.append(ANY)
        args.append(into)
    o_spec = pl.BlockSpec(o_blk, lambda i, j, k: o_map(i, j))
    out_specs, out_shapes = o_spec, jax.ShapeDtypeStruct(out_shape, out_dtype)
    semantics = ("parallel", "parallel", "arbitrary")
    if ln_bwd is not None:
        vec = pl.BlockSpec((1, o_blk[-1]), lambda i, j, k: (0, 0))
        in_specs += [o_spec, pl.BlockSpec((o_blk[0], 1), lambda i, j, k: (i, 0)), vec]
        args += list(ln_bwd)
        vshape = jax.ShapeDtypeStruct((1, o_blk[-1]), F32)
        out_specs = [o_spec, o_spec, vec, vec, vec]
        out_shapes = [out_shapes, jax.ShapeDtypeStruct(out_shape, BF16), vshape, vshape, vshape]
        semantics = ("arbitrary", "arbitrary", "arbitrary")
    return pl.pallas_call(
        kern, name=name, grid=grid, in_specs=in_specs, out_specs=out_specs, out_shape=out_shapes,
        scratch_shapes=[pltpu.VMEM(acc_shape if nk > 1 else (SUBLANES, 128), F32)],
        input_output_aliases=aliases,
        compiler_params=_params(semantics),
    )(*args)


NN = ((1,), (0,))
NT = ((1,), (1,))
TN = ((0,), (0,))
TM = 512


def _sigmoid(x):
    return 0.5 * jnp.tanh(0.5 * x) + 0.5


_GELU_C = math.sqrt(2.0 / math.pi)


def _gelu(x):
    return 0.5 * x * (1.0 + jnp.tanh(_GELU_C * (x + 0.044715 * x * x * x)))


def _gelu_grad(x):
    th = jnp.tanh(_GELU_C * (x + 0.044715 * x * x * x))
    return 0.5 * (1.0 + th) + 0.5 * x * (1.0 - th * th) * _GELU_C * (1.0 + 3 * 0.044715 * x * x)


def _rope_swap(t):
    lane = lax.broadcasted_iota(jnp.int32, t.shape, 1)
    lo = (lane % HEAD_DIM) < (HEAD_DIM // 2)
    return jnp.where(lo, pltpu.roll(t, 128 - HEAD_DIM // 2, 1), pltpu.roll(t, HEAD_DIM // 2, 1))


D_QKV = D_ATTN + 2 * D_KV
TMM = 1024


def _in_proj(xb, w_in, b_in, cos, sin_s, layer):
    t = xb.shape[0]
    tm = _row_tile(t, TMM)

    def kern(x_ref, w_ref, b_ref, c_ref, s_ref, q_ref, u_ref):
        x = x_ref[...]
        c = c_ref[...]
        s = s_ref[...]
        for j in range(N_CHIPS):
            pj = _dot(x, w_ref[j], NN) + b_ref[:, j * IN_SH:(j + 1) * IN_SH]
            for ch in range(IN_SH // 128):
                col = j * IN_SH + ch * 128
                v = pj[:, ch * 128:(ch + 1) * 128]
                if col < D_ATTN + D_KV:
                    v = v * c + _rope_swap(v) * s
                if col < D_ATTN:
                    v = v * (HEAD_DIM ** -0.5)
                if col < D_QKV:
                    q_ref[:, col:col + 128] = v.astype(BF16)
                else:
                    u_ref[:, col - D_QKV:col - D_QKV + 128] = v

    row = lambda w: pl.BlockSpec((tm, w), lambda i: (i, 0))
    return pl.pallas_call(
        kern, name="in_proj", grid=(t // tm,),
        in_specs=[row(D_MODEL), pl.BlockSpec((None, N_CHIPS, D_MODEL, IN_SH), lambda i: (layer, 0, 0, 0)),
                  pl.BlockSpec((1, D_IN), lambda i: (0, 0)), row(128), row(128)],
        out_specs=[row(D_QKV), row(D_IN - D_QKV)],
        out_shape=[jax.ShapeDtypeStruct((t, D_QKV), BF16), jax.ShapeDtypeStruct((t, D_IN - D_QKV), F32)],
        compiler_params=_params(("parallel",)),
    )(xb, w_in, b_in, cos, sin_s)


def _attn_mask(i):
    qi = lax.broadcasted_iota(jnp.int32, (ATTN_BLOCK, 2 * ATTN_BLOCK), 0)
    si = lax.broadcasted_iota(jnp.int32, (ATTN_BLOCK, 2 * ATTN_BLOCK), 1)
    diff = qi + ATTN_BLOCK - si
    return (diff >= 0) & (diff < ATTN_BLOCK) & ((si >= ATTN_BLOCK) | (i > 0))


def _row_sums(x, ones):
    hi = x.astype(BF16)
    lo = (x - hi.astype(F32)).astype(BF16)
    return _dot(hi, ones, NN) + _dot(lo, ones, NN)


def _attn_fwd(qkv, sinks):
    t = qkv.shape[0]
    nb = t // ATTN_BLOCK

    def kern(q_ref, kp_ref, kc_ref, vp_ref, vc_ref, s_ref, o_ref, l_ref):
        i = pl.program_id(0)
        valid = _attn_mask(i)
        ones = jnp.ones((2 * ATTN_BLOCK, 128), BF16)
        kband = jnp.concatenate([kp_ref[...], kc_ref[...]], axis=0)
        vband = jnp.concatenate([vp_ref[...], vc_ref[...]], axis=0)
        ks = [kband[:, kh * HEAD_DIM:(kh + 1) * HEAD_DIM] for kh in range(N_KV_HEADS)]
        vs = [vband[:, kh * HEAD_DIM:(kh + 1) * HEAD_DIM] for kh in range(N_KV_HEADS)]
        scores = [_dot(q_ref[:, h * HEAD_DIM:(h + 1) * HEAD_DIM], ks[h // Q_PER_KV], NT) for h in range(N_Q_HEADS)]
        probs, lses = [], []
        for h in range(N_Q_HEADS):
            s = jnp.where(valid, scores[h], -jnp.inf)
            sink = s_ref[0:1, h:h + 1]
            m = jnp.maximum(jnp.max(s, axis=-1, keepdims=True), sink)
            e = jnp.exp(s - m)
            denom = _row_sums(e, ones)[:, 0:1] + jnp.exp(sink - m)
            probs.append((e * (1.0 / denom)).astype(BF16))
            lses.append(m + jnp.log(denom))
        outs = [_dot(probs[h], vs[h // Q_PER_KV], NN) for h in range(N_Q_HEADS)]
        for h in range(N_Q_HEADS):
            o_ref[:, h * HEAD_DIM:(h + 1) * HEAD_DIM] = outs[h]
            l_ref[:, h:h + 1] = lses[h]

    blk = lambda w, f: pl.BlockSpec((ATTN_BLOCK, w), f)
    return pl.pallas_call(
        kern, name="attn_fwd", grid=(nb,),
        in_specs=[blk(512, lambda i: (i, 0)),
                  blk(128, lambda i: (jnp.maximum(i - 1, 0), 4)), blk(128, lambda i: (i, 4)),
                  blk(128, lambda i: (jnp.maximum(i - 1, 0), 5)), blk(128, lambda i: (i, 5)),
                  pl.BlockSpec((1, N_Q_HEADS), lambda i: (0, 0))],
        out_specs=[blk(512, lambda i: (i, 0)), blk(N_Q_HEADS, lambda i: (i, 0))],
        out_shape=[jax.ShapeDtypeStruct((t, D_ATTN), F32), jax.ShapeDtypeStruct((t, N_Q_HEADS), F32)],
        compiler_params=_params(("parallel",)),
    )(qkv, qkv, qkv, qkv, qkv, sinks)


def _attn_bwd(qkv, o, do, lse, sinks):
    t = qkv.shape[0]
    nb = t // ATTN_BLOCK

    def kern(q_ref, kp_ref, kc_ref, vp_ref, vc_ref, o_ref, do_ref, l_ref, s_ref,
             dq_ref, dk_ref, dv_ref, ds_ref, ck, cv):
        i = pl.program_id(0)

        @pl.when(i == 0)
        def _():
            ds_ref[...] = jnp.zeros_like(ds_ref)
            ck[...] = jnp.zeros_like(ck)
            cv[...] = jnp.zeros_like(cv)

        @pl.when(i < nb)
        def _():
            valid = _attn_mask(i)
            kband = jnp.concatenate([kp_ref[...], kc_ref[...]], axis=0)
            vband = jnp.concatenate([vp_ref[...], vc_ref[...]], axis=0)
            heads = range(N_Q_HEADS)
            sl = [slice(h * HEAD_DIM, (h + 1) * HEAD_DIM) for h in heads]
            ks = [kband[:, kh * HEAD_DIM:(kh + 1) * HEAD_DIM] for kh in range(N_KV_HEADS)]
            vs = [vband[:, kh * HEAD_DIM:(kh + 1) * HEAD_DIM] for kh in range(N_KV_HEADS)]
            qs = [q_ref[:, sl[h]] for h in heads]
            d_os = [do_ref[:, sl[h]] for h in heads]
            dobs = [d.astype(BF16) for d in d_os]
            scores = [_dot(qs[h], ks[h // Q_PER_KV], NT) for h in heads]
            dps = [_dot(dobs[h], vs[h // Q_PER_KV], NT) for h in heads]
            col_head = lax.broadcasted_iota(jnp.int32, (D_ATTN, 128), 0) // HEAD_DIM
            head_ones = (col_head == lax.broadcasted_iota(jnp.int32, (D_ATTN, 128), 1)).astype(BF16)
            deltas = _row_sums(do_ref[...] * o_ref[...], head_ones)
            pbs, dscs = [], []
            for h in heads:
                lse_h = l_ref[:, h:h + 1]
                p = jnp.where(valid, jnp.exp(scores[h] - lse_h), 0.0)
                delta = deltas[:, h:h + 1]
                pbs.append(p.astype(BF16))
                dscs.append((p * (dps[h] - delta)).astype(BF16))
                psink = jnp.exp(s_ref[0:1, h:h + 1] - lse_h)
                ds_ref[0:1, h:h + 1] += -jnp.sum(psink * delta, axis=0, keepdims=True)
            dqs = [_dot(dscs[h], ks[h // Q_PER_KV], NN) for h in heads]
            dkb = [sum(_dot(dscs[h], qs[h], TN) for h in heads if h // Q_PER_KV == kh) for kh in range(N_KV_HEADS)]
            dvb = [sum(_dot(pbs[h], dobs[h], TN) for h in heads if h // Q_PER_KV == kh) for kh in range(N_KV_HEADS)]
            for h in heads:
                dq_ref[:, sl[h]] = dqs[h]
            dk_band = jnp.concatenate(dkb, axis=1)
            dv_band = jnp.concatenate(dvb, axis=1)
            dk_ref[...] = ck[...] + dk_band[:ATTN_BLOCK]
            dv_ref[...] = cv[...] + dv_band[:ATTN_BLOCK]
            ck[...] = dk_band[ATTN_BLOCK:]
            cv[...] = dv_band[ATTN_BLOCK:]

        @pl.when(i == nb)
        def _():
            dk_ref[...] = ck[...]
            dv_ref[...] = cv[...]

    blk = lambda w, f: pl.BlockSpec((ATTN_BLOCK, w), f)
    cur = lambda i: jnp.minimum(i, nb - 1)
    prev = lambda i: jnp.clip(i - 1, 0, nb - 1)
    return pl.pallas_call(
        kern, name="attn_bwd", grid=(nb + 1,),
        in_specs=[blk(512, lambda i: (cur(i), 0)),
                  blk(128, lambda i: (prev(i), 4)), blk(128, lambda i: (cur(i), 4)),
                  blk(128, lambda i: (prev(i), 5)), blk(128, lambda i: (cur(i), 5)),
                  blk(512, lambda i: (cur(i), 0)), blk(512, lambda i: (cur(i), 0)),
                  blk(N_Q_HEADS, lambda i: (cur(i), 0)),
                  pl.BlockSpec((1, N_Q_HEADS), lambda i: (0, 0))],
        out_specs=[blk(512, lambda i: (cur(i), 0)), blk(128, lambda i: (prev(i), 0)),
                   blk(128, lambda i: (prev(i), 0)), pl.BlockSpec((1, N_Q_HEADS), lambda i: (0, 0))],
        out_shape=[jax.ShapeDtypeStruct((t, D_ATTN), F32), jax.ShapeDtypeStruct((t, D_KV), F32),
                   jax.ShapeDtypeStruct((t, D_KV), F32), jax.ShapeDtypeStruct((1, N_Q_HEADS), F32)],
        scratch_shapes=[pltpu.VMEM((ATTN_BLOCK, D_KV), F32), pltpu.VMEM((ATTN_BLOCK, D_KV), F32)],
        compiler_params=_params(("arbitrary",)),
    )(qkv, qkv, qkv, qkv, qkv, o, do, lse, sinks)


_GROUPS = ((0, D_ATTN), (D_ATTN, D_ATTN + D_S5), (D_ATTN + D_S5, D_MODEL))


def _rms_fwd(ya, ys, yl, g):
    t = ya.shape[0]
    tm = _row_tile(t, TM)

    def kern(a_ref, s_ref, l_ref, g_ref, o_ref):
        for (lo, hi), ref in zip(_GROUPS, (a_ref, s_ref, l_ref)):
            y = ref[...]
            n = y * lax.rsqrt(jnp.mean(y * y, axis=-1, keepdims=True) + RMS_EPS)
            o_ref[:, lo:hi] = (n * g_ref[:, lo:hi]).astype(BF16)

    row = lambda w: pl.BlockSpec((tm, w), lambda i: (i, 0))
    return pl.pallas_call(
        kern, name="rms_fwd", grid=(t // tm,),
        in_specs=[row(D_ATTN), row(D_S5), row(D_LRU), pl.BlockSpec((1, D_MODEL), lambda i: (0, 0))],
        out_specs=row(D_MODEL), out_shape=jax.ShapeDtypeStruct((t, D_MODEL), BF16),
        compiler_params=_params(("parallel",)),
    )(ya, ys, yl, g)


def _rms_bwd(dmix, ya, ys, yl, g):
    t = ya.shape[0]
    tm = _row_tile(t, TM)

    def kern(d_ref, a_ref, s_ref, l_ref, g_ref, da_ref, ds_ref, dl_ref, dg_ref):
        @pl.when(pl.program_id(0) == 0)
        def _():
            dg_ref[...] = jnp.zeros_like(dg_ref)

        for (lo, hi), ref, out in zip(_GROUPS, (a_ref, s_ref, l_ref), (da_ref, ds_ref, dl_ref)):
            y = ref[...]
            rstd = lax.rsqrt(jnp.mean(y * y, axis=-1, keepdims=True) + RMS_EPS)
            n = y * rstd
            dm = d_ref[:, lo:hi]
            dg_ref[:, lo:hi] += jnp.sum(dm * n, axis=0, keepdims=True)
            dn = dm * g_ref[:, lo:hi]
            out[...] = rstd * (dn - n * jnp.mean(dn * n, axis=-1, keepdims=True))

    row = lambda w: pl.BlockSpec((tm, w), lambda i: (i, 0))
    vec = pl.BlockSpec((1, D_MODEL), lambda i: (0, 0))
    return pl.pallas_call(
        kern, name="rms_bwd", grid=(t // tm,),
        in_specs=[row(D_MODEL), row(D_ATTN), row(D_S5), row(D_LRU), vec],
        out_specs=[row(D_ATTN), row(D_S5), row(D_LRU), vec],
        out_shape=[jax.ShapeDtypeStruct((t, D_ATTN), F32), jax.ShapeDtypeStruct((t, D_S5), F32),
                   jax.ShapeDtypeStruct((t, D_LRU), F32), jax.ShapeDtypeStruct((1, D_MODEL), F32)],
        compiler_params=_params(("arbitrary",)),
    )(dmix, ya, ys, yl, g)


def _mix_out_ln(ya, ys, yl, mg, w_out, b_out, xres, g, b, layer):
    t = xres.shape[0]
    tm = _row_tile(t, TM)

    def kern(a_ref, s_ref, l_ref, mg_ref, w_ref, bias_ref, x_ref, g_ref, b_ref, m_ref, y_ref, yb_ref, h_ref, r_ref):
        for (lo, hi), ref in zip(_GROUPS, (a_ref, s_ref, l_ref)):
            v = ref[...]
            n = v * lax.rsqrt(jnp.mean(v * v, axis=-1, keepdims=True) + RMS_EPS)
            m_ref[:, lo:hi] = (n * mg_ref[:, lo:hi]).astype(BF16)
        r = ALPHA * x_ref[...] + _dot(m_ref[...], w_ref[...], NN) + bias_ref[...]
        mu = jnp.mean(r, axis=-1, keepdims=True)
        xc = r - mu
        rstd = lax.rsqrt(jnp.mean(xc * xc, axis=-1, keepdims=True) + LN_EPS)
        xhat = xc * rstd
        h_ref[...] = xhat
        r_ref[...] = rstd
        y = xhat * g_ref[...] + b_ref[...]
        y_ref[...] = y
        yb_ref[...] = y.astype(BF16)

    rowb = lambda w: pl.BlockSpec((tm, w), lambda i: (i, 0))
    row = rowb(D_MODEL)
    vec = pl.BlockSpec((1, D_MODEL), lambda i: (0, 0))
    big = lambda dt: jax.ShapeDtypeStruct((t, D_MODEL), dt)
    return pl.pallas_call(
        kern, name="mix_out_ln", grid=(t // tm,),
        in_specs=[rowb(D_ATTN), rowb(D_S5), rowb(D_LRU), vec,
                  pl.BlockSpec((None, D_MODEL, D_MODEL), lambda i: (layer, 0, 0)), vec, row, vec, vec],
        out_specs=[row, row, row, row, pl.BlockSpec((tm, 1), lambda i: (i, 0))],
        out_shape=[big(BF16), big(F32), big(BF16), big(F32), jax.ShapeDtypeStruct((t, 1), F32)],
        compiler_params=_params(("parallel",)),
    )(ya, ys, yl, mg, w_out, b_out, xres, g, b)


def _d_mix_rms(dx1, xhat, rstd, lg, w_out, ya, ys, yl, mg, layer):
    t = dx1.shape[0]
    tm = _row_tile(t, TM)

    def kern(d_ref, h_ref, r_ref, lg_ref, w_ref, a_ref, s_ref, l_ref, g_ref,
             dr_ref, drb_ref, dlg_ref, dlb_ref, sr_ref, da_ref, ds_ref, dl_ref, dg_ref):
        @pl.when(pl.program_id(0) == 0)
        def _():
            for ref in (dlg_ref, dlb_ref, sr_ref, dg_ref):
                ref[...] = jnp.zeros_like(ref)

        _ln_bwd_tile(d_ref[...], h_ref[...], r_ref[...], lg_ref[...], dr_ref, drb_ref, dlg_ref, dlb_ref, sr_ref)
        dmix = _dot(drb_ref[...], w_ref[...], NT)
        for (lo, hi), ref, out in zip(_GROUPS, (a_ref, s_ref, l_ref), (da_ref, ds_ref, dl_ref)):
            v = ref[...]
            rstd = lax.rsqrt(jnp.mean(v * v, axis=-1, keepdims=True) + RMS_EPS)
            n = v * rstd
            dm = dmix[:, lo:hi]
            dg_ref[:, lo:hi] += jnp.sum(dm * n, axis=0, keepdims=True)
            dn = dm * g_ref[:, lo:hi]
            out[...] = rstd * (dn - n * jnp.mean(dn * n, axis=-1, keepdims=True))

    rowb = lambda w: pl.BlockSpec((tm, w), lambda i: (i, 0))
    vec = pl.BlockSpec((1, D_MODEL), lambda i: (0, 0))
    vshape = jax.ShapeDtypeStruct((1, D_MODEL), F32)
    return pl.pallas_call(
        kern, name="d_mix_rms", grid=(t // tm,),
        in_specs=[rowb(D_MODEL), rowb(D_MODEL), pl.BlockSpec((tm, 1), lambda i: (i, 0)), vec,
                  pl.BlockSpec((None, D_MODEL, D_MODEL), lambda i: (layer, 0, 0)),
                  rowb(D_ATTN), rowb(D_S5), rowb(D_LRU), vec],
        out_specs=[rowb(D_MODEL), rowb(D_MODEL), vec, vec, vec, rowb(D_ATTN), rowb(D_S5), rowb(D_LRU), vec],
        out_shape=[jax.ShapeDtypeStruct((t, D_MODEL), F32), jax.ShapeDtypeStruct((t, D_MODEL), BF16),
                   vshape, vshape, vshape, jax.ShapeDtypeStruct((t, D_ATTN), F32),
                   jax.ShapeDtypeStruct((t, D_S5), F32), jax.ShapeDtypeStruct((t, D_LRU), F32), vshape],
        compiler_params=_params(("arbitrary",)),
    )(dx1, xhat, rstd, lg, w_out, ya, ys, yl, mg)


def _matmul_ln(name, a, w, bias, xres, g, b, a_blk, a_map, w_blk, parts, layer):
    t = xres.shape[0]
    tm = a_blk[-2]

    def kern(a_ref, w_ref, bias_ref, x_ref, g_ref, b_ref, y_ref, yb_ref, h_ref, r_ref):
        if parts is None:
            f = _dot(a_ref[...], w_ref[...], NN)
        else:
            f = sum(_dot(a_ref[j], w_ref[j], NN) for j in range(parts))
        r = ALPHA * x_ref[...] + f + bias_ref[...]
        mu = jnp.mean(r, axis=-1, keepdims=True)
        xc = r - mu
        rstd = lax.rsqrt(jnp.mean(xc * xc, axis=-1, keepdims=True) + LN_EPS)
        xhat = xc * rstd
        h_ref[...] = xhat
        r_ref[...] = rstd
        y = xhat * g_ref[...] + b_ref[...]
        y_ref[...] = y
        yb_ref[...] = y.astype(BF16)

    row = pl.BlockSpec((tm, D_MODEL), lambda i: (i, 0))
    vec = pl.BlockSpec((1, D_MODEL), lambda i: (0, 0))
    big = lambda dt: jax.ShapeDtypeStruct((t, D_MODEL), dt)
    return pl.pallas_call(
        kern, name=name, grid=(t // tm,),
        in_specs=[pl.BlockSpec(a_blk, a_map), pl.BlockSpec((None,) + w_blk, lambda i: (layer,) + (0,) * len(w_blk)), vec, row, vec, vec],
        out_specs=[row, row, row, pl.BlockSpec((tm, 1), lambda i: (i, 0))],
        out_shape=[big(F32), big(BF16), big(F32), jax.ShapeDtypeStruct((t, 1), F32)],
        compiler_params=_params(("parallel",)),
    )(a, w, bias, xres, g, b)


def _ln_bwd(dy, xhat, rstd, g):
    t = dy.shape[0]
    tm = _row_tile(t, TM)

    def kern(d_ref, h_ref, r_ref, g_ref, dr_ref, drb_ref, dg_ref, db_ref, sr_ref):
        @pl.when(pl.program_id(0) == 0)
        def _():
            dg_ref[...] = jnp.zeros_like(dg_ref)
            db_ref[...] = jnp.zeros_like(db_ref)
            sr_ref[...] = jnp.zeros_like(sr_ref)

        d = d_ref[...]
        xhat = h_ref[...]
        dg_ref[...] += jnp.sum(d * xhat, axis=0, keepdims=True)
        db_ref[...] += jnp.sum(d, axis=0, keepdims=True)
        dh = d * g_ref[...]
        dr = r_ref[...] * (dh - jnp.mean(dh, axis=-1, keepdims=True)
                           - xhat * jnp.mean(dh * xhat, axis=-1, keepdims=True))
        dr_ref[...] = dr
        drb_ref[...] = dr.astype(BF16)
        sr_ref[...] += jnp.sum(dr, axis=0, keepdims=True)

    row = pl.BlockSpec((tm, D_MODEL), lambda i: (i, 0))
    vec = pl.BlockSpec((1, D_MODEL), lambda i: (0, 0))
    vshape = jax.ShapeDtypeStruct((1, D_MODEL), F32)
    return pl.pallas_call(
        kern, name="ln_bwd", grid=(t // tm,),
        in_specs=[row, row, pl.BlockSpec((tm, 1), lambda i: (i, 0)), vec],
        out_specs=[row, row, vec, vec, vec],
        out_shape=[jax.ShapeDtypeStruct((t, D_MODEL), F32), jax.ShapeDtypeStruct((t, D_MODEL), BF16),
                   vshape, vshape, vshape],
        compiler_params=_params(("arbitrary",)),
    )(dy, xhat, rstd, g)


def _ln_bwd_tile(d, xhat, rstd, g, dr_ref, drb_ref, dg_ref, db_ref, sr_ref):
    dg_ref[...] += jnp.sum(d * xhat, axis=0, keepdims=True)
    db_ref[...] += jnp.sum(d, axis=0, keepdims=True)
    dh = d * g
    dr = rstd * (dh - jnp.mean(dh, axis=-1, keepdims=True) - xhat * jnp.mean(dh * xhat, axis=-1, keepdims=True))
    dr_ref[...] = dr
    drb_ref[...] = dr.astype(BF16)
    sr_ref[...] += jnp.sum(dr, axis=0, keepdims=True)


def _loss_head(y, target, xhat, rstd, g):
    t = y.shape[0]
    tm = _row_tile(t, TM)

    def kern(y_ref, t_ref, h_ref, r_ref, g_ref, l_ref, dr_ref, drb_ref, dg_ref, db_ref, sr_ref):
        @pl.when(pl.program_id(0) == 0)
        def _():
            for ref in (l_ref, dg_ref, db_ref, sr_ref):
                ref[...] = jnp.zeros_like(ref)

        err = y_ref[...] - t_ref[...]
        part = jnp.sum(jnp.sum(err * err, axis=-1, keepdims=True), axis=0, keepdims=True)
        l_ref[...] += jnp.broadcast_to(part * (0.5 / D_MODEL), l_ref.shape)
        _ln_bwd_tile(err * (1.0 / D_MODEL), h_ref[...], r_ref[...], g_ref[...], dr_ref, drb_ref, dg_ref, db_ref, sr_ref)

    row = pl.BlockSpec((tm, D_MODEL), lambda i: (i, 0))
    vec = pl.BlockSpec((1, D_MODEL), lambda i: (0, 0))
    vshape = jax.ShapeDtypeStruct((1, D_MODEL), F32)
    return pl.pallas_call(
        kern, name="loss_head", grid=(t // tm,),
        in_specs=[row, row, row, pl.BlockSpec((tm, 1), lambda i: (i, 0)), vec],
        out_specs=[pl.BlockSpec((1, 128), lambda i: (0, 0)), row, row, vec, vec, vec],
        out_shape=[jax.ShapeDtypeStruct((1, 128), F32), jax.ShapeDtypeStruct((t, D_MODEL), F32),
                   jax.ShapeDtypeStruct((t, D_MODEL), BF16), vshape, vshape, vshape],
        compiler_params=_params(("arbitrary",)),
    )(y, target, xhat, rstd, g)


HALO = 8


def _ffn_mid_specs(t, tm):
    main = pl.BlockSpec((None, tm, FF_SH), lambda j, i: (j, i, 0))
    prev = pl.BlockSpec((None, HALO, FF_SH), lambda j, i: (j, jnp.maximum(i * (tm // HALO) - 1, 0), 0))
    cw = pl.BlockSpec((None, FFN_CONV, FF_SH), lambda j, i: (j, 0, 0))
    cb = pl.BlockSpec((None, 1, FF_SH), lambda j, i: (j, 0, 0))
    return main, prev, cw, cb


def _ffn_conv(ext, g_ref, p_ref, w_ref, b_ref, tm):
    i = pl.program_id(1)
    ext[0:HALO, :] = jnp.where(i > 0, p_ref[...], 0.0)
    ext[HALO:, :] = g_ref[...]
    taps = [ext[pl.ds(HALO - (FFN_CONV - 1) + k, tm), :] for k in range(FFN_CONV)]
    gc = b_ref[...] + sum(w_ref[k:k + 1, :] * taps[k] for k in range(FFN_CONV))
    return gc, taps


def _ffn_mid_fwd(gpre, up, cw, cb):
    t = gpre.shape[1]
    tm = _row_tile(t, TM)

    def kern(g_ref, p_ref, u_ref, w_ref, b_ref, o_ref, ext):
        gc, _ = _ffn_conv(ext, g_ref, p_ref, w_ref, b_ref, tm)
        o_ref[...] = (gc * _sigmoid(gc) * u_ref[...]).astype(BF16)

    main, prev, cws, cbs = _ffn_mid_specs(t, tm)
    return pl.pallas_call(
        kern, name="ffn_mid_fwd", grid=(N_CHIPS, t // tm),
        in_specs=[main, prev, main, cws, cbs], out_specs=main,
        out_shape=jax.ShapeDtypeStruct((N_CHIPS, t, FF_SH), BF16),
        scratch_shapes=[pltpu.VMEM((tm + HALO, FF_SH), F32)],
        compiler_params=_params(("parallel", "parallel")),
    )(gpre, gpre, up, cw, cb)


def _ffn_mid_bwd(gpre, up, dhmid, cw, cb):
    t = gpre.shape[1]
    tm = _row_tile(t, TM)

    def kern(g_ref, p_ref, u_ref, d_ref, w_ref, b_ref, h_ref, du_ref, dg_ref, dw_ref, db_ref, ext):
        @pl.when(pl.program_id(1) == 0)
        def _():
            dw_ref[...] = jnp.zeros_like(dw_ref)
            db_ref[...] = jnp.zeros_like(db_ref)

        gc, taps = _ffn_conv(ext, g_ref, p_ref, w_ref, b_ref, tm)
        sg = _sigmoid(gc)
        s = gc * sg
        u = u_ref[...]
        d = d_ref[...]
        h_ref[...] = (s * u).astype(BF16)
        du_ref[...] = (d * s).astype(BF16)
        dgc = d * u * (sg * (1.0 + gc * (1.0 - sg)))
        dg_ref[...] = dgc
        db_ref[...] += jnp.sum(dgc, axis=0, keepdims=True)
        for k in range(FFN_CONV):
            dw_ref[k:k + 1, :] += jnp.sum(dgc * taps[k], axis=0, keepdims=True)

    main, prev, cws, cbs = _ffn_mid_specs(t, tm)
    big = lambda dt: jax.ShapeDtypeStruct((N_CHIPS, t, FF_SH), dt)
    return pl.pallas_call(
        kern, name="ffn_mid_bwd", grid=(N_CHIPS, t // tm),
        in_specs=[main, prev, main, main, cws, cbs], out_specs=[main, main, main, cws, cbs],
        out_shape=[big(BF16), big(BF16), big(F32), jax.ShapeDtypeStruct((N_CHIPS, FFN_CONV, FF_SH), F32),
                   jax.ShapeDtypeStruct((N_CHIPS, 1, FF_SH), F32)],
        scratch_shapes=[pltpu.VMEM((tm + HALO, FF_SH), F32)],
        compiler_params=_params(("parallel", "arbitrary")),
    )(gpre, gpre, up, dhmid, cw, cb)


def _ffn_conv_t(dgc, cw):
    t = dgc.shape[1]
    tm = _row_tile(t, TM)
    nt = t // tm

    def kern(d_ref, n_ref, w_ref, o_ref, ext):
        i = pl.program_id(1)
        ext[0:tm, :] = d_ref[...]
        ext[tm:, :] = jnp.where(i < nt - 1, n_ref[...], 0.0)
        acc = sum(w_ref[k:k + 1, :] * ext[pl.ds(FFN_CONV - 1 - k, tm), :] for k in range(FFN_CONV))
        o_ref[...] = acc.astype(BF16)

    main, _, cws, _ = _ffn_mid_specs(t, tm)
    nxt = pl.BlockSpec((None, HALO, FF_SH),
                       lambda j, i: (j, jnp.minimum((i + 1) * (tm // HALO), t // HALO - 1), 0))
    return pl.pallas_call(
        kern, name="ffn_conv_t", grid=(N_CHIPS, nt),
        in_specs=[main, nxt, cws], out_specs=main,
        out_shape=jax.ShapeDtypeStruct((N_CHIPS, t, FF_SH), BF16),
        scratch_shapes=[pltpu.VMEM((tm + HALO, FF_SH), F32)],
        compiler_params=_params(("parallel", "parallel")),
    )(dgc, dgc, cw)


def _ffn_hidden_fwd(xb, wg, wu, cw, cb, layer):
    t = xb.shape[0]
    tm = _row_tile(t, TM)

    def kern(x_ref, wg_ref, wu_ref, cw_ref, cb_ref, g_ref, c_ref, u_ref, h_ref, ext):
        @pl.when(pl.program_id(1) == 0)
        def _():
            ext[0:HALO, :] = jnp.zeros((HALO, FF_SH), F32)

        x = x_ref[...]
        gb = _dot(x, wg_ref[...], NN).astype(BF16)
        ub = _dot(x, wu_ref[...], NN).astype(BF16)
        g_ref[...] = gb
        u_ref[...] = ub
        g = gb.astype(F32)
        w = [cw_ref[k:k + 1, :] for k in range(FFN_CONV)]
        body = cb_ref[...] + w[2] * g + w[1] * pltpu.roll(g, 1, 0) + w[0] * pltpu.roll(g, 2, 0)
        ext[HALO:, :] = g[0:HALO, :]
        head = cb_ref[...] + sum(w[k] * ext[pl.ds(HALO - (FFN_CONV - 1) + k, HALO), :] for k in range(FFN_CONV))
        gcb = jnp.concatenate([head, body[HALO:, :]], axis=0).astype(BF16)
        c_ref[...] = gcb
        gc = gcb.astype(F32)
        h_ref[...] = (gc * _sigmoid(gc) * ub.astype(F32)).astype(BF16)
        ext[0:HALO, :] = g[tm - HALO:, :]

    col = pl.BlockSpec((None, tm, FF_SH), lambda j, i: (j, i, 0))
    wspec = pl.BlockSpec((None, None, D_MODEL, FF_SH), lambda j, i: (layer, j, 0, 0))
    big = jax.ShapeDtypeStruct((N_CHIPS, t, FF_SH), BF16)
    return pl.pallas_call(
        kern, name="ffn_hidden_fwd", grid=(N_CHIPS, t // tm),
        in_specs=[pl.BlockSpec((tm, D_MODEL), lambda j, i: (i, 0)), wspec, wspec,
                  pl.BlockSpec((None, FFN_CONV, FF_SH), lambda j, i: (j, 0, 0)),
                  pl.BlockSpec((None, 1, FF_SH), lambda j, i: (j, 0, 0))],
        out_specs=[col, col, col, col], out_shape=[big, big, big, big],
        scratch_shapes=[pltpu.VMEM((2 * HALO, FF_SH), F32)],
        compiler_params=_params(("parallel", "arbitrary")),
    )(xb, wg, wu, cw, cb)


def _ffn_hidden_bwd(drb, gpre, gconv, up, wd, cw, layer):
    t = drb.shape[0]
    tm = _row_tile(t, TM)
    nt = t // tm
    rb = lambda i: nt - 1 - i

    def kern(d_ref, g_ref, c_ref, u_ref, wd_ref, cw_ref, du_ref, dg_ref, dw_ref, db_ref, ext):
        @pl.when(pl.program_id(1) == 0)
        def _():
            dw_ref[...] = jnp.zeros_like(dw_ref)
            db_ref[...] = jnp.zeros_like(db_ref)
            ext[HALO:, :] = jnp.zeros((HALO, FF_SH), F32)

        dh = _dot(d_ref[...], wd_ref[...], NT)
        gc = c_ref[...].astype(F32)
        sg = _sigmoid(gc)
        du_ref[...] = (dh * (gc * sg)).astype(BF16)
        dgc = dh * u_ref[...].astype(F32) * (sg * (1.0 + gc * (1.0 - sg)))
        db_ref[...] += jnp.sum(dgc, axis=0, keepdims=True)
        g = g_ref[...].astype(F32)
        w = [cw_ref[k:k + 1, :] for k in range(FFN_CONV)]
        taps = [pltpu.roll(dgc, tm - 2, 0), pltpu.roll(dgc, tm - 1, 0), dgc]
        body = sum(w[k] * taps[k] for k in range(FFN_CONV))
        last = slice(tm - HALO, tm)
        ext[0:HALO, :] = dgc[last, :]
        tail_taps = [ext[pl.ds(FFN_CONV - 1 - k, HALO), :] for k in range(FFN_CONV)]
        tail = sum(w[k] * tail_taps[k] for k in range(FFN_CONV))
        dg_ref[...] = jnp.concatenate([body[0:tm - HALO, :], tail], axis=0).astype(BF16)
        for k in range(FFN_CONV):
            dw_ref[k:k + 1, :] += (jnp.sum(g * taps[k], axis=0, keepdims=True)
                                   + jnp.sum(g[last, :] * (tail_taps[k] - taps[k][last, :]), axis=0, keepdims=True))
        ext[HALO:, :] = dgc[0:HALO, :]

    col = pl.BlockSpec((None, tm, FF_SH), lambda j, i: (j, rb(i), 0))
    cws = pl.BlockSpec((None, FFN_CONV, FF_SH), lambda j, i: (j, 0, 0))
    cbs = pl.BlockSpec((None, 1, FF_SH), lambda j, i: (j, 0, 0))
    big = jax.ShapeDtypeStruct((N_CHIPS, t, FF_SH), BF16)
    return pl.pallas_call(
        kern, name="ffn_hidden_bwd", grid=(N_CHIPS, nt),
        in_specs=[pl.BlockSpec((tm, D_MODEL), lambda j, i: (rb(i), 0)), col, col, col,
                  pl.BlockSpec((None, None, FF_SH, D_MODEL), lambda j, i: (layer, j, 0, 0)), cws],
        out_specs=[col, col, cws, cbs],
        out_shape=[big, big, jax.ShapeDtypeStruct((N_CHIPS, FFN_CONV, FF_SH), F32),
                   jax.ShapeDtypeStruct((N_CHIPS, 1, FF_SH), F32)],
        scratch_shapes=[pltpu.VMEM((2 * HALO, FF_SH), F32)],
        compiler_params=_params(("parallel", "arbitrary")),
    )(drb, gpre, gconv, up, wd, cw)


def _s5_coefs(ar, ai, reverse):
    if reverse:
        ai = -ai
    pw = [(ar, ai)]
    for _ in range(SUBLANES - 1):
        pr, pi = pw[-1]
        pw.append((pr * ar - pi * ai, pr * ai + pi * ar))
    rows = jnp.arange(SUBLANES)[:, None]
    out = []
    for s in (1, 2, 4):
        keep = (rows + s <= SUBLANES - 1) if reverse else (rows >= s)
        out += [jnp.where(keep, pw[s - 1][0][None], 0.0), jnp.where(keep, pw[s - 1][1][None], 0.0)]
    order = list(range(SUBLANES - 1, -1, -1)) if reverse else list(range(SUBLANES))
    out += [jnp.stack([pw[k][0] for k in order]), jnp.stack([pw[k][1] for k in order])]
    return jnp.stack(out).astype(F32)


def _s5_scan(buf, coef_ref, carry, tm, reverse):
    n8 = tm // SUBLANES

    def body(it, c):
        cre, cim = c
        blk = (n8 - 1 - it) if reverse else it
        r0 = pl.multiple_of(blk * SUBLANES, SUBLANES)
        xre = buf[pl.ds(r0, SUBLANES), 0:N_STATE]
        xim = buf[pl.ds(r0, SUBLANES), N_STATE:]
        for idx, s in enumerate((1, 2, 4)):
            sh = (SUBLANES - s) if reverse else s
            sre = pltpu.roll(xre, sh, 0)
            sim = pltpu.roll(xim, sh, 0)
            are = coef_ref[2 * idx]
            aim = coef_ref[2 * idx + 1]
            xre, xim = xre + are * sre - aim * sim, xim + are * sim + aim * sre
        pre = coef_ref[6]
        pim = coef_ref[7]
        hre = xre + pre * cre - pim * cim
        him = xim + pre * cim + pim * cre
        buf[pl.ds(r0, SUBLANES), 0:N_STATE] = hre
        buf[pl.ds(r0, SUBLANES), N_STATE:] = him
        row = 0 if reverse else SUBLANES - 1
        return (jnp.broadcast_to(hre[row:row + 1], (SUBLANES, N_STATE)),
                jnp.broadcast_to(him[row:row + 1], (SUBLANES, N_STATE)))

    cre, cim = lax.fori_loop(0, n8, body, (carry[:, 0:N_STATE], carry[:, N_STATE:]))
    carry[:, 0:N_STATE] = cre
    carry[:, N_STATE:] = cim


def _real_scan(abuf, bbuf, carry, tm, reverse):
    n8 = tm // SUBLANES
    width = bbuf.shape[1]

    def body(it, c):
        blk = (n8 - 1 - it) if reverse else it
        r0 = pl.multiple_of(blk * SUBLANES, SUBLANES)
        a = abuf[pl.ds(r0, SUBLANES), :]
        b = bbuf[pl.ds(r0, SUBLANES), :]
        rows = lax.broadcasted_iota(jnp.int32, (SUBLANES, width), 0)
        for s in (1, 2, 4):
            sh = (SUBLANES - s) if reverse else s
            keep = (rows + s <= SUBLANES - 1) if reverse else (rows >= s)
            sa = pltpu.roll(a, sh, 0)
            sb = pltpu.roll(b, sh, 0)
            b = b + a * jnp.where(keep, sb, 0.0)
            a = a * jnp.where(keep, sa, 1.0)
        h = b + a * c
        bbuf[pl.ds(r0, SUBLANES), :] = h
        row = 0 if reverse else SUBLANES - 1
        return jnp.broadcast_to(h[row:row + 1], (SUBLANES, width))

    carry[...] = lax.fori_loop(0, n8, body, carry[...])


def _dot(a, b, dims):
    return lax.dot_general(a, b, (dims, ((), ())), preferred_element_type=F32)


TS5 = 512
HALO16 = 16


def _s5_fwd(proj, bmat, coef, cmat, dvec, gw, gb):
    t = proj.shape[0]
    tm = _row_tile(t, TS5)

    def kern(u_ref, b_ref, coef_ref, c_ref, d_ref, gw_ref, gb_ref, h_ref, y_ref, hbuf, carry):
        @pl.when(pl.program_id(0) == 0)
        def _():
            carry[...] = jnp.zeros_like(carry)

        u = u_ref[...]
        hbuf[...] = _dot(u.astype(BF16), b_ref[...], NN)
        _s5_scan(hbuf, coef_ref, carry, tm, False)
        hb = hbuf[...].astype(BF16)
        h_ref[...] = hb
        y = _dot(hb, c_ref[...], NN) + d_ref[...] * u
        ys = _gelu(y)
        z = _dot(ys.astype(BF16), gw_ref[...], NN) + gb_ref[...]
        y_ref[...] = ys * _sigmoid(z)

    full = lambda shp: pl.BlockSpec(shp, lambda i: (0,) * len(shp))
    return pl.pallas_call(
        kern, name="s5_fwd", grid=(t // tm,),
        in_specs=[pl.BlockSpec((tm, D_S5), lambda i: (i, 0)), full((D_S5, 2 * N_STATE)),
                  full((8, SUBLANES, N_STATE)), full((2 * N_STATE, D_S5)), full((1, D_S5)),
                  full((D_S5, D_S5)), full((1, D_S5))],
        out_specs=[pl.BlockSpec((tm, 2 * N_STATE), lambda i: (i, 0)), pl.BlockSpec((tm, D_S5), lambda i: (i, 0))],
        out_shape=[jax.ShapeDtypeStruct((t, 2 * N_STATE), BF16), jax.ShapeDtypeStruct((t, D_S5), F32)],
        scratch_shapes=[pltpu.VMEM((tm, 2 * N_STATE), F32), pltpu.VMEM((SUBLANES, 2 * N_STATE), F32)],
        compiler_params=_params(("arbitrary",)),
    )(proj, bmat, coef, cmat, dvec, gw, gb)


def _s5_bwd(proj, h, dout, bmat, coef_b, cmat, dvec, gw, gb):
    t = proj.shape[0]
    tm = _row_tile(t, TS5)
    nt = t // tm
    rb = lambda i: nt - 1 - i

    def kern(u_ref, h_ref, hp_ref, d_ref, b_ref, coef_ref, c_ref, dv_ref, gw_ref, gb_ref,
             du_ref, dc_ref, db_ref, da_ref, dd_ref, dgw_ref, dgb_ref, gbuf, hext, carry):
        i = pl.program_id(0)

        @pl.when(i == 0)
        def _():
            carry[...] = jnp.zeros_like(carry)
            for r in (dc_ref, db_ref, da_ref, dd_ref, dgw_ref, dgb_ref):
                r[...] = jnp.zeros_like(r)

        u = u_ref[...]
        hb = h_ref[...]
        y = _dot(hb, c_ref[...], NN) + dv_ref[...] * u
        ys = _gelu(y)
        ysb = ys.astype(BF16)
        sg = _sigmoid(_dot(ysb, gw_ref[...], NN) + gb_ref[...])
        d_o = d_ref[...]
        dz = d_o * ys * sg * (1.0 - sg)
        dzb = dz.astype(BF16)
        dys = d_o * sg + _dot(dzb, gw_ref[...], NT)
        dgw_ref[...] += _dot(ysb, dzb, TN)
        dgb_ref[...] += jnp.sum(dz, axis=0, keepdims=True)
        dy = dys * _gelu_grad(y)
        dd_ref[...] += jnp.sum(dy * u, axis=0, keepdims=True)
        dyb = dy.astype(BF16)
        dc_ref[...] += _dot(hb, dyb, TN)
        gbuf[...] = _dot(dyb, c_ref[...], NT)
        _s5_scan(gbuf, coef_ref, carry, tm, True)
        g = gbuf[...]
        first = jnp.where(i < nt - 1, hp_ref[HALO16 - 1:HALO16, :].astype(F32), 0.0)
        hext[SUBLANES - 1:SUBLANES, :] = first
        hext[SUBLANES:, :] = hb.astype(F32)
        hprev = hext[pl.ds(SUBLANES - 1, tm), :]
        gre, gim = g[:, 0:N_STATE], g[:, N_STATE:]
        pre, pim = hprev[:, 0:N_STATE], hprev[:, N_STATE:]
        da_ref[0:1, :] += jnp.sum(gre * pre + gim * pim, axis=0, keepdims=True)
        da_ref[1:2, :] += jnp.sum(gim * pre - gre * pim, axis=0, keepdims=True)
        gb16 = g.astype(BF16)
        db_ref[...] += _dot(u.astype(BF16), gb16, TN)
        du_ref[...] = dy * dv_ref[...] + _dot(gb16, b_ref[...], NT)

    full = lambda shp: pl.BlockSpec(shp, lambda i: (0,) * len(shp))
    shape = lambda shp: jax.ShapeDtypeStruct(shp, F32)
    return pl.pallas_call(
        kern, name="s5_bwd", grid=(nt,),
        in_specs=[pl.BlockSpec((tm, D_S5), lambda i: (rb(i), 0)),
                  pl.BlockSpec((tm, 2 * N_STATE), lambda i: (rb(i), 0)),
                  pl.BlockSpec((HALO16, 2 * N_STATE), lambda i: (jnp.maximum(rb(i) * (tm // HALO16) - 1, 0), 0)),
                  pl.BlockSpec((tm, D_S5), lambda i: (rb(i), 0)),
                  full((D_S5, 2 * N_STATE)), full((8, SUBLANES, N_STATE)), full((2 * N_STATE, D_S5)),
                  full((1, D_S5)), full((D_S5, D_S5)), full((1, D_S5))],
        out_specs=[pl.BlockSpec((tm, D_S5), lambda i: (rb(i), 0)), full((2 * N_STATE, D_S5)),
                   full((D_S5, 2 * N_STATE)), full((2, N_STATE)), full((1, D_S5)), full((D_S5, D_S5)),
                   full((1, D_S5))],
        out_shape=[shape((t, D_S5)), shape((2 * N_STATE, D_S5)), shape((D_S5, 2 * N_STATE)),
                   shape((2, N_STATE)), shape((1, D_S5)), shape((D_S5, D_S5)), shape((1, D_S5))],
        scratch_shapes=[pltpu.VMEM((tm, 2 * N_STATE), F32), pltpu.VMEM((tm + SUBLANES, 2 * N_STATE), F32),
                        pltpu.VMEM((SUBLANES, 2 * N_STATE), F32)],
        compiler_params=_params(("arbitrary",)),
    )(proj, h, h, dout, bmat, coef_b, cmat, dvec, gw, gb)


def _lru_gates(ext, x_ref, p_ref, cw_ref, cb_ref, wx_ref, bx_ref, wa_ref, ba_ref, ap_ref, first_tile, row0, tm):
    ext[0:HALO, :] = jnp.where(first_tile, 0.0, p_ref[...])
    ext[HALO:, :] = x_ref[...]
    taps = [ext[pl.ds(HALO - (LRU_CONV - 1) + k, tm), :] for k in range(LRU_CONV)]
    xc = cb_ref[...] + sum(cw_ref[k:k + 1, :] * taps[k] for k in range(LRU_CONV))
    xcb = xc.astype(BF16)
    gx = _sigmoid(_dot(xcb, wx_ref[...], NN) + bx_ref[...])
    ga = _sigmoid(_dot(xcb, wa_ref[...], NN) + ba_ref[...])
    z = -ap_ref[...]
    sp = jnp.maximum(z, 0.0) + jnp.log(1.0 + jnp.exp(-jnp.abs(z)))
    log_a = -LRU_C * ga * sp
    a = jnp.exp(log_a)
    tok = row0 + lax.broadcasted_iota(jnp.int32, a.shape, 0)
    is0 = tok == 0
    mult = jnp.where(is0, 1.0, jnp.sqrt(1.0 - jnp.exp(2.0 * log_a)))
    return taps, xc, xcb, gx, ga, sp, a, mult, is0


def _lru_specs(tm, blk_of):
    col = lambda cidx: pl.BlockSpec((tm, D_LRU), lambda i: (blk_of(i), cidx))
    prev = lambda cidx: pl.BlockSpec((HALO, D_LRU), lambda i: (jnp.maximum(blk_of(i) * (tm // HALO) - 1, 0), cidx))
    full = lambda shp: pl.BlockSpec(shp, lambda i: (0,) * len(shp))
    wts = [full((LRU_CONV, D_LRU)), full((1, D_LRU)), full((D_LRU, D_LRU)), full((1, D_LRU)),
           full((D_LRU, D_LRU)), full((1, D_LRU)), full((1, D_LRU))]
    return col, prev, full, wts


def _lru_fwd(proj, cw, cb, wx, bx, wa, ba, ap):
    t = proj.shape[0]
    tm = _row_tile(t, TM)

    def kern(x_ref, p_ref, g_ref, cw_ref, cb_ref, wx_ref, bx_ref, wa_ref, ba_ref, ap_ref,
             y_ref, h_ref, ext, abuf, carry):
        i = pl.program_id(0)

        @pl.when(i == 0)
        def _():
            carry[...] = jnp.zeros_like(carry)

        _, xc, _, gx, _, _, a, mult, _ = _lru_gates(ext, x_ref, p_ref, cw_ref, cb_ref, wx_ref, bx_ref, wa_ref,
                                                    ba_ref, ap_ref, i == 0, i * tm, tm)
        abuf[...] = a
        h_ref[...] = mult * gx * xc
        _real_scan(abuf, h_ref, carry, tm, False)
        y_ref[...] = h_ref[...] * _gelu(g_ref[...])

    col, prev, full, wts = _lru_specs(tm, lambda i: i)
    out = pl.BlockSpec((tm, D_LRU), lambda i: (i, 0))
    return pl.pallas_call(
        kern, name="lru_fwd", grid=(t // tm,),
        in_specs=[col(1), prev(1), col(2)] + wts, out_specs=[out, out],
        out_shape=[jax.ShapeDtypeStruct((t, D_LRU), F32), jax.ShapeDtypeStruct((t, D_LRU), F32)],
        scratch_shapes=[pltpu.VMEM((tm + HALO, D_LRU), F32), pltpu.VMEM((tm, D_LRU), F32),
                        pltpu.VMEM((SUBLANES, D_LRU), F32)],
        compiler_params=_params(("arbitrary",)),
    )(proj, proj, proj, cw, cb, wx, bx, wa, ba, ap)


def _lru_bwd(proj, h, dout, cw, cb, wx, bx, wa, ba, ap):
    t = proj.shape[0]
    tm = _row_tile(t, TM)
    nt = t // tm
    rb = lambda i: nt - 1 - i

    def kern(x_ref, p_ref, g_ref, h_ref, hp_ref, d_ref, cw_ref, cb_ref, wx_ref, bx_ref, wa_ref, ba_ref, ap_ref,
             dxc_ref, dg_ref, dcw_ref, dcb_ref, dwx_ref, dbx_ref, dwa_ref, dba_ref, dap_ref,
             ext, aext, abuf, gbuf, carry, acarry):
        i = pl.program_id(0)
        blk = nt - 1 - i

        @pl.when(i == 0)
        def _():
            carry[...] = jnp.zeros_like(carry)
            acarry[...] = jnp.zeros_like(acarry)
            for r in (dcw_ref, dcb_ref, dwx_ref, dbx_ref, dwa_ref, dba_ref, dap_ref):
                r[...] = jnp.zeros_like(r)

        taps, xc, xcb, gx, ga, sp, a, mult, is0 = _lru_gates(
            ext, x_ref, p_ref, cw_ref, cb_ref, wx_ref, bx_ref, wa_ref, ba_ref, ap_ref, blk == 0, blk * tm, tm)
        gate = g_ref[...]
        d_o = d_ref[...]
        hcur = h_ref[...]
        dg_ref[...] = d_o * hcur * _gelu_grad(gate)
        aext[0:tm, :] = a
        aext[tm:, :] = acarry[...]
        abuf[...] = aext[pl.ds(1, tm), :]
        gbuf[...] = d_o * _gelu(gate)
        _real_scan(abuf, gbuf, carry, tm, True)
        acarry[...] = jnp.broadcast_to(a[0:1], acarry.shape)
        g = gbuf[...]
        ext[0:HALO, :] = jnp.where(blk == 0, 0.0, hp_ref[...])
        ext[HALO:, :] = hcur
        hprev = ext[pl.ds(HALO - 1, tm), :]
        dmult = jnp.where(is0, 0.0, g * gx * xc)
        dgx = g * mult * xc
        dxc = g * mult * gx
        dlog_a = g * hprev * a - dmult * (a * a) / mult
        dga = dlog_a * (-LRU_C * sp)
        dsp = jnp.sum(dlog_a * (-LRU_C * ga), axis=0, keepdims=True)
        dap_ref[...] += dsp * (-_sigmoid(-ap_ref[...]))
        dpa = (dga * ga * (1.0 - ga))
        dpx = (dgx * gx * (1.0 - gx))
        dpab, dpxb = dpa.astype(BF16), dpx.astype(BF16)
        dwx_ref[...] += _dot(xcb, dpxb, TN)
        dwa_ref[...] += _dot(xcb, dpab, TN)
        dbx_ref[...] += jnp.sum(dpx, axis=0, keepdims=True)
        dba_ref[...] += jnp.sum(dpa, axis=0, keepdims=True)
        dxc = dxc + _dot(dpxb, wx_ref[...], NT) + _dot(dpab, wa_ref[...], NT)
        dxc_ref[...] = dxc
        dcb_ref[...] += jnp.sum(dxc, axis=0, keepdims=True)
        for k in range(LRU_CONV):
            dcw_ref[k:k + 1, :] += jnp.sum(dxc * taps[k], axis=0, keepdims=True)

    col, prev, full, wts = _lru_specs(tm, rb)
    row = pl.BlockSpec((tm, D_LRU), lambda i: (rb(i), 0))
    hprev_spec = pl.BlockSpec((HALO, D_LRU), lambda i: (jnp.maximum(rb(i) * (tm // HALO) - 1, 0), 0))
    shape = lambda shp: jax.ShapeDtypeStruct(shp, F32)
    vec = (1, D_LRU)
    sq = (D_LRU, D_LRU)
    return pl.pallas_call(
        kern, name="lru_bwd", grid=(nt,),
        in_specs=[col(1), prev(1), col(2), row, hprev_spec, row] + wts,
        out_specs=[row, row, full((LRU_CONV, D_LRU)), full(vec), full(sq), full(vec), full(sq), full(vec), full(vec)],
        out_shape=[shape((t, D_LRU)), shape((t, D_LRU)), shape((LRU_CONV, D_LRU)), shape(vec), shape(sq),
                   shape(vec), shape(sq), shape(vec), shape(vec)],
        scratch_shapes=[pltpu.VMEM((tm + HALO, D_LRU), F32), pltpu.VMEM((tm + HALO, D_LRU), F32),
                        pltpu.VMEM((tm, D_LRU), F32), pltpu.VMEM((tm, D_LRU), F32),
                        pltpu.VMEM((SUBLANES, D_LRU), F32), pltpu.VMEM((SUBLANES, D_LRU), F32)],
        compiler_params=_params(("arbitrary",)),
    )(proj, proj, proj, h, h, dout, cw, cb, wx, bx, wa, ba, ap)


def _assemble_dproj(dq, dk, dv, du, dxc, dgate, cos, sin_s, cw):
    t = dq.shape[0]
    tm = _row_tile(t, TM)
    nt = t // tm

    def kern(dq_ref, dk_ref, dv_ref, du_ref, dx_ref, dn_ref, dg_ref, c_ref, s_ref, cw_ref, o_ref, b_ref, ext):
        i = pl.program_id(0)

        @pl.when(i == 0)
        def _():
            b_ref[...] = jnp.zeros_like(b_ref)

        def put(lo, val):
            hi = lo + val.shape[1]
            o_ref[:, lo:hi] = val.astype(BF16)
            b_ref[:, lo:hi] += jnp.sum(val, axis=0, keepdims=True)

        c = c_ref[...]
        s = s_ref[...]
        for ch in range(4):
            x = dq_ref[:, ch * 128:(ch + 1) * 128] * (HEAD_DIM ** -0.5)
            put(ch * 128, x * c - _rope_swap(x) * s)
        x = dk_ref[...]
        put(512, x * c - _rope_swap(x) * s)
        put(640, dv_ref[...])
        put(768, du_ref[...])
        ext[0:tm, :] = dx_ref[...]
        ext[tm:, :] = jnp.where(i < nt - 1, dn_ref[...], 0.0)
        put(1024, sum(cw_ref[k:k + 1, :] * ext[pl.ds(LRU_CONV - 1 - k, tm), :] for k in range(LRU_CONV)))
        put(1280, dg_ref[...])

    row = lambda w: pl.BlockSpec((tm, w), lambda i: (i, 0))
    nxt = pl.BlockSpec((HALO, D_LRU), lambda i: (jnp.minimum((i + 1) * (tm // HALO), t // HALO - 1), 0))
    return pl.pallas_call(
        kern, name="assemble_dproj", grid=(nt,),
        in_specs=[row(512), row(128), row(128), row(256), row(256), nxt, row(256), row(128), row(128),
                  pl.BlockSpec((LRU_CONV, D_LRU), lambda i: (0, 0))],
        out_specs=[row(D_IN), pl.BlockSpec((1, D_IN), lambda i: (0, 0))],
        out_shape=[jax.ShapeDtypeStruct((t, D_IN), BF16), jax.ShapeDtypeStruct((1, D_IN), F32)],
        scratch_shapes=[pltpu.VMEM((tm + HALO, D_LRU), F32)],
        compiler_params=_params(("arbitrary",)),
    )(dq, dk, dv, du, dxc, dxc, dgate, cos, sin_s, cw)


def _blockdiag_s5(bbar_re, bbar_im, c_re, c_im):
    eye = jnp.eye(S5_GROUPS, dtype=F32)
    b_of = lambda m: jnp.einsum('gpc,gh->gchp', m, eye).reshape(D_S5, N_STATE)
    c_of = lambda m: jnp.einsum('gcp,gh->gphc', m, eye).reshape(N_STATE, D_S5)
    bmat = jnp.concatenate([b_of(bbar_re), b_of(bbar_im)], axis=1)
    cmat = jnp.concatenate([c_of(c_re), -c_of(c_im)], axis=0)
    return bmat, cmat


def _s5_prepare(a_re, a_im, b_re, b_im, c_re, c_im, log_dt):
    lam_re = jnp.minimum(a_re, -1e-4)
    lam_im = a_im
    dt = jnp.exp(log_dt)[:, None]
    decay = jnp.exp(dt * lam_re)
    ang = dt * lam_im
    abar_re = decay * jnp.cos(ang)
    abar_im = decay * jnp.sin(ang)
    den = jnp.square(lam_re) + jnp.square(lam_im)
    nr = abar_re - 1.0
    ni = abar_im
    coef_re = (nr * lam_re + ni * lam_im) / den
    coef_im = (ni * lam_re - nr * lam_im) / den
    bbar_re = coef_re[..., None] * b_re - coef_im[..., None] * b_im
    bbar_im = coef_re[..., None] * b_im + coef_im[..., None] * b_re
    bmat, cmat = _blockdiag_s5(bbar_re, bbar_im, c_re, c_im)
    return abar_re.reshape(N_STATE), abar_im.reshape(N_STATE), bmat, cmat


def _blockdiag_lru(w):
    eye = jnp.eye(LRU_HEADS, dtype=F32)
    return jnp.einsum('hij,hk->hikj', w, eye).reshape(D_LRU, D_LRU)


def _rope_tables(t):
    inv_freq = ROPE_THETA ** (-jnp.arange(0, HEAD_DIM, 2, dtype=F32) / HEAD_DIM)
    ang = jnp.arange(t, dtype=F32)[:, None] * inv_freq[None, :]
    cos, sin = jnp.cos(ang), jnp.sin(ang)
    return jnp.tile(jnp.concatenate([cos, cos], axis=1), (1, 2)), jnp.tile(jnp.concatenate([-sin, sin], axis=1), (1, 2))


def _vec(v):
    return v.reshape(1, -1)


def _layer_weights(p):
    abar_re, abar_im, bmat, cmat = _s5_prepare(p['s5_a_re'], p['s5_a_im'], p['s5_b_re'], p['s5_b_im'],
                                               p['s5_c_re'], p['s5_c_im'], p['s5_log_dt'])
    return dict(
        coef_f=_s5_coefs(abar_re, abar_im, False), coef_b=_s5_coefs(abar_re, abar_im, True),
        bmat=bmat.astype(BF16), cmat=cmat.astype(BF16),
        wx=_blockdiag_lru(p['lru_wx']).astype(BF16), wa=_blockdiag_lru(p['lru_wa']).astype(BF16),
        gw=p['s5_glu_w'].astype(BF16))


def _layer_fwd(x, xb, p, w, cos, sin_s):
    t = x.shape[0]
    tm = _row_tile(t, TM)
    layer = p['layer']
    qkv, uxg = _in_proj(xb, p['w_in'], _vec(p['b_in']), cos, sin_s, layer)
    ya, lse = _attn_fwd(qkv, _vec(p['attn_sinks']))
    h5, ys = _s5_fwd(uxg, w['bmat'], w['coef_f'], w['cmat'], _vec(p['s5_d']), w['gw'], _vec(p['s5_glu_b']))
    lru_w = (p['lru_conv_w'], _vec(p['lru_conv_b']), w['wx'], _vec(p['lru_bx']), w['wa'], _vec(p['lru_ba']),
             _vec(p['lru_a_param']))
    yl, hl = _lru_fwd(uxg, *lru_w)
    mix, x1, x1b, xhat1, rstd1 = _mix_out_ln(ya, ys, yl, _vec(p['mix_norm_g']), p['w_out'], _vec(p['b_out']), x,
                                             _vec(p['ln1_g']), _vec(p['ln1_b']), layer)
    gpre, gconv, up, hmid = _ffn_hidden_fwd(x1b, p['ffn_w_gate'], p['ffn_w_up'], p['ffn_conv_w'], p['ffn_conv_b'],
                                            layer)
    x2, x2b, xhat2, rstd2 = _matmul_ln(
        "ffn_down_ln", hmid, p['ffn_w_down'], jnp.zeros((1, D_MODEL), F32), x1, _vec(p['ln2_g']), _vec(p['ln2_b']),
        a_blk=(N_CHIPS, tm, FF_SH), a_map=lambda i: (0, i, 0), w_blk=(N_CHIPS, FF_SH, D_MODEL), parts=N_CHIPS,
        layer=layer)
    saved = dict(xb=xb, uxg=uxg, qkv=qkv, ya=ya, lse=lse, h5=h5, ys=ys, yl=yl, hl=hl, mix=mix, x1b=x1b, xhat1=xhat1,
                 rstd1=rstd1, gpre=gpre, gconv=gconv, up=up, hmid=hmid, xhat2=xhat2, rstd2=rstd2, lru_w=lru_w)
    return x2, x2b, saved


def _layer_bwd(dr2, dr2b, s, p, w, cos, sin_s, big, below):
    t = dr2.shape[0]
    tk = _row_tile(t, TMM)
    nk = t // tk
    tm = _row_tile(t, TM)
    layer = p['layer']
    big = dict(big)
    g = {}
    dup, dgpre, g['ffn_conv_w'], g['ffn_conv_b'] = _ffn_hidden_bwd(
        dr2b, s['gpre'], s['gconv'], s['up'], p['ffn_w_down'], p['ffn_conv_w'], layer)
    big['ffn_w_down'] = _matmul(
        "d_w_down", s['hmid'], dr2b, a_blk=(None, tk, FF_SH), a_map=lambda i, j, k: (i, k, 0), b_blk=(tk, D_MODEL),
        b_map=lambda i, j, k: (k, 0), out_shape=(DEPTH, N_CHIPS, FF_SH, D_MODEL), o_blk=(None, None, FF_SH, D_MODEL),
        o_map=lambda i, j: (layer, i, 0, 0), grid=(N_CHIPS, 1, nk), dims=TN, into=big['ffn_w_down'])
    d_ffn_w = lambda name, dact, buf: _matmul(
        name, s['x1b'], dact, a_blk=(tk, D_MODEL), a_map=lambda i, j, k: (k, 0), b_blk=(None, tk, FF_SH),
        b_map=lambda i, j, k: (j, k, 0), out_shape=(DEPTH, N_CHIPS, D_MODEL, FF_SH),
        o_blk=(None, None, D_MODEL, FF_SH), o_map=lambda i, j: (layer, j, 0, 0), grid=(1, N_CHIPS, nk), dims=TN,
        into=buf)
    big['ffn_w_gate'] = d_ffn_w("d_w_gate", dgpre, big['ffn_w_gate'])
    big['ffn_w_up'] = d_ffn_w("d_w_up", dup, big['ffn_w_up'])
    wspec = dict(b_blk=(None, None, D_MODEL, FF_SH), b_map=lambda i, j, k: (layer, k, 0, 0))
    dx1 = _matmul(
        "d_x1", dgpre, p['ffn_w_gate'], pair2=(dup, p['ffn_w_up']), a_blk=(None, tk, FF_SH),
        a_map=lambda i, j, k: (k, i, 0), out_shape=(t, D_MODEL), o_blk=(tk, D_MODEL), o_map=lambda i, j: (i, 0),
        grid=(nk, 1, N_CHIPS), dims=NT, add=dr2, add_scale=ALPHA, **wspec)
    dr1, dr1b, g['ln1_g'], g['ln1_b'], g['b_out'], dya, dys, dyl, g['mix_norm_g'] = _d_mix_rms(
        dx1, s['xhat1'], s['rstd1'], _vec(p['ln1_g']), p['w_out'], s['ya'], s['ys'], s['yl'],
        _vec(p['mix_norm_g']), layer)
    big['w_out'] = _matmul(
        "d_w_out", s['mix'], dr1b, a_blk=(tk, D_MODEL), a_map=lambda i, j, k: (k, 0), b_blk=(tk, D_MODEL),
        b_map=lambda i, j, k: (k, 0), out_shape=(DEPTH, D_MODEL, D_MODEL), o_blk=(None, D_MODEL, D_MODEL),
        o_map=lambda i, j: (layer, 0, 0), grid=(1, 1, nk), dims=TN, into=big['w_out'])
    dq, dk, dv, g['attn_sinks'] = _attn_bwd(s['qkv'], s['ya'], dya, s['lse'], _vec(p['attn_sinks']))
    du, dcmat, dbmat, dabar, g['s5_d'], g['s5_glu_w'], g['s5_glu_b'] = _s5_bwd(
        s['uxg'], s['h5'], dys, w['bmat'], w['coef_b'], w['cmat'], _vec(p['s5_d']), w['gw'], _vec(p['s5_glu_b']))
    (dxc, dgate, g['lru_conv_w'], g['lru_conv_b'], dwx, g['lru_bx'], dwa, g['lru_ba'],
     g['lru_a_param']) = _lru_bwd(s['uxg'], s['hl'], dyl, *s['lru_w'])
    dproj, g['b_in'] = _assemble_dproj(dq, dk, dv, du, dxc, dgate, cos, sin_s, p['lru_conv_w'])
    big['w_in'] = _matmul(
        "d_w_in", s['xb'], dproj, a_blk=(tk, D_MODEL), a_map=lambda i, j, k: (k, 0), b_blk=(tk, IN_SH),
        b_map=lambda i, j, k: (k, j), out_shape=(DEPTH, N_CHIPS, D_MODEL, IN_SH), o_blk=(None, None, D_MODEL, IN_SH),
        o_map=lambda i, j: (layer, j, 0, 0), grid=(1, N_CHIPS, nk), dims=TN, into=big['w_in'])
    dx = _matmul("d_x", dproj, p['w_in'], a_blk=(tk, IN_SH), a_map=lambda i, j, k: (i, k),
                 b_blk=(None, None, D_MODEL, IN_SH), b_map=lambda i, j, k: (layer, k, 0, 0), out_shape=(t, D_MODEL),
                 o_blk=(tk, D_MODEL), o_map=lambda i, j: (i, 0), grid=(nk, 1, N_CHIPS), dims=NT,
                 add=dr1, add_scale=ALPHA)
    if below is not None:
        dx = _ln_bwd(dx, below[0]['xhat2'], below[0]['rstd2'], _vec(below[1]['ln2_g']))
    return dx, _param_chain(g, p, dabar, dbmat, dcmat, dwx, dwa), big


def _layer_fwd_v1(x, p, w, cos, sin_s):
    t = x.shape[0]
    nt = t // _row_tile(t, TM)
    tm = t // nt
    proj = _matmul("in_proj", x, p['w_in'], a_blk=(tm, D_MODEL), a_map=lambda i, j, k: (i, 0),
                   b_blk=(None, D_MODEL, IN_SH), b_map=lambda i, j, k: (j, 0, 0), out_shape=(t, D_IN),
                   o_blk=(tm, IN_SH), o_map=lambda i, j: (i, j), grid=(nt, N_CHIPS, 1), dims=NN,
                   bias=_vec(p['b_in']), bias_blk=(1, IN_SH), bias_map=lambda i, j, k: (0, j))
    qkv = _qkv_post(proj, cos, sin_s)
    ya, lse = _attn_fwd(qkv, _vec(p['attn_sinks']))
    h5, ys = _s5_fwd(proj, w['bmat'], w['coef_f'], w['cmat'], _vec(p['s5_d']), w['gw'], _vec(p['s5_glu_b']))
    lru_w = (p['lru_conv_w'], _vec(p['lru_conv_b']), w['wx'], _vec(p['lru_bx']), w['wa'], _vec(p['lru_ba']),
             _vec(p['lru_a_param']))
    yl, hl = _lru_fwd(proj, *lru_w)
    mix = _rms_fwd(ya, ys, yl, _vec(p['mix_norm_g']))
    f1 = _matmul("out_proj", mix, p['w_out'], a_blk=(tm, D_MODEL), a_map=lambda i, j, k: (i, 0),
                 b_blk=(D_MODEL, D_MODEL), b_map=lambda i, j, k: (0, 0), out_shape=(t, D_MODEL),
                 o_blk=(tm, D_MODEL), o_map=lambda i, j: (i, 0), grid=(nt, 1, 1), dims=NN,
                 bias=_vec(p['b_out']), bias_blk=(1, D_MODEL), bias_map=lambda i, j, k: (0, 0))
    x1, xhat1, rstd1 = _ln_fwd(x, f1, _vec(p['ln1_g']), _vec(p['ln1_b']))
    ffn_in = lambda name, wmat: _matmul(
        name, x1, wmat, a_blk=(tm, D_MODEL), a_map=lambda i, j, k: (i, 0), b_blk=(None, D_MODEL, FF_SH),
        b_map=lambda i, j, k: (j, 0, 0), out_shape=(N_CHIPS, t, FF_SH), o_blk=(None, tm, FF_SH),
        o_map=lambda i, j: (j, i, 0), grid=(nt, N_CHIPS, 1), dims=NN)
    gpre = ffn_in("ffn_gate", p['ffn_w_gate'])
    up = ffn_in("ffn_up", p['ffn_w_up'])
    hmid = _ffn_mid_fwd(gpre, up, p['ffn_conv_w'], p['ffn_conv_b'])
    f2 = _matmul("ffn_down", hmid, p['ffn_w_down'], a_blk=(None, tm, FF_SH), a_map=lambda i, j, k: (k, i, 0),
                 b_blk=(None, FF_SH, D_MODEL), b_map=lambda i, j, k: (k, 0, 0), out_shape=(t, D_MODEL),
                 o_blk=(tm, D_MODEL), o_map=lambda i, j: (i, 0), grid=(nt, 1, N_CHIPS), dims=NN)
    x2, xhat2, rstd2 = _ln_fwd(x1, f2, _vec(p['ln2_g']), _vec(p['ln2_b']))
    saved = dict(x=x, proj=proj, qkv=qkv, ya=ya, lse=lse, h5=h5, ys=ys, yl=yl, hl=hl, mix=mix, x1=x1, xhat1=xhat1,
                 rstd1=rstd1, gpre=gpre, up=up, xhat2=xhat2, rstd2=rstd2, lru_w=lru_w)
    return x2, saved


def _param_chain(g, p, dabar, dbmat, dcmat, dwx, dwa):
    s5_names = ('s5_a_re', 's5_a_im', 's5_b_re', 's5_b_im', 's5_c_re', 's5_c_im', 's5_log_dt')
    _, s5_vjp = jax.vjp(_s5_prepare, *[p[n] for n in s5_names])
    for n, val in zip(s5_names, s5_vjp((dabar[0], dabar[1], dbmat, dcmat))):
        g[n] = val
    g['lru_wx'] = jax.vjp(_blockdiag_lru, p['lru_wx'])[1](dwx)[0]
    g['lru_wa'] = jax.vjp(_blockdiag_lru, p['lru_wa'])[1](dwa)[0]
    return g


def _layer_bwd_v1(dx2, s, p, w, cos, sin_s):
    t = dx2.shape[0]
    nt = t // _row_tile(t, TM)
    tm = t // nt
    g = {}
    dr2, g['ln2_g'], g['ln2_b'], _ = _ln_bwd(dx2, s['xhat2'], s['rstd2'], _vec(p['ln2_g']))
    dhmid = _matmul("d_hmid", dr2, p['ffn_w_down'], a_blk=(tm, D_MODEL), a_map=lambda i, j, k: (i, 0),
                    b_blk=(None, FF_SH, D_MODEL), b_map=lambda i, j, k: (j, 0, 0), out_shape=(N_CHIPS, t, FF_SH),
                    o_blk=(None, tm, FF_SH), o_map=lambda i, j: (j, i, 0), grid=(nt, N_CHIPS, 1), dims=NT)
    hmid, dup, dgc, g['ffn_conv_w'], g['ffn_conv_b'] = _ffn_mid_bwd(s['gpre'], s['up'], dhmid, p['ffn_conv_w'],
                                                                    p['ffn_conv_b'])
    g['ffn_w_down'] = _matmul("d_w_down", hmid, dr2, a_blk=(None, tm, FF_SH), a_map=lambda i, j, k: (i, k, 0),
                              b_blk=(tm, D_MODEL), b_map=lambda i, j, k: (k, 0), out_shape=(N_CHIPS, FF_SH, D_MODEL),
                              o_blk=(None, FF_SH, D_MODEL), o_map=lambda i, j: (i, 0, 0), grid=(N_CHIPS, 1, nt), dims=TN)
    dgpre = _ffn_conv_t(dgc, p['ffn_conv_w'])
    d_ffn_w = lambda name, dact: _matmul(
        name, s['x1'], dact, a_blk=(tm, D_MODEL), a_map=lambda i, j, k: (k, 0), b_blk=(None, tm, FF_SH),
        b_map=lambda i, j, k: (j, k, 0), out_shape=(N_CHIPS, D_MODEL, FF_SH), o_blk=(None, D_MODEL, FF_SH),
        o_map=lambda i, j: (j, 0, 0), grid=(1, N_CHIPS, nt), dims=TN)
    g['ffn_w_gate'] = d_ffn_w("d_w_gate", dgpre)
    g['ffn_w_up'] = d_ffn_w("d_w_up", dup)
    d_ffn_x = lambda name, dact, wmat, add, scale: _matmul(
        name, dact, wmat, a_blk=(None, tm, FF_SH), a_map=lambda i, j, k: (k, i, 0), b_blk=(None, D_MODEL, FF_SH),
        b_map=lambda i, j, k: (k, 0, 0), out_shape=(t, D_MODEL), o_blk=(tm, D_MODEL), o_map=lambda i, j: (i, 0),
        grid=(nt, 1, N_CHIPS), dims=NT, add=add, add_scale=scale)
    dx1 = d_ffn_x("d_x1_gate", dgpre, p['ffn_w_gate'], dr2, ALPHA)
    dx1 = d_ffn_x("d_x1_up", dup, p['ffn_w_up'], dx1, 1.0)
    dr1, g['ln1_g'], g['ln1_b'], g['b_out'] = _ln_bwd(dx1, s['xhat1'], s['rstd1'], _vec(p['ln1_g']))
    g['w_out'] = _matmul("d_w_out", s['mix'], dr1, a_blk=(tm, D_MODEL), a_map=lambda i, j, k: (k, 0),
                         b_blk=(tm, D_MODEL), b_map=lambda i, j, k: (k, 0), out_shape=(D_MODEL, D_MODEL),
                         o_blk=(D_MODEL, D_MODEL), o_map=lambda i, j: (0, 0), grid=(1, 1, nt), dims=TN)
    dmix = _matmul("d_mix", dr1, p['w_out'], a_blk=(tm, D_MODEL), a_map=lambda i, j, k: (i, 0),
                   b_blk=(D_MODEL, D_MODEL), b_map=lambda i, j, k: (0, 0), out_shape=(t, D_MODEL),
                   o_blk=(tm, D_MODEL), o_map=lambda i, j: (i, 0), grid=(nt, 1, 1), dims=NT)
    dya, dys, dyl, g['mix_norm_g'] = _rms_bwd(dmix, s['ya'], s['ys'], s['yl'], _vec(p['mix_norm_g']))
    dq, dk, dv, g['attn_sinks'] = _attn_bwd(s['qkv'], s['ya'], dya, s['lse'], _vec(p['attn_sinks']))
    du, dcmat, dbmat, dabar, g['s5_d'], g['s5_glu_w'], g['s5_glu_b'] = _s5_bwd(
        s['proj'], s['h5'], dys, w['bmat'], w['coef_b'], w['cmat'], _vec(p['s5_d']), w['gw'], _vec(p['s5_glu_b']))
    (dxc, dgate, g['lru_conv_w'], g['lru_conv_b'], dwx, g['lru_bx'], dwa, g['lru_ba'],
     g['lru_a_param']) = _lru_bwd(s['proj'], s['hl'], dyl, *s['lru_w'])
    dproj, g['b_in'] = _assemble_dproj(dq, dk, dv, du, dxc, dgate, cos, sin_s, p['lru_conv_w'])
    g['w_in'] = _matmul("d_w_in", s['x'], dproj, a_blk=(tm, D_MODEL), a_map=lambda i, j, k: (k, 0),
                        b_blk=(tm, IN_SH), b_map=lambda i, j, k: (k, j), out_shape=(N_CHIPS, D_MODEL, IN_SH),
                        o_blk=(None, D_MODEL, IN_SH), o_map=lambda i, j: (j, 0, 0), grid=(1, N_CHIPS, nt), dims=TN)
    dx = _matmul("d_x", dproj, p['w_in'], a_blk=(tm, IN_SH), a_map=lambda i, j, k: (i, k),
                 b_blk=(None, D_MODEL, IN_SH), b_map=lambda i, j, k: (k, 0, 0), out_shape=(t, D_MODEL),
                 o_blk=(tm, D_MODEL), o_map=lambda i, j: (i, 0), grid=(nt, 1, N_CHIPS), dims=NT,
                 add=dr1, add_scale=ALPHA)
    s5_names = ('s5_a_re', 's5_a_im', 's5_b_re', 's5_b_im', 's5_c_re', 's5_c_im', 's5_log_dt')
    _, s5_vjp = jax.vjp(_s5_prepare, *[p[n] for n in s5_names])
    for n, val in zip(s5_names, s5_vjp((dabar[0], dabar[1], dbmat, dcmat))):
        g[n] = val
    g['lru_wx'] = jax.vjp(_blockdiag_lru, p['lru_wx'])[1](dwx)[0]
    g['lru_wa'] = jax.vjp(_blockdiag_lru, p['lru_wa'])[1](dwa)[0]
    return dx, g


ROW_TILE = 512


def _pick_rows(rows):
    for rt in range(min(rows, ROW_TILE), 0, -1):
        if rows % rt == 0 and (rt % 16 == 0 or rt == rows):
            return rt
    return rows


def _cast_bf16(a):
    a2 = a.reshape(-1, a.shape[-1])
    rows, c = a2.shape
    rt = _pick_rows(rows)

    def kern(a_ref, o_ref):
        o_ref[...] = a_ref[...].astype(BF16)

    spec = pl.BlockSpec((rt, c), lambda i: (i, 0))
    out = pl.pallas_call(kern, name="cast_bf16", grid=(rows // rt,), in_specs=[spec], out_specs=spec,
                         out_shape=jax.ShapeDtypeStruct((rows, c), BF16), compiler_params=_params(("parallel",)))(a2)
    return out.reshape(a.shape)


def _sum_parts(name, parts, shape):
    c = shape[-1]
    rows = math.prod(shape[:-1])
    rt = _pick_rows(rows)
    n = len(parts)

    def kern(*refs):
        acc = refs[0][...].astype(F32)
        for r in refs[1:n]:
            acc = acc + r[...].astype(F32)
        refs[n][...] = acc

    specs, args = [], []
    for arr, j in parts:
        if j is None:
            specs.append(pl.BlockSpec((rt, c), lambda i: (i, 0)))
            args.append(arr.reshape(rows, c))
        else:
            specs.append(pl.BlockSpec((None, rt, c), functools.partial(lambda i, jj: (jj, i, 0), jj=j)))
            args.append(arr.reshape(arr.shape[0], rows, c))
    out = pl.pallas_call(kern, name=name, grid=(rows // rt,), in_specs=specs,
                         out_specs=pl.BlockSpec((rt, c), lambda i: (i, 0)),
                         out_shape=jax.ShapeDtypeStruct((rows, c), F32), compiler_params=_params(("parallel",)))(*args)
    return out.reshape(shape)


def _adamw(name, w, g, m, v):
    shape = w.shape
    c = shape[-1]
    rows = math.prod(shape[:-1])
    rt = _pick_rows(rows)

    def kern(w_ref, g_ref, m_ref, v_ref, d_ref, nm_ref, nv_ref):
        g_ = g_ref[...]
        m_ = ADAM_B1 * m_ref[...] + (1.0 - ADAM_B1) * g_
        v_ = ADAM_B2 * v_ref[...] + (1.0 - ADAM_B2) * jnp.square(g_)
        m_hat = m_ / (1.0 - ADAM_B1 ** ADAM_STEP)
        v_hat = v_ / (1.0 - ADAM_B2 ** ADAM_STEP)
        d_ref[...] = -ADAM_LR * (m_hat / (jnp.sqrt(v_hat) + ADAM_EPS) + ADAM_WD * w_ref[...])
        nm_ref[...] = m_
        nv_ref[...] = v_

    spec = pl.BlockSpec((rt, c), lambda i: (i, 0))
    outs = pl.pallas_call(kern, name=name, grid=(rows // rt,), in_specs=[spec] * 4, out_specs=[spec] * 3,
                          out_shape=[jax.ShapeDtypeStruct((rows, c), F32)] * 3,
                          compiler_params=_params(("parallel",)))(*[a.reshape(rows, c) for a in (w, g, m, v)])
    return tuple(o.reshape(shape) for o in outs)


def _position():
    return lax.axis_index("x"), lax.axis_index("y"), lax.axis_index("c")


def _other_chips(x, y):
    return [(1 - x, y), (x, 1 - y), (1 - x, 1 - y)]


def _comm_call(name, kern, arrs, out_shapes, n_remote, n_local):
    return pl.pallas_call(
        kern, name=name, in_specs=[ANY] * len(arrs), out_specs=[ANY] * len(out_shapes), out_shape=out_shapes,
        scratch_shapes=[pltpu.SemaphoreType.DMA((n_remote,)), pltpu.SemaphoreType.DMA((n_remote,)),
                        pltpu.SemaphoreType.DMA((n_local,))],
    )(*arrs)


def _allgather_chips(arrs):
    n = len(arrs)

    def kern(*refs):
        ins, outs = refs[:n], refs[n:2 * n]
        send, recv, loc = refs[2 * n:]
        x, y, c = _position()
        me = 2 * x + y
        chips = _other_chips(x, y)
        own, sent = [], []
        for t in range(n):
            own.append(pltpu.make_async_copy(ins[t], outs[t].at[:, pl.ds(me, 1)], loc.at[t]))
            own[-1].start()
            for j, (px, py) in enumerate(chips):
                sent.append(pltpu.make_async_remote_copy(
                    src_ref=ins[t], dst_ref=outs[t].at[:, pl.ds(me, 1)], send_sem=send.at[3 * t + j],
                    recv_sem=recv.at[3 * t + j], device_id=(px, py, c), device_id_type=MESH))
                sent[-1].start()
        for t in range(n):
            for j, (px, py) in enumerate(chips):
                pltpu.make_async_remote_copy(
                    src_ref=ins[t], dst_ref=outs[t].at[:, pl.ds(2 * px + py, 1)], send_sem=send.at[3 * t + j],
                    recv_sem=recv.at[3 * t + j], device_id=(px, py, c), device_id_type=MESH).wait_recv()
        for cp in sent:
            cp.wait_send()
        for cp in own:
            cp.wait()

    outs = [jax.ShapeDtypeStruct((a.shape[0], N_CHIPS) + a.shape[2:], a.dtype) for a in arrs]
    return _comm_call("allgather_chips", kern, arrs, outs, 3 * n, n)


def _pair_exchange(arrs):
    n = len(arrs)

    def kern(*refs):
        ins, outs = refs[:n], refs[n:3 * n]
        send, recv, loc = refs[3 * n:]
        x, y, c = _position()
        own, sent = [], []
        for t in range(n):
            r2 = ins[t].shape[2] // 2
            own.append(pltpu.make_async_copy(ins[t].at[:, :, pl.ds(c * r2, r2)], outs[2 * t], loc.at[t]))
            own[-1].start()
            sent.append(pltpu.make_async_remote_copy(
                src_ref=ins[t].at[:, :, pl.ds((1 - c) * r2, r2)], dst_ref=outs[2 * t + 1], send_sem=send.at[t],
                recv_sem=recv.at[t], device_id=(x, y, 1 - c), device_id_type=MESH))
            sent[-1].start()
        for cp in sent:
            cp.wait()
        for cp in own:
            cp.wait()

    outs = []
    for a in arrs:
        half = jax.ShapeDtypeStruct(a.shape[:2] + (a.shape[2] // 2, a.shape[3]), a.dtype)
        outs += [half, half]
    return _comm_call("pair_exchange", kern, arrs, outs, n, n)


def _chip_scatter(arrs):
    n = len(arrs)

    def kern(*refs):
        ins, outs = refs[:n], refs[n:3 * n]
        send, recv, loc = refs[3 * n:]
        x, y, c = _position()
        me = 2 * x + y
        chips = _other_chips(x, y)
        own, sent = [], []
        for t in range(n):
            own.append(pltpu.make_async_copy(ins[t].at[:, pl.ds(me, 1)], outs[2 * t], loc.at[t]))
            own[-1].start()
            for j, (px, py) in enumerate(chips):
                sent.append(pltpu.make_async_remote_copy(
                    src_ref=ins[t].at[:, pl.ds(2 * px + py, 1)], dst_ref=outs[2 * t + 1].at[j],
                    send_sem=send.at[3 * t + j], recv_sem=recv.at[3 * t + j], device_id=(px, py, c),
                    device_id_type=MESH))
                sent[-1].start()
        for cp in sent:
            cp.wait()
        for cp in own:
            cp.wait()

    outs = []
    for a in arrs:
        one = (a.shape[0], 1) + a.shape[2:]
        outs += [jax.ShapeDtypeStruct(one, a.dtype), jax.ShapeDtypeStruct((3,) + one, a.dtype)]
    return _comm_call("chip_scatter", kern, arrs, outs, 3 * n, n)


def _pair_gather(arrs):
    n = len(arrs)

    def kern(*refs):
        ins, outs = refs[:n], refs[n:2 * n]
        send, recv, loc = refs[2 * n:]
        x, y, c = _position()
        own, sent = [], []
        for t in range(n):
            own.append(pltpu.make_async_copy(ins[t], outs[t].at[:, pl.ds(c, 1)], loc.at[t]))
            own[-1].start()
            sent.append(pltpu.make_async_remote_copy(
                src_ref=ins[t], dst_ref=outs[t].at[:, pl.ds(c, 1)], send_sem=send.at[t], recv_sem=recv.at[t],
                device_id=(x, y, 1 - c), device_id_type=MESH))
            sent[-1].start()
        for t in range(n):
            sent[t].wait_send()
            pltpu.make_async_remote_copy(
                src_ref=ins[t], dst_ref=outs[t].at[:, pl.ds(1 - c, 1)], send_sem=send.at[t], recv_sem=recv.at[t],
                device_id=(x, y, 1 - c), device_id_type=MESH).wait_recv()
        for cp in own:
            cp.wait()

    outs = [jax.ShapeDtypeStruct((a.shape[0], 2) + a.shape[2:], a.dtype) for a in arrs]
    return _comm_call("pair_gather", kern, arrs, outs, n, n)


_FLIPS = [(0, 0, 1), (1, 0, 0), (0, 1, 0), (1, 1, 0), (1, 0, 1), (0, 1, 1), (1, 1, 1)]


def _allgather_devices(v):
    def kern(v_ref, o_ref, send, recv, loc):
        x, y, c = _position()
        me = 4 * x + 2 * y + c
        peers = [((1 - x) if fx else x, (1 - y) if fy else y, (1 - c) if fc else c) for fx, fy, fc in _FLIPS]
        own = pltpu.make_async_copy(v_ref, o_ref.at[pl.ds(me, 1)], loc.at[0])
        own.start()
        sent = []
        for k, peer in enumerate(peers):
            sent.append(pltpu.make_async_remote_copy(
                src_ref=v_ref, dst_ref=o_ref.at[pl.ds(me, 1)], send_sem=send.at[k], recv_sem=recv.at[k],
                device_id=peer, device_id_type=MESH))
            sent[-1].start()
        for k, (px, py, pc) in enumerate(peers):
            pltpu.make_async_remote_copy(
                src_ref=v_ref, dst_ref=o_ref.at[pl.ds(4 * px + 2 * py + pc, 1)], send_sem=send.at[k],
                recv_sem=recv.at[k], device_id=(px, py, pc), device_id_type=MESH).wait_recv()
        for cp in sent:
            cp.wait_send()
        own.wait()

    out = jax.ShapeDtypeStruct((N_DEV,) + v.shape[1:], v.dtype)
    return _comm_call("allgather_devices", kern, [v], [out], len(_FLIPS), 1)[0]


def _exchange(name, arrs, out_shapes, n_local, n_remote, plan):
    n_in, n_out = len(arrs), len(out_shapes)

    def kern(*refs):
        ins, outs = refs[:n_in], refs[n_in:n_in + n_out]
        send, recv, loc = refs[n_in + n_out:]
        local, remote = plan(ins, outs, *_position())
        assert len(local) == n_local and len(remote) == n_remote
        own = [pltpu.make_async_copy(s, d, loc.at[k]) for k, (s, d) in enumerate(local)]
        for cp in own:
            cp.start()
        sent = [pltpu.make_async_remote_copy(src_ref=s, dst_ref=d, send_sem=send.at[k], recv_sem=recv.at[k],
                                             device_id=peer, device_id_type=MESH)
                for k, (s, d, peer, _) in enumerate(remote)]
        for cp in sent:
            cp.start()
        for k, (s, _, peer, landing) in enumerate(remote):
            pltpu.make_async_remote_copy(src_ref=s, dst_ref=landing, send_sem=send.at[k], recv_sem=recv.at[k],
                                         device_id=peer, device_id_type=MESH).wait_recv()
        for cp in sent:
            cp.wait_send()
        for cp in own:
            cp.wait()

    return pl.pallas_call(
        kern, name=name, in_specs=[ANY] * n_in, out_specs=[ANY] * n_out, out_shape=out_shapes,
        scratch_shapes=[pltpu.SemaphoreType.DMA((n_remote,)), pltpu.SemaphoreType.DMA((n_remote,)),
                        pltpu.SemaphoreType.DMA((max(n_local, 1),))],
    )(*arrs)


def _allgather_chips(arrs, halved=()):
    n = len(arrs)
    layers = arrs[0].shape[0]

    def plan(ins, outs, x, y, c):
        me = 2 * x + y
        local, remote = [], []
        for t in range(n):
            for l in range(layers):
                src = ins[t].at[l]
                if t in halved:
                    r2 = ins[t].shape[2] // 2
                    src = ins[t].at[l, :, pl.ds(c * r2, r2)]
                local.append((src, outs[t].at[l, pl.ds(me, 1)]))
                for px, py in _other_chips(x, y):
                    remote.append((src, outs[t].at[l, pl.ds(me, 1)], (px, py, c),
                                   outs[t].at[l, pl.ds(2 * px + py, 1)]))
        return local, remote

    outs = []
    for t, a in enumerate(arrs):
        tail = (a.shape[2] // 2,) + a.shape[3:] if t in halved else a.shape[2:]
        outs.append(jax.ShapeDtypeStruct((a.shape[0], N_CHIPS) + tail, a.dtype))
    return _exchange("allgather_chips", arrs, outs, n * layers, 3 * n * layers, plan)


def _pair_exchange(arrs):
    n = len(arrs)
    layers, shards = arrs[0].shape[:2]

    def plan(ins, outs, x, y, c):
        local, remote = [], []
        for t in range(n):
            r2 = ins[t].shape[2] // 2
            for l in range(layers):
                for s in range(shards):
                    local.append((ins[t].at[l, s, pl.ds(c * r2, r2)], outs[2 * t].at[l, s]))
                    remote.append((ins[t].at[l, s, pl.ds((1 - c) * r2, r2)], outs[2 * t + 1].at[l, s],
                                   (x, y, 1 - c), outs[2 * t + 1].at[l, s]))
        return local, remote

    outs = []
    for a in arrs:
        half = jax.ShapeDtypeStruct(a.shape[:2] + (a.shape[2] // 2, a.shape[3]), a.dtype)
        outs += [half, half]
    return _exchange("pair_exchange", arrs, outs, n * layers * shards, n * layers * shards, plan)


def _chip_scatter(arrs):
    n = len(arrs)
    layers = arrs[0].shape[0]

    def plan(ins, outs, x, y, c):
        me = 2 * x + y
        local, remote = [], []
        for t in range(n):
            for l in range(layers):
                local.append((ins[t].at[l, pl.ds(me, 1)], outs[2 * t].at[l]))
                for j, (px, py) in enumerate(_other_chips(x, y)):
                    remote.append((ins[t].at[l, pl.ds(2 * px + py, 1)], outs[2 * t + 1].at[j, l], (px, py, c),
                                   outs[2 * t + 1].at[j, l]))
        return local, remote

    outs = []
    for a in arrs:
        one = (a.shape[0], 1) + a.shape[2:]
        outs += [jax.ShapeDtypeStruct(one, a.dtype), jax.ShapeDtypeStruct((3,) + one, a.dtype)]
    return _exchange("chip_scatter", arrs, outs, n * layers, 3 * n * layers, plan)


def _pair_gather(arrs):
    n = len(arrs)
    layers = arrs[0].shape[0]

    def plan(ins, outs, x, y, c):
        local, remote = [], []
        for t in range(n):
            for l in range(layers):
                local.append((ins[t].at[l], outs[t].at[l, pl.ds(c, 1)]))
                remote.append((ins[t].at[l], outs[t].at[l, pl.ds(c, 1)], (x, y, 1 - c),
                               outs[t].at[l, pl.ds(1 - c, 1)]))
        return local, remote

    outs = [jax.ShapeDtypeStruct((a.shape[0], 2) + a.shape[2:], a.dtype) for a in arrs]
    return _exchange("pair_gather", arrs, outs, n * layers, n * layers, plan)


def _allgather_devices(v):
    def kern(v_ref, o_ref, send, recv, loc):
        x, y, c = _position()
        me, sibling = (x, y, c), (x, y, 1 - c)
        chips = _other_chips(x, y)

        def rows(px, py, pc):
            return o_ref.at[pl.ds(4 * px + 2 * py + pc, 1)]

        def copy(k, block, to, src=None):
            return pltpu.make_async_remote_copy(
                src_ref=rows(*block) if src is None else src, dst_ref=rows(*block), send_sem=send.at[k],
                recv_sem=recv.at[k], device_id=to, device_id_type=MESH)

        mine = pltpu.make_async_copy(v_ref, rows(*me), loc.at[0])
        mine.start()
        first = [copy(0, me, sibling, src=v_ref)]
        first += [copy(1 + j, me, (*chip, c), src=v_ref) for j, chip in enumerate(chips)]
        for cp in first:
            cp.start()
        passed = [copy(4 + j, (*chip, c), sibling) for j, chip in enumerate(chips)]
        for j, chip in enumerate(chips):
            copy(1 + j, (*chip, c), me).wait_recv()
            passed[j].start()
        copy(0, sibling, me).wait_recv()
        for j, chip in enumerate(chips):
            copy(4 + j, (*chip, 1 - c), me).wait_recv()
        for cp in first + passed:
            cp.wait_send()
        mine.wait()

    vmem = pl.BlockSpec(memory_space=pltpu.VMEM)
    return pl.pallas_call(
        kern, name="allgather_devices", in_specs=[vmem], out_specs=vmem,
        out_shape=jax.ShapeDtypeStruct((N_DEV,) + v.shape[1:], v.dtype),
        scratch_shapes=[pltpu.SemaphoreType.DMA((7,)), pltpu.SemaphoreType.DMA((7,)), pltpu.SemaphoreType.DMA((1,))],
        compiler_params=pltpu.CompilerParams(vmem_limit_bytes=VMEM_MB << 20),
    )(v)


WEIGHTS = ['w_in', 'b_in', 'attn_sinks', 's5_a_re', 's5_a_im', 's5_b_re', 's5_b_im', 's5_c_re', 's5_c_im', 's5_d',
           's5_log_dt', 's5_glu_w', 's5_glu_b', 'lru_conv_w', 'lru_conv_b', 'lru_wx', 'lru_bx', 'lru_wa', 'lru_ba',
           'lru_a_param', 'mix_norm_g', 'w_out', 'b_out', 'ln1_g', 'ln1_b', 'ffn_w_gate', 'ffn_w_up', 'ffn_conv_w',
           'ffn_conv_b', 'ffn_w_down', 'ln2_g', 'ln2_b']
BIG = ('w_in', 'w_out', 'ffn_w_gate', 'ffn_w_up', 'ffn_w_down')
SMALL = tuple(n for n in WEIGHTS if n not in BIG)
PACK_ROWS = ROW_TILE


def _pack(arrs):
    flat = jnp.concatenate([a.reshape(-1) for a in arrs])
    unit = 128 * PACK_ROWS
    size = -(-flat.shape[0] // unit) * unit
    return jnp.pad(flat, (0, size - flat.shape[0])).reshape(-1, 128)


def _unpack(packed, shapes):
    flat = packed.reshape(-1)
    out, pos = [], 0
    for shp in shapes:
        n = math.prod(shp)
        out.append(flat[pos:pos + n].reshape(shp))
        pos += n
    return out


def _pair_reduce(name, g):
    layers, shards, rows, cols = g.shape
    r2 = rows // 2
    rt = _pick_rows(r2)
    nr = r2 // rt
    nsteps = layers * shards * nr

    def kern(c_ref, mine_ref, other_ref, o_ref, buf, send, recv, credit):
        x, y, c = _position()
        sibling = (x, y, 1 - c)
        k = pl.program_id(0) * nr + pl.program_id(1)
        slot = k % 2

        @pl.when(k >= 2)
        def _():
            pl.semaphore_wait(credit, 1)

        cp = pltpu.make_async_remote_copy(src_ref=other_ref, dst_ref=buf.at[slot], send_sem=send.at[slot],
                                          recv_sem=recv.at[slot], device_id=sibling, device_id_type=MESH)
        cp.start()
        cp.wait_recv()
        o_ref[...] = (mine_ref[...] + buf[slot]).astype(BF16)
        cp.wait_send()

        @pl.when(k + 2 < nsteps)
        def _():
            pl.semaphore_signal(credit, 1, device_id=sibling, device_id_type=MESH)

    blk = (1, rt, cols)
    grid_spec = pltpu.PrefetchScalarGridSpec(
        num_scalar_prefetch=1, grid=(layers * shards, nr),
        in_specs=[pl.BlockSpec(blk, lambda m, r, c_ref: (m, c_ref[0] * nr + r, 0)),
                  pl.BlockSpec(blk, lambda m, r, c_ref: (m, (1 - c_ref[0]) * nr + r, 0))],
        out_specs=pl.BlockSpec(blk, lambda m, r, c_ref: (m, r, 0)),
        scratch_shapes=[pltpu.VMEM((2,) + blk, F32), pltpu.SemaphoreType.DMA((2,)),
                        pltpu.SemaphoreType.DMA((2,)), pltpu.SemaphoreType.REGULAR])
    core = lax.axis_index("c").astype(jnp.int32).reshape(1)
    g3 = g.reshape(layers * shards, rows, cols)
    out = pl.pallas_call(
        kern, name=name, grid_spec=grid_spec,
        out_shape=jax.ShapeDtypeStruct((layers * shards, r2, cols), BF16),
        compiler_params=_params(("arbitrary", "arbitrary")),
    )(core, g3, g3)
    return out.reshape(layers, shards, r2, cols)


def _pair_merge(name, h):
    m, r2, cols = h.shape
    rt = _pick_rows(r2)
    nr = r2 // rt
    nsteps = m * nr

    def kern(h_ref, o_ref, buf, send, recv, credit):
        x, y, c = _position()
        sibling = (x, y, 1 - c)
        k = pl.program_id(0) * nr + pl.program_id(1)
        slot = k % 2

        @pl.when(k >= 2)
        def _():
            pl.semaphore_wait(credit, 1)

        cp = pltpu.make_async_remote_copy(src_ref=h_ref, dst_ref=buf.at[slot], send_sem=send.at[slot],
                                          recv_sem=recv.at[slot], device_id=sibling, device_id_type=MESH)
        cp.start()
        cp.wait_recv()
        o_ref[0, pl.ds(c, 1)] = h_ref[...]
        o_ref[0, pl.ds(1 - c, 1)] = buf[slot]
        cp.wait_send()

        @pl.when(k + 2 < nsteps)
        def _():
            pl.semaphore_signal(credit, 1, device_id=sibling, device_id_type=MESH)

    blk = (1, rt, cols)
    out = pl.pallas_call(
        kern, name=name, grid=(m, nr),
        in_specs=[pl.BlockSpec(blk, lambda i, r: (i, r, 0))],
        out_specs=pl.BlockSpec((1, 2, rt, cols), lambda i, r: (i, 0, r, 0)),
        out_shape=jax.ShapeDtypeStruct((m, 2, r2, cols), h.dtype),
        scratch_shapes=[pltpu.VMEM((2,) + blk, h.dtype), pltpu.SemaphoreType.DMA((2,)),
                        pltpu.SemaphoreType.DMA((2,)), pltpu.SemaphoreType.REGULAR],
        compiler_params=_params(("arbitrary", "arbitrary")),
    )(h)
    return out.reshape(m, 2 * r2, cols)


def _reduce_big(grads):
    pair = [_pair_reduce("pair_reduce_" + n, g) for n, g in zip(BIG, grads)]
    scat = _chip_scatter(pair)
    out = []
    for t, n in enumerate(BIG):
        own, got = scat[2 * t], scat[2 * t + 1]
        half = _sum_parts("chip_sum", [(own, None)] + [(got, j) for j in range(3)], own.shape)
        out.append(_pair_merge("grad_merge_" + n, half.reshape(half.shape[0], half.shape[2], half.shape[3])))
    return out


def _step(a):
    x = a['x'][0]
    target = a['loss_target'][0]
    t = x.shape[0]
    xi, yi, _ = _position()
    chip = 2 * xi + yi
    cos, sin_s = _rope_tables(t)

    gathered = _allgather_chips([_cast_bf16(a[n])[:, None] for n in BIG]
                                + [a[n][:, None] for n in ('s5_glu_w', 'lru_conv_w', 'ffn_conv_w')],
                                halved=range(len(BIG)))
    full = dict(zip(BIG + ('s5_glu_w', 'lru_conv_w', 'ffn_conv_w'), gathered))
    for n in BIG:
        layers, chips, r2, cols = full[n].shape
        full[n] = _pair_merge("weight_merge_" + n, full[n].reshape(layers * chips, r2, cols)).reshape(
            layers, chips, 2 * r2, cols)

    def layer_params(l):
        p = {n: a[n][l] for n in SMALL}
        p['layer'] = l
        p['w_in'] = full['w_in']
        p['w_out'] = full['w_out'].reshape(DEPTH, D_MODEL, D_MODEL)
        p['ffn_w_gate'] = full['ffn_w_gate']
        p['ffn_w_up'] = full['ffn_w_up']
        p['ffn_w_down'] = full['ffn_w_down']
        p['s5_glu_w'] = full['s5_glu_w'][l].reshape(D_S5, D_S5)
        p['lru_conv_w'] = full['lru_conv_w'][l].transpose(1, 0, 2).reshape(LRU_CONV, D_LRU)
        p['ffn_conv_w'] = full['ffn_conv_w'][l]
        p['ffn_conv_b'] = a['ffn_conv_b'][l].reshape(N_CHIPS, 1, FF_SH)
        return p

    params = [layer_params(l) for l in range(DEPTH)]
    derived = [_layer_weights(p) for p in params]
    saved = []
    h, hb = x, _cast_bf16(x)
    for l in range(DEPTH):
        h, hb, s = _layer_fwd(h, hb, params[l], derived[l], cos, sin_s)
        saved.append(s)
    loss_part, dr, drb, ln2_g, ln2_b, _ = _loss_head(h, target, saved[-1]['xhat2'], saved[-1]['rstd2'],
                                                     _vec(params[-1]['ln2_g']))
    loss = lax.psum(loss_part[0, 0], ("x", "y", "c"))
    grads = [None] * DEPTH
    big = {n: lax.empty((DEPTH, N_CHIPS) + a[n].shape[1:], F32) for n in BIG}
    big['w_out'] = big['w_out'].reshape(DEPTH, D_MODEL, D_MODEL)
    for l in reversed(range(DEPTH)):
        below = (saved[l - 1], params[l - 1]) if l > 0 else None
        out, grads[l], big = _layer_bwd(dr, drb, saved[l], params[l], derived[l], cos, sin_s, big, below)
        grads[l]['ln2_g'], grads[l]['ln2_b'] = ln2_g, ln2_b
        if l > 0:
            dr, drb, ln2_g, ln2_b, _ = out
        else:
            grad_x = out[None]

    def stacked(n):
        return jnp.stack([grads[l][n] for l in range(DEPTH)])

    big['w_out'] = big['w_out'].reshape(DEPTH, N_CHIPS, OUT_SH, D_MODEL)
    grad = dict(zip(BIG, _reduce_big([big[n] for n in BIG])))
    small_local = [stacked(n) for n in SMALL]
    packed = _allgather_devices(_pack(small_local)[None])
    total = _sum_parts("device_sum", [(packed, j) for j in range(N_DEV)], packed.shape[1:])
    small_sum = dict(zip(SMALL, _unpack(total, [g.shape for g in small_local])))
    for n in SMALL:
        g = small_sum[n]
        if n == 's5_glu_w':
            g = lax.dynamic_slice_in_dim(g, chip * (D_S5 // N_CHIPS), D_S5 // N_CHIPS, axis=1)
        elif n == 'lru_conv_w':
            g = lax.dynamic_slice_in_dim(g, chip * (D_LRU // N_CHIPS), D_LRU // N_CHIPS, axis=2)
        elif n == 'ffn_conv_w':
            g = lax.dynamic_index_in_dim(g, chip, axis=1, keepdims=False)
        grad[n] = g.reshape(a[n].shape)

    delta, new_m, new_v = {}, {}, {}
    for n in WEIGHTS:
        delta[n], new_m[n], new_v[n] = _adamw("adamw_" + n, a[n], grad[n], a['m_' + n], a['v_' + n])
    return (loss, grad_x, *[grad[n] for n in WEIGHTS], *[delta[n] for n in WEIGHTS],
            *[new_m[n] for n in WEIGHTS], *[new_v[n] for n in WEIGHTS])


def kernel(x, w_in, b_in, attn_sinks, s5_a_re, s5_a_im, s5_b_re, s5_b_im, s5_c_re, s5_c_im, s5_d, s5_log_dt, s5_glu_w, s5_glu_b, lru_conv_w, lru_conv_b, lru_wx, lru_bx, lru_wa, lru_ba, lru_a_param, mix_norm_g, w_out, b_out, ln1_g, ln1_b, ffn_w_gate, ffn_w_up, ffn_conv_w, ffn_conv_b, ffn_w_down, ln2_g, ln2_b, loss_target, m_w_in, m_b_in, m_attn_sinks, m_s5_a_re, m_s5_a_im, m_s5_b_re, m_s5_b_im, m_s5_c_re, m_s5_c_im, m_s5_d, m_s5_log_dt, m_s5_glu_w, m_s5_glu_b, m_lru_conv_w, m_lru_conv_b, m_lru_wx, m_lru_bx, m_lru_wa, m_lru_ba, m_lru_a_param, m_mix_norm_g, m_w_out, m_b_out, m_ln1_g, m_ln1_b, m_ffn_w_gate, m_ffn_w_up, m_ffn_conv_w, m_ffn_conv_b, m_ffn_w_down, m_ln2_g, m_ln2_b, v_w_in, v_b_in, v_attn_sinks, v_s5_a_re, v_s5_a_im, v_s5_b_re, v_s5_b_im, v_s5_c_re, v_s5_c_im, v_s5_d, v_s5_log_dt, v_s5_glu_w, v_s5_glu_b, v_lru_conv_w, v_lru_conv_b, v_lru_wx, v_lru_bx, v_lru_wa, v_lru_ba, v_lru_a_param, v_mix_norm_g, v_w_out, v_b_out, v_ln1_g, v_ln1_b, v_ffn_w_gate, v_ffn_w_up, v_ffn_conv_w, v_ffn_conv_b, v_ffn_w_down, v_ln2_g, v_ln2_b):
    return _step(dict(locals()))
```

```python
import functools
import math

import jax
import jax.numpy as jnp
from jax import lax
from jax.experimental import pallas as pl
from jax.experimental.pallas import tpu as pltpu

F32 = jnp.float32
BF16 = jnp.bfloat16
MESH = pl.DeviceIdType.MESH
ANY = pl.BlockSpec(memory_space=pl.ANY)

D_MODEL = 1024
DEPTH = 4
HEAD_DIM = 64
N_Q_HEADS = 8
N_KV_HEADS = 2
Q_PER_KV = 4
D_ATTN = 512
D_KV = 128
ATTN_BLOCK = 128
ROPE_THETA = 10000.0
D_S5 = 256
S5_GROUP = 16
S5_GROUPS = 16
S5_STATE = 64
N_STATE = S5_GROUPS * S5_STATE
D_LRU = 256
LRU_HEADS = 4
LRU_HEAD_DIM = 64
LRU_CONV = 4
LRU_C = 8.0
D_IN = 1536
D_FF = 2816
FFN_CONV = 3
N_CHIPS = 4
N_DEV = 8
IN_SH = D_IN // N_CHIPS
FF_SH = D_FF // N_CHIPS
OUT_SH = D_MODEL // N_CHIPS
ALPHA = (2 * DEPTH) ** 0.25
LN_EPS = 1e-5
RMS_EPS = 1e-6
ADAM_LR = 0.001
ADAM_B1 = 0.9
ADAM_B2 = 0.999
ADAM_EPS = 1e-08
ADAM_WD = 0.01
ADAM_STEP = 10

SUBLANES = 8
VMEM_MB = 56


def _params(sem):
    return pltpu.CompilerParams(dimension_semantics=sem, vmem_limit_bytes=VMEM_MB << 20)


def _row_tile(t, pref):
    return min(t, pref)


def _matmul(name, a, b, *, a_blk, a_map, b_blk, b_map, out_shape, o_blk, o_map, grid, dims,
            out_dtype=F32, bias=None, bias_blk=None, bias_map=None, add=None, add_scale=1.0, pair2=None,
            into=None, ln_bwd=None):
    nk = grid[2]
    acc_shape = tuple(d for d in o_blk if d is not None)
    n_in = 2 if pair2 is None else 4

    def kern(*refs):
        p = n_in
        bias_ref = add_ref = None
        if bias is not None:
            bias_ref = refs[p]
            p += 1
        if add is not None:
            add_ref = refs[p]
            p += 1
        if into is not None:
            p += 1
        if ln_bwd is not None:
            ln_in = refs[p:p + 3]
            ln_out = refs[p + 4:p + 8]
            o_ref, acc = refs[p + 3], refs[p + 8]
        else:
            o_ref, acc = refs[p], refs[p + 1]
        k = pl.program_id(2)
        first_tile = pl.program_id(0) == 0

        def product():
            r = _dot(refs[0][...].astype(BF16), refs[1][...].astype(BF16), dims)
            if pair2 is not None:
                r = r + _dot(refs[2][...].astype(BF16), refs[3][...].astype(BF16), dims)
            return r

        def finish(r):
            if bias_ref is not None:
                r = r + bias_ref[...]
            if add_ref is not None:
                r = r + add_scale * add_ref[...]
            if ln_bwd is None:
                o_ref[...] = r.astype(out_dtype)
            else:
                @pl.when(first_tile)
                def _():
                    for ref in ln_out[1:]:
                        ref[...] = jnp.zeros_like(ref)

                _ln_bwd_tile(r, ln_in[0][...], ln_in[1][...], ln_in[2][...], o_ref, *ln_out)

        if nk == 1:
            finish(product())
        else:
            @pl.when(k == 0)
            def _():
                acc[...] = jnp.zeros_like(acc)

            acc[...] += product()

            @pl.when(k == nk - 1)
            def _():
                finish(acc[...])

    in_specs = [pl.BlockSpec(a_blk, a_map), pl.BlockSpec(b_blk, b_map)]
    args = [a, b]
    if pair2 is not None:
        in_specs += [pl.BlockSpec(a_blk, a_map), pl.BlockSpec(b_blk, b_map)]
        args += list(pair2)
    if bias is not None:
        in_specs.append(pl.BlockSpec(bias_blk, bias_map))
        args.append(bias)
    if add is not None:
        in_specs.append(pl.BlockSpec(o_blk, lambda i, j, k: o_map(i, j)))
        args.append(add)
    aliases = {}
    if into is not None:
        aliases = {len(args): 0}
        in_specs.append(ANY)
        args.append(into)
    o_spec = pl.BlockSpec(o_blk, lambda i, j, k: o_map(i, j))
    out_specs, out_shapes = o_spec, jax.ShapeDtypeStruct(out_shape, out_dtype)
    semantics = ("parallel", "parallel", "arbitrary")
    if ln_bwd is not None:
        vec = pl.BlockSpec((1, o_blk[-1]), lambda i, j, k: (0, 0))
        in_specs += [o_spec, pl.BlockSpec((o_blk[0], 1), lambda i, j, k: (i, 0)), vec]
        args += list(ln_bwd)
        vshape = jax.ShapeDtypeStruct((1, o_blk[-1]), F32)
        out_specs = [o_spec, o_spec, vec, vec, vec]
        out_shapes = [out_shapes, jax.ShapeDtypeStruct(out_shape, BF16), vshape, vshape, vshape]
        semantics = ("arbitrary", "arbitrary", "arbitrary")
    return pl.pallas_call(
        kern, name=name, grid=grid, in_specs=in_specs, out_specs=out_specs, out_shape=out_shapes,
        scratch_shapes=[pltpu.VMEM(acc_shape if nk > 1 else (SUBLANES, 128), F32)],
        input_output_aliases=aliases,
        compiler_params=_params(semantics),
    )(*args)


NN = ((1,), (0,))
NT = ((1,), (1,))
TN = ((0,), (0,))
TM = 512


def _sigmoid(x):
    return 0.5 * jnp.tanh(0.5 * x) + 0.5


_GELU_C = math.sqrt(2.0 / math.pi)


def _gelu(x):
    return 0.5 * x * (1.0 + jnp.tanh(_GELU_C * (x + 0.044715 * x * x * x)))


def _gelu_grad(x):
    th = jnp.tanh(_GELU_C * (x + 0.044715 * x * x * x))
    return 0.5 * (1.0 + th) + 0.5 * x * (1.0 - th * th) * _GELU_C * (1.0 + 3 * 0.044715 * x * x)


def _rope_swap(t):
    lane = lax.broadcasted_iota(jnp.int32, t.shape, 1)
    lo = (lane % HEAD_DIM) < (HEAD_DIM // 2)
    return jnp.where(lo, pltpu.roll(t, 128 - HEAD_DIM // 2, 1), pltpu.roll(t, HEAD_DIM // 2, 1))


D_QKV = D_ATTN + 2 * D_KV
TMM = 1024


def _in_proj(xb, w_in, b_in, cos, sin_s, layer):
    t = xb.shape[0]
    tm = _row_tile(t, TMM)

    def kern(x_ref, w_ref, b_ref, c_ref, s_ref, q_ref, u_ref):
        x = x_ref[...]
        c = c_ref[...]
        s = s_ref[...]
        for j in range(N_CHIPS):
            pj = _dot(x, w_ref[j], NN) + b_ref[:, j * IN_SH:(j + 1) * IN_SH]
            for ch in range(IN_SH // 128):
                col = j * IN_SH + ch * 128
                v = pj[:, ch * 128:(ch + 1) * 128]
                if col < D_ATTN + D_KV:
                    v = v * c + _rope_swap(v) * s
                if col < D_ATTN:
                    v = v * (HEAD_DIM ** -0.5)
                if col < D_QKV:
                    q_ref[:, col:col + 128] = v.astype(BF16)
                else:
                    u_ref[:, col - D_QKV:col - D_QKV + 128] = v

    row = lambda w: pl.BlockSpec((tm, w), lambda i: (i, 0))
    return pl.pallas_call(
        kern, name="in_proj", grid=(t // tm,),
        in_specs=[row(D_MODEL), pl.BlockSpec((None, N_CHIPS, D_MODEL, IN_SH), lambda i: (layer, 0, 0, 0)),
                  pl.BlockSpec((1, D_IN), lambda i: (0, 0)), row(128), row(128)],
        out_specs=[row(D_QKV), row(D_IN - D_QKV)],
        out_shape=[jax.ShapeDtypeStruct((t, D_QKV), BF16), jax.ShapeDtypeStruct((t, D_IN - D_QKV), F32)],
        compiler_params=_params(("parallel",)),
    )(xb, w_in, b_in, cos, sin_s)


def _attn_mask(i):
    qi = lax.broadcasted_iota(jnp.int32, (ATTN_BLOCK, 2 * ATTN_BLOCK), 0)
    si = lax.broadcasted_iota(jnp.int32, (ATTN_BLOCK, 2 * ATTN_BLOCK), 1)
    diff = qi + ATTN_BLOCK - si
    return (diff >= 0) & (diff < ATTN_BLOCK) & ((si >= ATTN_BLOCK) | (i > 0))


def _row_sums(x, ones):
    hi = x.astype(BF16)
    lo = (x - hi.astype(F32)).astype(BF16)
    return _dot(hi, ones, NN) + _dot(lo, ones, NN)


def _attn_fwd(qkv, sinks):
    t = qkv.shape[0]
    nb = t // ATTN_BLOCK

    def kern(q_ref, kp_ref, kc_ref, vp_ref, vc_ref, s_ref, o_ref, l_ref):
        i = pl.program_id(0)
        si = lax.broadcasted_iota(jnp.int32, (2 * ATTN_BLOCK, ATTN_BLOCK), 0)
        qi = lax.broadcasted_iota(jnp.int32, (2 * ATTN_BLOCK, ATTN_BLOCK), 1)
        diff = qi + ATTN_BLOCK - si
        valid = (diff >= 0) & (diff < ATTN_BLOCK) & ((si >= ATTN_BLOCK) | (i > 0))
        kband = jnp.concatenate([kp_ref[...], kc_ref[...]], axis=0)
        vband = jnp.concatenate([vp_ref[...], vc_ref[...]], axis=0)
        ks = [kband[:, kh * HEAD_DIM:(kh + 1) * HEAD_DIM] for kh in range(N_KV_HEADS)]
        vs = [vband[:, kh * HEAD_DIM:(kh + 1) * HEAD_DIM] for kh in range(N_KV_HEADS)]
        scores = [_dot(ks[h // Q_PER_KV], q_ref[:, h * HEAD_DIM:(h + 1) * HEAD_DIM], NT) for h in range(N_Q_HEADS)]
        probs, lses = [], []
        for h in range(N_Q_HEADS):
            s = jnp.where(valid, scores[h], -jnp.inf)
            sink = s_ref[0:1, h:h + 1]
            m = jnp.maximum(jnp.max(s, axis=0, keepdims=True), sink)
            e = jnp.exp(s - m)
            denom = jnp.sum(e, axis=0, keepdims=True) + jnp.exp(sink - m)
            probs.append((e * (1.0 / denom)).astype(BF16))
            lses.append(m + jnp.log(denom))
        outs = [_dot(vs[h // Q_PER_KV], probs[h], TN) for h in range(N_Q_HEADS)]
        for c in range(N_Q_HEADS // 2):
            o_ref[:, c * 128:(c + 1) * 128] = jnp.concatenate([outs[2 * c], outs[2 * c + 1]], axis=0).T
        rid = lax.broadcasted_iota(jnp.int32, (N_Q_HEADS, ATTN_BLOCK), 0)
        rows = jnp.zeros((N_Q_HEADS, ATTN_BLOCK), F32)
        for h in range(N_Q_HEADS):
            rows = jnp.where(rid == h, lses[h], rows)
        rows = jnp.concatenate([rows, jnp.zeros((ATTN_BLOCK - N_Q_HEADS, ATTN_BLOCK), F32)], axis=0)
        l_ref[...] = rows.T[:, 0:N_Q_HEADS]

    blk = lambda w, f: pl.BlockSpec((ATTN_BLOCK, w), f)
    return pl.pallas_call(
        kern, name="attn_fwd", grid=(nb,),
        in_specs=[blk(512, lambda i: (i, 0)),
                  blk(128, lambda i: (jnp.maximum(i - 1, 0), 4)), blk(128, lambda i: (i, 4)),
                  blk(128, lambda i: (jnp.maximum(i - 1, 0), 5)), blk(128, lambda i: (i, 5)),
                  pl.BlockSpec((1, N_Q_HEADS), lambda i: (0, 0))],
        out_specs=[blk(512, lambda i: (i, 0)), blk(N_Q_HEADS, lambda i: (i, 0))],
        out_shape=[jax.ShapeDtypeStruct((t, D_ATTN), F32), jax.ShapeDtypeStruct((t, N_Q_HEADS), F32)],
        compiler_params=_params(("parallel",)),
    )(qkv, qkv, qkv, qkv, qkv, sinks)


def _attn_bwd(qkv, o, do, lse, sinks):
    t = qkv.shape[0]
    nb = t // ATTN_BLOCK

    def kern(q_ref, kp_ref, kc_ref, vp_ref, vc_ref, o_ref, do_ref, l_ref, s_ref,
             dq_ref, dk_ref, dv_ref, ds_ref, ck, cv):
        i = pl.program_id(0)

        @pl.when(i == 0)
        def _():
            ds_ref[...] = jnp.zeros_like(ds_ref)
            ck[...] = jnp.zeros_like(ck)
            cv[...] = jnp.zeros_like(cv)

        @pl.when(i < nb)
        def _():
            valid = _attn_mask(i)
            kband = jnp.concatenate([kp_ref[...], kc_ref[...]], axis=0)
            vband = jnp.concatenate([vp_ref[...], vc_ref[...]], axis=0)
            heads = range(N_Q_HEADS)
            sl = [slice(h * HEAD_DIM, (h + 1) * HEAD_DIM) for h in heads]
            ks = [kband[:, kh * HEAD_DIM:(kh + 1) * HEAD_DIM] for kh in range(N_KV_HEADS)]
            vs = [vband[:, kh * HEAD_DIM:(kh + 1) * HEAD_DIM] for kh in range(N_KV_HEADS)]
            qs = [q_ref[:, sl[h]] for h in heads]
            d_os = [do_ref[:, sl[h]] for h in heads]
            dobs = [d.astype(BF16) for d in d_os]
            scores = [_dot(qs[h], ks[h // Q_PER_KV], NT) for h in heads]
            dps = [_dot(dobs[h], vs[h // Q_PER_KV], NT) for h in heads]
            col_head = lax.broadcasted_iota(jnp.int32, (D_ATTN, 128), 0) // HEAD_DIM
            head_ones = (col_head == lax.broadcasted_iota(jnp.int32, (D_ATTN, 128), 1)).astype(BF16)
            deltas = _row_sums(do_ref[...] * o_ref[...], head_ones)
            pbs, dscs = [], []
            for h in heads:
                lse_h = l_ref[:, h:h + 1]
                p = jnp.where(valid, jnp.exp(scores[h] - lse_h), 0.0)
                delta = deltas[:, h:h + 1]
                pbs.append(p.astype(BF16))
                dscs.append((p * (dps[h] - delta)).astype(BF16))
                psink = jnp.exp(s_ref[0:1, h:h + 1] - lse_h)
                ds_ref[0:1, h:h + 1] += -jnp.sum(psink * delta, axis=0, keepdims=True)
            dqs = [_dot(dscs[h], ks[h // Q_PER_KV], NN) for h in heads]
            dkb = [sum(_dot(dscs[h], qs[h], TN) for h in heads if h // Q_PER_KV == kh) for kh in range(N_KV_HEADS)]
            dvb = [sum(_dot(pbs[h], dobs[h], TN) for h in heads if h // Q_PER_KV == kh) for kh in range(N_KV_HEADS)]
            for h in heads:
                dq_ref[:, sl[h]] = dqs[h]
            dk_band = jnp.concatenate(dkb, axis=1)
            dv_band = jnp.concatenate(dvb, axis=1)
            dk_ref[...] = ck[...] + dk_band[:ATTN_BLOCK]
            dv_ref[...] = cv[...] + dv_band[:ATTN_BLOCK]
            ck[...] = dk_band[ATTN_BLOCK:]
            cv[...] = dv_band[ATTN_BLOCK:]

        @pl.when(i == nb)
        def _():
            dk_ref[...] = ck[...]
            dv_ref[...] = cv[...]

    blk = lambda w, f: pl.BlockSpec((ATTN_BLOCK, w), f)
    cur = lambda i: jnp.minimum(i, nb - 1)
    prev = lambda i: jnp.clip(i - 1, 0, nb - 1)
    return pl.pallas_call(
        kern, name="attn_bwd", grid=(nb + 1,),
        in_specs=[blk(512, lambda i: (cur(i), 0)),
                  blk(128, lambda i: (prev(i), 4)), blk(128, lambda i: (cur(i), 4)),
                  blk(128, lambda i: (prev(i), 5)), blk(128, lambda i: (cur(i), 5)),
                  blk(512, lambda i: (cur(i), 0)), blk(512, lambda i: (cur(i), 0)),
                  blk(N_Q_HEADS, lambda i: (cur(i), 0)),
                  pl.BlockSpec((1, N_Q_HEADS), lambda i: (0, 0))],
        out_specs=[blk(512, lambda i: (cur(i), 0)), blk(128, lambda i: (prev(i), 0)),
                   blk(128, lambda i: (prev(i), 0)), pl.BlockSpec((1, N_Q_HEADS), lambda i: (0, 0))],
        out_shape=[jax.ShapeDtypeStruct((t, D_ATTN), F32), jax.ShapeDtypeStruct((t, D_KV), F32),
                   jax.ShapeDtypeStruct((t, D_KV), F32), jax.ShapeDtypeStruct((1, N_Q_HEADS), F32)],
        scratch_shapes=[pltpu.VMEM((ATTN_BLOCK, D_KV), F32), pltpu.VMEM((ATTN_BLOCK, D_KV), F32)],
        compiler_params=_params(("arbitrary",)),
    )(qkv, qkv, qkv, qkv, qkv, o, do, lse, sinks)


_GROUPS = ((0, D_ATTN), (D_ATTN, D_ATTN + D_S5), (D_ATTN + D_S5, D_MODEL))


def _mix_out_ln(ya, ys, yl, mg, w_out, b_out, xres, g, b, layer):
    t = xres.shape[0]
    tm = _row_tile(t, TM)

    def kern(a_ref, s_ref, l_ref, mg_ref, w_ref, bias_ref, x_ref, g_ref, b_ref, m_ref, y_ref, yb_ref, h_ref, r_ref):
        for (lo, hi), ref in zip(_GROUPS, (a_ref, s_ref, l_ref)):
            v = ref[...]
            n = v * lax.rsqrt(jnp.mean(v * v, axis=-1, keepdims=True) + RMS_EPS)
            m_ref[:, lo:hi] = (n * mg_ref[:, lo:hi]).astype(BF16)
        r = ALPHA * x_ref[...] + _dot(m_ref[...], w_ref[...], NN) + bias_ref[...]
        mu = jnp.mean(r, axis=-1, keepdims=True)
        xc = r - mu
        rstd = lax.rsqrt(jnp.mean(xc * xc, axis=-1, keepdims=True) + LN_EPS)
        xhat = xc * rstd
        h_ref[...] = xhat
        r_ref[...] = rstd
        y = xhat * g_ref[...] + b_ref[...]
        y_ref[...] = y
        yb_ref[...] = y.astype(BF16)

    rowb = lambda w: pl.BlockSpec((tm, w), lambda i: (i, 0))
    row = rowb(D_MODEL)
    vec = pl.BlockSpec((1, D_MODEL), lambda i: (0, 0))
    big = lambda dt: jax.ShapeDtypeStruct((t, D_MODEL), dt)
    return pl.pallas_call(
        kern, name="mix_out_ln", grid=(t // tm,),
        in_specs=[rowb(D_ATTN), rowb(D_S5), rowb(D_LRU), vec,
                  pl.BlockSpec((None, D_MODEL, D_MODEL), lambda i: (layer, 0, 0)), vec, row, vec, vec],
        out_specs=[row, row, row, row, pl.BlockSpec((tm, 1), lambda i: (i, 0))],
        out_shape=[big(BF16), big(F32), big(BF16), big(F32), jax.ShapeDtypeStruct((t, 1), F32)],
        compiler_params=_params(("parallel",)),
    )(ya, ys, yl, mg, w_out, b_out, xres, g, b)


def _d_mix_rms(dx1, xhat, rstd, lg, w_out, ya, ys, yl, mg, layer):
    t = dx1.shape[0]
    tm = _row_tile(t, TM)

    def kern(d_ref, h_ref, r_ref, lg_ref, w_ref, a_ref, s_ref, l_ref, g_ref,
             dr_ref, drb_ref, dlg_ref, dlb_ref, sr_ref, da_ref, ds_ref, dl_ref, dg_ref):
        @pl.when(pl.program_id(0) == 0)
        def _():
            for ref in (dlg_ref, dlb_ref, sr_ref, dg_ref):
                ref[...] = jnp.zeros_like(ref)

        _ln_bwd_tile(d_ref[...], h_ref[...], r_ref[...], lg_ref[...], dr_ref, drb_ref, dlg_ref, dlb_ref, sr_ref)
        dmix = _dot(drb_ref[...], w_ref[...], NT)
        for (lo, hi), ref, out in zip(_GROUPS, (a_ref, s_ref, l_ref), (da_ref, ds_ref, dl_ref)):
            v = ref[...]
            rstd = lax.rsqrt(jnp.mean(v * v, axis=-1, keepdims=True) + RMS_EPS)
            n = v * rstd
            dm = dmix[:, lo:hi]
            dg_ref[:, lo:hi] += jnp.sum(dm * n, axis=0, keepdims=True)
            dn = dm * g_ref[:, lo:hi]
            out[...] = rstd * (dn - n * jnp.mean(dn * n, axis=-1, keepdims=True))

    rowb = lambda w: pl.BlockSpec((tm, w), lambda i: (i, 0))
    vec = pl.BlockSpec((1, D_MODEL), lambda i: (0, 0))
    vshape = jax.ShapeDtypeStruct((1, D_MODEL), F32)
    return pl.pallas_call(
        kern, name="d_mix_rms", grid=(t // tm,),
        in_specs=[rowb(D_MODEL), rowb(D_MODEL), pl.BlockSpec((tm, 1), lambda i: (i, 0)), vec,
                  pl.BlockSpec((None, D_MODEL, D_MODEL), lambda i: (layer, 0, 0)),
                  rowb(D_ATTN), rowb(D_S5), rowb(D_LRU), vec],
        out_specs=[rowb(D_MODEL), rowb(D_MODEL), vec, vec, vec, rowb(D_ATTN), rowb(D_S5), rowb(D_LRU), vec],
        out_shape=[jax.ShapeDtypeStruct((t, D_MODEL), F32), jax.ShapeDtypeStruct((t, D_MODEL), BF16),
                   vshape, vshape, vshape, jax.ShapeDtypeStruct((t, D_ATTN), F32),
                   jax.ShapeDtypeStruct((t, D_S5), F32), jax.ShapeDtypeStruct((t, D_LRU), F32), vshape],
        compiler_params=_params(("arbitrary",)),
    )(dx1, xhat, rstd, lg, w_out, ya, ys, yl, mg)


def _matmul_ln(name, a, w, bias, xres, g, b, a_blk, a_map, w_blk, parts, layer):
    t = xres.shape[0]
    tm = a_blk[-2]

    def kern(a_ref, w_ref, bias_ref, x_ref, g_ref, b_ref, y_ref, yb_ref, h_ref, r_ref):
        if parts is None:
            f = _dot(a_ref[...], w_ref[...], NN)
        else:
            f = sum(_dot(a_ref[j], w_ref[j], NN) for j in range(parts))
        r = ALPHA * x_ref[...] + f + bias_ref[...]
        mu = jnp.mean(r, axis=-1, keepdims=True)
        xc = r - mu
        rstd = lax.rsqrt(jnp.mean(xc * xc, axis=-1, keepdims=True) + LN_EPS)
        xhat = xc * rstd
        h_ref[...] = xhat
        r_ref[...] = rstd
        y = xhat * g_ref[...] + b_ref[...]
        y_ref[...] = y
        yb_ref[...] = y.astype(BF16)

    row = pl.BlockSpec((tm, D_MODEL), lambda i: (i, 0))
    vec = pl.BlockSpec((1, D_MODEL), lambda i: (0, 0))
    big = lambda dt: jax.ShapeDtypeStruct((t, D_MODEL), dt)
    return pl.pallas_call(
        kern, name=name, grid=(t // tm,),
        in_specs=[pl.BlockSpec(a_blk, a_map), pl.BlockSpec((None,) + w_blk, lambda i: (layer,) + (0,) * len(w_blk)), vec, row, vec, vec],
        out_specs=[row, row, row, pl.BlockSpec((tm, 1), lambda i: (i, 0))],
        out_shape=[big(F32), big(BF16), big(F32), jax.ShapeDtypeStruct((t, 1), F32)],
        compiler_params=_params(("parallel",)),
    )(a, w, bias, xres, g, b)


def _ln_bwd(dy, xhat, rstd, g):
    t = dy.shape[0]
    tm = _row_tile(t, TM)

    def kern(d_ref, h_ref, r_ref, g_ref, dr_ref, drb_ref, dg_ref, db_ref, sr_ref):
        @pl.when(pl.program_id(0) == 0)
        def _():
            dg_ref[...] = jnp.zeros_like(dg_ref)
            db_ref[...] = jnp.zeros_like(db_ref)
            sr_ref[...] = jnp.zeros_like(sr_ref)

        d = d_ref[...]
        xhat = h_ref[...]
        dg_ref[...] += jnp.sum(d * xhat, axis=0, keepdims=True)
        db_ref[...] += jnp.sum(d, axis=0, keepdims=True)
        dh = d * g_ref[...]
        dr = r_ref[...] * (dh - jnp.mean(dh, axis=-1, keepdims=True)
                           - xhat * jnp.mean(dh * xhat, axis=-1, keepdims=True))
        dr_ref[...] = dr
        drb_ref[...] = dr.astype(BF16)
        sr_ref[...] += jnp.sum(dr, axis=0, keepdims=True)

    row = pl.BlockSpec((tm, D_MODEL), lambda i: (i, 0))
    vec = pl.BlockSpec((1, D_MODEL), lambda i: (0, 0))
    vshape = jax.ShapeDtypeStruct((1, D_MODEL), F32)
    return pl.pallas_call(
        kern, name="ln_bwd", grid=(t // tm,),
        in_specs=[row, row, pl.BlockSpec((tm, 1), lambda i: (i, 0)), vec],
        out_specs=[row, row, vec, vec, vec],
        out_shape=[jax.ShapeDtypeStruct((t, D_MODEL), F32), jax.ShapeDtypeStruct((t, D_MODEL), BF16),
                   vshape, vshape, vshape],
        compiler_params=_params(("arbitrary",)),
    )(dy, xhat, rstd, g)


def _ln_bwd_tile(d, xhat, rstd, g, dr_ref, drb_ref, dg_ref, db_ref, sr_ref):
    dg_ref[...] += jnp.sum(d * xhat, axis=0, keepdims=True)
    db_ref[...] += jnp.sum(d, axis=0, keepdims=True)
    dh = d * g
    dr = rstd * (dh - jnp.mean(dh, axis=-1, keepdims=True) - xhat * jnp.mean(dh * xhat, axis=-1, keepdims=True))
    dr_ref[...] = dr
    drb_ref[...] = dr.astype(BF16)
    sr_ref[...] += jnp.sum(dr, axis=0, keepdims=True)


def _loss_head(y, target, xhat, rstd, g):
    t = y.shape[0]
    tm = _row_tile(t, TM)

    def kern(y_ref, t_ref, h_ref, r_ref, g_ref, l_ref, dr_ref, drb_ref, dg_ref, db_ref, sr_ref):
        @pl.when(pl.program_id(0) == 0)
        def _():
            for ref in (l_ref, dg_ref, db_ref, sr_ref):
                ref[...] = jnp.zeros_like(ref)

        err = y_ref[...] - t_ref[...]
        part = jnp.sum(jnp.sum(err * err, axis=-1, keepdims=True), axis=0, keepdims=True)
        l_ref[...] += jnp.broadcast_to(part * (0.5 / D_MODEL), l_ref.shape)
        _ln_bwd_tile(err * (1.0 / D_MODEL), h_ref[...], r_ref[...], g_ref[...], dr_ref, drb_ref, dg_ref, db_ref, sr_ref)

    row = pl.BlockSpec((tm, D_MODEL), lambda i: (i, 0))
    vec = pl.BlockSpec((1, D_MODEL), lambda i: (0, 0))
    vshape = jax.ShapeDtypeStruct((1, D_MODEL), F32)
    return pl.pallas_call(
        kern, name="loss_head", grid=(t // tm,),
        in_specs=[row, row, row, pl.BlockSpec((tm, 1), lambda i: (i, 0)), vec],
        out_specs=[pl.BlockSpec((1, 128), lambda i: (0, 0)), row, row, vec, vec, vec],
        out_shape=[jax.ShapeDtypeStruct((1, 128), F32), jax.ShapeDtypeStruct((t, D_MODEL), F32),
                   jax.ShapeDtypeStruct((t, D_MODEL), BF16), vshape, vshape, vshape],
        compiler_params=_params(("arbitrary",)),
    )(y, target, xhat, rstd, g)


HALO = 8


def _ffn_hidden_fwd(xb, wg, wu, cw, cb, layer):
    t = xb.shape[0]
    tm = _row_tile(t, TM)

    def kern(x_ref, wg_ref, wu_ref, cw_ref, cb_ref, g_ref, c_ref, u_ref, h_ref, ext):
        @pl.when(pl.program_id(1) == 0)
        def _():
            ext[0:HALO, :] = jnp.zeros((HALO, FF_SH), F32)

        x = x_ref[...]
        gb = _dot(x, wg_ref[...], NN).astype(BF16)
        ub = _dot(x, wu_ref[...], NN).astype(BF16)
        g_ref[...] = gb
        u_ref[...] = ub
        g = gb.astype(F32)
        w = [cw_ref[k:k + 1, :] for k in range(FFN_CONV)]
        body = cb_ref[...] + w[2] * g + w[1] * pltpu.roll(g, 1, 0) + w[0] * pltpu.roll(g, 2, 0)
        ext[HALO:, :] = g[0:HALO, :]
        head = cb_ref[...] + sum(w[k] * ext[pl.ds(HALO - (FFN_CONV - 1) + k, HALO), :] for k in range(FFN_CONV))
        gcb = jnp.concatenate([head, body[HALO:, :]], axis=0).astype(BF16)
        c_ref[...] = gcb
        gc = gcb.astype(F32)
        h_ref[...] = (gc * _sigmoid(gc) * ub.astype(F32)).astype(BF16)
        ext[0:HALO, :] = g[tm - HALO:, :]

    col = pl.BlockSpec((None, tm, FF_SH), lambda j, i: (j, i, 0))
    wspec = pl.BlockSpec((None, None, D_MODEL, FF_SH), lambda j, i: (layer, j, 0, 0))
    big = jax.ShapeDtypeStruct((N_CHIPS, t, FF_SH), BF16)
    return pl.pallas_call(
        kern, name="ffn_hidden_fwd", grid=(N_CHIPS, t // tm),
        in_specs=[pl.BlockSpec((tm, D_MODEL), lambda j, i: (i, 0)), wspec, wspec,
                  pl.BlockSpec((None, FFN_CONV, FF_SH), lambda j, i: (j, 0, 0)),
                  pl.BlockSpec((None, 1, FF_SH), lambda j, i: (j, 0, 0))],
        out_specs=[col, col, col, col], out_shape=[big, big, big, big],
        scratch_shapes=[pltpu.VMEM((2 * HALO, FF_SH), F32)],
        compiler_params=_params(("parallel", "arbitrary")),
    )(xb, wg, wu, cw, cb)


def _ffn_hidden_bwd(drb, gpre, gconv, up, wd, cw, layer):
    t = drb.shape[0]
    tm = _row_tile(t, TM)
    nt = t // tm
    rb = lambda i: nt - 1 - i

    def kern(d_ref, g_ref, c_ref, u_ref, wd_ref, cw_ref, du_ref, dg_ref, dw_ref, db_ref, ext):
        @pl.when(pl.program_id(1) == 0)
        def _():
            dw_ref[...] = jnp.zeros_like(dw_ref)
            db_ref[...] = jnp.zeros_like(db_ref)
            ext[HALO:, :] = jnp.zeros((HALO, FF_SH), F32)

        dh = _dot(d_ref[...], wd_ref[...], NT)
        gc = c_ref[...].astype(F32)
        sg = _sigmoid(gc)
        du_ref[...] = (dh * (gc * sg)).astype(BF16)
        dgc = dh * u_ref[...].astype(F32) * (sg * (1.0 + gc * (1.0 - sg)))
        db_ref[...] += jnp.sum(dgc, axis=0, keepdims=True)
        g = g_ref[...].astype(F32)
        w = [cw_ref[k:k + 1, :] for k in range(FFN_CONV)]
        taps = [pltpu.roll(dgc, tm - 2, 0), pltpu.roll(dgc, tm - 1, 0), dgc]
        body = sum(w[k] * taps[k] for k in range(FFN_CONV))
        last = slice(tm - HALO, tm)
        ext[0:HALO, :] = dgc[last, :]
        tail_taps = [ext[pl.ds(FFN_CONV - 1 - k, HALO), :] for k in range(FFN_CONV)]
        tail = sum(w[k] * tail_taps[k] for k in range(FFN_CONV))
        dg_ref[...] = jnp.concatenate([body[0:tm - HALO, :], tail], axis=0).astype(BF16)
        for k in range(FFN_CONV):
            dw_ref[k:k + 1, :] += (jnp.sum(g * taps[k], axis=0, keepdims=True)
                                   + jnp.sum(g[last, :] * (tail_taps[k] - taps[k][last, :]), axis=0, keepdims=True))
        ext[HALO:, :] = dgc[0:HALO, :]

    col = pl.BlockSpec((None, tm, FF_SH), lambda j, i: (j, rb(i), 0))
    cws = pl.BlockSpec((None, FFN_CONV, FF_SH), lambda j, i: (j, 0, 0))
    cbs = pl.BlockSpec((None, 1, FF_SH), lambda j, i: (j, 0, 0))
    big = jax.ShapeDtypeStruct((N_CHIPS, t, FF_SH), BF16)
    return pl.pallas_call(
        kern, name="ffn_hidden_bwd", grid=(N_CHIPS, nt),
        in_specs=[pl.BlockSpec((tm, D_MODEL), lambda j, i: (rb(i), 0)), col, col, col,
                  pl.BlockSpec((None, None, FF_SH, D_MODEL), lambda j, i: (layer, j, 0, 0)), cws],
        out_specs=[col, col, cws, cbs],
        out_shape=[big, big, jax.ShapeDtypeStruct((N_CHIPS, FFN_CONV, FF_SH), F32),
                   jax.ShapeDtypeStruct((N_CHIPS, 1, FF_SH), F32)],
        scratch_shapes=[pltpu.VMEM((2 * HALO, FF_SH), F32)],
        compiler_params=_params(("parallel", "arbitrary")),
    )(drb, gpre, gconv, up, wd, cw)


def _s5_coefs(ar, ai, reverse):
    if reverse:
        ai = -ai
    pw = [(ar, ai)]
    for _ in range(SUBLANES - 1):
        pr, pi = pw[-1]
        pw.append((pr * ar - pi * ai, pr * ai + pi * ar))
    rows = jnp.arange(SUBLANES)[:, None]
    out = []
    for s in (1, 2, 4):
        keep = (rows + s <= SUBLANES - 1) if reverse else (rows >= s)
        out += [jnp.where(keep, pw[s - 1][0][None], 0.0), jnp.where(keep, pw[s - 1][1][None], 0.0)]
    order = list(range(SUBLANES - 1, -1, -1)) if reverse else list(range(SUBLANES))
    out += [jnp.stack([pw[k][0] for k in order]), jnp.stack([pw[k][1] for k in order])]
    return jnp.stack(out).astype(F32)


def _s5_scan(buf, coef_ref, carry, tm, reverse):
    n8 = tm // SUBLANES

    def body(it, c):
        cre, cim = c
        blk = (n8 - 1 - it) if reverse else it
        r0 = pl.multiple_of(blk * SUBLANES, SUBLANES)
        xre = buf[pl.ds(r0, SUBLANES), 0:N_STATE]
        xim = buf[pl.ds(r0, SUBLANES), N_STATE:]
        for idx, s in enumerate((1, 2, 4)):
            sh = (SUBLANES - s) if reverse else s
            sre = pltpu.roll(xre, sh, 0)
            sim = pltpu.roll(xim, sh, 0)
            are = coef_ref[2 * idx]
            aim = coef_ref[2 * idx + 1]
            xre, xim = xre + are * sre - aim * sim, xim + are * sim + aim * sre
        pre = coef_ref[6]
        pim = coef_ref[7]
        hre = xre + pre * cre - pim * cim
        him = xim + pre * cim + pim * cre
        buf[pl.ds(r0, SUBLANES), 0:N_STATE] = hre
        buf[pl.ds(r0, SUBLANES), N_STATE:] = him
        row = 0 if reverse else SUBLANES - 1
        return (jnp.broadcast_to(hre[row:row + 1], (SUBLANES, N_STATE)),
                jnp.broadcast_to(him[row:row + 1], (SUBLANES, N_STATE)))

    cre, cim = lax.fori_loop(0, n8, body, (carry[:, 0:N_STATE], carry[:, N_STATE:]))
    carry[:, 0:N_STATE] = cre
    carry[:, N_STATE:] = cim


def _real_scan(abuf, bbuf, carry, tm, reverse):
    n8 = tm // SUBLANES
    width = bbuf.shape[1]

    def body(it, c):
        blk = (n8 - 1 - it) if reverse else it
        r0 = pl.multiple_of(blk * SUBLANES, SUBLANES)
        a = abuf[pl.ds(r0, SUBLANES), :]
        b = bbuf[pl.ds(r0, SUBLANES), :]
        rows = lax.broadcasted_iota(jnp.int32, (SUBLANES, width), 0)
        for s in (1, 2, 4):
            sh = (SUBLANES - s) if reverse else s
            keep = (rows + s <= SUBLANES - 1) if reverse else (rows >= s)
            sa = pltpu.roll(a, sh, 0)
            sb = pltpu.roll(b, sh, 0)
            b = b + a * jnp.where(keep, sb, 0.0)
            a = a * jnp.where(keep, sa, 1.0)
        h = b + a * c
        bbuf[pl.ds(r0, SUBLANES), :] = h
        row = 0 if reverse else SUBLANES - 1
        return jnp.broadcast_to(h[row:row + 1], (SUBLANES, width))

    carry[...] = lax.fori_loop(0, n8, body, carry[...])


def _dot(a, b, dims):
    return lax.dot_general(a, b, (dims, ((), ())), preferred_element_type=F32)


TS5 = 512
HALO16 = 16


def _s5_fwd(proj, bmat, coef, cmat, dvec, gw, gb):
    t = proj.shape[0]
    tm = _row_tile(t, TS5)

    def kern(u_ref, b_ref, coef_ref, c_ref, d_ref, gw_ref, gb_ref, h_ref, y_ref, hbuf, carry):
        @pl.when(pl.program_id(0) == 0)
        def _():
            carry[...] = jnp.zeros_like(carry)

        u = u_ref[...]
        hbuf[...] = _dot(u.astype(BF16), b_ref[...], NN)
        _s5_scan(hbuf, coef_ref, carry, tm, False)
        hb = hbuf[...].astype(BF16)
        h_ref[...] = hb
        y = _dot(hb, c_ref[...], NN) + d_ref[...] * u
        ys = _gelu(y)
        z = _dot(ys.astype(BF16), gw_ref[...], NN) + gb_ref[...]
        y_ref[...] = ys * _sigmoid(z)

    full = lambda shp: pl.BlockSpec(shp, lambda i: (0,) * len(shp))
    return pl.pallas_call(
        kern, name="s5_fwd", grid=(t // tm,),
        in_specs=[pl.BlockSpec((tm, D_S5), lambda i: (i, 0)), full((D_S5, 2 * N_STATE)),
                  full((8, SUBLANES, N_STATE)), full((2 * N_STATE, D_S5)), full((1, D_S5)),
                  full((D_S5, D_S5)), full((1, D_S5))],
        out_specs=[pl.BlockSpec((tm, 2 * N_STATE), lambda i: (i, 0)), pl.BlockSpec((tm, D_S5), lambda i: (i, 0))],
        out_shape=[jax.ShapeDtypeStruct((t, 2 * N_STATE), BF16), jax.ShapeDtypeStruct((t, D_S5), F32)],
        scratch_shapes=[pltpu.VMEM((tm, 2 * N_STATE), F32), pltpu.VMEM((SUBLANES, 2 * N_STATE), F32)],
        compiler_params=_params(("arbitrary",)),
    )(proj, bmat, coef, cmat, dvec, gw, gb)


def _s5_bwd(proj, h, dout, bmat, coef_b, cmat, dvec, gw, gb):
    t = proj.shape[0]
    tm = _row_tile(t, TS5)
    nt = t // tm
    rb = lambda i: nt - 1 - i

    def kern(u_ref, h_ref, hp_ref, d_ref, b_ref, coef_ref, c_ref, dv_ref, gw_ref, gb_ref,
             du_ref, dc_ref, db_ref, da_ref, dd_ref, dgw_ref, dgb_ref, gbuf, hext, carry):
        i = pl.program_id(0)

        @pl.when(i == 0)
        def _():
            carry[...] = jnp.zeros_like(carry)
            for r in (dc_ref, db_ref, da_ref, dd_ref, dgw_ref, dgb_ref):
                r[...] = jnp.zeros_like(r)

        u = u_ref[...]
        hb = h_ref[...]
        y = _dot(hb, c_ref[...], NN) + dv_ref[...] * u
        ys = _gelu(y)
        ysb = ys.astype(BF16)
        sg = _sigmoid(_dot(ysb, gw_ref[...], NN) + gb_ref[...])
        d_o = d_ref[...]
        dz = d_o * ys * sg * (1.0 - sg)
        dzb = dz.astype(BF16)
        dys = d_o * sg + _dot(dzb, gw_ref[...], NT)
        dgw_ref[...] += _dot(ysb, dzb, TN)
        dgb_ref[...] += jnp.sum(dz, axis=0, keepdims=True)
        dy = dys * _gelu_grad(y)
        dd_ref[...] += jnp.sum(dy * u, axis=0, keepdims=True)
        dyb = dy.astype(BF16)
        dc_ref[...] += _dot(hb, dyb, TN)
        gbuf[...] = _dot(dyb, c_ref[...], NT)
        _s5_scan(gbuf, coef_ref, carry, tm, True)
        g = gbuf[...]
        first = jnp.where(i < nt - 1, hp_ref[HALO16 - 1:HALO16, :].astype(F32), 0.0)
        hext[SUBLANES - 1:SUBLANES, :] = first
        hext[SUBLANES:, :] = hb.astype(F32)
        hprev = hext[pl.ds(SUBLANES - 1, tm), :]
        gre, gim = g[:, 0:N_STATE], g[:, N_STATE:]
        pre, pim = hprev[:, 0:N_STATE], hprev[:, N_STATE:]
        da_ref[0:1, :] += jnp.sum(gre * pre + gim * pim, axis=0, keepdims=True)
        da_ref[1:2, :] += jnp.sum(gim * pre - gre * pim, axis=0, keepdims=True)
        gb16 = g.astype(BF16)
        db_ref[...] += _dot(u.astype(BF16), gb16, TN)
        du_ref[...] = dy * dv_ref[...] + _dot(gb16, b_ref[...], NT)

    full = lambda shp: pl.BlockSpec(shp, lambda i: (0,) * len(shp))
    shape = lambda shp: jax.ShapeDtypeStruct(shp, F32)
    return pl.pallas_call(
        kern, name="s5_bwd", grid=(nt,),
        in_specs=[pl.BlockSpec((tm, D_S5), lambda i: (rb(i), 0)),
                  pl.BlockSpec((tm, 2 * N_STATE), lambda i: (rb(i), 0)),
                  pl.BlockSpec((HALO16, 2 * N_STATE), lambda i: (jnp.maximum(rb(i) * (tm // HALO16) - 1, 0), 0)),
                  pl.BlockSpec((tm, D_S5), lambda i: (rb(i), 0)),
                  full((D_S5, 2 * N_STATE)), full((8, SUBLANES, N_STATE)), full((2 * N_STATE, D_S5)),
                  full((1, D_S5)), full((D_S5, D_S5)), full((1, D_S5))],
        out_specs=[pl.BlockSpec((tm, D_S5), lambda i: (rb(i), 0)), full((2 * N_STATE, D_S5)),
                   full((D_S5, 2 * N_STATE)), full((2, N_STATE)), full((1, D_S5)), full((D_S5, D_S5)),
                   full((1, D_S5))],
        out_shape=[shape((t, D_S5)), shape((2 * N_STATE, D_S5)), shape((D_S5, 2 * N_STATE)),
                   shape((2, N_STATE)), shape((1, D_S5)), shape((D_S5, D_S5)), shape((1, D_S5))],
        scratch_shapes=[pltpu.VMEM((tm, 2 * N_STATE), F32), pltpu.VMEM((tm + SUBLANES, 2 * N_STATE), F32),
                        pltpu.VMEM((SUBLANES, 2 * N_STATE), F32)],
        compiler_params=_params(("arbitrary",)),
    )(proj, h, h, dout, bmat, coef_b, cmat, dvec, gw, gb)


def _lru_gates(ext, x_ref, p_ref, cw_ref, cb_ref, wx_ref, bx_ref, wa_ref, ba_ref, ap_ref, first_tile, row0, tm):
    ext[0:HALO, :] = jnp.where(first_tile, 0.0, p_ref[...])
    ext[HALO:, :] = x_ref[...]
    taps = [ext[pl.ds(HALO - (LRU_CONV - 1) + k, tm), :] for k in range(LRU_CONV)]
    xc = cb_ref[...] + sum(cw_ref[k:k + 1, :] * taps[k] for k in range(LRU_CONV))
    xcb = xc.astype(BF16)
    gx = _sigmoid(_dot(xcb, wx_ref[...], NN) + bx_ref[...])
    ga = _sigmoid(_dot(xcb, wa_ref[...], NN) + ba_ref[...])
    z = -ap_ref[...]
    sp = jnp.maximum(z, 0.0) + jnp.log(1.0 + jnp.exp(-jnp.abs(z)))
    log_a = -LRU_C * ga * sp
    a = jnp.exp(log_a)
    tok = row0 + lax.broadcasted_iota(jnp.int32, a.shape, 0)
    is0 = tok == 0
    mult = jnp.where(is0, 1.0, jnp.sqrt(1.0 - jnp.exp(2.0 * log_a)))
    return taps, xc, xcb, gx, ga, sp, a, mult, is0


def _lru_specs(tm, blk_of):
    col = lambda cidx: pl.BlockSpec((tm, D_LRU), lambda i: (blk_of(i), cidx))
    prev = lambda cidx: pl.BlockSpec((HALO, D_LRU), lambda i: (jnp.maximum(blk_of(i) * (tm // HALO) - 1, 0), cidx))
    full = lambda shp: pl.BlockSpec(shp, lambda i: (0,) * len(shp))
    wts = [full((LRU_CONV, D_LRU)), full((1, D_LRU)), full((D_LRU, D_LRU)), full((1, D_LRU)),
           full((D_LRU, D_LRU)), full((1, D_LRU)), full((1, D_LRU))]
    return col, prev, full, wts


def _lru_fwd(proj, cw, cb, wx, bx, wa, ba, ap):
    t = proj.shape[0]
    tm = _row_tile(t, TM)

    def kern(x_ref, p_ref, g_ref, cw_ref, cb_ref, wx_ref, bx_ref, wa_ref, ba_ref, ap_ref,
             y_ref, h_ref, ext, abuf, carry):
        i = pl.program_id(0)

        @pl.when(i == 0)
        def _():
            carry[...] = jnp.zeros_like(carry)

        _, xc, _, gx, _, _, a, mult, _ = _lru_gates(ext, x_ref, p_ref, cw_ref, cb_ref, wx_ref, bx_ref, wa_ref,
                                                    ba_ref, ap_ref, i == 0, i * tm, tm)
        abuf[...] = a
        h_ref[...] = mult * gx * xc
        _real_scan(abuf, h_ref, carry, tm, False)
        y_ref[...] = h_ref[...] * _gelu(g_ref[...])

    col, prev, full, wts = _lru_specs(tm, lambda i: i)
    out = pl.BlockSpec((tm, D_LRU), lambda i: (i, 0))
    return pl.pallas_call(
        kern, name="lru_fwd", grid=(t // tm,),
        in_specs=[col(1), prev(1), col(2)] + wts, out_specs=[out, out],
        out_shape=[jax.ShapeDtypeStruct((t, D_LRU), F32), jax.ShapeDtypeStruct((t, D_LRU), F32)],
        scratch_shapes=[pltpu.VMEM((tm + HALO, D_LRU), F32), pltpu.VMEM((tm, D_LRU), F32),
                        pltpu.VMEM((SUBLANES, D_LRU), F32)],
        compiler_params=_params(("arbitrary",)),
    )(proj, proj, proj, cw, cb, wx, bx, wa, ba, ap)


def _lru_bwd(proj, h, dout, cw, cb, wx, bx, wa, ba, ap):
    t = proj.shape[0]
    tm = _row_tile(t, TM)
    nt = t // tm
    rb = lambda i: nt - 1 - i

    def kern(x_ref, p_ref, g_ref, h_ref, hp_ref, d_ref, cw_ref, cb_ref, wx_ref, bx_ref, wa_ref, ba_ref, ap_ref,
             dxc_ref, dg_ref, dcw_ref, dcb_ref, dwx_ref, dbx_ref, dwa_ref, dba_ref, dap_ref,
             ext, aext, abuf, gbuf, carry, acarry):
        i = pl.program_id(0)
        blk = nt - 1 - i

        @pl.when(i == 0)
        def _():
            carry[...] = jnp.zeros_like(carry)
            acarry[...] = jnp.zeros_like(acarry)
            for r in (dcw_ref, dcb_ref, dwx_ref, dbx_ref, dwa_ref, dba_ref, dap_ref):
                r[...] = jnp.zeros_like(r)

        taps, xc, xcb, gx, ga, sp, a, mult, is0 = _lru_gates(
            ext, x_ref, p_ref, cw_ref, cb_ref, wx_ref, bx_ref, wa_ref, ba_ref, ap_ref, blk == 0, blk * tm, tm)
        gate = g_ref[...]
        d_o = d_ref[...]
        hcur = h_ref[...]
        dg_ref[...] = d_o * hcur * _gelu_grad(gate)
        aext[0:tm, :] = a
        aext[tm:, :] = acarry[...]
        abuf[...] = aext[pl.ds(1, tm), :]
        gbuf[...] = d_o * _gelu(gate)
        _real_scan(abuf, gbuf, carry, tm, True)
        acarry[...] = jnp.broadcast_to(a[0:1], acarry.shape)
        g = gbuf[...]
        ext[0:HALO, :] = jnp.where(blk == 0, 0.0, hp_ref[...])
        ext[HALO:, :] = hcur
        hprev = ext[pl.ds(HALO - 1, tm), :]
        dmult = jnp.where(is0, 0.0, g * gx * xc)
        dgx = g * mult * xc
        dxc = g * mult * gx
        dlog_a = g * hprev * a - dmult * (a * a) / mult
        dga = dlog_a * (-LRU_C * sp)
        dsp = jnp.sum(dlog_a * (-LRU_C * ga), axis=0, keepdims=True)
        dap_ref[...] += dsp * (-_sigmoid(-ap_ref[...]))
        dpa = (dga * ga * (1.0 - ga))
        dpx = (dgx * gx * (1.0 - gx))
        dpab, dpxb = dpa.astype(BF16), dpx.astype(BF16)
        dwx_ref[...] += _dot(xcb, dpxb, TN)
        dwa_ref[...] += _dot(xcb, dpab, TN)
        dbx_ref[...] += jnp.sum(dpx, axis=0, keepdims=True)
        dba_ref[...] += jnp.sum(dpa, axis=0, keepdims=True)
        dxc = dxc + _dot(dpxb, wx_ref[...], NT) + _dot(dpab, wa_ref[...], NT)
        dxc_ref[...] = dxc
        dcb_ref[...] += jnp.sum(dxc, axis=0, keepdims=True)
        for k in range(LRU_CONV):
            dcw_ref[k:k + 1, :] += jnp.sum(dxc * taps[k], axis=0, keepdims=True)

    col, prev, full, wts = _lru_specs(tm, rb)
    row = pl.BlockSpec((tm, D_LRU), lambda i: (rb(i), 0))
    hprev_spec = pl.BlockSpec((HALO, D_LRU), lambda i: (jnp.maximum(rb(i) * (tm // HALO) - 1, 0), 0))
    shape = lambda shp: jax.ShapeDtypeStruct(shp, F32)
    vec = (1, D_LRU)
    sq = (D_LRU, D_LRU)
    return pl.pallas_call(
        kern, name="lru_bwd", grid=(nt,),
        in_specs=[col(1), prev(1), col(2), row, hprev_spec, row] + wts,
        out_specs=[row, row, full((LRU_CONV, D_LRU)), full(vec), full(sq), full(vec), full(sq), full(vec), full(vec)],
        out_shape=[shape((t, D_LRU)), shape((t, D_LRU)), shape((LRU_CONV, D_LRU)), shape(vec), shape(sq),
                   shape(vec), shape(sq), shape(vec), shape(vec)],
        scratch_shapes=[pltpu.VMEM((tm + HALO, D_LRU), F32), pltpu.VMEM((tm + HALO, D_LRU), F32),
                        pltpu.VMEM((tm, D_LRU), F32), pltpu.VMEM((tm, D_LRU), F32),
                        pltpu.VMEM((SUBLANES, D_LRU), F32), pltpu.VMEM((SUBLANES, D_LRU), F32)],
        compiler_params=_params(("arbitrary",)),
    )(proj, proj, proj, h, h, dout, cw, cb, wx, bx, wa, ba, ap)


def _assemble_dproj(dq, dk, dv, du, dxc, dgate, cos, sin_s, cw):
    t = dq.shape[0]
    tm = _row_tile(t, TM)
    nt = t // tm

    def kern(dq_ref, dk_ref, dv_ref, du_ref, dx_ref, dn_ref, dg_ref, c_ref, s_ref, cw_ref, o_ref, b_ref, ext):
        i = pl.program_id(0)

        @pl.when(i == 0)
        def _():
            b_ref[...] = jnp.zeros_like(b_ref)

        def put(lo, val):
            hi = lo + val.shape[1]
            o_ref[:, lo:hi] = val.astype(BF16)
            b_ref[:, lo:hi] += jnp.sum(val, axis=0, keepdims=True)

        c = c_ref[...]
        s = s_ref[...]
        for ch in range(4):
            x = dq_ref[:, ch * 128:(ch + 1) * 128] * (HEAD_DIM ** -0.5)
            put(ch * 128, x * c - _rope_swap(x) * s)
        x = dk_ref[...]
        put(512, x * c - _rope_swap(x) * s)
        put(640, dv_ref[...])
        put(768, du_ref[...])
        ext[0:tm, :] = dx_ref[...]
        ext[tm:, :] = jnp.where(i < nt - 1, dn_ref[...], 0.0)
        put(1024, sum(cw_ref[k:k + 1, :] * ext[pl.ds(LRU_CONV - 1 - k, tm), :] for k in range(LRU_CONV)))
        put(1280, dg_ref[...])

    row = lambda w: pl.BlockSpec((tm, w), lambda i: (i, 0))
    nxt = pl.BlockSpec((HALO, D_LRU), lambda i: (jnp.minimum((i + 1) * (tm // HALO), t // HALO - 1), 0))
    return pl.pallas_call(
        kern, name="assemble_dproj", grid=(nt,),
        in_specs=[row(512), row(128), row(128), row(256), row(256), nxt, row(256), row(128), row(128),
                  pl.BlockSpec((LRU_CONV, D_LRU), lambda i: (0, 0))],
        out_specs=[row(D_IN), pl.BlockSpec((1, D_IN), lambda i: (0, 0))],
        out_shape=[jax.ShapeDtypeStruct((t, D_IN), BF16), jax.ShapeDtypeStruct((1, D_IN), F32)],
        scratch_shapes=[pltpu.VMEM((tm + HALO, D_LRU), F32)],
        compiler_params=_params(("arbitrary",)),
    )(dq, dk, dv, du, dxc, dxc, dgate, cos, sin_s, cw)


def _blockdiag_s5(bbar_re, bbar_im, c_re, c_im):
    eye = jnp.eye(S5_GROUPS, dtype=F32)
    b_of = lambda m: jnp.einsum('gpc,gh->gchp', m, eye).reshape(D_S5, N_STATE)
    c_of = lambda m: jnp.einsum('gcp,gh->gphc', m, eye).reshape(N_STATE, D_S5)
    bmat = jnp.concatenate([b_of(bbar_re), b_of(bbar_im)], axis=1)
    cmat = jnp.concatenate([c_of(c_re), -c_of(c_im)], axis=0)
    return bmat, cmat


def _s5_prepare(a_re, a_im, b_re, b_im, c_re, c_im, log_dt):
    lam_re = jnp.minimum(a_re, -1e-4)
    lam_im = a_im
    dt = jnp.exp(log_dt)[:, None]
    decay = jnp.exp(dt * lam_re)
    ang = dt * lam_im
    abar_re = decay * jnp.cos(ang)
    abar_im = decay * jnp.sin(ang)
    den = jnp.square(lam_re) + jnp.square(lam_im)
    nr = abar_re - 1.0
    ni = abar_im
    coef_re = (nr * lam_re + ni * lam_im) / den
    coef_im = (ni * lam_re - nr * lam_im) / den
    bbar_re = coef_re[..., None] * b_re - coef_im[..., None] * b_im
    bbar_im = coef_re[..., None] * b_im + coef_im[..., None] * b_re
    bmat, cmat = _blockdiag_s5(bbar_re, bbar_im, c_re, c_im)
    return abar_re.reshape(N_STATE), abar_im.reshape(N_STATE), bmat, cmat


def _blockdiag_lru(w):
    eye = jnp.eye(LRU_HEADS, dtype=F32)
    return jnp.einsum('hij,hk->hikj', w, eye).reshape(D_LRU, D_LRU)


def _rope_tables(t):
    inv_freq = ROPE_THETA ** (-jnp.arange(0, HEAD_DIM, 2, dtype=F32) / HEAD_DIM)
    ang = jnp.arange(t, dtype=F32)[:, None] * inv_freq[None, :]
    cos, sin = jnp.cos(ang), jnp.sin(ang)
    return jnp.tile(jnp.concatenate([cos, cos], axis=1), (1, 2)), jnp.tile(jnp.concatenate([-sin, sin], axis=1), (1, 2))


def _vec(v):
    return v.reshape(1, -1)


def _layer_weights(p):
    abar_re, abar_im, bmat, cmat = _s5_prepare(p['s5_a_re'], p['s5_a_im'], p['s5_b_re'], p['s5_b_im'],
                                               p['s5_c_re'], p['s5_c_im'], p['s5_log_dt'])
    return dict(
        coef_f=_s5_coefs(abar_re, abar_im, False), coef_b=_s5_coefs(abar_re, abar_im, True),
        bmat=bmat.astype(BF16), cmat=cmat.astype(BF16),
        wx=_blockdiag_lru(p['lru_wx']).astype(BF16), wa=_blockdiag_lru(p['lru_wa']).astype(BF16),
        gw=p['s5_glu_w'].astype(BF16))


def _layer_fwd(x, xb, p, w, cos, sin_s):
    t = x.shape[0]
    tm = _row_tile(t, TM)
    layer = p['layer']
    qkv, uxg = _in_proj(xb, p['w_in'], _vec(p['b_in']), cos, sin_s, layer)
    ya, lse = _attn_fwd(qkv, _vec(p['attn_sinks']))
    h5, ys = _s5_fwd(uxg, w['bmat'], w['coef_f'], w['cmat'], _vec(p['s5_d']), w['gw'], _vec(p['s5_glu_b']))
    lru_w = (p['lru_conv_w'], _vec(p['lru_conv_b']), w['wx'], _vec(p['lru_bx']), w['wa'], _vec(p['lru_ba']),
             _vec(p['lru_a_param']))
    yl, hl = _lru_fwd(uxg, *lru_w)
    mix, x1, x1b, xhat1, rstd1 = _mix_out_ln(ya, ys, yl, _vec(p['mix_norm_g']), p['w_out'], _vec(p['b_out']), x,
                                             _vec(p['ln1_g']), _vec(p['ln1_b']), layer)
    gpre, gconv, up, hmid = _ffn_hidden_fwd(x1b, p['ffn_w_gate'], p['ffn_w_up'], p['ffn_conv_w'], p['ffn_conv_b'],
                                            layer)
    x2, x2b, xhat2, rstd2 = _matmul_ln(
        "ffn_down_ln", hmid, p['ffn_w_down'], jnp.zeros((1, D_MODEL), F32), x1, _vec(p['ln2_g']), _vec(p['ln2_b']),
        a_blk=(N_CHIPS, tm, FF_SH), a_map=lambda i: (0, i, 0), w_blk=(N_CHIPS, FF_SH, D_MODEL), parts=N_CHIPS,
        layer=layer)
    saved = dict(xb=xb, uxg=uxg, qkv=qkv, ya=ya, lse=lse, h5=h5, ys=ys, yl=yl, hl=hl, mix=mix, x1b=x1b, xhat1=xhat1,
                 rstd1=rstd1, gpre=gpre, gconv=gconv, up=up, hmid=hmid, xhat2=xhat2, rstd2=rstd2, lru_w=lru_w)
    return x2, x2b, saved


def _layer_bwd(dr2, dr2b, s, p, w, cos, sin_s, big, below):
    t = dr2.shape[0]
    tk = _row_tile(t, TMM)
    nk = t // tk
    tm = _row_tile(t, TM)
    layer = p['layer']
    big = dict(big)
    g = {}
    dup, dgpre, g['ffn_conv_w'], g['ffn_conv_b'] = _ffn_hidden_bwd(
        dr2b, s['gpre'], s['gconv'], s['up'], p['ffn_w_down'], p['ffn_conv_w'], layer)
    big['ffn_w_down'] = _matmul(
        "d_w_down", s['hmid'], dr2b, a_blk=(None, tk, FF_SH), a_map=lambda i, j, k: (i, k, 0), b_blk=(tk, D_MODEL),
        b_map=lambda i, j, k: (k, 0), out_shape=(DEPTH, N_CHIPS, FF_SH, D_MODEL), o_blk=(None, None, FF_SH, D_MODEL),
        o_map=lambda i, j: (layer, i, 0, 0), grid=(N_CHIPS, 1, nk), dims=TN, into=big['ffn_w_down'])
    d_ffn_w = lambda name, dact, buf: _matmul(
        name, s['x1b'], dact, a_blk=(tk, D_MODEL), a_map=lambda i, j, k: (k, 0), b_blk=(None, tk, FF_SH),
        b_map=lambda i, j, k: (j, k, 0), out_shape=(DEPTH, N_CHIPS, D_MODEL, FF_SH),
        o_blk=(None, None, D_MODEL, FF_SH), o_map=lambda i, j: (layer, j, 0, 0), grid=(1, N_CHIPS, nk), dims=TN,
        into=buf)
    big['ffn_w_gate'] = d_ffn_w("d_w_gate", dgpre, big['ffn_w_gate'])
    big['ffn_w_up'] = d_ffn_w("d_w_up", dup, big['ffn_w_up'])
    wspec = dict(b_blk=(None, None, D_MODEL, FF_SH), b_map=lambda i, j, k: (layer, k, 0, 0))
    dx1 = _matmul(
        "d_x1", dgpre, p['ffn_w_gate'], pair2=(dup, p['ffn_w_up']), a_blk=(None, tk, FF_SH),
        a_map=lambda i, j, k: (k, i, 0), out_shape=(t, D_MODEL), o_blk=(tk, D_MODEL), o_map=lambda i, j: (i, 0),
        grid=(nk, 1, N_CHIPS), dims=NT, add=dr2, add_scale=ALPHA, **wspec)
    dr1, dr1b, g['ln1_g'], g['ln1_b'], g['b_out'], dya, dys, dyl, g['mix_norm_g'] = _d_mix_rms(
        dx1, s['xhat1'], s['rstd1'], _vec(p['ln1_g']), p['w_out'], s['ya'], s['ys'], s['yl'],
        _vec(p['mix_norm_g']), layer)
    big['w_out'] = _matmul(
        "d_w_out", s['mix'], dr1b, a_blk=(tk, D_MODEL), a_map=lambda i, j, k: (k, 0), b_blk=(tk, D_MODEL),
        b_map=lambda i, j, k: (k, 0), out_shape=(DEPTH, D_MODEL, D_MODEL), o_blk=(None, D_MODEL, D_MODEL),
        o_map=lambda i, j: (layer, 0, 0), grid=(1, 1, nk), dims=TN, into=big['w_out'])
    dq, dk, dv, g['attn_sinks'] = _attn_bwd(s['qkv'], s['ya'], dya, s['lse'], _vec(p['attn_sinks']))
    du, dcmat, dbmat, dabar, g['s5_d'], g['s5_glu_w'], g['s5_glu_b'] = _s5_bwd(
        s['uxg'], s['h5'], dys, w['bmat'], w['coef_b'], w['cmat'], _vec(p['s5_d']), w['gw'], _vec(p['s5_glu_b']))
    (dxc, dgate, g['lru_conv_w'], g['lru_conv_b'], dwx, g['lru_bx'], dwa, g['lru_ba'],
     g['lru_a_param']) = _lru_bwd(s['uxg'], s['hl'], dyl, *s['lru_w'])
    dproj, g['b_in'] = _assemble_dproj(dq, dk, dv, du, dxc, dgate, cos, sin_s, p['lru_conv_w'])
    big['w_in'] = _matmul(
        "d_w_in", s['xb'], dproj, a_blk=(tk, D_MODEL), a_map=lambda i, j, k: (k, 0), b_blk=(tk, IN_SH),
        b_map=lambda i, j, k: (k, j), out_shape=(DEPTH, N_CHIPS, D_MODEL, IN_SH), o_blk=(None, None, D_MODEL, IN_SH),
        o_map=lambda i, j: (layer, j, 0, 0), grid=(1, N_CHIPS, nk), dims=TN, into=big['w_in'])
    dx = _matmul("d_x", dproj, p['w_in'], a_blk=(tk, IN_SH), a_map=lambda i, j, k: (i, k),
                 b_blk=(None, None, D_MODEL, IN_SH), b_map=lambda i, j, k: (layer, k, 0, 0), out_shape=(t, D_MODEL),
                 o_blk=(tk, D_MODEL), o_map=lambda i, j: (i, 0), grid=(nk, 1, N_CHIPS), dims=NT,
                 add=dr1, add_scale=ALPHA)
    if below is not None:
        dx = _ln_bwd(dx, below[0]['xhat2'], below[0]['rstd2'], _vec(below[1]['ln2_g']))
    return dx, _param_chain(g, p, dabar, dbmat, dcmat, dwx, dwa), big


def _param_chain(g, p, dabar, dbmat, dcmat, dwx, dwa):
    s5_names = ('s5_a_re', 's5_a_im', 's5_b_re', 's5_b_im', 's5_c_re', 's5_c_im', 's5_log_dt')
    _, s5_vjp = jax.vjp(_s5_prepare, *[p[n] for n in s5_names])
    for n, val in zip(s5_names, s5_vjp((dabar[0], dabar[1], dbmat, dcmat))):
        g[n] = val
    g['lru_wx'] = jax.vjp(_blockdiag_lru, p['lru_wx'])[1](dwx)[0]
    g['lru_wa'] = jax.vjp(_blockdiag_lru, p['lru_wa'])[1](dwa)[0]
    return g


ROW_TILE = 512


def _pick_rows(rows):
    for rt in range(min(rows, ROW_TILE), 0, -1):
        if rows % rt == 0 and (rt % 16 == 0 or rt == rows):
            return rt
    return rows


def _cast_bf16(a):
    a2 = a.reshape(-1, a.shape[-1])
    rows, c = a2.shape
    rt = _pick_rows(rows)

    def kern(a_ref, o_ref):
        o_ref[...] = a_ref[...].astype(BF16)

    spec = pl.BlockSpec((rt, c), lambda i: (i, 0))
    out = pl.pallas_call(kern, name="cast_bf16", grid=(rows // rt,), in_specs=[spec], out_specs=spec,
                         out_shape=jax.ShapeDtypeStruct((rows, c), BF16), compiler_params=_params(("parallel",)))(a2)
    return out.reshape(a.shape)


def _sum_parts(name, parts, shape):
    c = shape[-1]
    rows = math.prod(shape[:-1])
    rt = _pick_rows(rows)
    n = len(parts)

    def kern(*refs):
        acc = refs[0][...].astype(F32)
        for r in refs[1:n]:
            acc = acc + r[...].astype(F32)
        refs[n][...] = acc

    specs, args = [], []
    for arr, j in parts:
        if j is None:
            specs.append(pl.BlockSpec((rt, c), lambda i: (i, 0)))
            args.append(arr.reshape(rows, c))
        else:
            specs.append(pl.BlockSpec((None, rt, c), functools.partial(lambda i, jj: (jj, i, 0), jj=j)))
            args.append(arr.reshape(arr.shape[0], rows, c))
    out = pl.pallas_call(kern, name=name, grid=(rows // rt,), in_specs=specs,
                         out_specs=pl.BlockSpec((rt, c), lambda i: (i, 0)),
                         out_shape=jax.ShapeDtypeStruct((rows, c), F32), compiler_params=_params(("parallel",)))(*args)
    return out.reshape(shape)


def _adamw(name, w, g, m, v):
    shape = w.shape
    c = shape[-1]
    rows = math.prod(shape[:-1])
    rt = _pick_rows(rows)

    def kern(w_ref, g_ref, m_ref, v_ref, d_ref, nm_ref, nv_ref):
        g_ = g_ref[...]
        m_ = ADAM_B1 * m_ref[...] + (1.0 - ADAM_B1) * g_
        v_ = ADAM_B2 * v_ref[...] + (1.0 - ADAM_B2) * jnp.square(g_)
        m_hat = m_ / (1.0 - ADAM_B1 ** ADAM_STEP)
        v_hat = v_ / (1.0 - ADAM_B2 ** ADAM_STEP)
        d_ref[...] = -ADAM_LR * (m_hat / (jnp.sqrt(v_hat) + ADAM_EPS) + ADAM_WD * w_ref[...])
        nm_ref[...] = m_
        nv_ref[...] = v_

    spec = pl.BlockSpec((rt, c), lambda i: (i, 0))
    outs = pl.pallas_call(kern, name=name, grid=(rows // rt,), in_specs=[spec] * 4, out_specs=[spec] * 3,
                          out_shape=[jax.ShapeDtypeStruct((rows, c), F32)] * 3,
                          compiler_params=_params(("parallel",)))(*[a.reshape(rows, c) for a in (w, g, m, v)])
    return tuple(o.reshape(shape) for o in outs)


def _position():
    return lax.axis_index("x"), lax.axis_index("y"), lax.axis_index("c")


def _other_chips(x, y):
    return [(1 - x, y), (x, 1 - y), (1 - x, 1 - y)]


def _exchange(name, arrs, out_shapes, n_local, n_remote, plan):
    n_in, n_out = len(arrs), len(out_shapes)

    def kern(*refs):
        ins, outs = refs[:n_in], refs[n_in:n_in + n_out]
        send, recv, loc = refs[n_in + n_out:]
        local, remote = plan(ins, outs, *_position())
        assert len(local) == n_local and len(remote) == n_remote
        own = [pltpu.make_async_copy(s, d, loc.at[k]) for k, (s, d) in enumerate(local)]
        for cp in own:
            cp.start()
        sent = [pltpu.make_async_remote_copy(src_ref=s, dst_ref=d, send_sem=send.at[k], recv_sem=recv.at[k],
                                             device_id=peer, device_id_type=MESH)
                for k, (s, d, peer, _) in enumerate(remote)]
        for cp in sent:
            cp.start()
        for k, (s, _, peer, landing) in enumerate(remote):
            pltpu.make_async_remote_copy(src_ref=s, dst_ref=landing, send_sem=send.at[k], recv_sem=recv.at[k],
                                         device_id=peer, device_id_type=MESH).wait_recv()
        for cp in sent:
            cp.wait_send()
        for cp in own:
            cp.wait()

    return pl.pallas_call(
        kern, name=name, in_specs=[ANY] * n_in, out_specs=[ANY] * n_out, out_shape=out_shapes,
        scratch_shapes=[pltpu.SemaphoreType.DMA((n_remote,)), pltpu.SemaphoreType.DMA((n_remote,)),
                        pltpu.SemaphoreType.DMA((max(n_local, 1),))],
    )(*arrs)


def _allgather_chips(arrs, halved=()):
    n = len(arrs)
    layers = arrs[0].shape[0]

    def plan(ins, outs, x, y, c):
        me = 2 * x + y
        local, remote = [], []
        for t in range(n):
            for l in range(layers):
                src = ins[t].at[l]
                if t in halved:
                    r2 = ins[t].shape[2] // 2
                    src = ins[t].at[l, :, pl.ds(c * r2, r2)]
                local.append((src, outs[t].at[l, pl.ds(me, 1)]))
                for px, py in _other_chips(x, y):
                    remote.append((src, outs[t].at[l, pl.ds(me, 1)], (px, py, c),
                                   outs[t].at[l, pl.ds(2 * px + py, 1)]))
        return local, remote

    outs = []
    for t, a in enumerate(arrs):
        tail = (a.shape[2] // 2,) + a.shape[3:] if t in halved else a.shape[2:]
        outs.append(jax.ShapeDtypeStruct((a.shape[0], N_CHIPS) + tail, a.dtype))
    return _exchange("allgather_chips", arrs, outs, n * layers, 3 * n * layers, plan)


def _chip_scatter(arrs):
    n = len(arrs)
    layers = arrs[0].shape[0]

    def plan(ins, outs, x, y, c):
        me = 2 * x + y
        local, remote = [], []
        for t in range(n):
            for l in range(layers):
                local.append((ins[t].at[l, pl.ds(me, 1)], outs[2 * t].at[l]))
                for j, (px, py) in enumerate(_other_chips(x, y)):
                    remote.append((ins[t].at[l, pl.ds(2 * px + py, 1)], outs[2 * t + 1].at[j, l], (px, py, c),
                                   outs[2 * t + 1].at[j, l]))
        return local, remote

    outs = []
    for a in arrs:
        one = (a.shape[0], 1) + a.shape[2:]
        outs += [jax.ShapeDtypeStruct(one, a.dtype), jax.ShapeDtypeStruct((3,) + one, a.dtype)]
    return _exchange("chip_scatter", arrs, outs, n * layers, 3 * n * layers, plan)


def _allgather_devices(v):
    def kern(v_ref, o_ref, send, recv, loc):
        x, y, c = _position()
        me, sibling = (x, y, c), (x, y, 1 - c)
        chips = _other_chips(x, y)

        def rows(px, py, pc):
            return o_ref.at[pl.ds(4 * px + 2 * py + pc, 1)]

        def copy(k, block, to, src=None):
            return pltpu.make_async_remote_copy(
                src_ref=rows(*block) if src is None else src, dst_ref=rows(*block), send_sem=send.at[k],
                recv_sem=recv.at[k], device_id=to, device_id_type=MESH)

        mine = pltpu.make_async_copy(v_ref, rows(*me), loc.at[0])
        mine.start()
        first = [copy(0, me, sibling, src=v_ref)]
        first += [copy(1 + j, me, (*chip, c), src=v_ref) for j, chip in enumerate(chips)]
        for cp in first:
            cp.start()
        passed = [copy(4 + j, (*chip, c), sibling) for j, chip in enumerate(chips)]
        for j, chip in enumerate(chips):
            copy(1 + j, (*chip, c), me).wait_recv()
            passed[j].start()
        copy(0, sibling, me).wait_recv()
        for j, chip in enumerate(chips):
            copy(4 + j, (*chip, 1 - c), me).wait_recv()
        for cp in first + passed:
            cp.wait_send()
        mine.wait()

    vmem = pl.BlockSpec(memory_space=pltpu.VMEM)
    return pl.pallas_call(
        kern, name="allgather_devices", in_specs=[vmem], out_specs=vmem,
        out_shape=jax.ShapeDtypeStruct((N_DEV,) + v.shape[1:], v.dtype),
        scratch_shapes=[pltpu.SemaphoreType.DMA((7,)), pltpu.SemaphoreType.DMA((7,)), pltpu.SemaphoreType.DMA((1,))],
        compiler_params=pltpu.CompilerParams(vmem_limit_bytes=VMEM_MB << 20),
    )(v)


WEIGHTS = ['w_in', 'b_in', 'attn_sinks', 's5_a_re', 's5_a_im', 's5_b_re', 's5_b_im', 's5_c_re', 's5_c_im', 's5_d',
           's5_log_dt', 's5_glu_w', 's5_glu_b', 'lru_conv_w', 'lru_conv_b', 'lru_wx', 'lru_bx', 'lru_wa', 'lru_ba',
           'lru_a_param', 'mix_norm_g', 'w_out', 'b_out', 'ln1_g', 'ln1_b', 'ffn_w_gate', 'ffn_w_up', 'ffn_conv_w',
           'ffn_conv_b', 'ffn_w_down', 'ln2_g', 'ln2_b']
BIG = ('w_in', 'w_out', 'ffn_w_gate', 'ffn_w_up', 'ffn_w_down')
SMALL = tuple(n for n in WEIGHTS if n not in BIG)
PACK_ROWS = ROW_TILE


def _pack(arrs):
    flat = jnp.concatenate([a.reshape(-1) for a in arrs])
    unit = 128 * PACK_ROWS
    size = -(-flat.shape[0] // unit) * unit
    return jnp.pad(flat, (0, size - flat.shape[0])).reshape(-1, 128)


def _unpack(packed, shapes):
    flat = packed.reshape(-1)
    out, pos = [], 0
    for shp in shapes:
        n = math.prod(shp)
        out.append(flat[pos:pos + n].reshape(shp))
        pos += n
    return out


def _pair_reduce(name, g):
    layers, shards, rows, cols = g.shape
    r2 = rows // 2
    rt = _pick_rows(r2)
    nr = r2 // rt
    nsteps = layers * shards * nr

    def kern(c_ref, mine_ref, other_ref, o_ref, buf, send, recv, credit):
        x, y, c = _position()
        sibling = (x, y, 1 - c)
        k = pl.program_id(0) * nr + pl.program_id(1)
        slot = k % 2

        @pl.when(k >= 2)
        def _():
            pl.semaphore_wait(credit, 1)

        cp = pltpu.make_async_remote_copy(src_ref=other_ref, dst_ref=buf.at[slot], send_sem=send.at[slot],
                                          recv_sem=recv.at[slot], device_id=sibling, device_id_type=MESH)
        cp.start()
        cp.wait_recv()
        o_ref[...] = (mine_ref[...] + buf[slot]).astype(BF16)
        cp.wait_send()

        @pl.when(k + 2 < nsteps)
        def _():
            pl.semaphore_signal(credit, 1, device_id=sibling, device_id_type=MESH)

    blk = (1, rt, cols)
    grid_spec = pltpu.PrefetchScalarGridSpec(
        num_scalar_prefetch=1, grid=(layers * shards, nr),
        in_specs=[pl.BlockSpec(blk, lambda m, r, c_ref: (m, c_ref[0] * nr + r, 0)),
                  pl.BlockSpec(blk, lambda m, r, c_ref: (m, (1 - c_ref[0]) * nr + r, 0))],
        out_specs=pl.BlockSpec(blk, lambda m, r, c_ref: (m, r, 0)),
        scratch_shapes=[pltpu.VMEM((2,) + blk, F32), pltpu.SemaphoreType.DMA((2,)),
                        pltpu.SemaphoreType.DMA((2,)), pltpu.SemaphoreType.REGULAR])
    core = lax.axis_index("c").astype(jnp.int32).reshape(1)
    g3 = g.reshape(layers * shards, rows, cols)
    out = pl.pallas_call(
        kern, name=name, grid_spec=grid_spec,
        out_shape=jax.ShapeDtypeStruct((layers * shards, r2, cols), BF16),
        compiler_params=_params(("arbitrary", "arbitrary")),
    )(core, g3, g3)
    return out.reshape(layers, shards, r2, cols)


def _pair_merge(name, h):
    m, r2, cols = h.shape
    rt = _pick_rows(r2)
    nr = r2 // rt
    nsteps = m * nr

    def kern(h_ref, o_ref, buf, send, recv, credit):
        x, y, c = _position()
        sibling = (x, y, 1 - c)
        k = pl.program_id(0) * nr + pl.program_id(1)
        slot = k % 2

        @pl.when(k >= 2)
        def _():
            pl.semaphore_wait(credit, 1)

        cp = pltpu.make_async_remote_copy(src_ref=h_ref, dst_ref=buf.at[slot], send_sem=send.at[slot],
                                          recv_sem=recv.at[slot], device_id=sibling, device_id_type=MESH)
        cp.start()
        cp.wait_recv()
        o_ref[0, pl.ds(c, 1)] = h_ref[...]
        o_ref[0, pl.ds(1 - c, 1)] = buf[slot]
        cp.wait_send()

        @pl.when(k + 2 < nsteps)
        def _():
            pl.semaphore_signal(credit, 1, device_id=sibling, device_id_type=MESH)

    blk = (1, rt, cols)
    out = pl.pallas_call(
        kern, name=name, grid=(m, nr),
        in_specs=[pl.BlockSpec(blk, lambda i, r: (i, r, 0))],
        out_specs=pl.BlockSpec((1, 2, rt, cols), lambda i, r: (i, 0, r, 0)),
        out_shape=jax.ShapeDtypeStruct((m, 2, r2, cols), h.dtype),
        scratch_shapes=[pltpu.VMEM((2,) + blk, h.dtype), pltpu.SemaphoreType.DMA((2,)),
                        pltpu.SemaphoreType.DMA((2,)), pltpu.SemaphoreType.REGULAR],
        compiler_params=_params(("arbitrary", "arbitrary")),
    )(h)
    return out.reshape(m, 2 * r2, cols)


def _reduce_big(grads):
    pair = [_pair_reduce("pair_reduce_" + n, g) for n, g in zip(BIG, grads)]
    scat = _chip_scatter(pair)
    out = []
    for t, n in enumerate(BIG):
        own, got = scat[2 * t], scat[2 * t + 1]
        half = _sum_parts("chip_sum", [(own, None)] + [(got, j) for j in range(3)], own.shape)
        out.append(_pair_merge("grad_merge_" + n, half.reshape(half.shape[0], half.shape[2], half.shape[3])))
    return out


def _step(a):
    x = a['x'][0]
    target = a['loss_target'][0]
    t = x.shape[0]
    xi, yi, _ = _position()
    chip = 2 * xi + yi
    cos, sin_s = _rope_tables(t)

    gathered = _allgather_chips([_cast_bf16(a[n])[:, None] for n in BIG]
                                + [a[n][:, None] for n in ('s5_glu_w', 'lru_conv_w', 'ffn_conv_w')],
                                halved=range(len(BIG)))
    full = dict(zip(BIG + ('s5_glu_w', 'lru_conv_w', 'ffn_conv_w'), gathered))
    for n in BIG:
        layers, chips, r2, cols = full[n].shape
        full[n] = _pair_merge("weight_merge_" + n, full[n].reshape(layers * chips, r2, cols)).reshape(
            layers, chips, 2 * r2, cols)

    def layer_params(l):
        p = {n: a[n][l] for n in SMALL}
        p['layer'] = l
        p['w_in'] = full['w_in']
        p['w_out'] = full['w_out'].reshape(DEPTH, D_MODEL, D_MODEL)
        p['ffn_w_gate'] = full['ffn_w_gate']
        p['ffn_w_up'] = full['ffn_w_up']
        p['ffn_w_down'] = full['ffn_w_down']
        p['s5_glu_w'] = full['s5_glu_w'][l].reshape(D_S5, D_S5)
        p['lru_conv_w'] = full['lru_conv_w'][l].transpose(1, 0, 2).reshape(LRU_CONV, D_LRU)
        p['ffn_conv_w'] = full['ffn_conv_w'][l]
        p['ffn_conv_b'] = a['ffn_conv_b'][l].reshape(N_CHIPS, 1, FF_SH)
        return p

    params = [layer_params(l) for l in range(DEPTH)]
    derived = [_layer_weights(p) for p in params]
    saved = []
    h, hb = x, _cast_bf16(x)
    for l in range(DEPTH):
        h, hb, s = _layer_fwd(h, hb, params[l], derived[l], cos, sin_s)
        saved.append(s)
    loss_part, dr, drb, ln2_g, ln2_b, _ = _loss_head(h, target, saved[-1]['xhat2'], saved[-1]['rstd2'],
                                                     _vec(params[-1]['ln2_g']))
    loss = lax.psum(loss_part[0, 0], ("x", "y", "c"))
    grads = [None] * DEPTH
    big = {n: lax.empty((DEPTH, N_CHIPS) + a[n].shape[1:], F32) for n in BIG}
    big['w_out'] = big['w_out'].reshape(DEPTH, D_MODEL, D_MODEL)
    for l in reversed(range(DEPTH)):
        below = (saved[l - 1], params[l - 1]) if l > 0 else None
        out, grads[l], big = _layer_bwd(dr, drb, saved[l], params[l], derived[l], cos, sin_s, big, below)
        grads[l]['ln2_g'], grads[l]['ln2_b'] = ln2_g, ln2_b
        if l > 0:
            dr, drb, ln2_g, ln2_b, _ = out
        else:
            grad_x = out[None]

    def stacked(n):
        return jnp.stack([grads[l][n] for l in range(DEPTH)])

    big['w_out'] = big['w_out'].reshape(DEPTH, N_CHIPS, OUT_SH, D_MODEL)
    grad = dict(zip(BIG, _reduce_big([big[n] for n in BIG])))
    small_local = [stacked(n) for n in SMALL]
    packed = _allgather_devices(_pack(small_local)[None])
    total = _sum_parts("device_sum", [(packed, j) for j in range(N_DEV)], packed.shape[1:])
    small_sum = dict(zip(SMALL, _unpack(total, [g.shape for g in small_local])))
    for n in SMALL:
        g = small_sum[n]
        if n == 's5_glu_w':
            g = lax.dynamic_slice_in_dim(g, chip * (D_S5 // N_CHIPS), D_S5 // N_CHIPS, axis=1)
        elif n == 'lru_conv_w':
            g = lax.dynamic_slice_in_dim(g, chip * (D_LRU // N_CHIPS), D_LRU // N_CHIPS, axis=2)
        elif n == 'ffn_conv_w':
            g = lax.dynamic_index_in_dim(g, chip, axis=1, keepdims=False)
        grad[n] = g.reshape(a[n].shape)

    delta, new_m, new_v = {}, {}, {}
    for n in WEIGHTS:
        delta[n], new_m[n], new_v[n] = _adamw("adamw_" + n, a[n], grad[n], a['m_' + n], a['v_' + n])
    return (loss, grad_x, *[grad[n] for n in WEIGHTS], *[delta[n] for n in WEIGHTS],
            *[new_m[n] for n in WEIGHTS], *[new_v[n] for n in WEIGHTS])


def kernel(x, w_in, b_in, attn_sinks, s5_a_re, s5_a_im, s5_b_re, s5_b_im, s5_c_re, s5_c_im, s5_d, s5_log_dt, s5_glu_w, s5_glu_b, lru_conv_w, lru_conv_b, lru_wx, lru_bx, lru_wa, lru_ba, lru_a_param, mix_norm_g, w_out, b_out, ln1_g, ln1_b, ffn_w_gate, ffn_w_up, ffn_conv_w, ffn_conv_b, ffn_w_down, ln2_g, ln2_b, loss_target, m_w_in, m_b_in, m_attn_sinks, m_s5_a_re, m_s5_a_im, m_s5_b_re, m_s5_b_im, m_s5_c_re, m_s5_c_im, m_s5_d, m_s5_log_dt, m_s5_glu_w, m_s5_glu_b, m_lru_conv_w, m_lru_conv_b, m_lru_wx, m_lru_bx, m_lru_wa, m_lru_ba, m_lru_a_param, m_mix_norm_g, m_w_out, m_b_out, m_ln1_g, m_ln1_b, m_ffn_w_gate, m_ffn_w_up, m_ffn_conv_w, m_ffn_conv_b, m_ffn_w_down, m_ln2_g, m_ln2_b, v_w_in, v_b_in, v_attn_sinks, v_s5_a_re, v_s5_a_im, v_s5_b_re, v_s5_b_im, v_s5_c_re, v_s5_c_im, v_s5_d, v_s5_log_dt, v_s5_glu_w, v_s5_glu_b, v_lru_conv_w, v_lru_conv_b, v_lru_wx, v_lru_bx, v_lru_wa, v_lru_ba, v_lru_a_param, v_mix_norm_g, v_w_out, v_b_out, v_ln1_g, v_ln1_b, v_ffn_w_gate, v_ffn_w_up, v_ffn_conv_w, v_ffn_conv_b, v_ffn_w_down, v_ln2_g, v_ln2_b):
    return _step(dict(locals()))
```

```python
import functools
import math

import jax
import jax.numpy as jnp
from jax import lax
from jax.experimental import pallas as pl
from jax.experimental.pallas import tpu as pltpu

F32 = jnp.float32
BF16 = jnp.bfloat16
MESH = pl.DeviceIdType.MESH
ANY = pl.BlockSpec(memory_space=pl.ANY)

D_MODEL = 1024
DEPTH = 4
HEAD_DIM = 64
N_Q_HEADS = 8
N_KV_HEADS = 2
Q_PER_KV = 4
D_ATTN = 512
D_KV = 128
ATTN_BLOCK = 128
ROPE_THETA = 10000.0
D_S5 = 256
S5_GROUP = 16
S5_GROUPS = 16
S5_STATE = 64
N_STATE = S5_GROUPS * S5_STATE
D_LRU = 256
LRU_HEADS = 4
LRU_HEAD_DIM = 64
LRU_CONV = 4
LRU_C = 8.0
D_IN = 1536
D_FF = 2816
FFN_CONV = 3
N_CHIPS = 4
N_DEV = 8
IN_SH = D_IN // N_CHIPS
FF_SH = D_FF // N_CHIPS
OUT_SH = D_MODEL // N_CHIPS
ALPHA = (2 * DEPTH) ** 0.25
LN_EPS = 1e-5
RMS_EPS = 1e-6
ADAM_LR = 0.001
ADAM_B1 = 0.9
ADAM_B2 = 0.999
ADAM_EPS = 1e-08
ADAM_WD = 0.01
ADAM_STEP = 10

SUBLANES = 8
VMEM_MB = 56


def _params(sem):
    return pltpu.CompilerParams(dimension_semantics=sem, vmem_limit_bytes=VMEM_MB << 20)


def _row_tile(t, pref):
    return min(t, pref)


def _matmul(name, a, b, *, a_blk, a_map, b_blk, b_map, out_shape, o_blk, o_map, grid, dims,
            out_dtype=F32, bias=None, bias_blk=None, bias_map=None, add=None, add_scale=1.0, pair2=None,
            into=None, ln_bwd=None):
    nk = grid[2]
    acc_shape = tuple(d for d in o_blk if d is not None)
    n_in = 2 if pair2 is None else 4

    def kern(*refs):
        p = n_in
        bias_ref = add_ref = None
        if bias is not None:
            bias_ref = refs[p]
            p += 1
        if add is not None:
            add_ref = refs[p]
            p += 1
        if into is not None:
            p += 1
        if ln_bwd is not None:
            ln_in = refs[p:p + 3]
            ln_out = refs[p + 4:p + 8]
            o_ref, acc = refs[p + 3], refs[p + 8]
        else:
            o_ref, acc = refs[p], refs[p + 1]
        k = pl.program_id(2)
        first_tile = pl.program_id(0) == 0

        def product():
            r = _dot(refs[0][...].astype(BF16), refs[1][...].astype(BF16), dims)
            if pair2 is not None:
                r = r + _dot(refs[2][...].astype(BF16), refs[3][...].astype(BF16), dims)
            return r

        def finish(r):
            if bias_ref is not None:
                r = r + bias_ref[...]
            if add_ref is not None:
                r = r + add_scale * add_ref[...]
            if ln_bwd is None:
                o_ref[...] = r.astype(out_dtype)
            else:
                @pl.when(first_tile)
                def _():
                    for ref in ln_out[1:]:
                        ref[...] = jnp.zeros_like(ref)

                _ln_bwd_tile(r, ln_in[0][...], ln_in[1][...], ln_in[2][...], o_ref, *ln_out)

        if nk == 1:
            finish(product())
        else:
            @pl.when(k == 0)
            def _():
                acc[...] = jnp.zeros_like(acc)

            acc[...] += product()

            @pl.when(k == nk - 1)
            def _():
                finish(acc[...])

    in_specs = [pl.BlockSpec(a_blk, a_map), pl.BlockSpec(b_blk, b_map)]
    args = [a, b]
    if pair2 is not None:
        in_specs += [pl.BlockSpec(a_blk, a_map), pl.BlockSpec(b_blk, b_map)]
        args += list(pair2)
    if bias is not None:
        in_specs.append(pl.BlockSpec(bias_blk, bias_map))
        args.append(bias)
    if add is not None:
        in_specs.append(pl.BlockSpec(o_blk, lambda i, j, k: o_map(i, j)))
        args.append(add)
    aliases = {}
    if into is not None:
        aliases = {len(args): 0}
        in_specs.append(ANY)
        args.append(into)
    o_spec = pl.BlockSpec(o_blk, lambda i, j, k: o_map(i, j))
    out_specs, out_shapes = o_spec, jax.ShapeDtypeStruct(out_shape, out_dtype)
    semantics = ("parallel", "parallel", "arbitrary")
    if ln_bwd is not None:
        vec = pl.BlockSpec((1, o_blk[-1]), lambda i, j, k: (0, 0))
        in_specs += [o_spec, pl.BlockSpec((o_blk[0], 1), lambda i, j, k: (i, 0)), vec]
        args += list(ln_bwd)
        vshape = jax.ShapeDtypeStruct((1, o_blk[-1]), F32)
        out_specs = [o_spec, o_spec, vec, vec, vec]
        out_shapes = [out_shapes, jax.ShapeDtypeStruct(out_shape, BF16), vshape, vshape, vshape]
        semantics = ("arbitrary", "arbitrary", "arbitrary")
    return pl.pallas_call(
        kern, name=name, grid=grid, in_specs=in_specs, out_specs=out_specs, out_shape=out_shapes,
        scratch_shapes=[pltpu.VMEM(acc_shape if nk > 1 else (SUBLANES, 128), F32)],
        input_output_aliases=aliases,
        compiler_params=_params(semantics),
    )(*args)


NN = ((1,), (0,))
NT = ((1,), (1,))
TN = ((0,), (0,))
TM = 512


def _sigmoid(x):
    return 0.5 * jnp.tanh(0.5 * x) + 0.5


_GELU_C = math.sqrt(2.0 / math.pi)


def _gelu(x):
    return 0.5 * x * (1.0 + jnp.tanh(_GELU_C * (x + 0.044715 * x * x * x)))


def _gelu_grad(x):
    th = jnp.tanh(_GELU_C * (x + 0.044715 * x * x * x))
    return 0.5 * (1.0 + th) + 0.5 * x * (1.0 - th * th) * _GELU_C * (1.0 + 3 * 0.044715 * x * x)


def _rope_swap(t):
    lane = lax.broadcasted_iota(jnp.int32, t.shape, 1)
    lo = (lane % HEAD_DIM) < (HEAD_DIM // 2)
    return jnp.where(lo, pltpu.roll(t, 128 - HEAD_DIM // 2, 1), pltpu.roll(t, HEAD_DIM // 2, 1))


D_QKV = D_ATTN + 2 * D_KV
TMM = 1024


def _in_proj(xb, w_in, b_in, cos, sin_s, layer):
    t = xb.shape[0]
    tm = _row_tile(t, TMM)

    def kern(x_ref, w_ref, b_ref, c_ref, s_ref, q_ref, u_ref):
        x = x_ref[...]
        c = c_ref[...]
        s = s_ref[...]
        for j in range(N_CHIPS):
            pj = _dot(x, w_ref[j], NN) + b_ref[:, j * IN_SH:(j + 1) * IN_SH]
            for ch in range(IN_SH // 128):
                col = j * IN_SH + ch * 128
                v = pj[:, ch * 128:(ch + 1) * 128]
                if col < D_ATTN + D_KV:
                    v = v * c + _rope_swap(v) * s
                if col < D_ATTN:
                    v = v * (HEAD_DIM ** -0.5)
                if col < D_QKV:
                    q_ref[:, col:col + 128] = v.astype(BF16)
                else:
                    u_ref[:, col - D_QKV:col - D_QKV + 128] = v

    row = lambda w: pl.BlockSpec((tm, w), lambda i: (i, 0))
    return pl.pallas_call(
        kern, name="in_proj", grid=(t // tm,),
        in_specs=[row(D_MODEL), pl.BlockSpec((None, N_CHIPS, D_MODEL, IN_SH), lambda i: (layer, 0, 0, 0)),
                  pl.BlockSpec((1, D_IN), lambda i: (0, 0)), row(128), row(128)],
        out_specs=[row(D_QKV), row(D_IN - D_QKV)],
        out_shape=[jax.ShapeDtypeStruct((t, D_QKV), BF16), jax.ShapeDtypeStruct((t, D_IN - D_QKV), F32)],
        compiler_params=_params(("parallel",)),
    )(xb, w_in, b_in, cos, sin_s)


def _attn_mask(i):
    qi = lax.broadcasted_iota(jnp.int32, (ATTN_BLOCK, 2 * ATTN_BLOCK), 0)
    si = lax.broadcasted_iota(jnp.int32, (ATTN_BLOCK, 2 * ATTN_BLOCK), 1)
    diff = qi + ATTN_BLOCK - si
    return (diff >= 0) & (diff < ATTN_BLOCK) & ((si >= ATTN_BLOCK) | (i > 0))


def _row_sums(x, ones):
    hi = x.astype(BF16)
    lo = (x - hi.astype(F32)).astype(BF16)
    return _dot(hi, ones, NN) + _dot(lo, ones, NN)


def _attn_fwd(qkv, sinks):
    t = qkv.shape[0]
    nb = t // ATTN_BLOCK

    def kern(q_ref, kp_ref, kc_ref, vp_ref, vc_ref, s_ref, o_ref, l_ref):
        i = pl.program_id(0)
        si = lax.broadcasted_iota(jnp.int32, (2 * ATTN_BLOCK, ATTN_BLOCK), 0)
        qi = lax.broadcasted_iota(jnp.int32, (2 * ATTN_BLOCK, ATTN_BLOCK), 1)
        diff = qi + ATTN_BLOCK - si
        valid = (diff >= 0) & (diff < ATTN_BLOCK) & ((si >= ATTN_BLOCK) | (i > 0))
        kband = jnp.concatenate([kp_ref[...], kc_ref[...]], axis=0)
        vband = jnp.concatenate([vp_ref[...], vc_ref[...]], axis=0)
        ks = [kband[:, kh * HEAD_DIM:(kh + 1) * HEAD_DIM] for kh in range(N_KV_HEADS)]
        vs = [vband[:, kh * HEAD_DIM:(kh + 1) * HEAD_DIM] for kh in range(N_KV_HEADS)]
        scores = [_dot(ks[h // Q_PER_KV], q_ref[:, h * HEAD_DIM:(h + 1) * HEAD_DIM], NT) for h in range(N_Q_HEADS)]
        probs, lses = [], []
        for h in range(N_Q_HEADS):
            s = jnp.where(valid, scores[h], -jnp.inf)
            sink = s_ref[0:1, h:h + 1]
            m = jnp.maximum(jnp.max(s, axis=0, keepdims=True), sink)
            e = jnp.exp(s - m)
            denom = jnp.sum(e, axis=0, keepdims=True) + jnp.exp(sink - m)
            probs.append((e * (1.0 / denom)).astype(BF16))
            lses.append(m + jnp.log(denom))
        outs = [_dot(vs[h // Q_PER_KV], probs[h], TN) for h in range(N_Q_HEADS)]
        for c in range(N_Q_HEADS // 2):
            o_ref[:, c * 128:(c + 1) * 128] = jnp.concatenate([outs[2 * c], outs[2 * c + 1]], axis=0).T
        rid = lax.broadcasted_iota(jnp.int32, (N_Q_HEADS, ATTN_BLOCK), 0)
        rows = jnp.zeros((N_Q_HEADS, ATTN_BLOCK), F32)
        for h in range(N_Q_HEADS):
            rows = jnp.where(rid == h, lses[h], rows)
        rows = jnp.concatenate([rows, jnp.zeros((ATTN_BLOCK - N_Q_HEADS, ATTN_BLOCK), F32)], axis=0)
        l_ref[...] = rows.T[:, 0:N_Q_HEADS]

    blk = lambda w, f: pl.BlockSpec((ATTN_BLOCK, w), f)
    return pl.pallas_call(
        kern, name="attn_fwd", grid=(nb,),
        in_specs=[blk(512, lambda i: (i, 0)),
                  blk(128, lambda i: (jnp.maximum(i - 1, 0), 4)), blk(128, lambda i: (i, 4)),
                  blk(128, lambda i: (jnp.maximum(i - 1, 0), 5)), blk(128, lambda i: (i, 5)),
                  pl.BlockSpec((1, N_Q_HEADS), lambda i: (0, 0))],
        out_specs=[blk(512, lambda i: (i, 0)), blk(N_Q_HEADS, lambda i: (i, 0))],
        out_shape=[jax.ShapeDtypeStruct((t, D_ATTN), F32), jax.ShapeDtypeStruct((t, N_Q_HEADS), F32)],
        compiler_params=_params(("parallel",)),
    )(qkv, qkv, qkv, qkv, qkv, sinks)


def _attn_bwd(qkv, o, do, lse, sinks):
    t = qkv.shape[0]
    nb = t // ATTN_BLOCK

    def kern(q_ref, kp_ref, kc_ref, vp_ref, vc_ref, o_ref, do_ref, l_ref, s_ref,
             dq_ref, dk_ref, dv_ref, ds_ref, ck, cv):
        i = pl.program_id(0)

        @pl.when(i == 0)
        def _():
            ds_ref[...] = jnp.zeros_like(ds_ref)
            ck[...] = jnp.zeros_like(ck)
            cv[...] = jnp.zeros_like(cv)

        @pl.when(i < nb)
        def _():
            valid = _attn_mask(i)
            kband = jnp.concatenate([kp_ref[...], kc_ref[...]], axis=0)
            vband = jnp.concatenate([vp_ref[...], vc_ref[...]], axis=0)
            heads = range(N_Q_HEADS)
            sl = [slice(h * HEAD_DIM, (h + 1) * HEAD_DIM) for h in heads]
            ks = [kband[:, kh * HEAD_DIM:(kh + 1) * HEAD_DIM] for kh in range(N_KV_HEADS)]
            vs = [vband[:, kh * HEAD_DIM:(kh + 1) * HEAD_DIM] for kh in range(N_KV_HEADS)]
            qs = [q_ref[:, sl[h]] for h in heads]
            d_os = [do_ref[:, sl[h]] for h in heads]
            dobs = [d.astype(BF16) for d in d_os]
            scores = [_dot(qs[h], ks[h // Q_PER_KV], NT) for h in heads]
            dps = [_dot(dobs[h], vs[h // Q_PER_KV], NT) for h in heads]
            col_head = lax.broadcasted_iota(jnp.int32, (D_ATTN, 128), 0) // HEAD_DIM
            head_ones = (col_head == lax.broadcasted_iota(jnp.int32, (D_ATTN, 128), 1)).astype(BF16)
            deltas = _row_sums(do_ref[...] * o_ref[...], head_ones)
            pbs, dscs = [], []
            for h in heads:
                lse_h = l_ref[:, h:h + 1]
                p = jnp.where(valid, jnp.exp(scores[h] - lse_h), 0.0)
                delta = deltas[:, h:h + 1]
                pbs.append(p.astype(BF16))
                dscs.append((p * (dps[h] - delta)).astype(BF16))
                psink = jnp.exp(s_ref[0:1, h:h + 1] - lse_h)
                ds_ref[0:1, h:h + 1] += -jnp.sum(psink * delta, axis=0, keepdims=True)
            dqs = [_dot(dscs[h], ks[h // Q_PER_KV], NN) for h in heads]
            dkb = [sum(_dot(dscs[h], qs[h], TN) for h in heads if h // Q_PER_KV == kh) for kh in range(N_KV_HEADS)]
            dvb = [sum(_dot(pbs[h], dobs[h], TN) for h in heads if h // Q_PER_KV == kh) for kh in range(N_KV_HEADS)]
            for h in heads:
                dq_ref[:, sl[h]] = dqs[h]
            dk_band = jnp.concatenate(dkb, axis=1)
            dv_band = jnp.concatenate(dvb, axis=1)
            dk_ref[...] = ck[...] + dk_band[:ATTN_BLOCK]
            dv_ref[...] = cv[...] + dv_band[:ATTN_BLOCK]
            ck[...] = dk_band[ATTN_BLOCK:]
            cv[...] = dv_band[ATTN_BLOCK:]

        @pl.when(i == nb)
        def _():
            dk_ref[...] = ck[...]
            dv_ref[...] = cv[...]

    blk = lambda w, f: pl.BlockSpec((ATTN_BLOCK, w), f)
    cur = lambda i: jnp.minimum(i, nb - 1)
    prev = lambda i: jnp.clip(i - 1, 0, nb - 1)
    return pl.pallas_call(
        kern, name="attn_bwd", grid=(nb + 1,),
        in_specs=[blk(512, lambda i: (cur(i), 0)),
                  blk(128, lambda i: (prev(i), 4)), blk(128, lambda i: (cur(i), 4)),
                  blk(128, lambda i: (prev(i), 5)), blk(128, lambda i: (cur(i), 5)),
                  blk(512, lambda i: (cur(i), 0)), blk(512, lambda i: (cur(i), 0)),
                  blk(N_Q_HEADS, lambda i: (cur(i), 0)),
                  pl.BlockSpec((1, N_Q_HEADS), lambda i: (0, 0))],
        out_specs=[blk(512, lambda i: (cur(i), 0)), blk(128, lambda i: (prev(i), 0)),
                   blk(128, lambda i: (prev(i), 0)), pl.BlockSpec((1, N_Q_HEADS), lambda i: (0, 0))],
        out_shape=[jax.ShapeDtypeStruct((t, D_ATTN), F32), jax.ShapeDtypeStruct((t, D_KV), F32),
                   jax.ShapeDtypeStruct((t, D_KV), F32), jax.ShapeDtypeStruct((1, N_Q_HEADS), F32)],
        scratch_shapes=[pltpu.VMEM((ATTN_BLOCK, D_KV), F32), pltpu.VMEM((ATTN_BLOCK, D_KV), F32)],
        compiler_params=_params(("arbitrary",)),
    )(qkv, qkv, qkv, qkv, qkv, o, do, lse, sinks)


_GROUPS = ((0, D_ATTN), (D_ATTN, D_ATTN + D_S5), (D_ATTN + D_S5, D_MODEL))


def _mix_out_ln(ya, ys, yl, mg, w_out, b_out, xres, g, b, layer):
    t = xres.shape[0]
    tm = _row_tile(t, TM)

    def kern(a_ref, s_ref, l_ref, mg_ref, w_ref, bias_ref, x_ref, g_ref, b_ref, m_ref, y_ref, yb_ref, h_ref, r_ref):
        for (lo, hi), ref in zip(_GROUPS, (a_ref, s_ref, l_ref)):
            v = ref[...]
            n = v * lax.rsqrt(jnp.mean(v * v, axis=-1, keepdims=True) + RMS_EPS)
            m_ref[:, lo:hi] = (n * mg_ref[:, lo:hi]).astype(BF16)
        r = ALPHA * x_ref[...] + _dot(m_ref[...], w_ref[...], NN) + bias_ref[...]
        mu = jnp.mean(r, axis=-1, keepdims=True)
        xc = r - mu
        rstd = lax.rsqrt(jnp.mean(xc * xc, axis=-1, keepdims=True) + LN_EPS)
        xhat = xc * rstd
        h_ref[...] = xhat
        r_ref[...] = rstd
        y = xhat * g_ref[...] + b_ref[...]
        y_ref[...] = y
        yb_ref[...] = y.astype(BF16)

    rowb = lambda w: pl.BlockSpec((tm, w), lambda i: (i, 0))
    row = rowb(D_MODEL)
    vec = pl.BlockSpec((1, D_MODEL), lambda i: (0, 0))
    big = lambda dt: jax.ShapeDtypeStruct((t, D_MODEL), dt)
    return pl.pallas_call(
        kern, name="mix_out_ln", grid=(t // tm,),
        in_specs=[rowb(D_ATTN), rowb(D_S5), rowb(D_LRU), vec,
                  pl.BlockSpec((None, D_MODEL, D_MODEL), lambda i: (layer, 0, 0)), vec, row, vec, vec],
        out_specs=[row, row, row, row, pl.BlockSpec((tm, 1), lambda i: (i, 0))],
        out_shape=[big(BF16), big(F32), big(BF16), big(F32), jax.ShapeDtypeStruct((t, 1), F32)],
        compiler_params=_params(("parallel",)),
    )(ya, ys, yl, mg, w_out, b_out, xres, g, b)


def _d_mix_rms(dx1, xhat, rstd, lg, w_out, ya, ys, yl, mg, layer):
    t = dx1.shape[0]
    tm = _row_tile(t, TM)

    def kern(d_ref, h_ref, r_ref, lg_ref, w_ref, a_ref, s_ref, l_ref, g_ref,
             dr_ref, drb_ref, dlg_ref, dlb_ref, sr_ref, da_ref, ds_ref, dl_ref, dg_ref):
        @pl.when(pl.program_id(0) == 0)
        def _():
            for ref in (dlg_ref, dlb_ref, sr_ref, dg_ref):
                ref[...] = jnp.zeros_like(ref)

        _ln_bwd_tile(d_ref[...], h_ref[...], r_ref[...], lg_ref[...], dr_ref, drb_ref, dlg_ref, dlb_ref, sr_ref)
        dmix = _dot(drb_ref[...], w_ref[...], NT)
        for (lo, hi), ref, out in zip(_GROUPS, (a_ref, s_ref, l_ref), (da_ref, ds_ref, dl_ref)):
            v = ref[...]
            rstd = lax.rsqrt(jnp.mean(v * v, axis=-1, keepdims=True) + RMS_EPS)
            n = v * rstd
            dm = dmix[:, lo:hi]
            dg_ref[:, lo:hi] += jnp.sum(dm * n, axis=0, keepdims=True)
            dn = dm * g_ref[:, lo:hi]
            out[...] = rstd * (dn - n * jnp.mean(dn * n, axis=-1, keepdims=True))

    rowb = lambda w: pl.BlockSpec((tm, w), lambda i: (i, 0))
    vec = pl.BlockSpec((1, D_MODEL), lambda i: (0, 0))
    vshape = jax.ShapeDtypeStruct((1, D_MODEL), F32)
    return pl.pallas_call(
        kern, name="d_mix_rms", grid=(t // tm,),
        in_specs=[rowb(D_MODEL), rowb(D_MODEL), pl.BlockSpec((tm, 1), lambda i: (i, 0)), vec,
                  pl.BlockSpec((None, D_MODEL, D_MODEL), lambda i: (layer, 0, 0)),
                  rowb(D_ATTN), rowb(D_S5), rowb(D_LRU), vec],
        out_specs=[rowb(D_MODEL), rowb(D_MODEL), vec, vec, vec, rowb(D_ATTN), rowb(D_S5), rowb(D_LRU), vec],
        out_shape=[jax.ShapeDtypeStruct((t, D_MODEL), F32), jax.ShapeDtypeStruct((t, D_MODEL), BF16),
                   vshape, vshape, vshape, jax.ShapeDtypeStruct((t, D_ATTN), F32),
                   jax.ShapeDtypeStruct((t, D_S5), F32), jax.ShapeDtypeStruct((t, D_LRU), F32), vshape],
        compiler_params=_params(("arbitrary",)),
    )(dx1, xhat, rstd, lg, w_out, ya, ys, yl, mg)


def _matmul_ln(name, a, w, bias, xres, g, b, a_blk, a_map, w_blk, parts, layer):
    t = xres.shape[0]
    tm = a_blk[-2]

    def kern(a_ref, w_ref, bias_ref, x_ref, g_ref, b_ref, y_ref, yb_ref, h_ref, r_ref):
        if parts is None:
            f = _dot(a_ref[...], w_ref[...], NN)
        else:
            f = sum(_dot(a_ref[j], w_ref[j], NN) for j in range(parts))
        r = ALPHA * x_ref[...] + f + bias_ref[...]
        mu = jnp.mean(r, axis=-1, keepdims=True)
        xc = r - mu
        rstd = lax.rsqrt(jnp.mean(xc * xc, axis=-1, keepdims=True) + LN_EPS)
        xhat = xc * rstd
        h_ref[...] = xhat
        r_ref[...] = rstd
        y = xhat * g_ref[...] + b_ref[...]
        y_ref[...] = y
        yb_ref[...] = y.astype(BF16)

    row = pl.BlockSpec((tm, D_MODEL), lambda i: (i, 0))
    vec = pl.BlockSpec((1, D_MODEL), lambda i: (0, 0))
    big = lambda dt: jax.ShapeDtypeStruct((t, D_MODEL), dt)
    return pl.pallas_call(
        kern, name=name, grid=(t // tm,),
        in_specs=[pl.BlockSpec(a_blk, a_map), pl.BlockSpec((None,) + w_blk, lambda i: (layer,) + (0,) * len(w_blk)), vec, row, vec, vec],
        out_specs=[row, row, row, pl.BlockSpec((tm, 1), lambda i: (i, 0))],
        out_shape=[big(F32), big(BF16), big(F32), jax.ShapeDtypeStruct((t, 1), F32)],
        compiler_params=_params(("parallel",)),
    )(a, w, bias, xres, g, b)


def _ln_bwd(dy, xhat, rstd, g):
    t = dy.shape[0]
    tm = _row_tile(t, TM)

    def kern(d_ref, h_ref, r_ref, g_ref, dr_ref, drb_ref, dg_ref, db_ref, sr_ref):
        @pl.when(pl.program_id(0) == 0)
        def _():
            dg_ref[...] = jnp.zeros_like(dg_ref)
            db_ref[...] = jnp.zeros_like(db_ref)
            sr_ref[...] = jnp.zeros_like(sr_ref)

        d = d_ref[...]
        xhat = h_ref[...]
        dg_ref[...] += jnp.sum(d * xhat, axis=0, keepdims=True)
        db_ref[...] += jnp.sum(d, axis=0, keepdims=True)
        dh = d * g_ref[...]
        dr = r_ref[...] * (dh - jnp.mean(dh, axis=-1, keepdims=True)
                           - xhat * jnp.mean(dh * xhat, axis=-1, keepdims=True))
        dr_ref[...] = dr
        drb_ref[...] = dr.astype(BF16)
        sr_ref[...] += jnp.sum(dr, axis=0, keepdims=True)

    row = pl.BlockSpec((tm, D_MODEL), lambda i: (i, 0))
    vec = pl.BlockSpec((1, D_MODEL), lambda i: (0, 0))
    vshape = jax.ShapeDtypeStruct((1, D_MODEL), F32)
    return pl.pallas_call(
        kern, name="ln_bwd", grid=(t // tm,),
        in_specs=[row, row, pl.BlockSpec((tm, 1), lambda i: (i, 0)), vec],
        out_specs=[row, row, vec, vec, vec],
        out_shape=[jax.ShapeDtypeStruct((t, D_MODEL), F32), jax.ShapeDtypeStruct((t, D_MODEL), BF16),
                   vshape, vshape, vshape],
        compiler_params=_params(("arbitrary",)),
    )(dy, xhat, rstd, g)


def _ln_bwd_tile(d, xhat, rstd, g, dr_ref, drb_ref, dg_ref, db_ref, sr_ref):
    dg_ref[...] += jnp.sum(d * xhat, axis=0, keepdims=True)
    db_ref[...] += jnp.sum(d, axis=0, keepdims=True)
    dh = d * g
    dr = rstd * (dh - jnp.mean(dh, axis=-1, keepdims=True) - xhat * jnp.mean(dh * xhat, axis=-1, keepdims=True))
    dr_ref[...] = dr
    drb_ref[...] = dr.astype(BF16)
    sr_ref[...] += jnp.sum(dr, axis=0, keepdims=True)


def _loss_head(y, target, xhat, rstd, g):
    t = y.shape[0]
    tm = _row_tile(t, TM)

    def kern(y_ref, t_ref, h_ref, r_ref, g_ref, l_ref, dr_ref, drb_ref, dg_ref, db_ref, sr_ref):
        @pl.when(pl.program_id(0) == 0)
        def _():
            for ref in (l_ref, dg_ref, db_ref, sr_ref):
                ref[...] = jnp.zeros_like(ref)

        err = y_ref[...] - t_ref[...]
        part = jnp.sum(jnp.sum(err * err, axis=-1, keepdims=True), axis=0, keepdims=True)
        l_ref[...] += jnp.broadcast_to(part * (0.5 / D_MODEL), l_ref.shape)
        _ln_bwd_tile(err * (1.0 / D_MODEL), h_ref[...], r_ref[...], g_ref[...], dr_ref, drb_ref, dg_ref, db_ref, sr_ref)

    row = pl.BlockSpec((tm, D_MODEL), lambda i: (i, 0))
    vec = pl.BlockSpec((1, D_MODEL), lambda i: (0, 0))
    vshape = jax.ShapeDtypeStruct((1, D_MODEL), F32)
    return pl.pallas_call(
        kern, name="loss_head", grid=(t // tm,),
        in_specs=[row, row, row, pl.BlockSpec((tm, 1), lambda i: (i, 0)), vec],
        out_specs=[pl.BlockSpec((1, 128), lambda i: (0, 0)), row, row, vec, vec, vec],
        out_shape=[jax.ShapeDtypeStruct((1, 128), F32), jax.ShapeDtypeStruct((t, D_MODEL), F32),
                   jax.ShapeDtypeStruct((t, D_MODEL), BF16), vshape, vshape, vshape],
        compiler_params=_params(("arbitrary",)),
    )(y, target, xhat, rstd, g)


HALO = 8


def _ffn_hidden_fwd(xb, wg, wu, cw, cb, layer):
    t = xb.shape[0]
    tm = _row_tile(t, TMM)

    def kern(x_ref, wg_ref, wu_ref, cw_ref, cb_ref, g_ref, c_ref, u_ref, h_ref, ext):
        @pl.when(pl.program_id(1) == 0)
        def _():
            ext[0:HALO, :] = jnp.zeros((HALO, FF_SH), F32)

        x = x_ref[...]
        gb = _dot(x, wg_ref[...], NN).astype(BF16)
        ub = _dot(x, wu_ref[...], NN).astype(BF16)
        g_ref[...] = gb
        u_ref[...] = ub
        g = gb.astype(F32)
        w = [cw_ref[k:k + 1, :] for k in range(FFN_CONV)]
        body = cb_ref[...] + w[2] * g + w[1] * pltpu.roll(g, 1, 0) + w[0] * pltpu.roll(g, 2, 0)
        ext[HALO:, :] = g[0:HALO, :]
        head = cb_ref[...] + sum(w[k] * ext[pl.ds(HALO - (FFN_CONV - 1) + k, HALO), :] for k in range(FFN_CONV))
        gcb = jnp.concatenate([head, body[HALO:, :]], axis=0).astype(BF16)
        c_ref[...] = gcb
        gc = gcb.astype(F32)
        h_ref[...] = (gc * _sigmoid(gc) * ub.astype(F32)).astype(BF16)
        ext[0:HALO, :] = g[tm - HALO:, :]

    col = pl.BlockSpec((None, tm, FF_SH), lambda j, i: (j, i, 0))
    wspec = pl.BlockSpec((None, None, D_MODEL, FF_SH), lambda j, i: (layer, j, 0, 0))
    big = jax.ShapeDtypeStruct((N_CHIPS, t, FF_SH), BF16)
    return pl.pallas_call(
        kern, name="ffn_hidden_fwd", grid=(N_CHIPS, t // tm),
        in_specs=[pl.BlockSpec((tm, D_MODEL), lambda j, i: (i, 0)), wspec, wspec,
                  pl.BlockSpec((None, FFN_CONV, FF_SH), lambda j, i: (j, 0, 0)),
                  pl.BlockSpec((None, 1, FF_SH), lambda j, i: (j, 0, 0))],
        out_specs=[col, col, col, col], out_shape=[big, big, big, big],
        scratch_shapes=[pltpu.VMEM((2 * HALO, FF_SH), F32)],
        compiler_params=_params(("parallel", "arbitrary")),
    )(xb, wg, wu, cw, cb)


def _ffn_hidden_bwd(drb, gpre, gconv, up, wd, cw, layer):
    t = drb.shape[0]
    tm = _row_tile(t, TMM)
    nt = t // tm
    rb = lambda i: nt - 1 - i

    def kern(d_ref, g_ref, c_ref, u_ref, wd_ref, cw_ref, du_ref, dg_ref, dw_ref, db_ref, ext):
        @pl.when(pl.program_id(1) == 0)
        def _():
            dw_ref[...] = jnp.zeros_like(dw_ref)
            db_ref[...] = jnp.zeros_like(db_ref)
            ext[HALO:, :] = jnp.zeros((HALO, FF_SH), F32)

        dh = _dot(d_ref[...], wd_ref[...], NT)
        gc = c_ref[...].astype(F32)
        sg = _sigmoid(gc)
        du_ref[...] = (dh * (gc * sg)).astype(BF16)
        dgc = dh * u_ref[...].astype(F32) * (sg * (1.0 + gc * (1.0 - sg)))
        db_ref[...] += jnp.sum(dgc, axis=0, keepdims=True)
        g = g_ref[...].astype(F32)
        w = [cw_ref[k:k + 1, :] for k in range(FFN_CONV)]
        taps = [pltpu.roll(dgc, tm - 2, 0), pltpu.roll(dgc, tm - 1, 0), dgc]
        body = sum(w[k] * taps[k] for k in range(FFN_CONV))
        last = slice(tm - HALO, tm)
        ext[0:HALO, :] = dgc[last, :]
        tail_taps = [ext[pl.ds(FFN_CONV - 1 - k, HALO), :] for k in range(FFN_CONV)]
        tail = sum(w[k] * tail_taps[k] for k in range(FFN_CONV))
        dg_ref[...] = jnp.concatenate([body[0:tm - HALO, :], tail], axis=0).astype(BF16)
        for k in range(FFN_CONV):
            dw_ref[k:k + 1, :] += (jnp.sum(g * taps[k], axis=0, keepdims=True)
                                   + jnp.sum(g[last, :] * (tail_taps[k] - taps[k][last, :]), axis=0, keepdims=True))
        ext[HALO:, :] = dgc[0:HALO, :]

    col = pl.BlockSpec((None, tm, FF_SH), lambda j, i: (j, rb(i), 0))
    cws = pl.BlockSpec((None, FFN_CONV, FF_SH), lambda j, i: (j, 0, 0))
    cbs = pl.BlockSpec((None, 1, FF_SH), lambda j, i: (j, 0, 0))
    big = jax.ShapeDtypeStruct((N_CHIPS, t, FF_SH), BF16)
    return pl.pallas_call(
        kern, name="ffn_hidden_bwd", grid=(N_CHIPS, nt),
        in_specs=[pl.BlockSpec((tm, D_MODEL), lambda j, i: (rb(i), 0)), col, col, col,
                  pl.BlockSpec((None, None, FF_SH, D_MODEL), lambda j, i: (layer, j, 0, 0)), cws],
        out_specs=[col, col, cws, cbs],
        out_shape=[big, big, jax.ShapeDtypeStruct((N_CHIPS, FFN_CONV, FF_SH), F32),
                   jax.ShapeDtypeStruct((N_CHIPS, 1, FF_SH), F32)],
        scratch_shapes=[pltpu.VMEM((2 * HALO, FF_SH), F32)],
        compiler_params=_params(("parallel", "arbitrary")),
    )(drb, gpre, gconv, up, wd, cw)


def _s5_coefs(ar, ai, reverse):
    if reverse:
        ai = -ai
    pw = [(ar, ai)]
    for _ in range(SUBLANES - 1):
        pr, pi = pw[-1]
        pw.append((pr * ar - pi * ai, pr * ai + pi * ar))
    rows = jnp.arange(SUBLANES)[:, None]
    out = []
    for s in (1, 2, 4):
        keep = (rows + s <= SUBLANES - 1) if reverse else (rows >= s)
        out += [jnp.where(keep, pw[s - 1][0][None], 0.0), jnp.where(keep, pw[s - 1][1][None], 0.0)]
    order = list(range(SUBLANES - 1, -1, -1)) if reverse else list(range(SUBLANES))
    out += [jnp.stack([pw[k][0] for k in order]), jnp.stack([pw[k][1] for k in order])]
    return jnp.stack(out).astype(F32)


def _s5_scan(buf, coef_ref, carry, tm, reverse):
    n8 = tm // SUBLANES

    def body(it, c):
        cre, cim = c
        blk = (n8 - 1 - it) if reverse else it
        r0 = pl.multiple_of(blk * SUBLANES, SUBLANES)
        xre = buf[pl.ds(r0, SUBLANES), 0:N_STATE]
        xim = buf[pl.ds(r0, SUBLANES), N_STATE:]
        for idx, s in enumerate((1, 2, 4)):
            sh = (SUBLANES - s) if reverse else s
            sre = pltpu.roll(xre, sh, 0)
            sim = pltpu.roll(xim, sh, 0)
            are = coef_ref[2 * idx]
            aim = coef_ref[2 * idx + 1]
            xre, xim = xre + are * sre - aim * sim, xim + are * sim + aim * sre
        pre = coef_ref[6]
        pim = coef_ref[7]
        hre = xre + pre * cre - pim * cim
        him = xim + pre * cim + pim * cre
        buf[pl.ds(r0, SUBLANES), 0:N_STATE] = hre
        buf[pl.ds(r0, SUBLANES), N_STATE:] = him
        row = 0 if reverse else SUBLANES - 1
        return (jnp.broadcast_to(hre[row:row + 1], (SUBLANES, N_STATE)),
                jnp.broadcast_to(him[row:row + 1], (SUBLANES, N_STATE)))

    cre, cim = lax.fori_loop(0, n8, body, (carry[:, 0:N_STATE], carry[:, N_STATE:]))
    carry[:, 0:N_STATE] = cre
    carry[:, N_STATE:] = cim


def _real_scan(abuf, bbuf, carry, tm, reverse):
    n8 = tm // SUBLANES
    width = bbuf.shape[1]

    def body(it, c):
        blk = (n8 - 1 - it) if reverse else it
        r0 = pl.multiple_of(blk * SUBLANES, SUBLANES)
        a = abuf[pl.ds(r0, SUBLANES), :]
        b = bbuf[pl.ds(r0, SUBLANES), :]
        rows = lax.broadcasted_iota(jnp.int32, (SUBLANES, width), 0)
        for s in (1, 2, 4):
            sh = (SUBLANES - s) if reverse else s
            keep = (rows + s <= SUBLANES - 1) if reverse else (rows >= s)
            sa = pltpu.roll(a, sh, 0)
            sb = pltpu.roll(b, sh, 0)
            b = b + a * jnp.where(keep, sb, 0.0)
            a = a * jnp.where(keep, sa, 1.0)
        h = b + a * c
        bbuf[pl.ds(r0, SUBLANES), :] = h
        row = 0 if reverse else SUBLANES - 1
        return jnp.broadcast_to(h[row:row + 1], (SUBLANES, width))

    carry[...] = lax.fori_loop(0, n8, body, carry[...])


def _dot(a, b, dims):
    return lax.dot_general(a, b, (dims, ((), ())), preferred_element_type=F32)


TS5 = 512
HALO16 = 16


def _s5_fwd(proj, bmat, coef, cmat, dvec, gw, gb):
    t = proj.shape[0]
    tm = _row_tile(t, TS5)

    def kern(u_ref, b_ref, coef_ref, c_ref, d_ref, gw_ref, gb_ref, h_ref, y_ref, hbuf, carry):
        @pl.when(pl.program_id(0) == 0)
        def _():
            carry[...] = jnp.zeros_like(carry)

        u = u_ref[...]
        hbuf[...] = _dot(u.astype(BF16), b_ref[...], NN)
        _s5_scan(hbuf, coef_ref, carry, tm, False)
        hb = hbuf[...].astype(BF16)
        h_ref[...] = hb
        y = _dot(hb, c_ref[...], NN) + d_ref[...] * u
        ys = _gelu(y)
        z = _dot(ys.astype(BF16), gw_ref[...], NN) + gb_ref[...]
        y_ref[...] = ys * _sigmoid(z)

    full = lambda shp: pl.BlockSpec(shp, lambda i: (0,) * len(shp))
    return pl.pallas_call(
        kern, name="s5_fwd", grid=(t // tm,),
        in_specs=[pl.BlockSpec((tm, D_S5), lambda i: (i, 0)), full((D_S5, 2 * N_STATE)),
                  full((8, SUBLANES, N_STATE)), full((2 * N_STATE, D_S5)), full((1, D_S5)),
                  full((D_S5, D_S5)), full((1, D_S5))],
        out_specs=[pl.BlockSpec((tm, 2 * N_STATE), lambda i: (i, 0)), pl.BlockSpec((tm, D_S5), lambda i: (i, 0))],
        out_shape=[jax.ShapeDtypeStruct((t, 2 * N_STATE), BF16), jax.ShapeDtypeStruct((t, D_S5), F32)],
        scratch_shapes=[pltpu.VMEM((tm, 2 * N_STATE), F32), pltpu.VMEM((SUBLANES, 2 * N_STATE), F32)],
        compiler_params=_params(("arbitrary",)),
    )(proj, bmat, coef, cmat, dvec, gw, gb)


def _s5_bwd(proj, h, dout, bmat, coef_b, cmat, dvec, gw, gb):
    t = proj.shape[0]
    tm = _row_tile(t, TS5)
    nt = t // tm
    rb = lambda i: nt - 1 - i

    def kern(u_ref, h_ref, hp_ref, d_ref, b_ref, coef_ref, c_ref, dv_ref, gw_ref, gb_ref,
             du_ref, dc_ref, db_ref, da_ref, dd_ref, dgw_ref, dgb_ref, gbuf, hext, carry):
        i = pl.program_id(0)

        @pl.when(i == 0)
        def _():
            carry[...] = jnp.zeros_like(carry)
            for r in (dc_ref, db_ref, da_ref, dd_ref, dgw_ref, dgb_ref):
                r[...] = jnp.zeros_like(r)

        u = u_ref[...]
        hb = h_ref[...]
        y = _dot(hb, c_ref[...], NN) + dv_ref[...] * u
        ys = _gelu(y)
        ysb = ys.astype(BF16)
        sg = _sigmoid(_dot(ysb, gw_ref[...], NN) + gb_ref[...])
        d_o = d_ref[...]
        dz = d_o * ys * sg * (1.0 - sg)
        dzb = dz.astype(BF16)
        dys = d_o * sg + _dot(dzb, gw_ref[...], NT)
        dgw_ref[...] += _dot(ysb, dzb, TN)
        dgb_ref[...] += jnp.sum(dz, axis=0, keepdims=True)
        dy = dys * _gelu_grad(y)
        dd_ref[...] += jnp.sum(dy * u, axis=0, keepdims=True)
        dyb = dy.astype(BF16)
        dc_ref[...] += _dot(hb, dyb, TN)
        gbuf[...] = _dot(dyb, c_ref[...], NT)
        _s5_scan(gbuf, coef_ref, carry, tm, True)
        g = gbuf[...]
        first = jnp.where(i < nt - 1, hp_ref[HALO16 - 1:HALO16, :].astype(F32), 0.0)
        hext[SUBLANES - 1:SUBLANES, :] = first
        hext[SUBLANES:, :] = hb.astype(F32)
        hprev = hext[pl.ds(SUBLANES - 1, tm), :]
        gre, gim = g[:, 0:N_STATE], g[:, N_STATE:]
        pre, pim = hprev[:, 0:N_STATE], hprev[:, N_STATE:]
        da_ref[0:1, :] += jnp.sum(gre * pre + gim * pim, axis=0, keepdims=True)
        da_ref[1:2, :] += jnp.sum(gim * pre - gre * pim, axis=0, keepdims=True)
        gb16 = g.astype(BF16)
        db_ref[...] += _dot(u.astype(BF16), gb16, TN)
        du_ref[...] = dy * dv_ref[...] + _dot(gb16, b_ref[...], NT)

    full = lambda shp: pl.BlockSpec(shp, lambda i: (0,) * len(shp))
    shape = lambda shp: jax.ShapeDtypeStruct(shp, F32)
    return pl.pallas_call(
        kern, name="s5_bwd", grid=(nt,),
        in_specs=[pl.BlockSpec((tm, D_S5), lambda i: (rb(i), 0)),
                  pl.BlockSpec((tm, 2 * N_STATE), lambda i: (rb(i), 0)),
                  pl.BlockSpec((HALO16, 2 * N_STATE), lambda i: (jnp.maximum(rb(i) * (tm // HALO16) - 1, 0), 0)),
                  pl.BlockSpec((tm, D_S5), lambda i: (rb(i), 0)),
                  full((D_S5, 2 * N_STATE)), full((8, SUBLANES, N_STATE)), full((2 * N_STATE, D_S5)),
                  full((1, D_S5)), full((D_S5, D_S5)), full((1, D_S5))],
        out_specs=[pl.BlockSpec((tm, D_S5), lambda i: (rb(i), 0)), full((2 * N_STATE, D_S5)),
                   full((D_S5, 2 * N_STATE)), full((2, N_STATE)), full((1, D_S5)), full((D_S5, D_S5)),
                   full((1, D_S5))],
        out_shape=[shape((t, D_S5)), shape((2 * N_STATE, D_S5)), shape((D_S5, 2 * N_STATE)),
                   shape((2, N_STATE)), shape((1, D_S5)), shape((D_S5, D_S5)), shape((1, D_S5))],
        scratch_shapes=[pltpu.VMEM((tm, 2 * N_STATE), F32), pltpu.VMEM((tm + SUBLANES, 2 * N_STATE), F32),
                        pltpu.VMEM((SUBLANES, 2 * N_STATE), F32)],
        compiler_params=_params(("arbitrary",)),
    )(proj, h, h, dout, bmat, coef_b, cmat, dvec, gw, gb)


def _lru_gates(ext, x_ref, p_ref, cw_ref, cb_ref, wx_ref, bx_ref, wa_ref, ba_ref, ap_ref, first_tile, row0, tm):
    ext[0:HALO, :] = jnp.where(first_tile, 0.0, p_ref[...])
    ext[HALO:, :] = x_ref[...]
    taps = [ext[pl.ds(HALO - (LRU_CONV - 1) + k, tm), :] for k in range(LRU_CONV)]
    xc = cb_ref[...] + sum(cw_ref[k:k + 1, :] * taps[k] for k in range(LRU_CONV))
    xcb = xc.astype(BF16)
    gx = _sigmoid(_dot(xcb, wx_ref[...], NN) + bx_ref[...])
    ga = _sigmoid(_dot(xcb, wa_ref[...], NN) + ba_ref[...])
    z = -ap_ref[...]
    sp = jnp.maximum(z, 0.0) + jnp.log(1.0 + jnp.exp(-jnp.abs(z)))
    log_a = -LRU_C * ga * sp
    a = jnp.exp(log_a)
    tok = row0 + lax.broadcasted_iota(jnp.int32, a.shape, 0)
    is0 = tok == 0
    mult = jnp.where(is0, 1.0, jnp.sqrt(1.0 - jnp.exp(2.0 * log_a)))
    return taps, xc, xcb, gx, ga, sp, a, mult, is0


def _lru_specs(tm, blk_of):
    col = lambda cidx: pl.BlockSpec((tm, D_LRU), lambda i: (blk_of(i), cidx))
    prev = lambda cidx: pl.BlockSpec((HALO, D_LRU), lambda i: (jnp.maximum(blk_of(i) * (tm // HALO) - 1, 0), cidx))
    full = lambda shp: pl.BlockSpec(shp, lambda i: (0,) * len(shp))
    wts = [full((LRU_CONV, D_LRU)), full((1, D_LRU)), full((D_LRU, D_LRU)), full((1, D_LRU)),
           full((D_LRU, D_LRU)), full((1, D_LRU)), full((1, D_LRU))]
    return col, prev, full, wts


def _lru_fwd(proj, cw, cb, wx, bx, wa, ba, ap):
    t = proj.shape[0]
    tm = _row_tile(t, TM)

    def kern(x_ref, p_ref, g_ref, cw_ref, cb_ref, wx_ref, bx_ref, wa_ref, ba_ref, ap_ref,
             y_ref, h_ref, ext, abuf, carry):
        i = pl.program_id(0)

        @pl.when(i == 0)
        def _():
            carry[...] = jnp.zeros_like(carry)

        _, xc, _, gx, _, _, a, mult, _ = _lru_gates(ext, x_ref, p_ref, cw_ref, cb_ref, wx_ref, bx_ref, wa_ref,
                                                    ba_ref, ap_ref, i == 0, i * tm, tm)
        abuf[...] = a
        h_ref[...] = mult * gx * xc
        _real_scan(abuf, h_ref, carry, tm, False)
        y_ref[...] = h_ref[...] * _gelu(g_ref[...])

    col, prev, full, wts = _lru_specs(tm, lambda i: i)
    out = pl.BlockSpec((tm, D_LRU), lambda i: (i, 0))
    return pl.pallas_call(
        kern, name="lru_fwd", grid=(t // tm,),
        in_specs=[col(1), prev(1), col(2)] + wts, out_specs=[out, out],
        out_shape=[jax.ShapeDtypeStruct((t, D_LRU), F32), jax.ShapeDtypeStruct((t, D_LRU), F32)],
        scratch_shapes=[pltpu.VMEM((tm + HALO, D_LRU), F32), pltpu.VMEM((tm, D_LRU), F32),
                        pltpu.VMEM((SUBLANES, D_LRU), F32)],
        compiler_params=_params(("arbitrary",)),
    )(proj, proj, proj, cw, cb, wx, bx, wa, ba, ap)


def _lru_bwd(proj, h, dout, cw, cb, wx, bx, wa, ba, ap):
    t = proj.shape[0]
    tm = _row_tile(t, TM)
    nt = t // tm
    rb = lambda i: nt - 1 - i

    def kern(x_ref, p_ref, g_ref, h_ref, hp_ref, d_ref, cw_ref, cb_ref, wx_ref, bx_ref, wa_ref, ba_ref, ap_ref,
             dxc_ref, dg_ref, dcw_ref, dcb_ref, dwx_ref, dbx_ref, dwa_ref, dba_ref, dap_ref,
             ext, aext, abuf, gbuf, carry, acarry):
        i = pl.program_id(0)
        blk = nt - 1 - i

        @pl.when(i == 0)
        def _():
            carry[...] = jnp.zeros_like(carry)
            acarry[...] = jnp.zeros_like(acarry)
            for r in (dcw_ref, dcb_ref, dwx_ref, dbx_ref, dwa_ref, dba_ref, dap_ref):
                r[...] = jnp.zeros_like(r)

        taps, xc, xcb, gx, ga, sp, a, mult, is0 = _lru_gates(
            ext, x_ref, p_ref, cw_ref, cb_ref, wx_ref, bx_ref, wa_ref, ba_ref, ap_ref, blk == 0, blk * tm, tm)
        gate = g_ref[...]
        d_o = d_ref[...]
        hcur = h_ref[...]
        dg_ref[...] = d_o * hcur * _gelu_grad(gate)
        aext[0:tm, :] = a
        aext[tm:, :] = acarry[...]
        abuf[...] = aext[pl.ds(1, tm), :]
        gbuf[...] = d_o * _gelu(gate)
        _real_scan(abuf, gbuf, carry, tm, True)
        acarry[...] = jnp.broadcast_to(a[0:1], acarry.shape)
        g = gbuf[...]
        ext[0:HALO, :] = jnp.where(blk == 0, 0.0, hp_ref[...])
        ext[HALO:, :] = hcur
        hprev = ext[pl.ds(HALO - 1, tm), :]
        dmult = jnp.where(is0, 0.0, g * gx * xc)
        dgx = g * mult * xc
        dxc = g * mult * gx
        dlog_a = g * hprev * a - dmult * (a * a) / mult
        dga = dlog_a * (-LRU_C * sp)
        dsp = jnp.sum(dlog_a * (-LRU_C * ga), axis=0, keepdims=True)
        dap_ref[...] += dsp * (-_sigmoid(-ap_ref[...]))
        dpa = (dga * ga * (1.0 - ga))
        dpx = (dgx * gx * (1.0 - gx))
        dpab, dpxb = dpa.astype(BF16), dpx.astype(BF16)
        dwx_ref[...] += _dot(xcb, dpxb, TN)
        dwa_ref[...] += _dot(xcb, dpab, TN)
        dbx_ref[...] += jnp.sum(dpx, axis=0, keepdims=True)
        dba_ref[...] += jnp.sum(dpa, axis=0, keepdims=True)
        dxc = dxc + _dot(dpxb, wx_ref[...], NT) + _dot(dpab, wa_ref[...], NT)
        dxc_ref[...] = dxc
        dcb_ref[...] += jnp.sum(dxc, axis=0, keepdims=True)
        for k in range(LRU_CONV):
            dcw_ref[k:k + 1, :] += jnp.sum(dxc * taps[k], axis=0, keepdims=True)

    col, prev, full, wts = _lru_specs(tm, rb)
    row = pl.BlockSpec((tm, D_LRU), lambda i: (rb(i), 0))
    hprev_spec = pl.BlockSpec((HALO, D_LRU), lambda i: (jnp.maximum(rb(i) * (tm // HALO) - 1, 0), 0))
    shape = lambda shp: jax.ShapeDtypeStruct(shp, F32)
    vec = (1, D_LRU)
    sq = (D_LRU, D_LRU)
    return pl.pallas_call(
        kern, name="lru_bwd", grid=(nt,),
        in_specs=[col(1), prev(1), col(2), row, hprev_spec, row] + wts,
        out_specs=[row, row, full((LRU_CONV, D_LRU)), full(vec), full(sq), full(vec), full(sq), full(vec), full(vec)],
        out_shape=[shape((t, D_LRU)), shape((t, D_LRU)), shape((LRU_CONV, D_LRU)), shape(vec), shape(sq),
                   shape(vec), shape(sq), shape(vec), shape(vec)],
        scratch_shapes=[pltpu.VMEM((tm + HALO, D_LRU), F32), pltpu.VMEM((tm + HALO, D_LRU), F32),
                        pltpu.VMEM((tm, D_LRU), F32), pltpu.VMEM((tm, D_LRU), F32),
                        pltpu.VMEM((SUBLANES, D_LRU), F32), pltpu.VMEM((SUBLANES, D_LRU), F32)],
        compiler_params=_params(("arbitrary",)),
    )(proj, proj, proj, h, h, dout, cw, cb, wx, bx, wa, ba, ap)


def _assemble_dproj(dq, dk, dv, du, dxc, dgate, cos, sin_s, cw):
    t = dq.shape[0]
    tm = _row_tile(t, TM)
    nt = t // tm

    def kern(dq_ref, dk_ref, dv_ref, du_ref, dx_ref, dn_ref, dg_ref, c_ref, s_ref, cw_ref, o_ref, b_ref, ext):
        i = pl.program_id(0)

        @pl.when(i == 0)
        def _():
            b_ref[...] = jnp.zeros_like(b_ref)

        def put(lo, val):
            hi = lo + val.shape[1]
            o_ref[:, lo:hi] = val.astype(BF16)
            b_ref[:, lo:hi] += jnp.sum(val, axis=0, keepdims=True)

        c = c_ref[...]
        s = s_ref[...]
        for ch in range(4):
            x = dq_ref[:, ch * 128:(ch + 1) * 128] * (HEAD_DIM ** -0.5)
            put(ch * 128, x * c - _rope_swap(x) * s)
        x = dk_ref[...]
        put(512, x * c - _rope_swap(x) * s)
        put(640, dv_ref[...])
        put(768, du_ref[...])
        ext[0:tm, :] = dx_ref[...]
        ext[tm:, :] = jnp.where(i < nt - 1, dn_ref[...], 0.0)
        put(1024, sum(cw_ref[k:k + 1, :] * ext[pl.ds(LRU_CONV - 1 - k, tm), :] for k in range(LRU_CONV)))
        put(1280, dg_ref[...])

    row = lambda w: pl.BlockSpec((tm, w), lambda i: (i, 0))
    nxt = pl.BlockSpec((HALO, D_LRU), lambda i: (jnp.minimum((i + 1) * (tm // HALO), t // HALO - 1), 0))
    return pl.pallas_call(
        kern, name="assemble_dproj", grid=(nt,),
        in_specs=[row(512), row(128), row(128), row(256), row(256), nxt, row(256), row(128), row(128),
                  pl.BlockSpec((LRU_CONV, D_LRU), lambda i: (0, 0))],
        out_specs=[row(D_IN), pl.BlockSpec((1, D_IN), lambda i: (0, 0))],
        out_shape=[jax.ShapeDtypeStruct((t, D_IN), BF16), jax.ShapeDtypeStruct((1, D_IN), F32)],
        scratch_shapes=[pltpu.VMEM((tm + HALO, D_LRU), F32)],
        compiler_params=_params(("arbitrary",)),
    )(dq, dk, dv, du, dxc, dxc, dgate, cos, sin_s, cw)


def _blockdiag_s5(bbar_re, bbar_im, c_re, c_im):
    eye = jnp.eye(S5_GROUPS, dtype=F32)
    b_of = lambda m: jnp.einsum('gpc,gh->gchp', m, eye).reshape(D_S5, N_STATE)
    c_of = lambda m: jnp.einsum('gcp,gh->gphc', m, eye).reshape(N_STATE, D_S5)
    bmat = jnp.concatenate([b_of(bbar_re), b_of(bbar_im)], axis=1)
    cmat = jnp.concatenate([c_of(c_re), -c_of(c_im)], axis=0)
    return bmat, cmat


def _s5_prepare(a_re, a_im, b_re, b_im, c_re, c_im, log_dt):
    lam_re = jnp.minimum(a_re, -1e-4)
    lam_im = a_im
    dt = jnp.exp(log_dt)[:, None]
    decay = jnp.exp(dt * lam_re)
    ang = dt * lam_im
    abar_re = decay * jnp.cos(ang)
    abar_im = decay * jnp.sin(ang)
    den = jnp.square(lam_re) + jnp.square(lam_im)
    nr = abar_re - 1.0
    ni = abar_im
    coef_re = (nr * lam_re + ni * lam_im) / den
    coef_im = (ni * lam_re - nr * lam_im) / den
    bbar_re = coef_re[..., None] * b_re - coef_im[..., None] * b_im
    bbar_im = coef_re[..., None] * b_im + coef_im[..., None] * b_re
    bmat, cmat = _blockdiag_s5(bbar_re, bbar_im, c_re, c_im)
    return abar_re.reshape(N_STATE), abar_im.reshape(N_STATE), bmat, cmat


def _blockdiag_lru(w):
    eye = jnp.eye(LRU_HEADS, dtype=F32)
    return jnp.einsum('hij,hk->hikj', w, eye).reshape(D_LRU, D_LRU)


def _rope_tables(t):
    inv_freq = ROPE_THETA ** (-jnp.arange(0, HEAD_DIM, 2, dtype=F32) / HEAD_DIM)
    ang = jnp.arange(t, dtype=F32)[:, None] * inv_freq[None, :]
    cos, sin = jnp.cos(ang), jnp.sin(ang)
    return jnp.tile(jnp.concatenate([cos, cos], axis=1), (1, 2)), jnp.tile(jnp.concatenate([-sin, sin], axis=1), (1, 2))


def _vec(v):
    return v.reshape(1, -1)


def _layer_weights(p):
    abar_re, abar_im, bmat, cmat = _s5_prepare(p['s5_a_re'], p['s5_a_im'], p['s5_b_re'], p['s5_b_im'],
                                               p['s5_c_re'], p['s5_c_im'], p['s5_log_dt'])
    return dict(
        coef_f=_s5_coefs(abar_re, abar_im, False), coef_b=_s5_coefs(abar_re, abar_im, True),
        bmat=bmat.astype(BF16), cmat=cmat.astype(BF16),
        wx=_blockdiag_lru(p['lru_wx']).astype(BF16), wa=_blockdiag_lru(p['lru_wa']).astype(BF16),
        gw=p['s5_glu_w'].astype(BF16))


def _layer_fwd(x, xb, p, w, cos, sin_s):
    t = x.shape[0]
    tm = _row_tile(t, TM)
    layer = p['layer']
    qkv, uxg = _in_proj(xb, p['w_in'], _vec(p['b_in']), cos, sin_s, layer)
    ya, lse = _attn_fwd(qkv, _vec(p['attn_sinks']))
    h5, ys = _s5_fwd(uxg, w['bmat'], w['coef_f'], w['cmat'], _vec(p['s5_d']), w['gw'], _vec(p['s5_glu_b']))
    lru_w = (p['lru_conv_w'], _vec(p['lru_conv_b']), w['wx'], _vec(p['lru_bx']), w['wa'], _vec(p['lru_ba']),
             _vec(p['lru_a_param']))
    yl, hl = _lru_fwd(uxg, *lru_w)
    mix, x1, x1b, xhat1, rstd1 = _mix_out_ln(ya, ys, yl, _vec(p['mix_norm_g']), p['w_out'], _vec(p['b_out']), x,
                                             _vec(p['ln1_g']), _vec(p['ln1_b']), layer)
    gpre, gconv, up, hmid = _ffn_hidden_fwd(x1b, p['ffn_w_gate'], p['ffn_w_up'], p['ffn_conv_w'], p['ffn_conv_b'],
                                            layer)
    x2, x2b, xhat2, rstd2 = _matmul_ln(
        "ffn_down_ln", hmid, p['ffn_w_down'], jnp.zeros((1, D_MODEL), F32), x1, _vec(p['ln2_g']), _vec(p['ln2_b']),
        a_blk=(N_CHIPS, tm, FF_SH), a_map=lambda i: (0, i, 0), w_blk=(N_CHIPS, FF_SH, D_MODEL), parts=N_CHIPS,
        layer=layer)
    saved = dict(xb=xb, uxg=uxg, qkv=qkv, ya=ya, lse=lse, h5=h5, ys=ys, yl=yl, hl=hl, mix=mix, x1b=x1b, xhat1=xhat1,
                 rstd1=rstd1, gpre=gpre, gconv=gconv, up=up, hmid=hmid, xhat2=xhat2, rstd2=rstd2, lru_w=lru_w)
    return x2, x2b, saved


def _layer_bwd(dr2, dr2b, s, p, w, cos, sin_s, big, below):
    t = dr2.shape[0]
    tk = _row_tile(t, TMM)
    nk = t // tk
    tm = _row_tile(t, TM)
    layer = p['layer']
    big = dict(big)
    g = {}
    dup, dgpre, g['ffn_conv_w'], g['ffn_conv_b'] = _ffn_hidden_bwd(
        dr2b, s['gpre'], s['gconv'], s['up'], p['ffn_w_down'], p['ffn_conv_w'], layer)
    big['ffn_w_down'] = _matmul(
        "d_w_down", s['hmid'], dr2b, a_blk=(None, tk, FF_SH), a_map=lambda i, j, k: (i, k, 0), b_blk=(tk, D_MODEL),
        b_map=lambda i, j, k: (k, 0), out_shape=(DEPTH, N_CHIPS, FF_SH, D_MODEL), o_blk=(None, None, FF_SH, D_MODEL),
        o_map=lambda i, j: (layer, i, 0, 0), grid=(N_CHIPS, 1, nk), dims=TN, into=big['ffn_w_down'])
    d_ffn_w = lambda name, dact, buf: _matmul(
        name, s['x1b'], dact, a_blk=(tk, D_MODEL), a_map=lambda i, j, k: (k, 0), b_blk=(None, tk, FF_SH),
        b_map=lambda i, j, k: (j, k, 0), out_shape=(DEPTH, N_CHIPS, D_MODEL, FF_SH),
        o_blk=(None, None, D_MODEL, FF_SH), o_map=lambda i, j: (layer, j, 0, 0), grid=(1, N_CHIPS, nk), dims=TN,
        into=buf)
    big['ffn_w_gate'] = d_ffn_w("d_w_gate", dgpre, big['ffn_w_gate'])
    big['ffn_w_up'] = d_ffn_w("d_w_up", dup, big['ffn_w_up'])
    wspec = dict(b_blk=(None, None, D_MODEL, FF_SH), b_map=lambda i, j, k: (layer, k, 0, 0))
    dx1 = _matmul(
        "d_x1", dgpre, p['ffn_w_gate'], pair2=(dup, p['ffn_w_up']), a_blk=(None, tk, FF_SH),
        a_map=lambda i, j, k: (k, i, 0), out_shape=(t, D_MODEL), o_blk=(tk, D_MODEL), o_map=lambda i, j: (i, 0),
        grid=(nk, 1, N_CHIPS), dims=NT, add=dr2, add_scale=ALPHA, **wspec)
    dr1, dr1b, g['ln1_g'], g['ln1_b'], g['b_out'], dya, dys, dyl, g['mix_norm_g'] = _d_mix_rms(
        dx1, s['xhat1'], s['rstd1'], _vec(p['ln1_g']), p['w_out'], s['ya'], s['ys'], s['yl'],
        _vec(p['mix_norm_g']), layer)
    big['w_out'] = _matmul(
        "d_w_out", s['mix'], dr1b, a_blk=(tk, D_MODEL), a_map=lambda i, j, k: (k, 0), b_blk=(tk, D_MODEL),
        b_map=lambda i, j, k: (k, 0), out_shape=(DEPTH, D_MODEL, D_MODEL), o_blk=(None, D_MODEL, D_MODEL),
        o_map=lambda i, j: (layer, 0, 0), grid=(1, 1, nk), dims=TN, into=big['w_out'])
    dq, dk, dv, g['attn_sinks'] = _attn_bwd(s['qkv'], s['ya'], dya, s['lse'], _vec(p['attn_sinks']))
    du, dcmat, dbmat, dabar, g['s5_d'], g['s5_glu_w'], g['s5_glu_b'] = _s5_bwd(
        s['uxg'], s['h5'], dys, w['bmat'], w['coef_b'], w['cmat'], _vec(p['s5_d']), w['gw'], _vec(p['s5_glu_b']))
    (dxc, dgate, g['lru_conv_w'], g['lru_conv_b'], dwx, g['lru_bx'], dwa, g['lru_ba'],
     g['lru_a_param']) = _lru_bwd(s['uxg'], s['hl'], dyl, *s['lru_w'])
    dproj, g['b_in'] = _assemble_dproj(dq, dk, dv, du, dxc, dgate, cos, sin_s, p['lru_conv_w'])
    big['w_in'] = _matmul(
        "d_w_in", s['xb'], dproj, a_blk=(tk, D_MODEL), a_map=lambda i, j, k: (k, 0), b_blk=(tk, IN_SH),
        b_map=lambda i, j, k: (k, j), out_shape=(DEPTH, N_CHIPS, D_MODEL, IN_SH), o_blk=(None, None, D_MODEL, IN_SH),
        o_map=lambda i, j: (layer, j, 0, 0), grid=(1, N_CHIPS, nk), dims=TN, into=big['w_in'])
    dx = _matmul("d_x", dproj, p['w_in'], a_blk=(tk, IN_SH), a_map=lambda i, j, k: (i, k),
                 b_blk=(None, None, D_MODEL, IN_SH), b_map=lambda i, j, k: (layer, k, 0, 0), out_shape=(t, D_MODEL),
                 o_blk=(tk, D_MODEL), o_map=lambda i, j: (i, 0), grid=(nk, 1, N_CHIPS), dims=NT,
                 add=dr1, add_scale=ALPHA)
    if below is not None:
        dx = _ln_bwd(dx, below[0]['xhat2'], below[0]['rstd2'], _vec(below[1]['ln2_g']))
    return dx, _param_chain(g, p, dabar, dbmat, dcmat, dwx, dwa), big


def _param_chain(g, p, dabar, dbmat, dcmat, dwx, dwa):
    s5_names = ('s5_a_re', 's5_a_im', 's5_b_re', 's5_b_im', 's5_c_re', 's5_c_im', 's5_log_dt')
    _, s5_vjp = jax.vjp(_s5_prepare, *[p[n] for n in s5_names])
    for n, val in zip(s5_names, s5_vjp((dabar[0], dabar[1], dbmat, dcmat))):
        g[n] = val
    g['lru_wx'] = jax.vjp(_blockdiag_lru, p['lru_wx'])[1](dwx)[0]
    g['lru_wa'] = jax.vjp(_blockdiag_lru, p['lru_wa'])[1](dwa)[0]
    return g


ROW_TILE = 512


def _pick_rows(rows):
    for rt in range(min(rows, ROW_TILE), 0, -1):
        if rows % rt == 0 and (rt % 16 == 0 or rt == rows):
            return rt
    return rows


def _cast_bf16(a):
    a2 = a.reshape(-1, a.shape[-1])
    rows, c = a2.shape
    rt = _pick_rows(rows)

    def kern(a_ref, o_ref):
        o_ref[...] = a_ref[...].astype(BF16)

    spec = pl.BlockSpec((rt, c), lambda i: (i, 0))
    out = pl.pallas_call(kern, name="cast_bf16", grid=(rows // rt,), in_specs=[spec], out_specs=spec,
                         out_shape=jax.ShapeDtypeStruct((rows, c), BF16), compiler_params=_params(("parallel",)))(a2)
    return out.reshape(a.shape)


def _sum_parts(name, parts, shape):
    c = shape[-1]
    rows = math.prod(shape[:-1])
    rt = _pick_rows(rows)
    n = len(parts)

    def kern(*refs):
        acc = refs[0][...].astype(F32)
        for r in refs[1:n]:
            acc = acc + r[...].astype(F32)
        refs[n][...] = acc

    specs, args = [], []
    for arr, j in parts:
        if j is None:
            specs.append(pl.BlockSpec((rt, c), lambda i: (i, 0)))
            args.append(arr.reshape(rows, c))
        else:
            specs.append(pl.BlockSpec((None, rt, c), functools.partial(lambda i, jj: (jj, i, 0), jj=j)))
            args.append(arr.reshape(arr.shape[0], rows, c))
    out = pl.pallas_call(kern, name=name, grid=(rows // rt,), in_specs=specs,
                         out_specs=pl.BlockSpec((rt, c), lambda i: (i, 0)),
                         out_shape=jax.ShapeDtypeStruct((rows, c), F32), compiler_params=_params(("parallel",)))(*args)
    return out.reshape(shape)


def _adamw(name, w, g, m, v):
    shape = w.shape
    c = shape[-1]
    rows = math.prod(shape[:-1])
    rt = _pick_rows(rows)

    def kern(w_ref, g_ref, m_ref, v_ref, d_ref, nm_ref, nv_ref):
        g_ = g_ref[...]
        m_ = ADAM_B1 * m_ref[...] + (1.0 - ADAM_B1) * g_
        v_ = ADAM_B2 * v_ref[...] + (1.0 - ADAM_B2) * jnp.square(g_)
        m_hat = m_ / (1.0 - ADAM_B1 ** ADAM_STEP)
        v_hat = v_ / (1.0 - ADAM_B2 ** ADAM_STEP)
        d_ref[...] = -ADAM_LR * (m_hat / (jnp.sqrt(v_hat) + ADAM_EPS) + ADAM_WD * w_ref[...])
        nm_ref[...] = m_
        nv_ref[...] = v_

    spec = pl.BlockSpec((rt, c), lambda i: (i, 0))
    outs = pl.pallas_call(kern, name=name, grid=(rows // rt,), in_specs=[spec] * 4, out_specs=[spec] * 3,
                          out_shape=[jax.ShapeDtypeStruct((rows, c), F32)] * 3,
                          compiler_params=_params(("parallel",)))(*[a.reshape(rows, c) for a in (w, g, m, v)])
    return tuple(o.reshape(shape) for o in outs)


def _position():
    return lax.axis_index("x"), lax.axis_index("y"), lax.axis_index("c")


def _other_chips(x, y):
    return [(1 - x, y), (x, 1 - y), (1 - x, 1 - y)]


def _exchange(name, arrs, out_shapes, n_local, n_remote, plan):
    n_in, n_out = len(arrs), len(out_shapes)

    def kern(*refs):
        ins, outs = refs[:n_in], refs[n_in:n_in + n_out]
        send, recv, loc = refs[n_in + n_out:]
        local, remote = plan(ins, outs, *_position())
        assert len(local) == n_local and len(remote) == n_remote
        own = [pltpu.make_async_copy(s, d, loc.at[k]) for k, (s, d) in enumerate(local)]
        for cp in own:
            cp.start()
        sent = [pltpu.make_async_remote_copy(src_ref=s, dst_ref=d, send_sem=send.at[k], recv_sem=recv.at[k],
                                             device_id=peer, device_id_type=MESH)
                for k, (s, d, peer, _) in enumerate(remote)]
        for cp in sent:
            cp.start()
        for k, (s, _, peer, landing) in enumerate(remote):
            pltpu.make_async_remote_copy(src_ref=s, dst_ref=landing, send_sem=send.at[k], recv_sem=recv.at[k],
                                         device_id=peer, device_id_type=MESH).wait_recv()
        for cp in sent:
            cp.wait_send()
        for cp in own:
            cp.wait()

    return pl.pallas_call(
        kern, name=name, in_specs=[ANY] * n_in, out_specs=[ANY] * n_out, out_shape=out_shapes,
        scratch_shapes=[pltpu.SemaphoreType.DMA((n_remote,)), pltpu.SemaphoreType.DMA((n_remote,)),
                        pltpu.SemaphoreType.DMA((max(n_local, 1),))],
    )(*arrs)


def _allgather_chips(arrs, halved=()):
    n = len(arrs)
    layers = arrs[0].shape[0]

    def plan(ins, outs, x, y, c):
        me = 2 * x + y
        local, remote = [], []
        for t in range(n):
            for l in range(layers):
                src = ins[t].at[l]
                if t in halved:
                    r2 = ins[t].shape[2] // 2
                    src = ins[t].at[l, :, pl.ds(c * r2, r2)]
                local.append((src, outs[t].at[l, pl.ds(me, 1)]))
                for px, py in _other_chips(x, y):
                    remote.append((src, outs[t].at[l, pl.ds(me, 1)], (px, py, c),
                                   outs[t].at[l, pl.ds(2 * px + py, 1)]))
        return local, remote

    outs = []
    for t, a in enumerate(arrs):
        tail = (a.shape[2] // 2,) + a.shape[3:] if t in halved else a.shape[2:]
        outs.append(jax.ShapeDtypeStruct((a.shape[0], N_CHIPS) + tail, a.dtype))
    return _exchange("allgather_chips", arrs, outs, n * layers, 3 * n * layers, plan)


def _chip_scatter(arrs):
    n = len(arrs)
    layers = arrs[0].shape[0]

    def plan(ins, outs, x, y, c):
        me = 2 * x + y
        local, remote = [], []
        for t in range(n):
            for l in range(layers):
                local.append((ins[t].at[l, pl.ds(me, 1)], outs[2 * t].at[l]))
                for j, (px, py) in enumerate(_other_chips(x, y)):
                    remote.append((ins[t].at[l, pl.ds(2 * px + py, 1)], outs[2 * t + 1].at[j, l], (px, py, c),
                                   outs[2 * t + 1].at[j, l]))
        return local, remote

    outs = []
    for a in arrs:
        one = (a.shape[0], 1) + a.shape[2:]
        outs += [jax.ShapeDtypeStruct(one, a.dtype), jax.ShapeDtypeStruct((3,) + one, a.dtype)]
    return _exchange("chip_scatter", arrs, outs, n * layers, 3 * n * layers, plan)


def _allgather_devices(v):
    def kern(v_ref, o_ref, send, recv, loc):
        x, y, c = _position()
        me, sibling = (x, y, c), (x, y, 1 - c)
        chips = _other_chips(x, y)

        def rows(px, py, pc):
            return o_ref.at[pl.ds(4 * px + 2 * py + pc, 1)]

        def copy(k, block, to, src=None):
            return pltpu.make_async_remote_copy(
                src_ref=rows(*block) if src is None else src, dst_ref=rows(*block), send_sem=send.at[k],
                recv_sem=recv.at[k], device_id=to, device_id_type=MESH)

        mine = pltpu.make_async_copy(v_ref, rows(*me), loc.at[0])
        mine.start()
        first = [copy(0, me, sibling, src=v_ref)]
        first += [copy(1 + j, me, (*chip, c), src=v_ref) for j, chip in enumerate(chips)]
        for cp in first:
            cp.start()
        passed = [copy(4 + j, (*chip, c), sibling) for j, chip in enumerate(chips)]
        for j, chip in enumerate(chips):
            copy(1 + j, (*chip, c), me).wait_recv()
            passed[j].start()
        copy(0, sibling, me).wait_recv()
        for j, chip in enumerate(chips):
            copy(4 + j, (*chip, 1 - c), me).wait_recv()
        for cp in first + passed:
            cp.wait_send()
        mine.wait()

    vmem = pl.BlockSpec(memory_space=pltpu.VMEM)
    return pl.pallas_call(
        kern, name="allgather_devices", in_specs=[vmem], out_specs=vmem,
        out_shape=jax.ShapeDtypeStruct((N_DEV,) + v.shape[1:], v.dtype),
        scratch_shapes=[pltpu.SemaphoreType.DMA((7,)), pltpu.SemaphoreType.DMA((7,)), pltpu.SemaphoreType.DMA((1,))],
        compiler_params=pltpu.CompilerParams(vmem_limit_bytes=VMEM_MB << 20),
    )(v)


WEIGHTS = ['w_in', 'b_in', 'attn_sinks', 's5_a_re', 's5_a_im', 's5_b_re', 's5_b_im', 's5_c_re', 's5_c_im', 's5_d',
           's5_log_dt', 's5_glu_w', 's5_glu_b', 'lru_conv_w', 'lru_conv_b', 'lru_wx', 'lru_bx', 'lru_wa', 'lru_ba',
           'lru_a_param', 'mix_norm_g', 'w_out', 'b_out', 'ln1_g', 'ln1_b', 'ffn_w_gate', 'ffn_w_up', 'ffn_conv_w',
           'ffn_conv_b', 'ffn_w_down', 'ln2_g', 'ln2_b']
BIG = ('w_in', 'w_out', 'ffn_w_gate', 'ffn_w_up', 'ffn_w_down')
SMALL = tuple(n for n in WEIGHTS if n not in BIG)
PACK_ROWS = ROW_TILE


def _pack(arrs):
    flat = jnp.concatenate([a.reshape(-1) for a in arrs])
    unit = 128 * PACK_ROWS
    size = -(-flat.shape[0] // unit) * unit
    return jnp.pad(flat, (0, size - flat.shape[0])).reshape(-1, 128)


def _unpack(packed, shapes):
    flat = packed.reshape(-1)
    out, pos = [], 0
    for shp in shapes:
        n = math.prod(shp)
        out.append(flat[pos:pos + n].reshape(shp))
        pos += n
    return out


def _pair_reduce(name, g):
    layers, shards, rows, cols = g.shape
    r2 = rows // 2
    rt = _pick_rows(r2)
    nr = r2 // rt
    nsteps = layers * shards * nr

    def kern(c_ref, mine_ref, other_ref, o_ref, buf, send, recv, credit):
        x, y, c = _position()
        sibling = (x, y, 1 - c)
        k = pl.program_id(0) * nr + pl.program_id(1)
        slot = k % 2

        @pl.when(k >= 2)
        def _():
            pl.semaphore_wait(credit, 1)

        cp = pltpu.make_async_remote_copy(src_ref=other_ref, dst_ref=buf.at[slot], send_sem=send.at[slot],
                                          recv_sem=recv.at[slot], device_id=sibling, device_id_type=MESH)
        cp.start()
        cp.wait_recv()
        o_ref[...] = (mine_ref[...] + buf[slot]).astype(BF16)
        cp.wait_send()

        @pl.when(k + 2 < nsteps)
        def _():
            pl.semaphore_signal(credit, 1, device_id=sibling, device_id_type=MESH)

    blk = (1, rt, cols)
    grid_spec = pltpu.PrefetchScalarGridSpec(
        num_scalar_prefetch=1, grid=(layers * shards, nr),
        in_specs=[pl.BlockSpec(blk, lambda m, r, c_ref: (m, c_ref[0] * nr + r, 0)),
                  pl.BlockSpec(blk, lambda m, r, c_ref: (m, (1 - c_ref[0]) * nr + r, 0))],
        out_specs=pl.BlockSpec(blk, lambda m, r, c_ref: (m, r, 0)),
        scratch_shapes=[pltpu.VMEM((2,) + blk, F32), pltpu.SemaphoreType.DMA((2,)),
                        pltpu.SemaphoreType.DMA((2,)), pltpu.SemaphoreType.REGULAR])
    core = lax.axis_index("c").astype(jnp.int32).reshape(1)
    g3 = g.reshape(layers * shards, rows, cols)
    out = pl.pallas_call(
        kern, name=name, grid_spec=grid_spec,
        out_shape=jax.ShapeDtypeStruct((layers * shards, r2, cols), BF16),
        compiler_params=_params(("arbitrary", "arbitrary")),
    )(core, g3, g3)
    return out.reshape(layers, shards, r2, cols)


def _pair_merge(name, h):
    m, r2, cols = h.shape
    rt = _pick_rows(r2)
    nr = r2 // rt
    nsteps = m * nr

    def kern(h_ref, o_ref, buf, send, recv, credit):
        x, y, c = _position()
        sibling = (x, y, 1 - c)
        k = pl.program_id(0) * nr + pl.program_id(1)
        slot = k % 2

        @pl.when(k >= 2)
        def _():
            pl.semaphore_wait(credit, 1)

        cp = pltpu.make_async_remote_copy(src_ref=h_ref, dst_ref=buf.at[slot], send_sem=send.at[slot],
                                          recv_sem=recv.at[slot], device_id=sibling, device_id_type=MESH)
        cp.start()
        cp.wait_recv()
        o_ref[0, pl.ds(c, 1)] = h_ref[...]
        o_ref[0, pl.ds(1 - c, 1)] = buf[slot]
        cp.wait_send()

        @pl.when(k + 2 < nsteps)
        def _():
            pl.semaphore_signal(credit, 1, device_id=sibling, device_id_type=MESH)

    blk = (1, rt, cols)
    out = pl.pallas_call(
        kern, name=name, grid=(m, nr),
        in_specs=[pl.BlockSpec(blk, lambda i, r: (i, r, 0))],
        out_specs=pl.BlockSpec((1, 2, rt, cols), lambda i, r: (i, 0, r, 0)),
        out_shape=jax.ShapeDtypeStruct((m, 2, r2, cols), h.dtype),
        scratch_shapes=[pltpu.VMEM((2,) + blk, h.dtype), pltpu.SemaphoreType.DMA((2,)),
                        pltpu.SemaphoreType.DMA((2,)), pltpu.SemaphoreType.REGULAR],
        compiler_params=_params(("arbitrary", "arbitrary")),
    )(h)
    return out.reshape(m, 2 * r2, cols)


def _reduce_big(grads):
    pair = [_pair_reduce("pair_reduce_" + n, g) for n, g in zip(BIG, grads)]
    scat = _chip_scatter(pair)
    out = []
    for t, n in enumerate(BIG):
        own, got = scat[2 * t], scat[2 * t + 1]
        half = _sum_parts("chip_sum", [(own, None)] + [(got, j) for j in range(3)], own.shape)
        out.append(_pair_merge("grad_merge_" + n, half.reshape(half.shape[0], half.shape[2], half.shape[3])))
    return out


def _step(a):
    x = a['x'][0]
    target = a['loss_target'][0]
    t = x.shape[0]
    xi, yi, _ = _position()
    chip = 2 * xi + yi
    cos, sin_s = _rope_tables(t)

    gathered = _allgather_chips([_cast_bf16(a[n])[:, None] for n in BIG]
                                + [a[n][:, None] for n in ('s5_glu_w', 'lru_conv_w', 'ffn_conv_w')],
                                halved=range(len(BIG)))
    full = dict(zip(BIG + ('s5_glu_w', 'lru_conv_w', 'ffn_conv_w'), gathered))
    for n in BIG:
        layers, chips, r2, cols = full[n].shape
        full[n] = _pair_merge("weight_merge_" + n, full[n].reshape(layers * chips, r2, cols)).reshape(
            layers, chips, 2 * r2, cols)

    def layer_params(l):
        p = {n: a[n][l] for n in SMALL}
        p['layer'] = l
        p['w_in'] = full['w_in']
        p['w_out'] = full['w_out'].reshape(DEPTH, D_MODEL, D_MODEL)
        p['ffn_w_gate'] = full['ffn_w_gate']
        p['ffn_w_up'] = full['ffn_w_up']
        p['ffn_w_down'] = full['ffn_w_down']
        p['s5_glu_w'] = full['s5_glu_w'][l].reshape(D_S5, D_S5)
        p['lru_conv_w'] = full['lru_conv_w'][l].transpose(1, 0, 2).reshape(LRU_CONV, D_LRU)
        p['ffn_conv_w'] = full['ffn_conv_w'][l]
        p['ffn_conv_b'] = a['ffn_conv_b'][l].reshape(N_CHIPS, 1, FF_SH)
        return p

    params = [layer_params(l) for l in range(DEPTH)]
    derived = [_layer_weights(p) for p in params]
    saved = []
    h, hb = x, _cast_bf16(x)
    for l in range(DEPTH):
        h, hb, s = _layer_fwd(h, hb, params[l], derived[l], cos, sin_s)
        saved.append(s)
    loss_part, dr, drb, ln2_g, ln2_b, _ = _loss_head(h, target, saved[-1]['xhat2'], saved[-1]['rstd2'],
                                                     _vec(params[-1]['ln2_g']))
    loss = lax.psum(loss_part[0, 0], ("x", "y", "c"))
    grads = [None] * DEPTH
    big = {n: lax.empty((DEPTH, N_CHIPS) + a[n].shape[1:], F32) for n in BIG}
    big['w_out'] = big['w_out'].reshape(DEPTH, D_MODEL, D_MODEL)
    for l in reversed(range(DEPTH)):
        below = (saved[l - 1], params[l - 1]) if l > 0 else None
        out, grads[l], big = _layer_bwd(dr, drb, saved[l], params[l], derived[l], cos, sin_s, big, below)
        grads[l]['ln2_g'], grads[l]['ln2_b'] = ln2_g, ln2_b
        if l > 0:
            dr, drb, ln2_g, ln2_b, _ = out
        else:
            grad_x = out[None]

    def stacked(n):
        return jnp.stack([grads[l][n] for l in range(DEPTH)])

    big['w_out'] = big['w_out'].reshape(DEPTH, N_CHIPS, OUT_SH, D_MODEL)
    grad = dict(zip(BIG, _reduce_big([big[n] for n in BIG])))
    small_local = [stacked(n) for n in SMALL]
    packed = _allgather_devices(_pack(small_local)[None])
    total = _sum_parts("device_sum", [(packed, j) for j in range(N_DEV)], packed.shape[1:])
    small_sum = dict(zip(SMALL, _unpack(total, [g.shape for g in small_local])))
    for n in SMALL:
        g = small_sum[n]
        if n == 's5_glu_w':
            g = lax.dynamic_slice_in_dim(g, chip * (D_S5 // N_CHIPS), D_S5 // N_CHIPS, axis=1)
        elif n == 'lru_conv_w':
            g = lax.dynamic_slice_in_dim(g, chip * (D_LRU // N_CHIPS), D_LRU // N_CHIPS, axis=2)
        elif n == 'ffn_conv_w':
            g = lax.dynamic_index_in_dim(g, chip, axis=1, keepdims=False)
        grad[n] = g.reshape(a[n].shape)

    delta, new_m, new_v = {}, {}, {}
    for n in WEIGHTS:
        delta[n], new_m[n], new_v[n] = _adamw("adamw_" + n, a[n], grad[n], a['m_' + n], a['v_' + n])
    return (loss, grad_x, *[grad[n] for n in WEIGHTS], *[delta[n] for n in WEIGHTS],
            *[new_m[n] for n in WEIGHTS], *[new_v[n] for n in WEIGHTS])


def kernel(x, w_in, b_in, attn_sinks, s5_a_re, s5_a_im, s5_b_re, s5_b_im, s5_c_re, s5_c_im, s5_d, s5_log_dt, s5_glu_w, s5_glu_b, lru_conv_w, lru_conv_b, lru_wx, lru_bx, lru_wa, lru_ba, lru_a_param, mix_norm_g, w_out, b_out, ln1_g, ln1_b, ffn_w_gate, ffn_w_up, ffn_conv_w, ffn_conv_b, ffn_w_down, ln2_g, ln2_b, loss_target, m_w_in, m_b_in, m_attn_sinks, m_s5_a_re, m_s5_a_im, m_s5_b_re, m_s5_b_im, m_s5_c_re, m_s5_c_im, m_s5_d, m_s5_log_dt, m_s5_glu_w, m_s5_glu_b, m_lru_conv_w, m_lru_conv_b, m_lru_wx, m_lru_bx, m_lru_wa, m_lru_ba, m_lru_a_param, m_mix_norm_g, m_w_out, m_b_out, m_ln1_g, m_ln1_b, m_ffn_w_gate, m_ffn_w_up, m_ffn_conv_w, m_ffn_conv_b, m_ffn_w_down, m_ln2_g, m_ln2_b, v_w_in, v_b_in, v_attn_sinks, v_s5_a_re, v_s5_a_im, v_s5_b_re, v_s5_b_im, v_s5_c_re, v_s5_c_im, v_s5_d, v_s5_log_dt, v_s5_glu_w, v_s5_glu_b, v_lru_conv_w, v_lru_conv_b, v_lru_wx, v_lru_bx, v_lru_wa, v_lru_ba, v_lru_a_param, v_mix_norm_g, v_w_out, v_b_out, v_ln1_g, v_ln1_b, v_ffn_w_gate, v_ffn_w_up, v_ffn_conv_w, v_ffn_conv_b, v_ffn_w_down, v_ln2_g, v_ln2_b):
    return _step(dict(locals()))
```

```python
import functools
import math

import jax
import jax.numpy as jnp
from jax import lax
from jax.experimental import pallas as pl
from jax.experimental.pallas import tpu as pltpu

F32 = jnp.float32
BF16 = jnp.bfloat16
MESH = pl.DeviceIdType.MESH
ANY = pl.BlockSpec(memory_space=pl.ANY)

D_MODEL = 1024
DEPTH = 4
HEAD_DIM = 64
N_Q_HEADS = 8
N_KV_HEADS = 2
Q_PER_KV = 4
D_ATTN = 512
D_KV = 128
ATTN_BLOCK = 128
ROPE_THETA = 10000.0
D_S5 = 256
S5_GROUP = 16
S5_GROUPS = 16
S5_STATE = 64
N_STATE = S5_GROUPS * S5_STATE
D_LRU = 256
LRU_HEADS = 4
LRU_HEAD_DIM = 64
LRU_CONV = 4
LRU_C = 8.0
D_IN = 1536
D_FF = 2816
FFN_CONV = 3
N_CHIPS = 4
N_DEV = 8
IN_SH = D_IN // N_CHIPS
FF_SH = D_FF // N_CHIPS
OUT_SH = D_MODEL // N_CHIPS
ALPHA = (2 * DEPTH) ** 0.25
LN_EPS = 1e-5
RMS_EPS = 1e-6
ADAM_LR = 0.001
ADAM_B1 = 0.9
ADAM_B2 = 0.999
ADAM_EPS = 1e-08
ADAM_WD = 0.01
ADAM_STEP = 10

SUBLANES = 8
VMEM_MB = 56


def _params(sem):
    return pltpu.CompilerParams(dimension_semantics=sem, vmem_limit_bytes=VMEM_MB << 20)


def _row_tile(t, pref):
    return min(t, pref)


def _matmul(name, a, b, *, a_blk, a_map, b_blk, b_map, out_shape, o_blk, o_map, grid, dims,
            out_dtype=F32, bias=None, bias_blk=None, bias_map=None, add=None, add_scale=1.0, pair2=None,
            into=None, ln_bwd=None):
    nk = grid[2]
    acc_shape = tuple(d for d in o_blk if d is not None)
    n_in = 2 if pair2 is None else 4

    def kern(*refs):
        p = n_in
        bias_ref = add_ref = None
        if bias is not None:
            bias_ref = refs[p]
            p += 1
        if add is not None:
            add_ref = refs[p]
            p += 1
        if into is not None:
            p += 1
        if ln_bwd is not None:
            ln_in = refs[p:p + 3]
            ln_out = refs[p + 4:p + 8]
            o_ref, acc = refs[p + 3], refs[p + 8]
        else:
            o_ref, acc = refs[p], refs[p + 1]
        k = pl.program_id(2)
        first_tile = pl.program_id(0) == 0

        def product():
            r = _dot(refs[0][...].astype(BF16), refs[1][...].astype(BF16), dims)
            if pair2 is not None:
                r = r + _dot(refs[2][...].astype(BF16), refs[3][...].astype(BF16), dims)
            return r

        def finish(r):
            if bias_ref is not None:
                r = r + bias_ref[...]
            if add_ref is not None:
                r = r + add_scale * add_ref[...]
            if ln_bwd is None:
                o_ref[...] = r.astype(out_dtype)
            else:
                @pl.when(first_tile)
                def _():
                    for ref in ln_out[1:]:
                        ref[...] = jnp.zeros_like(ref)

                _ln_bwd_tile(r, ln_in[0][...], ln_in[1][...], ln_in[2][...], o_ref, *ln_out)

        if nk == 1:
            finish(product())
        else:
            @pl.when(k == 0)
            def _():
                acc[...] = jnp.zeros_like(acc)

            acc[...] += product()

            @pl.when(k == nk - 1)
            def _():
                finish(acc[...])

    in_specs = [pl.BlockSpec(a_blk, a_map), pl.BlockSpec(b_blk, b_map)]
    args = [a, b]
    if pair2 is not None:
        in_specs += [pl.BlockSpec(a_blk, a_map), pl.BlockSpec(b_blk, b_map)]
        args += list(pair2)
    if bias is not None:
        in_specs.append(pl.BlockSpec(bias_blk, bias_map))
        args.append(bias)
    if add is not None:
        in_specs.append(pl.BlockSpec(o_blk, lambda i, j, k: o_map(i, j)))
        args.append(add)
    aliases = {}
    if into is not None:
        aliases = {len(args): 0}
        in_specs.append(ANY)
        args.append(into)
    o_spec = pl.BlockSpec(o_blk, lambda i, j, k: o_map(i, j))
    out_specs, out_shapes = o_spec, jax.ShapeDtypeStruct(out_shape, out_dtype)
    semantics = ("parallel", "parallel", "arbitrary")
    if ln_bwd is not None:
        vec = pl.BlockSpec((1, o_blk[-1]), lambda i, j, k: (0, 0))
        in_specs += [o_spec, pl.BlockSpec((o_blk[0], 1), lambda i, j, k: (i, 0)), vec]
        args += list(ln_bwd)
        vshape = jax.ShapeDtypeStruct((1, o_blk[-1]), F32)
        out_specs = [o_spec, o_spec, vec, vec, vec]
        out_shapes = [out_shapes, jax.ShapeDtypeStruct(out_shape, BF16), vshape, vshape, vshape]
        semantics = ("arbitrary", "arbitrary", "arbitrary")
    return pl.pallas_call(
        kern, name=name, grid=grid, in_specs=in_specs, out_specs=out_specs, out_shape=out_shapes,
        scratch_shapes=[pltpu.VMEM(acc_shape if nk > 1 else (SUBLANES, 128), F32)],
        input_output_aliases=aliases,
        compiler_params=_params(semantics),
    )(*args)


NN = ((1,), (0,))
NT = ((1,), (1,))
TN = ((0,), (0,))
TM = 512


def _sigmoid(x):
    return 0.5 * jnp.tanh(0.5 * x) + 0.5


_GELU_C = math.sqrt(2.0 / math.pi)


def _gelu(x):
    return 0.5 * x * (1.0 + jnp.tanh(_GELU_C * (x + 0.044715 * x * x * x)))


def _gelu_grad(x):
    th = jnp.tanh(_GELU_C * (x + 0.044715 * x * x * x))
    return 0.5 * (1.0 + th) + 0.5 * x * (1.0 - th * th) * _GELU_C * (1.0 + 3 * 0.044715 * x * x)


def _rope_swap(t):
    lane = lax.broadcasted_iota(jnp.int32, t.shape, 1)
    lo = (lane % HEAD_DIM) < (HEAD_DIM // 2)
    return jnp.where(lo, pltpu.roll(t, 128 - HEAD_DIM // 2, 1), pltpu.roll(t, HEAD_DIM // 2, 1))


D_QKV = D_ATTN + 2 * D_KV
TMM = 1024


def _in_proj(xb, w_in, b_in, cos, sin_s, layer):
    t = xb.shape[0]
    tm = _row_tile(t, TMM)

    def kern(x_ref, w_ref, b_ref, c_ref, s_ref, q_ref, u_ref):
        x = x_ref[...]
        c = c_ref[...]
        s = s_ref[...]
        for j in range(N_CHIPS):
            pj = _dot(x, w_ref[j], NN) + b_ref[:, j * IN_SH:(j + 1) * IN_SH]
            for ch in range(IN_SH // 128):
                col = j * IN_SH + ch * 128
                v = pj[:, ch * 128:(ch + 1) * 128]
                if col < D_ATTN + D_KV:
                    v = v * c + _rope_swap(v) * s
                if col < D_ATTN:
                    v = v * (HEAD_DIM ** -0.5)
                if col < D_QKV:
                    q_ref[:, col:col + 128] = v.astype(BF16)
                else:
                    u_ref[:, col - D_QKV:col - D_QKV + 128] = v

    row = lambda w: pl.BlockSpec((tm, w), lambda i: (i, 0))
    return pl.pallas_call(
        kern, name="in_proj", grid=(t // tm,),
        in_specs=[row(D_MODEL), pl.BlockSpec((None, N_CHIPS, D_MODEL, IN_SH), lambda i: (layer, 0, 0, 0)),
                  pl.BlockSpec((1, D_IN), lambda i: (0, 0)), row(128), row(128)],
        out_specs=[row(D_QKV), row(D_IN - D_QKV)],
        out_shape=[jax.ShapeDtypeStruct((t, D_QKV), BF16), jax.ShapeDtypeStruct((t, D_IN - D_QKV), F32)],
        compiler_params=_params(("parallel",)),
    )(xb, w_in, b_in, cos, sin_s)


def _attn_mask(i):
    qi = lax.broadcasted_iota(jnp.int32, (ATTN_BLOCK, 2 * ATTN_BLOCK), 0)
    si = lax.broadcasted_iota(jnp.int32, (ATTN_BLOCK, 2 * ATTN_BLOCK), 1)
    diff = qi + ATTN_BLOCK - si
    return (diff >= 0) & (diff < ATTN_BLOCK) & ((si >= ATTN_BLOCK) | (i > 0))


def _row_sums(x, ones):
    hi = x.astype(BF16)
    lo = (x - hi.astype(F32)).astype(BF16)
    return _dot(hi, ones, NN) + _dot(lo, ones, NN)


def _attn_fwd(qkv, sinks):
    t = qkv.shape[0]
    nb = t // ATTN_BLOCK

    def kern(q_ref, kp_ref, kc_ref, vp_ref, vc_ref, s_ref, o_ref, l_ref):
        i = pl.program_id(0)
        si = lax.broadcasted_iota(jnp.int32, (2 * ATTN_BLOCK, ATTN_BLOCK), 0)
        qi = lax.broadcasted_iota(jnp.int32, (2 * ATTN_BLOCK, ATTN_BLOCK), 1)
        diff = qi + ATTN_BLOCK - si
        valid = (diff >= 0) & (diff < ATTN_BLOCK) & ((si >= ATTN_BLOCK) | (i > 0))
        kband = jnp.concatenate([kp_ref[...], kc_ref[...]], axis=0)
        vband = jnp.concatenate([vp_ref[...], vc_ref[...]], axis=0)
        ks = [kband[:, kh * HEAD_DIM:(kh + 1) * HEAD_DIM] for kh in range(N_KV_HEADS)]
        vs = [vband[:, kh * HEAD_DIM:(kh + 1) * HEAD_DIM] for kh in range(N_KV_HEADS)]
        scores = [_dot(ks[h // Q_PER_KV], q_ref[:, h * HEAD_DIM:(h + 1) * HEAD_DIM], NT) for h in range(N_Q_HEADS)]
        probs, lses = [], []
        for h in range(N_Q_HEADS):
            s = jnp.where(valid, scores[h], -jnp.inf)
            sink = s_ref[0:1, h:h + 1]
            m = jnp.maximum(jnp.max(s, axis=0, keepdims=True), sink)
            e = jnp.exp(s - m)
            denom = jnp.sum(e, axis=0, keepdims=True) + jnp.exp(sink - m)
            probs.append((e * (1.0 / denom)).astype(BF16))
            lses.append(m + jnp.log(denom))
        outs = [_dot(vs[h // Q_PER_KV], probs[h], TN) for h in range(N_Q_HEADS)]
        for c in range(N_Q_HEADS // 2):
            o_ref[:, c * 128:(c + 1) * 128] = jnp.concatenate([outs[2 * c], outs[2 * c + 1]], axis=0).T
        rid = lax.broadcasted_iota(jnp.int32, (N_Q_HEADS, ATTN_BLOCK), 0)
        rows = jnp.zeros((N_Q_HEADS, ATTN_BLOCK), F32)
        for h in range(N_Q_HEADS):
            rows = jnp.where(rid == h, lses[h], rows)
        rows = jnp.concatenate([rows, jnp.zeros((ATTN_BLOCK - N_Q_HEADS, ATTN_BLOCK), F32)], axis=0)
        l_ref[...] = rows.T[:, 0:N_Q_HEADS]

    blk = lambda w, f: pl.BlockSpec((ATTN_BLOCK, w), f)
    return pl.pallas_call(
        kern, name="attn_fwd", grid=(nb,),
        in_specs=[blk(512, lambda i: (i, 0)),
                  blk(128, lambda i: (jnp.maximum(i - 1, 0), 4)), blk(128, lambda i: (i, 4)),
                  blk(128, lambda i: (jnp.maximum(i - 1, 0), 5)), blk(128, lambda i: (i, 5)),
                  pl.BlockSpec((1, N_Q_HEADS), lambda i: (0, 0))],
        out_specs=[blk(512, lambda i: (i, 0)), blk(N_Q_HEADS, lambda i: (i, 0))],
        out_shape=[jax.ShapeDtypeStruct((t, D_ATTN), F32), jax.ShapeDtypeStruct((t, N_Q_HEADS), F32)],
        compiler_params=_params(("parallel",)),
    )(qkv, qkv, qkv, qkv, qkv, sinks)


def _attn_bwd(qkv, o, do, lse, sinks):
    t = qkv.shape[0]
    nb = t // ATTN_BLOCK

    def kern(q_ref, kp_ref, kc_ref, vp_ref, vc_ref, o_ref, do_ref, l_ref, s_ref,
             dq_ref, dk_ref, dv_ref, ds_ref, ck, cv):
        i = pl.program_id(0)

        @pl.when(i == 0)
        def _():
            ds_ref[...] = jnp.zeros_like(ds_ref)
            ck[...] = jnp.zeros_like(ck)
            cv[...] = jnp.zeros_like(cv)

        @pl.when(i < nb)
        def _():
            valid = _attn_mask(i)
            kband = jnp.concatenate([kp_ref[...], kc_ref[...]], axis=0)
            vband = jnp.concatenate([vp_ref[...], vc_ref[...]], axis=0)
            heads = range(N_Q_HEADS)
            sl = [slice(h * HEAD_DIM, (h + 1) * HEAD_DIM) for h in heads]
            ks = [kband[:, kh * HEAD_DIM:(kh + 1) * HEAD_DIM] for kh in range(N_KV_HEADS)]
            vs = [vband[:, kh * HEAD_DIM:(kh + 1) * HEAD_DIM] for kh in range(N_KV_HEADS)]
            qs = [q_ref[:, sl[h]] for h in heads]
            d_os = [do_ref[:, sl[h]] for h in heads]
            dobs = [d.astype(BF16) for d in d_os]
            scores = [_dot(qs[h], ks[h // Q_PER_KV], NT) for h in heads]
            dps = [_dot(dobs[h], vs[h // Q_PER_KV], NT) for h in heads]
            col_head = lax.broadcasted_iota(jnp.int32, (D_ATTN, 128), 0) // HEAD_DIM
            head_ones = (col_head == lax.broadcasted_iota(jnp.int32, (D_ATTN, 128), 1)).astype(BF16)
            deltas = _row_sums(do_ref[...] * o_ref[...], head_ones)
            pbs, dscs = [], []
            for h in heads:
                lse_h = l_ref[:, h:h + 1]
                p = jnp.where(valid, jnp.exp(scores[h] - lse_h), 0.0)
                delta = deltas[:, h:h + 1]
                pbs.append(p.astype(BF16))
                dscs.append((p * (dps[h] - delta)).astype(BF16))
                psink = jnp.exp(s_ref[0:1, h:h + 1] - lse_h)
                ds_ref[0:1, h:h + 1] += -jnp.sum(psink * delta, axis=0, keepdims=True)
            dqs = [_dot(dscs[h], ks[h // Q_PER_KV], NN) for h in heads]
            dkb = [sum(_dot(dscs[h], qs[h], TN) for h in heads if h // Q_PER_KV == kh) for kh in range(N_KV_HEADS)]
            dvb = [sum(_dot(pbs[h], dobs[h], TN) for h in heads if h // Q_PER_KV == kh) for kh in range(N_KV_HEADS)]
            for h in heads:
                dq_ref[:, sl[h]] = dqs[h]
            dk_band = jnp.concatenate(dkb, axis=1)
            dv_band = jnp.concatenate(dvb, axis=1)
            dk_ref[...] = ck[...] + dk_band[:ATTN_BLOCK]
            dv_ref[...] = cv[...] + dv_band[:ATTN_BLOCK]
            ck[...] = dk_band[ATTN_BLOCK:]
            cv[...] = dv_band[ATTN_BLOCK:]

        @pl.when(i == nb)
        def _():
            dk_ref[...] = ck[...]
            dv_ref[...] = cv[...]

    blk = lambda w, f: pl.BlockSpec((ATTN_BLOCK, w), f)
    cur = lambda i: jnp.minimum(i, nb - 1)
    prev = lambda i: jnp.clip(i - 1, 0, nb - 1)
    return pl.pallas_call(
        kern, name="attn_bwd", grid=(nb + 1,),
        in_specs=[blk(512, lambda i: (cur(i), 0)),
                  blk(128, lambda i: (prev(i), 4)), blk(128, lambda i: (cur(i), 4)),
                  blk(128, lambda i: (prev(i), 5)), blk(128, lambda i: (cur(i), 5)),
                  blk(512, lambda i: (cur(i), 0)), blk(512, lambda i: (cur(i), 0)),
                  blk(N_Q_HEADS, lambda i: (cur(i), 0)),
                  pl.BlockSpec((1, N_Q_HEADS), lambda i: (0, 0))],
        out_specs=[blk(512, lambda i: (cur(i), 0)), blk(128, lambda i: (prev(i), 0)),
                   blk(128, lambda i: (prev(i), 0)), pl.BlockSpec((1, N_Q_HEADS), lambda i: (0, 0))],
        out_shape=[jax.ShapeDtypeStruct((t, D_ATTN), F32), jax.ShapeDtypeStruct((t, D_KV), F32),
                   jax.ShapeDtypeStruct((t, D_KV), F32), jax.ShapeDtypeStruct((1, N_Q_HEADS), F32)],
        scratch_shapes=[pltpu.VMEM((ATTN_BLOCK, D_KV), F32), pltpu.VMEM((ATTN_BLOCK, D_KV), F32)],
        compiler_params=_params(("arbitrary",)),
    )(qkv, qkv, qkv, qkv, qkv, o, do, lse, sinks)


_GROUPS = ((0, D_ATTN), (D_ATTN, D_ATTN + D_S5), (D_ATTN + D_S5, D_MODEL))


def _mix_out_ln(ya, ys, yl, mg, w_out, b_out, xres, g, b, layer):
    t = xres.shape[0]
    tm = _row_tile(t, TM)

    def kern(a_ref, s_ref, l_ref, mg_ref, w_ref, bias_ref, x_ref, g_ref, b_ref, m_ref, y_ref, yb_ref, h_ref, r_ref):
        for (lo, hi), ref in zip(_GROUPS, (a_ref, s_ref, l_ref)):
            v = ref[...]
            n = v * lax.rsqrt(jnp.mean(v * v, axis=-1, keepdims=True) + RMS_EPS)
            m_ref[:, lo:hi] = (n * mg_ref[:, lo:hi]).astype(BF16)
        r = ALPHA * x_ref[...] + _dot(m_ref[...], w_ref[...], NN) + bias_ref[...]
        mu = jnp.mean(r, axis=-1, keepdims=True)
        xc = r - mu
        rstd = lax.rsqrt(jnp.mean(xc * xc, axis=-1, keepdims=True) + LN_EPS)
        xhat = xc * rstd
        h_ref[...] = xhat
        r_ref[...] = rstd
        y = xhat * g_ref[...] + b_ref[...]
        y_ref[...] = y
        yb_ref[...] = y.astype(BF16)

    rowb = lambda w: pl.BlockSpec((tm, w), lambda i: (i, 0))
    row = rowb(D_MODEL)
    vec = pl.BlockSpec((1, D_MODEL), lambda i: (0, 0))
    big = lambda dt: jax.ShapeDtypeStruct((t, D_MODEL), dt)
    return pl.pallas_call(
        kern, name="mix_out_ln", grid=(t // tm,),
        in_specs=[rowb(D_ATTN), rowb(D_S5), rowb(D_LRU), vec,
                  pl.BlockSpec((None, D_MODEL, D_MODEL), lambda i: (layer, 0, 0)), vec, row, vec, vec],
        out_specs=[row, row, row, row, pl.BlockSpec((tm, 1), lambda i: (i, 0))],
        out_shape=[big(BF16), big(F32), big(BF16), big(F32), jax.ShapeDtypeStruct((t, 1), F32)],
        compiler_params=_params(("parallel",)),
    )(ya, ys, yl, mg, w_out, b_out, xres, g, b)


def _d_mix_rms(dx1, xhat, rstd, lg, w_out, ya, ys, yl, mg, layer):
    t = dx1.shape[0]
    tm = _row_tile(t, TM)

    def kern(d_ref, h_ref, r_ref, lg_ref, w_ref, a_ref, s_ref, l_ref, g_ref,
             dr_ref, drb_ref, dlg_ref, dlb_ref, sr_ref, da_ref, ds_ref, dl_ref, dg_ref):
        @pl.when(pl.program_id(0) == 0)
        def _():
            for ref in (dlg_ref, dlb_ref, sr_ref, dg_ref):
                ref[...] = jnp.zeros_like(ref)

        _ln_bwd_tile(d_ref[...], h_ref[...], r_ref[...], lg_ref[...], dr_ref, drb_ref, dlg_ref, dlb_ref, sr_ref)
        dmix = _dot(drb_ref[...], w_ref[...], NT)
        for (lo, hi), ref, out in zip(_GROUPS, (a_ref, s_ref, l_ref), (da_ref, ds_ref, dl_ref)):
            v = ref[...]
            rstd = lax.rsqrt(jnp.mean(v * v, axis=-1, keepdims=True) + RMS_EPS)
            n = v * rstd
            dm = dmix[:, lo:hi]
            dg_ref[:, lo:hi] += jnp.sum(dm * n, axis=0, keepdims=True)
            dn = dm * g_ref[:, lo:hi]
            out[...] = rstd * (dn - n * jnp.mean(dn * n, axis=-1, keepdims=True))

    rowb = lambda w: pl.BlockSpec((tm, w), lambda i: (i, 0))
    vec = pl.BlockSpec((1, D_MODEL), lambda i: (0, 0))
    vshape = jax.ShapeDtypeStruct((1, D_MODEL), F32)
    return pl.pallas_call(
        kern, name="d_mix_rms", grid=(t // tm,),
        in_specs=[rowb(D_MODEL), rowb(D_MODEL), pl.BlockSpec((tm, 1), lambda i: (i, 0)), vec,
                  pl.BlockSpec((None, D_MODEL, D_MODEL), lambda i: (layer, 0, 0)),
                  rowb(D_ATTN), rowb(D_S5), rowb(D_LRU), vec],
        out_specs=[rowb(D_MODEL), rowb(D_MODEL), vec, vec, vec, rowb(D_ATTN), rowb(D_S5), rowb(D_LRU), vec],
        out_shape=[jax.ShapeDtypeStruct((t, D_MODEL), F32), jax.ShapeDtypeStruct((t, D_MODEL), BF16),
                   vshape, vshape, vshape, jax.ShapeDtypeStruct((t, D_ATTN), F32),
                   jax.ShapeDtypeStruct((t, D_S5), F32), jax.ShapeDtypeStruct((t, D_LRU), F32), vshape],
        compiler_params=_params(("arbitrary",)),
    )(dx1, xhat, rstd, lg, w_out, ya, ys, yl, mg)


def _matmul_ln(name, a, w, bias, xres, g, b, a_blk, a_map, w_blk, parts, layer):
    t = xres.shape[0]
    tm = a_blk[-2]

    def kern(a_ref, w_ref, bias_ref, x_ref, g_ref, b_ref, y_ref, yb_ref, h_ref, r_ref):
        if parts is None:
            f = _dot(a_ref[...], w_ref[...], NN)
        else:
            f = sum(_dot(a_ref[j], w_ref[j], NN) for j in range(parts))
        r = ALPHA * x_ref[...] + f + bias_ref[...]
        mu = jnp.mean(r, axis=-1, keepdims=True)
        xc = r - mu
        rstd = lax.rsqrt(jnp.mean(xc * xc, axis=-1, keepdims=True) + LN_EPS)
        xhat = xc * rstd
        h_ref[...] = xhat
        r_ref[...] = rstd
        y = xhat * g_ref[...] + b_ref[...]
        y_ref[...] = y
        yb_ref[...] = y.astype(BF16)

    row = pl.BlockSpec((tm, D_MODEL), lambda i: (i, 0))
    vec = pl.BlockSpec((1, D_MODEL), lambda i: (0, 0))
    big = lambda dt: jax.ShapeDtypeStruct((t, D_MODEL), dt)
    return pl.pallas_call(
        kern, name=name, grid=(t // tm,),
        in_specs=[pl.BlockSpec(a_blk, a_map), pl.BlockSpec((None,) + w_blk, lambda i: (layer,) + (0,) * len(w_blk)), vec, row, vec, vec],
        out_specs=[row, row, row, pl.BlockSpec((tm, 1), lambda i: (i, 0))],
        out_shape=[big(F32), big(BF16), big(F32), jax.ShapeDtypeStruct((t, 1), F32)],
        compiler_params=_params(("parallel",)),
    )(a, w, bias, xres, g, b)


def _ln_bwd(dy, xhat, rstd, g):
    t = dy.shape[0]
    tm = _row_tile(t, TM)

    def kern(d_ref, h_ref, r_ref, g_ref, dr_ref, drb_ref, dg_ref, db_ref, sr_ref):
        @pl.when(pl.program_id(0) == 0)
        def _():
            dg_ref[...] = jnp.zeros_like(dg_ref)
            db_ref[...] = jnp.zeros_like(db_ref)
            sr_ref[...] = jnp.zeros_like(sr_ref)

        d = d_ref[...]
        xhat = h_ref[...]
        dg_ref[...] += jnp.sum(d * xhat, axis=0, keepdims=True)
        db_ref[...] += jnp.sum(d, axis=0, keepdims=True)
        dh = d * g_ref[...]
        dr = r_ref[...] * (dh - jnp.mean(dh, axis=-1, keepdims=True)
                           - xhat * jnp.mean(dh * xhat, axis=-1, keepdims=True))
        dr_ref[...] = dr
        drb_ref[...] = dr.astype(BF16)
        sr_ref[...] += jnp.sum(dr, axis=0, keepdims=True)

    row = pl.BlockSpec((tm, D_MODEL), lambda i: (i, 0))
    vec = pl.BlockSpec((1, D_MODEL), lambda i: (0, 0))
    vshape = jax.ShapeDtypeStruct((1, D_MODEL), F32)
    return pl.pallas_call(
        kern, name="ln_bwd", grid=(t // tm,),
        in_specs=[row, row, pl.BlockSpec((tm, 1), lambda i: (i, 0)), vec],
        out_specs=[row, row, vec, vec, vec],
        out_shape=[jax.ShapeDtypeStruct((t, D_MODEL), F32), jax.ShapeDtypeStruct((t, D_MODEL), BF16),
                   vshape, vshape, vshape],
        compiler_params=_params(("arbitrary",)),
    )(dy, xhat, rstd, g)


def _ln_bwd_tile(d, xhat, rstd, g, dr_ref, drb_ref, dg_ref, db_ref, sr_ref):
    dg_ref[...] += jnp.sum(d * xhat, axis=0, keepdims=True)
    db_ref[...] += jnp.sum(d, axis=0, keepdims=True)
    dh = d * g
    dr = rstd * (dh - jnp.mean(dh, axis=-1, keepdims=True) - xhat * jnp.mean(dh * xhat, axis=-1, keepdims=True))
    dr_ref[...] = dr
    drb_ref[...] = dr.astype(BF16)
    sr_ref[...] += jnp.sum(dr, axis=0, keepdims=True)


def _loss_head(y, target, xhat, rstd, g):
    t = y.shape[0]
    tm = _row_tile(t, TM)

    def kern(y_ref, t_ref, h_ref, r_ref, g_ref, l_ref, dr_ref, drb_ref, dg_ref, db_ref, sr_ref):
        @pl.when(pl.program_id(0) == 0)
        def _():
            for ref in (l_ref, dg_ref, db_ref, sr_ref):
                ref[...] = jnp.zeros_like(ref)

        err = y_ref[...] - t_ref[...]
        part = jnp.sum(jnp.sum(err * err, axis=-1, keepdims=True), axis=0, keepdims=True)
        l_ref[...] += jnp.broadcast_to(part * (0.5 / D_MODEL), l_ref.shape)
        _ln_bwd_tile(err * (1.0 / D_MODEL), h_ref[...], r_ref[...], g_ref[...], dr_ref, drb_ref, dg_ref, db_ref, sr_ref)

    row = pl.BlockSpec((tm, D_MODEL), lambda i: (i, 0))
    vec = pl.BlockSpec((1, D_MODEL), lambda i: (0, 0))
    vshape = jax.ShapeDtypeStruct((1, D_MODEL), F32)
    return pl.pallas_call(
        kern, name="loss_head", grid=(t // tm,),
        in_specs=[row, row, row, pl.BlockSpec((tm, 1), lambda i: (i, 0)), vec],
        out_specs=[pl.BlockSpec((1, 128), lambda i: (0, 0)), row, row, vec, vec, vec],
        out_shape=[jax.ShapeDtypeStruct((1, 128), F32), jax.ShapeDtypeStruct((t, D_MODEL), F32),
                   jax.ShapeDtypeStruct((t, D_MODEL), BF16), vshape, vshape, vshape],
        compiler_params=_params(("arbitrary",)),
    )(y, target, xhat, rstd, g)


HALO = 8


def _ffn_hidden_fwd(xb, wg, wu, cw, cb, layer):
    t = xb.shape[0]
    tm = _row_tile(t, TMM)

    def kern(x_ref, wg_ref, wu_ref, cw_ref, cb_ref, g_ref, c_ref, u_ref, h_ref, ext):
        @pl.when(pl.program_id(1) == 0)
        def _():
            ext[0:HALO, :] = jnp.zeros((HALO, FF_SH), F32)

        x = x_ref[...]
        gb = _dot(x, wg_ref[...], NN).astype(BF16)
        ub = _dot(x, wu_ref[...], NN).astype(BF16)
        g_ref[...] = gb
        u_ref[...] = ub
        g = gb.astype(F32)
        w = [cw_ref[k:k + 1, :] for k in range(FFN_CONV)]
        body = cb_ref[...] + w[2] * g + w[1] * pltpu.roll(g, 1, 0) + w[0] * pltpu.roll(g, 2, 0)
        ext[HALO:, :] = g[0:HALO, :]
        head = cb_ref[...] + sum(w[k] * ext[pl.ds(HALO - (FFN_CONV - 1) + k, HALO), :] for k in range(FFN_CONV))
        gcb = jnp.concatenate([head, body[HALO:, :]], axis=0).astype(BF16)
        c_ref[...] = gcb
        gc = gcb.astype(F32)
        h_ref[...] = (gc * _sigmoid(gc) * ub.astype(F32)).astype(BF16)
        ext[0:HALO, :] = g[tm - HALO:, :]

    col = pl.BlockSpec((None, tm, FF_SH), lambda j, i: (j, i, 0))
    wspec = pl.BlockSpec((None, None, D_MODEL, FF_SH), lambda j, i: (layer, j, 0, 0))
    big = jax.ShapeDtypeStruct((N_CHIPS, t, FF_SH), BF16)
    return pl.pallas_call(
        kern, name="ffn_hidden_fwd", grid=(N_CHIPS, t // tm),
        in_specs=[pl.BlockSpec((tm, D_MODEL), lambda j, i: (i, 0)), wspec, wspec,
                  pl.BlockSpec((None, FFN_CONV, FF_SH), lambda j, i: (j, 0, 0)),
                  pl.BlockSpec((None, 1, FF_SH), lambda j, i: (j, 0, 0))],
        out_specs=[col, col, col, col], out_shape=[big, big, big, big],
        scratch_shapes=[pltpu.VMEM((2 * HALO, FF_SH), F32)],
        compiler_params=_params(("parallel", "arbitrary")),
    )(xb, wg, wu, cw, cb)


def _ffn_hidden_bwd(drb, gpre, gconv, up, wd, cw, layer):
    t = drb.shape[0]
    tm = _row_tile(t, TMM)
    nt = t // tm
    rb = lambda i: nt - 1 - i

    def kern(d_ref, g_ref, c_ref, u_ref, wd_ref, cw_ref, du_ref, dg_ref, dw_ref, db_ref, ext):
        @pl.when(pl.program_id(1) == 0)
        def _():
            dw_ref[...] = jnp.zeros_like(dw_ref)
            db_ref[...] = jnp.zeros_like(db_ref)
            ext[HALO:, :] = jnp.zeros((HALO, FF_SH), F32)

        dh = _dot(d_ref[...], wd_ref[...], NT)
        gc = c_ref[...].astype(F32)
        sg = _sigmoid(gc)
        du_ref[...] = (dh * (gc * sg)).astype(BF16)
        dgc = dh * u_ref[...].astype(F32) * (sg * (1.0 + gc * (1.0 - sg)))
        db_ref[...] += jnp.sum(dgc, axis=0, keepdims=True)
        g = g_ref[...].astype(F32)
        w = [cw_ref[k:k + 1, :] for k in range(FFN_CONV)]
        taps = [pltpu.roll(dgc, tm - 2, 0), pltpu.roll(dgc, tm - 1, 0), dgc]
        body = sum(w[k] * taps[k] for k in range(FFN_CONV))
        last = slice(tm - HALO, tm)
        ext[0:HALO, :] = dgc[last, :]
        tail_taps = [ext[pl.ds(FFN_CONV - 1 - k, HALO), :] for k in range(FFN_CONV)]
        tail = sum(w[k] * tail_taps[k] for k in range(FFN_CONV))
        dg_ref[...] = jnp.concatenate([body[0:tm - HALO, :], tail], axis=0).astype(BF16)
        for k in range(FFN_CONV):
            dw_ref[k:k + 1, :] += (jnp.sum(g * taps[k], axis=0, keepdims=True)
                                   + jnp.sum(g[last, :] * (tail_taps[k] - taps[k][last, :]), axis=0, keepdims=True))
        ext[HALO:, :] = dgc[0:HALO, :]

    col = pl.BlockSpec((None, tm, FF_SH), lambda j, i: (j, rb(i), 0))
    cws = pl.BlockSpec((None, FFN_CONV, FF_SH), lambda j, i: (j, 0, 0))
    cbs = pl.BlockSpec((None, 1, FF_SH), lambda j, i: (j, 0, 0))
    big = jax.ShapeDtypeStruct((N_CHIPS, t, FF_SH), BF16)
    return pl.pallas_call(
        kern, name="ffn_hidden_bwd", grid=(N_CHIPS, nt),
        in_specs=[pl.BlockSpec((tm, D_MODEL), lambda j, i: (rb(i), 0)), col, col, col,
                  pl.BlockSpec((None, None, FF_SH, D_MODEL), lambda j, i: (layer, j, 0, 0)), cws],
        out_specs=[col, col, cws, cbs],
        out_shape=[big, big, jax.ShapeDtypeStruct((N_CHIPS, FFN_CONV, FF_SH), F32),
                   jax.ShapeDtypeStruct((N_CHIPS, 1, FF_SH), F32)],
        scratch_shapes=[pltpu.VMEM((2 * HALO, FF_SH), F32)],
        compiler_params=_params(("parallel", "arbitrary")),
    )(drb, gpre, gconv, up, wd, cw)


def _s5_coefs(ar, ai, reverse):
    if reverse:
        ai = -ai
    pw = [(ar, ai)]
    for _ in range(SUBLANES - 1):
        pr, pi = pw[-1]
        pw.append((pr * ar - pi * ai, pr * ai + pi * ar))
    rows = jnp.arange(SUBLANES)[:, None]
    out = []
    for s in (1, 2, 4):
        keep = (rows + s <= SUBLANES - 1) if reverse else (rows >= s)
        out += [jnp.where(keep, pw[s - 1][0][None], 0.0), jnp.where(keep, pw[s - 1][1][None], 0.0)]
    order = list(range(SUBLANES - 1, -1, -1)) if reverse else list(range(SUBLANES))
    out += [jnp.stack([pw[k][0] for k in order]), jnp.stack([pw[k][1] for k in order])]
    return jnp.stack(out).astype(F32)


def _s5_scan(buf, coef_ref, carry, tm, reverse):
    n8 = tm // SUBLANES

    def body(it, c):
        cre, cim = c
        blk = (n8 - 1 - it) if reverse else it
        r0 = pl.multiple_of(blk * SUBLANES, SUBLANES)
        xre = buf[pl.ds(r0, SUBLANES), 0:N_STATE]
        xim = buf[pl.ds(r0, SUBLANES), N_STATE:]
        for idx, s in enumerate((1, 2, 4)):
            sh = (SUBLANES - s) if reverse else s
            sre = pltpu.roll(xre, sh, 0)
            sim = pltpu.roll(xim, sh, 0)
            are = coef_ref[2 * idx]
            aim = coef_ref[2 * idx + 1]
            xre, xim = xre + are * sre - aim * sim, xim + are * sim + aim * sre
        pre = coef_ref[6]
        pim = coef_ref[7]
        hre = xre + pre * cre - pim * cim
        him = xim + pre * cim + pim * cre
        buf[pl.ds(r0, SUBLANES), 0:N_STATE] = hre
        buf[pl.ds(r0, SUBLANES), N_STATE:] = him
        row = 0 if reverse else SUBLANES - 1
        return (jnp.broadcast_to(hre[row:row + 1], (SUBLANES, N_STATE)),
                jnp.broadcast_to(him[row:row + 1], (SUBLANES, N_STATE)))

    cre, cim = lax.fori_loop(0, n8, body, (carry[:, 0:N_STATE], carry[:, N_STATE:]))
    carry[:, 0:N_STATE] = cre
    carry[:, N_STATE:] = cim


def _real_scan(abuf, bbuf, carry, tm, reverse):
    n8 = tm // SUBLANES
    width = bbuf.shape[1]

    def body(it, c):
        blk = (n8 - 1 - it) if reverse else it
        r0 = pl.multiple_of(blk * SUBLANES, SUBLANES)
        a = abuf[pl.ds(r0, SUBLANES), :]
        b = bbuf[pl.ds(r0, SUBLANES), :]
        rows = lax.broadcasted_iota(jnp.int32, (SUBLANES, width), 0)
        for s in (1, 2, 4):
            sh = (SUBLANES - s) if reverse else s
            keep = (rows + s <= SUBLANES - 1) if reverse else (rows >= s)
            sa = pltpu.roll(a, sh, 0)
            sb = pltpu.roll(b, sh, 0)
            b = b + a * jnp.where(keep, sb, 0.0)
            a = a * jnp.where(keep, sa, 1.0)
        h = b + a * c
        bbuf[pl.ds(r0, SUBLANES), :] = h
        row = 0 if reverse else SUBLANES - 1
        return jnp.broadcast_to(h[row:row + 1], (SUBLANES, width))

    carry[...] = lax.fori_loop(0, n8, body, carry[...])


def _dot(a, b, dims):
    return lax.dot_general(a, b, (dims, ((), ())), preferred_element_type=F32)


TS5 = 512
HALO16 = 16


def _s5_fwd(proj, bmat, coef, cmat, dvec, gw, gb):
    t = proj.shape[0]
    tm = _row_tile(t, TS5)

    def kern(u_ref, b_ref, coef_ref, c_ref, d_ref, gw_ref, gb_ref, h_ref, y_ref, hbuf, carry):
        @pl.when(pl.program_id(0) == 0)
        def _():
            carry[...] = jnp.zeros_like(carry)

        u = u_ref[...]
        hbuf[...] = _dot(u.astype(BF16), b_ref[...], NN)
        _s5_scan(hbuf, coef_ref, carry, tm, False)
        hb = hbuf[...].astype(BF16)
        h_ref[...] = hb
        y = _dot(hb, c_ref[...], NN) + d_ref[...] * u
        ys = _gelu(y)
        z = _dot(ys.astype(BF16), gw_ref[...], NN) + gb_ref[...]
        y_ref[...] = ys * _sigmoid(z)

    full = lambda shp: pl.BlockSpec(shp, lambda i: (0,) * len(shp))
    return pl.pallas_call(
        kern, name="s5_fwd", grid=(t // tm,),
        in_specs=[pl.BlockSpec((tm, D_S5), lambda i: (i, 0)), full((D_S5, 2 * N_STATE)),
                  full((8, SUBLANES, N_STATE)), full((2 * N_STATE, D_S5)), full((1, D_S5)),
                  full((D_S5, D_S5)), full((1, D_S5))],
        out_specs=[pl.BlockSpec((tm, 2 * N_STATE), lambda i: (i, 0)), pl.BlockSpec((tm, D_S5), lambda i: (i, 0))],
        out_shape=[jax.ShapeDtypeStruct((t, 2 * N_STATE), BF16), jax.ShapeDtypeStruct((t, D_S5), F32)],
        scratch_shapes=[pltpu.VMEM((tm, 2 * N_STATE), F32), pltpu.VMEM((SUBLANES, 2 * N_STATE), F32)],
        compiler_params=_params(("arbitrary",)),
    )(proj, bmat, coef, cmat, dvec, gw, gb)


def _s5_bwd(proj, h, dout, bmat, coef_b, cmat, dvec, gw, gb):
    t = proj.shape[0]
    tm = _row_tile(t, TS5)
    nt = t // tm
    rb = lambda i: nt - 1 - i

    def kern(u_ref, h_ref, hp_ref, d_ref, b_ref, coef_ref, c_ref, dv_ref, gw_ref, gb_ref,
             du_ref, dc_ref, db_ref, da_ref, dd_ref, dgw_ref, dgb_ref, gbuf, hext, carry):
        i = pl.program_id(0)

        @pl.when(i == 0)
        def _():
            carry[...] = jnp.zeros_like(carry)
            for r in (dc_ref, db_ref, da_ref, dd_ref, dgw_ref, dgb_ref):
                r[...] = jnp.zeros_like(r)

        u = u_ref[...]
        hb = h_ref[...]
        y = _dot(hb, c_ref[...], NN) + dv_ref[...] * u
        ys = _gelu(y)
        ysb = ys.astype(BF16)
        sg = _sigmoid(_dot(ysb, gw_ref[...], NN) + gb_ref[...])
        d_o = d_ref[...]
        dz = d_o * ys * sg * (1.0 - sg)
        dzb = dz.astype(BF16)
        dys = d_o * sg + _dot(dzb, gw_ref[...], NT)
        dgw_ref[...] += _dot(ysb, dzb, TN)
        dgb_ref[...] += jnp.sum(dz, axis=0, keepdims=True)
        dy = dys * _gelu_grad(y)
        dd_ref[...] += jnp.sum(dy * u, axis=0, keepdims=True)
        dyb = dy.astype(BF16)
        dc_ref[...] += _dot(hb, dyb, TN)
        gbuf[...] = _dot(dyb, c_ref[...], NT)
        _s5_scan(gbuf, coef_ref, carry, tm, True)
        g = gbuf[...]
        first = jnp.where(i < nt - 1, hp_ref[HALO16 - 1:HALO16, :].astype(F32), 0.0)
        hext[SUBLANES - 1:SUBLANES, :] = first
        hext[SUBLANES:, :] = hb.astype(F32)
        hprev = hext[pl.ds(SUBLANES - 1, tm), :]
        gre, gim = g[:, 0:N_STATE], g[:, N_STATE:]
        pre, pim = hprev[:, 0:N_STATE], hprev[:, N_STATE:]
        da_ref[0:1, :] += jnp.sum(gre * pre + gim * pim, axis=0, keepdims=True)
        da_ref[1:2, :] += jnp.sum(gim * pre - gre * pim, axis=0, keepdims=True)
        gb16 = g.astype(BF16)
        db_ref[...] += _dot(u.astype(BF16), gb16, TN)
        du_ref[...] = dy * dv_ref[...] + _dot(gb16, b_ref[...], NT)

    full = lambda shp: pl.BlockSpec(shp, lambda i: (0,) * len(shp))
    shape = lambda shp: jax.ShapeDtypeStruct(shp, F32)
    return pl.pallas_call(
        kern, name="s5_bwd", grid=(nt,),
        in_specs=[pl.BlockSpec((tm, D_S5), lambda i: (rb(i), 0)),
                  pl.BlockSpec((tm, 2 * N_STATE), lambda i: (rb(i), 0)),
                  pl.BlockSpec((HALO16, 2 * N_STATE), lambda i: (jnp.maximum(rb(i) * (tm // HALO16) - 1, 0), 0)),
                  pl.BlockSpec((tm, D_S5), lambda i: (rb(i), 0)),
                  full((D_S5, 2 * N_STATE)), full((8, SUBLANES, N_STATE)), full((2 * N_STATE, D_S5)),
                  full((1, D_S5)), full((D_S5, D_S5)), full((1, D_S5))],
        out_specs=[pl.BlockSpec((tm, D_S5), lambda i: (rb(i), 0)), full((2 * N_STATE, D_S5)),
                   full((D_S5, 2 * N_STATE)), full((2, N_STATE)), full((1, D_S5)), full((D_S5, D_S5)),
                   full((1, D_S5))],
        out_shape=[shape((t, D_S5)), shape((2 * N_STATE, D_S5)), shape((D_S5, 2 * N_STATE)),
                   shape((2, N_STATE)), shape((1, D_S5)), shape((D_S5, D_S5)), shape((1, D_S5))],
        scratch_shapes=[pltpu.VMEM((tm, 2 * N_STATE), F32), pltpu.VMEM((tm + SUBLANES, 2 * N_STATE), F32),
                        pltpu.VMEM((SUBLANES, 2 * N_STATE), F32)],
        compiler_params=_params(("arbitrary",)),
    )(proj, h, h, dout, bmat, coef_b, cmat, dvec, gw, gb)


def _lru_gates(ext, x_ref, p_ref, cw_ref, cb_ref, wx_ref, bx_ref, wa_ref, ba_ref, ap_ref, first_tile, row0, tm):
    ext[0:HALO, :] = jnp.where(first_tile, 0.0, p_ref[...])
    ext[HALO:, :] = x_ref[...]
    taps = [ext[pl.ds(HALO - (LRU_CONV - 1) + k, tm), :] for k in range(LRU_CONV)]
    xc = cb_ref[...] + sum(cw_ref[k:k + 1, :] * taps[k] for k in range(LRU_CONV))
    xcb = xc.astype(BF16)
    gx = _sigmoid(_dot(xcb, wx_ref[...], NN) + bx_ref[...])
    ga = _sigmoid(_dot(xcb, wa_ref[...], NN) + ba_ref[...])
    z = -ap_ref[...]
    sp = jnp.maximum(z, 0.0) + jnp.log(1.0 + jnp.exp(-jnp.abs(z)))
    log_a = -LRU_C * ga * sp
    a = jnp.exp(log_a)
    tok = row0 + lax.broadcasted_iota(jnp.int32, a.shape, 0)
    is0 = tok == 0
    mult = jnp.where(is0, 1.0, jnp.sqrt(1.0 - jnp.exp(2.0 * log_a)))
    return taps, xc, xcb, gx, ga, sp, a, mult, is0


def _lru_specs(tm, blk_of):
    col = lambda cidx: pl.BlockSpec((tm, D_LRU), lambda i: (blk_of(i), cidx))
    prev = lambda cidx: pl.BlockSpec((HALO, D_LRU), lambda i: (jnp.maximum(blk_of(i) * (tm // HALO) - 1, 0), cidx))
    full = lambda shp: pl.BlockSpec(shp, lambda i: (0,) * len(shp))
    wts = [full((LRU_CONV, D_LRU)), full((1, D_LRU)), full((D_LRU, D_LRU)), full((1, D_LRU)),
           full((D_LRU, D_LRU)), full((1, D_LRU)), full((1, D_LRU))]
    return col, prev, full, wts


def _lru_fwd(proj, cw, cb, wx, bx, wa, ba, ap):
    t = proj.shape[0]
    tm = _row_tile(t, TMM)

    def kern(x_ref, p_ref, g_ref, cw_ref, cb_ref, wx_ref, bx_ref, wa_ref, ba_ref, ap_ref,
             y_ref, h_ref, ext, abuf, carry):
        i = pl.program_id(0)

        @pl.when(i == 0)
        def _():
            carry[...] = jnp.zeros_like(carry)

        _, xc, _, gx, _, _, a, mult, _ = _lru_gates(ext, x_ref, p_ref, cw_ref, cb_ref, wx_ref, bx_ref, wa_ref,
                                                    ba_ref, ap_ref, i == 0, i * tm, tm)
        abuf[...] = a
        h_ref[...] = mult * gx * xc
        _real_scan(abuf, h_ref, carry, tm, False)
        y_ref[...] = h_ref[...] * _gelu(g_ref[...])

    col, prev, full, wts = _lru_specs(tm, lambda i: i)
    out = pl.BlockSpec((tm, D_LRU), lambda i: (i, 0))
    return pl.pallas_call(
        kern, name="lru_fwd", grid=(t // tm,),
        in_specs=[col(1), prev(1), col(2)] + wts, out_specs=[out, out],
        out_shape=[jax.ShapeDtypeStruct((t, D_LRU), F32), jax.ShapeDtypeStruct((t, D_LRU), F32)],
        scratch_shapes=[pltpu.VMEM((tm + HALO, D_LRU), F32), pltpu.VMEM((tm, D_LRU), F32),
                        pltpu.VMEM((SUBLANES, D_LRU), F32)],
        compiler_params=_params(("arbitrary",)),
    )(proj, proj, proj, cw, cb, wx, bx, wa, ba, ap)


def _lru_bwd(proj, h, dout, cw, cb, wx, bx, wa, ba, ap):
    t = proj.shape[0]
    tm = _row_tile(t, TMM)
    nt = t // tm
    rb = lambda i: nt - 1 - i

    def kern(x_ref, p_ref, g_ref, h_ref, hp_ref, d_ref, cw_ref, cb_ref, wx_ref, bx_ref, wa_ref, ba_ref, ap_ref,
             dxc_ref, dg_ref, dcw_ref, dcb_ref, dwx_ref, dbx_ref, dwa_ref, dba_ref, dap_ref,
             ext, aext, abuf, gbuf, carry, acarry):
        i = pl.program_id(0)
        blk = nt - 1 - i

        @pl.when(i == 0)
        def _():
            carry[...] = jnp.zeros_like(carry)
            acarry[...] = jnp.zeros_like(acarry)
            for r in (dcw_ref, dcb_ref, dwx_ref, dbx_ref, dwa_ref, dba_ref, dap_ref):
                r[...] = jnp.zeros_like(r)

        taps, xc, xcb, gx, ga, sp, a, mult, is0 = _lru_gates(
            ext, x_ref, p_ref, cw_ref, cb_ref, wx_ref, bx_ref, wa_ref, ba_ref, ap_ref, blk == 0, blk * tm, tm)
        gate = g_ref[...]
        d_o = d_ref[...]
        hcur = h_ref[...]
        dg_ref[...] = d_o * hcur * _gelu_grad(gate)
        aext[0:tm, :] = a
        aext[tm:, :] = acarry[...]
        abuf[...] = aext[pl.ds(1, tm), :]
        gbuf[...] = d_o * _gelu(gate)
        _real_scan(abuf, gbuf, carry, tm, True)
        acarry[...] = jnp.broadcast_to(a[0:1], acarry.shape)
        g = gbuf[...]
        ext[0:HALO, :] = jnp.where(blk == 0, 0.0, hp_ref[...])
        ext[HALO:, :] = hcur
        hprev = ext[pl.ds(HALO - 1, tm), :]
        dmult = jnp.where(is0, 0.0, g * gx * xc)
        dgx = g * mult * xc
        dxc = g * mult * gx
        dlog_a = g * hprev * a - dmult * (a * a) / mult
        dga = dlog_a * (-LRU_C * sp)
        dsp = jnp.sum(dlog_a * (-LRU_C * ga), axis=0, keepdims=True)
        dap_ref[...] += dsp * (-_sigmoid(-ap_ref[...]))
        dpa = (dga * ga * (1.0 - ga))
        dpx = (dgx * gx * (1.0 - gx))
        dpab, dpxb = dpa.astype(BF16), dpx.astype(BF16)
        dwx_ref[...] += _dot(xcb, dpxb, TN)
        dwa_ref[...] += _dot(xcb, dpab, TN)
        dbx_ref[...] += jnp.sum(dpx, axis=0, keepdims=True)
        dba_ref[...] += jnp.sum(dpa, axis=0, keepdims=True)
        dxc = dxc + _dot(dpxb, wx_ref[...], NT) + _dot(dpab, wa_ref[...], NT)
        dxc_ref[...] = dxc
        dcb_ref[...] += jnp.sum(dxc, axis=0, keepdims=True)
        for k in range(LRU_CONV):
            dcw_ref[k:k + 1, :] += jnp.sum(dxc * taps[k], axis=0, keepdims=True)

    col, prev, full, wts = _lru_specs(tm, rb)
    row = pl.BlockSpec((tm, D_LRU), lambda i: (rb(i), 0))
    hprev_spec = pl.BlockSpec((HALO, D_LRU), lambda i: (jnp.maximum(rb(i) * (tm // HALO) - 1, 0), 0))
    shape = lambda shp: jax.ShapeDtypeStruct(shp, F32)
    vec = (1, D_LRU)
    sq = (D_LRU, D_LRU)
    return pl.pallas_call(
        kern, name="lru_bwd", grid=(nt,),
        in_specs=[col(1), prev(1), col(2), row, hprev_spec, row] + wts,
        out_specs=[row, row, full((LRU_CONV, D_LRU)), full(vec), full(sq), full(vec), full(sq), full(vec), full(vec)],
        out_shape=[shape((t, D_LRU)), shape((t, D_LRU)), shape((LRU_CONV, D_LRU)), shape(vec), shape(sq),
                   shape(vec), shape(sq), shape(vec), shape(vec)],
        scratch_shapes=[pltpu.VMEM((tm + HALO, D_LRU), F32), pltpu.VMEM((tm + HALO, D_LRU), F32),
                        pltpu.VMEM((tm, D_LRU), F32), pltpu.VMEM((tm, D_LRU), F32),
                        pltpu.VMEM((SUBLANES, D_LRU), F32), pltpu.VMEM((SUBLANES, D_LRU), F32)],
        compiler_params=_params(("arbitrary",)),
    )(proj, proj, proj, h, h, dout, cw, cb, wx, bx, wa, ba, ap)


def _assemble_dproj(dq, dk, dv, du, dxc, dgate, cos, sin_s, cw):
    t = dq.shape[0]
    tm = _row_tile(t, TM)
    nt = t // tm

    def kern(dq_ref, dk_ref, dv_ref, du_ref, dx_ref, dn_ref, dg_ref, c_ref, s_ref, cw_ref, o_ref, b_ref, ext):
        i = pl.program_id(0)

        @pl.when(i == 0)
        def _():
            b_ref[...] = jnp.zeros_like(b_ref)

        def put(lo, val):
            hi = lo + val.shape[1]
            o_ref[:, lo:hi] = val.astype(BF16)
            b_ref[:, lo:hi] += jnp.sum(val, axis=0, keepdims=True)

        c = c_ref[...]
        s = s_ref[...]
        for ch in range(4):
            x = dq_ref[:, ch * 128:(ch + 1) * 128] * (HEAD_DIM ** -0.5)
            put(ch * 128, x * c - _rope_swap(x) * s)
        x = dk_ref[...]
        put(512, x * c - _rope_swap(x) * s)
        put(640, dv_ref[...])
        put(768, du_ref[...])
        ext[0:tm, :] = dx_ref[...]
        ext[tm:, :] = jnp.where(i < nt - 1, dn_ref[...], 0.0)
        put(1024, sum(cw_ref[k:k + 1, :] * ext[pl.ds(LRU_CONV - 1 - k, tm), :] for k in range(LRU_CONV)))
        put(1280, dg_ref[...])

    row = lambda w: pl.BlockSpec((tm, w), lambda i: (i, 0))
    nxt = pl.BlockSpec((HALO, D_LRU), lambda i: (jnp.minimum((i + 1) * (tm // HALO), t // HALO - 1), 0))
    return pl.pallas_call(
        kern, name="assemble_dproj", grid=(nt,),
        in_specs=[row(512), row(128), row(128), row(256), row(256), nxt, row(256), row(128), row(128),
                  pl.BlockSpec((LRU_CONV, D_LRU), lambda i: (0, 0))],
        out_specs=[row(D_IN), pl.BlockSpec((1, D_IN), lambda i: (0, 0))],
        out_shape=[jax.ShapeDtypeStruct((t, D_IN), BF16), jax.ShapeDtypeStruct((1, D_IN), F32)],
        scratch_shapes=[pltpu.VMEM((tm + HALO, D_LRU), F32)],
        compiler_params=_params(("arbitrary",)),
    )(dq, dk, dv, du, dxc, dxc, dgate, cos, sin_s, cw)


def _blockdiag_s5(bbar_re, bbar_im, c_re, c_im):
    eye = jnp.eye(S5_GROUPS, dtype=F32)
    b_of = lambda m: jnp.einsum('gpc,gh->gchp', m, eye).reshape(D_S5, N_STATE)
    c_of = lambda m: jnp.einsum('gcp,gh->gphc', m, eye).reshape(N_STATE, D_S5)
    bmat = jnp.concatenate([b_of(bbar_re), b_of(bbar_im)], axis=1)
    cmat = jnp.concatenate([c_of(c_re), -c_of(c_im)], axis=0)
    return bmat, cmat


def _s5_prepare(a_re, a_im, b_re, b_im, c_re, c_im, log_dt):
    lam_re = jnp.minimum(a_re, -1e-4)
    lam_im = a_im
    dt = jnp.exp(log_dt)[:, None]
    decay = jnp.exp(dt * lam_re)
    ang = dt * lam_im
    abar_re = decay * jnp.cos(ang)
    abar_im = decay * jnp.sin(ang)
    den = jnp.square(lam_re) + jnp.square(lam_im)
    nr = abar_re - 1.0
    ni = abar_im
    coef_re = (nr * lam_re + ni * lam_im) / den
    coef_im = (ni * lam_re - nr * lam_im) / den
    bbar_re = coef_re[..., None] * b_re - coef_im[..., None] * b_im
    bbar_im = coef_re[..., None] * b_im + coef_im[..., None] * b_re
    bmat, cmat = _blockdiag_s5(bbar_re, bbar_im, c_re, c_im)
    return abar_re.reshape(N_STATE), abar_im.reshape(N_STATE), bmat, cmat


def _blockdiag_lru(w):
    eye = jnp.eye(LRU_HEADS, dtype=F32)
    return jnp.einsum('hij,hk->hikj', w, eye).reshape(D_LRU, D_LRU)


def _rope_tables(t):
    inv_freq = ROPE_THETA ** (-jnp.arange(0, HEAD_DIM, 2, dtype=F32) / HEAD_DIM)
    ang = jnp.arange(t, dtype=F32)[:, None] * inv_freq[None, :]
    cos, sin = jnp.cos(ang), jnp.sin(ang)
    return jnp.tile(jnp.concatenate([cos, cos], axis=1), (1, 2)), jnp.tile(jnp.concatenate([-sin, sin], axis=1), (1, 2))


def _vec(v):
    return v.reshape(1, -1)


def _layer_weights(p):
    abar_re, abar_im, bmat, cmat = _s5_prepare(p['s5_a_re'], p['s5_a_im'], p['s5_b_re'], p['s5_b_im'],
                                               p['s5_c_re'], p['s5_c_im'], p['s5_log_dt'])
    return dict(
        coef_f=_s5_coefs(abar_re, abar_im, False), coef_b=_s5_coefs(abar_re, abar_im, True),
        bmat=bmat.astype(BF16), cmat=cmat.astype(BF16),
        wx=_blockdiag_lru(p['lru_wx']).astype(BF16), wa=_blockdiag_lru(p['lru_wa']).astype(BF16),
        gw=p['s5_glu_w'].astype(BF16))


def _layer_fwd(x, xb, p, w, cos, sin_s):
    t = x.shape[0]
    tm = _row_tile(t, TM)
    layer = p['layer']
    qkv, uxg = _in_proj(xb, p['w_in'], _vec(p['b_in']), cos, sin_s, layer)
    ya, lse = _attn_fwd(qkv, _vec(p['attn_sinks']))
    h5, ys = _s5_fwd(uxg, w['bmat'], w['coef_f'], w['cmat'], _vec(p['s5_d']), w['gw'], _vec(p['s5_glu_b']))
    lru_w = (p['lru_conv_w'], _vec(p['lru_conv_b']), w['wx'], _vec(p['lru_bx']), w['wa'], _vec(p['lru_ba']),
             _vec(p['lru_a_param']))
    yl, hl = _lru_fwd(uxg, *lru_w)
    mix, x1, x1b, xhat1, rstd1 = _mix_out_ln(ya, ys, yl, _vec(p['mix_norm_g']), p['w_out'], _vec(p['b_out']), x,
                                             _vec(p['ln1_g']), _vec(p['ln1_b']), layer)
    gpre, gconv, up, hmid = _ffn_hidden_fwd(x1b, p['ffn_w_gate'], p['ffn_w_up'], p['ffn_conv_w'], p['ffn_conv_b'],
                                            layer)
    x2, x2b, xhat2, rstd2 = _matmul_ln(
        "ffn_down_ln", hmid, p['ffn_w_down'], jnp.zeros((1, D_MODEL), F32), x1, _vec(p['ln2_g']), _vec(p['ln2_b']),
        a_blk=(N_CHIPS, tm, FF_SH), a_map=lambda i: (0, i, 0), w_blk=(N_CHIPS, FF_SH, D_MODEL), parts=N_CHIPS,
        layer=layer)
    saved = dict(xb=xb, uxg=uxg, qkv=qkv, ya=ya, lse=lse, h5=h5, ys=ys, yl=yl, hl=hl, mix=mix, x1b=x1b, xhat1=xhat1,
                 rstd1=rstd1, gpre=gpre, gconv=gconv, up=up, hmid=hmid, xhat2=xhat2, rstd2=rstd2, lru_w=lru_w)
    return x2, x2b, saved


def _layer_bwd(dr2, dr2b, s, p, w, cos, sin_s, big, below):
    t = dr2.shape[0]
    tk = _row_tile(t, TMM)
    nk = t // tk
    tm = _row_tile(t, TM)
    layer = p['layer']
    big = dict(big)
    g = {}
    dup, dgpre, g['ffn_conv_w'], g['ffn_conv_b'] = _ffn_hidden_bwd(
        dr2b, s['gpre'], s['gconv'], s['up'], p['ffn_w_down'], p['ffn_conv_w'], layer)
    big['ffn_w_down'] = _matmul(
        "d_w_down", s['hmid'], dr2b, a_blk=(None, tk, FF_SH), a_map=lambda i, j, k: (i, k, 0), b_blk=(tk, D_MODEL),
        b_map=lambda i, j, k: (k, 0), out_shape=(DEPTH, N_CHIPS, FF_SH, D_MODEL), o_blk=(None, None, FF_SH, D_MODEL),
        o_map=lambda i, j: (layer, i, 0, 0), grid=(N_CHIPS, 1, nk), dims=TN, into=big['ffn_w_down'])
    d_ffn_w = lambda name, dact, buf: _matmul(
        name, s['x1b'], dact, a_blk=(tk, D_MODEL), a_map=lambda i, j, k: (k, 0), b_blk=(None, tk, FF_SH),
        b_map=lambda i, j, k: (j, k, 0), out_shape=(DEPTH, N_CHIPS, D_MODEL, FF_SH),
        o_blk=(None, None, D_MODEL, FF_SH), o_map=lambda i, j: (layer, j, 0, 0), grid=(1, N_CHIPS, nk), dims=TN,
        into=buf)
    big['ffn_w_gate'] = d_ffn_w("d_w_gate", dgpre, big['ffn_w_gate'])
    big['ffn_w_up'] = d_ffn_w("d_w_up", dup, big['ffn_w_up'])
    wspec = dict(b_blk=(None, None, D_MODEL, FF_SH), b_map=lambda i, j, k: (layer, k, 0, 0))
    dx1 = _matmul(
        "d_x1", dgpre, p['ffn_w_gate'], pair2=(dup, p['ffn_w_up']), a_blk=(None, tk, FF_SH),
        a_map=lambda i, j, k: (k, i, 0), out_shape=(t, D_MODEL), o_blk=(tk, D_MODEL), o_map=lambda i, j: (i, 0),
        grid=(nk, 1, N_CHIPS), dims=NT, add=dr2, add_scale=ALPHA, **wspec)
    dr1, dr1b, g['ln1_g'], g['ln1_b'], g['b_out'], dya, dys, dyl, g['mix_norm_g'] = _d_mix_rms(
        dx1, s['xhat1'], s['rstd1'], _vec(p['ln1_g']), p['w_out'], s['ya'], s['ys'], s['yl'],
        _vec(p['mix_norm_g']), layer)
    big['w_out'] = _matmul(
        "d_w_out", s['mix'], dr1b, a_blk=(tk, D_MODEL), a_map=lambda i, j, k: (k, 0), b_blk=(tk, D_MODEL),
        b_map=lambda i, j, k: (k, 0), out_shape=(DEPTH, D_MODEL, D_MODEL), o_blk=(None, D_MODEL, D_MODEL),
        o_map=lambda i, j: (layer, 0, 0), grid=(1, 1, nk), dims=TN, into=big['w_out'])
    dq, dk, dv, g['attn_sinks'] = _attn_bwd(s['qkv'], s['ya'], dya, s['lse'], _vec(p['attn_sinks']))
    du, dcmat, dbmat, dabar, g['s5_d'], g['s5_glu_w'], g['s5_glu_b'] = _s5_bwd(
        s['uxg'], s['h5'], dys, w['bmat'], w['coef_b'], w['cmat'], _vec(p['s5_d']), w['gw'], _vec(p['s5_glu_b']))
    (dxc, dgate, g['lru_conv_w'], g['lru_conv_b'], dwx, g['lru_bx'], dwa, g['lru_ba'],
     g['lru_a_param']) = _lru_bwd(s['uxg'], s['hl'], dyl, *s['lru_w'])
    dproj, g['b_in'] = _assemble_dproj(dq, dk, dv, du, dxc, dgate, cos, sin_s, p['lru_conv_w'])
    big['w_in'] = _matmul(
        "d_w_in", s['xb'], dproj, a_blk=(tk, D_MODEL), a_map=lambda i, j, k: (k, 0), b_blk=(tk, IN_SH),
        b_map=lambda i, j, k: (k, j), out_shape=(DEPTH, N_CHIPS, D_MODEL, IN_SH), o_blk=(None, None, D_MODEL, IN_SH),
        o_map=lambda i, j: (layer, j, 0, 0), grid=(1, N_CHIPS, nk), dims=TN, into=big['w_in'])
    dx = _matmul("d_x", dproj, p['w_in'], a_blk=(tk, IN_SH), a_map=lambda i, j, k: (i, k),
                 b_blk=(None, None, D_MODEL, IN_SH), b_map=lambda i, j, k: (layer, k, 0, 0), out_shape=(t, D_MODEL),
                 o_blk=(tk, D_MODEL), o_map=lambda i, j: (i, 0), grid=(nk, 1, N_CHIPS), dims=NT,
                 add=dr1, add_scale=ALPHA)
    if below is not None:
        dx = _ln_bwd(dx, below[0]['xhat2'], below[0]['rstd2'], _vec(below[1]['ln2_g']))
    return dx, _param_chain(g, p, dabar, dbmat, dcmat, dwx, dwa), big


def _param_chain(g, p, dabar, dbmat, dcmat, dwx, dwa):
    s5_names = ('s5_a_re', 's5_a_im', 's5_b_re', 's5_b_im', 's5_c_re', 's5_c_im', 's5_log_dt')
    _, s5_vjp = jax.vjp(_s5_prepare, *[p[n] for n in s5_names])
    for n, val in zip(s5_names, s5_vjp((dabar[0], dabar[1], dbmat, dcmat))):
        g[n] = val
    g['lru_wx'] = jax.vjp(_blockdiag_lru, p['lru_wx'])[1](dwx)[0]
    g['lru_wa'] = jax.vjp(_blockdiag_lru, p['lru_wa'])[1](dwa)[0]
    return g


ROW_TILE = 512


def _pick_rows(rows):
    for rt in range(min(rows, ROW_TILE), 0, -1):
        if rows % rt == 0 and (rt % 16 == 0 or rt == rows):
            return rt
    return rows


def _cast_bf16(a):
    a2 = a.reshape(-1, a.shape[-1])
    rows, c = a2.shape
    rt = _pick_rows(rows)

    def kern(a_ref, o_ref):
        o_ref[...] = a_ref[...].astype(BF16)

    spec = pl.BlockSpec((rt, c), lambda i: (i, 0))
    out = pl.pallas_call(kern, name="cast_bf16", grid=(rows // rt,), in_specs=[spec], out_specs=spec,
                         out_shape=jax.ShapeDtypeStruct((rows, c), BF16), compiler_params=_params(("parallel",)))(a2)
    return out.reshape(a.shape)


def _sum_parts(name, parts, shape):
    c = shape[-1]
    rows = math.prod(shape[:-1])
    rt = _pick_rows(rows)
    n = len(parts)

    def kern(*refs):
        acc = refs[0][...].astype(F32)
        for r in refs[1:n]:
            acc = acc + r[...].astype(F32)
        refs[n][...] = acc

    specs, args = [], []
    for arr, j in parts:
        if j is None:
            specs.append(pl.BlockSpec((rt, c), lambda i: (i, 0)))
            args.append(arr.reshape(rows, c))
        else:
            specs.append(pl.BlockSpec((None, rt, c), functools.partial(lambda i, jj: (jj, i, 0), jj=j)))
            args.append(arr.reshape(arr.shape[0], rows, c))
    out = pl.pallas_call(kern, name=name, grid=(rows // rt,), in_specs=specs,
                         out_specs=pl.BlockSpec((rt, c), lambda i: (i, 0)),
                         out_shape=jax.ShapeDtypeStruct((rows, c), F32), compiler_params=_params(("parallel",)))(*args)
    return out.reshape(shape)


def _adamw(name, w, g, m, v):
    shape = w.shape
    c = shape[-1]
    rows = math.prod(shape[:-1])
    rt = _pick_rows(rows)

    def kern(w_ref, g_ref, m_ref, v_ref, d_ref, nm_ref, nv_ref):
        g_ = g_ref[...]
        m_ = ADAM_B1 * m_ref[...] + (1.0 - ADAM_B1) * g_
        v_ = ADAM_B2 * v_ref[...] + (1.0 - ADAM_B2) * jnp.square(g_)
        m_hat = m_ / (1.0 - ADAM_B1 ** ADAM_STEP)
        v_hat = v_ / (1.0 - ADAM_B2 ** ADAM_STEP)
        d_ref[...] = -ADAM_LR * (m_hat / (jnp.sqrt(v_hat) + ADAM_EPS) + ADAM_WD * w_ref[...])
        nm_ref[...] = m_
        nv_ref[...] = v_

    spec = pl.BlockSpec((rt, c), lambda i: (i, 0))
    outs = pl.pallas_call(kern, name=name, grid=(rows // rt,), in_specs=[spec] * 4, out_specs=[spec] * 3,
                          out_shape=[jax.ShapeDtypeStruct((rows, c), F32)] * 3,
                          compiler_params=_params(("parallel",)))(*[a.reshape(rows, c) for a in (w, g, m, v)])
    return tuple(o.reshape(shape) for o in outs)


def _position():
    return lax.axis_index("x"), lax.axis_index("y"), lax.axis_index("c")


def _other_chips(x, y):
    return [(1 - x, y), (x, 1 - y), (1 - x, 1 - y)]


def _exchange(name, arrs, out_shapes, n_local, n_remote, plan):
    n_in, n_out = len(arrs), len(out_shapes)

    def kern(*refs):
        ins, outs = refs[:n_in], refs[n_in:n_in + n_out]
        send, recv, loc = refs[n_in + n_out:]
        local, remote = plan(ins, outs, *_position())
        assert len(local) == n_local and len(remote) == n_remote
        own = [pltpu.make_async_copy(s, d, loc.at[k]) for k, (s, d) in enumerate(local)]
        for cp in own:
            cp.start()
        sent = [pltpu.make_async_remote_copy(src_ref=s, dst_ref=d, send_sem=send.at[k], recv_sem=recv.at[k],
                                             device_id=peer, device_id_type=MESH)
                for k, (s, d, peer, _) in enumerate(remote)]
        for cp in sent:
            cp.start()
        for k, (s, _, peer, landing) in enumerate(remote):
            pltpu.make_async_remote_copy(src_ref=s, dst_ref=landing, send_sem=send.at[k], recv_sem=recv.at[k],
                                         device_id=peer, device_id_type=MESH).wait_recv()
        for cp in sent:
            cp.wait_send()
        for cp in own:
            cp.wait()

    return pl.pallas_call(
        kern, name=name, in_specs=[ANY] * n_in, out_specs=[ANY] * n_out, out_shape=out_shapes,
        scratch_shapes=[pltpu.SemaphoreType.DMA((n_remote,)), pltpu.SemaphoreType.DMA((n_remote,)),
                        pltpu.SemaphoreType.DMA((max(n_local, 1),))],
    )(*arrs)


def _allgather_chips(arrs, halved=()):
    n = len(arrs)
    layers = arrs[0].shape[0]

    def plan(ins, outs, x, y, c):
        me = 2 * x + y
        local, remote = [], []
        for t in range(n):
            for l in range(layers):
                src = ins[t].at[l]
                if t in halved:
                    r2 = ins[t].shape[2] // 2
                    src = ins[t].at[l, :, pl.ds(c * r2, r2)]
                local.append((src, outs[t].at[l, pl.ds(me, 1)]))
                for px, py in _other_chips(x, y):
                    remote.append((src, outs[t].at[l, pl.ds(me, 1)], (px, py, c),
                                   outs[t].at[l, pl.ds(2 * px + py, 1)]))
        return local, remote

    outs = []
    for t, a in enumerate(arrs):
        tail = (a.shape[2] // 2,) + a.shape[3:] if t in halved else a.shape[2:]
        outs.append(jax.ShapeDtypeStruct((a.shape[0], N_CHIPS) + tail, a.dtype))
    return _exchange("allgather_chips", arrs, outs, n * layers, 3 * n * layers, plan)


def _chip_scatter(arrs):
    n = len(arrs)
    layers = arrs[0].shape[0]

    def plan(ins, outs, x, y, c):
        me = 2 * x + y
        local, remote = [], []
        for t in range(n):
            for l in range(layers):
                local.append((ins[t].at[l, pl.ds(me, 1)], outs[2 * t].at[l]))
                for j, (px, py) in enumerate(_other_chips(x, y)):
                    remote.append((ins[t].at[l, pl.ds(2 * px + py, 1)], outs[2 * t + 1].at[j, l], (px, py, c),
                                   outs[2 * t + 1].at[j, l]))
        return local, remote

    outs = []
    for a in arrs:
        one = (a.shape[0], 1) + a.shape[2:]
        outs += [jax.ShapeDtypeStruct(one, a.dtype), jax.ShapeDtypeStruct((3,) + one, a.dtype)]
    return _exchange("chip_scatter", arrs, outs, n * layers, 3 * n * layers, plan)


def _allgather_devices(v):
    def kern(v_ref, o_ref, send, recv, loc):
        x, y, c = _position()
        me, sibling = (x, y, c), (x, y, 1 - c)
        chips = _other_chips(x, y)

        def rows(px, py, pc):
            return o_ref.at[pl.ds(4 * px + 2 * py + pc, 1)]

        def copy(k, block, to, src=None):
            return pltpu.make_async_remote_copy(
                src_ref=rows(*block) if src is None else src, dst_ref=rows(*block), send_sem=send.at[k],
                recv_sem=recv.at[k], device_id=to, device_id_type=MESH)

        mine = pltpu.make_async_copy(v_ref, rows(*me), loc.at[0])
        mine.start()
        first = [copy(0, me, sibling, src=v_ref)]
        first += [copy(1 + j, me, (*chip, c), src=v_ref) for j, chip in enumerate(chips)]
        for cp in first:
            cp.start()
        passed = [copy(4 + j, (*chip, c), sibling) for j, chip in enumerate(chips)]
        for j, chip in enumerate(chips):
            copy(1 + j, (*chip, c), me).wait_recv()
            passed[j].start()
        copy(0, sibling, me).wait_recv()
        for j, chip in enumerate(chips):
            copy(4 + j, (*chip, 1 - c), me).wait_recv()
        for cp in first + passed:
            cp.wait_send()
        mine.wait()

    vmem = pl.BlockSpec(memory_space=pltpu.VMEM)
    return pl.pallas_call(
        kern, name="allgather_devices", in_specs=[vmem], out_specs=vmem,
        out_shape=jax.ShapeDtypeStruct((N_DEV,) + v.shape[1:], v.dtype),
        scratch_shapes=[pltpu.SemaphoreType.DMA((7,)), pltpu.SemaphoreType.DMA((7,)), pltpu.SemaphoreType.DMA((1,))],
        compiler_params=pltpu.CompilerParams(vmem_limit_bytes=VMEM_MB << 20),
    )(v)


WEIGHTS = ['w_in', 'b_in', 'attn_sinks', 's5_a_re', 's5_a_im', 's5_b_re', 's5_b_im', 's5_c_re', 's5_c_im', 's5_d',
           's5_log_dt', 's5_glu_w', 's5_glu_b', 'lru_conv_w', 'lru_conv_b', 'lru_wx', 'lru_bx', 'lru_wa', 'lru_ba',
           'lru_a_param', 'mix_norm_g', 'w_out', 'b_out', 'ln1_g', 'ln1_b', 'ffn_w_gate', 'ffn_w_up', 'ffn_conv_w',
           'ffn_conv_b', 'ffn_w_down', 'ln2_g', 'ln2_b']
BIG = ('w_in', 'w_out', 'ffn_w_gate', 'ffn_w_up', 'ffn_w_down')
SMALL = tuple(n for n in WEIGHTS if n not in BIG)
PACK_ROWS = ROW_TILE


def _pack(arrs):
    flat = jnp.concatenate([a.reshape(-1) for a in arrs])
    unit = 128 * PACK_ROWS
    size = -(-flat.shape[0] // unit) * unit
    return jnp.pad(flat, (0, size - flat.shape[0])).reshape(-1, 128)


def _unpack(packed, shapes):
    flat = packed.reshape(-1)
    out, pos = [], 0
    for shp in shapes:
        n = math.prod(shp)
        out.append(flat[pos:pos + n].reshape(shp))
        pos += n
    return out


def _pair_reduce(name, g):
    layers, shards, rows, cols = g.shape
    r2 = rows // 2
    rt = _pick_rows(r2)
    nr = r2 // rt
    nsteps = layers * shards * nr

    def kern(c_ref, mine_ref, other_ref, o_ref, buf, send, recv, credit):
        x, y, c = _position()
        sibling = (x, y, 1 - c)
        k = pl.program_id(0) * nr + pl.program_id(1)
        slot = k % 2

        @pl.when(k >= 2)
        def _():
            pl.semaphore_wait(credit, 1)

        cp = pltpu.make_async_remote_copy(src_ref=other_ref, dst_ref=buf.at[slot], send_sem=send.at[slot],
                                          recv_sem=recv.at[slot], device_id=sibling, device_id_type=MESH)
        cp.start()
        cp.wait_recv()
        o_ref[...] = (mine_ref[...] + buf[slot]).astype(BF16)
        cp.wait_send()

        @pl.when(k + 2 < nsteps)
        def _():
            pl.semaphore_signal(credit, 1, device_id=sibling, device_id_type=MESH)

    blk = (1, rt, cols)
    grid_spec = pltpu.PrefetchScalarGridSpec(
        num_scalar_prefetch=1, grid=(layers * shards, nr),
        in_specs=[pl.BlockSpec(blk, lambda m, r, c_ref: (m, c_ref[0] * nr + r, 0)),
                  pl.BlockSpec(blk, lambda m, r, c_ref: (m, (1 - c_ref[0]) * nr + r, 0))],
        out_specs=pl.BlockSpec(blk, lambda m, r, c_ref: (m, r, 0)),
        scratch_shapes=[pltpu.VMEM((2,) + blk, F32), pltpu.SemaphoreType.DMA((2,)),
                        pltpu.SemaphoreType.DMA((2,)), pltpu.SemaphoreType.REGULAR])
    core = lax.axis_index("c").astype(jnp.int32).reshape(1)
    g3 = g.reshape(layers * shards, rows, cols)
    out = pl.pallas_call(
        kern, name=name, grid_spec=grid_spec,
        out_shape=jax.ShapeDtypeStruct((layers * shards, r2, cols), BF16),
        compiler_params=_params(("arbitrary", "arbitrary")),
    )(core, g3, g3)
    return out.reshape(layers, shards, r2, cols)


def _pair_merge(name, h):
    m, r2, cols = h.shape
    rt = _pick_rows(r2)
    nr = r2 // rt
    nsteps = m * nr

    def kern(h_ref, o_ref, buf, send, recv, credit):
        x, y, c = _position()
        sibling = (x, y, 1 - c)
        k = pl.program_id(0) * nr + pl.program_id(1)
        slot = k % 2

        @pl.when(k >= 2)
        def _():
            pl.semaphore_wait(credit, 1)

        cp = pltpu.make_async_remote_copy(src_ref=h_ref, dst_ref=buf.at[slot], send_sem=send.at[slot],
                                          recv_sem=recv.at[slot], device_id=sibling, device_id_type=MESH)
        cp.start()
        cp.wait_recv()
        o_ref[0, pl.ds(c, 1)] = h_ref[...]
        o_ref[0, pl.ds(1 - c, 1)] = buf[slot]
        cp.wait_send()

        @pl.when(k + 2 < nsteps)
        def _():
            pl.semaphore_signal(credit, 1, device_id=sibling, device_id_type=MESH)

    blk = (1, rt, cols)
    out = pl.pallas_call(
        kern, name=name, grid=(m, nr),
        in_specs=[pl.BlockSpec(blk, lambda i, r: (i, r, 0))],
        out_specs=pl.BlockSpec((1, 2, rt, cols), lambda i, r: (i, 0, r, 0)),
        out_shape=jax.ShapeDtypeStruct((m, 2, r2, cols), h.dtype),
        scratch_shapes=[pltpu.VMEM((2,) + blk, h.dtype), pltpu.SemaphoreType.DMA((2,)),
                        pltpu.SemaphoreType.DMA((2,)), pltpu.SemaphoreType.REGULAR],
        compiler_params=_params(("arbitrary", "arbitrary")),
    )(h)
    return out.reshape(m, 2 * r2, cols)


def _reduce_big(grads):
    pair = [_pair_reduce("pair_reduce_" + n, g) for n, g in zip(BIG, grads)]
    scat = _chip_scatter(pair)
    out = []
    for t, n in enumerate(BIG):
        own, got = scat[2 * t], scat[2 * t + 1]
        half = _sum_parts("chip_sum", [(own, None)] + [(got, j) for j in range(3)], own.shape)
        out.append(_pair_merge("grad_merge_" + n, half.reshape(half.shape[0], half.shape[2], half.shape[3])))
    return out


def _step(a):
    x = a['x'][0]
    target = a['loss_target'][0]
    t = x.shape[0]
    xi, yi, _ = _position()
    chip = 2 * xi + yi
    cos, sin_s = _rope_tables(t)

    gathered = _allgather_chips([_cast_bf16(a[n])[:, None] for n in BIG]
                                + [a[n][:, None] for n in ('s5_glu_w', 'lru_conv_w', 'ffn_conv_w')],
                                halved=range(len(BIG)))
    full = dict(zip(BIG + ('s5_glu_w', 'lru_conv_w', 'ffn_conv_w'), gathered))
    for n in BIG:
        layers, chips, r2, cols = full[n].shape
        full[n] = _pair_merge("weight_merge_" + n, full[n].reshape(layers * chips, r2, cols)).reshape(
            layers, chips, 2 * r2, cols)

    def layer_params(l):
        p = {n: a[n][l] for n in SMALL}
        p['layer'] = l
        p['w_in'] = full['w_in']
        p['w_out'] = full['w_out'].reshape(DEPTH, D_MODEL, D_MODEL)
        p['ffn_w_gate'] = full['ffn_w_gate']
        p['ffn_w_up'] = full['ffn_w_up']
        p['ffn_w_down'] = full['ffn_w_down']
        p['s5_glu_w'] = full['s5_glu_w'][l].reshape(D_S5, D_S5)
        p['lru_conv_w'] = full['lru_conv_w'][l].transpose(1, 0, 2).reshape(LRU_CONV, D_LRU)
        p['ffn_conv_w'] = full['ffn_conv_w'][l]
        p['ffn_conv_b'] = a['ffn_conv_b'][l].reshape(N_CHIPS, 1, FF_SH)
        return p

    params = [layer_params(l) for l in range(DEPTH)]
    derived = [_layer_weights(p) for p in params]
    saved = []
    h, hb = x, _cast_bf16(x)
    for l in range(DEPTH):
        h, hb, s = _layer_fwd(h, hb, params[l], derived[l], cos, sin_s)
        saved.append(s)
    loss_part, dr, drb, ln2_g, ln2_b, _ = _loss_head(h, target, saved[-1]['xhat2'], saved[-1]['rstd2'],
                                                     _vec(params[-1]['ln2_g']))
    loss = lax.psum(loss_part[0, 0], ("x", "y", "c"))
    grads = [None] * DEPTH
    big = {n: lax.empty((DEPTH, N_CHIPS) + a[n].shape[1:], F32) for n in BIG}
    big['w_out'] = big['w_out'].reshape(DEPTH, D_MODEL, D_MODEL)
    for l in reversed(range(DEPTH)):
        below = (saved[l - 1], params[l - 1]) if l > 0 else None
        out, grads[l], big = _layer_bwd(dr, drb, saved[l], params[l], derived[l], cos, sin_s, big, below)
        grads[l]['ln2_g'], grads[l]['ln2_b'] = ln2_g, ln2_b
        if l > 0:
            dr, drb, ln2_g, ln2_b, _ = out
        else:
            grad_x = out[None]

    def stacked(n):
        return jnp.stack([grads[l][n] for l in range(DEPTH)])

    big['w_out'] = big['w_out'].reshape(DEPTH, N_CHIPS, OUT_SH, D_MODEL)
    grad = dict(zip(BIG, _reduce_big([big[n] for n in BIG])))
    small_local = [stacked(n) for n in SMALL]
    packed = _allgather_devices(_pack(small_local)[None])
    total = _sum_parts("device_sum", [(packed, j) for j in range(N_DEV)], packed.shape[1:])
    small_sum = dict(zip(SMALL, _unpack(total, [g.shape for g in small_local])))
    for n in SMALL:
        g = small_sum[n]
        if n == 's5_glu_w':
            g = lax.dynamic_slice_in_dim(g, chip * (D_S5 // N_CHIPS), D_S5 // N_CHIPS, axis=1)
        elif n == 'lru_conv_w':
            g = lax.dynamic_slice_in_dim(g, chip * (D_LRU // N_CHIPS), D_LRU // N_CHIPS, axis=2)
        elif n == 'ffn_conv_w':
            g = lax.dynamic_index_in_dim(g, chip, axis=1, keepdims=False)
        grad[n] = g.reshape(a[n].shape)

    delta, new_m, new_v = {}, {}, {}
    for n in WEIGHTS:
        delta[n], new_m[n], new_v[n] = _adamw("adamw_" + n, a[n], grad[n], a['m_' + n], a['v_' + n])
    return (loss, grad_x, *[grad[n] for n in WEIGHTS], *[delta[n] for n in WEIGHTS],
            *[new_m[n] for n in WEIGHTS], *[new_v[n] for n in WEIGHTS])


def kernel(x, w_in, b_in, attn_sinks, s5_a_re, s5_a_im, s5_b_re, s5_b_im, s5_c_re, s5_c_im, s5_d, s5_log_dt, s5_glu_w, s5_glu_b, lru_conv_w, lru_conv_b, lru_wx, lru_bx, lru_wa, lru_ba, lru_a_param, mix_norm_g, w_out, b_out, ln1_g, ln1_b, ffn_w_gate, ffn_w_up, ffn_conv_w, ffn_conv_b, ffn_w_down, ln2_g, ln2_b, loss_target, m_w_in, m_b_in, m_attn_sinks, m_s5_a_re, m_s5_a_im, m_s5_b_re, m_s5_b_im, m_s5_c_re, m_s5_c_im, m_s5_d, m_s5_log_dt, m_s5_glu_w, m_s5_glu_b, m_lru_conv_w, m_lru_conv_b, m_lru_wx, m_lru_bx, m_lru_wa, m_lru_ba, m_lru_a_param, m_mix_norm_g, m_w_out, m_b_out, m_ln1_g, m_ln1_b, m_ffn_w_gate, m_ffn_w_up, m_ffn_conv_w, m_ffn_conv_b, m_ffn_w_down, m_ln2_g, m_ln2_b, v_w_in, v_b_in, v_attn_sinks, v_s5_a_re, v_s5_a_im, v_s5_b_re, v_s5_b_im, v_s5_c_re, v_s5_c_im, v_s5_d, v_s5_log_dt, v_s5_glu_w, v_s5_glu_b, v_lru_conv_w, v_lru_conv_b, v_lru_wx, v_lru_bx, v_lru_wa, v_lru_ba, v_lru_a_param, v_mix_norm_g, v_w_out, v_b_out, v_ln1_g, v_ln1_b, v_ffn_w_gate, v_ffn_w_up, v_ffn_conv_w, v_ffn_conv_b, v_ffn_w_down, v_ln2_g, v_ln2_b):
    return _step(dict(locals()))
```

```python
import functools
import math

import jax
import jax.numpy as jnp
from jax import lax
from jax.experimental import pallas as pl
from jax.experimental.pallas import tpu as pltpu

F32 = jnp.float32
BF16 = jnp.bfloat16
MESH = pl.DeviceIdType.MESH
ANY = pl.BlockSpec(memory_space=pl.ANY)

D_MODEL = 1024
DEPTH = 4
HEAD_DIM = 64
N_Q_HEADS = 8
N_KV_HEADS = 2
Q_PER_KV = 4
D_ATTN = 512
D_KV = 128
ATTN_BLOCK = 128
ROPE_THETA = 10000.0
D_S5 = 256
S5_GROUP = 16
S5_GROUPS = 16
S5_STATE = 64
N_STATE = S5_GROUPS * S5_STATE
D_LRU = 256
LRU_HEADS = 4
LRU_HEAD_DIM = 64
LRU_CONV = 4
LRU_C = 8.0
D_IN = 1536
D_FF = 2816
FFN_CONV = 3
N_CHIPS = 4
N_DEV = 8
IN_SH = D_IN // N_CHIPS
FF_SH = D_FF // N_CHIPS
OUT_SH = D_MODEL // N_CHIPS
ALPHA = (2 * DEPTH) ** 0.25
LN_EPS = 1e-5
RMS_EPS = 1e-6
ADAM_LR = 0.001
ADAM_B1 = 0.9
ADAM_B2 = 0.999
ADAM_EPS = 1e-08
ADAM_WD = 0.01
ADAM_STEP = 10

SUBLANES = 8
VMEM_MB = 56


def _params(sem):
    return pltpu.CompilerParams(dimension_semantics=sem, vmem_limit_bytes=VMEM_MB << 20)


def _row_tile(t, pref):
    return min(t, pref)


def _matmul(name, a, b, *, a_blk, a_map, b_blk, b_map, out_shape, o_blk, o_map, grid, dims,
            out_dtype=F32, bias=None, bias_blk=None, bias_map=None, add=None, add_scale=1.0, pair2=None,
            into=None, ln_bwd=None):
    nk = grid[2]
    acc_shape = tuple(d for d in o_blk if d is not None)
    n_in = 2 if pair2 is None else 4

    def kern(*refs):
        p = n_in
        bias_ref = add_ref = None
        if bias is not None:
            bias_ref = refs[p]
            p += 1
        if add is not None:
            add_ref = refs[p]
            p += 1
        if into is not None:
            p += 1
        if ln_bwd is not None:
            ln_in = refs[p:p + 3]
            ln_out = refs[p + 4:p + 8]
            o_ref, acc = refs[p + 3], refs[p + 8]
        else:
            o_ref, acc = refs[p], refs[p + 1]
        k = pl.program_id(2)
        first_tile = pl.program_id(0) == 0

        def product():
            r = _dot(refs[0][...].astype(BF16), refs[1][...].astype(BF16), dims)
            if pair2 is not None:
                r = r + _dot(refs[2][...].astype(BF16), refs[3][...].astype(BF16), dims)
            return r

        def finish(r):
            if bias_ref is not None:
                r = r + bias_ref[...]
            if add_ref is not None:
                r = r + add_scale * add_ref[...]
            if ln_bwd is None:
                o_ref[...] = r.astype(out_dtype)
            else:
                @pl.when(first_tile)
                def _():
                    for ref in ln_out[1:]:
                        ref[...] = jnp.zeros_like(ref)

                _ln_bwd_tile(r, ln_in[0][...], ln_in[1][...], ln_in[2][...], o_ref, *ln_out)

        if nk == 1:
            finish(product())
        else:
            @pl.when(k == 0)
            def _():
                acc[...] = jnp.zeros_like(acc)

            acc[...] += product()

            @pl.when(k == nk - 1)
            def _():
                finish(acc[...])

    in_specs = [pl.BlockSpec(a_blk, a_map), pl.BlockSpec(b_blk, b_map)]
    args = [a, b]
    if pair2 is not None:
        in_specs += [pl.BlockSpec(a_blk, a_map), pl.BlockSpec(b_blk, b_map)]
        args += list(pair2)
    if bias is not None:
        in_specs.append(pl.BlockSpec(bias_blk, bias_map))
        args.append(bias)
    if add is not None:
        in_specs.append(pl.BlockSpec(o_blk, lambda i, j, k: o_map(i, j)))
        args.append(add)
    aliases = {}
    if into is not None:
        aliases = {len(args): 0}
        in_specs.append(ANY)
        args.append(into)
    o_spec = pl.BlockSpec(o_blk, lambda i, j, k: o_map(i, j))
    out_specs, out_shapes = o_spec, jax.ShapeDtypeStruct(out_shape, out_dtype)
    semantics = ("parallel", "parallel", "arbitrary")
    if ln_bwd is not None:
        vec = pl.BlockSpec((1, o_blk[-1]), lambda i, j, k: (0, 0))
        in_specs += [o_spec, pl.BlockSpec((o_blk[0], 1), lambda i, j, k: (i, 0)), vec]
        args += list(ln_bwd)
        vshape = jax.ShapeDtypeStruct((1, o_blk[-1]), F32)
        out_specs = [o_spec, o_spec, vec, vec, vec]
        out_shapes = [out_shapes, jax.ShapeDtypeStruct(out_shape, BF16), vshape, vshape, vshape]
        semantics = ("arbitrary", "arbitrary", "arbitrary")
    return pl.pallas_call(
        kern, name=name, grid=grid, in_specs=in_specs, out_specs=out_specs, out_shape=out_shapes,
        scratch_shapes=[pltpu.VMEM(acc_shape if nk > 1 else (SUBLANES, 128), F32)],
        input_output_aliases=aliases,
        compiler_params=_params(semantics),
    )(*args)


NN = ((1,), (0,))
NT = ((1,), (1,))
TN = ((0,), (0,))
TM = 512


def _sigmoid(x):
    return 0.5 * jnp.tanh(0.5 * x) + 0.5


_GELU_C = math.sqrt(2.0 / math.pi)


def _gelu(x):
    return 0.5 * x * (1.0 + jnp.tanh(_GELU_C * (x + 0.044715 * x * x * x)))


def _gelu_grad(x):
    th = jnp.tanh(_GELU_C * (x + 0.044715 * x * x * x))
    return 0.5 * (1.0 + th) + 0.5 * x * (1.0 - th * th) * _GELU_C * (1.0 + 3 * 0.044715 * x * x)


def _rope_swap(t):
    lane = lax.broadcasted_iota(jnp.int32, t.shape, 1)
    lo = (lane % HEAD_DIM) < (HEAD_DIM // 2)
    return jnp.where(lo, pltpu.roll(t, 128 - HEAD_DIM // 2, 1), pltpu.roll(t, HEAD_DIM // 2, 1))


D_QKV = D_ATTN + 2 * D_KV
TMM = 1024


def _in_proj(xb, w_in, b_in, cos, sin_s, layer):
    t = xb.shape[0]
    tm = _row_tile(t, TMM)

    def kern(x_ref, w_ref, b_ref, c_ref, s_ref, q_ref, u_ref):
        x = x_ref[...]
        c = c_ref[...]
        s = s_ref[...]
        for j in range(N_CHIPS):
            pj = _dot(x, w_ref[j], NN) + b_ref[:, j * IN_SH:(j + 1) * IN_SH]
            for ch in range(IN_SH // 128):
                col = j * IN_SH + ch * 128
                v = pj[:, ch * 128:(ch + 1) * 128]
                if col < D_ATTN + D_KV:
                    v = v * c + _rope_swap(v) * s
                if col < D_ATTN:
                    v = v * (HEAD_DIM ** -0.5)
                if col < D_QKV:
                    q_ref[:, col:col + 128] = v.astype(BF16)
                else:
                    u_ref[:, col - D_QKV:col - D_QKV + 128] = v

    row = lambda w: pl.BlockSpec((tm, w), lambda i: (i, 0))
    return pl.pallas_call(
        kern, name="in_proj", grid=(t // tm,),
        in_specs=[row(D_MODEL), pl.BlockSpec((None, N_CHIPS, D_MODEL, IN_SH), lambda i: (layer, 0, 0, 0)),
                  pl.BlockSpec((1, D_IN), lambda i: (0, 0)), row(128), row(128)],
        out_specs=[row(D_QKV), row(D_IN - D_QKV)],
        out_shape=[jax.ShapeDtypeStruct((t, D_QKV), BF16), jax.ShapeDtypeStruct((t, D_IN - D_QKV), F32)],
        compiler_params=_params(("parallel",)),
    )(xb, w_in, b_in, cos, sin_s)


def _attn_mask(i):
    qi = lax.broadcasted_iota(jnp.int32, (ATTN_BLOCK, 2 * ATTN_BLOCK), 0)
    si = lax.broadcasted_iota(jnp.int32, (ATTN_BLOCK, 2 * ATTN_BLOCK), 1)
    diff = qi + ATTN_BLOCK - si
    return (diff >= 0) & (diff < ATTN_BLOCK) & ((si >= ATTN_BLOCK) | (i > 0))


def _row_sums(x, ones):
    hi = x.astype(BF16)
    lo = (x - hi.astype(F32)).astype(BF16)
    return _dot(hi, ones, NN) + _dot(lo, ones, NN)


def _attn_fwd(qkv, sinks):
    t = qkv.shape[0]
    nb = t // ATTN_BLOCK

    def kern(q_ref, kp_ref, kc_ref, vp_ref, vc_ref, s_ref, o_ref, l_ref):
        i = pl.program_id(0)
        si = lax.broadcasted_iota(jnp.int32, (2 * ATTN_BLOCK, ATTN_BLOCK), 0)
        qi = lax.broadcasted_iota(jnp.int32, (2 * ATTN_BLOCK, ATTN_BLOCK), 1)
        diff = qi + ATTN_BLOCK - si
        valid = (diff >= 0) & (diff < ATTN_BLOCK) & ((si >= ATTN_BLOCK) | (i > 0))
        kband = jnp.concatenate([kp_ref[...], kc_ref[...]], axis=0)
        vband = jnp.concatenate([vp_ref[...], vc_ref[...]], axis=0)
        ks = [kband[:, kh * HEAD_DIM:(kh + 1) * HEAD_DIM] for kh in range(N_KV_HEADS)]
        vs = [vband[:, kh * HEAD_DIM:(kh + 1) * HEAD_DIM] for kh in range(N_KV_HEADS)]
        scores = [_dot(ks[h // Q_PER_KV], q_ref[:, h * HEAD_DIM:(h + 1) * HEAD_DIM], NT) for h in range(N_Q_HEADS)]
        probs, lses = [], []
        for h in range(N_Q_HEADS):
            s = jnp.where(valid, scores[h], -jnp.inf)
            sink = s_ref[0:1, h:h + 1]
            m = jnp.maximum(jnp.max(s, axis=0, keepdims=True), sink)
            e = jnp.exp(s - m)
            denom = jnp.sum(e, axis=0, keepdims=True) + jnp.exp(sink - m)
            probs.append((e * (1.0 / denom)).astype(BF16))
            lses.append(m + jnp.log(denom))
        outs = [_dot(vs[h // Q_PER_KV], probs[h], TN) for h in range(N_Q_HEADS)]
        for c in range(N_Q_HEADS // 2):
            o_ref[:, c * 128:(c + 1) * 128] = jnp.concatenate([outs[2 * c], outs[2 * c + 1]], axis=0).T
        rid = lax.broadcasted_iota(jnp.int32, (N_Q_HEADS, ATTN_BLOCK), 0)
        rows = jnp.zeros((N_Q_HEADS, ATTN_BLOCK), F32)
        for h in range(N_Q_HEADS):
            rows = jnp.where(rid == h, lses[h], rows)
        rows = jnp.concatenate([rows, jnp.zeros((ATTN_BLOCK - N_Q_HEADS, ATTN_BLOCK), F32)], axis=0)
        l_ref[...] = rows.T[:, 0:N_Q_HEADS]

    blk = lambda w, f: pl.BlockSpec((ATTN_BLOCK, w), f)
    return pl.pallas_call(
        kern, name="attn_fwd", grid=(nb,),
        in_specs=[blk(512, lambda i: (i, 0)),
                  blk(128, lambda i: (jnp.maximum(i - 1, 0), 4)), blk(128, lambda i: (i, 4)),
                  blk(128, lambda i: (jnp.maximum(i - 1, 0), 5)), blk(128, lambda i: (i, 5)),
                  pl.BlockSpec((1, N_Q_HEADS), lambda i: (0, 0))],
        out_specs=[blk(512, lambda i: (i, 0)), blk(N_Q_HEADS, lambda i: (i, 0))],
        out_shape=[jax.ShapeDtypeStruct((t, D_ATTN), F32), jax.ShapeDtypeStruct((t, N_Q_HEADS), F32)],
        compiler_params=_params(("parallel",)),
    )(qkv, qkv, qkv, qkv, qkv, sinks)


def _attn_bwd(qkv, o, do, lse, sinks):
    t = qkv.shape[0]
    nb = t // ATTN_BLOCK

    def kern(q_ref, kp_ref, kc_ref, vp_ref, vc_ref, o_ref, do_ref, l_ref, s_ref,
             dq_ref, dk_ref, dv_ref, ds_ref, ck, cv):
        i = pl.program_id(0)

        @pl.when(i == 0)
        def _():
            ds_ref[...] = jnp.zeros_like(ds_ref)
            ck[...] = jnp.zeros_like(ck)
            cv[...] = jnp.zeros_like(cv)

        @pl.when(i < nb)
        def _():
            valid = _attn_mask(i)
            kband = jnp.concatenate([kp_ref[...], kc_ref[...]], axis=0)
            vband = jnp.concatenate([vp_ref[...], vc_ref[...]], axis=0)
            heads = range(N_Q_HEADS)
            sl = [slice(h * HEAD_DIM, (h + 1) * HEAD_DIM) for h in heads]
            ks = [kband[:, kh * HEAD_DIM:(kh + 1) * HEAD_DIM] for kh in range(N_KV_HEADS)]
            vs = [vband[:, kh * HEAD_DIM:(kh + 1) * HEAD_DIM] for kh in range(N_KV_HEADS)]
            qs = [q_ref[:, sl[h]] for h in heads]
            d_os = [do_ref[:, sl[h]] for h in heads]
            dobs = [d.astype(BF16) for d in d_os]
            scores = [_dot(qs[h], ks[h // Q_PER_KV], NT) for h in heads]
            dps = [_dot(dobs[h], vs[h // Q_PER_KV], NT) for h in heads]
            col_head = lax.broadcasted_iota(jnp.int32, (D_ATTN, 128), 0) // HEAD_DIM
            head_ones = (col_head == lax.broadcasted_iota(jnp.int32, (D_ATTN, 128), 1)).astype(BF16)
            deltas = _row_sums(do_ref[...] * o_ref[...], head_ones)
            pbs, dscs = [], []
            for h in heads:
                lse_h = l_ref[:, h:h + 1]
                p = jnp.where(valid, jnp.exp(scores[h] - lse_h), 0.0)
                delta = deltas[:, h:h + 1]
                pbs.append(p.astype(BF16))
                dscs.append((p * (dps[h] - delta)).astype(BF16))
                psink = jnp.exp(s_ref[0:1, h:h + 1] - lse_h)
                ds_ref[0:1, h:h + 1] += -jnp.sum(psink * delta, axis=0, keepdims=True)
            dqs = [_dot(dscs[h], ks[h // Q_PER_KV], NN) for h in heads]
            dkb = [sum(_dot(dscs[h], qs[h], TN) for h in heads if h // Q_PER_KV == kh) for kh in range(N_KV_HEADS)]
            dvb = [sum(_dot(pbs[h], dobs[h], TN) for h in heads if h // Q_PER_KV == kh) for kh in range(N_KV_HEADS)]
            for h in heads:
                dq_ref[:, sl[h]] = dqs[h]
            dk_band = jnp.concatenate(dkb, axis=1)
            dv_band = jnp.concatenate(dvb, axis=1)
            dk_ref[...] = ck[...] + dk_band[:ATTN_BLOCK]
            dv_ref[...] = cv[...] + dv_band[:ATTN_BLOCK]
            ck[...] = dk_band[ATTN_BLOCK:]
            cv[...] = dv_band[ATTN_BLOCK:]

        @pl.when(i == nb)
        def _():
            dk_ref[...] = ck[...]
            dv_ref[...] = cv[...]

    blk = lambda w, f: pl.BlockSpec((ATTN_BLOCK, w), f)
    cur = lambda i: jnp.minimum(i, nb - 1)
    prev = lambda i: jnp.clip(i - 1, 0, nb - 1)
    return pl.pallas_call(
        kern, name="attn_bwd", grid=(nb + 1,),
        in_specs=[blk(512, lambda i: (cur(i), 0)),
                  blk(128, lambda i: (prev(i), 4)), blk(128, lambda i: (cur(i), 4)),
                  blk(128, lambda i: (prev(i), 5)), blk(128, lambda i: (cur(i), 5)),
                  blk(512, lambda i: (cur(i), 0)), blk(512, lambda i: (cur(i), 0)),
                  blk(N_Q_HEADS, lambda i: (cur(i), 0)),
                  pl.BlockSpec((1, N_Q_HEADS), lambda i: (0, 0))],
        out_specs=[blk(512, lambda i: (cur(i), 0)), blk(128, lambda i: (prev(i), 0)),
                   blk(128, lambda i: (prev(i), 0)), pl.BlockSpec((1, N_Q_HEADS), lambda i: (0, 0))],
        out_shape=[jax.ShapeDtypeStruct((t, D_ATTN), F32), jax.ShapeDtypeStruct((t, D_KV), F32),
                   jax.ShapeDtypeStruct((t, D_KV), F32), jax.ShapeDtypeStruct((1, N_Q_HEADS), F32)],
        scratch_shapes=[pltpu.VMEM((ATTN_BLOCK, D_KV), F32), pltpu.VMEM((ATTN_BLOCK, D_KV), F32)],
        compiler_params=_params(("arbitrary",)),
    )(qkv, qkv, qkv, qkv, qkv, o, do, lse, sinks)


_GROUPS = ((0, D_ATTN), (D_ATTN, D_ATTN + D_S5), (D_ATTN + D_S5, D_MODEL))


def _mix_out_ln(ya, ys, yl, mg, w_out, b_out, xres, g, b, layer):
    t = xres.shape[0]
    tm = _row_tile(t, TM)

    def kern(a_ref, s_ref, l_ref, mg_ref, w_ref, bias_ref, x_ref, g_ref, b_ref, m_ref, y_ref, yb_ref, h_ref, r_ref):
        for (lo, hi), ref in zip(_GROUPS, (a_ref, s_ref, l_ref)):
            v = ref[...]
            n = v * lax.rsqrt(jnp.mean(v * v, axis=-1, keepdims=True) + RMS_EPS)
            m_ref[:, lo:hi] = (n * mg_ref[:, lo:hi]).astype(BF16)
        r = ALPHA * x_ref[...] + _dot(m_ref[...], w_ref[...], NN) + bias_ref[...]
        mu = jnp.mean(r, axis=-1, keepdims=True)
        xc = r - mu
        rstd = lax.rsqrt(jnp.mean(xc * xc, axis=-1, keepdims=True) + LN_EPS)
        xhat = xc * rstd
        h_ref[...] = xhat
        r_ref[...] = rstd
        y = xhat * g_ref[...] + b_ref[...]
        y_ref[...] = y
        yb_ref[...] = y.astype(BF16)

    rowb = lambda w: pl.BlockSpec((tm, w), lambda i: (i, 0))
    row = rowb(D_MODEL)
    vec = pl.BlockSpec((1, D_MODEL), lambda i: (0, 0))
    big = lambda dt: jax.ShapeDtypeStruct((t, D_MODEL), dt)
    return pl.pallas_call(
        kern, name="mix_out_ln", grid=(t // tm,),
        in_specs=[rowb(D_ATTN), rowb(D_S5), rowb(D_LRU), vec,
                  pl.BlockSpec((None, D_MODEL, D_MODEL), lambda i: (layer, 0, 0)), vec, row, vec, vec],
        out_specs=[row, row, row, row, pl.BlockSpec((tm, 1), lambda i: (i, 0))],
        out_shape=[big(BF16), big(F32), big(BF16), big(F32), jax.ShapeDtypeStruct((t, 1), F32)],
        compiler_params=_params(("parallel",)),
    )(ya, ys, yl, mg, w_out, b_out, xres, g, b)


def _d_mix_rms(dx1, xhat, rstd, lg, w_out, ya, ys, yl, mg, layer):
    t = dx1.shape[0]
    tm = _row_tile(t, TM)

    def kern(d_ref, h_ref, r_ref, lg_ref, w_ref, a_ref, s_ref, l_ref, g_ref,
             dr_ref, drb_ref, dlg_ref, dlb_ref, sr_ref, da_ref, ds_ref, dl_ref, dg_ref):
        @pl.when(pl.program_id(0) == 0)
        def _():
            for ref in (dlg_ref, dlb_ref, sr_ref, dg_ref):
                ref[...] = jnp.zeros_like(ref)

        _ln_bwd_tile(d_ref[...], h_ref[...], r_ref[...], lg_ref[...], dr_ref, drb_ref, dlg_ref, dlb_ref, sr_ref)
        dmix = _dot(drb_ref[...], w_ref[...], NT)
        for (lo, hi), ref, out in zip(_GROUPS, (a_ref, s_ref, l_ref), (da_ref, ds_ref, dl_ref)):
            v = ref[...]
            rstd = lax.rsqrt(jnp.mean(v * v, axis=-1, keepdims=True) + RMS_EPS)
            n = v * rstd
            dm = dmix[:, lo:hi]
            dg_ref[:, lo:hi] += jnp.sum(dm * n, axis=0, keepdims=True)
            dn = dm * g_ref[:, lo:hi]
            out[...] = rstd * (dn - n * jnp.mean(dn * n, axis=-1, keepdims=True))

    rowb = lambda w: pl.BlockSpec((tm, w), lambda i: (i, 0))
    vec = pl.BlockSpec((1, D_MODEL), lambda i: (0, 0))
    vshape = jax.ShapeDtypeStruct((1, D_MODEL), F32)
    return pl.pallas_call(
        kern, name="d_mix_rms", grid=(t // tm,),
        in_specs=[rowb(D_MODEL), rowb(D_MODEL), pl.BlockSpec((tm, 1), lambda i: (i, 0)), vec,
                  pl.BlockSpec((None, D_MODEL, D_MODEL), lambda i: (layer, 0, 0)),
                  rowb(D_ATTN), rowb(D_S5), rowb(D_LRU), vec],
        out_specs=[rowb(D_MODEL), rowb(D_MODEL), vec, vec, vec, rowb(D_ATTN), rowb(D_S5), rowb(D_LRU), vec],
        out_shape=[jax.ShapeDtypeStruct((t, D_MODEL), F32), jax.ShapeDtypeStruct((t, D_MODEL), BF16),
                   vshape, vshape, vshape, jax.ShapeDtypeStruct((t, D_ATTN), F32),
                   jax.ShapeDtypeStruct((t, D_S5), F32), jax.ShapeDtypeStruct((t, D_LRU), F32), vshape],
        compiler_params=_params(("arbitrary",)),
    )(dx1, xhat, rstd, lg, w_out, ya, ys, yl, mg)


def _matmul_ln(name, a, w, bias, xres, g, b, a_blk, a_map, w_blk, parts, layer):
    t = xres.shape[0]
    tm = a_blk[-2]

    def kern(a_ref, w_ref, bias_ref, x_ref, g_ref, b_ref, y_ref, yb_ref, h_ref, r_ref):
        if parts is None:
            f = _dot(a_ref[...], w_ref[...], NN)
        else:
            f = sum(_dot(a_ref[j], w_ref[j], NN) for j in range(parts))
        r = ALPHA * x_ref[...] + f + bias_ref[...]
        mu = jnp.mean(r, axis=-1, keepdims=True)
        xc = r - mu
        rstd = lax.rsqrt(jnp.mean(xc * xc, axis=-1, keepdims=True) + LN_EPS)
        xhat = xc * rstd
        h_ref[...] = xhat
        r_ref[...] = rstd
        y = xhat * g_ref[...] + b_ref[...]
        y_ref[...] = y
        yb_ref[...] = y.astype(BF16)

    row = pl.BlockSpec((tm, D_MODEL), lambda i: (i, 0))
    vec = pl.BlockSpec((1, D_MODEL), lambda i: (0, 0))
    big = lambda dt: jax.ShapeDtypeStruct((t, D_MODEL), dt)
    return pl.pallas_call(
        kern, name=name, grid=(t // tm,),
        in_specs=[pl.BlockSpec(a_blk, a_map), pl.BlockSpec((None,) + w_blk, lambda i: (layer,) + (0,) * len(w_blk)), vec, row, vec, vec],
        out_specs=[row, row, row, pl.BlockSpec((tm, 1), lambda i: (i, 0))],
        out_shape=[big(F32), big(BF16), big(F32), jax.ShapeDtypeStruct((t, 1), F32)],
        compiler_params=_params(("parallel",)),
    )(a, w, bias, xres, g, b)


def _ln_bwd(dy, xhat, rstd, g):
    t = dy.shape[0]
    tm = _row_tile(t, TM)

    def kern(d_ref, h_ref, r_ref, g_ref, dr_ref, drb_ref, dg_ref, db_ref, sr_ref):
        @pl.when(pl.program_id(0) == 0)
        def _():
            dg_ref[...] = jnp.zeros_like(dg_ref)
            db_ref[...] = jnp.zeros_like(db_ref)
            sr_ref[...] = jnp.zeros_like(sr_ref)

        d = d_ref[...]
        xhat = h_ref[...]
        dg_ref[...] += jnp.sum(d * xhat, axis=0, keepdims=True)
        db_ref[...] += jnp.sum(d, axis=0, keepdims=True)
        dh = d * g_ref[...]
        dr = r_ref[...] * (dh - jnp.mean(dh, axis=-1, keepdims=True)
                           - xhat * jnp.mean(dh * xhat, axis=-1, keepdims=True))
        dr_ref[...] = dr
        drb_ref[...] = dr.astype(BF16)
        sr_ref[...] += jnp.sum(dr, axis=0, keepdims=True)

    row = pl.BlockSpec((tm, D_MODEL), lambda i: (i, 0))
    vec = pl.BlockSpec((1, D_MODEL), lambda i: (0, 0))
    vshape = jax.ShapeDtypeStruct((1, D_MODEL), F32)
    return pl.pallas_call(
        kern, name="ln_bwd", grid=(t // tm,),
        in_specs=[row, row, pl.BlockSpec((tm, 1), lambda i: (i, 0)), vec],
        out_specs=[row, row, vec, vec, vec],
        out_shape=[jax.ShapeDtypeStruct((t, D_MODEL), F32), jax.ShapeDtypeStruct((t, D_MODEL), BF16),
                   vshape, vshape, vshape],
        compiler_params=_params(("arbitrary",)),
    )(dy, xhat, rstd, g)


def _ln_bwd_tile(d, xhat, rstd, g, dr_ref, drb_ref, dg_ref, db_ref, sr_ref):
    dg_ref[...] += jnp.sum(d * xhat, axis=0, keepdims=True)
    db_ref[...] += jnp.sum(d, axis=0, keepdims=True)
    dh = d * g
    dr = rstd * (dh - jnp.mean(dh, axis=-1, keepdims=True) - xhat * jnp.mean(dh * xhat, axis=-1, keepdims=True))
    dr_ref[...] = dr
    drb_ref[...] = dr.astype(BF16)
    sr_ref[...] += jnp.sum(dr, axis=0, keepdims=True)


def _loss_head(y, target, xhat, rstd, g):
    t = y.shape[0]
    tm = _row_tile(t, TM)

    def kern(y_ref, t_ref, h_ref, r_ref, g_ref, l_ref, dr_ref, drb_ref, dg_ref, db_ref, sr_ref):
        @pl.when(pl.program_id(0) == 0)
        def _():
            for ref in (l_ref, dg_ref, db_ref, sr_ref):
                ref[...] = jnp.zeros_like(ref)

        err = y_ref[...] - t_ref[...]
        part = jnp.sum(jnp.sum(err * err, axis=-1, keepdims=True), axis=0, keepdims=True)
        l_ref[...] += jnp.broadcast_to(part * (0.5 / D_MODEL), l_ref.shape)
        _ln_bwd_tile(err * (1.0 / D_MODEL), h_ref[...], r_ref[...], g_ref[...], dr_ref, drb_ref, dg_ref, db_ref, sr_ref)

    row = pl.BlockSpec((tm, D_MODEL), lambda i: (i, 0))
    vec = pl.BlockSpec((1, D_MODEL), lambda i: (0, 0))
    vshape = jax.ShapeDtypeStruct((1, D_MODEL), F32)
    return pl.pallas_call(
        kern, name="loss_head", grid=(t // tm,),
        in_specs=[row, row, row, pl.BlockSpec((tm, 1), lambda i: (i, 0)), vec],
        out_specs=[pl.BlockSpec((1, 128), lambda i: (0, 0)), row, row, vec, vec, vec],
        out_shape=[jax.ShapeDtypeStruct((1, 128), F32), jax.ShapeDtypeStruct((t, D_MODEL), F32),
                   jax.ShapeDtypeStruct((t, D_MODEL), BF16), vshape, vshape, vshape],
        compiler_params=_params(("arbitrary",)),
    )(y, target, xhat, rstd, g)


HALO = 8


def _ffn_hidden_fwd(xb, wg, wu, cw, cb, layer):
    t = xb.shape[0]
    tm = _row_tile(t, TMM)

    def kern(x_ref, wg_ref, wu_ref, cw_ref, cb_ref, g_ref, c_ref, u_ref, h_ref, ext):
        @pl.when(pl.program_id(1) == 0)
        def _():
            ext[0:HALO, :] = jnp.zeros((HALO, FF_SH), F32)

        x = x_ref[...]
        gb = _dot(x, wg_ref[...], NN).astype(BF16)
        ub = _dot(x, wu_ref[...], NN).astype(BF16)
        g_ref[...] = gb
        u_ref[...] = ub
        g = gb.astype(F32)
        w = [cw_ref[k:k + 1, :] for k in range(FFN_CONV)]
        body = cb_ref[...] + w[2] * g + w[1] * pltpu.roll(g, 1, 0) + w[0] * pltpu.roll(g, 2, 0)
        ext[HALO:, :] = g[0:HALO, :]
        head = cb_ref[...] + sum(w[k] * ext[pl.ds(HALO - (FFN_CONV - 1) + k, HALO), :] for k in range(FFN_CONV))
        gcb = jnp.concatenate([head, body[HALO:, :]], axis=0).astype(BF16)
        c_ref[...] = gcb
        gc = gcb.astype(F32)
        h_ref[...] = (gc * _sigmoid(gc) * ub.astype(F32)).astype(BF16)
        ext[0:HALO, :] = g[tm - HALO:, :]

    col = pl.BlockSpec((None, tm, FF_SH), lambda j, i: (j, i, 0))
    wspec = pl.BlockSpec((None, None, D_MODEL, FF_SH), lambda j, i: (layer, j, 0, 0))
    big = jax.ShapeDtypeStruct((N_CHIPS, t, FF_SH), BF16)
    return pl.pallas_call(
        kern, name="ffn_hidden_fwd", grid=(N_CHIPS, t // tm),
        in_specs=[pl.BlockSpec((tm, D_MODEL), lambda j, i: (i, 0)), wspec, wspec,
                  pl.BlockSpec((None, FFN_CONV, FF_SH), lambda j, i: (j, 0, 0)),
                  pl.BlockSpec((None, 1, FF_SH), lambda j, i: (j, 0, 0))],
        out_specs=[col, col, col, col], out_shape=[big, big, big, big],
        scratch_shapes=[pltpu.VMEM((2 * HALO, FF_SH), F32)],
        compiler_params=_params(("parallel", "arbitrary")),
    )(xb, wg, wu, cw, cb)


def _ffn_hidden_bwd(drb, gpre, gconv, up, wd, cw, layer):
    t = drb.shape[0]
    tm = _row_tile(t, TMM)
    nt = t // tm
    rb = lambda i: nt - 1 - i

    def kern(d_ref, g_ref, c_ref, u_ref, wd_ref, cw_ref, du_ref, dg_ref, dw_ref, db_ref, ext):
        @pl.when(pl.program_id(1) == 0)
        def _():
            dw_ref[...] = jnp.zeros_like(dw_ref)
            db_ref[...] = jnp.zeros_like(db_ref)
            ext[HALO:, :] = jnp.zeros((HALO, FF_SH), F32)

        dh = _dot(d_ref[...], wd_ref[...], NT)
        gc = c_ref[...].astype(F32)
        sg = _sigmoid(gc)
        du_ref[...] = (dh * (gc * sg)).astype(BF16)
        dgc = dh * u_ref[...].astype(F32) * (sg * (1.0 + gc * (1.0 - sg)))
        db_ref[...] += jnp.sum(dgc, axis=0, keepdims=True)
        g = g_ref[...].astype(F32)
        w = [cw_ref[k:k + 1, :] for k in range(FFN_CONV)]
        taps = [pltpu.roll(dgc, tm - 2, 0), pltpu.roll(dgc, tm - 1, 0), dgc]
        body = sum(w[k] * taps[k] for k in range(FFN_CONV))
        last = slice(tm - HALO, tm)
        ext[0:HALO, :] = dgc[last, :]
        tail_taps = [ext[pl.ds(FFN_CONV - 1 - k, HALO), :] for k in range(FFN_CONV)]
        tail = sum(w[k] * tail_taps[k] for k in range(FFN_CONV))
        dg_ref[...] = jnp.concatenate([body[0:tm - HALO, :], tail], axis=0).astype(BF16)
        for k in range(FFN_CONV):
            dw_ref[k:k + 1, :] += (jnp.sum(g * taps[k], axis=0, keepdims=True)
                                   + jnp.sum(g[last, :] * (tail_taps[k] - taps[k][last, :]), axis=0, keepdims=True))
        ext[HALO:, :] = dgc[0:HALO, :]

    col = pl.BlockSpec((None, tm, FF_SH), lambda j, i: (j, rb(i), 0))
    cws = pl.BlockSpec((None, FFN_CONV, FF_SH), lambda j, i: (j, 0, 0))
    cbs = pl.BlockSpec((None, 1, FF_SH), lambda j, i: (j, 0, 0))
    big = jax.ShapeDtypeStruct((N_CHIPS, t, FF_SH), BF16)
    return pl.pallas_call(
        kern, name="ffn_hidden_bwd", grid=(N_CHIPS, nt),
        in_specs=[pl.BlockSpec((tm, D_MODEL), lambda j, i: (rb(i), 0)), col, col, col,
                  pl.BlockSpec((None, None, FF_SH, D_MODEL), lambda j, i: (layer, j, 0, 0)), cws],
        out_specs=[col, col, cws, cbs],
        out_shape=[big, big, jax.ShapeDtypeStruct((N_CHIPS, FFN_CONV, FF_SH), F32),
                   jax.ShapeDtypeStruct((N_CHIPS, 1, FF_SH), F32)],
        scratch_shapes=[pltpu.VMEM((2 * HALO, FF_SH), F32)],
        compiler_params=_params(("parallel", "arbitrary")),
    )(drb, gpre, gconv, up, wd, cw)


def _s5_coefs(ar, ai, reverse):
    if reverse:
        ai = -ai
    pw = [(ar, ai)]
    for _ in range(SUBLANES - 1):
        pr, pi = pw[-1]
        pw.append((pr * ar - pi * ai, pr * ai + pi * ar))
    rows = jnp.arange(SUBLANES)[:, None]
    out = []
    for s in (1, 2, 4):
        keep = (rows + s <= SUBLANES - 1) if reverse else (rows >= s)
        out += [jnp.where(keep, pw[s - 1][0][None], 0.0), jnp.where(keep, pw[s - 1][1][None], 0.0)]
    order = list(range(SUBLANES - 1, -1, -1)) if reverse else list(range(SUBLANES))
    out += [jnp.stack([pw[k][0] for k in order]), jnp.stack([pw[k][1] for k in order])]
    return jnp.stack(out).astype(F32)


def _s5_scan(buf, coef_ref, carry, tm, reverse):
    n8 = tm // SUBLANES

    def body(it, c):
        cre, cim = c
        blk = (n8 - 1 - it) if reverse else it
        r0 = pl.multiple_of(blk * SUBLANES, SUBLANES)
        xre = buf[pl.ds(r0, SUBLANES), 0:N_STATE]
        xim = buf[pl.ds(r0, SUBLANES), N_STATE:]
        for idx, s in enumerate((1, 2, 4)):
            sh = (SUBLANES - s) if reverse else s
            sre = pltpu.roll(xre, sh, 0)
            sim = pltpu.roll(xim, sh, 0)
            are = coef_ref[2 * idx]
            aim = coef_ref[2 * idx + 1]
            xre, xim = xre + are * sre - aim * sim, xim + are * sim + aim * sre
        pre = coef_ref[6]
        pim = coef_ref[7]
        hre = xre + pre * cre - pim * cim
        him = xim + pre * cim + pim * cre
        buf[pl.ds(r0, SUBLANES), 0:N_STATE] = hre
        buf[pl.ds(r0, SUBLANES), N_STATE:] = him
        row = 0 if reverse else SUBLANES - 1
        return (jnp.broadcast_to(hre[row:row + 1], (SUBLANES, N_STATE)),
                jnp.broadcast_to(him[row:row + 1], (SUBLANES, N_STATE)))

    cre, cim = lax.fori_loop(0, n8, body, (carry[:, 0:N_STATE], carry[:, N_STATE:]))
    carry[:, 0:N_STATE] = cre
    carry[:, N_STATE:] = cim


def _real_scan(abuf, bbuf, carry, tm, reverse):
    n8 = tm // SUBLANES
    width = bbuf.shape[1]

    def body(it, c):
        blk = (n8 - 1 - it) if reverse else it
        r0 = pl.multiple_of(blk * SUBLANES, SUBLANES)
        a = abuf[pl.ds(r0, SUBLANES), :]
        b = bbuf[pl.ds(r0, SUBLANES), :]
        rows = lax.broadcasted_iota(jnp.int32, (SUBLANES, width), 0)
        for s in (1, 2, 4):
            sh = (SUBLANES - s) if reverse else s
            keep = (rows + s <= SUBLANES - 1) if reverse else (rows >= s)
            sa = pltpu.roll(a, sh, 0)
            sb = pltpu.roll(b, sh, 0)
            b = b + a * jnp.where(keep, sb, 0.0)
            a = a * jnp.where(keep, sa, 1.0)
        h = b + a * c
        bbuf[pl.ds(r0, SUBLANES), :] = h
        row = 0 if reverse else SUBLANES - 1
        return jnp.broadcast_to(h[row:row + 1], (SUBLANES, width))

    carry[...] = lax.fori_loop(0, n8, body, carry[...])


def _dot(a, b, dims):
    return lax.dot_general(a, b, (dims, ((), ())), preferred_element_type=F32)


TS5 = 512
HALO16 = 16


def _s5_fwd(proj, bmat, coef, cmat, dvec, gw, gb):
    t = proj.shape[0]
    tm = _row_tile(t, TS5)

    def kern(u_ref, b_ref, coef_ref, c_ref, d_ref, gw_ref, gb_ref, h_ref, y_ref, hbuf, carry):
        @pl.when(pl.program_id(0) == 0)
        def _():
            carry[...] = jnp.zeros_like(carry)

        u = u_ref[...]
        hbuf[...] = _dot(u.astype(BF16), b_ref[...], NN)
        _s5_scan(hbuf, coef_ref, carry, tm, False)
        hb = hbuf[...].astype(BF16)
        h_ref[...] = hb
        y = _dot(hb, c_ref[...], NN) + d_ref[...] * u
        ys = _gelu(y)
        z = _dot(ys.astype(BF16), gw_ref[...], NN) + gb_ref[...]
        y_ref[...] = ys * _sigmoid(z)

    full = lambda shp: pl.BlockSpec(shp, lambda i: (0,) * len(shp))
    return pl.pallas_call(
        kern, name="s5_fwd", grid=(t // tm,),
        in_specs=[pl.BlockSpec((tm, D_S5), lambda i: (i, 0)), full((D_S5, 2 * N_STATE)),
                  full((8, SUBLANES, N_STATE)), full((2 * N_STATE, D_S5)), full((1, D_S5)),
                  full((D_S5, D_S5)), full((1, D_S5))],
        out_specs=[pl.BlockSpec((tm, 2 * N_STATE), lambda i: (i, 0)), pl.BlockSpec((tm, D_S5), lambda i: (i, 0))],
        out_shape=[jax.ShapeDtypeStruct((t, 2 * N_STATE), BF16), jax.ShapeDtypeStruct((t, D_S5), F32)],
        scratch_shapes=[pltpu.VMEM((tm, 2 * N_STATE), F32), pltpu.VMEM((SUBLANES, 2 * N_STATE), F32)],
        compiler_params=_params(("arbitrary",)),
    )(proj, bmat, coef, cmat, dvec, gw, gb)


def _s5_bwd(proj, h, dout, bmat, coef_b, cmat, dvec, gw, gb):
    t = proj.shape[0]
    tm = _row_tile(t, TS5)
    nt = t // tm
    rb = lambda i: nt - 1 - i

    def kern(u_ref, h_ref, hp_ref, d_ref, b_ref, coef_ref, c_ref, dv_ref, gw_ref, gb_ref,
             du_ref, dc_ref, db_ref, da_ref, dd_ref, dgw_ref, dgb_ref, gbuf, hext, carry):
        i = pl.program_id(0)

        @pl.when(i == 0)
        def _():
            carry[...] = jnp.zeros_like(carry)
            for r in (dc_ref, db_ref, da_ref, dd_ref, dgw_ref, dgb_ref):
                r[...] = jnp.zeros_like(r)

        u = u_ref[...]
        hb = h_ref[...]
        y = _dot(hb, c_ref[...], NN) + dv_ref[...] * u
        ys = _gelu(y)
        ysb = ys.astype(BF16)
        sg = _sigmoid(_dot(ysb, gw_ref[...], NN) + gb_ref[...])
        d_o = d_ref[...]
        dz = d_o * ys * sg * (1.0 - sg)
        dzb = dz.astype(BF16)
        dys = d_o * sg + _dot(dzb, gw_ref[...], NT)
        dgw_ref[...] += _dot(ysb, dzb, TN)
        dgb_ref[...] += jnp.sum(dz, axis=0, keepdims=True)
        dy = dys * _gelu_grad(y)
        dd_ref[...] += jnp.sum(dy * u, axis=0, keepdims=True)
        dyb = dy.astype(BF16)
        dc_ref[...] += _dot(hb, dyb, TN)
        gbuf[...] = _dot(dyb, c_ref[...], NT)
        _s5_scan(gbuf, coef_ref, carry, tm, True)
        g = gbuf[...]
        first = jnp.where(i < nt - 1, hp_ref[HALO16 - 1:HALO16, :].astype(F32), 0.0)
        hext[SUBLANES - 1:SUBLANES, :] = first
        hext[SUBLANES:, :] = hb.astype(F32)
        hprev = hext[pl.ds(SUBLANES - 1, tm), :]
        gre, gim = g[:, 0:N_STATE], g[:, N_STATE:]
        pre, pim = hprev[:, 0:N_STATE], hprev[:, N_STATE:]
        da_ref[0:1, :] += jnp.sum(gre * pre + gim * pim, axis=0, keepdims=True)
        da_ref[1:2, :] += jnp.sum(gim * pre - gre * pim, axis=0, keepdims=True)
        gb16 = g.astype(BF16)
        db_ref[...] += _dot(u.astype(BF16), gb16, TN)
        du_ref[...] = dy * dv_ref[...] + _dot(gb16, b_ref[...], NT)

    full = lambda shp: pl.BlockSpec(shp, lambda i: (0,) * len(shp))
    shape = lambda shp: jax.ShapeDtypeStruct(shp, F32)
    return pl.pallas_call(
        kern, name="s5_bwd", grid=(nt,),
        in_specs=[pl.BlockSpec((tm, D_S5), lambda i: (rb(i), 0)),
                  pl.BlockSpec((tm, 2 * N_STATE), lambda i: (rb(i), 0)),
                  pl.BlockSpec((HALO16, 2 * N_STATE), lambda i: (jnp.maximum(rb(i) * (tm // HALO16) - 1, 0), 0)),
                  pl.BlockSpec((tm, D_S5), lambda i: (rb(i), 0)),
                  full((D_S5, 2 * N_STATE)), full((8, SUBLANES, N_STATE)), full((2 * N_STATE, D_S5)),
                  full((1, D_S5)), full((D_S5, D_S5)), full((1, D_S5))],
        out_specs=[pl.BlockSpec((tm, D_S5), lambda i: (rb(i), 0)), full((2 * N_STATE, D_S5)),
                   full((D_S5, 2 * N_STATE)), full((2, N_STATE)), full((1, D_S5)), full((D_S5, D_S5)),
                   full((1, D_S5))],
        out_shape=[shape((t, D_S5)), shape((2 * N_STATE, D_S5)), shape((D_S5, 2 * N_STATE)),
                   shape((2, N_STATE)), shape((1, D_S5)), shape((D_S5, D_S5)), shape((1, D_S5))],
        scratch_shapes=[pltpu.VMEM((tm, 2 * N_STATE), F32), pltpu.VMEM((tm + SUBLANES, 2 * N_STATE), F32),
                        pltpu.VMEM((SUBLANES, 2 * N_STATE), F32)],
        compiler_params=_params(("arbitrary",)),
    )(proj, h, h, dout, bmat, coef_b, cmat, dvec, gw, gb)


def _lru_gates(ext, x_ref, p_ref, cw_ref, cb_ref, wx_ref, bx_ref, wa_ref, ba_ref, ap_ref, first_tile, row0, tm):
    ext[0:HALO, :] = jnp.where(first_tile, 0.0, p_ref[...])
    ext[HALO:, :] = x_ref[...]
    taps = [ext[pl.ds(HALO - (LRU_CONV - 1) + k, tm), :] for k in range(LRU_CONV)]
    xc = cb_ref[...] + sum(cw_ref[k:k + 1, :] * taps[k] for k in range(LRU_CONV))
    xcb = xc.astype(BF16)
    gx = _sigmoid(_dot(xcb, wx_ref[...], NN) + bx_ref[...])
    ga = _sigmoid(_dot(xcb, wa_ref[...], NN) + ba_ref[...])
    z = -ap_ref[...]
    sp = jnp.maximum(z, 0.0) + jnp.log(1.0 + jnp.exp(-jnp.abs(z)))
    log_a = -LRU_C * ga * sp
    a = jnp.exp(log_a)
    tok = row0 + lax.broadcasted_iota(jnp.int32, a.shape, 0)
    is0 = tok == 0
    mult = jnp.where(is0, 1.0, jnp.sqrt(1.0 - jnp.exp(2.0 * log_a)))
    return taps, xc, xcb, gx, ga, sp, a, mult, is0


def _lru_specs(tm, blk_of):
    col = lambda cidx: pl.BlockSpec((tm, D_LRU), lambda i: (blk_of(i), cidx))
    prev = lambda cidx: pl.BlockSpec((HALO, D_LRU), lambda i: (jnp.maximum(blk_of(i) * (tm // HALO) - 1, 0), cidx))
    full = lambda shp: pl.BlockSpec(shp, lambda i: (0,) * len(shp))
    wts = [full((LRU_CONV, D_LRU)), full((1, D_LRU)), full((D_LRU, D_LRU)), full((1, D_LRU)),
           full((D_LRU, D_LRU)), full((1, D_LRU)), full((1, D_LRU))]
    return col, prev, full, wts


def _lru_fwd(proj, cw, cb, wx, bx, wa, ba, ap):
    t = proj.shape[0]
    tm = _row_tile(t, TMM)

    def kern(x_ref, p_ref, g_ref, cw_ref, cb_ref, wx_ref, bx_ref, wa_ref, ba_ref, ap_ref,
             y_ref, h_ref, ext, abuf, carry):
        i = pl.program_id(0)

        @pl.when(i == 0)
        def _():
            carry[...] = jnp.zeros_like(carry)

        _, xc, _, gx, _, _, a, mult, _ = _lru_gates(ext, x_ref, p_ref, cw_ref, cb_ref, wx_ref, bx_ref, wa_ref,
                                                    ba_ref, ap_ref, i == 0, i * tm, tm)
        abuf[...] = a
        h_ref[...] = mult * gx * xc
        _real_scan(abuf, h_ref, carry, tm, False)
        y_ref[...] = h_ref[...] * _gelu(g_ref[...])

    col, prev, full, wts = _lru_specs(tm, lambda i: i)
    out = pl.BlockSpec((tm, D_LRU), lambda i: (i, 0))
    return pl.pallas_call(
        kern, name="lru_fwd", grid=(t // tm,),
        in_specs=[col(1), prev(1), col(2)] + wts, out_specs=[out, out],
        out_shape=[jax.ShapeDtypeStruct((t, D_LRU), F32), jax.ShapeDtypeStruct((t, D_LRU), F32)],
        scratch_shapes=[pltpu.VMEM((tm + HALO, D_LRU), F32), pltpu.VMEM((tm, D_LRU), F32),
                        pltpu.VMEM((SUBLANES, D_LRU), F32)],
        compiler_params=_params(("arbitrary",)),
    )(proj, proj, proj, cw, cb, wx, bx, wa, ba, ap)


def _lru_bwd(proj, h, dout, cw, cb, wx, bx, wa, ba, ap):
    t = proj.shape[0]
    tm = _row_tile(t, TMM)
    nt = t // tm
    rb = lambda i: nt - 1 - i

    def kern(x_ref, p_ref, g_ref, h_ref, hp_ref, d_ref, cw_ref, cb_ref, wx_ref, bx_ref, wa_ref, ba_ref, ap_ref,
             dxc_ref, dg_ref, dcw_ref, dcb_ref, dwx_ref, dbx_ref, dwa_ref, dba_ref, dap_ref,
             ext, aext, abuf, gbuf, carry, acarry):
        i = pl.program_id(0)
        blk = nt - 1 - i

        @pl.when(i == 0)
        def _():
            carry[...] = jnp.zeros_like(carry)
            acarry[...] = jnp.zeros_like(acarry)
            for r in (dcw_ref, dcb_ref, dwx_ref, dbx_ref, dwa_ref, dba_ref, dap_ref):
                r[...] = jnp.zeros_like(r)

        taps, xc, xcb, gx, ga, sp, a, mult, is0 = _lru_gates(
            ext, x_ref, p_ref, cw_ref, cb_ref, wx_ref, bx_ref, wa_ref, ba_ref, ap_ref, blk == 0, blk * tm, tm)
        gate = g_ref[...]
        d_o = d_ref[...]
        hcur = h_ref[...]
        dg_ref[...] = d_o * hcur * _gelu_grad(gate)
        aext[0:tm, :] = a
        aext[tm:, :] = acarry[...]
        abuf[...] = aext[pl.ds(1, tm), :]
        gbuf[...] = d_o * _gelu(gate)
        _real_scan(abuf, gbuf, carry, tm, True)
        acarry[...] = jnp.broadcast_to(a[0:1], acarry.shape)
        g = gbuf[...]
        ext[0:HALO, :] = jnp.where(blk == 0, 0.0, hp_ref[...])
        ext[HALO:, :] = hcur
        hprev = ext[pl.ds(HALO - 1, tm), :]
        dmult = jnp.where(is0, 0.0, g * gx * xc)
        dgx = g * mult * xc
        dxc = g * mult * gx
        dlog_a = g * hprev * a - dmult * (a * a) / mult
        dga = dlog_a * (-LRU_C * sp)
        dsp = jnp.sum(dlog_a * (-LRU_C * ga), axis=0, keepdims=True)
        dap_ref[...] += dsp * (-_sigmoid(-ap_ref[...]))
        dpa = (dga * ga * (1.0 - ga))
        dpx = (dgx * gx * (1.0 - gx))
        dpab, dpxb = dpa.astype(BF16), dpx.astype(BF16)
        dwx_ref[...] += _dot(xcb, dpxb, TN)
        dwa_ref[...] += _dot(xcb, dpab, TN)
        dbx_ref[...] += jnp.sum(dpx, axis=0, keepdims=True)
        dba_ref[...] += jnp.sum(dpa, axis=0, keepdims=True)
        dxc = dxc + _dot(dpxb, wx_ref[...], NT) + _dot(dpab, wa_ref[...], NT)
        dxc_ref[...] = dxc
        dcb_ref[...] += jnp.sum(dxc, axis=0, keepdims=True)
        for k in range(LRU_CONV):
            dcw_ref[k:k + 1, :] += jnp.sum(dxc * taps[k], axis=0, keepdims=True)

    col, prev, full, wts = _lru_specs(tm, rb)
    row = pl.BlockSpec((tm, D_LRU), lambda i: (rb(i), 0))
    hprev_spec = pl.BlockSpec((HALO, D_LRU), lambda i: (jnp.maximum(rb(i) * (tm // HALO) - 1, 0), 0))
    shape = lambda shp: jax.ShapeDtypeStruct(shp, F32)
    vec = (1, D_LRU)
    sq = (D_LRU, D_LRU)
    return pl.pallas_call(
        kern, name="lru_bwd", grid=(nt,),
        in_specs=[col(1), prev(1), col(2), row, hprev_spec, row] + wts,
        out_specs=[row, row, full((LRU_CONV, D_LRU)), full(vec), full(sq), full(vec), full(sq), full(vec), full(vec)],
        out_shape=[shape((t, D_LRU)), shape((t, D_LRU)), shape((LRU_CONV, D_LRU)), shape(vec), shape(sq),
                   shape(vec), shape(sq), shape(vec), shape(vec)],
        scratch_shapes=[pltpu.VMEM((tm + HALO, D_LRU), F32), pltpu.VMEM((tm + HALO, D_LRU), F32),
                        pltpu.VMEM((tm, D_LRU), F32), pltpu.VMEM((tm, D_LRU), F32),
                        pltpu.VMEM((SUBLANES, D_LRU), F32), pltpu.VMEM((SUBLANES, D_LRU), F32)],
        compiler_params=_params(("arbitrary",)),
    )(proj, proj, proj, h, h, dout, cw, cb, wx, bx, wa, ba, ap)


def _assemble_dproj(dq, dk, dv, du, dxc, dgate, cos, sin_s, cw):
    t = dq.shape[0]
    tm = _row_tile(t, TM)
    nt = t // tm

    def kern(dq_ref, dk_ref, dv_ref, du_ref, dx_ref, dn_ref, dg_ref, c_ref, s_ref, cw_ref, o_ref, b_ref, ext):
        i = pl.program_id(0)

        @pl.when(i == 0)
        def _():
            b_ref[...] = jnp.zeros_like(b_ref)

        def put(lo, val):
            hi = lo + val.shape[1]
            o_ref[:, lo:hi] = val.astype(BF16)
            b_ref[:, lo:hi] += jnp.sum(val, axis=0, keepdims=True)

        c = c_ref[...]
        s = s_ref[...]
        for ch in range(4):
            x = dq_ref[:, ch * 128:(ch + 1) * 128] * (HEAD_DIM ** -0.5)
            put(ch * 128, x * c - _rope_swap(x) * s)
        x = dk_ref[...]
        put(512, x * c - _rope_swap(x) * s)
        put(640, dv_ref[...])
        put(768, du_ref[...])
        ext[0:tm, :] = dx_ref[...]
        ext[tm:, :] = jnp.where(i < nt - 1, dn_ref[...], 0.0)
        put(1024, sum(cw_ref[k:k + 1, :] * ext[pl.ds(LRU_CONV - 1 - k, tm), :] for k in range(LRU_CONV)))
        put(1280, dg_ref[...])

    row = lambda w: pl.BlockSpec((tm, w), lambda i: (i, 0))
    nxt = pl.BlockSpec((HALO, D_LRU), lambda i: (jnp.minimum((i + 1) * (tm // HALO), t // HALO - 1), 0))
    return pl.pallas_call(
        kern, name="assemble_dproj", grid=(nt,),
        in_specs=[row(512), row(128), row(128), row(256), row(256), nxt, row(256), row(128), row(128),
                  pl.BlockSpec((LRU_CONV, D_LRU), lambda i: (0, 0))],
        out_specs=[row(D_IN), pl.BlockSpec((1, D_IN), lambda i: (0, 0))],
        out_shape=[jax.ShapeDtypeStruct((t, D_IN), BF16), jax.ShapeDtypeStruct((1, D_IN), F32)],
        scratch_shapes=[pltpu.VMEM((tm + HALO, D_LRU), F32)],
        compiler_params=_params(("arbitrary",)),
    )(dq, dk, dv, du, dxc, dxc, dgate, cos, sin_s, cw)


def _blockdiag_s5(bbar_re, bbar_im, c_re, c_im):
    eye = jnp.eye(S5_GROUPS, dtype=F32)
    b_of = lambda m: jnp.einsum('gpc,gh->gchp', m, eye).reshape(D_S5, N_STATE)
    c_of = lambda m: jnp.einsum('gcp,gh->gphc', m, eye).reshape(N_STATE, D_S5)
    bmat = jnp.concatenate([b_of(bbar_re), b_of(bbar_im)], axis=1)
    cmat = jnp.concatenate([c_of(c_re), -c_of(c_im)], axis=0)
    return bmat, cmat


def _s5_prepare(a_re, a_im, b_re, b_im, c_re, c_im, log_dt):
    lam_re = jnp.minimum(a_re, -1e-4)
    lam_im = a_im
    dt = jnp.exp(log_dt)[:, None]
    decay = jnp.exp(dt * lam_re)
    ang = dt * lam_im
    abar_re = decay * jnp.cos(ang)
    abar_im = decay * jnp.sin(ang)
    den = jnp.square(lam_re) + jnp.square(lam_im)
    nr = abar_re - 1.0
    ni = abar_im
    coef_re = (nr * lam_re + ni * lam_im) / den
    coef_im = (ni * lam_re - nr * lam_im) / den
    bbar_re = coef_re[..., None] * b_re - coef_im[..., None] * b_im
    bbar_im = coef_re[..., None] * b_im + coef_im[..., None] * b_re
    bmat, cmat = _blockdiag_s5(bbar_re, bbar_im, c_re, c_im)
    return abar_re.reshape(N_STATE), abar_im.reshape(N_STATE), bmat, cmat


def _blockdiag_lru(w):
    eye = jnp.eye(LRU_HEADS, dtype=F32)
    return jnp.einsum('hij,hk->hikj', w, eye).reshape(D_LRU, D_LRU)


def _rope_tables(t):
    inv_freq = ROPE_THETA ** (-jnp.arange(0, HEAD_DIM, 2, dtype=F32) / HEAD_DIM)
    ang = jnp.arange(t, dtype=F32)[:, None] * inv_freq[None, :]
    cos, sin = jnp.cos(ang), jnp.sin(ang)
    return jnp.tile(jnp.concatenate([cos, cos], axis=1), (1, 2)), jnp.tile(jnp.concatenate([-sin, sin], axis=1), (1, 2))


def _vec(v):
    return v.reshape(1, -1)


def _layer_weights(p):
    abar_re, abar_im, bmat, cmat = _s5_prepare(p['s5_a_re'], p['s5_a_im'], p['s5_b_re'], p['s5_b_im'],
                                               p['s5_c_re'], p['s5_c_im'], p['s5_log_dt'])
    return dict(
        coef_f=_s5_coefs(abar_re, abar_im, False), coef_b=_s5_coefs(abar_re, abar_im, True),
        bmat=bmat.astype(BF16), cmat=cmat.astype(BF16),
        wx=_blockdiag_lru(p['lru_wx']).astype(BF16), wa=_blockdiag_lru(p['lru_wa']).astype(BF16),
        gw=p['s5_glu_w'].astype(BF16))


def _layer_fwd(x, xb, p, w, cos, sin_s):
    t = x.shape[0]
    tm = _row_tile(t, TM)
    layer = p['layer']
    qkv, uxg = _in_proj(xb, p['w_in'], _vec(p['b_in']), cos, sin_s, layer)
    ya, lse = _attn_fwd(qkv, _vec(p['attn_sinks']))
    h5, ys = _s5_fwd(uxg, w['bmat'], w['coef_f'], w['cmat'], _vec(p['s5_d']), w['gw'], _vec(p['s5_glu_b']))
    lru_w = (p['lru_conv_w'], _vec(p['lru_conv_b']), w['wx'], _vec(p['lru_bx']), w['wa'], _vec(p['lru_ba']),
             _vec(p['lru_a_param']))
    yl, hl = _lru_fwd(uxg, *lru_w)
    mix, x1, x1b, xhat1, rstd1 = _mix_out_ln(ya, ys, yl, _vec(p['mix_norm_g']), p['w_out'], _vec(p['b_out']), x,
                                             _vec(p['ln1_g']), _vec(p['ln1_b']), layer)
    gpre, gconv, up, hmid = _ffn_hidden_fwd(x1b, p['ffn_w_gate'], p['ffn_w_up'], p['ffn_conv_w'], p['ffn_conv_b'],
                                            layer)
    x2, x2b, xhat2, rstd2 = _matmul_ln(
        "ffn_down_ln", hmid, p['ffn_w_down'], jnp.zeros((1, D_MODEL), F32), x1, _vec(p['ln2_g']), _vec(p['ln2_b']),
        a_blk=(N_CHIPS, tm, FF_SH), a_map=lambda i: (0, i, 0), w_blk=(N_CHIPS, FF_SH, D_MODEL), parts=N_CHIPS,
        layer=layer)
    saved = dict(xb=xb, uxg=uxg, qkv=qkv, ya=ya, lse=lse, h5=h5, ys=ys, yl=yl, hl=hl, mix=mix, x1b=x1b, xhat1=xhat1,
                 rstd1=rstd1, gpre=gpre, gconv=gconv, up=up, hmid=hmid, xhat2=xhat2, rstd2=rstd2, lru_w=lru_w)
    return x2, x2b, saved


def _layer_bwd(dr2, dr2b, s, p, w, cos, sin_s, big, below):
    t = dr2.shape[0]
    tk = _row_tile(t, TMM)
    nk = t // tk
    tw = _row_tile(t, 2 * TMM)
    tm = _row_tile(t, TM)
    layer = p['layer']
    big = dict(big)
    g = {}
    dup, dgpre, g['ffn_conv_w'], g['ffn_conv_b'] = _ffn_hidden_bwd(
        dr2b, s['gpre'], s['gconv'], s['up'], p['ffn_w_down'], p['ffn_conv_w'], layer)
    big['ffn_w_down'] = _matmul(
        "d_w_down", s['hmid'], dr2b, a_blk=(None, tw, FF_SH), a_map=lambda i, j, k: (i, k, 0), b_blk=(tw, D_MODEL),
        b_map=lambda i, j, k: (k, 0), out_shape=(DEPTH, N_CHIPS, FF_SH, D_MODEL), o_blk=(None, None, FF_SH, D_MODEL),
        o_map=lambda i, j: (layer, i, 0, 0), grid=(N_CHIPS, 1, t // tw), dims=TN, into=big['ffn_w_down'])
    d_ffn_w = lambda name, dact, buf: _matmul(
        name, s['x1b'], dact, a_blk=(tw, D_MODEL), a_map=lambda i, j, k: (k, 0), b_blk=(None, tw, FF_SH),
        b_map=lambda i, j, k: (j, k, 0), out_shape=(DEPTH, N_CHIPS, D_MODEL, FF_SH),
        o_blk=(None, None, D_MODEL, FF_SH), o_map=lambda i, j: (layer, j, 0, 0), grid=(1, N_CHIPS, t // tw), dims=TN,
        into=buf)
    big['ffn_w_gate'] = d_ffn_w("d_w_gate", dgpre, big['ffn_w_gate'])
    big['ffn_w_up'] = d_ffn_w("d_w_up", dup, big['ffn_w_up'])
    wspec = dict(b_blk=(None, None, D_MODEL, FF_SH), b_map=lambda i, j, k: (layer, k, 0, 0))
    dx1 = _matmul(
        "d_x1", dgpre, p['ffn_w_gate'], pair2=(dup, p['ffn_w_up']), a_blk=(None, tk, FF_SH),
        a_map=lambda i, j, k: (k, i, 0), out_shape=(t, D_MODEL), o_blk=(tk, D_MODEL), o_map=lambda i, j: (i, 0),
        grid=(nk, 1, N_CHIPS), dims=NT, add=dr2, add_scale=ALPHA, **wspec)
    dr1, dr1b, g['ln1_g'], g['ln1_b'], g['b_out'], dya, dys, dyl, g['mix_norm_g'] = _d_mix_rms(
        dx1, s['xhat1'], s['rstd1'], _vec(p['ln1_g']), p['w_out'], s['ya'], s['ys'], s['yl'],
        _vec(p['mix_norm_g']), layer)
    big['w_out'] = _matmul(
        "d_w_out", s['mix'], dr1b, a_blk=(tk, D_MODEL), a_map=lambda i, j, k: (k, 0), b_blk=(tk, D_MODEL),
        b_map=lambda i, j, k: (k, 0), out_shape=(DEPTH, D_MODEL, D_MODEL), o_blk=(None, D_MODEL, D_MODEL),
        o_map=lambda i, j: (layer, 0, 0), grid=(1, 1, nk), dims=TN, into=big['w_out'])
    dq, dk, dv, g['attn_sinks'] = _attn_bwd(s['qkv'], s['ya'], dya, s['lse'], _vec(p['attn_sinks']))
    du, dcmat, dbmat, dabar, g['s5_d'], g['s5_glu_w'], g['s5_glu_b'] = _s5_bwd(
        s['uxg'], s['h5'], dys, w['bmat'], w['coef_b'], w['cmat'], _vec(p['s5_d']), w['gw'], _vec(p['s5_glu_b']))
    (dxc, dgate, g['lru_conv_w'], g['lru_conv_b'], dwx, g['lru_bx'], dwa, g['lru_ba'],
     g['lru_a_param']) = _lru_bwd(s['uxg'], s['hl'], dyl, *s['lru_w'])
    dproj, g['b_in'] = _assemble_dproj(dq, dk, dv, du, dxc, dgate, cos, sin_s, p['lru_conv_w'])
    big['w_in'] = _matmul(
        "d_w_in", s['xb'], dproj, a_blk=(tk, D_MODEL), a_map=lambda i, j, k: (k, 0), b_blk=(tk, IN_SH),
        b_map=lambda i, j, k: (k, j), out_shape=(DEPTH, N_CHIPS, D_MODEL, IN_SH), o_blk=(None, None, D_MODEL, IN_SH),
        o_map=lambda i, j: (layer, j, 0, 0), grid=(1, N_CHIPS, nk), dims=TN, into=big['w_in'])
    dx = _matmul("d_x", dproj, p['w_in'], a_blk=(tk, IN_SH), a_map=lambda i, j, k: (i, k),
                 b_blk=(None, None, D_MODEL, IN_SH), b_map=lambda i, j, k: (layer, k, 0, 0), out_shape=(t, D_MODEL),
                 o_blk=(tk, D_MODEL), o_map=lambda i, j: (i, 0), grid=(nk, 1, N_CHIPS), dims=NT,
                 add=dr1, add_scale=ALPHA)
    if below is not None:
        dx = _ln_bwd(dx, below[0]['xhat2'], below[0]['rstd2'], _vec(below[1]['ln2_g']))
    return dx, _param_chain(g, p, dabar, dbmat, dcmat, dwx, dwa), big


def _param_chain(g, p, dabar, dbmat, dcmat, dwx, dwa):
    s5_names = ('s5_a_re', 's5_a_im', 's5_b_re', 's5_b_im', 's5_c_re', 's5_c_im', 's5_log_dt')
    _, s5_vjp = jax.vjp(_s5_prepare, *[p[n] for n in s5_names])
    for n, val in zip(s5_names, s5_vjp((dabar[0], dabar[1], dbmat, dcmat))):
        g[n] = val
    g['lru_wx'] = jax.vjp(_blockdiag_lru, p['lru_wx'])[1](dwx)[0]
    g['lru_wa'] = jax.vjp(_blockdiag_lru, p['lru_wa'])[1](dwa)[0]
    return g


ROW_TILE = 512


def _pick_rows(rows):
    for rt in range(min(rows, ROW_TILE), 0, -1):
        if rows % rt == 0 and (rt % 16 == 0 or rt == rows):
            return rt
    return rows


def _cast_bf16(a):
    a2 = a.reshape(-1, a.shape[-1])
    rows, c = a2.shape
    rt = _pick_rows(rows)

    def kern(a_ref, o_ref):
        o_ref[...] = a_ref[...].astype(BF16)

    spec = pl.BlockSpec((rt, c), lambda i: (i, 0))
    out = pl.pallas_call(kern, name="cast_bf16", grid=(rows // rt,), in_specs=[spec], out_specs=spec,
                         out_shape=jax.ShapeDtypeStruct((rows, c), BF16), compiler_params=_params(("parallel",)))(a2)
    return out.reshape(a.shape)


def _sum_parts(name, parts, shape):
    c = shape[-1]
    rows = math.prod(shape[:-1])
    rt = _pick_rows(rows)
    n = len(parts)

    def kern(*refs):
        acc = refs[0][...].astype(F32)
        for r in refs[1:n]:
            acc = acc + r[...].astype(F32)
        refs[n][...] = acc

    specs, args = [], []
    for arr, j in parts:
        if j is None:
            specs.append(pl.BlockSpec((rt, c), lambda i: (i, 0)))
            args.append(arr.reshape(rows, c))
        else:
            specs.append(pl.BlockSpec((None, rt, c), functools.partial(lambda i, jj: (jj, i, 0), jj=j)))
            args.append(arr.reshape(arr.shape[0], rows, c))
    out = pl.pallas_call(kern, name=name, grid=(rows // rt,), in_specs=specs,
                         out_specs=pl.BlockSpec((rt, c), lambda i: (i, 0)),
                         out_shape=jax.ShapeDtypeStruct((rows, c), F32), compiler_params=_params(("parallel",)))(*args)
    return out.reshape(shape)


def _adamw(name, w, g, m, v):
    shape = w.shape
    c = shape[-1]
    rows = math.prod(shape[:-1])
    rt = _pick_rows(rows)

    def kern(w_ref, g_ref, m_ref, v_ref, d_ref, nm_ref, nv_ref):
        g_ = g_ref[...]
        m_ = ADAM_B1 * m_ref[...] + (1.0 - ADAM_B1) * g_
        v_ = ADAM_B2 * v_ref[...] + (1.0 - ADAM_B2) * jnp.square(g_)
        m_hat = m_ / (1.0 - ADAM_B1 ** ADAM_STEP)
        v_hat = v_ / (1.0 - ADAM_B2 ** ADAM_STEP)
        d_ref[...] = -ADAM_LR * (m_hat / (jnp.sqrt(v_hat) + ADAM_EPS) + ADAM_WD * w_ref[...])
        nm_ref[...] = m_
        nv_ref[...] = v_

    spec = pl.BlockSpec((rt, c), lambda i: (i, 0))
    outs = pl.pallas_call(kern, name=name, grid=(rows // rt,), in_specs=[spec] * 4, out_specs=[spec] * 3,
                          out_shape=[jax.ShapeDtypeStruct((rows, c), F32)] * 3,
                          compiler_params=_params(("parallel",)))(*[a.reshape(rows, c) for a in (w, g, m, v)])
    return tuple(o.reshape(shape) for o in outs)


def _position():
    return lax.axis_index("x"), lax.axis_index("y"), lax.axis_index("c")


def _other_chips(x, y):
    return [(1 - x, y), (x, 1 - y), (1 - x, 1 - y)]


def _exchange(name, arrs, out_shapes, n_local, n_remote, plan):
    n_in, n_out = len(arrs), len(out_shapes)

    def kern(*refs):
        ins, outs = refs[:n_in], refs[n_in:n_in + n_out]
        send, recv, loc = refs[n_in + n_out:]
        local, remote = plan(ins, outs, *_position())
        assert len(local) == n_local and len(remote) == n_remote
        own = [pltpu.make_async_copy(s, d, loc.at[k]) for k, (s, d) in enumerate(local)]
        for cp in own:
            cp.start()
        sent = [pltpu.make_async_remote_copy(src_ref=s, dst_ref=d, send_sem=send.at[k], recv_sem=recv.at[k],
                                             device_id=peer, device_id_type=MESH)
                for k, (s, d, peer, _) in enumerate(remote)]
        for cp in sent:
            cp.start()
        for k, (s, _, peer, landing) in enumerate(remote):
            pltpu.make_async_remote_copy(src_ref=s, dst_ref=landing, send_sem=send.at[k], recv_sem=recv.at[k],
                                         device_id=peer, device_id_type=MESH).wait_recv()
        for cp in sent:
            cp.wait_send()
        for cp in own:
            cp.wait()

    return pl.pallas_call(
        kern, name=name, in_specs=[ANY] * n_in, out_specs=[ANY] * n_out, out_shape=out_shapes,
        scratch_shapes=[pltpu.SemaphoreType.DMA((n_remote,)), pltpu.SemaphoreType.DMA((n_remote,)),
                        pltpu.SemaphoreType.DMA((max(n_local, 1),))],
    )(*arrs)


def _allgather_chips(arrs, halved=()):
    n = len(arrs)
    layers = arrs[0].shape[0]

    def plan(ins, outs, x, y, c):
        me = 2 * x + y
        local, remote = [], []
        for t in range(n):
            for l in range(layers):
                src = ins[t].at[l]
                if t in halved:
                    r2 = ins[t].shape[2] // 2
                    src = ins[t].at[l, :, pl.ds(c * r2, r2)]
                local.append((src, outs[t].at[l, pl.ds(me, 1)]))
                for px, py in _other_chips(x, y):
                    remote.append((src, outs[t].at[l, pl.ds(me, 1)], (px, py, c),
                                   outs[t].at[l, pl.ds(2 * px + py, 1)]))
        return local, remote

    outs = []
    for t, a in enumerate(arrs):
        tail = (a.shape[2] // 2,) + a.shape[3:] if t in halved else a.shape[2:]
        outs.append(jax.ShapeDtypeStruct((a.shape[0], N_CHIPS) + tail, a.dtype))
    return _exchange("allgather_chips", arrs, outs, n * layers, 3 * n * layers, plan)


def _chip_scatter(arrs):
    n = len(arrs)
    layers = arrs[0].shape[0]

    def plan(ins, outs, x, y, c):
        me = 2 * x + y
        local, remote = [], []
        for t in range(n):
            for l in range(layers):
                local.append((ins[t].at[l, pl.ds(me, 1)], outs[2 * t].at[l]))
                for j, (px, py) in enumerate(_other_chips(x, y)):
                    remote.append((ins[t].at[l, pl.ds(2 * px + py, 1)], outs[2 * t + 1].at[j, l], (px, py, c),
                                   outs[2 * t + 1].at[j, l]))
        return local, remote

    outs = []
    for a in arrs:
        one = (a.shape[0], 1) + a.shape[2:]
        outs += [jax.ShapeDtypeStruct(one, a.dtype), jax.ShapeDtypeStruct((3,) + one, a.dtype)]
    return _exchange("chip_scatter", arrs, outs, n * layers, 3 * n * layers, plan)


def _allgather_devices(v):
    def kern(v_ref, o_ref, send, recv, loc):
        x, y, c = _position()
        me, sibling = (x, y, c), (x, y, 1 - c)
        chips = _other_chips(x, y)

        def rows(px, py, pc):
            return o_ref.at[pl.ds(4 * px + 2 * py + pc, 1)]

        def copy(k, block, to, src=None):
            return pltpu.make_async_remote_copy(
                src_ref=rows(*block) if src is None else src, dst_ref=rows(*block), send_sem=send.at[k],
                recv_sem=recv.at[k], device_id=to, device_id_type=MESH)

        mine = pltpu.make_async_copy(v_ref, rows(*me), loc.at[0])
        mine.start()
        first = [copy(0, me, sibling, src=v_ref)]
        first += [copy(1 + j, me, (*chip, c), src=v_ref) for j, chip in enumerate(chips)]
        for cp in first:
            cp.start()
        passed = [copy(4 + j, (*chip, c), sibling) for j, chip in enumerate(chips)]
        for j, chip in enumerate(chips):
            copy(1 + j, (*chip, c), me).wait_recv()
            passed[j].start()
        copy(0, sibling, me).wait_recv()
        for j, chip in enumerate(chips):
            copy(4 + j, (*chip, 1 - c), me).wait_recv()
        for cp in first + passed:
            cp.wait_send()
        mine.wait()

    vmem = pl.BlockSpec(memory_space=pltpu.VMEM)
    return pl.pallas_call(
        kern, name="allgather_devices", in_specs=[vmem], out_specs=vmem,
        out_shape=jax.ShapeDtypeStruct((N_DEV,) + v.shape[1:], v.dtype),
        scratch_shapes=[pltpu.SemaphoreType.DMA((7,)), pltpu.SemaphoreType.DMA((7,)), pltpu.SemaphoreType.DMA((1,))],
        compiler_params=pltpu.CompilerParams(vmem_limit_bytes=VMEM_MB << 20),
    )(v)


WEIGHTS = ['w_in', 'b_in', 'attn_sinks', 's5_a_re', 's5_a_im', 's5_b_re', 's5_b_im', 's5_c_re', 's5_c_im', 's5_d',
           's5_log_dt', 's5_glu_w', 's5_glu_b', 'lru_conv_w', 'lru_conv_b', 'lru_wx', 'lru_bx', 'lru_wa', 'lru_ba',
           'lru_a_param', 'mix_norm_g', 'w_out', 'b_out', 'ln1_g', 'ln1_b', 'ffn_w_gate', 'ffn_w_up', 'ffn_conv_w',
           'ffn_conv_b', 'ffn_w_down', 'ln2_g', 'ln2_b']
BIG = ('w_in', 'w_out', 'ffn_w_gate', 'ffn_w_up', 'ffn_w_down')
SMALL = tuple(n for n in WEIGHTS if n not in BIG)
PACK_ROWS = ROW_TILE


def _pack(arrs):
    flat = jnp.concatenate([a.reshape(-1) for a in arrs])
    unit = 128 * PACK_ROWS
    size = -(-flat.shape[0] // unit) * unit
    return jnp.pad(flat, (0, size - flat.shape[0])).reshape(-1, 128)


def _unpack(packed, shapes):
    flat = packed.reshape(-1)
    out, pos = [], 0
    for shp in shapes:
        n = math.prod(shp)
        out.append(flat[pos:pos + n].reshape(shp))
        pos += n
    return out


def _pair_reduce(name, g):
    layers, shards, rows, cols = g.shape
    r2 = rows // 2
    rt = _pick_rows(r2)
    nr = r2 // rt
    nsteps = layers * shards * nr

    def kern(c_ref, mine_ref, other_ref, o_ref, buf, send, recv, credit):
        x, y, c = _position()
        sibling = (x, y, 1 - c)
        k = pl.program_id(0) * nr + pl.program_id(1)
        slot = k % 2

        @pl.when(k >= 2)
        def _():
            pl.semaphore_wait(credit, 1)

        cp = pltpu.make_async_remote_copy(src_ref=other_ref, dst_ref=buf.at[slot], send_sem=send.at[slot],
                                          recv_sem=recv.at[slot], device_id=sibling, device_id_type=MESH)
        cp.start()
        cp.wait_recv()
        o_ref[...] = (mine_ref[...] + buf[slot]).astype(BF16)
        cp.wait_send()

        @pl.when(k + 2 < nsteps)
        def _():
            pl.semaphore_signal(credit, 1, device_id=sibling, device_id_type=MESH)

    blk = (1, rt, cols)
    grid_spec = pltpu.PrefetchScalarGridSpec(
        num_scalar_prefetch=1, grid=(layers * shards, nr),
        in_specs=[pl.BlockSpec(blk, lambda m, r, c_ref: (m, c_ref[0] * nr + r, 0)),
                  pl.BlockSpec(blk, lambda m, r, c_ref: (m, (1 - c_ref[0]) * nr + r, 0))],
        out_specs=pl.BlockSpec(blk, lambda m, r, c_ref: (m, r, 0)),
        scratch_shapes=[pltpu.VMEM((2,) + blk, F32), pltpu.SemaphoreType.DMA((2,)),
                        pltpu.SemaphoreType.DMA((2,)), pltpu.SemaphoreType.REGULAR])
    core = lax.axis_index("c").astype(jnp.int32).reshape(1)
    g3 = g.reshape(layers * shards, rows, cols)
    out = pl.pallas_call(
        kern, name=name, grid_spec=grid_spec,
        out_shape=jax.ShapeDtypeStruct((layers * shards, r2, cols), BF16),
        compiler_params=_params(("arbitrary", "arbitrary")),
    )(core, g3, g3)
    return out.reshape(layers, shards, r2, cols)


def _pair_merge(name, h):
    m, r2, cols = h.shape
    rt = _pick_rows(r2)
    nr = r2 // rt
    nsteps = m * nr

    def kern(h_ref, o_ref, buf, send, recv, credit):
        x, y, c = _position()
        sibling = (x, y, 1 - c)
        k = pl.program_id(0) * nr + pl.program_id(1)
        slot = k % 2

        @pl.when(k >= 2)
        def _():
            pl.semaphore_wait(credit, 1)

        cp = pltpu.make_async_remote_copy(src_ref=h_ref, dst_ref=buf.at[slot], send_sem=send.at[slot],
                                          recv_sem=recv.at[slot], device_id=sibling, device_id_type=MESH)
        cp.start()
        cp.wait_recv()
        o_ref[0, pl.ds(c, 1)] = h_ref[...]
        o_ref[0, pl.ds(1 - c, 1)] = buf[slot]
        cp.wait_send()

        @pl.when(k + 2 < nsteps)
        def _():
            pl.semaphore_signal(credit, 1, device_id=sibling, device_id_type=MESH)

    blk = (1, rt, cols)
    out = pl.pallas_call(
        kern, name=name, grid=(m, nr),
        in_specs=[pl.BlockSpec(blk, lambda i, r: (i, r, 0))],
        out_specs=pl.BlockSpec((1, 2, rt, cols), lambda i, r: (i, 0, r, 0)),
        out_shape=jax.ShapeDtypeStruct((m, 2, r2, cols), h.dtype),
        scratch_shapes=[pltpu.VMEM((2,) + blk, h.dtype), pltpu.SemaphoreType.DMA((2,)),
                        pltpu.SemaphoreType.DMA((2,)), pltpu.SemaphoreType.REGULAR],
        compiler_params=_params(("arbitrary", "arbitrary")),
    )(h)
    return out.reshape(m, 2 * r2, cols)


def _reduce_big(grads):
    pair = [_pair_reduce("pair_reduce_" + n, g) for n, g in zip(BIG, grads)]
    scat = _chip_scatter(pair)
    out = []
    for t, n in enumerate(BIG):
        own, got = scat[2 * t], scat[2 * t + 1]
        half = _sum_parts("chip_sum", [(own, None)] + [(got, j) for j in range(3)], own.shape)
        out.append(_pair_merge("grad_merge_" + n, half.reshape(half.shape[0], half.shape[2], half.shape[3])))
    return out


def _step(a):
    x = a['x'][0]
    target = a['loss_target'][0]
    t = x.shape[0]
    xi, yi, _ = _position()
    chip = 2 * xi + yi
    cos, sin_s = _rope_tables(t)

    gathered = _allgather_chips([_cast_bf16(a[n])[:, None] for n in BIG]
                                + [a[n][:, None] for n in ('s5_glu_w', 'lru_conv_w', 'ffn_conv_w')],
                                halved=range(len(BIG)))
    full = dict(zip(BIG + ('s5_glu_w', 'lru_conv_w', 'ffn_conv_w'), gathered))
    for n in BIG:
        layers, chips, r2, cols = full[n].shape
        full[n] = _pair_merge("weight_merge_" + n, full[n].reshape(layers * chips, r2, cols)).reshape(
            layers, chips, 2 * r2, cols)

    def layer_params(l):
        p = {n: a[n][l] for n in SMALL}
        p['layer'] = l
        p['w_in'] = full['w_in']
        p['w_out'] = full['w_out'].reshape(DEPTH, D_MODEL, D_MODEL)
        p['ffn_w_gate'] = full['ffn_w_gate']
        p['ffn_w_up'] = full['ffn_w_up']
        p['ffn_w_down'] = full['ffn_w_down']
        p['s5_glu_w'] = full['s5_glu_w'][l].reshape(D_S5, D_S5)
        p['lru_conv_w'] = full['lru_conv_w'][l].transpose(1, 0, 2).reshape(LRU_CONV, D_LRU)
        p['ffn_conv_w'] = full['ffn_conv_w'][l]
        p['ffn_conv_b'] = a['ffn_conv_b'][l].reshape(N_CHIPS, 1, FF_SH)
        return p

    params = [layer_params(l) for l in range(DEPTH)]
    derived = [_layer_weights(p) for p in params]
    saved = []
    h, hb = x, _cast_bf16(x)
    for l in range(DEPTH):
        h, hb, s = _layer_fwd(h, hb, params[l], derived[l], cos, sin_s)
        saved.append(s)
    loss_part, dr, drb, ln2_g, ln2_b, _ = _loss_head(h, target, saved[-1]['xhat2'], saved[-1]['rstd2'],
                                                     _vec(params[-1]['ln2_g']))
    loss = lax.psum(loss_part[0, 0], ("x", "y", "c"))
    grads = [None] * DEPTH
    big = {n: lax.empty((DEPTH, N_CHIPS) + a[n].shape[1:], F32) for n in BIG}
    big['w_out'] = big['w_out'].reshape(DEPTH, D_MODEL, D_MODEL)
    for l in reversed(range(DEPTH)):
        below = (saved[l - 1], params[l - 1]) if l > 0 else None
        out, grads[l], big = _layer_bwd(dr, drb, saved[l], params[l], derived[l], cos, sin_s, big, below)
        grads[l]['ln2_g'], grads[l]['ln2_b'] = ln2_g, ln2_b
        if l > 0:
            dr, drb, ln2_g, ln2_b, _ = out
        else:
            grad_x = out[None]

    def stacked(n):
        return jnp.stack([grads[l][n] for l in range(DEPTH)])

    big['w_out'] = big['w_out'].reshape(DEPTH, N_CHIPS, OUT_SH, D_MODEL)
    grad = dict(zip(BIG, _reduce_big([big[n] for n in BIG])))
    small_local = [stacked(n) for n in SMALL]
    packed = _allgather_devices(_pack(small_local)[None])
    total = _sum_parts("device_sum", [(packed, j) for j in range(N_DEV)], packed.shape[1:])
    small_sum = dict(zip(SMALL, _unpack(total, [g.shape for g in small_local])))
    for n in SMALL:
        g = small_sum[n]
        if n == 's5_glu_w':
            g = lax.dynamic_slice_in_dim(g, chip * (D_S5 // N_CHIPS), D_S5 // N_CHIPS, axis=1)
        elif n == 'lru_conv_w':
            g = lax.dynamic_slice_in_dim(g, chip * (D_LRU // N_CHIPS), D_LRU // N_CHIPS, axis=2)
        elif n == 'ffn_conv_w':
            g = lax.dynamic_index_in_dim(g, chip, axis=1, keepdims=False)
        grad[n] = g.reshape(a[n].shape)

    delta, new_m, new_v = {}, {}, {}
    for n in WEIGHTS:
        delta[n], new_m[n], new_v[n] = _adamw("adamw_" + n, a[n], grad[n], a['m_' + n], a['v_' + n])
    return (loss, grad_x, *[grad[n] for n in WEIGHTS], *[delta[n] for n in WEIGHTS],
            *[new_m[n] for n in WEIGHTS], *[new_v[n] for n in WEIGHTS])


def kernel(x, w_in, b_in, attn_sinks, s5_a_re, s5_a_im, s5_b_re, s5_b_im, s5_c_re, s5_c_im, s5_d, s5_log_dt, s5_glu_w, s5_glu_b, lru_conv_w, lru_conv_b, lru_wx, lru_bx, lru_wa, lru_ba, lru_a_param, mix_norm_g, w_out, b_out, ln1_g, ln1_b, ffn_w_gate, ffn_w_up, ffn_conv_w, ffn_conv_b, ffn_w_down, ln2_g, ln2_b, loss_target, m_w_in, m_b_in, m_attn_sinks, m_s5_a_re, m_s5_a_im, m_s5_b_re, m_s5_b_im, m_s5_c_re, m_s5_c_im, m_s5_d, m_s5_log_dt, m_s5_glu_w, m_s5_glu_b, m_lru_conv_w, m_lru_conv_b, m_lru_wx, m_lru_bx, m_lru_wa, m_lru_ba, m_lru_a_param, m_mix_norm_g, m_w_out, m_b_out, m_ln1_g, m_ln1_b, m_ffn_w_gate, m_ffn_w_up, m_ffn_conv_w, m_ffn_conv_b, m_ffn_w_down, m_ln2_g, m_ln2_b, v_w_in, v_b_in, v_attn_sinks, v_s5_a_re, v_s5_a_im, v_s5_b_re, v_s5_b_im, v_s5_c_re, v_s5_c_im, v_s5_d, v_s5_log_dt, v_s5_glu_w, v_s5_glu_b, v_lru_conv_w, v_lru_conv_b, v_lru_wx, v_lru_bx, v_lru_wa, v_lru_ba, v_lru_a_param, v_mix_norm_g, v_w_out, v_b_out, v_ln1_g, v_ln1_b, v_ffn_w_gate, v_ffn_w_up, v_ffn_conv_w, v_ffn_conv_b, v_ffn_w_down, v_ln2_g, v_ln2_b):
    return _step(dict(locals()))
```

```python
import functools
import math

import jax
import jax.numpy as jnp
from jax import lax
from jax.experimental import pallas as pl
from jax.experimental.pallas import tpu as pltpu

F32 = jnp.float32
BF16 = jnp.bfloat16
MESH = pl.DeviceIdType.MESH
ANY = pl.BlockSpec(memory_space=pl.ANY)

D_MODEL = 1024
DEPTH = 4
HEAD_DIM = 64
N_Q_HEADS = 8
N_KV_HEADS = 2
Q_PER_KV = 4
D_ATTN = 512
D_KV = 128
ATTN_BLOCK = 128
ROPE_THETA = 10000.0
D_S5 = 256
S5_GROUP = 16
S5_GROUPS = 16
S5_STATE = 64
N_STATE = S5_GROUPS * S5_STATE
D_LRU = 256
LRU_HEADS = 4
LRU_HEAD_DIM = 64
LRU_CONV = 4
LRU_C = 8.0
D_IN = 1536
D_FF = 2816
FFN_CONV = 3
N_CHIPS = 4
N_DEV = 8
IN_SH = D_IN // N_CHIPS
FF_SH = D_FF // N_CHIPS
OUT_SH = D_MODEL // N_CHIPS
ALPHA = (2 * DEPTH) ** 0.25
LN_EPS = 1e-5
RMS_EPS = 1e-6
ADAM_LR = 0.001
ADAM_B1 = 0.9
ADAM_B2 = 0.999
ADAM_EPS = 1e-08
ADAM_WD = 0.01
ADAM_STEP = 10

SUBLANES = 8
VMEM_MB = 56


def _params(sem):
    return pltpu.CompilerParams(dimension_semantics=sem, vmem_limit_bytes=VMEM_MB << 20)


def _row_tile(t, pref):
    return min(t, pref)


def _matmul(name, a, b, *, a_blk, a_map, b_blk, b_map, out_shape, o_blk, o_map, grid, dims,
            out_dtype=F32, bias=None, bias_blk=None, bias_map=None, add=None, add_scale=1.0, pair2=None,
            into=None, ln_bwd=None):
    nk = grid[2]
    acc_shape = tuple(d for d in o_blk if d is not None)
    n_in = 2 if pair2 is None else 4

    def kern(*refs):
        p = n_in
        bias_ref = add_ref = None
        if bias is not None:
            bias_ref = refs[p]
            p += 1
        if add is not None:
            add_ref = refs[p]
            p += 1
        if into is not None:
            p += 1
        if ln_bwd is not None:
            ln_in = refs[p:p + 3]
            ln_out = refs[p + 4:p + 8]
            o_ref, acc = refs[p + 3], refs[p + 8]
        else:
            o_ref, acc = refs[p], refs[p + 1]
        k = pl.program_id(2)
        first_tile = pl.program_id(0) == 0

        def product():
            r = _dot(refs[0][...].astype(BF16), refs[1][...].astype(BF16), dims)
            if pair2 is not None:
                r = r + _dot(refs[2][...].astype(BF16), refs[3][...].astype(BF16), dims)
            return r

        def finish(r):
            if bias_ref is not None:
                r = r + bias_ref[...]
            if add_ref is not None:
                r = r + add_scale * add_ref[...]
            if ln_bwd is None:
                o_ref[...] = r.astype(out_dtype)
            else:
                @pl.when(first_tile)
                def _():
                    for ref in ln_out[1:]:
                        ref[...] = jnp.zeros_like(ref)

                _ln_bwd_tile(r, ln_in[0][...], ln_in[1][...], ln_in[2][...], o_ref, *ln_out)

        if nk == 1:
            finish(product())
        else:
            @pl.when(k == 0)
            def _():
                acc[...] = jnp.zeros_like(acc)

            acc[...] += product()

            @pl.when(k == nk - 1)
            def _():
                finish(acc[...])

    in_specs = [pl.BlockSpec(a_blk, a_map), pl.BlockSpec(b_blk, b_map)]
    args = [a, b]
    if pair2 is not None:
        in_specs += [pl.BlockSpec(a_blk, a_map), pl.BlockSpec(b_blk, b_map)]
        args += list(pair2)
    if bias is not None:
        in_specs.append(pl.BlockSpec(bias_blk, bias_map))
        args.append(bias)
    if add is not None:
        in_specs.append(pl.BlockSpec(o_blk, lambda i, j, k: o_map(i, j)))
        args.append(add)
    aliases = {}
    if into is not None:
        aliases = {len(args): 0}
        in_specs.append(ANY)
        args.append(into)
    o_spec = pl.BlockSpec(o_blk, lambda i, j, k: o_map(i, j))
    out_specs, out_shapes = o_spec, jax.ShapeDtypeStruct(out_shape, out_dtype)
    semantics = ("parallel", "parallel", "arbitrary")
    if ln_bwd is not None:
        vec = pl.BlockSpec((1, o_blk[-1]), lambda i, j, k: (0, 0))
        in_specs += [o_spec, pl.BlockSpec((o_blk[0], 1), lambda i, j, k: (i, 0)), vec]
        args += list(ln_bwd)
        vshape = jax.ShapeDtypeStruct((1, o_blk[-1]), F32)
        out_specs = [o_spec, o_spec, vec, vec, vec]
        out_shapes = [out_shapes, jax.ShapeDtypeStruct(out_shape, BF16), vshape, vshape, vshape]
        semantics = ("arbitrary", "arbitrary", "arbitrary")
    return pl.pallas_call(
        kern, name=name, grid=grid, in_specs=in_specs, out_specs=out_specs, out_shape=out_shapes,
        scratch_shapes=[pltpu.VMEM(acc_shape if nk > 1 else (SUBLANES, 128), F32)],
        input_output_aliases=aliases,
        compiler_params=_params(semantics),
    )(*args)


NN = ((1,), (0,))
NT = ((1,), (1,))
TN = ((0,), (0,))
TM = 512


def _sigmoid(x):
    return 0.5 * jnp.tanh(0.5 * x) + 0.5


_GELU_C = math.sqrt(2.0 / math.pi)


def _gelu(x):
    return 0.5 * x * (1.0 + jnp.tanh(_GELU_C * (x + 0.044715 * x * x * x)))


def _gelu_grad(x):
    th = jnp.tanh(_GELU_C * (x + 0.044715 * x * x * x))
    return 0.5 * (1.0 + th) + 0.5 * x * (1.0 - th * th) * _GELU_C * (1.0 + 3 * 0.044715 * x * x)


def _rope_swap(t):
    lane = lax.broadcasted_iota(jnp.int32, t.shape, 1)
    lo = (lane % HEAD_DIM) < (HEAD_DIM // 2)
    return jnp.where(lo, pltpu.roll(t, 128 - HEAD_DIM // 2, 1), pltpu.roll(t, HEAD_DIM // 2, 1))


D_QKV = D_ATTN + 2 * D_KV
TMM = 1024


def _in_proj(xb, w_in, b_in, cos, sin_s, layer):
    t = xb.shape[0]
    tm = _row_tile(t, TMM)

    def kern(x_ref, w_ref, b_ref, c_ref, s_ref, q_ref, u_ref):
        x = x_ref[...]
        c = c_ref[...]
        s = s_ref[...]
        for j in range(N_CHIPS):
            pj = _dot(x, w_ref[j], NN) + b_ref[:, j * IN_SH:(j + 1) * IN_SH]
            for ch in range(IN_SH // 128):
                col = j * IN_SH + ch * 128
                v = pj[:, ch * 128:(ch + 1) * 128]
                if col < D_ATTN + D_KV:
                    v = v * c + _rope_swap(v) * s
                if col < D_ATTN:
                    v = v * (HEAD_DIM ** -0.5)
                if col < D_QKV:
                    q_ref[:, col:col + 128] = v.astype(BF16)
                else:
                    u_ref[:, col - D_QKV:col - D_QKV + 128] = v

    row = lambda w: pl.BlockSpec((tm, w), lambda i: (i, 0))
    return pl.pallas_call(
        kern, name="in_proj", grid=(t // tm,),
        in_specs=[row(D_MODEL), pl.BlockSpec((None, N_CHIPS, D_MODEL, IN_SH), lambda i: (layer, 0, 0, 0)),
                  pl.BlockSpec((1, D_IN), lambda i: (0, 0)), row(128), row(128)],
        out_specs=[row(D_QKV), row(D_IN - D_QKV)],
        out_shape=[jax.ShapeDtypeStruct((t, D_QKV), BF16), jax.ShapeDtypeStruct((t, D_IN - D_QKV), F32)],
        compiler_params=_params(("parallel",)),
    )(xb, w_in, b_in, cos, sin_s)


def _attn_mask(i):
    qi = lax.broadcasted_iota(jnp.int32, (ATTN_BLOCK, 2 * ATTN_BLOCK), 0)
    si = lax.broadcasted_iota(jnp.int32, (ATTN_BLOCK, 2 * ATTN_BLOCK), 1)
    diff = qi + ATTN_BLOCK - si
    return (diff >= 0) & (diff < ATTN_BLOCK) & ((si >= ATTN_BLOCK) | (i > 0))


def _row_sums(x, ones):
    hi = x.astype(BF16)
    lo = (x - hi.astype(F32)).astype(BF16)
    return _dot(hi, ones, NN) + _dot(lo, ones, NN)


def _attn_fwd(qkv, sinks):
    t = qkv.shape[0]
    nb = t // ATTN_BLOCK

    def kern(q_ref, kp_ref, kc_ref, vp_ref, vc_ref, s_ref, o_ref, l_ref):
        i = pl.program_id(0)
        si = lax.broadcasted_iota(jnp.int32, (2 * ATTN_BLOCK, ATTN_BLOCK), 0)
        qi = lax.broadcasted_iota(jnp.int32, (2 * ATTN_BLOCK, ATTN_BLOCK), 1)
        diff = qi + ATTN_BLOCK - si
        valid = (diff >= 0) & (diff < ATTN_BLOCK) & ((si >= ATTN_BLOCK) | (i > 0))
        kband = jnp.concatenate([kp_ref[...], kc_ref[...]], axis=0)
        vband = jnp.concatenate([vp_ref[...], vc_ref[...]], axis=0)
        ks = [kband[:, kh * HEAD_DIM:(kh + 1) * HEAD_DIM] for kh in range(N_KV_HEADS)]
        vs = [vband[:, kh * HEAD_DIM:(kh + 1) * HEAD_DIM] for kh in range(N_KV_HEADS)]
        scores = [_dot(ks[h // Q_PER_KV], q_ref[:, h * HEAD_DIM:(h + 1) * HEAD_DIM], NT) for h in range(N_Q_HEADS)]
        probs, lses = [], []
        for h in range(N_Q_HEADS):
            s = jnp.where(valid, scores[h], -jnp.inf)
            sink = s_ref[0:1, h:h + 1]
            m = jnp.maximum(jnp.max(s, axis=0, keepdims=True), sink)
            e = jnp.exp(s - m)
            denom = jnp.sum(e, axis=0, keepdims=True) + jnp.exp(sink - m)
            probs.append((e * (1.0 / denom)).astype(BF16))
            lses.append(m + jnp.log(denom))
        outs = [_dot(vs[h // Q_PER_KV], probs[h], TN) for h in range(N_Q_HEADS)]
        for c in range(N_Q_HEADS // 2):
            o_ref[:, c * 128:(c + 1) * 128] = jnp.concatenate([outs[2 * c], outs[2 * c + 1]], axis=0).T
        rid = lax.broadcasted_iota(jnp.int32, (N_Q_HEADS, ATTN_BLOCK), 0)
        rows = jnp.zeros((N_Q_HEADS, ATTN_BLOCK), F32)
        for h in range(N_Q_HEADS):
            rows = jnp.where(rid == h, lses[h], rows)
        rows = jnp.concatenate([rows, jnp.zeros((ATTN_BLOCK - N_Q_HEADS, ATTN_BLOCK), F32)], axis=0)
        l_ref[...] = rows.T[:, 0:N_Q_HEADS]

    blk = lambda w, f: pl.BlockSpec((ATTN_BLOCK, w), f)
    return pl.pallas_call(
        kern, name="attn_fwd", grid=(nb,),
        in_specs=[blk(512, lambda i: (i, 0)),
                  blk(128, lambda i: (jnp.maximum(i - 1, 0), 4)), blk(128, lambda i: (i, 4)),
                  blk(128, lambda i: (jnp.maximum(i - 1, 0), 5)), blk(128, lambda i: (i, 5)),
                  pl.BlockSpec((1, N_Q_HEADS), lambda i: (0, 0))],
        out_specs=[blk(512, lambda i: (i, 0)), blk(N_Q_HEADS, lambda i: (i, 0))],
        out_shape=[jax.ShapeDtypeStruct((t, D_ATTN), F32), jax.ShapeDtypeStruct((t, N_Q_HEADS), F32)],
        compiler_params=_params(("parallel",)),
    )(qkv, qkv, qkv, qkv, qkv, sinks)


def _attn_bwd(qkv, o, do, lse, sinks):
    t = qkv.shape[0]
    nb = t // ATTN_BLOCK

    def kern(q_ref, kp_ref, kc_ref, vp_ref, vc_ref, o_ref, do_ref, l_ref, s_ref,
             dq_ref, dk_ref, dv_ref, ds_ref, ck, cv):
        i = pl.program_id(0)

        @pl.when(i == 0)
        def _():
            ds_ref[...] = jnp.zeros_like(ds_ref)
            ck[...] = jnp.zeros_like(ck)
            cv[...] = jnp.zeros_like(cv)

        @pl.when(i < nb)
        def _():
            valid = _attn_mask(i)
            kband = jnp.concatenate([kp_ref[...], kc_ref[...]], axis=0)
            vband = jnp.concatenate([vp_ref[...], vc_ref[...]], axis=0)
            heads = range(N_Q_HEADS)
            sl = [slice(h * HEAD_DIM, (h + 1) * HEAD_DIM) for h in heads]
            ks = [kband[:, kh * HEAD_DIM:(kh + 1) * HEAD_DIM] for kh in range(N_KV_HEADS)]
            vs = [vband[:, kh * HEAD_DIM:(kh + 1) * HEAD_DIM] for kh in range(N_KV_HEADS)]
            qs = [q_ref[:, sl[h]] for h in heads]
            d_os = [do_ref[:, sl[h]] for h in heads]
            dobs = [d.astype(BF16) for d in d_os]
            scores = [_dot(qs[h], ks[h // Q_PER_KV], NT) for h in heads]
            dps = [_dot(dobs[h], vs[h // Q_PER_KV], NT) for h in heads]
            col_head = lax.broadcasted_iota(jnp.int32, (D_ATTN, 128), 0) // HEAD_DIM
            head_ones = (col_head == lax.broadcasted_iota(jnp.int32, (D_ATTN, 128), 1)).astype(BF16)
            deltas = _row_sums(do_ref[...] * o_ref[...], head_ones)
            pbs, dscs = [], []
            for h in heads:
                lse_h = l_ref[:, h:h + 1]
                p = jnp.where(valid, jnp.exp(scores[h] - lse_h), 0.0)
                delta = deltas[:, h:h + 1]
                pbs.append(p.astype(BF16))
                dscs.append((p * (dps[h] - delta)).astype(BF16))
                psink = jnp.exp(s_ref[0:1, h:h + 1] - lse_h)
                ds_ref[0:1, h:h + 1] += -jnp.sum(psink * delta, axis=0, keepdims=True)
            dqs = [_dot(dscs[h], ks[h // Q_PER_KV], NN) for h in heads]
            dkb = [sum(_dot(dscs[h], qs[h], TN) for h in heads if h // Q_PER_KV == kh) for kh in range(N_KV_HEADS)]
            dvb = [sum(_dot(pbs[h], dobs[h], TN) for h in heads if h // Q_PER_KV == kh) for kh in range(N_KV_HEADS)]
            for h in heads:
                dq_ref[:, sl[h]] = dqs[h]
            dk_band = jnp.concatenate(dkb, axis=1)
            dv_band = jnp.concatenate(dvb, axis=1)
            dk_ref[...] = ck[...] + dk_band[:ATTN_BLOCK]
            dv_ref[...] = cv[...] + dv_band[:ATTN_BLOCK]
            ck[...] = dk_band[ATTN_BLOCK:]
            cv[...] = dv_band[ATTN_BLOCK:]

        @pl.when(i == nb)
        def _():
            dk_ref[...] = ck[...]
            dv_ref[...] = cv[...]

    blk = lambda w, f: pl.BlockSpec((ATTN_BLOCK, w), f)
    cur = lambda i: jnp.minimum(i, nb - 1)
    prev = lambda i: jnp.clip(i - 1, 0, nb - 1)
    return pl.pallas_call(
        kern, name="attn_bwd", grid=(nb + 1,),
        in_specs=[blk(512, lambda i: (cur(i), 0)),
                  blk(128, lambda i: (prev(i), 4)), blk(128, lambda i: (cur(i), 4)),
                  blk(128, lambda i: (prev(i), 5)), blk(128, lambda i: (cur(i), 5)),
                  blk(512, lambda i: (cur(i), 0)), blk(512, lambda i: (cur(i), 0)),
                  blk(N_Q_HEADS, lambda i: (cur(i), 0)),
                  pl.BlockSpec((1, N_Q_HEADS), lambda i: (0, 0))],
        out_specs=[blk(512, lambda i: (cur(i), 0)), blk(128, lambda i: (prev(i), 0)),
                   blk(128, lambda i: (prev(i), 0)), pl.BlockSpec((1, N_Q_HEADS), lambda i: (0, 0))],
        out_shape=[jax.ShapeDtypeStruct((t, D_ATTN), F32), jax.ShapeDtypeStruct((t, D_KV), F32),
                   jax.ShapeDtypeStruct((t, D_KV), F32), jax.ShapeDtypeStruct((1, N_Q_HEADS), F32)],
        scratch_shapes=[pltpu.VMEM((ATTN_BLOCK, D_KV), F32), pltpu.VMEM((ATTN_BLOCK, D_KV), F32)],
        compiler_params=_params(("arbitrary",)),
    )(qkv, qkv, qkv, qkv, qkv, o, do, lse, sinks)


_GROUPS = ((0, D_ATTN), (D_ATTN, D_ATTN + D_S5), (D_ATTN + D_S5, D_MODEL))


def _mix_out_ln(ya, ys, yl, mg, w_out, b_out, xres, g, b, layer):
    t = xres.shape[0]
    tm = _row_tile(t, TM)

    def kern(a_ref, s_ref, l_ref, mg_ref, w_ref, bias_ref, x_ref, g_ref, b_ref, m_ref, y_ref, yb_ref, h_ref, r_ref):
        for (lo, hi), ref in zip(_GROUPS, (a_ref, s_ref, l_ref)):
            v = ref[...]
            n = v * lax.rsqrt(jnp.mean(v * v, axis=-1, keepdims=True) + RMS_EPS)
            m_ref[:, lo:hi] = (n * mg_ref[:, lo:hi]).astype(BF16)
        r = ALPHA * x_ref[...] + _dot(m_ref[...], w_ref[...], NN) + bias_ref[...]
        mu = jnp.mean(r, axis=-1, keepdims=True)
        xc = r - mu
        rstd = lax.rsqrt(jnp.mean(xc * xc, axis=-1, keepdims=True) + LN_EPS)
        xhat = xc * rstd
        h_ref[...] = xhat
        r_ref[...] = rstd
        y = xhat * g_ref[...] + b_ref[...]
        y_ref[...] = y
        yb_ref[...] = y.astype(BF16)

    rowb = lambda w: pl.BlockSpec((tm, w), lambda i: (i, 0))
    row = rowb(D_MODEL)
    vec = pl.BlockSpec((1, D_MODEL), lambda i: (0, 0))
    big = lambda dt: jax.ShapeDtypeStruct((t, D_MODEL), dt)
    return pl.pallas_call(
        kern, name="mix_out_ln", grid=(t // tm,),
        in_specs=[rowb(D_ATTN), rowb(D_S5), rowb(D_LRU), vec,
                  pl.BlockSpec((None, D_MODEL, D_MODEL), lambda i: (layer, 0, 0)), vec, row, vec, vec],
        out_specs=[row, row, row, row, pl.BlockSpec((tm, 1), lambda i: (i, 0))],
        out_shape=[big(BF16), big(F32), big(BF16), big(F32), jax.ShapeDtypeStruct((t, 1), F32)],
        compiler_params=_params(("parallel",)),
    )(ya, ys, yl, mg, w_out, b_out, xres, g, b)


def _d_mix_rms(dx1, xhat, rstd, lg, w_out, ya, ys, yl, mg, layer):
    t = dx1.shape[0]
    tm = _row_tile(t, TM)

    def kern(d_ref, h_ref, r_ref, lg_ref, w_ref, a_ref, s_ref, l_ref, g_ref,
             dr_ref, drb_ref, dlg_ref, dlb_ref, sr_ref, da_ref, ds_ref, dl_ref, dg_ref):
        @pl.when(pl.program_id(0) == 0)
        def _():
            for ref in (dlg_ref, dlb_ref, sr_ref, dg_ref):
                ref[...] = jnp.zeros_like(ref)

        _ln_bwd_tile(d_ref[...], h_ref[...], r_ref[...], lg_ref[...], dr_ref, drb_ref, dlg_ref, dlb_ref, sr_ref)
        dmix = _dot(drb_ref[...], w_ref[...], NT)
        for (lo, hi), ref, out in zip(_GROUPS, (a_ref, s_ref, l_ref), (da_ref, ds_ref, dl_ref)):
            v = ref[...]
            rstd = lax.rsqrt(jnp.mean(v * v, axis=-1, keepdims=True) + RMS_EPS)
            n = v * rstd
            dm = dmix[:, lo:hi]
            dg_ref[:, lo:hi] += jnp.sum(dm * n, axis=0, keepdims=True)
            dn = dm * g_ref[:, lo:hi]
            out[...] = rstd * (dn - n * jnp.mean(dn * n, axis=-1, keepdims=True))

    rowb = lambda w: pl.BlockSpec((tm, w), lambda i: (i, 0))
    vec = pl.BlockSpec((1, D_MODEL), lambda i: (0, 0))
    vshape = jax.ShapeDtypeStruct((1, D_MODEL), F32)
    return pl.pallas_call(
        kern, name="d_mix_rms", grid=(t // tm,),
        in_specs=[rowb(D_MODEL), rowb(D_MODEL), pl.BlockSpec((tm, 1), lambda i: (i, 0)), vec,
                  pl.BlockSpec((None, D_MODEL, D_MODEL), lambda i: (layer, 0, 0)),
                  rowb(D_ATTN), rowb(D_S5), rowb(D_LRU), vec],
        out_specs=[rowb(D_MODEL), rowb(D_MODEL), vec, vec, vec, rowb(D_ATTN), rowb(D_S5), rowb(D_LRU), vec],
        out_shape=[jax.ShapeDtypeStruct((t, D_MODEL), F32), jax.ShapeDtypeStruct((t, D_MODEL), BF16),
                   vshape, vshape, vshape, jax.ShapeDtypeStruct((t, D_ATTN), F32),
                   jax.ShapeDtypeStruct((t, D_S5), F32), jax.ShapeDtypeStruct((t, D_LRU), F32), vshape],
        compiler_params=_params(("arbitrary",)),
    )(dx1, xhat, rstd, lg, w_out, ya, ys, yl, mg)


def _matmul_ln(name, a, w, bias, xres, g, b, a_blk, a_map, w_blk, parts, layer):
    t = xres.shape[0]
    tm = a_blk[-2]

    def kern(a_ref, w_ref, bias_ref, x_ref, g_ref, b_ref, y_ref, yb_ref, h_ref, r_ref):
        if parts is None:
            f = _dot(a_ref[...], w_ref[...], NN)
        else:
            f = sum(_dot(a_ref[j], w_ref[j], NN) for j in range(parts))
        r = ALPHA * x_ref[...] + f + bias_ref[...]
        mu = jnp.mean(r, axis=-1, keepdims=True)
        xc = r - mu
        rstd = lax.rsqrt(jnp.mean(xc * xc, axis=-1, keepdims=True) + LN_EPS)
        xhat = xc * rstd
        h_ref[...] = xhat
        r_ref[...] = rstd
        y = xhat * g_ref[...] + b_ref[...]
        y_ref[...] = y
        yb_ref[...] = y.astype(BF16)

    row = pl.BlockSpec((tm, D_MODEL), lambda i: (i, 0))
    vec = pl.BlockSpec((1, D_MODEL), lambda i: (0, 0))
    big = lambda dt: jax.ShapeDtypeStruct((t, D_MODEL), dt)
    return pl.pallas_call(
        kern, name=name, grid=(t // tm,),
        in_specs=[pl.BlockSpec(a_blk, a_map), pl.BlockSpec((None,) + w_blk, lambda i: (layer,) + (0,) * len(w_blk)), vec, row, vec, vec],
        out_specs=[row, row, row, pl.BlockSpec((tm, 1), lambda i: (i, 0))],
        out_shape=[big(F32), big(BF16), big(F32), jax.ShapeDtypeStruct((t, 1), F32)],
        compiler_params=_params(("parallel",)),
    )(a, w, bias, xres, g, b)


def _ln_bwd(dy, xhat, rstd, g):
    t = dy.shape[0]
    tm = _row_tile(t, TM)

    def kern(d_ref, h_ref, r_ref, g_ref, dr_ref, drb_ref, dg_ref, db_ref, sr_ref):
        @pl.when(pl.program_id(0) == 0)
        def _():
            dg_ref[...] = jnp.zeros_like(dg_ref)
            db_ref[...] = jnp.zeros_like(db_ref)
            sr_ref[...] = jnp.zeros_like(sr_ref)

        d = d_ref[...]
        xhat = h_ref[...]
        dg_ref[...] += jnp.sum(d * xhat, axis=0, keepdims=True)
        db_ref[...] += jnp.sum(d, axis=0, keepdims=True)
        dh = d * g_ref[...]
        dr = r_ref[...] * (dh - jnp.mean(dh, axis=-1, keepdims=True)
                           - xhat * jnp.mean(dh * xhat, axis=-1, keepdims=True))
        dr_ref[...] = dr
        drb_ref[...] = dr.astype(BF16)
        sr_ref[...] += jnp.sum(dr, axis=0, keepdims=True)

    row = pl.BlockSpec((tm, D_MODEL), lambda i: (i, 0))
    vec = pl.BlockSpec((1, D_MODEL), lambda i: (0, 0))
    vshape = jax.ShapeDtypeStruct((1, D_MODEL), F32)
    return pl.pallas_call(
        kern, name="ln_bwd", grid=(t // tm,),
        in_specs=[row, row, pl.BlockSpec((tm, 1), lambda i: (i, 0)), vec],
        out_specs=[row, row, vec, vec, vec],
        out_shape=[jax.ShapeDtypeStruct((t, D_MODEL), F32), jax.ShapeDtypeStruct((t, D_MODEL), BF16),
                   vshape, vshape, vshape],
        compiler_params=_params(("arbitrary",)),
    )(dy, xhat, rstd, g)


def _ln_bwd_tile(d, xhat, rstd, g, dr_ref, drb_ref, dg_ref, db_ref, sr_ref):
    dg_ref[...] += jnp.sum(d * xhat, axis=0, keepdims=True)
    db_ref[...] += jnp.sum(d, axis=0, keepdims=True)
    dh = d * g
    dr = rstd * (dh - jnp.mean(dh, axis=-1, keepdims=True) - xhat * jnp.mean(dh * xhat, axis=-1, keepdims=True))
    dr_ref[...] = dr
    drb_ref[...] = dr.astype(BF16)
    sr_ref[...] += jnp.sum(dr, axis=0, keepdims=True)


def _loss_head(y, target, xhat, rstd, g):
    t = y.shape[0]
    tm = _row_tile(t, TM)

    def kern(y_ref, t_ref, h_ref, r_ref, g_ref, l_ref, dr_ref, drb_ref, dg_ref, db_ref, sr_ref):
        @pl.when(pl.program_id(0) == 0)
        def _():
            for ref in (l_ref, dg_ref, db_ref, sr_ref):
                ref[...] = jnp.zeros_like(ref)

        err = y_ref[...] - t_ref[...]
        part = jnp.sum(jnp.sum(err * err, axis=-1, keepdims=True), axis=0, keepdims=True)
        l_ref[...] += jnp.broadcast_to(part * (0.5 / D_MODEL), l_ref.shape)
        _ln_bwd_tile(err * (1.0 / D_MODEL), h_ref[...], r_ref[...], g_ref[...], dr_ref, drb_ref, dg_ref, db_ref, sr_ref)

    row = pl.BlockSpec((tm, D_MODEL), lambda i: (i, 0))
    vec = pl.BlockSpec((1, D_MODEL), lambda i: (0, 0))
    vshape = jax.ShapeDtypeStruct((1, D_MODEL), F32)
    return pl.pallas_call(
        kern, name="loss_head", grid=(t // tm,),
        in_specs=[row, row, row, pl.BlockSpec((tm, 1), lambda i: (i, 0)), vec],
        out_specs=[pl.BlockSpec((1, 128), lambda i: (0, 0)), row, row, vec, vec, vec],
        out_shape=[jax.ShapeDtypeStruct((1, 128), F32), jax.ShapeDtypeStruct((t, D_MODEL), F32),
                   jax.ShapeDtypeStruct((t, D_MODEL), BF16), vshape, vshape, vshape],
        compiler_params=_params(("arbitrary",)),
    )(y, target, xhat, rstd, g)


HALO = 8


def _ffn_hidden_fwd(xb, wg, wu, cw, cb, layer):
    t = xb.shape[0]
    tm = _row_tile(t, TMM)

    def kern(x_ref, wg_ref, wu_ref, cw_ref, cb_ref, g_ref, c_ref, u_ref, h_ref, ext):
        @pl.when(pl.program_id(1) == 0)
        def _():
            ext[0:HALO, :] = jnp.zeros((HALO, FF_SH), F32)

        x = x_ref[...]
        gb = _dot(x, wg_ref[...], NN).astype(BF16)
        ub = _dot(x, wu_ref[...], NN).astype(BF16)
        g_ref[...] = gb
        u_ref[...] = ub
        g = gb.astype(F32)
        w = [cw_ref[k:k + 1, :] for k in range(FFN_CONV)]
        body = cb_ref[...] + w[2] * g + w[1] * pltpu.roll(g, 1, 0) + w[0] * pltpu.roll(g, 2, 0)
        ext[HALO:, :] = g[0:HALO, :]
        head = cb_ref[...] + sum(w[k] * ext[pl.ds(HALO - (FFN_CONV - 1) + k, HALO), :] for k in range(FFN_CONV))
        gcb = jnp.concatenate([head, body[HALO:, :]], axis=0).astype(BF16)
        c_ref[...] = gcb
        gc = gcb.astype(F32)
        h_ref[...] = (gc * _sigmoid(gc) * ub.astype(F32)).astype(BF16)
        ext[0:HALO, :] = g[tm - HALO:, :]

    col = pl.BlockSpec((None, tm, FF_SH), lambda j, i: (j, i, 0))
    wspec = pl.BlockSpec((None, None, D_MODEL, FF_SH), lambda j, i: (layer, j, 0, 0))
    big = jax.ShapeDtypeStruct((N_CHIPS, t, FF_SH), BF16)
    return pl.pallas_call(
        kern, name="ffn_hidden_fwd", grid=(N_CHIPS, t // tm),
        in_specs=[pl.BlockSpec((tm, D_MODEL), lambda j, i: (i, 0)), wspec, wspec,
                  pl.BlockSpec((None, FFN_CONV, FF_SH), lambda j, i: (j, 0, 0)),
                  pl.BlockSpec((None, 1, FF_SH), lambda j, i: (j, 0, 0))],
        out_specs=[col, col, col, col], out_shape=[big, big, big, big],
        scratch_shapes=[pltpu.VMEM((2 * HALO, FF_SH), F32)],
        compiler_params=_params(("parallel", "arbitrary")),
    )(xb, wg, wu, cw, cb)


def _ffn_hidden_bwd(drb, gpre, gconv, up, wd, cw, layer):
    t = drb.shape[0]
    tm = _row_tile(t, TMM)
    nt = t // tm
    rb = lambda i: nt - 1 - i

    def kern(d_ref, g_ref, c_ref, u_ref, wd_ref, cw_ref, du_ref, dg_ref, dw_ref, db_ref, ext):
        @pl.when(pl.program_id(1) == 0)
        def _():
            dw_ref[...] = jnp.zeros_like(dw_ref)
            db_ref[...] = jnp.zeros_like(db_ref)
            ext[HALO:, :] = jnp.zeros((HALO, FF_SH), F32)

        dh = _dot(d_ref[...], wd_ref[...], NT)
        gc = c_ref[...].astype(F32)
        sg = _sigmoid(gc)
        du_ref[...] = (dh * (gc * sg)).astype(BF16)
        dgc = dh * u_ref[...].astype(F32) * (sg * (1.0 + gc * (1.0 - sg)))
        db_ref[...] += jnp.sum(dgc, axis=0, keepdims=True)
        g = g_ref[...].astype(F32)
        w = [cw_ref[k:k + 1, :] for k in range(FFN_CONV)]
        taps = [pltpu.roll(dgc, tm - 2, 0), pltpu.roll(dgc, tm - 1, 0), dgc]
        body = sum(w[k] * taps[k] for k in range(FFN_CONV))
        last = slice(tm - HALO, tm)
        ext[0:HALO, :] = dgc[last, :]
        tail_taps = [ext[pl.ds(FFN_CONV - 1 - k, HALO), :] for k in range(FFN_CONV)]
        tail = sum(w[k] * tail_taps[k] for k in range(FFN_CONV))
        dg_ref[...] = jnp.concatenate([body[0:tm - HALO, :], tail], axis=0).astype(BF16)
        for k in range(FFN_CONV):
            dw_ref[k:k + 1, :] += (jnp.sum(g * taps[k], axis=0, keepdims=True)
                                   + jnp.sum(g[last, :] * (tail_taps[k] - taps[k][last, :]), axis=0, keepdims=True))
        ext[HALO:, :] = dgc[0:HALO, :]

    col = pl.BlockSpec((None, tm, FF_SH), lambda j, i: (j, rb(i), 0))
    cws = pl.BlockSpec((None, FFN_CONV, FF_SH), lambda j, i: (j, 0, 0))
    cbs = pl.BlockSpec((None, 1, FF_SH), lambda j, i: (j, 0, 0))
    big = jax.ShapeDtypeStruct((N_CHIPS, t, FF_SH), BF16)
    return pl.pallas_call(
        kern, name="ffn_hidden_bwd", grid=(N_CHIPS, nt),
        in_specs=[pl.BlockSpec((tm, D_MODEL), lambda j, i: (rb(i), 0)), col, col, col,
                  pl.BlockSpec((None, None, FF_SH, D_MODEL), lambda j, i: (layer, j, 0, 0)), cws],
        out_specs=[col, col, cws, cbs],
        out_shape=[big, big, jax.ShapeDtypeStruct((N_CHIPS, FFN_CONV, FF_SH), F32),
                   jax.ShapeDtypeStruct((N_CHIPS, 1, FF_SH), F32)],
        scratch_shapes=[pltpu.VMEM((2 * HALO, FF_SH), F32)],
        compiler_params=_params(("parallel", "arbitrary")),
    )(drb, gpre, gconv, up, wd, cw)


def _s5_coefs(ar, ai, reverse):
    if reverse:
        ai = -ai
    pw = [(ar, ai)]
    for _ in range(SUBLANES - 1):
        pr, pi = pw[-1]
        pw.append((pr * ar - pi * ai, pr * ai + pi * ar))
    rows = jnp.arange(SUBLANES)[:, None]
    out = []
    for s in (1, 2, 4):
        keep = (rows + s <= SUBLANES - 1) if reverse else (rows >= s)
        out += [jnp.where(keep, pw[s - 1][0][None], 0.0), jnp.where(keep, pw[s - 1][1][None], 0.0)]
    order = list(range(SUBLANES - 1, -1, -1)) if reverse else list(range(SUBLANES))
    out += [jnp.stack([pw[k][0] for k in order]), jnp.stack([pw[k][1] for k in order])]
    return jnp.stack(out).astype(F32)


def _s5_scan(buf, coef_ref, carry, tm, reverse):
    n8 = tm // SUBLANES

    def body(it, c):
        cre, cim = c
        blk = (n8 - 1 - it) if reverse else it
        r0 = pl.multiple_of(blk * SUBLANES, SUBLANES)
        xre = buf[pl.ds(r0, SUBLANES), 0:N_STATE]
        xim = buf[pl.ds(r0, SUBLANES), N_STATE:]
        for idx, s in enumerate((1, 2, 4)):
            sh = (SUBLANES - s) if reverse else s
            sre = pltpu.roll(xre, sh, 0)
            sim = pltpu.roll(xim, sh, 0)
            are = coef_ref[2 * idx]
            aim = coef_ref[2 * idx + 1]
            xre, xim = xre + are * sre - aim * sim, xim + are * sim + aim * sre
        pre = coef_ref[6]
        pim = coef_ref[7]
        hre = xre + pre * cre - pim * cim
        him = xim + pre * cim + pim * cre
        buf[pl.ds(r0, SUBLANES), 0:N_STATE] = hre
        buf[pl.ds(r0, SUBLANES), N_STATE:] = him
        row = 0 if reverse else SUBLANES - 1
        return (jnp.broadcast_to(hre[row:row + 1], (SUBLANES, N_STATE)),
                jnp.broadcast_to(him[row:row + 1], (SUBLANES, N_STATE)))

    cre, cim = lax.fori_loop(0, n8, body, (carry[:, 0:N_STATE], carry[:, N_STATE:]))
    carry[:, 0:N_STATE] = cre
    carry[:, N_STATE:] = cim


def _real_scan(abuf, bbuf, carry, tm, reverse):
    n8 = tm // SUBLANES
    width = bbuf.shape[1]

    def body(it, c):
        blk = (n8 - 1 - it) if reverse else it
        r0 = pl.multiple_of(blk * SUBLANES, SUBLANES)
        a = abuf[pl.ds(r0, SUBLANES), :]
        b = bbuf[pl.ds(r0, SUBLANES), :]
        rows = lax.broadcasted_iota(jnp.int32, (SUBLANES, width), 0)
        for s in (1, 2, 4):
            sh = (SUBLANES - s) if reverse else s
            keep = (rows + s <= SUBLANES - 1) if reverse else (rows >= s)
            sa = pltpu.roll(a, sh, 0)
            sb = pltpu.roll(b, sh, 0)
            b = b + a * jnp.where(keep, sb, 0.0)
            a = a * jnp.where(keep, sa, 1.0)
        h = b + a * c
        bbuf[pl.ds(r0, SUBLANES), :] = h
        row = 0 if reverse else SUBLANES - 1
        return jnp.broadcast_to(h[row:row + 1], (SUBLANES, width))

    carry[...] = lax.fori_loop(0, n8, body, carry[...])


def _dot(a, b, dims):
    return lax.dot_general(a, b, (dims, ((), ())), preferred_element_type=F32)


TS5 = 512
HALO16 = 16


def _s5_fwd(proj, bmat, coef, cmat, dvec, gw, gb):
    t = proj.shape[0]
    tm = _row_tile(t, TS5)

    def kern(u_ref, b_ref, coef_ref, c_ref, d_ref, gw_ref, gb_ref, h_ref, y_ref, hbuf, carry):
        @pl.when(pl.program_id(0) == 0)
        def _():
            carry[...] = jnp.zeros_like(carry)

        u = u_ref[...]
        hbuf[...] = _dot(u.astype(BF16), b_ref[...], NN)
        _s5_scan(hbuf, coef_ref, carry, tm, False)
        hb = hbuf[...].astype(BF16)
        h_ref[...] = hb
        y = _dot(hb, c_ref[...], NN) + d_ref[...] * u
        ys = _gelu(y)
        z = _dot(ys.astype(BF16), gw_ref[...], NN) + gb_ref[...]
        y_ref[...] = ys * _sigmoid(z)

    full = lambda shp: pl.BlockSpec(shp, lambda i: (0,) * len(shp))
    return pl.pallas_call(
        kern, name="s5_fwd", grid=(t // tm,),
        in_specs=[pl.BlockSpec((tm, D_S5), lambda i: (i, 0)), full((D_S5, 2 * N_STATE)),
                  full((8, SUBLANES, N_STATE)), full((2 * N_STATE, D_S5)), full((1, D_S5)),
                  full((D_S5, D_S5)), full((1, D_S5))],
        out_specs=[pl.BlockSpec((tm, 2 * N_STATE), lambda i: (i, 0)), pl.BlockSpec((tm, D_S5), lambda i: (i, 0))],
        out_shape=[jax.ShapeDtypeStruct((t, 2 * N_STATE), BF16), jax.ShapeDtypeStruct((t, D_S5), F32)],
        scratch_shapes=[pltpu.VMEM((tm, 2 * N_STATE), F32), pltpu.VMEM((SUBLANES, 2 * N_STATE), F32)],
        compiler_params=_params(("arbitrary",)),
    )(proj, bmat, coef, cmat, dvec, gw, gb)


def _s5_bwd(proj, h, dout, bmat, coef_b, cmat, dvec, gw, gb):
    t = proj.shape[0]
    tm = _row_tile(t, TS5)
    nt = t // tm
    rb = lambda i: nt - 1 - i

    def kern(u_ref, h_ref, hp_ref, d_ref, b_ref, coef_ref, c_ref, dv_ref, gw_ref, gb_ref,
             du_ref, dc_ref, db_ref, da_ref, dd_ref, dgw_ref, dgb_ref, gbuf, hext, carry):
        i = pl.program_id(0)

        @pl.when(i == 0)
        def _():
            carry[...] = jnp.zeros_like(carry)
            for r in (dc_ref, db_ref, da_ref, dd_ref, dgw_ref, dgb_ref):
                r[...] = jnp.zeros_like(r)

        u = u_ref[...]
        hb = h_ref[...]
        y = _dot(hb, c_ref[...], NN) + dv_ref[...] * u
        ys = _gelu(y)
        ysb = ys.astype(BF16)
        sg = _sigmoid(_dot(ysb, gw_ref[...], NN) + gb_ref[...])
        d_o = d_ref[...]
        dz = d_o * ys * sg * (1.0 - sg)
        dzb = dz.astype(BF16)
        dys = d_o * sg + _dot(dzb, gw_ref[...], NT)
        dgw_ref[...] += _dot(ysb, dzb, TN)
        dgb_ref[...] += jnp.sum(dz, axis=0, keepdims=True)
        dy = dys * _gelu_grad(y)
        dd_ref[...] += jnp.sum(dy * u, axis=0, keepdims=True)
        dyb = dy.astype(BF16)
        dc_ref[...] += _dot(hb, dyb, TN)
        gbuf[...] = _dot(dyb, c_ref[...], NT)
        _s5_scan(gbuf, coef_ref, carry, tm, True)
        g = gbuf[...]
        first = jnp.where(i < nt - 1, hp_ref[HALO16 - 1:HALO16, :].astype(F32), 0.0)
        hext[SUBLANES - 1:SUBLANES, :] = first
        hext[SUBLANES:, :] = hb.astype(F32)
        hprev = hext[pl.ds(SUBLANES - 1, tm), :]
        gre, gim = g[:, 0:N_STATE], g[:, N_STATE:]
        pre, pim = hprev[:, 0:N_STATE], hprev[:, N_STATE:]
        da_ref[0:1, :] += jnp.sum(gre * pre + gim * pim, axis=0, keepdims=True)
        da_ref[1:2, :] += jnp.sum(gim * pre - gre * pim, axis=0, keepdims=True)
        gb16 = g.astype(BF16)
        db_ref[...] += _dot(u.astype(BF16), gb16, TN)
        du_ref[...] = dy * dv_ref[...] + _dot(gb16, b_ref[...], NT)

    full = lambda shp: pl.BlockSpec(shp, lambda i: (0,) * len(shp))
    shape = lambda shp: jax.ShapeDtypeStruct(shp, F32)
    return pl.pallas_call(
        kern, name="s5_bwd", grid=(nt,),
        in_specs=[pl.BlockSpec((tm, D_S5), lambda i: (rb(i), 0)),
                  pl.BlockSpec((tm, 2 * N_STATE), lambda i: (rb(i), 0)),
                  pl.BlockSpec((HALO16, 2 * N_STATE), lambda i: (jnp.maximum(rb(i) * (tm // HALO16) - 1, 0), 0)),
                  pl.BlockSpec((tm, D_S5), lambda i: (rb(i), 0)),
                  full((D_S5, 2 * N_STATE)), full((8, SUBLANES, N_STATE)), full((2 * N_STATE, D_S5)),
                  full((1, D_S5)), full((D_S5, D_S5)), full((1, D_S5))],
        out_specs=[pl.BlockSpec((tm, D_S5), lambda i: (rb(i), 0)), full((2 * N_STATE, D_S5)),
                   full((D_S5, 2 * N_STATE)), full((2, N_STATE)), full((1, D_S5)), full((D_S5, D_S5)),
                   full((1, D_S5))],
        out_shape=[shape((t, D_S5)), shape((2 * N_STATE, D_S5)), shape((D_S5, 2 * N_STATE)),
                   shape((2, N_STATE)), shape((1, D_S5)), shape((D_S5, D_S5)), shape((1, D_S5))],
        scratch_shapes=[pltpu.VMEM((tm, 2 * N_STATE), F32), pltpu.VMEM((tm + SUBLANES, 2 * N_STATE), F32),
                        pltpu.VMEM((SUBLANES, 2 * N_STATE), F32)],
        compiler_params=_params(("arbitrary",)),
    )(proj, h, h, dout, bmat, coef_b, cmat, dvec, gw, gb)


def _lru_gates(ext, x_ref, p_ref, cw_ref, cb_ref, wx_ref, bx_ref, wa_ref, ba_ref, ap_ref, first_tile, row0, tm):
    ext[0:HALO, :] = jnp.where(first_tile, 0.0, p_ref[...])
    ext[HALO:, :] = x_ref[...]
    taps = [ext[pl.ds(HALO - (LRU_CONV - 1) + k, tm), :] for k in range(LRU_CONV)]
    xc = cb_ref[...] + sum(cw_ref[k:k + 1, :] * taps[k] for k in range(LRU_CONV))
    xcb = xc.astype(BF16)
    gx = _sigmoid(_dot(xcb, wx_ref[...], NN) + bx_ref[...])
    ga = _sigmoid(_dot(xcb, wa_ref[...], NN) + ba_ref[...])
    z = -ap_ref[...]
    sp = jnp.maximum(z, 0.0) + jnp.log(1.0 + jnp.exp(-jnp.abs(z)))
    log_a = -LRU_C * ga * sp
    a = jnp.exp(log_a)
    tok = row0 + lax.broadcasted_iota(jnp.int32, a.shape, 0)
    is0 = tok == 0
    mult = jnp.where(is0, 1.0, jnp.sqrt(1.0 - jnp.exp(2.0 * log_a)))
    return taps, xc, xcb, gx, ga, sp, a, mult, is0


def _lru_specs(tm, blk_of):
    col = lambda cidx: pl.BlockSpec((tm, D_LRU), lambda i: (blk_of(i), cidx))
    prev = lambda cidx: pl.BlockSpec((HALO, D_LRU), lambda i: (jnp.maximum(blk_of(i) * (tm // HALO) - 1, 0), cidx))
    full = lambda shp: pl.BlockSpec(shp, lambda i: (0,) * len(shp))
    wts = [full((LRU_CONV, D_LRU)), full((1, D_LRU)), full((D_LRU, D_LRU)), full((1, D_LRU)),
           full((D_LRU, D_LRU)), full((1, D_LRU)), full((1, D_LRU))]
    return col, prev, full, wts


def _lru_fwd(proj, cw, cb, wx, bx, wa, ba, ap):
    t = proj.shape[0]
    tm = _row_tile(t, TMM)

    def kern(x_ref, p_ref, g_ref, cw_ref, cb_ref, wx_ref, bx_ref, wa_ref, ba_ref, ap_ref,
             y_ref, h_ref, ext, abuf, carry):
        i = pl.program_id(0)

        @pl.when(i == 0)
        def _():
            carry[...] = jnp.zeros_like(carry)

        _, xc, _, gx, _, _, a, mult, _ = _lru_gates(ext, x_ref, p_ref, cw_ref, cb_ref, wx_ref, bx_ref, wa_ref,
                                                    ba_ref, ap_ref, i == 0, i * tm, tm)
        abuf[...] = a
        h_ref[...] = mult * gx * xc
        _real_scan(abuf, h_ref, carry, tm, False)
        y_ref[...] = h_ref[...] * _gelu(g_ref[...])

    col, prev, full, wts = _lru_specs(tm, lambda i: i)
    out = pl.BlockSpec((tm, D_LRU), lambda i: (i, 0))
    return pl.pallas_call(
        kern, name="lru_fwd", grid=(t // tm,),
        in_specs=[col(1), prev(1), col(2)] + wts, out_specs=[out, out],
        out_shape=[jax.ShapeDtypeStruct((t, D_LRU), F32), jax.ShapeDtypeStruct((t, D_LRU), F32)],
        scratch_shapes=[pltpu.VMEM((tm + HALO, D_LRU), F32), pltpu.VMEM((tm, D_LRU), F32),
                        pltpu.VMEM((SUBLANES, D_LRU), F32)],
        compiler_params=_params(("arbitrary",)),
    )(proj, proj, proj, cw, cb, wx, bx, wa, ba, ap)


def _lru_bwd(proj, h, dout, cw, cb, wx, bx, wa, ba, ap):
    t = proj.shape[0]
    tm = _row_tile(t, TMM)
    nt = t // tm
    rb = lambda i: nt - 1 - i

    def kern(x_ref, p_ref, g_ref, h_ref, hp_ref, d_ref, cw_ref, cb_ref, wx_ref, bx_ref, wa_ref, ba_ref, ap_ref,
             dxc_ref, dg_ref, dcw_ref, dcb_ref, dwx_ref, dbx_ref, dwa_ref, dba_ref, dap_ref,
             ext, aext, abuf, gbuf, carry, acarry):
        i = pl.program_id(0)
        blk = nt - 1 - i

        @pl.when(i == 0)
        def _():
            carry[...] = jnp.zeros_like(carry)
            acarry[...] = jnp.zeros_like(acarry)
            for r in (dcw_ref, dcb_ref, dwx_ref, dbx_ref, dwa_ref, dba_ref, dap_ref):
                r[...] = jnp.zeros_like(r)

        taps, xc, xcb, gx, ga, sp, a, mult, is0 = _lru_gates(
            ext, x_ref, p_ref, cw_ref, cb_ref, wx_ref, bx_ref, wa_ref, ba_ref, ap_ref, blk == 0, blk * tm, tm)
        gate = g_ref[...]
        d_o = d_ref[...]
        hcur = h_ref[...]
        dg_ref[...] = d_o * hcur * _gelu_grad(gate)
        aext[0:tm, :] = a
        aext[tm:, :] = acarry[...]
        abuf[...] = aext[pl.ds(1, tm), :]
        gbuf[...] = d_o * _gelu(gate)
        _real_scan(abuf, gbuf, carry, tm, True)
        acarry[...] = jnp.broadcast_to(a[0:1], acarry.shape)
        g = gbuf[...]
        ext[0:HALO, :] = jnp.where(blk == 0, 0.0, hp_ref[...])
        ext[HALO:, :] = hcur
        hprev = ext[pl.ds(HALO - 1, tm), :]
        dmult = jnp.where(is0, 0.0, g * gx * xc)
        dgx = g * mult * xc
        dxc = g * mult * gx
        dlog_a = g * hprev * a - dmult * (a * a) / mult
        dga = dlog_a * (-LRU_C * sp)
        dsp = jnp.sum(dlog_a * (-LRU_C * ga), axis=0, keepdims=True)
        dap_ref[...] += dsp * (-_sigmoid(-ap_ref[...]))
        dpa = (dga * ga * (1.0 - ga))
        dpx = (dgx * gx * (1.0 - gx))
        dpab, dpxb = dpa.astype(BF16), dpx.astype(BF16)
        dwx_ref[...] += _dot(xcb, dpxb, TN)
        dwa_ref[...] += _dot(xcb, dpab, TN)
        dbx_ref[...] += jnp.sum(dpx, axis=0, keepdims=True)
        dba_ref[...] += jnp.sum(dpa, axis=0, keepdims=True)
        dxc = dxc + _dot(dpxb, wx_ref[...], NT) + _dot(dpab, wa_ref[...], NT)
        dxc_ref[...] = dxc
        dcb_ref[...] += jnp.sum(dxc, axis=0, keepdims=True)
        for k in range(LRU_CONV):
            dcw_ref[k:k + 1, :] += jnp.sum(dxc * taps[k], axis=0, keepdims=True)

    col, prev, full, wts = _lru_specs(tm, rb)
    row = pl.BlockSpec((tm, D_LRU), lambda i: (rb(i), 0))
    hprev_spec = pl.BlockSpec((HALO, D_LRU), lambda i: (jnp.maximum(rb(i) * (tm // HALO) - 1, 0), 0))
    shape = lambda shp: jax.ShapeDtypeStruct(shp, F32)
    vec = (1, D_LRU)
    sq = (D_LRU, D_LRU)
    return pl.pallas_call(
        kern, name="lru_bwd", grid=(nt,),
        in_specs=[col(1), prev(1), col(2), row, hprev_spec, row] + wts,
        out_specs=[row, row, full((LRU_CONV, D_LRU)), full(vec), full(sq), full(vec), full(sq), full(vec), full(vec)],
        out_shape=[shape((t, D_LRU)), shape((t, D_LRU)), shape((LRU_CONV, D_LRU)), shape(vec), shape(sq),
                   shape(vec), shape(sq), shape(vec), shape(vec)],
        scratch_shapes=[pltpu.VMEM((tm + HALO, D_LRU), F32), pltpu.VMEM((tm + HALO, D_LRU), F32),
                        pltpu.VMEM((tm, D_LRU), F32), pltpu.VMEM((tm, D_LRU), F32),
                        pltpu.VMEM((SUBLANES, D_LRU), F32), pltpu.VMEM((SUBLANES, D_LRU), F32)],
        compiler_params=_params(("arbitrary",)),
    )(proj, proj, proj, h, h, dout, cw, cb, wx, bx, wa, ba, ap)


def _assemble_dproj(dq, dk, dv, du, dxc, dgate, cos, sin_s, cw):
    t = dq.shape[0]
    tm = _row_tile(t, TM)
    nt = t // tm

    def kern(dq_ref, dk_ref, dv_ref, du_ref, dx_ref, dn_ref, dg_ref, c_ref, s_ref, cw_ref, o_ref, b_ref, ext):
        i = pl.program_id(0)

        @pl.when(i == 0)
        def _():
            b_ref[...] = jnp.zeros_like(b_ref)

        def put(lo, val):
            hi = lo + val.shape[1]
            o_ref[:, lo:hi] = val.astype(BF16)
            b_ref[:, lo:hi] += jnp.sum(val, axis=0, keepdims=True)

        c = c_ref[...]
        s = s_ref[...]
        for ch in range(4):
            x = dq_ref[:, ch * 128:(ch + 1) * 128] * (HEAD_DIM ** -0.5)
            put(ch * 128, x * c - _rope_swap(x) * s)
        x = dk_ref[...]
        put(512, x * c - _rope_swap(x) * s)
        put(640, dv_ref[...])
        put(768, du_ref[...])
        ext[0:tm, :] = dx_ref[...]
        ext[tm:, :] = jnp.where(i < nt - 1, dn_ref[...], 0.0)
        put(1024, sum(cw_ref[k:k + 1, :] * ext[pl.ds(LRU_CONV - 1 - k, tm), :] for k in range(LRU_CONV)))
        put(1280, dg_ref[...])

    row = lambda w: pl.BlockSpec((tm, w), lambda i: (i, 0))
    nxt = pl.BlockSpec((HALO, D_LRU), lambda i: (jnp.minimum((i + 1) * (tm // HALO), t // HALO - 1), 0))
    return pl.pallas_call(
        kern, name="assemble_dproj", grid=(nt,),
        in_specs=[row(512), row(128), row(128), row(256), row(256), nxt, row(256), row(128), row(128),
                  pl.BlockSpec((LRU_CONV, D_LRU), lambda i: (0, 0))],
        out_specs=[row(D_IN), pl.BlockSpec((1, D_IN), lambda i: (0, 0))],
        out_shape=[jax.ShapeDtypeStruct((t, D_IN), BF16), jax.ShapeDtypeStruct((1, D_IN), F32)],
        scratch_shapes=[pltpu.VMEM((tm + HALO, D_LRU), F32)],
        compiler_params=_params(("arbitrary",)),
    )(dq, dk, dv, du, dxc, dxc, dgate, cos, sin_s, cw)


def _blockdiag_s5(bbar_re, bbar_im, c_re, c_im):
    eye = jnp.eye(S5_GROUPS, dtype=F32)
    b_of = lambda m: jnp.einsum('gpc,gh->gchp', m, eye).reshape(D_S5, N_STATE)
    c_of = lambda m: jnp.einsum('gcp,gh->gphc', m, eye).reshape(N_STATE, D_S5)
    bmat = jnp.concatenate([b_of(bbar_re), b_of(bbar_im)], axis=1)
    cmat = jnp.concatenate([c_of(c_re), -c_of(c_im)], axis=0)
    return bmat, cmat


def _s5_prepare(a_re, a_im, b_re, b_im, c_re, c_im, log_dt):
    lam_re = jnp.minimum(a_re, -1e-4)
    lam_im = a_im
    dt = jnp.exp(log_dt)[:, None]
    decay = jnp.exp(dt * lam_re)
    ang = dt * lam_im
    abar_re = decay * jnp.cos(ang)
    abar_im = decay * jnp.sin(ang)
    den = jnp.square(lam_re) + jnp.square(lam_im)
    nr = abar_re - 1.0
    ni = abar_im
    coef_re = (nr * lam_re + ni * lam_im) / den
    coef_im = (ni * lam_re - nr * lam_im) / den
    bbar_re = coef_re[..., None] * b_re - coef_im[..., None] * b_im
    bbar_im = coef_re[..., None] * b_im + coef_im[..., None] * b_re
    bmat, cmat = _blockdiag_s5(bbar_re, bbar_im, c_re, c_im)
    return abar_re.reshape(N_STATE), abar_im.reshape(N_STATE), bmat, cmat


def _blockdiag_lru(w):
    eye = jnp.eye(LRU_HEADS, dtype=F32)
    return jnp.einsum('hij,hk->hikj', w, eye).reshape(D_LRU, D_LRU)


def _rope_tables(t):
    inv_freq = ROPE_THETA ** (-jnp.arange(0, HEAD_DIM, 2, dtype=F32) / HEAD_DIM)
    ang = jnp.arange(t, dtype=F32)[:, None] * inv_freq[None, :]
    cos, sin = jnp.cos(ang), jnp.sin(ang)
    return jnp.tile(jnp.concatenate([cos, cos], axis=1), (1, 2)), jnp.tile(jnp.concatenate([-sin, sin], axis=1), (1, 2))


def _vec(v):
    return v.reshape(1, -1)


def _layer_weights(p):
    abar_re, abar_im, bmat, cmat = _s5_prepare(p['s5_a_re'], p['s5_a_im'], p['s5_b_re'], p['s5_b_im'],
                                               p['s5_c_re'], p['s5_c_im'], p['s5_log_dt'])
    return dict(
        coef_f=_s5_coefs(abar_re, abar_im, False), coef_b=_s5_coefs(abar_re, abar_im, True),
        bmat=bmat.astype(BF16), cmat=cmat.astype(BF16),
        wx=_blockdiag_lru(p['lru_wx']).astype(BF16), wa=_blockdiag_lru(p['lru_wa']).astype(BF16),
        gw=p['s5_glu_w'].astype(BF16))


def _layer_fwd(x, xb, p, w, cos, sin_s):
    t = x.shape[0]
    tm = _row_tile(t, TM)
    layer = p['layer']
    qkv, uxg = _in_proj(xb, p['w_in'], _vec(p['b_in']), cos, sin_s, layer)
    ya, lse = _attn_fwd(qkv, _vec(p['attn_sinks']))
    h5, ys = _s5_fwd(uxg, w['bmat'], w['coef_f'], w['cmat'], _vec(p['s5_d']), w['gw'], _vec(p['s5_glu_b']))
    lru_w = (p['lru_conv_w'], _vec(p['lru_conv_b']), w['wx'], _vec(p['lru_bx']), w['wa'], _vec(p['lru_ba']),
             _vec(p['lru_a_param']))
    yl, hl = _lru_fwd(uxg, *lru_w)
    mix, x1, x1b, xhat1, rstd1 = _mix_out_ln(ya, ys, yl, _vec(p['mix_norm_g']), p['w_out'], _vec(p['b_out']), x,
                                             _vec(p['ln1_g']), _vec(p['ln1_b']), layer)
    gpre, gconv, up, hmid = _ffn_hidden_fwd(x1b, p['ffn_w_gate'], p['ffn_w_up'], p['ffn_conv_w'], p['ffn_conv_b'],
                                            layer)
    x2, x2b, xhat2, rstd2 = _matmul_ln(
        "ffn_down_ln", hmid, p['ffn_w_down'], jnp.zeros((1, D_MODEL), F32), x1, _vec(p['ln2_g']), _vec(p['ln2_b']),
        a_blk=(N_CHIPS, tm, FF_SH), a_map=lambda i: (0, i, 0), w_blk=(N_CHIPS, FF_SH, D_MODEL), parts=N_CHIPS,
        layer=layer)
    saved = dict(xb=xb, uxg=uxg, qkv=qkv, ya=ya, lse=lse, h5=h5, ys=ys, yl=yl, hl=hl, mix=mix, x1b=x1b, xhat1=xhat1,
                 rstd1=rstd1, gpre=gpre, gconv=gconv, up=up, hmid=hmid, xhat2=xhat2, rstd2=rstd2, lru_w=lru_w)
    return x2, x2b, saved


def _layer_bwd(dr2, dr2b, s, p, w, cos, sin_s, big, below):
    t = dr2.shape[0]
    tk = _row_tile(t, TMM)
    nk = t // tk
    tw = _row_tile(t, 2 * TMM)
    tm = _row_tile(t, TM)
    layer = p['layer']
    big = dict(big)
    g = {}
    dup, dgpre, g['ffn_conv_w'], g['ffn_conv_b'] = _ffn_hidden_bwd(
        dr2b, s['gpre'], s['gconv'], s['up'], p['ffn_w_down'], p['ffn_conv_w'], layer)
    big['ffn_w_down'] = _matmul(
        "d_w_down", s['hmid'], dr2b, a_blk=(None, tw, FF_SH), a_map=lambda i, j, k: (i, k, 0), b_blk=(tw, D_MODEL),
        b_map=lambda i, j, k: (k, 0), out_shape=(DEPTH, N_CHIPS, FF_SH, D_MODEL), o_blk=(None, None, FF_SH, D_MODEL),
        o_map=lambda i, j: (layer, i, 0, 0), grid=(N_CHIPS, 1, t // tw), dims=TN, into=big['ffn_w_down'])
    d_ffn_w = lambda name, dact, buf: _matmul(
        name, s['x1b'], dact, a_blk=(tw, D_MODEL), a_map=lambda i, j, k: (k, 0), b_blk=(None, tw, FF_SH),
        b_map=lambda i, j, k: (j, k, 0), out_shape=(DEPTH, N_CHIPS, D_MODEL, FF_SH),
        o_blk=(None, None, D_MODEL, FF_SH), o_map=lambda i, j: (layer, j, 0, 0), grid=(1, N_CHIPS, t // tw), dims=TN,
        into=buf)
    big['ffn_w_gate'] = d_ffn_w("d_w_gate", dgpre, big['ffn_w_gate'])
    big['ffn_w_up'] = d_ffn_w("d_w_up", dup, big['ffn_w_up'])
    wspec = dict(b_blk=(None, None, D_MODEL, FF_SH), b_map=lambda i, j, k: (layer, k, 0, 0))
    dx1 = _matmul(
        "d_x1", dgpre, p['ffn_w_gate'], pair2=(dup, p['ffn_w_up']), a_blk=(None, tk, FF_SH),
        a_map=lambda i, j, k: (k, i, 0), out_shape=(t, D_MODEL), o_blk=(tk, D_MODEL), o_map=lambda i, j: (i, 0),
        grid=(nk, 1, N_CHIPS), dims=NT, add=dr2, add_scale=ALPHA, **wspec)
    dr1, dr1b, g['ln1_g'], g['ln1_b'], g['b_out'], dya, dys, dyl, g['mix_norm_g'] = _d_mix_rms(
        dx1, s['xhat1'], s['rstd1'], _vec(p['ln1_g']), p['w_out'], s['ya'], s['ys'], s['yl'],
        _vec(p['mix_norm_g']), layer)
    big['w_out'] = _matmul(
        "d_w_out", s['mix'], dr1b, a_blk=(tw, D_MODEL), a_map=lambda i, j, k: (k, 0), b_blk=(tw, D_MODEL),
        b_map=lambda i, j, k: (k, 0), out_shape=(DEPTH, D_MODEL, D_MODEL), o_blk=(None, D_MODEL, D_MODEL),
        o_map=lambda i, j: (layer, 0, 0), grid=(1, 1, t // tw), dims=TN, into=big['w_out'])
    dq, dk, dv, g['attn_sinks'] = _attn_bwd(s['qkv'], s['ya'], dya, s['lse'], _vec(p['attn_sinks']))
    du, dcmat, dbmat, dabar, g['s5_d'], g['s5_glu_w'], g['s5_glu_b'] = _s5_bwd(
        s['uxg'], s['h5'], dys, w['bmat'], w['coef_b'], w['cmat'], _vec(p['s5_d']), w['gw'], _vec(p['s5_glu_b']))
    (dxc, dgate, g['lru_conv_w'], g['lru_conv_b'], dwx, g['lru_bx'], dwa, g['lru_ba'],
     g['lru_a_param']) = _lru_bwd(s['uxg'], s['hl'], dyl, *s['lru_w'])
    dproj, g['b_in'] = _assemble_dproj(dq, dk, dv, du, dxc, dgate, cos, sin_s, p['lru_conv_w'])
    big['w_in'] = _matmul(
        "d_w_in", s['xb'], dproj, a_blk=(tw, D_MODEL), a_map=lambda i, j, k: (k, 0), b_blk=(tw, IN_SH),
        b_map=lambda i, j, k: (k, j), out_shape=(DEPTH, N_CHIPS, D_MODEL, IN_SH), o_blk=(None, None, D_MODEL, IN_SH),
        o_map=lambda i, j: (layer, j, 0, 0), grid=(1, N_CHIPS, t // tw), dims=TN, into=big['w_in'])
    dx = _matmul("d_x", dproj, p['w_in'], a_blk=(tk, IN_SH), a_map=lambda i, j, k: (i, k),
                 b_blk=(None, None, D_MODEL, IN_SH), b_map=lambda i, j, k: (layer, k, 0, 0), out_shape=(t, D_MODEL),
                 o_blk=(tk, D_MODEL), o_map=lambda i, j: (i, 0), grid=(nk, 1, N_CHIPS), dims=NT,
                 add=dr1, add_scale=ALPHA)
    if below is not None:
        dx = _ln_bwd(dx, below[0]['xhat2'], below[0]['rstd2'], _vec(below[1]['ln2_g']))
    return dx, _param_chain(g, p, dabar, dbmat, dcmat, dwx, dwa), big


def _param_chain(g, p, dabar, dbmat, dcmat, dwx, dwa):
    s5_names = ('s5_a_re', 's5_a_im', 's5_b_re', 's5_b_im', 's5_c_re', 's5_c_im', 's5_log_dt')
    _, s5_vjp = jax.vjp(_s5_prepare, *[p[n] for n in s5_names])
    for n, val in zip(s5_names, s5_vjp((dabar[0], dabar[1], dbmat, dcmat))):
        g[n] = val
    g['lru_wx'] = jax.vjp(_blockdiag_lru, p['lru_wx'])[1](dwx)[0]
    g['lru_wa'] = jax.vjp(_blockdiag_lru, p['lru_wa'])[1](dwa)[0]
    return g


ROW_TILE = 512


def _pick_rows(rows):
    for rt in range(min(rows, ROW_TILE), 0, -1):
        if rows % rt == 0 and (rt % 16 == 0 or rt == rows):
            return rt
    return rows


def _cast_bf16(a):
    a2 = a.reshape(-1, a.shape[-1])
    rows, c = a2.shape
    rt = _pick_rows(rows)

    def kern(a_ref, o_ref):
        o_ref[...] = a_ref[...].astype(BF16)

    spec = pl.BlockSpec((rt, c), lambda i: (i, 0))
    out = pl.pallas_call(kern, name="cast_bf16", grid=(rows // rt,), in_specs=[spec], out_specs=spec,
                         out_shape=jax.ShapeDtypeStruct((rows, c), BF16), compiler_params=_params(("parallel",)))(a2)
    return out.reshape(a.shape)


def _sum_parts(name, parts, shape):
    c = shape[-1]
    rows = math.prod(shape[:-1])
    rt = _pick_rows(rows)
    n = len(parts)

    def kern(*refs):
        acc = refs[0][...].astype(F32)
        for r in refs[1:n]:
            acc = acc + r[...].astype(F32)
        refs[n][...] = acc

    specs, args = [], []
    for arr, j in parts:
        if j is None:
            specs.append(pl.BlockSpec((rt, c), lambda i: (i, 0)))
            args.append(arr.reshape(rows, c))
        else:
            specs.append(pl.BlockSpec((None, rt, c), functools.partial(lambda i, jj: (jj, i, 0), jj=j)))
            args.append(arr.reshape(arr.shape[0], rows, c))
    out = pl.pallas_call(kern, name=name, grid=(rows // rt,), in_specs=specs,
                         out_specs=pl.BlockSpec((rt, c), lambda i: (i, 0)),
                         out_shape=jax.ShapeDtypeStruct((rows, c), F32), compiler_params=_params(("parallel",)))(*args)
    return out.reshape(shape)


def _adamw(name, w, g, m, v):
    shape = w.shape
    c = shape[-1]
    rows = math.prod(shape[:-1])
    rt = _pick_rows(rows)

    def kern(w_ref, g_ref, m_ref, v_ref, d_ref, nm_ref, nv_ref):
        g_ = g_ref[...]
        m_ = ADAM_B1 * m_ref[...] + (1.0 - ADAM_B1) * g_
        v_ = ADAM_B2 * v_ref[...] + (1.0 - ADAM_B2) * jnp.square(g_)
        m_hat = m_ / (1.0 - ADAM_B1 ** ADAM_STEP)
        v_hat = v_ / (1.0 - ADAM_B2 ** ADAM_STEP)
        d_ref[...] = -ADAM_LR * (m_hat / (jnp.sqrt(v_hat) + ADAM_EPS) + ADAM_WD * w_ref[...])
        nm_ref[...] = m_
        nv_ref[...] = v_

    spec = pl.BlockSpec((rt, c), lambda i: (i, 0))
    outs = pl.pallas_call(kern, name=name, grid=(rows // rt,), in_specs=[spec] * 4, out_specs=[spec] * 3,
                          out_shape=[jax.ShapeDtypeStruct((rows, c), F32)] * 3,
                          compiler_params=_params(("parallel",)))(*[a.reshape(rows, c) for a in (w, g, m, v)])
    return tuple(o.reshape(shape) for o in outs)


def _position():
    return lax.axis_index("x"), lax.axis_index("y"), lax.axis_index("c")


def _other_chips(x, y):
    return [(1 - x, y), (x, 1 - y), (1 - x, 1 - y)]


def _exchange(name, arrs, out_shapes, n_local, n_remote, plan):
    n_in, n_out = len(arrs), len(out_shapes)

    def kern(*refs):
        ins, outs = refs[:n_in], refs[n_in:n_in + n_out]
        send, recv, loc = refs[n_in + n_out:]
        local, remote = plan(ins, outs, *_position())
        assert len(local) == n_local and len(remote) == n_remote
        own = [pltpu.make_async_copy(s, d, loc.at[k]) for k, (s, d) in enumerate(local)]
        for cp in own:
            cp.start()
        sent = [pltpu.make_async_remote_copy(src_ref=s, dst_ref=d, send_sem=send.at[k], recv_sem=recv.at[k],
                                             device_id=peer, device_id_type=MESH)
                for k, (s, d, peer, _) in enumerate(remote)]
        for cp in sent:
            cp.start()
        for k, (s, _, peer, landing) in enumerate(remote):
            pltpu.make_async_remote_copy(src_ref=s, dst_ref=landing, send_sem=send.at[k], recv_sem=recv.at[k],
                                         device_id=peer, device_id_type=MESH).wait_recv()
        for cp in sent:
            cp.wait_send()
        for cp in own:
            cp.wait()

    return pl.pallas_call(
        kern, name=name, in_specs=[ANY] * n_in, out_specs=[ANY] * n_out, out_shape=out_shapes,
        scratch_shapes=[pltpu.SemaphoreType.DMA((n_remote,)), pltpu.SemaphoreType.DMA((n_remote,)),
                        pltpu.SemaphoreType.DMA((max(n_local, 1),))],
    )(*arrs)


def _allgather_chips(arrs, halved=()):
    n = len(arrs)
    layers = arrs[0].shape[0]

    def plan(ins, outs, x, y, c):
        me = 2 * x + y
        local, remote = [], []
        for t in range(n):
            for l in range(layers):
                src = ins[t].at[l]
                if t in halved:
                    r2 = ins[t].shape[2] // 2
                    src = ins[t].at[l, :, pl.ds(c * r2, r2)]
                local.append((src, outs[t].at[l, pl.ds(me, 1)]))
                for px, py in _other_chips(x, y):
                    remote.append((src, outs[t].at[l, pl.ds(me, 1)], (px, py, c),
                                   outs[t].at[l, pl.ds(2 * px + py, 1)]))
        return local, remote

    outs = []
    for t, a in enumerate(arrs):
        tail = (a.shape[2] // 2,) + a.shape[3:] if t in halved else a.shape[2:]
        outs.append(jax.ShapeDtypeStruct((a.shape[0], N_CHIPS) + tail, a.dtype))
    return _exchange("allgather_chips", arrs, outs, n * layers, 3 * n * layers, plan)


def _chip_scatter(arrs):
    n = len(arrs)
    layers = arrs[0].shape[0]

    def plan(ins, outs, x, y, c):
        me = 2 * x + y
        local, remote = [], []
        for t in range(n):
            for l in range(layers):
                local.append((ins[t].at[l, pl.ds(me, 1)], outs[2 * t].at[l]))
                for j, (px, py) in enumerate(_other_chips(x, y)):
                    remote.append((ins[t].at[l, pl.ds(2 * px + py, 1)], outs[2 * t + 1].at[j, l], (px, py, c),
                                   outs[2 * t + 1].at[j, l]))
        return local, remote

    outs = []
    for a in arrs:
        one = (a.shape[0], 1) + a.shape[2:]
        outs += [jax.ShapeDtypeStruct(one, a.dtype), jax.ShapeDtypeStruct((3,) + one, a.dtype)]
    return _exchange("chip_scatter", arrs, outs, n * layers, 3 * n * layers, plan)


def _allgather_devices(v):
    def kern(v_ref, o_ref, send, recv, loc):
        x, y, c = _position()
        me, sibling = (x, y, c), (x, y, 1 - c)
        chips = _other_chips(x, y)

        def rows(px, py, pc):
            return o_ref.at[pl.ds(4 * px + 2 * py + pc, 1)]

        def copy(k, block, to, src=None):
            return pltpu.make_async_remote_copy(
                src_ref=rows(*block) if src is None else src, dst_ref=rows(*block), send_sem=send.at[k],
                recv_sem=recv.at[k], device_id=to, device_id_type=MESH)

        mine = pltpu.make_async_copy(v_ref, rows(*me), loc.at[0])
        mine.start()
        first = [copy(0, me, sibling, src=v_ref)]
        first += [copy(1 + j, me, (*chip, c), src=v_ref) for j, chip in enumerate(chips)]
        for cp in first:
            cp.start()
        passed = [copy(4 + j, (*chip, c), sibling) for j, chip in enumerate(chips)]
        for j, chip in enumerate(chips):
            copy(1 + j, (*chip, c), me).wait_recv()
            passed[j].start()
        copy(0, sibling, me).wait_recv()
        for j, chip in enumerate(chips):
            copy(4 + j, (*chip, 1 - c), me).wait_recv()
        for cp in first + passed:
            cp.wait_send()
        mine.wait()

    vmem = pl.BlockSpec(memory_space=pltpu.VMEM)
    return pl.pallas_call(
        kern, name="allgather_devices", in_specs=[vmem], out_specs=vmem,
        out_shape=jax.ShapeDtypeStruct((N_DEV,) + v.shape[1:], v.dtype),
        scratch_shapes=[pltpu.SemaphoreType.DMA((7,)), pltpu.SemaphoreType.DMA((7,)), pltpu.SemaphoreType.DMA((1,))],
        compiler_params=pltpu.CompilerParams(vmem_limit_bytes=VMEM_MB << 20),
    )(v)


WEIGHTS = ['w_in', 'b_in', 'attn_sinks', 's5_a_re', 's5_a_im', 's5_b_re', 's5_b_im', 's5_c_re', 's5_c_im', 's5_d',
           's5_log_dt', 's5_glu_w', 's5_glu_b', 'lru_conv_w', 'lru_conv_b', 'lru_wx', 'lru_bx', 'lru_wa', 'lru_ba',
           'lru_a_param', 'mix_norm_g', 'w_out', 'b_out', 'ln1_g', 'ln1_b', 'ffn_w_gate', 'ffn_w_up', 'ffn_conv_w',
           'ffn_conv_b', 'ffn_w_down', 'ln2_g', 'ln2_b']
BIG = ('w_in', 'w_out', 'ffn_w_gate', 'ffn_w_up', 'ffn_w_down')
SMALL = tuple(n for n in WEIGHTS if n not in BIG)
PACK_ROWS = ROW_TILE


def _pack(arrs):
    flat = jnp.concatenate([a.reshape(-1) for a in arrs])
    unit = 128 * PACK_ROWS
    size = -(-flat.shape[0] // unit) * unit
    return jnp.pad(flat, (0, size - flat.shape[0])).reshape(-1, 128)


def _unpack(packed, shapes):
    flat = packed.reshape(-1)
    out, pos = [], 0
    for shp in shapes:
        n = math.prod(shp)
        out.append(flat[pos:pos + n].reshape(shp))
        pos += n
    return out


def _pair_reduce(name, g):
    layers, shards, rows, cols = g.shape
    r2 = rows // 2
    rt = _pick_rows(r2)
    nr = r2 // rt
    nsteps = layers * shards * nr

    def kern(c_ref, mine_ref, other_ref, o_ref, buf, send, recv, credit):
        x, y, c = _position()
        sibling = (x, y, 1 - c)
        k = pl.program_id(0) * nr + pl.program_id(1)
        slot = k % 2

        @pl.when(k >= 2)
        def _():
            pl.semaphore_wait(credit, 1)

        cp = pltpu.make_async_remote_copy(src_ref=other_ref, dst_ref=buf.at[slot], send_sem=send.at[slot],
                                          recv_sem=recv.at[slot], device_id=sibling, device_id_type=MESH)
        cp.start()
        cp.wait_recv()
        o_ref[...] = (mine_ref[...] + buf[slot]).astype(BF16)
        cp.wait_send()

        @pl.when(k + 2 < nsteps)
        def _():
            pl.semaphore_signal(credit, 1, device_id=sibling, device_id_type=MESH)

    blk = (1, rt, cols)
    grid_spec = pltpu.PrefetchScalarGridSpec(
        num_scalar_prefetch=1, grid=(layers * shards, nr),
        in_specs=[pl.BlockSpec(blk, lambda m, r, c_ref: (m, c_ref[0] * nr + r, 0)),
                  pl.BlockSpec(blk, lambda m, r, c_ref: (m, (1 - c_ref[0]) * nr + r, 0))],
        out_specs=pl.BlockSpec(blk, lambda m, r, c_ref: (m, r, 0)),
        scratch_shapes=[pltpu.VMEM((2,) + blk, F32), pltpu.SemaphoreType.DMA((2,)),
                        pltpu.SemaphoreType.DMA((2,)), pltpu.SemaphoreType.REGULAR])
    core = lax.axis_index("c").astype(jnp.int32).reshape(1)
    g3 = g.reshape(layers * shards, rows, cols)
    out = pl.pallas_call(
        kern, name=name, grid_spec=grid_spec,
        out_shape=jax.ShapeDtypeStruct((layers * shards, r2, cols), BF16),
        compiler_params=_params(("arbitrary", "arbitrary")),
    )(core, g3, g3)
    return out.reshape(layers, shards, r2, cols)


def _pair_merge(name, h):
    m, r2, cols = h.shape
    rt = _pick_rows(r2)
    nr = r2 // rt
    nsteps = m * nr

    def kern(h_ref, o_ref, buf, send, recv, credit):
        x, y, c = _position()
        sibling = (x, y, 1 - c)
        k = pl.program_id(0) * nr + pl.program_id(1)
        slot = k % 2

        @pl.when(k >= 2)
        def _():
            pl.semaphore_wait(credit, 1)

        cp = pltpu.make_async_remote_copy(src_ref=h_ref, dst_ref=buf.at[slot], send_sem=send.at[slot],
                                          recv_sem=recv.at[slot], device_id=sibling, device_id_type=MESH)
        cp.start()
        cp.wait_recv()
        o_ref[0, pl.ds(c, 1)] = h_ref[...]
        o_ref[0, pl.ds(1 - c, 1)] = buf[slot]
        cp.wait_send()

        @pl.when(k + 2 < nsteps)
        def _():
            pl.semaphore_signal(credit, 1, device_id=sibling, device_id_type=MESH)

    blk = (1, rt, cols)
    out = pl.pallas_call(
        kern, name=name, grid=(m, nr),
        in_specs=[pl.BlockSpec(blk, lambda i, r: (i, r, 0))],
        out_specs=pl.BlockSpec((1, 2, rt, cols), lambda i, r: (i, 0, r, 0)),
        out_shape=jax.ShapeDtypeStruct((m, 2, r2, cols), h.dtype),
        scratch_shapes=[pltpu.VMEM((2,) + blk, h.dtype), pltpu.SemaphoreType.DMA((2,)),
                        pltpu.SemaphoreType.DMA((2,)), pltpu.SemaphoreType.REGULAR],
        compiler_params=_params(("arbitrary", "arbitrary")),
    )(h)
    return out.reshape(m, 2 * r2, cols)


def _reduce_big(grads):
    pair = [_pair_reduce("pair_reduce_" + n, g) for n, g in zip(BIG, grads)]
    scat = _chip_scatter(pair)
    out = []
    for t, n in enumerate(BIG):
        own, got = scat[2 * t], scat[2 * t + 1]
        half = _sum_parts("chip_sum", [(own, None)] + [(got, j) for j in range(3)], own.shape)
        out.append(_pair_merge("grad_merge_" + n, half.reshape(half.shape[0], half.shape[2], half.shape[3])))
    return out


def _step(a):
    x = a['x'][0]
    target = a['loss_target'][0]
    t = x.shape[0]
    xi, yi, _ = _position()
    chip = 2 * xi + yi
    cos, sin_s = _rope_tables(t)

    gathered = _allgather_chips([_cast_bf16(a[n])[:, None] for n in BIG]
                                + [a[n][:, None] for n in ('s5_glu_w', 'lru_conv_w', 'ffn_conv_w')],
                                halved=range(len(BIG)))
    full = dict(zip(BIG + ('s5_glu_w', 'lru_conv_w', 'ffn_conv_w'), gathered))
    for n in BIG:
        layers, chips, r2, cols = full[n].shape
        full[n] = _pair_merge("weight_merge_" + n, full[n].reshape(layers * chips, r2, cols)).reshape(
            layers, chips, 2 * r2, cols)

    def layer_params(l):
        p = {n: a[n][l] for n in SMALL}
        p['layer'] = l
        p['w_in'] = full['w_in']
        p['w_out'] = full['w_out'].reshape(DEPTH, D_MODEL, D_MODEL)
        p['ffn_w_gate'] = full['ffn_w_gate']
        p['ffn_w_up'] = full['ffn_w_up']
        p['ffn_w_down'] = full['ffn_w_down']
        p['s5_glu_w'] = full['s5_glu_w'][l].reshape(D_S5, D_S5)
        p['lru_conv_w'] = full['lru_conv_w'][l].transpose(1, 0, 2).reshape(LRU_CONV, D_LRU)
        p['ffn_conv_w'] = full['ffn_conv_w'][l]
        p['ffn_conv_b'] = a['ffn_conv_b'][l].reshape(N_CHIPS, 1, FF_SH)
        return p

    params = [layer_params(l) for l in range(DEPTH)]
    derived = [_layer_weights(p) for p in params]
    saved = []
    h, hb = x, _cast_bf16(x)
    for l in range(DEPTH):
        h, hb, s = _layer_fwd(h, hb, params[l], derived[l], cos, sin_s)
        saved.append(s)
    loss_part, dr, drb, ln2_g, ln2_b, _ = _loss_head(h, target, saved[-1]['xhat2'], saved[-1]['rstd2'],
                                                     _vec(params[-1]['ln2_g']))
    loss = lax.psum(loss_part[0, 0], ("x", "y", "c"))
    grads = [None] * DEPTH
    big = {n: lax.empty((DEPTH, N_CHIPS) + a[n].shape[1:], F32) for n in BIG}
    big['w_out'] = big['w_out'].reshape(DEPTH, D_MODEL, D_MODEL)
    for l in reversed(range(DEPTH)):
        below = (saved[l - 1], params[l - 1]) if l > 0 else None
        out, grads[l], big = _layer_bwd(dr, drb, saved[l], params[l], derived[l], cos, sin_s, big, below)
        grads[l]['ln2_g'], grads[l]['ln2_b'] = ln2_g, ln2_b
        if l > 0:
            dr, drb, ln2_g, ln2_b, _ = out
        else:
            grad_x = out[None]

    def stacked(n):
        return jnp.stack([grads[l][n] for l in range(DEPTH)])

    big['w_out'] = big['w_out'].reshape(DEPTH, N_CHIPS, OUT_SH, D_MODEL)
    grad = dict(zip(BIG, _reduce_big([big[n] for n in BIG])))
    small_local = [stacked(n) for n in SMALL]
    packed = _allgather_devices(_pack(small_local)[None])
    total = _sum_parts("device_sum", [(packed, j) for j in range(N_DEV)], packed.shape[1:])
    small_sum = dict(zip(SMALL, _unpack(total, [g.shape for g in small_local])))
    for n in SMALL:
        g = small_sum[n]
        if n == 's5_glu_w':
            g = lax.dynamic_slice_in_dim(g, chip * (D_S5 // N_CHIPS), D_S5 // N_CHIPS, axis=1)
        elif n == 'lru_conv_w':
            g = lax.dynamic_slice_in_dim(g, chip * (D_LRU // N_CHIPS), D_LRU // N_CHIPS, axis=2)
        elif n == 'ffn_conv_w':
            g = lax.dynamic_index_in_dim(g, chip, axis=1, keepdims=False)
        grad[n] = g.reshape(a[n].shape)

    delta, new_m, new_v = {}, {}, {}
    for n in WEIGHTS:
        delta[n], new_m[n], new_v[n] = _adamw("adamw_" + n, a[n], grad[n], a['m_' + n], a['v_' + n])
    return (loss, grad_x, *[grad[n] for n in WEIGHTS], *[delta[n] for n in WEIGHTS],
            *[new_m[n] for n in WEIGHTS], *[new_v[n] for n in WEIGHTS])


def kernel(x, w_in, b_in, attn_sinks, s5_a_re, s5_a_im, s5_b_re, s5_b_im, s5_c_re, s5_c_im, s5_d, s5_log_dt, s5_glu_w, s5_glu_b, lru_conv_w, lru_conv_b, lru_wx, lru_bx, lru_wa, lru_ba, lru_a_param, mix_norm_g, w_out, b_out, ln1_g, ln1_b, ffn_w_gate, ffn_w_up, ffn_conv_w, ffn_conv_b, ffn_w_down, ln2_g, ln2_b, loss_target, m_w_in, m_b_in, m_attn_sinks, m_s5_a_re, m_s5_a_im, m_s5_b_re, m_s5_b_im, m_s5_c_re, m_s5_c_im, m_s5_d, m_s5_log_dt, m_s5_glu_w, m_s5_glu_b, m_lru_conv_w, m_lru_conv_b, m_lru_wx, m_lru_bx, m_lru_wa, m_lru_ba, m_lru_a_param, m_mix_norm_g, m_w_out, m_b_out, m_ln1_g, m_ln1_b, m_ffn_w_gate, m_ffn_w_up, m_ffn_conv_w, m_ffn_conv_b, m_ffn_w_down, m_ln2_g, m_ln2_b, v_w_in, v_b_in, v_attn_sinks, v_s5_a_re, v_s5_a_im, v_s5_b_re, v_s5_b_im, v_s5_c_re, v_s5_c_im, v_s5_d, v_s5_log_dt, v_s5_glu_w, v_s5_glu_b, v_lru_conv_w, v_lru_conv_b, v_lru_wx, v_lru_bx, v_lru_wa, v_lru_ba, v_lru_a_param, v_mix_norm_g, v_w_out, v_b_out, v_ln1_g, v_ln1_b, v_ffn_w_gate, v_ffn_w_up, v_ffn_conv_w, v_ffn_conv_b, v_ffn_w_down, v_ln2_g, v_ln2_b):
    return _step(dict(locals()))
```
